```python
import functools
import jax, jax.numpy as jnp
from jax import lax
import numpy as np

D_MODEL = 2048
BATCH = 8
SEQ = 2048
DEPTH = 2

GRID_W = 64
CTX_LEN = 256
HEAD_DIM = 128
N_MIX_HEADS = D_MODEL // HEAD_DIM
ATT_Q_HEADS = N_MIX_HEADS // 2
ATT_KV_HEADS = ATT_Q_HEADS // 4
RET_HEADS = N_MIX_HEADS // 4
RET_DIM = HEAD_DIM
GLA_HEADS = N_MIX_HEADS // 4
GLA_DK = HEAD_DIM // 2
GLA_DV = HEAD_DIM
GLA_GATE_RANK = 16
GLA_TAU = 16.0
D_FF = 5632
ROPE_THETA = 10000.0
Q_BLOCK = 128
RET_CHUNK = 128
GLA_CHUNK = 64
N_MOD = 6
EPS = 1e-6
ATT_W = ATT_Q_HEADS * HEAD_DIM
ATT_KV_W = ATT_KV_HEADS * HEAD_DIM
RET_W = RET_HEADS * RET_DIM
GLA_K_W = GLA_HEADS * GLA_DK
GLA_V_W = GLA_HEADS * GLA_DV
IN_SPLITS = (ATT_W, ATT_KV_W, ATT_KV_W, RET_W, RET_W, RET_W, RET_W, GLA_K_W, GLA_K_W, GLA_V_W, GLA_V_W, 2 * GLA_GATE_RANK)
N_IN = ATT_W + 2 * ATT_KV_W + 4 * RET_W + 2 * GLA_K_W + 2 * GLA_V_W + 2 * GLA_GATE_RANK

kernel_name = "hymba_style_diffusion_hybrid_attn_retnet_gla"


def rms_norm(x, g):
    xf = x.astype(jnp.float32)
    y = xf * lax.rsqrt(jnp.mean(xf * xf, axis=-1, keepdims=True) + EPS)
    return (y * g.astype(jnp.float32)).astype(x.dtype)


def modulate(h, shift, scale):
    return h * (1 + scale) + shift


def split_heads(t, n_heads):
    B, L, _ = t.shape
    return t.reshape(B, L, n_heads, -1).transpose(0, 2, 1, 3)


def merge_heads(t):
    B, H, L, d = t.shape
    return t.transpose(0, 2, 1, 3).reshape(B, L, H * d)


def axial_rope(n_tokens):
    rows = n_tokens // GRID_W
    row = jnp.repeat(jnp.arange(rows, dtype=jnp.float32), GRID_W)
    col = jnp.tile(jnp.arange(GRID_W, dtype=jnp.float32), rows)
    n_freq = HEAD_DIM // 4
    inv_freq = ROPE_THETA ** (-jnp.arange(n_freq, dtype=jnp.float32) / n_freq)
    ang = jnp.concatenate([row[:, None] * inv_freq, col[:, None] * inv_freq], axis=-1)
    return jnp.cos(ang), jnp.sin(ang)


def apply_rope(t, cos, sin):
    half = t.shape[-1] // 2
    t1, t2 = t[..., :half], t[..., half:]
    cos = cos.astype(t.dtype)
    sin = sin.astype(t.dtype)
    return jnp.concatenate([t1 * cos - t2 * sin, t1 * sin + t2 * cos], axis=-1)


def project_inputs(h, w_in, q_norm_g, k_norm_g, gla_gate_up, gla_gate_b, rope):
    B, L, _ = h.shape
    f32 = jnp.float32
    z = h @ w_in
    cuts = np.cumsum(IN_SPLITS)[:-1].tolist()
    aq, ak, av, rq, rk, rv, rg, gq, gk, gv, gr, ga = jnp.split(z, cuts, axis=-1)
    aq = rms_norm(split_heads(aq, ATT_Q_HEADS), q_norm_g)
    ak = rms_norm(split_heads(ak, ATT_KV_HEADS), k_norm_g)
    rq = split_heads(rq, RET_HEADS)
    rk = split_heads(rk, RET_HEADS) * (RET_DIM ** -0.5)
    if rope is not None:
        cos, sin = rope
        aq, ak, rq, rk = (apply_rope(t, cos, sin) for t in (aq, ak, rq, rk))
    logit = jnp.einsum('blnr,nrk->nblk', ga.reshape(B, L, 2, GLA_GATE_RANK), gla_gate_up) + gla_gate_b[:, None, None, :]
    log_a = jax.nn.log_sigmoid(logit.astype(f32)) / GLA_TAU
    return dict(
        aq=aq, ak=ak, av=split_heads(av, ATT_KV_HEADS),
        rq=rq.astype(f32), rk=rk.astype(f32), rv=split_heads(rv, RET_HEADS).astype(f32), rg=rg,
        gq=(split_heads(gq, GLA_HEADS) * (GLA_DK ** -0.5)).astype(f32),
        gk=split_heads(gk, GLA_HEADS).astype(f32), gv=split_heads(gv, GLA_HEADS).astype(f32), gr=gr,
        la_f=split_heads(log_a[0], GLA_HEADS), la_b=split_heads(log_a[1], GLA_HEADS))


def softmax_attention(q, k, v):
    B, Hq, Lq, d = q.shape
    Hkv = k.shape[1]
    G = Hq // Hkv
    nb = Lq // Q_BLOCK
    qb = q.reshape(B, Hkv, G, nb, Q_BLOCK, d).transpose(3, 0, 1, 2, 4, 5)
    scale = d ** -0.5

    def block(qi):
        s = jnp.einsum('bkgqd,bksd->bkgqs', qi, k).astype(jnp.float32) * scale
        p = jax.nn.softmax(s, axis=-1).astype(v.dtype)
        return jnp.einsum('bkgqs,bksd->bkgqd', p, v)

    o = lax.map(block, qb)
    return o.transpose(1, 2, 3, 0, 4, 5).reshape(B, Hq, Lq, d)


def retention_scan(log_g, q, k, v, s0):
    B, H, L, _ = q.shape
    C = RET_CHUNK
    n = L // C
    f32 = jnp.float32
    lg = log_g.astype(f32)[:, None]
    idx = jnp.arange(C, dtype=f32)
    rel = idx[:, None] - idx[None, :]
    causal = rel >= 0
    d_in = jnp.where(causal, jnp.exp(lg[:, :, None] * jnp.where(causal, rel, 0.0)), 0.0)
    q_dec = jnp.exp(lg * (idx + 1))[..., None]
    k_dec = jnp.exp(lg * (C - 1 - idx))[..., None]
    c_dec = jnp.exp(lg * C)[..., None]

    def chunks(t):
        return jnp.moveaxis(t.reshape(B, H, n, C, t.shape[-1]), 2, 0)

    def step(S, inp):
        qc, kc, vc = inp
        att = jnp.einsum('bhid,bhjd->bhij', qc, kc) * d_in
        o = jnp.einsum('bhij,bhjv->bhiv', att, vc) + jnp.einsum('bhid,bhdv->bhiv', qc * q_dec, S)
        S = c_dec * S + jnp.einsum('bhjd,bhjv->bhdv', kc * k_dec, vc)
        return S, o

    S, o = lax.scan(step, s0, (chunks(q), chunks(k), chunks(v)))
    return jnp.moveaxis(o, 0, 2).reshape(B, H, L, -1), S


def gla_scan(q, k, v, log_a, s0):
    B, H, L, _ = q.shape
    C = GLA_CHUNK
    n = L // C
    causal = jnp.tril(jnp.ones((C, C), dtype=bool))[:, :, None]

    def chunks(t):
        return jnp.moveaxis(t.reshape(B, H, n, C, t.shape[-1]), 2, 0)

    def step(S, inp):
        qc, kc, vc, ac = inp
        b = jnp.cumsum(ac, axis=2)
        diff = b[:, :, :, None, :] - b[:, :, None, :, :]
        dec = jnp.where(causal, jnp.exp(jnp.where(causal, diff, 0.0)), 0.0)
        att = jnp.einsum('bhid,bhjd,bhijd->bhij', qc, kc, dec)
        o = jnp.einsum('bhij,bhjv->bhiv', att, vc) + jnp.einsum('bhid,bhdv->bhiv', qc * jnp.exp(b), S)
        b_end = b[:, :, -1:, :]
        S = jnp.exp(b_end[:, :, 0, :, None]) * S + jnp.einsum('bhjd,bhjv->bhdv', kc * jnp.exp(b_end - b), vc)
        return S, o

    S, o = lax.scan(step, s0, (chunks(q), chunks(k), chunks(v), chunks(log_a)))
    return jnp.moveaxis(o, 0, 2).reshape(B, H, L, -1), S


def run_bidirectional(scan_f, scan_b, ctx_f, ctx_b, lat_f, lat_b, s0):
    flip = lambda ts: tuple(jnp.flip(t, axis=2) for t in ts)
    o_cf, s_cf = scan_f(*ctx_f, s0)
    o_cb, s_cb = scan_b(*flip(ctx_b), s0)
    o_lf, _ = scan_f(*lat_f, s_cf)
    o_lb, _ = scan_b(*flip(lat_b), s_cb)
    return o_cf + jnp.flip(o_cb, axis=2), o_lf + jnp.flip(o_lb, axis=2)


def mixer_output(o_att, o_ret, g_ret, o_gla, g_gla, ret_norm_g, gla_norm_g, w_out):
    dt = g_ret.dtype
    att = merge_heads(o_att).astype(dt)
    ret = merge_heads(rms_norm(o_ret, ret_norm_g)).astype(dt) * jax.nn.silu(g_ret)
    gla = merge_heads(rms_norm(o_gla, gla_norm_g)).astype(dt) * jax.nn.silu(g_gla)
    return jnp.concatenate([att, ret, gla], axis=-1) @ w_out


def dwconv3(u, w, b):
    up = jnp.pad(u, ((0, 0), (1, 1), (0, 0)))
    return up[:, :-2] * w[0] + up[:, 1:-1] * w[1] + up[:, 2:] * w[2] + b


def conv_glu(h, w_up, conv_w, conv_b, w_down):
    a, v = jnp.split(h @ w_up, 2, axis=-1)
    return (jax.nn.silu(dwconv3(a, conv_w, conv_b)) * v) @ w_down


def _fwd_setup_inputs(seed: int = 0) -> dict:
    key = jax.random.key(seed)
    ks = jax.random.split(key, 22)
    f32 = jnp.float32

    def nrm(k, shape, scale):
        return jax.random.normal(k, shape, f32) * scale

    def gain(k, shape):
        return 1.0 + 0.02 * jax.random.normal(k, shape, f32)

    decay0 = jnp.log(1.0 - 2.0 ** (-5.0 - jnp.arange(RET_HEADS, dtype=f32)))
    return {
        "x": nrm(ks[0], (BATCH, SEQ, D_MODEL), 1.0),
        "c": nrm(ks[1], (BATCH, D_MODEL), 1.0),
        "ctx": nrm(ks[2], (BATCH, CTX_LEN, D_MODEL), 1.0),
        "c_ctx": nrm(ks[3], (D_MODEL,), 1.0),
        "ada_w": nrm(ks[4], (DEPTH, D_MODEL, N_MOD * D_MODEL), 0.5 * D_MODEL ** -0.5),
        "ada_b": nrm(ks[5], (DEPTH, N_MOD * D_MODEL), 0.01),
        "norm1_g": gain(ks[6], (DEPTH, D_MODEL)),
        "w_in": nrm(ks[7], (DEPTH, D_MODEL, N_IN), D_MODEL ** -0.5),
        "q_norm_g": gain(ks[8], (DEPTH, HEAD_DIM)),
        "k_norm_g": gain(ks[9], (DEPTH, HEAD_DIM)),
        "ret_log_decay": decay0 * (1.0 + 0.05 * jax.random.normal(ks[10], (DEPTH, 2, RET_HEADS), f32)),
        "ret_norm_g": gain(ks[11], (DEPTH, RET_DIM)),
        "gla_gate_up": nrm(ks[12], (DEPTH, 2, GLA_GATE_RANK, GLA_K_W), GLA_GATE_RANK ** -0.5),
        "gla_gate_b": nrm(ks[13], (DEPTH, 2, GLA_K_W), 0.1),
        "gla_norm_g": gain(ks[14], (DEPTH, GLA_DV)),
        "w_out": nrm(ks[15], (DEPTH, D_MODEL, D_MODEL), D_MODEL ** -0.5),
        "norm2_g": gain(ks[16], (DEPTH, D_MODEL)),
        "w_up": nrm(ks[17], (DEPTH, D_MODEL, 2 * D_FF), D_MODEL ** -0.5),
        "conv_w": nrm(ks[18], (DEPTH, 3, D_FF), 3 ** -0.5),
        "conv_b": nrm(ks[19], (DEPTH, D_FF), 0.01),
        "w_down": nrm(ks[20], (DEPTH, D_FF, D_MODEL), D_FF ** -0.5),
        "final_norm_g": gain(ks[21], (D_MODEL,)),
    }


def _fwd_reference(x, c, ctx, c_ctx, ada_w, ada_b, norm1_g, w_in, q_norm_g, k_norm_g, ret_log_decay, ret_norm_g, gla_gate_up, gla_gate_b, gla_norm_g, w_out, norm2_g, w_up, conv_w, conv_b, w_down, final_norm_g):
    B, L, D = x.shape
    f32 = jnp.float32
    rope = axial_rope(L)
    s0_ret = jnp.zeros((B, RET_HEADS, RET_DIM, RET_DIM), f32)
    s0_gla = jnp.zeros((B, GLA_HEADS, GLA_DK, GLA_DV), f32)
    xc = ctx
    for l in range(DEPTH):
        last = l == DEPTH - 1
        mod = (jax.nn.silu(c) @ ada_w[l] + ada_b[l]).reshape(B, N_MOD, 1, D)
        mod_c = (jax.nn.silu(c_ctx) @ ada_w[l] + ada_b[l]).reshape(N_MOD, 1, 1, D)

        h = modulate(rms_norm(x, norm1_g[l]), mod[:, 0], mod[:, 1])
        hc = modulate(rms_norm(xc, norm1_g[l]), mod_c[0], mod_c[1])
        lat = project_inputs(h, w_in[l], q_norm_g[l], k_norm_g[l], gla_gate_up[l], gla_gate_b[l], rope)
        cx = project_inputs(hc, w_in[l], q_norm_g[l], k_norm_g[l], gla_gate_up[l], gla_gate_b[l], None)

        k_all = jnp.concatenate([cx['ak'], lat['ak']], axis=2)
        v_all = jnp.concatenate([cx['av'], lat['av']], axis=2)
        att_lat = softmax_attention(lat['aq'], k_all, v_all)

        ret_ctx, ret_lat = run_bidirectional(
            functools.partial(retention_scan, ret_log_decay[l, 0]),
            functools.partial(retention_scan, ret_log_decay[l, 1]),
            (cx['rq'], cx['rk'], cx['rv']), (cx['rq'], cx['rk'], cx['rv']),
            (lat['rq'], lat['rk'], lat['rv']), (lat['rq'], lat['rk'], lat['rv']), s0_ret)

        gla_ctx, gla_lat = run_bidirectional(
            gla_scan, gla_scan,
            (cx['gq'], cx['gk'], cx['gv'], cx['la_f']), (cx['gq'], cx['gk'], cx['gv'], cx['la_b']),
            (lat['gq'], lat['gk'], lat['gv'], lat['la_f']), (lat['gq'], lat['gk'], lat['gv'], lat['la_b']), s0_gla)

        y = mixer_output(att_lat, ret_lat, lat['rg'], gla_lat, lat['gr'], ret_norm_g[l], gla_norm_g[l], w_out[l])
        x = x + mod[:, 2] * y

        h2 = modulate(rms_norm(x, norm2_g[l]), mod[:, 3], mod[:, 4])
        x = x + mod[:, 5] * conv_glu(h2, w_up[l], conv_w[l], conv_b[l], w_down[l])

        if not last:
            att_ctx = softmax_attention(cx['aq'], cx['ak'], cx['av'])
            yc = mixer_output(att_ctx, ret_ctx, cx['rg'], gla_ctx, cx['gr'], ret_norm_g[l], gla_norm_g[l], w_out[l])
            xc = xc + mod_c[2] * yc
            hc2 = modulate(rms_norm(xc, norm2_g[l]), mod_c[3], mod_c[4])
            xc = xc + mod_c[5] * conv_glu(hc2, w_up[l], conv_w[l], conv_b[l], w_down[l])
    return rms_norm(x, final_norm_g)


import jax as _jax
import jax.numpy as _jnp

TWIN_FORMAT = 'train_step'
FWD_PARAMS = ['x', 'c', 'ctx', 'c_ctx', 'ada_w', 'ada_b', 'norm1_g', 'w_in', 'q_norm_g', 'k_norm_g', 'ret_log_decay', 'ret_norm_g', 'gla_gate_up', 'gla_gate_b', 'gla_norm_g', 'w_out', 'norm2_g', 'w_up', 'conv_w', 'conv_b', 'w_down', 'final_norm_g']
TWIN_WEIGHTS = ['c_ctx', 'ada_w', 'ada_b', 'norm1_g', 'w_in', 'q_norm_g', 'k_norm_g', 'ret_log_decay', 'ret_norm_g', 'gla_gate_up', 'gla_gate_b', 'gla_norm_g', 'w_out', 'norm2_g', 'w_up', 'conv_w', 'conv_b', 'w_down', 'final_norm_g']
TWIN_DIFF_INPUT = 'x'
TWIN_INPUTS = ['x', 'c', 'ctx', 'c_ctx', 'ada_w', 'ada_b', 'norm1_g', 'w_in', 'q_norm_g', 'k_norm_g', 'ret_log_decay', 'ret_norm_g', 'gla_gate_up', 'gla_gate_b', 'gla_norm_g', 'w_out', 'norm2_g', 'w_up', 'conv_w', 'conv_b', 'w_down', 'final_norm_g', 'loss_target', 'm_c_ctx', 'm_ada_w', 'm_ada_b', 'm_norm1_g', 'm_w_in', 'm_q_norm_g', 'm_k_norm_g', 'm_ret_log_decay', 'm_ret_norm_g', 'm_gla_gate_up', 'm_gla_gate_b', 'm_gla_norm_g', 'm_w_out', 'm_norm2_g', 'm_w_up', 'm_conv_w', 'm_conv_b', 'm_w_down', 'm_final_norm_g', 'v_c_ctx', 'v_ada_w', 'v_ada_b', 'v_norm1_g', 'v_w_in', 'v_q_norm_g', 'v_k_norm_g', 'v_ret_log_decay', 'v_ret_norm_g', 'v_gla_gate_up', 'v_gla_gate_b', 'v_gla_norm_g', 'v_w_out', 'v_norm2_g', 'v_w_up', 'v_conv_w', 'v_conv_b', 'v_w_down', 'v_final_norm_g']
TWIN_OUTPUTS = ['loss', 'grad_x', 'grad_c_ctx', 'grad_ada_w', 'grad_ada_b', 'grad_norm1_g', 'grad_w_in', 'grad_q_norm_g', 'grad_k_norm_g', 'grad_ret_log_decay', 'grad_ret_norm_g', 'grad_gla_gate_up', 'grad_gla_gate_b', 'grad_gla_norm_g', 'grad_w_out', 'grad_norm2_g', 'grad_w_up', 'grad_conv_w', 'grad_conv_b', 'grad_w_down', 'grad_final_norm_g', 'delta_c_ctx', 'delta_ada_w', 'delta_ada_b', 'delta_norm1_g', 'delta_w_in', 'delta_q_norm_g', 'delta_k_norm_g', 'delta_ret_log_decay', 'delta_ret_norm_g', 'delta_gla_gate_up', 'delta_gla_gate_b', 'delta_gla_norm_g', 'delta_w_out', 'delta_norm2_g', 'delta_w_up', 'delta_conv_w', 'delta_conv_b', 'delta_w_down', 'delta_final_norm_g', 'new_m_c_ctx', 'new_m_ada_w', 'new_m_ada_b', 'new_m_norm1_g', 'new_m_w_in', 'new_m_q_norm_g', 'new_m_k_norm_g', 'new_m_ret_log_decay', 'new_m_ret_norm_g', 'new_m_gla_gate_up', 'new_m_gla_gate_b', 'new_m_gla_norm_g', 'new_m_w_out', 'new_m_norm2_g', 'new_m_w_up', 'new_m_conv_w', 'new_m_conv_b', 'new_m_w_down', 'new_m_final_norm_g', 'new_v_c_ctx', 'new_v_ada_w', 'new_v_ada_b', 'new_v_norm1_g', 'new_v_w_in', 'new_v_q_norm_g', 'new_v_k_norm_g', 'new_v_ret_log_decay', 'new_v_ret_norm_g', 'new_v_gla_gate_up', 'new_v_gla_gate_b', 'new_v_gla_norm_g', 'new_v_w_out', 'new_v_norm2_g', 'new_v_w_up', 'new_v_conv_w', 'new_v_conv_b', 'new_v_w_down', 'new_v_final_norm_g']
TWIN_LEAF_KINDS = {'loss': 'loss', 'grad_x': 'grad_x', 'grad_c_ctx': 'grad_w', 'grad_ada_w': 'grad_w', 'grad_ada_b': 'grad_w', 'grad_norm1_g': 'grad_w', 'grad_w_in': 'grad_w', 'grad_q_norm_g': 'grad_w', 'grad_k_norm_g': 'grad_w', 'grad_ret_log_decay': 'grad_w', 'grad_ret_norm_g': 'grad_w', 'grad_gla_gate_up': 'grad_w', 'grad_gla_gate_b': 'grad_w', 'grad_gla_norm_g': 'grad_w', 'grad_w_out': 'grad_w', 'grad_norm2_g': 'grad_w', 'grad_w_up': 'grad_w', 'grad_conv_w': 'grad_w', 'grad_conv_b': 'grad_w', 'grad_w_down': 'grad_w', 'grad_final_norm_g': 'grad_w', 'delta_c_ctx': 'delta_w', 'delta_ada_w': 'delta_w', 'delta_ada_b': 'delta_w', 'delta_norm1_g': 'delta_w', 'delta_w_in': 'delta_w', 'delta_q_norm_g': 'delta_w', 'delta_k_norm_g': 'delta_w', 'delta_ret_log_decay': 'delta_w', 'delta_ret_norm_g': 'delta_w', 'delta_gla_gate_up': 'delta_w', 'delta_gla_gate_b': 'delta_w', 'delta_gla_norm_g': 'delta_w', 'delta_w_out': 'delta_w', 'delta_norm2_g': 'delta_w', 'delta_w_up': 'delta_w', 'delta_conv_w': 'delta_w', 'delta_conv_b': 'delta_w', 'delta_w_down': 'delta_w', 'delta_final_norm_g': 'delta_w', 'new_m_c_ctx': 'new_m', 'new_m_ada_w': 'new_m', 'new_m_ada_b': 'new_m', 'new_m_norm1_g': 'new_m', 'new_m_w_in': 'new_m', 'new_m_q_norm_g': 'new_m', 'new_m_k_norm_g': 'new_m', 'new_m_ret_log_decay': 'new_m', 'new_m_ret_norm_g': 'new_m', 'new_m_gla_gate_up': 'new_m', 'new_m_gla_gate_b': 'new_m', 'new_m_gla_norm_g': 'new_m', 'new_m_w_out': 'new_m', 'new_m_norm2_g': 'new_m', 'new_m_w_up': 'new_m', 'new_m_conv_w': 'new_m', 'new_m_conv_b': 'new_m', 'new_m_w_down': 'new_m', 'new_m_final_norm_g': 'new_m', 'new_v_c_ctx': 'new_v', 'new_v_ada_w': 'new_v', 'new_v_ada_b': 'new_v', 'new_v_norm1_g': 'new_v', 'new_v_w_in': 'new_v', 'new_v_q_norm_g': 'new_v', 'new_v_k_norm_g': 'new_v', 'new_v_ret_log_decay': 'new_v', 'new_v_ret_norm_g': 'new_v', 'new_v_gla_gate_up': 'new_v', 'new_v_gla_gate_b': 'new_v', 'new_v_gla_norm_g': 'new_v', 'new_v_w_out': 'new_v', 'new_v_norm2_g': 'new_v', 'new_v_w_up': 'new_v', 'new_v_conv_w': 'new_v', 'new_v_conv_b': 'new_v', 'new_v_w_down': 'new_v', 'new_v_final_norm_g': 'new_v'}


def _forward(args):
    return _fwd_reference(*[args[k] for k in FWD_PARAMS])


def _output_shape():
    out = _jax.eval_shape(lambda: _forward(_fwd_setup_inputs(0)))
    return out.shape, out.dtype

N_MICROBATCH = 1
ADAM_LR = 0.001
ADAM_B1 = 0.9
ADAM_B2 = 0.999
ADAM_EPS = 1e-08
ADAM_WD = 0.01
ADAM_STEP = 10
PER_EXAMPLE_BATCH_AXIS = {'x': 0, 'c': 0, 'ctx': 0, 'loss_target': 0}
SHARED_INPUTS = []
_WEIGHT_DTYPES = {'c_ctx': _jnp.float32, 'ada_w': _jnp.float32, 'ada_b': _jnp.float32, 'norm1_g': _jnp.float32, 'w_in': _jnp.float32, 'q_norm_g': _jnp.float32, 'k_norm_g': _jnp.float32, 'ret_log_decay': _jnp.float32, 'ret_norm_g': _jnp.float32, 'gla_gate_up': _jnp.float32, 'gla_gate_b': _jnp.float32, 'gla_norm_g': _jnp.float32, 'w_out': _jnp.float32, 'norm2_g': _jnp.float32, 'w_up': _jnp.float32, 'conv_w': _jnp.float32, 'conv_b': _jnp.float32, 'w_down': _jnp.float32, 'final_norm_g': _jnp.float32}
MOMENT_SCALE = {'c_ctx': 6.218124e-03, 'ada_w': 1.796564e-02, 'ada_b': 3.006311e-02, 'norm1_g': 1.707212e-02, 'w_in': 1.230315e-02, 'q_norm_g': 4.508784e-03, 'k_norm_g': 4.783609e-03, 'ret_log_decay': 6.283116e+00, 'ret_norm_g': 2.680673e-02, 'gla_gate_up': 2.138619e-03, 'gla_gate_b': 5.991974e-03, 'gla_norm_g': 2.465758e-02, 'w_out': 9.809050e-03, 'norm2_g': 1.861540e-02, 'w_up': 8.211066e-03, 'conv_w': 8.489564e-03, 'conv_b': 7.451853e-03, 'w_down': 1.340465e-02, 'final_norm_g': 8.007033e+00}


def _to_microbatches(a, axis):
    t = _jnp.moveaxis(a, axis, 0)
    t = t.reshape((N_MICROBATCH, t.shape[0] // N_MICROBATCH) + t.shape[1:])
    return _jnp.moveaxis(t, 1, axis + 1)


def setup_inputs(seed: int = 0) -> dict:
    inp = _fwd_setup_inputs(seed)
    key = _jax.random.fold_in(_jax.random.key(seed), 7919)
    shape, _ = _output_shape()
    out = dict(inp)
    out["loss_target"] = _jax.random.normal(_jax.random.fold_in(key, 0), shape, _jnp.float32)
    for i, name in enumerate(TWIN_WEIGHTS):
        w = inp[name].astype(_jnp.float32)
        if MOMENT_SCALE is None:
            s = _jnp.sqrt(_jnp.mean(_jnp.square(w)) + 1e-30)
        else:
            s = MOMENT_SCALE[name]
        km, kv = _jax.random.split(_jax.random.fold_in(key, i + 1))
        out[name] = w
        out["m_" + name] = s * _jax.random.normal(km, w.shape, _jnp.float32)
        out["v_" + name] = (s * s) * _jax.random.uniform(kv, w.shape, _jnp.float32, 0.5, 1.5)
    if N_MICROBATCH > 1:
        for name, axis in PER_EXAMPLE_BATCH_AXIS.items():
            out[name] = _to_microbatches(out[name], axis)
    return {'x': out['x'], 'c': out['c'], 'ctx': out['ctx'], 'c_ctx': out['c_ctx'], 'ada_w': out['ada_w'], 'ada_b': out['ada_b'], 'norm1_g': out['norm1_g'], 'w_in': out['w_in'], 'q_norm_g': out['q_norm_g'], 'k_norm_g': out['k_norm_g'], 'ret_log_decay': out['ret_log_decay'], 'ret_norm_g': out['ret_norm_g'], 'gla_gate_up': out['gla_gate_up'], 'gla_gate_b': out['gla_gate_b'], 'gla_norm_g': out['gla_norm_g'], 'w_out': out['w_out'], 'norm2_g': out['norm2_g'], 'w_up': out['w_up'], 'conv_w': out['conv_w'], 'conv_b': out['conv_b'], 'w_down': out['w_down'], 'final_norm_g': out['final_norm_g'], 'loss_target': out['loss_target'], 'm_c_ctx': out['m_c_ctx'], 'm_ada_w': out['m_ada_w'], 'm_ada_b': out['m_ada_b'], 'm_norm1_g': out['m_norm1_g'], 'm_w_in': out['m_w_in'], 'm_q_norm_g': out['m_q_norm_g'], 'm_k_norm_g': out['m_k_norm_g'], 'm_ret_log_decay': out['m_ret_log_decay'], 'm_ret_norm_g': out['m_ret_norm_g'], 'm_gla_gate_up': out['m_gla_gate_up'], 'm_gla_gate_b': out['m_gla_gate_b'], 'm_gla_norm_g': out['m_gla_norm_g'], 'm_w_out': out['m_w_out'], 'm_norm2_g': out['m_norm2_g'], 'm_w_up': out['m_w_up'], 'm_conv_w': out['m_conv_w'], 'm_conv_b': out['m_conv_b'], 'm_w_down': out['m_w_down'], 'm_final_norm_g': out['m_final_norm_g'], 'v_c_ctx': out['v_c_ctx'], 'v_ada_w': out['v_ada_w'], 'v_ada_b': out['v_ada_b'], 'v_norm1_g': out['v_norm1_g'], 'v_w_in': out['v_w_in'], 'v_q_norm_g': out['v_q_norm_g'], 'v_k_norm_g': out['v_k_norm_g'], 'v_ret_log_decay': out['v_ret_log_decay'], 'v_ret_norm_g': out['v_ret_norm_g'], 'v_gla_gate_up': out['v_gla_gate_up'], 'v_gla_gate_b': out['v_gla_gate_b'], 'v_gla_norm_g': out['v_gla_norm_g'], 'v_w_out': out['v_w_out'], 'v_norm2_g': out['v_norm2_g'], 'v_w_up': out['v_w_up'], 'v_conv_w': out['v_conv_w'], 'v_conv_b': out['v_conv_b'], 'v_w_down': out['v_w_down'], 'v_final_norm_g': out['v_final_norm_g']}


def _loss(weights, diff, rest, loss_target):
    with _jax.named_scope("forward"):
        args = {**rest, TWIN_DIFF_INPUT: diff, **{k: w.astype(_WEIGHT_DTYPES[k]) for k, w in weights.items()}}
        y = _forward(args)
    with _jax.named_scope("loss_head"):
        err = _jnp.square(y.astype(_jnp.float32) - loss_target)
        return 0.5 * _jnp.sum(_jnp.mean(err, axis=-1)) if err.ndim else 0.5 * err


def _adamw(w, g, m, v):
    m = ADAM_B1 * m + (1.0 - ADAM_B1) * g
    v = ADAM_B2 * v + (1.0 - ADAM_B2) * _jnp.square(g)
    m_hat = m / (1.0 - ADAM_B1 ** ADAM_STEP)
    v_hat = v / (1.0 - ADAM_B2 ** ADAM_STEP)
    delta = -ADAM_LR * (m_hat / (_jnp.sqrt(v_hat) + ADAM_EPS) + ADAM_WD * w)
    return delta, m, v


def reference(x, c, ctx, c_ctx, ada_w, ada_b, norm1_g, w_in, q_norm_g, k_norm_g, ret_log_decay, ret_norm_g, gla_gate_up, gla_gate_b, gla_norm_g, w_out, norm2_g, w_up, conv_w, conv_b, w_down, final_norm_g, loss_target, m_c_ctx, m_ada_w, m_ada_b, m_norm1_g, m_w_in, m_q_norm_g, m_k_norm_g, m_ret_log_decay, m_ret_norm_g, m_gla_gate_up, m_gla_gate_b, m_gla_norm_g, m_w_out, m_norm2_g, m_w_up, m_conv_w, m_conv_b, m_w_down, m_final_norm_g, v_c_ctx, v_ada_w, v_ada_b, v_norm1_g, v_w_in, v_q_norm_g, v_k_norm_g, v_ret_log_decay, v_ret_norm_g, v_gla_gate_up, v_gla_gate_b, v_gla_norm_g, v_w_out, v_norm2_g, v_w_up, v_conv_w, v_conv_b, v_w_down, v_final_norm_g):
    given = dict(x=x, c=c, ctx=ctx, c_ctx=c_ctx, ada_w=ada_w, ada_b=ada_b, norm1_g=norm1_g, w_in=w_in, q_norm_g=q_norm_g, k_norm_g=k_norm_g, ret_log_decay=ret_log_decay, ret_norm_g=ret_norm_g, gla_gate_up=gla_gate_up, gla_gate_b=gla_gate_b, gla_norm_g=gla_norm_g, w_out=w_out, norm2_g=norm2_g, w_up=w_up, conv_w=conv_w, conv_b=conv_b, w_down=w_down, final_norm_g=final_norm_g, loss_target=loss_target, m_c_ctx=m_c_ctx, m_ada_w=m_ada_w, m_ada_b=m_ada_b, m_norm1_g=m_norm1_g, m_w_in=m_w_in, m_q_norm_g=m_q_norm_g, m_k_norm_g=m_k_norm_g, m_ret_log_decay=m_ret_log_decay, m_ret_norm_g=m_ret_norm_g, m_gla_gate_up=m_gla_gate_up, m_gla_gate_b=m_gla_gate_b, m_gla_norm_g=m_gla_norm_g, m_w_out=m_w_out, m_norm2_g=m_norm2_g, m_w_up=m_w_up, m_conv_w=m_conv_w, m_conv_b=m_conv_b, m_w_down=m_w_down, m_final_norm_g=m_final_norm_g, v_c_ctx=v_c_ctx, v_ada_w=v_ada_w, v_ada_b=v_ada_b, v_norm1_g=v_norm1_g, v_w_in=v_w_in, v_q_norm_g=v_q_norm_g, v_k_norm_g=v_k_norm_g, v_ret_log_decay=v_ret_log_decay, v_ret_norm_g=v_ret_norm_g, v_gla_gate_up=v_gla_gate_up, v_gla_gate_b=v_gla_gate_b, v_gla_norm_g=v_gla_norm_g, v_w_out=v_w_out, v_norm2_g=v_norm2_g, v_w_up=v_w_up, v_conv_w=v_conv_w, v_conv_b=v_conv_b, v_w_down=v_w_down, v_final_norm_g=v_final_norm_g)
    weights = {n: given[n] for n in TWIN_WEIGHTS}
    shared = {n: given[n] for n in SHARED_INPUTS}
    per_example = {n: given[n] for n in ['x', 'c', 'ctx']}
    grad_fn = _jax.value_and_grad(_loss, argnums=(0, 1))

    def one_microbatch(ex, loss_target):
        ex = dict(ex)
        diff = ex.pop(TWIN_DIFF_INPUT)
        return grad_fn(weights, diff, {**shared, **ex}, loss_target)

    if N_MICROBATCH == 1:
        loss, (grad_w, grad_x) = one_microbatch(per_example, given["loss_target"])
    else:
        def body(carry, xs):
            loss_sum, grad_sum = carry
            l_k, (gw_k, gx_k) = one_microbatch(xs[0], xs[1])
            with _jax.named_scope("update"):
                return (loss_sum + l_k, _jax.tree.map(_jnp.add, grad_sum, gw_k)), gx_k

        init = (_jnp.zeros((), _jnp.float32), _jax.tree.map(_jnp.zeros_like, weights))
        (loss, grad_w), grad_x = _jax.lax.scan(body, init, (per_example, given["loss_target"]))
    with _jax.named_scope("update"):
        delta_w, new_m, new_v = {}, {}, {}
        for n in TWIN_WEIGHTS:
            delta_w[n], new_m[n], new_v[n] = _adamw(weights[n], grad_w[n], given["m_" + n], given["v_" + n])
    return (loss, grad_x, *[grad_w[n] for n in TWIN_WEIGHTS], *[delta_w[n] for n in TWIN_WEIGHTS],
            *[new_m[n] for n in TWIN_WEIGHTS], *[new_v[n] for n in TWIN_WEIGHTS])
```

```python
import functools
import math
import types

import jax
import jax.numpy as jnp
import numpy as np
from jax import lax
from jax.experimental import pallas as pl
from jax.experimental.pallas import tpu as pltpu

F32 = jnp.float32
BF16 = jnp.bfloat16
HI = lax.Precision.HIGHEST

D_MODEL = 2048
SEQ = 2048
CTX_LEN = 256
GRID_W = 64
D_FF = 5632
DEPTH = 2
N_DEV = 8
HEAD_DIM = 128
ROPE_THETA = 10000.0
GLA_TAU = 16.0
GLA_RANK = 16
GLA_CHUNK = 64
GLA_SUB = 16
EPS = 1e-6
N_MOD = 6
ADAM_LR = 0.001
ADAM_B1 = 0.9
ADAM_B2 = 0.999
ADAM_EPS = 1e-08
ADAM_WD = 0.01
ADAM_STEP = 10
LANE = 128
VMEM_LIMIT = 56 * 1024 * 1024
NEG = -1e30


def _cfg():
    d = types.SimpleNamespace()
    d.D, d.L, d.LC, d.F = D_MODEL, SEQ, CTX_LEN, D_FF
    d.T = d.L + d.LC
    nm = d.D // HEAD_DIM
    d.HQ, d.HKV, d.HR, d.HG = nm // 2, nm // 8, nm // 4, nm // 4
    d.G = d.HQ // d.HKV
    w = dict(aq=d.HQ * 128, ak=d.HKV * 128, av=d.HKV * 128, rq=d.HR * 128, rk=d.HR * 128, rv=d.HR * 128,
             rg=d.HR * 128, gq=d.HG * 64, gk=d.HG * 64, gv=d.HG * 128, gr=d.HG * 128, ga=2 * GLA_RANK)
    off, o = {}, 0
    for k, v in w.items():
        off[k] = o
        o += v
    d.W, d.OFF, d.NIN = w, off, o
    d.NZ = -(-(off['ga'] + LANE) // 256) * 256
    d.NINS = d.NIN // N_DEV
    d.TM = math.gcd(d.LC, 128)
    d.TQ = math.gcd(d.LC, 256)
    return d


def _cp(sem=None):
    return pltpu.CompilerParams(dimension_semantics=sem, vmem_limit_bytes=VMEM_LIMIT)


def _tile(n, target, mult=LANE):
    t = min(n, target)
    t -= t % mult
    while t > mult and n % t:
        t -= mult
    return t if t > 0 and n % t == 0 else n


_DN = {'nn': ((1,), (0,)), 'nt': ((1,), (1,)), 'tn': ((0,), (0,))}


def _mm(name, a, b, kind, M, N, K, out_dtype, tm=768, tn=768, tk=1024, a_spec=None, b_spec=None,
        out_shape=None, out_spec=None):
    tm, tn = _tile(M, tm, 128), _tile(N, tn, 128)
    tk = _tile(K, tk, 128)
    nk = K // tk

    def body(a_ref, b_ref, o_ref, acc):
        kk = pl.program_id(2)

        @pl.when(kk == 0)
        def _():
            acc[...] = jnp.zeros_like(acc)

        acc[...] += lax.dot_general(a_ref[...].astype(BF16), b_ref[...].astype(BF16), (_DN[kind], ((), ())),
                                    preferred_element_type=F32)

        @pl.when(kk == nk - 1)
        def _():
            o_ref[...] = acc[...].astype(o_ref.dtype)

    if a_spec is None:
        a_spec = pl.BlockSpec((tk, tm), lambda i, j, k: (k, i)) if kind == 'tn' else pl.BlockSpec((tm, tk), lambda i, j, k: (i, k))
    if b_spec is None:
        b_spec = pl.BlockSpec((tn, tk), lambda i, j, k: (j, k)) if kind == 'nt' else pl.BlockSpec((tk, tn), lambda i, j, k: (k, j))
    if out_spec is None:
        out_spec = pl.BlockSpec((tm, tn), lambda i, j, k: (i, j))
        out_shape = (M, N)
    return pl.pallas_call(
        body, name=name, grid=(M // tm, N // tn, nk), in_specs=[a_spec, b_spec], out_specs=out_spec,
        out_shape=jax.ShapeDtypeStruct(out_shape, out_dtype), scratch_shapes=[pltpu.VMEM((tm, tn), F32)],
        compiler_params=_cp(("parallel", "parallel", "arbitrary")))(a, b)


def _row_specs(cf, tm, tiles, params):
    nctx = cf.LC // tm
    specs = []
    for arr, w, colf, _ in tiles:
        specs.append(pl.BlockSpec((tm, w), functools.partial(lambda i, r, colf: (i, colf(r)), colf=colf)))
    for arr, kind, _ in params:
        nd = arr.ndim
        if kind == 'shared':
            specs.append(pl.BlockSpec(arr.shape, functools.partial(lambda i, r, nd: (0,) * nd, nd=nd)))
        else:
            specs.append(pl.BlockSpec((None,) + arr.shape[1:],
                                      functools.partial(lambda i, r, nd, nctx: (jnp.where(i >= nctx, 1, 0),) + (0,) * (nd - 1), nd=nd, nctx=nctx)))
    return specs


def _row_fwd(cf, name, f, tiles, params, outs, tm, nrep=1):
    nt, npar = len(tiles), len(params)

    def body(*refs):
        tv = [r[...].astype(F32) for r in refs[:nt]]
        pv = [r[...] for r in refs[nt:nt + npar]]
        res = f(*tv, *pv)
        for o, v in zip(refs[nt + npar:], res):
            o[...] = v.astype(o.dtype)

    out_specs = [pl.BlockSpec((tm, w), functools.partial(lambda i, r, colf: (i, colf(r)), colf=colf)) for w, colf, _, _ in outs]
    out_shape = [jax.ShapeDtypeStruct((cf.T, tw), dt) for _, _, tw, dt in outs]
    return pl.pallas_call(
        body, name=name, grid=(cf.T // tm, nrep), in_specs=_row_specs(cf, tm, tiles, params), out_specs=out_specs,
        out_shape=out_shape, compiler_params=_cp(("arbitrary", "arbitrary")))(*[t[0] for t in tiles], *[p[0] for p in params])


def _row_bwd(cf, name, f, tiles, params, cts, tgrads, tm, nrep=1):
    nt, npar, nc = len(tiles), len(params), len(cts)
    tdiff = [k for k, t in enumerate(tiles) if t[3]]
    pdiff = [k for k, p in enumerate(params) if p[2]]
    nctx = cf.LC // tm

    def body(*refs):
        i, r = pl.program_id(0), pl.program_id(1)
        tv = [x[...].astype(F32) for x in refs[:nt]]
        pv = [x[...] for x in refs[nt:nt + npar]]
        cv = tuple(x[...].astype(F32) for x in refs[nt + npar:nt + npar + nc])
        outs = refs[nt + npar + nc:]

        def g(*diff):
            tv2, pv2 = list(tv), list(pv)
            for k, v in zip(tdiff, diff[:len(tdiff)]):
                tv2[k] = v
            for k, v in zip(pdiff, diff[len(tdiff):]):
                pv2[k] = v
            return tuple(f(*tv2, *pv2))

        _, vjp_fn = jax.vjp(g, *[tv[k] for k in tdiff], *[pv[k] for k in pdiff])
        grads = vjp_fn(cv)
        for o, gv in zip(outs[:len(tdiff)], grads[:len(tdiff)]):
            o[...] = gv.astype(o.dtype)
        for n_, (o, gv) in enumerate(zip(outs[len(tdiff):], grads[len(tdiff):])):
            if params[pdiff[n_]][1] == 'shared':
                first = jnp.logical_and(i == 0, r == 0)
            else:
                first = jnp.logical_and(jnp.logical_or(i == 0, i == nctx), r == 0)

            @pl.when(first)
            def _():
                o[...] = gv

            @pl.when(jnp.logical_not(first))
            def _():
                o[...] += gv

    in_specs = _row_specs(cf, tm, tiles, params)
    in_specs += [pl.BlockSpec((tm, w), functools.partial(lambda i, r, colf: (i, colf(r)), colf=colf)) for _, w, colf in cts]
    out_specs = [pl.BlockSpec((tm, w), functools.partial(lambda i, r, colf: (i, colf(r)), colf=colf)) for w, colf, _, _ in tgrads]
    out_shape = [jax.ShapeDtypeStruct((cf.T, tw), dt) for _, _, tw, dt in tgrads]
    pspecs = _row_specs(cf, tm, [], [params[k] for k in pdiff])
    out_specs += pspecs
    out_shape += [jax.ShapeDtypeStruct(params[k][0].shape, F32) for k in pdiff]
    res = pl.pallas_call(
        body, name=name, grid=(cf.T // tm, nrep), in_specs=in_specs, out_specs=out_specs, out_shape=out_shape,
        compiler_params=_cp(("arbitrary", "arbitrary")))(*[t[0] for t in tiles], *[p[0] for p in params], *[c[0] for c in cts])
    return res[:len(tdiff)], res[len(tdiff):]


def _c0(r):
    return 0


def _col(base):
    return lambda r: base + r


def _rms(x, g):
    return x * lax.rsqrt(jnp.mean(x * x, axis=-1, keepdims=True) + EPS) * g


def _sigmoid(x):
    return 1.0 / (1.0 + jnp.exp(-x))


def _silu(x):
    return x * _sigmoid(x)


def _f_norm_mod(x, g, sh, sc):
    return (_rms(x, g) * (1 + sc) + sh,)


def _f_resid_norm_mod(x, y, gate, g, sh, sc):
    x1 = x + gate * y
    return (x1, _rms(x1, g) * (1 + sc) + sh)


@jax.custom_vjp
def _swap_halves(t):
    return pltpu.roll(t, HEAD_DIM // 2, axis=1)


def _swap_fwd(t):
    return _swap_halves(t), None


def _swap_bwd(_, g):
    return (pltpu.roll(g, HEAD_DIM // 2, axis=1),)


_swap_halves.defvjp(_swap_fwd, _swap_bwd)


def _rope(t, cs, sn):
    return t * cs + _swap_halves(t) * sn


def _f_prep_norm(t, cs, sn, g):
    return (_rope(_rms(t, g), cs, sn),)


def _f_prep_plain(t, cs, sn):
    return (_rope(t, cs, sn),)


def _f_prep_scaled(t, cs, sn):
    return (_rope(t * (HEAD_DIM ** -0.5), cs, sn),)


def _log_sigmoid(x):
    return jnp.minimum(x, 0.0) - jnp.log(1.0 + jnp.exp(-jnp.abs(x)))


def _f_gates(ga, gf, gb, bf, bb):
    gab = ga.astype(BF16)
    lf = jnp.dot(gab, gf.astype(BF16), preferred_element_type=F32) + bf
    lb = jnp.dot(gab, gb.astype(BF16), preferred_element_type=F32) + bb
    return (_log_sigmoid(lf) / GLA_TAU, _log_sigmoid(lb) / GLA_TAU)


def _f_gated_norm(o, g, n):
    return (_rms(o, n) * _silu(g),)


def _f_first(x, g, sh, sc):
    return (x, _rms(x, g) * (1 + sc) + sh)


def _loss_grad(cf, x1, yff, gate, gfin, tgt):
    tm, T, D = cf.TM, cf.T, cf.D
    nctx = cf.LC // tm

    def lossf(x1v, yv, gt, gf, tg):
        y = _rms(x1v + gt * yv, gf)
        e = y - tg
        return 0.5 * jnp.sum(jnp.mean(e * e, axis=-1, keepdims=True), axis=0, keepdims=True)

    def body(x1_ref, y_ref, gt_ref, gf_ref, tg_ref, dx_ref, dy_ref, dgt_ref, dgf_ref, ls_ref):
        i = pl.program_id(0)
        lat = (i >= nctx).astype(F32)
        val, vjp_fn = jax.vjp(lossf, x1_ref[...], y_ref[...].astype(F32), gt_ref[...], gf_ref[...], tg_ref[...])
        dx, dy, dgt, dgf, _ = vjp_fn(jnp.ones((1, 1), F32) * lat)
        dx_ref[...] = dx
        dy_ref[...] = dy.astype(dy_ref.dtype)
        first_s = jnp.logical_or(i == 0, i == nctx)

        @pl.when(first_s)
        def _():
            dgt_ref[...] = dgt

        @pl.when(jnp.logical_not(first_s))
        def _():
            dgt_ref[...] += dgt

        @pl.when(i == 0)
        def _():
            dgf_ref[...] = dgf
            ls_ref[...] = jnp.zeros_like(ls_ref) + val * lat

        @pl.when(i != 0)
        def _():
            dgf_ref[...] += dgf
            ls_ref[...] += val * lat

    row = pl.BlockSpec((tm, D), lambda i: (i, 0))
    strm = pl.BlockSpec((None, 1, D), lambda i: (jnp.where(i >= nctx, 1, 0), 0, 0))
    one = pl.BlockSpec((1, D), lambda i: (0, 0))
    return pl.pallas_call(
        body, name="loss_grad", grid=(T // tm,), in_specs=[row, row, strm, one, row],
        out_specs=[row, row, strm, one, pl.BlockSpec((8, LANE), lambda i: (0, 0))],
        out_shape=[jax.ShapeDtypeStruct((T, D), F32), jax.ShapeDtypeStruct((T, D), BF16),
                   jax.ShapeDtypeStruct((2, 1, D), F32), jax.ShapeDtypeStruct((1, D), F32),
                   jax.ShapeDtypeStruct((8, LANE), F32)],
        compiler_params=_cp(("arbitrary",)))(x1, yff, gate, gfin, tgt)


def _att_mask(cf, i, tq):
    col = lax.broadcasted_iota(jnp.int32, (tq, cf.T), 1)
    return jnp.logical_or(col < cf.LC, i >= cf.LC // tq)


def _att_probs(q, k, mask):
    s = lax.dot_general(q, k, (_DN['nt'], ((), ())), preferred_element_type=F32) * (HEAD_DIM ** -0.5)
    s = jnp.where(mask, s, NEG)
    e = jnp.exp(s - jnp.max(s, axis=-1, keepdims=True))
    return e / jnp.sum(e, axis=-1, keepdims=True)


def _att_fwd(cf, q, k, z):
    tq, T, G = cf.TQ, cf.T, cf.G
    vb = cf.OFF['av'] // LANE

    def body(q_ref, k_ref, v_ref, o_ref):
        mask = _att_mask(cf, pl.program_id(1), tq)
        kv, vv = k_ref[...], v_ref[...].astype(BF16)
        for j in range(G):
            p = _att_probs(q_ref[:, j * LANE:(j + 1) * LANE], kv, mask)
            o_ref[:, j * LANE:(j + 1) * LANE] = jnp.dot(p.astype(BF16), vv, preferred_element_type=F32).astype(o_ref.dtype)

    return pl.pallas_call(
        body, name="att_fwd", grid=(cf.HKV, T // tq),
        in_specs=[pl.BlockSpec((tq, G * LANE), lambda g, i: (i, g)), pl.BlockSpec((T, LANE), lambda g, i: (0, g)),
                  pl.BlockSpec((T, LANE), lambda g, i: (0, vb + g))],
        out_specs=pl.BlockSpec((tq, G * LANE), lambda g, i: (i, g)),
        out_shape=jax.ShapeDtypeStruct((T, cf.HQ * LANE), BF16), compiler_params=_cp(("arbitrary", "arbitrary")))(q, k, z)


def _att_bwd(cf, q, k, z, dcat):
    tq, T, G = cf.TQ, cf.T, cf.G
    vb = cf.OFF['av'] // LANE
    sc = HEAD_DIM ** -0.5

    def body(q_ref, k_ref, v_ref, do_ref, dq_ref, dk_ref, dv_ref):
        i = pl.program_id(1)
        mask = _att_mask(cf, i, tq)
        kv, vv = k_ref[...], v_ref[...].astype(BF16)
        dk = jnp.zeros((T, LANE), F32)
        dv = jnp.zeros((T, LANE), F32)
        for j in range(G):
            qj = q_ref[:, j * LANE:(j + 1) * LANE]
            do = do_ref[:, j * LANE:(j + 1) * LANE]
            p = _att_probs(qj, kv, mask)
            dv += lax.dot_general(p.astype(BF16), do, (_DN['tn'], ((), ())), preferred_element_type=F32)
            dp = lax.dot_general(do, vv, (_DN['nt'], ((), ())), preferred_element_type=F32)
            ds = p * (dp - jnp.sum(dp * p, axis=-1, keepdims=True)) * sc
            dsb = ds.astype(BF16)
            dq_ref[:, j * LANE:(j + 1) * LANE] = jnp.dot(dsb, kv, preferred_element_type=F32)
            dk += lax.dot_general(dsb, qj, (_DN['tn'], ((), ())), preferred_element_type=F32)

        @pl.when(i == 0)
        def _():
            dk_ref[...] = dk
            dv_ref[...] = dv

        @pl.when(i != 0)
        def _():
            dk_ref[...] += dk
            dv_ref[...] += dv

    qs = pl.BlockSpec((tq, G * LANE), lambda g, i: (i, g))
    ks = pl.BlockSpec((T, LANE), lambda g, i: (0, g))
    return pl.pallas_call(
        body, name="att_bwd", grid=(cf.HKV, T // tq),
        in_specs=[qs, ks, pl.BlockSpec((T, LANE), lambda g, i: (0, vb + g)), qs],
        out_specs=[qs, ks, ks],
        out_shape=[jax.ShapeDtypeStruct((T, cf.HQ * LANE), F32), jax.ShapeDtypeStruct((T, cf.HKV * LANE), F32),
                   jax.ShapeDtypeStruct((T, cf.HKV * LANE), F32)],
        compiler_params=_cp(("arbitrary", "arbitrary")))(q, k, z, dcat)


def _ret_masks(cf, i, tq, lgf, lgb):
    T, LC = cf.T, cf.LC
    row = lax.broadcasted_iota(jnp.int32, (tq, T), 0) + i * tq
    col = lax.broadcasted_iota(jnp.int32, (tq, T), 1)

    def pb(n):
        return jnp.where(n < LC, LC - 1 - n, T + LC - 1 - n)

    relf = row - col
    relb = pb(row) - pb(col)
    okf, okb = relf >= 0, relb >= 0
    rf = jnp.where(okf, relf, 0).astype(F32)
    rb = jnp.where(okb, relb, 0).astype(F32)
    mf = jnp.where(okf, jnp.exp(lgf * rf), 0.0)
    mb = jnp.where(okb, jnp.exp(lgb * rb), 0.0)
    return mf, mb, rf, rb


def _ret_fwd(cf, q, k, z, lg):
    tq, T = cf.TQ, cf.T
    vb = cf.OFF['rv'] // LANE

    def body(lg_ref, q_ref, k_ref, v_ref, o_ref):
        h, i = pl.program_id(0), pl.program_id(1)
        mf, mb, _, _ = _ret_masks(cf, i, tq, lg_ref[0, h], lg_ref[1, h])
        a = lax.dot_general(q_ref[...], k_ref[...], (_DN['nt'], ((), ())), preferred_element_type=F32)
        p = (a * (mf + mb)).astype(BF16)
        o_ref[...] = jnp.dot(p, v_ref[...].astype(BF16), preferred_element_type=F32)

    return pl.pallas_call(
        body, name="ret_fwd", grid=(cf.HR, T // tq),
        in_specs=[pl.BlockSpec(memory_space=pltpu.SMEM), pl.BlockSpec((tq, LANE), lambda h, i: (i, h)),
                  pl.BlockSpec((T, LANE), lambda h, i: (0, h)), pl.BlockSpec((T, LANE), lambda h, i: (0, vb + h))],
        out_specs=pl.BlockSpec((tq, LANE), lambda h, i: (i, h)),
        out_shape=jax.ShapeDtypeStruct((T, cf.HR * LANE), F32), compiler_params=_cp(("arbitrary", "arbitrary")))(lg, q, k, z)


def _ret_bwd(cf, q, k, z, lg, do):
    tq, T = cf.TQ, cf.T
    vb = cf.OFF['rv'] // LANE

    def body(lg_ref, q_ref, k_ref, v_ref, do_ref, dq_ref, dk_ref, dv_ref, dlg_ref):
        h, i = pl.program_id(0), pl.program_id(1)
        mf, mb, rf, rb = _ret_masks(cf, i, tq, lg_ref[0, h], lg_ref[1, h])
        qv, kv, vv = q_ref[...], k_ref[...], v_ref[...].astype(BF16)
        dob = do_ref[...].astype(BF16)
        a = lax.dot_general(qv, kv, (_DN['nt'], ((), ())), preferred_element_type=F32)
        m = mf + mb
        p = (a * m).astype(BF16)
        dv = lax.dot_general(p, dob, (_DN['tn'], ((), ())), preferred_element_type=F32)
        dp = lax.dot_general(dob, vv, (_DN['nt'], ((), ())), preferred_element_type=F32)
        da = (dp * m).astype(BF16)
        dq_ref[...] = jnp.dot(da, kv, preferred_element_type=F32)
        dk = lax.dot_general(da, qv, (_DN['tn'], ((), ())), preferred_element_type=F32)
        dm = dp * a
        dlf = jnp.sum(jnp.sum(dm * mf * rf, axis=-1, keepdims=True), axis=0, keepdims=True)
        dlb = jnp.sum(jnp.sum(dm * mb * rb, axis=-1, keepdims=True), axis=0, keepdims=True)
        rid = lax.broadcasted_iota(jnp.int32, (8, LANE), 0)
        dl = jnp.where(rid == 0, dlf, jnp.where(rid == 1, dlb, 0.0))

        @pl.when(i == 0)
        def _():
            dk_ref[...] = dk
            dv_ref[...] = dv
            dlg_ref[...] = dl

        @pl.when(i != 0)
        def _():
            dk_ref[...] += dk
            dv_ref[...] += dv
            dlg_ref[...] += dl

    qs = pl.BlockSpec((tq, LANE), lambda h, i: (i, h))
    ks = pl.BlockSpec((T, LANE), lambda h, i: (0, h))
    return pl.pallas_call(
        body, name="ret_bwd", grid=(cf.HR, T // tq),
        in_specs=[pl.BlockSpec(memory_space=pltpu.SMEM), qs, ks, pl.BlockSpec((T, LANE), lambda h, i: (0, vb + h)), qs],
        out_specs=[qs, ks, ks, pl.BlockSpec((None, 8, LANE), lambda h, i: (h, 0, 0))],
        out_shape=[jax.ShapeDtypeStruct((T, cf.HR * LANE), F32)] * 3 + [jax.ShapeDtypeStruct((cf.HR, 8, LANE), F32)],
        compiler_params=_cp(("arbitrary", "arbitrary")))(lg, q, k, z, do)


def _gla_consts(d):
    C, SB = GLA_CHUNK, GLA_SUB
    nsb = C // SB
    fwd = d == 0
    r = lax.broadcasted_iota(jnp.int32, (C, C), 0)
    m = lax.broadcasted_iota(jnp.int32, (C, C), 1)
    allowed = jnp.where(fwd, r - m, m - r) >= 0
    blocks = [allowed]
    vis = []
    rr = lax.broadcasted_iota(jnp.int32, (C, LANE), 0)
    for b in range(nsb):
        blocks.append(jnp.where(fwd, SB * b - 1 - m, m - SB * (b + 1)) >= 0)
        vis.append(jnp.where(fwd, SB * (b + 1) - 1 - rr, rr - SB * b) >= 0)
    cm = jnp.concatenate([x.astype(F32) for x in blocks] + [jnp.ones((LANE, C), F32)], axis=0)
    return cm, allowed, vis


def _gla_step(q, k, v, la, st, lmask, cm, allowed, vis):
    C, SB = GLA_CHUNK, GLA_SUB
    nsb = C // SB
    qs = q * lmask * ((HEAD_DIM // 2) ** -0.5)
    ks = k * lmask
    cums = jnp.dot(cm, la, precision=HI, preferred_element_type=F32)
    cum = cums[0:C]
    tot = cums[(1 + nsb) * C:]
    rows = []
    for b in range(nsb):
        ref = cums[(1 + b) * C:(2 + b) * C]
        sl = slice(b * SB, (b + 1) * SB)
        qt = qs[sl] * jnp.exp(cum[sl] - ref[sl])
        kt = jnp.where(vis[b], ks * jnp.exp(jnp.where(vis[b], ref - cum, 0.0)), 0.0)
        rows.append(lax.dot_general(qt, kt, (_DN['nt'], ((), ())), precision=HI, preferred_element_type=F32))
    att = jnp.where(allowed, jnp.concatenate(rows, axis=0), 0.0)
    o = jnp.dot(att.astype(BF16), v.astype(BF16), preferred_element_type=F32)
    o += lax.dot_general((qs * jnp.exp(cum)).astype(BF16), st.astype(BF16), (_DN['nt'], ((), ())), preferred_element_type=F32)
    kd = (ks * jnp.exp(tot[0:C] - cum)).astype(BF16)
    st_new = st * jnp.exp(tot) + lax.dot_general(v.astype(BF16), kd, (_DN['tn'], ((), ())), preferred_element_type=F32)
    return o, st_new


def _gla_chunk_id(cf, s, d):
    nct, nc = cf.LC // GLA_CHUNK, cf.T // GLA_CHUNK
    back = jnp.where(s < nct, nct - 1 - s, nc + nct - 1 - s)
    return jnp.where(d == 0, s, back)


def _gla_specs(cf):
    T = cf.T
    qb, kb, vb = cf.OFF['gq'] // LANE, cf.OFF['gk'] // LANE, cf.OFF['gv'] // LANE
    qs = pl.BlockSpec((T, LANE), lambda p, h, d: (0, qb + p))
    ks = pl.BlockSpec((T, LANE), lambda p, h, d: (0, kb + p))
    vs = pl.BlockSpec((T, LANE), lambda p, h, d: (0, vb + 2 * p + h))
    las = pl.BlockSpec((T, LANE), lambda p, h, d: (0, p))
    return qs, ks, vs, las


def _gla_fwd(cf, z, laf, lab):
    T, C = cf.T, GLA_CHUNK
    nc = T // C
    qs, ks, vs, las = _gla_specs(cf)

    def body(q_ref, k_ref, v_ref, laf_ref, lab_ref, o_ref, sst_ref, st_scr):
        h, d = pl.program_id(1), pl.program_id(2)
        cm, allowed, vis = _gla_consts(d)
        lmask = (lax.broadcasted_iota(jnp.int32, (1, LANE), 1) // (LANE // 2) == h).astype(F32)
        st_scr[...] = jnp.zeros_like(st_scr)

        def loop(s, carry):
            cid = _gla_chunk_id(cf, s, d)
            rows = pl.ds(pl.multiple_of(cid * C, C), C)
            la = jnp.where(d == 0, laf_ref[rows, :], lab_ref[rows, :])
            st = st_scr[...]
            sst_ref[cid] = st
            o, stn = _gla_step(q_ref[rows, :], k_ref[rows, :], v_ref[rows, :], la, st, lmask, cm, allowed, vis)
            st_scr[...] = stn

            @pl.when(d == 0)
            def _():
                o_ref[rows, :] = o

            @pl.when(d != 0)
            def _():
                o_ref[rows, :] += o

            return carry

        lax.fori_loop(0, nc, loop, 0)

    return pl.pallas_call(
        body, name="gla_fwd", grid=(cf.HG // 2, 2, 2), in_specs=[qs, ks, vs, las, las],
        out_specs=[pl.BlockSpec((T, LANE), lambda p, h, d: (0, 2 * p + h)),
                   pl.BlockSpec((None, nc, LANE, LANE), lambda p, h, d: ((2 * p + h) * 2 + d, 0, 0, 0))],
        out_shape=[jax.ShapeDtypeStruct((T, cf.HG * LANE), F32), jax.ShapeDtypeStruct((cf.HG * 2, nc, LANE, LANE), F32)],
        scratch_shapes=[pltpu.VMEM((LANE, LANE), F32)],
        compiler_params=_cp(("arbitrary", "arbitrary", "arbitrary")))(z, z, z, laf, lab)


def _gla_bwd(cf, z, laf, lab, sst, do):
    T, C = cf.T, GLA_CHUNK
    nc = T // C
    qs, ks, vs, las = _gla_specs(cf)

    def body(q_ref, k_ref, v_ref, laf_ref, lab_ref, sst_ref, do_ref, dq_ref, dk_ref, dv_ref, dlf_ref, dlb_ref, dst_scr):
        h, d = pl.program_id(1), pl.program_id(2)
        cm, allowed, vis = _gla_consts(d)
        lmask = (lax.broadcasted_iota(jnp.int32, (1, LANE), 1) // (LANE // 2) == h).astype(F32)
        dst_scr[...] = jnp.zeros_like(dst_scr)

        @pl.when(jnp.logical_and(h == 0, d == 0))
        def _():
            dq_ref[...] = jnp.zeros_like(dq_ref)
            dk_ref[...] = jnp.zeros_like(dk_ref)
            dlf_ref[...] = jnp.zeros_like(dlf_ref)
            dlb_ref[...] = jnp.zeros_like(dlb_ref)

        @pl.when(d == 0)
        def _():
            dv_ref[...] = jnp.zeros_like(dv_ref)

        isf = (d == 0).astype(F32)

        def loop(t, carry):
            cid = _gla_chunk_id(cf, nc - 1 - t, d)
            rows = pl.ds(pl.multiple_of(cid * C, C), C)
            la = jnp.where(d == 0, laf_ref[rows, :], lab_ref[rows, :])
            step = functools.partial(_gla_step, lmask=lmask, cm=cm, allowed=allowed, vis=vis)
            _, vjp_fn = jax.vjp(step, q_ref[rows, :], k_ref[rows, :], v_ref[rows, :], la, sst_ref[cid])
            dq, dk, dv, dla, dst = vjp_fn((do_ref[rows, :], dst_scr[...]))
            dst_scr[...] = dst
            dq_ref[rows, :] += dq
            dk_ref[rows, :] += dk
            dv_ref[rows, :] += dv
            dlf_ref[rows, :] += dla * isf
            dlb_ref[rows, :] += dla * (1.0 - isf)
            return carry

        lax.fori_loop(0, nc, loop, 0)

    pair = pl.BlockSpec((T, LANE), lambda p, h, d: (0, p))
    head = pl.BlockSpec((T, LANE), lambda p, h, d: (0, 2 * p + h))
    npair = cf.HG // 2
    return pl.pallas_call(
        body, name="gla_bwd", grid=(npair, 2, 2),
        in_specs=[qs, ks, vs, las, las, pl.BlockSpec((None, nc, LANE, LANE), lambda p, h, d: ((2 * p + h) * 2 + d, 0, 0, 0)), head],
        out_specs=[pair, pair, head, pair, pair],
        out_shape=[jax.ShapeDtypeStruct((T, npair * LANE), F32), jax.ShapeDtypeStruct((T, npair * LANE), F32),
                   jax.ShapeDtypeStruct((T, cf.HG * LANE), F32), jax.ShapeDtypeStruct((T, npair * LANE), F32),
                   jax.ShapeDtypeStruct((T, npair * LANE), F32)],
        scratch_shapes=[pltpu.VMEM((LANE, LANE), F32)],
        compiler_params=_cp(("arbitrary", "arbitrary", "arbitrary")))(z, z, z, laf, lab, sst, do)


def _conv_parts(cf, a, w_ref):
    T, LC = cf.T, cf.LC
    rid = lax.broadcasted_iota(jnp.int32, a.shape, 0)
    first = jnp.logical_or(rid == 0, rid == LC)
    last = jnp.logical_or(rid == LC - 1, rid == T - 1)
    ap = jnp.where(first, 0.0, pltpu.roll(a, 1, axis=0))
    an = jnp.where(last, 0.0, pltpu.roll(a, T - 1, axis=0))
    w0, w1, w2, b = w_ref[0:1, :], w_ref[1:2, :], w_ref[2:3, :], w_ref[3:4, :]
    ac = ap * w0 + a * w1 + an * w2 + b
    return ap, an, ac, first, last, (w0, w1, w2)


def _conv_fwd(cf, u, wb):
    T, Fd = cf.T, cf.F
    tc = _tile(Fd, 512)
    nj = Fd // tc

    def body(a_ref, v_ref, w_ref, t_ref):
        _, _, ac, _, _, _ = _conv_parts(cf, a_ref[...], w_ref)
        t_ref[...] = (_silu(ac) * v_ref[...]).astype(t_ref.dtype)

    return pl.pallas_call(
        body, name="conv_fwd", grid=(nj,),
        in_specs=[pl.BlockSpec((T, tc), lambda j: (0, j)), pl.BlockSpec((T, tc), lambda j: (0, nj + j)),
                  pl.BlockSpec((8, tc), lambda j: (0, j))],
        out_specs=pl.BlockSpec((T, tc), lambda j: (0, j)), out_shape=jax.ShapeDtypeStruct((T, Fd), BF16),
        compiler_params=_cp(("parallel",)))(u, u, wb)


def _conv_bwd(cf, u, wb, dt):
    T, Fd = cf.T, cf.F
    tc = _tile(Fd, 256)
    nj = Fd // tc

    def body(a_ref, v_ref, w_ref, dt_ref, da_ref, dv_ref, dw_ref):
        a, v, dtv = a_ref[...], v_ref[...], dt_ref[...].astype(F32)
        ap, an, ac, first, last, (w0, w1, w2) = _conv_parts(cf, a, w_ref)
        sg = _sigmoid(ac)
        dv_ref[...] = (dtv * ac * sg).astype(dv_ref.dtype)
        dac = dtv * v * (sg * (1.0 + ac * (1.0 - sg)))
        from_next = pltpu.roll(jnp.where(first, 0.0, dac), T - 1, axis=0)
        from_prev = pltpu.roll(jnp.where(last, 0.0, dac), 1, axis=0)
        da_ref[...] = (dac * w1 + from_next * w0 + from_prev * w2).astype(da_ref.dtype)
        rows = [jnp.sum(dac * ap, axis=0, keepdims=True), jnp.sum(dac * a, axis=0, keepdims=True),
                jnp.sum(dac * an, axis=0, keepdims=True), jnp.sum(dac, axis=0, keepdims=True)]
        rid = lax.broadcasted_iota(jnp.int32, (8, tc), 0)
        dw = jnp.zeros((8, tc), F32)
        for n_, rw in enumerate(rows):
            dw = jnp.where(rid == n_, rw, dw)
        dw_ref[...] = dw

    col = pl.BlockSpec((T, tc), lambda j: (0, j))
    return pl.pallas_call(
        body, name="conv_bwd", grid=(nj,),
        in_specs=[col, pl.BlockSpec((T, tc), lambda j: (0, nj + j)), pl.BlockSpec((8, tc), lambda j: (0, j)), col],
        out_specs=[col, col, pl.BlockSpec((8, tc), lambda j: (0, j))],
        out_shape=[jax.ShapeDtypeStruct((T, Fd), BF16), jax.ShapeDtypeStruct((T, Fd), BF16), jax.ShapeDtypeStruct((8, Fd), F32)],
        compiler_params=_cp(("parallel",)))(u, u, wb, dt)


def _me():
    x, y, c = lax.axis_index("x"), lax.axis_index("y"), lax.axis_index("c")
    return x, y, c, 4 * x + 2 * y + c


def _peer(x, y, c, k):
    px = 1 - x if (k >> 2) & 1 else x
    py = 1 - y if (k >> 1) & 1 else y
    pc = 1 - c if k & 1 else c
    return (px, py, pc), 4 * px + 2 * py + pc


def _rcopy(src, dst, ss, rs, tgt):
    return pltpu.make_async_remote_copy(src_ref=src, dst_ref=dst, send_sem=ss, recv_sem=rs, device_id=tgt,
                                        device_id_type=pl.DeviceIdType.MESH)


def _ag_small(name, v):
    R, Cc = v.shape

    def body(v_ref, o_ref, ssem, rsem, lsem):
        x, y, c, me = _me()
        loc = pltpu.make_async_copy(v_ref, o_ref.at[me], lsem)
        loc.start()
        sends = []
        for k in range(1, N_DEV):
            tgt, _ = _peer(x, y, c, k)
            cp = _rcopy(v_ref, o_ref.at[me], ssem.at[k - 1], rsem.at[k - 1], tgt)
            cp.start()
            sends.append(cp)
        for k in range(1, N_DEV):
            tgt, pi = _peer(x, y, c, k)
            _rcopy(v_ref, o_ref.at[pi], ssem.at[k - 1], rsem.at[k - 1], tgt).wait_recv()
        for cp in sends:
            cp.wait_send()
        loc.wait()

    vm = pl.BlockSpec(memory_space=pltpu.VMEM)
    return pl.pallas_call(
        body, name=name, in_specs=[vm], out_specs=vm, out_shape=jax.ShapeDtypeStruct((N_DEV, R, Cc), v.dtype),
        scratch_shapes=[pltpu.SemaphoreType.DMA((N_DEV - 1,)), pltpu.SemaphoreType.DMA((N_DEV - 1,)), pltpu.SemaphoreType.DMA],
        compiler_params=pltpu.CompilerParams(vmem_limit_bytes=VMEM_LIMIT))(v)


def _ag_weights(cf, w_in_s, w_out_s, w_up_s, w_down_s):
    D, Fd = cf.D, cf.F
    ro, cu, rd = D // N_DEV, 2 * Fd // N_DEV, Fd // N_DEV
    na = 4

    def body(i_ref, o_ref, u_ref, d_ref, gi, go, gu, gd, ssem, rsem, lsem):
        x, y, c, me = _me()

        def places(idx):
            return [gi.at[idx], go.at[:, pl.ds(pl.multiple_of(idx * ro, ro), ro), :],
                    gu.at[:, :, pl.ds(pl.multiple_of(idx * cu, cu), cu)], gd.at[:, pl.ds(pl.multiple_of(idx * rd, rd), rd), :]]

        srcs = [i_ref, o_ref, u_ref, d_ref]
        mine = places(me)
        locs = [pltpu.make_async_copy(s, p, lsem.at[n]) for n, (s, p) in enumerate(zip(srcs, mine))]
        for cp in locs:
            cp.start()
        sends = []
        for k in range(1, N_DEV):
            tgt, _ = _peer(x, y, c, k)
            for n in range(na):
                cp = _rcopy(srcs[n], mine[n], ssem.at[(k - 1) * na + n], rsem.at[(k - 1) * na + n], tgt)
                cp.start()
                sends.append(cp)
        for k in range(1, N_DEV):
            tgt, pi = _peer(x, y, c, k)
            theirs = places(pi)
            for n in range(na):
                _rcopy(srcs[n], theirs[n], ssem.at[(k - 1) * na + n], rsem.at[(k - 1) * na + n], tgt).wait_recv()
        for cp in sends:
            cp.wait_send()
        for cp in locs:
            cp.wait()

    hbm = pl.BlockSpec(memory_space=pl.ANY)
    ns = (N_DEV - 1) * na
    return pl.pallas_call(
        body, name="ag_weights", in_specs=[hbm] * 4, out_specs=[hbm] * 4,
        out_shape=[jax.ShapeDtypeStruct((N_DEV, DEPTH, D, cf.NINS), BF16), jax.ShapeDtypeStruct((DEPTH, D, D), BF16),
                   jax.ShapeDtypeStruct((DEPTH, D, 2 * Fd), BF16), jax.ShapeDtypeStruct((DEPTH, Fd, D), BF16)],
        scratch_shapes=[pltpu.SemaphoreType.DMA((ns,)), pltpu.SemaphoreType.DMA((ns,)), pltpu.SemaphoreType.DMA((na,))],
        compiler_params=pltpu.CompilerParams(vmem_limit_bytes=VMEM_LIMIT))(w_in_s, w_out_s, w_up_s, w_down_s)


def _rs_grads(cf, gin, gout, gup, gdown):
    D, Fd = cf.D, cf.F
    ro, rd = D // N_DEV, Fd // N_DEV
    na = 4 * DEPTH

    def body(*refs):
        srcs = refs[:na]
        rin, rout, rup, rdown = refs[na:na + 4]
        ssem, rsem, lsem = refs[na + 4:]
        x, y, c, me = _me()

        def slabs(j):
            out = []
            for l in range(DEPTH):
                out += [srcs[l].at[j], srcs[DEPTH + l].at[pl.ds(pl.multiple_of(j * ro, ro), ro), :],
                        srcs[2 * DEPTH + l].at[j], srcs[3 * DEPTH + l].at[pl.ds(pl.multiple_of(j * rd, rd), rd), :]]
            return out

        def slots(s):
            out = []
            for l in range(DEPTH):
                out += [rin.at[s, l], rout.at[s, l], rup.at[s, l], rdown.at[s, l]]
            return out

        mine = slots(me)
        locs = [pltpu.make_async_copy(s, p, lsem.at[n]) for n, (s, p) in enumerate(zip(slabs(me), mine))]
        for cp in locs:
            cp.start()
        sends = []
        for k in range(1, N_DEV):
            tgt, pi = _peer(x, y, c, k)
            for n, s in enumerate(slabs(pi)):
                cp = _rcopy(s, mine[n], ssem.at[(k - 1) * na + n], rsem.at[(k - 1) * na + n], tgt)
                cp.start()
                sends.append(cp)
        for k in range(1, N_DEV):
            tgt, pi = _peer(x, y, c, k)
            for n, (s, p) in enumerate(zip(slabs(me), slots(pi))):
                _rcopy(s, p, ssem.at[(k - 1) * na + n], rsem.at[(k - 1) * na + n], tgt).wait_recv()
        for cp in sends:
            cp.wait_send()
        for cp in locs:
            cp.wait()

    hbm = pl.BlockSpec(memory_space=pl.ANY)
    ns = (N_DEV - 1) * na
    return pl.pallas_call(
        body, name="rs_grads", in_specs=[hbm] * na, out_specs=[hbm] * 4,
        out_shape=[jax.ShapeDtypeStruct((N_DEV, DEPTH, D, cf.NINS), BF16), jax.ShapeDtypeStruct((N_DEV, DEPTH, ro, D), BF16),
                   jax.ShapeDtypeStruct((N_DEV, DEPTH, D, 2 * Fd // N_DEV), BF16), jax.ShapeDtypeStruct((N_DEV, DEPTH, rd, D), BF16)],
        scratch_shapes=[pltpu.SemaphoreType.DMA((ns,)), pltpu.SemaphoreType.DMA((ns,)), pltpu.SemaphoreType.DMA((na,))],
        compiler_params=pltpu.CompilerParams(vmem_limit_bytes=VMEM_LIMIT))(*gin, *gout, *gup, *gdown)


def _adam_vals(w, g, m, v):
    m2 = ADAM_B1 * m + (1.0 - ADAM_B1) * g
    v2 = ADAM_B2 * v + (1.0 - ADAM_B2) * (g * g)
    mh = m2 / (1.0 - ADAM_B1 ** ADAM_STEP)
    vh = v2 / (1.0 - ADAM_B2 ** ADAM_STEP)
    return -ADAM_LR * (mh / (jnp.sqrt(vh) + ADAM_EPS) + ADAM_WD * w), m2, v2


def _row_tile(R, Cc, budget_elems):
    t = max(16, min(R, (budget_elems // max(Cc, 1)) // 16 * 16))
    while t > 16 and R % t:
        t -= 16
    return t if R % t == 0 else R


def _cast_bf16(name, w):
    Dp, R, Cc = w.shape
    tr = _row_tile(R, Cc, 1 << 20)

    def body(w_ref, o_ref):
        o_ref[...] = w_ref[...].astype(BF16)

    spec = pl.BlockSpec((None, tr, Cc), lambda l, i: (l, i, 0))
    return pl.pallas_call(body, name=name, grid=(Dp, R // tr), in_specs=[spec], out_specs=spec,
                          out_shape=jax.ShapeDtypeStruct(w.shape, BF16), compiler_params=_cp(("parallel", "parallel")))(w)


def _sum_adam(name, recv, w, m, v):
    Dp, R, Cc = w.shape
    tr = _row_tile(R, Cc, 1 << 18)

    def body(r_ref, w_ref, m_ref, v_ref, g_ref, d_ref, mo_ref, vo_ref):
        g = r_ref[0].astype(F32)
        for s in range(1, N_DEV):
            g = g + r_ref[s].astype(F32)
        dl, m2, v2 = _adam_vals(w_ref[...], g, m_ref[...], v_ref[...])
        g_ref[...] = g
        d_ref[...] = dl
        mo_ref[...] = m2
        vo_ref[...] = v2

    spec = pl.BlockSpec((None, tr, Cc), lambda l, i: (l, i, 0))
    rspec = pl.BlockSpec((N_DEV, None, tr, Cc), lambda l, i: (0, l, i, 0))
    return pl.pallas_call(body, name=name, grid=(Dp, R // tr), in_specs=[rspec, spec, spec, spec], out_specs=[spec] * 4,
                          out_shape=[jax.ShapeDtypeStruct(w.shape, F32)] * 4, compiler_params=_cp(("parallel", "parallel")))(recv, w, m, v)


def _adam(name, w, g, m, v):
    R, Cc = w.shape
    tr = _row_tile(R, Cc, 1 << 18)

    def body(w_ref, g_ref, m_ref, v_ref, d_ref, mo_ref, vo_ref):
        dl, m2, v2 = _adam_vals(w_ref[...], g_ref[...], m_ref[...], v_ref[...])
        d_ref[...] = dl
        mo_ref[...] = m2
        vo_ref[...] = v2

    spec = pl.BlockSpec((tr, Cc), lambda i: (i, 0))
    return pl.pallas_call(body, name=name, grid=(R // tr,), in_specs=[spec] * 4, out_specs=[spec] * 3,
                          out_shape=[jax.ShapeDtypeStruct(w.shape, F32)] * 3, compiler_params=_cp(("parallel",)))(w, g, m, v)


def _sum8(name, a):
    n, R, Cc = a.shape

    def body(a_ref, o_ref):
        s = a_ref[0]
        for k in range(1, n):
            s = s + a_ref[k]
        o_ref[...] = s

    return pl.pallas_call(body, name=name, in_specs=[pl.BlockSpec(memory_space=pltpu.VMEM)],
                          out_specs=pl.BlockSpec(memory_space=pltpu.VMEM), out_shape=jax.ShapeDtypeStruct((R, Cc), F32),
                          compiler_params=pltpu.CompilerParams(vmem_limit_bytes=VMEM_LIMIT))(a)


def _ada_fwd(cf, c9, ada_w):
    D = cf.D
    NS = ada_w.shape[-1]
    tk = _tile(D, 512)
    nk = D // tk

    def body(c_ref, w_ref, o_ref):
        kk = pl.program_id(1)
        s = _silu(c_ref[...]).astype(BF16)
        part = jnp.dot(s, w_ref[...].astype(BF16), preferred_element_type=F32)

        @pl.when(kk == 0)
        def _():
            o_ref[...] = part

        @pl.when(kk != 0)
        def _():
            o_ref[...] += part

    return pl.pallas_call(
        body, name="ada_fwd", grid=(DEPTH, nk),
        in_specs=[pl.BlockSpec((16, tk), lambda l, k: (0, k)), pl.BlockSpec((None, tk, NS), lambda l, k: (l, k, 0))],
        out_specs=pl.BlockSpec((None, 16, NS), lambda l, k: (l, 0, 0)),
        out_shape=jax.ShapeDtypeStruct((DEPTH, 16, NS), F32), compiler_params=_cp(("parallel", "arbitrary")))(c9, ada_w)


def _ada_bwd(cf, c9, ada_w, dm9):
    D = cf.D
    NS = ada_w.shape[-1]
    tk = _tile(D, 512)
    nk = D // tk

    def body(c_ref, w_ref, dm_ref, gw_ref, ds_ref):
        cv = c_ref[...]
        sg = _sigmoid(cv)
        dmb = dm_ref[...].astype(BF16)
        gw_ref[...] = lax.dot_general((cv * sg).astype(BF16), dmb, (_DN['tn'], ((), ())), preferred_element_type=F32)
        ds = lax.dot_general(dmb, w_ref[...].astype(BF16), (_DN['nt'], ((), ())), preferred_element_type=F32)
        ds_ref[...] = ds * (sg * (1.0 + cv * (1.0 - sg)))

    return pl.pallas_call(
        body, name="ada_bwd", grid=(DEPTH, nk),
        in_specs=[pl.BlockSpec((16, tk), lambda l, k: (0, k)), pl.BlockSpec((None, tk, NS), lambda l, k: (l, k, 0)),
                  pl.BlockSpec((None, 16, NS), lambda l, k: (l, 0, 0))],
        out_specs=[pl.BlockSpec((None, tk, NS), lambda l, k: (l, k, 0)), pl.BlockSpec((None, 16, tk), lambda l, k: (l, 0, k))],
        out_shape=[jax.ShapeDtypeStruct((DEPTH, D, NS), F32), jax.ShapeDtypeStruct((DEPTH, 16, D), F32)],
        compiler_params=_cp(("parallel", "parallel")))(c9, ada_w, dm9)


def _rope_tables(cf):
    L, LC = cf.L, cf.LC
    rows = L // GRID_W
    row = jnp.repeat(jnp.arange(rows, dtype=F32), GRID_W)
    col = jnp.tile(jnp.arange(GRID_W, dtype=F32), rows)
    nf = HEAD_DIM // 4
    inv = ROPE_THETA ** (-jnp.arange(nf, dtype=F32) / nf)
    ang = jnp.concatenate([row[:, None] * inv, col[:, None] * inv], axis=-1)
    cos, sin = jnp.cos(ang), jnp.sin(ang)
    cs = jnp.concatenate([jnp.ones((LC, HEAD_DIM), F32), jnp.concatenate([cos, cos], -1)], 0)
    sn = jnp.concatenate([jnp.zeros((LC, HEAD_DIM), F32), jnp.concatenate([-sin, sin], -1)], 0)
    return cs, sn


def _prep_tiles(cf, z, cs, sn, key):
    b = cf.OFF[key] // LANE
    return [(z, LANE, _col(b), True), (cs, LANE, _c0, False), (sn, LANE, _c0, False)]


_PREP = {'aq': _f_prep_norm, 'ak': _f_prep_norm, 'rq': _f_prep_plain, 'rk': _f_prep_scaled}


def _prep_fwd(cf, z, cs, sn, key, g):
    nh = cf.W[key] // LANE
    params = [(g, 'shared', True)] if g is not None else []
    return _row_fwd(cf, "prep_fwd_" + key, _PREP[key], _prep_tiles(cf, z, cs, sn, key), params,
                    [(LANE, _col(0), cf.W[key], BF16)], cf.TQ, nrep=nh)[0]


def _prep_bwd(cf, z, cs, sn, key, g, dt):
    nh = cf.W[key] // LANE
    params = [(g, 'shared', True)] if g is not None else []
    tg, pg = _row_bwd(cf, "prep_bwd_" + key, _PREP[key], _prep_tiles(cf, z, cs, sn, key), params,
                      [(dt, LANE, _col(0))], [(LANE, _col(0), cf.W[key], BF16)], cf.TQ, nrep=nh)
    return tg[0], (pg[0] if g is not None else None)


def _gate_params(cf, gup, gb):
    K = gup.shape[-1]
    gf = jnp.zeros((LANE, K), F32).at[0:GLA_RANK].set(gup[0])
    gbm = jnp.zeros((LANE, K), F32).at[GLA_RANK:2 * GLA_RANK].set(gup[1])
    return [(gf, 'shared', True), (gbm, 'shared', True), (gb[0:1], 'shared', True), (gb[1:2], 'shared', True)]


def _mix_tiles(cf, z, o, key):
    return [(o, LANE, _col(0), True), (z, LANE, _col(cf.OFF[key] // LANE), True)]


def _mid_io(cf, l, W, mod, x, y):
    tiles = [(x, cf.D, _c0, True), (y, cf.D, _c0, True)]
    params = [(mod[2], 'stream', True), (W['norm2_g'][l], 'shared', True), (mod[3], 'stream', True), (mod[4], 'stream', True)]
    return tiles, params


def _layer_fwd(cf, l, W, mod, x, h, cs, sn):
    T, D, Fd = cf.T, cf.D, cf.F
    z = _mm("z_%d" % l, h, W['w_in'][l], 'nn', T, cf.NZ, D, F32)
    qa = _prep_fwd(cf, z, cs, sn, 'aq', W['q_norm_g'][l])
    ka = _prep_fwd(cf, z, cs, sn, 'ak', W['k_norm_g'][l])
    qr = _prep_fwd(cf, z, cs, sn, 'rq', None)
    kr = _prep_fwd(cf, z, cs, sn, 'rk', None)
    o_att = _att_fwd(cf, qa, ka, z)
    o_ret = _ret_fwd(cf, qr, kr, z, W['ret_log_decay'][l])
    gates = _gate_params(cf, W['gla_gate_up'][l], W['gla_gate_b'][l])
    ga_tile = [(z, LANE, _col(cf.OFF['ga'] // LANE), True)]
    K = cf.HG * 64
    laf, lab = _row_fwd(cf, "gates_fwd_%d" % l, _f_gates, ga_tile, gates, [(K, _c0, K, F32), (K, _c0, K, F32)], cf.TQ)
    o_gla, sst = _gla_fwd(cf, z, laf, lab)
    cat_r = _row_fwd(cf, "mixr_fwd_%d" % l, _f_gated_norm, _mix_tiles(cf, z, o_ret, 'rg'), [(W['ret_norm_g'][l], 'shared', True)],
                     [(LANE, _col(0), cf.HR * LANE, BF16)], cf.TQ, nrep=cf.HR)[0]
    cat_g = _row_fwd(cf, "mixg_fwd_%d" % l, _f_gated_norm, _mix_tiles(cf, z, o_gla, 'gr'), [(W['gla_norm_g'][l], 'shared', True)],
                     [(LANE, _col(0), cf.HG * LANE, BF16)], cf.TQ, nrep=cf.HG)[0]
    cat = jnp.concatenate([o_att, cat_r, cat_g], axis=-1)
    y = _mm("y_%d" % l, cat, W['w_out'][l], 'nn', T, D, D, F32)
    tiles, params = _mid_io(cf, l, W, mod, x, y)
    x1, h2 = _row_fwd(cf, "mid_fwd_%d" % l, _f_resid_norm_mod, tiles, params, [(D, _c0, D, F32), (D, _c0, D, BF16)], cf.TM)
    u = _mm("u_%d" % l, h2, W['w_up'][l], 'nn', T, 2 * Fd, D, F32)
    t = _conv_fwd(cf, u, W['conv_wb'][l])
    yff = _mm("yff_%d" % l, t, W['w_down'][l], 'nn', T, D, Fd, F32)
    return dict(x=x, h=h, z=z, qa=qa, ka=ka, qr=qr, kr=kr, laf=laf, lab=lab, sst=sst, o_ret=o_ret, o_gla=o_gla, cat=cat, y=y,
                x1=x1, h2=h2, u=u, t=t, yff=yff, gates=gates)


def _layer_bwd(cf, l, W, mod, sv, dx1, dyff, cs, sn):
    T, D, Fd = cf.T, cf.D, cf.F
    g = {}
    g['w_down'] = _mm("gwd_%d" % l, sv['t'], dyff, 'tn', Fd, D, T, BF16, tk=T)
    dt = _mm("dt_%d" % l, dyff, W['w_down'][l], 'nt', T, Fd, D, BF16)
    da, dv, g['conv_wb'] = _conv_bwd(cf, sv['u'], W['conv_wb'][l], dt)
    du = jnp.concatenate([da, dv], axis=-1)
    cu = 2 * Fd // N_DEV
    g['w_up'] = _mm("gwu_%d" % l, sv['h2'], du, 'tn', D, 2 * Fd, T, BF16, tn=cu, tk=T, out_shape=(N_DEV, D, cu),
                    out_spec=pl.BlockSpec((None, _tile(D, 768, 128), cu), lambda i, j, k: (j, i, 0)))
    dh2 = _mm("dh2_%d" % l, du, W['w_up'][l], 'nt', T, D, 2 * Fd, BF16)
    tiles, params = _mid_io(cf, l, W, mod, sv['x'], sv['y'])
    (dx, dy), (g['m2'], g['norm2_g'], g['m3'], g['m4']) = _row_bwd(
        cf, "mid_bwd_%d" % l, _f_resid_norm_mod, tiles, params, [(dx1, D, _c0), (dh2, D, _c0)],
        [(D, _c0, D, F32), (D, _c0, D, BF16)], cf.TM)
    g['w_out'] = _mm("gwo_%d" % l, sv['cat'], dy, 'tn', D, D, T, BF16, tk=T)
    dcat = _mm("dcat_%d" % l, dy, W['w_out'][l], 'nt', T, D, D, BF16)
    z = sv['z']
    (do_ret, drg), (g['ret_norm_g'],) = _row_bwd(
        cf, "mixr_bwd_%d" % l, _f_gated_norm, _mix_tiles(cf, z, sv['o_ret'], 'rg'), [(W['ret_norm_g'][l], 'shared', True)],
        [(dcat, LANE, _col(cf.HQ))], [(LANE, _col(0), cf.HR * LANE, F32), (LANE, _col(0), cf.HR * LANE, BF16)], cf.TQ, nrep=cf.HR)
    (do_gla, dgr), (g['gla_norm_g'],) = _row_bwd(
        cf, "mixg_bwd_%d" % l, _f_gated_norm, _mix_tiles(cf, z, sv['o_gla'], 'gr'), [(W['gla_norm_g'][l], 'shared', True)],
        [(dcat, LANE, _col(cf.HQ + cf.HR))], [(LANE, _col(0), cf.HG * LANE, F32), (LANE, _col(0), cf.HG * LANE, BF16)], cf.TQ, nrep=cf.HG)
    dqa, dka, dav = _att_bwd(cf, sv['qa'], sv['ka'], z, dcat)
    dqr, dkr, drv, dlg = _ret_bwd(cf, sv['qr'], sv['kr'], z, W['ret_log_decay'][l], do_ret)
    g['ret_log_decay'] = dlg[:, 0:2, 0].T
    dgq, dgk, dgv, dlaf, dlab = _gla_bwd(cf, z, sv['laf'], sv['lab'], sv['sst'], do_gla)
    K = cf.HG * 64
    ga_tile = [(z, LANE, _col(cf.OFF['ga'] // LANE), True)]
    (dga,), (ggf, ggb, gbf, gbb) = _row_bwd(cf, "gates_bwd_%d" % l, _f_gates, ga_tile, sv['gates'],
                                             [(dlaf, K, _c0), (dlab, K, _c0)], [(LANE, _c0, LANE, BF16)], cf.TQ)
    g['gla_gate_up'] = jnp.stack([ggf[0:GLA_RANK], ggb[GLA_RANK:2 * GLA_RANK]])
    g['gla_gate_b'] = jnp.concatenate([gbf, gbb], axis=0)
    daq, g['q_norm_g'] = _prep_bwd(cf, z, cs, sn, 'aq', W['q_norm_g'][l], dqa)
    dak, g['k_norm_g'] = _prep_bwd(cf, z, cs, sn, 'ak', W['k_norm_g'][l], dka)
    drq, _ = _prep_bwd(cf, z, cs, sn, 'rq', None, dqr)
    drk, _ = _prep_bwd(cf, z, cs, sn, 'rk', None, dkr)
    pad = jnp.zeros((T, cf.NZ - cf.OFF['ga'] - LANE), BF16)
    dz = jnp.concatenate([daq, dak, dav.astype(BF16), drq, drk, drv.astype(BF16), drg, dgq.astype(BF16), dgk.astype(BF16),
                          dgv.astype(BF16), dgr, dga, pad], axis=-1)
    g['w_in'] = _mm("gwi_%d" % l, sv['h'], dz, 'tn', D, cf.NZ, T, BF16, tk=T)
    dh = _mm("dh_%d" % l, dz, W['w_in'][l], 'nt', T, D, cf.NZ, BF16)
    return dx, dh, g


_WEIGHTS = ['c_ctx', 'ada_w', 'ada_b', 'norm1_g', 'w_in', 'q_norm_g', 'k_norm_g', 'ret_log_decay', 'ret_norm_g',
            'gla_gate_up', 'gla_gate_b', 'gla_norm_g', 'w_out', 'norm2_g', 'w_up', 'conv_w', 'conv_b', 'w_down', 'final_norm_g']
_BIG = ['w_in', 'w_out', 'w_up', 'w_down']
_SMALL = [n for n in _WEIGHTS if n not in _BIG and n != 'ada_w']
_COL_SHARDED = ['gla_gate_up', 'gla_gate_b', 'conv_w']


def _pack(arrs):
    flat = jnp.concatenate([a.reshape(-1) for a in arrs])
    n = flat.shape[0]
    tot = -(-n // (8 * LANE)) * (8 * LANE)
    return jnp.pad(flat, (0, tot - n)).reshape(tot // LANE, LANE)


def _unpack(flat, shapes):
    out, o = [], 0
    for s in shapes:
        n = int(np.prod(s))
        out.append(flat[..., o:o + n].reshape(flat.shape[:-1] + tuple(s)))
        o += n
    return out


def _unshard_last(a):
    return jnp.moveaxis(a, 0, -2).reshape(a.shape[1:-1] + (N_DEV * a.shape[-1],))


def _step(cf, x, c, ctx, loss_target, w, m, v):
    T, D, Fd, L, LC = cf.T, cf.D, cf.F, cf.L, cf.LC
    _, _, _, me = _me()
    NS = w['ada_w'].shape[-1]

    c_all = _ag_small("ag_c", jnp.pad(c, ((0, 7), (0, 0))))[:, 0, :]
    c9 = jnp.concatenate([c_all, w['c_ctx'][None], jnp.zeros((7, D), F32)], axis=0)
    pm = _ada_fwd(cf, c9, w['ada_w'])
    pm_all = _ag_small("ag_mod", pm.reshape(DEPTH * 16, NS)).reshape(N_DEV, DEPTH, 16, NS)
    mod_all = _unshard_last(pm_all) + w['ada_b'][:, None, :]
    mod_own = lax.dynamic_index_in_dim(mod_all, me, axis=1, keepdims=False)
    mods = []
    for l in range(DEPTH):
        mods.append([jnp.stack([mod_all[l, 8, k * D:(k + 1) * D], mod_own[l, k * D:(k + 1) * D]])[:, None, :] for k in range(N_MOD)])

    shard_shapes = [w[n].shape for n in _COL_SHARDED]
    got = _ag_small("ag_smallw", _pack([w[n] for n in _COL_SHARDED]))
    full = dict(zip(_COL_SHARDED, [_unshard_last(a) for a in _unpack(got.reshape(N_DEV, -1), shard_shapes)]))

    gi, w_out_f, w_up_f, w_down_f = _ag_weights(cf, *[_cast_bf16("cast_" + n, w[n]) for n in _BIG])
    w_in_f = jnp.pad(_unshard_last(gi), ((0, 0), (0, 0), (0, cf.NZ - cf.NIN)))
    conv_wb = jnp.concatenate([full['conv_w'], w['conv_b'][:, None, :], jnp.zeros((DEPTH, 4, Fd), F32)], axis=1)
    W = dict(w_in=w_in_f, w_out=w_out_f, w_up=w_up_f, w_down=w_down_f, conv_wb=conv_wb,
             gla_gate_up=full['gla_gate_up'], gla_gate_b=full['gla_gate_b'], ret_log_decay=w['ret_log_decay'])
    for n in ['q_norm_g', 'k_norm_g', 'ret_norm_g', 'gla_norm_g', 'norm1_g', 'norm2_g']:
        W[n] = w[n][:, None, :]

    cs, sn = _rope_tables(cf)
    x0 = jnp.concatenate([ctx[0], x[0]], axis=0)
    pre_tiles = [(x0, D, _c0, True)]
    pre_params = [(W['norm1_g'][0], 'shared', True), (mods[0][0], 'stream', True), (mods[0][1], 'stream', True)]
    h0 = _row_fwd(cf, "pre_fwd", _f_norm_mod, pre_tiles, pre_params, [(D, _c0, D, BF16)], cf.TM)[0]
    sv0 = _layer_fwd(cf, 0, W, mods[0], x0, h0, cs, sn)
    tr_tiles = [(sv0['x1'], D, _c0, True), (sv0['yff'], D, _c0, True)]
    tr_params = [(mods[0][5], 'stream', True), (W['norm1_g'][1], 'shared', True), (mods[1][0], 'stream', True), (mods[1][1], 'stream', True)]
    xb, hb = _row_fwd(cf, "tr_fwd", _f_resid_norm_mod, tr_tiles, tr_params, [(D, _c0, D, F32), (D, _c0, D, BF16)], cf.TM)
    sv1 = _layer_fwd(cf, 1, W, mods[1], xb, hb, cs, sn)
    tgt = jnp.concatenate([jnp.zeros((LC, D), F32), loss_target[0]], axis=0)
    dx1, dyff, dm5_1, g_final, ls = _loss_grad(cf, sv1['x1'], sv1['yff'], mods[1][5], w['final_norm_g'][None], tgt)
    loss = lax.psum(ls[0, 0], ("x", "y", "c"))

    dxb, dhb, g1 = _layer_bwd(cf, 1, W, mods[1], sv1, dx1, dyff, cs, sn)
    (dx1_0, dyff_0), (dm5_0, gn1_1, dm0_1, dm1_1) = _row_bwd(
        cf, "tr_bwd", _f_resid_norm_mod, tr_tiles, tr_params, [(dxb, D, _c0), (dhb, D, _c0)],
        [(D, _c0, D, F32), (D, _c0, D, BF16)], cf.TM)
    dx0, dh0, g0 = _layer_bwd(cf, 0, W, mods[0], sv0, dx1_0, dyff_0, cs, sn)
    (dxa,), (gn1_0, dm0_0, dm1_0) = _row_bwd(cf, "pre_bwd", _f_first, pre_tiles, pre_params, [(dx0, D, _c0), (dh0, D, _c0)],
                                            [(D, _c0, D, F32)], cf.TM)
    grad_x = dxa[LC:][None]

    dmod = jnp.stack([jnp.concatenate([dm0_0, dm1_0, g0['m2'], g0['m3'], g0['m4'], dm5_0], axis=-1)[:, 0],
                      jnp.concatenate([dm0_1, dm1_1, g1['m2'], g1['m3'], g1['m4'], dm5_1], axis=-1)[:, 0]])
    dm_all = _ag_small("ag_dmod", jnp.pad(dmod.reshape(2 * DEPTH, N_MOD * D), ((0, 8 - 2 * DEPTH), (0, 0))))
    dm_all = dm_all[:, :2 * DEPTH].reshape(N_DEV, DEPTH, 2, N_MOD * D)
    dctx = _sum8("sum_dmodc", jnp.pad(dm_all[:, :, 0], ((0, 0), (0, 8 - DEPTH), (0, 0))))[:DEPTH]
    dm9 = jnp.concatenate([jnp.moveaxis(dm_all[:, :, 1], 0, 1), dctx[:, None]], axis=1)
    g_ada_b = _sum8("sum_adab", jnp.pad(jnp.moveaxis(dm9, 1, 0), ((0, 0), (0, 8 - DEPTH), (0, 0))))[:DEPTH]
    dm9s = lax.dynamic_slice_in_dim(jnp.pad(dm9, ((0, 0), (0, 7), (0, 0))), me * NS, NS, axis=2)
    g_ada_w, dsil = _ada_bwd(cf, c9, w['ada_w'], dm9s)
    g_cctx_part = dsil[0, 8]
    for l in range(1, DEPTH):
        g_cctx_part = g_cctx_part + dsil[l, 8]

    def both(key):
        return jnp.stack([g0[key], g1[key]])

    gsmall = dict(c_ctx=g_cctx_part, norm1_g=jnp.stack([gn1_0[0], gn1_1[0]]), q_norm_g=both('q_norm_g')[:, 0],
                  k_norm_g=both('k_norm_g')[:, 0], ret_log_decay=both('ret_log_decay'), ret_norm_g=both('ret_norm_g')[:, 0],
                  gla_gate_up=both('gla_gate_up'), gla_gate_b=both('gla_gate_b'), gla_norm_g=both('gla_norm_g')[:, 0],
                  norm2_g=both('norm2_g')[:, 0], conv_w=both('conv_wb')[:, 0:3], conv_b=both('conv_wb')[:, 3], final_norm_g=g_final[0])
    snames = [n for n in _SMALL if n != 'ada_b']
    sshapes = [gsmall[n].shape for n in snames]
    gs_all = _ag_small("ag_gsmall", _pack([gsmall[n] for n in snames]))
    gs = dict(zip(snames, _unpack(_sum8("sum_gsmall", gs_all).reshape(-1), sshapes)))
    gs['ada_b'] = g_ada_b
    for n in _COL_SHARDED:
        ns_ = w[n].shape[-1]
        gs[n] = lax.dynamic_slice_in_dim(gs[n], me * ns_, ns_, axis=gs[n].ndim - 1)

    def slabs_in(gw):
        return jnp.moveaxis(gw[:, :cf.NIN].reshape(D, N_DEV, cf.NINS), 1, 0)

    recv = _rs_grads(cf, [slabs_in(g0['w_in']), slabs_in(g1['w_in'])], [g0['w_out'], g1['w_out']],
                     [g0['w_up'], g1['w_up']], [g0['w_down'], g1['w_down']])
    out_g, out_d, out_m, out_v = {}, {}, {}, {}
    for n, r in zip(_BIG, recv):
        out_g[n], out_d[n], out_m[n], out_v[n] = _sum_adam("adam_" + n, r, w[n], m[n], v[n])
    aw = [a.reshape(DEPTH * D, NS) for a in (w['ada_w'], g_ada_w, m['ada_w'], v['ada_w'])]
    out_g['ada_w'] = g_ada_w
    out_d['ada_w'], out_m['ada_w'], out_v['ada_w'] = [a.reshape(DEPTH, D, NS) for a in _adam("adam_ada_w", *aw)]
    shp = [w[n].shape for n in _SMALL]
    packed = [_pack([src[n] for n in _SMALL]) for src in (w, gs, m, v)]
    res = _adam("adam_small", *packed)
    for dst, pk in zip((out_d, out_m, out_v), res):
        dst.update(zip(_SMALL, _unpack(pk.reshape(-1), shp)))
    out_g.update({n: gs[n] for n in _SMALL})
    return (loss, grad_x, *[out_g[n] for n in _WEIGHTS], *[out_d[n] for n in _WEIGHTS], *[out_m[n] for n in _WEIGHTS],
            *[out_v[n] for n in _WEIGHTS])


def kernel(x, c, ctx, c_ctx, ada_w, ada_b, norm1_g, w_in, q_norm_g, k_norm_g, ret_log_decay, ret_norm_g, gla_gate_up, gla_gate_b, gla_norm_g, w_out, norm2_g, w_up, conv_w, conv_b, w_down, final_norm_g, loss_target, m_c_ctx, m_ada_w, m_ada_b, m_norm1_g, m_w_in, m_q_norm_g, m_k_norm_g, m_ret_log_decay, m_ret_norm_g, m_gla_gate_up, m_gla_gate_b, m_gla_norm_g, m_w_out, m_norm2_g, m_w_up, m_conv_w, m_conv_b, m_w_down, m_final_norm_g, v_c_ctx, v_ada_w, v_ada_b, v_norm1_g, v_w_in, v_q_norm_g, v_k_norm_g, v_ret_log_decay, v_ret_norm_g, v_gla_gate_up, v_gla_gate_b, v_gla_norm_g, v_w_out, v_norm2_g, v_w_up, v_conv_w, v_conv_b, v_w_down, v_final_norm_g):
    w = dict(c_ctx=c_ctx, ada_w=ada_w, ada_b=ada_b, norm1_g=norm1_g, w_in=w_in, q_norm_g=q_norm_g, k_norm_g=k_norm_g,
             ret_log_decay=ret_log_decay, ret_norm_g=ret_norm_g, gla_gate_up=gla_gate_up, gla_gate_b=gla_gate_b,
             gla_norm_g=gla_norm_g, w_out=w_out, norm2_g=norm2_g, w_up=w_up, conv_w=conv_w, conv_b=conv_b, w_down=w_down,
             final_norm_g=final_norm_g)
    m = dict(c_ctx=m_c_ctx, ada_w=m_ada_w, ada_b=m_ada_b, norm1_g=m_norm1_g, w_in=m_w_in, q_norm_g=m_q_norm_g,
             k_norm_g=m_k_norm_g, ret_log_decay=m_ret_log_decay, ret_norm_g=m_ret_norm_g, gla_gate_up=m_gla_gate_up,
             gla_gate_b=m_gla_gate_b, gla_norm_g=m_gla_norm_g, w_out=m_w_out, norm2_g=m_norm2_g, w_up=m_w_up,
             conv_w=m_conv_w, conv_b=m_conv_b, w_down=m_w_down, final_norm_g=m_final_norm_g)
    v = dict(c_ctx=v_c_ctx, ada_w=v_ada_w, ada_b=v_ada_b, norm1_g=v_norm1_g, w_in=v_w_in, q_norm_g=v_q_norm_g,
             k_norm_g=v_k_norm_g, ret_log_decay=v_ret_log_decay, ret_norm_g=v_ret_norm_g, gla_gate_up=v_gla_gate_up,
             gla_gate_b=v_gla_gate_b, gla_norm_g=v_gla_norm_g, w_out=v_w_out, norm2_g=v_norm2_g, w_up=v_w_up,
             conv_w=v_conv_w, conv_b=v_conv_b, w_down=v_w_down, final_norm_g=v_final_norm_g)
    return _step(_cfg(), x, c, ctx, loss_target, w, m, v)
```

```python
import functools
import math
import types

import jax
import jax.numpy as jnp
import numpy as np
from jax import lax
from jax.experimental import pallas as pl
from jax.experimental.pallas import tpu as pltpu

F32 = jnp.float32
BF16 = jnp.bfloat16
HI = lax.Precision.HIGHEST

D_MODEL = 2048
SEQ = 2048
CTX_LEN = 256
GRID_W = 64
D_FF = 5632
DEPTH = 2
N_DEV = 8
HEAD_DIM = 128
ROPE_THETA = 10000.0
GLA_TAU = 16.0
GLA_RANK = 16
GLA_CHUNK = 64
GLA_SUB = 16
EPS = 1e-6
N_MOD = 6
ADAM_LR = 0.001
ADAM_B1 = 0.9
ADAM_B2 = 0.999
ADAM_EPS = 1e-08
ADAM_WD = 0.01
ADAM_STEP = 10
LANE = 128
VMEM_LIMIT = 56 * 1024 * 1024
NEG = -1e30


def _cfg():
    d = types.SimpleNamespace()
    d.D, d.L, d.LC, d.F = D_MODEL, SEQ, CTX_LEN, D_FF
    d.T = d.L + d.LC
    nm = d.D // HEAD_DIM
    d.HQ, d.HKV, d.HR, d.HG = nm // 2, nm // 8, nm // 4, nm // 4
    d.G = d.HQ // d.HKV
    w = dict(aq=d.HQ * 128, ak=d.HKV * 128, av=d.HKV * 128, rq=d.HR * 128, rk=d.HR * 128, rv=d.HR * 128,
             rg=d.HR * 128, gq=d.HG * 64, gk=d.HG * 64, gv=d.HG * 128, gr=d.HG * 128, ga=2 * GLA_RANK)
    off, o = {}, 0
    for k, v in w.items():
        off[k] = o
        o += v
    d.W, d.OFF, d.NIN = w, off, o
    d.NZ = -(-(off['ga'] + LANE) // 256) * 256
    d.NINS = d.NIN // N_DEV
    d.TM = math.gcd(d.LC, 128)
    d.TQ = math.gcd(d.LC, 256)
    return d


def _cp(sem=None):
    return pltpu.CompilerParams(dimension_semantics=sem, vmem_limit_bytes=VMEM_LIMIT)


def _tile(n, target, mult=LANE):
    t = min(n, target)
    t -= t % mult
    while t > mult and n % t:
        t -= mult
    return t if t > 0 and n % t == 0 else n


_DN = {'nn': ((1,), (0,)), 'nt': ((1,), (1,)), 'tn': ((0,), (0,))}


def _mm(name, a, b, kind, M, N, K, out_dtype, tm=768, tn=768, tk=1024, a_spec=None, b_spec=None,
        out_shape=None, out_spec=None):
    tm, tn = _tile(M, tm, 128), _tile(N, tn, 128)
    tk = _tile(K, tk, 128)
    nk = K // tk

    def body(a_ref, b_ref, o_ref, acc):
        kk = pl.program_id(2)

        @pl.when(kk == 0)
        def _():
            acc[...] = jnp.zeros_like(acc)

        acc[...] += lax.dot_general(a_ref[...].astype(BF16), b_ref[...].astype(BF16), (_DN[kind], ((), ())),
                                    preferred_element_type=F32)

        @pl.when(kk == nk - 1)
        def _():
            o_ref[...] = acc[...].astype(o_ref.dtype)

    if a_spec is None:
        a_spec = pl.BlockSpec((tk, tm), lambda i, j, k: (k, i)) if kind == 'tn' else pl.BlockSpec((tm, tk), lambda i, j, k: (i, k))
    if b_spec is None:
        b_spec = pl.BlockSpec((tn, tk), lambda i, j, k: (j, k)) if kind == 'nt' else pl.BlockSpec((tk, tn), lambda i, j, k: (k, j))
    if out_spec is None:
        out_spec = pl.BlockSpec((tm, tn), lambda i, j, k: (i, j))
        out_shape = (M, N)
    return pl.pallas_call(
        body, name=name, grid=(M // tm, N // tn, nk), in_specs=[a_spec, b_spec], out_specs=out_spec,
        out_shape=jax.ShapeDtypeStruct(out_shape, out_dtype), scratch_shapes=[pltpu.VMEM((tm, tn), F32)],
        compiler_params=_cp(("parallel", "parallel", "arbitrary")))(a, b)


def _row_specs(cf, tm, tiles, params):
    nctx = cf.LC // tm
    specs = []
    for arr, w, colf, _ in tiles:
        specs.append(pl.BlockSpec((tm, w), functools.partial(lambda i, r, colf: (i, colf(r)), colf=colf)))
    for arr, kind, _ in params:
        nd = arr.ndim
        if kind == 'shared':
            specs.append(pl.BlockSpec(arr.shape, functools.partial(lambda i, r, nd: (0,) * nd, nd=nd)))
        else:
            specs.append(pl.BlockSpec((None,) + arr.shape[1:],
                                      functools.partial(lambda i, r, nd, nctx: (jnp.where(i >= nctx, 1, 0),) + (0,) * (nd - 1), nd=nd, nctx=nctx)))
    return specs


def _row_fwd(cf, name, f, tiles, params, outs, tm, nrep=1):
    nt, npar = len(tiles), len(params)

    def body(*refs):
        tv = [r[...].astype(F32) for r in refs[:nt]]
        pv = [r[...] for r in refs[nt:nt + npar]]
        res = f(*tv, *pv)
        for o, v in zip(refs[nt + npar:], res):
            o[...] = v.astype(o.dtype)

    out_specs = [pl.BlockSpec((tm, w), functools.partial(lambda i, r, colf: (i, colf(r)), colf=colf)) for w, colf, _, _ in outs]
    out_shape = [jax.ShapeDtypeStruct((cf.T, tw), dt) for _, _, tw, dt in outs]
    return pl.pallas_call(
        body, name=name, grid=(cf.T // tm, nrep), in_specs=_row_specs(cf, tm, tiles, params), out_specs=out_specs,
        out_shape=out_shape, compiler_params=_cp(("arbitrary", "arbitrary")))(*[t[0] for t in tiles], *[p[0] for p in params])


def _row_bwd(cf, name, f, tiles, params, cts, tgrads, tm, nrep=1):
    nt, npar, nc = len(tiles), len(params), len(cts)
    tdiff = [k for k, t in enumerate(tiles) if t[3]]
    pdiff = [k for k, p in enumerate(params) if p[2]]
    nctx = cf.LC // tm

    def body(*refs):
        i, r = pl.program_id(0), pl.program_id(1)
        tv = [x[...].astype(F32) for x in refs[:nt]]
        pv = [x[...] for x in refs[nt:nt + npar]]
        cv = tuple(x[...].astype(F32) for x in refs[nt + npar:nt + npar + nc])
        outs = refs[nt + npar + nc:]

        def g(*diff):
            tv2, pv2 = list(tv), list(pv)
            for k, v in zip(tdiff, diff[:len(tdiff)]):
                tv2[k] = v
            for k, v in zip(pdiff, diff[len(tdiff):]):
                pv2[k] = v
            return tuple(f(*tv2, *pv2))

        _, vjp_fn = jax.vjp(g, *[tv[k] for k in tdiff], *[pv[k] for k in pdiff])
        grads = vjp_fn(cv)
        for o, gv in zip(outs[:len(tdiff)], grads[:len(tdiff)]):
            o[...] = gv.astype(o.dtype)
        for n_, (o, gv) in enumerate(zip(outs[len(tdiff):], grads[len(tdiff):])):
            if params[pdiff[n_]][1] == 'shared':
                first = jnp.logical_and(i == 0, r == 0)
            else:
                first = jnp.logical_and(jnp.logical_or(i == 0, i == nctx), r == 0)

            @pl.when(first)
            def _():
                o[...] = gv

            @pl.when(jnp.logical_not(first))
            def _():
                o[...] += gv

    in_specs = _row_specs(cf, tm, tiles, params)
    in_specs += [pl.BlockSpec((tm, w), functools.partial(lambda i, r, colf: (i, colf(r)), colf=colf)) for _, w, colf in cts]
    out_specs = [pl.BlockSpec((tm, w), functools.partial(lambda i, r, colf: (i, colf(r)), colf=colf)) for w, colf, _, _ in tgrads]
    out_shape = [jax.ShapeDtypeStruct((cf.T, tw), dt) for _, _, tw, dt in tgrads]
    pspecs = _row_specs(cf, tm, [], [params[k] for k in pdiff])
    out_specs += pspecs
    out_shape += [jax.ShapeDtypeStruct(params[k][0].shape, F32) for k in pdiff]
    res = pl.pallas_call(
        body, name=name, grid=(cf.T // tm, nrep), in_specs=in_specs, out_specs=out_specs, out_shape=out_shape,
        compiler_params=_cp(("arbitrary", "arbitrary")))(*[t[0] for t in tiles], *[p[0] for p in params], *[c[0] for c in cts])
    return res[:len(tdiff)], res[len(tdiff):]


def _c0(r):
    return 0


def _col(base):
    return lambda r: base + r


def _rms(x, g):
    return x * lax.rsqrt(jnp.mean(x * x, axis=-1, keepdims=True) + EPS) * g


def _sigmoid(x):
    return 1.0 / (1.0 + jnp.exp(-x))


def _silu(x):
    return x * _sigmoid(x)


def _f_norm_mod(x, g, sh, sc):
    return (_rms(x, g) * (1 + sc) + sh,)


def _f_resid_norm_mod(x, y, gate, g, sh, sc):
    x1 = x + gate * y
    return (x1, _rms(x1, g) * (1 + sc) + sh)


@jax.custom_vjp
def _swap_halves(t):
    return pltpu.roll(t, HEAD_DIM // 2, axis=1)


def _swap_fwd(t):
    return _swap_halves(t), None


def _swap_bwd(_, g):
    return (pltpu.roll(g, HEAD_DIM // 2, axis=1),)


_swap_halves.defvjp(_swap_fwd, _swap_bwd)


def _rope(t, cs, sn):
    return t * cs + _swap_halves(t) * sn


def _f_prep_norm(t, cs, sn, g):
    return (_rope(_rms(t, g), cs, sn),)


def _f_prep_plain(t, cs, sn):
    return (_rope(t, cs, sn),)


def _f_prep_scaled(t, cs, sn):
    return (_rope(t * (HEAD_DIM ** -0.5), cs, sn),)


def _log_sigmoid(x):
    return jnp.minimum(x, 0.0) - jnp.log(1.0 + jnp.exp(-jnp.abs(x)))


def _f_gates(ga, gf, gb, bf, bb):
    gab = ga.astype(BF16)
    lf = jnp.dot(gab, gf.astype(BF16), preferred_element_type=F32) + bf
    lb = jnp.dot(gab, gb.astype(BF16), preferred_element_type=F32) + bb
    return (_log_sigmoid(lf) / GLA_TAU, _log_sigmoid(lb) / GLA_TAU)


def _f_gated_norm(o, g, n):
    return (_rms(o, n) * _silu(g),)


def _f_first(x, g, sh, sc):
    return (x, _rms(x, g) * (1 + sc) + sh)


def _loss_grad(cf, x1, yff, gate, gfin, tgt):
    tm, T, D = cf.TM, cf.T, cf.D
    nctx = cf.LC // tm

    def lossf(x1v, yv, gt, gf, tg):
        y = _rms(x1v + gt * yv, gf)
        e = y - tg
        return 0.5 * jnp.sum(jnp.mean(e * e, axis=-1, keepdims=True), axis=0, keepdims=True)

    def body(x1_ref, y_ref, gt_ref, gf_ref, tg_ref, dx_ref, dy_ref, dgt_ref, dgf_ref, ls_ref):
        i = pl.program_id(0)
        lat = (i >= nctx).astype(F32)
        val, vjp_fn = jax.vjp(lossf, x1_ref[...], y_ref[...].astype(F32), gt_ref[...], gf_ref[...], tg_ref[...])
        dx, dy, dgt, dgf, _ = vjp_fn(jnp.ones((1, 1), F32) * lat)
        dx_ref[...] = dx
        dy_ref[...] = dy.astype(dy_ref.dtype)
        first_s = jnp.logical_or(i == 0, i == nctx)

        @pl.when(first_s)
        def _():
            dgt_ref[...] = dgt

        @pl.when(jnp.logical_not(first_s))
        def _():
            dgt_ref[...] += dgt

        @pl.when(i == 0)
        def _():
            dgf_ref[...] = dgf
            ls_ref[...] = jnp.zeros_like(ls_ref) + val * lat

        @pl.when(i != 0)
        def _():
            dgf_ref[...] += dgf
            ls_ref[...] += val * lat

    row = pl.BlockSpec((tm, D), lambda i: (i, 0))
    strm = pl.BlockSpec((None, 1, D), lambda i: (jnp.where(i >= nctx, 1, 0), 0, 0))
    one = pl.BlockSpec((1, D), lambda i: (0, 0))
    return pl.pallas_call(
        body, name="loss_grad", grid=(T // tm,), in_specs=[row, row, strm, one, row],
        out_specs=[row, row, strm, one, pl.BlockSpec((8, LANE), lambda i: (0, 0))],
        out_shape=[jax.ShapeDtypeStruct((T, D), F32), jax.ShapeDtypeStruct((T, D), BF16),
                   jax.ShapeDtypeStruct((2, 1, D), F32), jax.ShapeDtypeStruct((1, D), F32),
                   jax.ShapeDtypeStruct((8, LANE), F32)],
        compiler_params=_cp(("arbitrary",)))(x1, yff, gate, gfin, tgt)


def _att_mask(cf, i, tq):
    col = lax.broadcasted_iota(jnp.int32, (tq, cf.T), 1)
    return jnp.logical_or(col < cf.LC, i >= cf.LC // tq)


def _att_probs(q, k, mask):
    s = lax.dot_general(q, k, (_DN['nt'], ((), ())), preferred_element_type=F32) * (HEAD_DIM ** -0.5)
    s = jnp.where(mask, s, NEG)
    e = jnp.exp(s - jnp.max(s, axis=-1, keepdims=True))
    return e / jnp.sum(e, axis=-1, keepdims=True)


def _att_fwd(cf, q, k, z):
    tq, T, G = cf.TQ, cf.T, cf.G
    vb = cf.OFF['av'] // LANE

    def body(q_ref, k_ref, v_ref, o_ref):
        mask = _att_mask(cf, pl.program_id(1), tq)
        kv, vv = k_ref[...], v_ref[...].astype(BF16)
        for j in range(G):
            p = _att_probs(q_ref[:, j * LANE:(j + 1) * LANE], kv, mask)
            o_ref[:, j * LANE:(j + 1) * LANE] = jnp.dot(p.astype(BF16), vv, preferred_element_type=F32).astype(o_ref.dtype)

    return pl.pallas_call(
        body, name="att_fwd", grid=(cf.HKV, T // tq),
        in_specs=[pl.BlockSpec((tq, G * LANE), lambda g, i: (i, g)), pl.BlockSpec((T, LANE), lambda g, i: (0, g)),
                  pl.BlockSpec((T, LANE), lambda g, i: (0, vb + g))],
        out_specs=pl.BlockSpec((tq, G * LANE), lambda g, i: (i, g)),
        out_shape=jax.ShapeDtypeStruct((T, cf.HQ * LANE), BF16), compiler_params=_cp(("arbitrary", "arbitrary")))(q, k, z)


def _att_bwd(cf, q, k, z, dcat):
    tq, T, G = cf.TQ, cf.T, cf.G
    vb = cf.OFF['av'] // LANE
    sc = HEAD_DIM ** -0.5

    def body(q_ref, k_ref, v_ref, do_ref, dq_ref, dk_ref, dv_ref):
        i = pl.program_id(1)
        mask = _att_mask(cf, i, tq)
        kv, vv = k_ref[...], v_ref[...].astype(BF16)
        dk = jnp.zeros((T, LANE), F32)
        dv = jnp.zeros((T, LANE), F32)
        for j in range(G):
            qj = q_ref[:, j * LANE:(j + 1) * LANE]
            do = do_ref[:, j * LANE:(j + 1) * LANE]
            p = _att_probs(qj, kv, mask)
            dv += lax.dot_general(p.astype(BF16), do, (_DN['tn'], ((), ())), preferred_element_type=F32)
            dp = lax.dot_general(do, vv, (_DN['nt'], ((), ())), preferred_element_type=F32)
            ds = p * (dp - jnp.sum(dp * p, axis=-1, keepdims=True)) * sc
            dsb = ds.astype(BF16)
            dq_ref[:, j * LANE:(j + 1) * LANE] = jnp.dot(dsb, kv, preferred_element_type=F32)
            dk += lax.dot_general(dsb, qj, (_DN['tn'], ((), ())), preferred_element_type=F32)

        @pl.when(i == 0)
        def _():
            dk_ref[...] = dk
            dv_ref[...] = dv

        @pl.when(i != 0)
        def _():
            dk_ref[...] += dk
            dv_ref[...] += dv

    qs = pl.BlockSpec((tq, G * LANE), lambda g, i: (i, g))
    ks = pl.BlockSpec((T, LANE), lambda g, i: (0, g))
    return pl.pallas_call(
        body, name="att_bwd", grid=(cf.HKV, T // tq),
        in_specs=[qs, ks, pl.BlockSpec((T, LANE), lambda g, i: (0, vb + g)), qs],
        out_specs=[qs, ks, ks],
        out_shape=[jax.ShapeDtypeStruct((T, cf.HQ * LANE), F32), jax.ShapeDtypeStruct((T, cf.HKV * LANE), F32),
                   jax.ShapeDtypeStruct((T, cf.HKV * LANE), F32)],
        compiler_params=_cp(("arbitrary", "arbitrary")))(q, k, z, dcat)


def _ret_masks(cf, i, tq, lgf, lgb):
    T, LC = cf.T, cf.LC
    row = lax.broadcasted_iota(jnp.int32, (tq, T), 0) + i * tq
    col = lax.broadcasted_iota(jnp.int32, (tq, T), 1)

    def pb(n):
        return jnp.where(n < LC, LC - 1 - n, T + LC - 1 - n)

    relf = row - col
    relb = pb(row) - pb(col)
    okf, okb = relf >= 0, relb >= 0
    rf = jnp.where(okf, relf, 0).astype(F32)
    rb = jnp.where(okb, relb, 0).astype(F32)
    mf = jnp.where(okf, jnp.exp(lgf * rf), 0.0)
    mb = jnp.where(okb, jnp.exp(lgb * rb), 0.0)
    return mf, mb, rf, rb


def _ret_fwd(cf, q, k, z, lg):
    tq, T = cf.TQ, cf.T
    vb = cf.OFF['rv'] // LANE

    def body(lg_ref, q_ref, k_ref, v_ref, o_ref):
        h, i = pl.program_id(0), pl.program_id(1)
        mf, mb, _, _ = _ret_masks(cf, i, tq, lg_ref[0, h], lg_ref[1, h])
        a = lax.dot_general(q_ref[...], k_ref[...], (_DN['nt'], ((), ())), preferred_element_type=F32)
        p = (a * (mf + mb)).astype(BF16)
        o_ref[...] = jnp.dot(p, v_ref[...].astype(BF16), preferred_element_type=F32)

    return pl.pallas_call(
        body, name="ret_fwd", grid=(cf.HR, T // tq),
        in_specs=[pl.BlockSpec(memory_space=pltpu.SMEM), pl.BlockSpec((tq, LANE), lambda h, i: (i, h)),
                  pl.BlockSpec((T, LANE), lambda h, i: (0, h)), pl.BlockSpec((T, LANE), lambda h, i: (0, vb + h))],
        out_specs=pl.BlockSpec((tq, LANE), lambda h, i: (i, h)),
        out_shape=jax.ShapeDtypeStruct((T, cf.HR * LANE), F32), compiler_params=_cp(("arbitrary", "arbitrary")))(lg, q, k, z)


def _ret_bwd(cf, q, k, z, lg, do):
    tq, T = cf.TQ, cf.T
    vb = cf.OFF['rv'] // LANE

    def body(lg_ref, q_ref, k_ref, v_ref, do_ref, dq_ref, dk_ref, dv_ref, dlg_ref):
        h, i = pl.program_id(0), pl.program_id(1)
        mf, mb, rf, rb = _ret_masks(cf, i, tq, lg_ref[0, h], lg_ref[1, h])
        qv, kv, vv = q_ref[...], k_ref[...], v_ref[...].astype(BF16)
        dob = do_ref[...].astype(BF16)
        a = lax.dot_general(qv, kv, (_DN['nt'], ((), ())), preferred_element_type=F32)
        m = mf + mb
        p = (a * m).astype(BF16)
        dv = lax.dot_general(p, dob, (_DN['tn'], ((), ())), preferred_element_type=F32)
        dp = lax.dot_general(dob, vv, (_DN['nt'], ((), ())), preferred_element_type=F32)
        da = (dp * m).astype(BF16)
        dq_ref[...] = jnp.dot(da, kv, preferred_element_type=F32)
        dk = lax.dot_general(da, qv, (_DN['tn'], ((), ())), preferred_element_type=F32)
        dm = dp * a
        dlf = jnp.sum(jnp.sum(dm * mf * rf, axis=-1, keepdims=True), axis=0, keepdims=True)
        dlb = jnp.sum(jnp.sum(dm * mb * rb, axis=-1, keepdims=True), axis=0, keepdims=True)
        rid = lax.broadcasted_iota(jnp.int32, (8, LANE), 0)
        dl = jnp.where(rid == 0, dlf, jnp.where(rid == 1, dlb, 0.0))

        @pl.when(i == 0)
        def _():
            dk_ref[...] = dk
            dv_ref[...] = dv
            dlg_ref[...] = dl

        @pl.when(i != 0)
        def _():
            dk_ref[...] += dk
            dv_ref[...] += dv
            dlg_ref[...] += dl

    qs = pl.BlockSpec((tq, LANE), lambda h, i: (i, h))
    ks = pl.BlockSpec((T, LANE), lambda h, i: (0, h))
    return pl.pallas_call(
        body, name="ret_bwd", grid=(cf.HR, T // tq),
        in_specs=[pl.BlockSpec(memory_space=pltpu.SMEM), qs, ks, pl.BlockSpec((T, LANE), lambda h, i: (0, vb + h)), qs],
        out_specs=[qs, ks, ks, pl.BlockSpec((None, 8, LANE), lambda h, i: (h, 0, 0))],
        out_shape=[jax.ShapeDtypeStruct((T, cf.HR * LANE), F32)] * 3 + [jax.ShapeDtypeStruct((cf.HR, 8, LANE), F32)],
        compiler_params=_cp(("arbitrary", "arbitrary")))(lg, q, k, z, do)


def _gla_consts(d):
    C, SB = GLA_CHUNK, GLA_SUB
    nsb = C // SB
    fwd = d == 0
    r = lax.broadcasted_iota(jnp.int32, (C, C), 0)
    m = lax.broadcasted_iota(jnp.int32, (C, C), 1)
    allowed = jnp.where(fwd, r - m, m - r) >= 0
    blocks = [allowed]
    vis = []
    rr = lax.broadcasted_iota(jnp.int32, (C, LANE), 0)
    for b in range(nsb):
        blocks.append(jnp.where(fwd, SB * b - 1 - m, m - SB * (b + 1)) >= 0)
        vis.append(jnp.where(fwd, SB * (b + 1) - 1 - rr, rr - SB * b) >= 0)
    cm = jnp.concatenate([x.astype(F32) for x in blocks] + [jnp.ones((LANE, C), F32)], axis=0)
    return cm, allowed, vis


def _gla_step(q, k, v, la, st, lmask, cm, allowed, vis):
    C, SB = GLA_CHUNK, GLA_SUB
    nsb = C // SB
    qs = q * lmask * ((HEAD_DIM // 2) ** -0.5)
    ks = k * lmask
    cums = jnp.dot(cm, la, precision=HI, preferred_element_type=F32)
    cum = cums[0:C]
    tot = cums[(1 + nsb) * C:]
    rows = []
    for b in range(nsb):
        ref = cums[(1 + b) * C:(2 + b) * C]
        sl = slice(b * SB, (b + 1) * SB)
        qt = qs[sl] * jnp.exp(cum[sl] - ref[sl])
        kt = jnp.where(vis[b], ks * jnp.exp(jnp.where(vis[b], ref - cum, 0.0)), 0.0)
        rows.append(lax.dot_general(qt, kt, (_DN['nt'], ((), ())), precision=HI, preferred_element_type=F32))
    att = jnp.where(allowed, jnp.concatenate(rows, axis=0), 0.0)
    o = jnp.dot(att.astype(BF16), v.astype(BF16), preferred_element_type=F32)
    o += lax.dot_general((qs * jnp.exp(cum)).astype(BF16), st.astype(BF16), (_DN['nt'], ((), ())), preferred_element_type=F32)
    kd = (ks * jnp.exp(tot[0:C] - cum)).astype(BF16)
    st_new = st * jnp.exp(tot) + lax.dot_general(v.astype(BF16), kd, (_DN['tn'], ((), ())), preferred_element_type=F32)
    return o, st_new


def _gla_chunk_id(cf, s, d):
    nct, nc = cf.LC // GLA_CHUNK, cf.T // GLA_CHUNK
    back = jnp.where(s < nct, nct - 1 - s, nc + nct - 1 - s)
    return jnp.where(d == 0, s, back)


def _gla_specs(cf):
    T = cf.T
    qb, kb, vb = cf.OFF['gq'] // LANE, cf.OFF['gk'] // LANE, cf.OFF['gv'] // LANE
    qs = pl.BlockSpec((T, LANE), lambda p, h, d: (0, qb + p))
    ks = pl.BlockSpec((T, LANE), lambda p, h, d: (0, kb + p))
    vs = pl.BlockSpec((T, LANE), lambda p, h, d: (0, vb + 2 * p + h))
    las = pl.BlockSpec((T, LANE), lambda p, h, d: (0, p))
    return qs, ks, vs, las


def _gla_fwd(cf, z, laf, lab):
    T, C = cf.T, GLA_CHUNK
    nc = T // C
    qs, ks, vs, las = _gla_specs(cf)

    def body(q_ref, k_ref, v_ref, laf_ref, lab_ref, o_ref, sst_ref, st_scr):
        h, d = pl.program_id(1), pl.program_id(2)
        cm, allowed, vis = _gla_consts(d)
        lmask = (lax.broadcasted_iota(jnp.int32, (1, LANE), 1) // (LANE // 2) == h).astype(F32)
        st_scr[...] = jnp.zeros_like(st_scr)

        def loop(s, carry):
            cid = _gla_chunk_id(cf, s, d)
            rows = pl.ds(pl.multiple_of(cid * C, C), C)
            la = jnp.where(d == 0, laf_ref[rows, :], lab_ref[rows, :])
            st = st_scr[...]
            sst_ref[cid] = st
            o, stn = _gla_step(q_ref[rows, :], k_ref[rows, :], v_ref[rows, :], la, st, lmask, cm, allowed, vis)
            st_scr[...] = stn

            @pl.when(d == 0)
            def _():
                o_ref[rows, :] = o

            @pl.when(d != 0)
            def _():
                o_ref[rows, :] += o

            return carry

        lax.fori_loop(0, nc, loop, 0)

    return pl.pallas_call(
        body, name="gla_fwd", grid=(cf.HG // 2, 2, 2), in_specs=[qs, ks, vs, las, las],
        out_specs=[pl.BlockSpec((T, LANE), lambda p, h, d: (0, 2 * p + h)),
                   pl.BlockSpec((None, nc, LANE, LANE), lambda p, h, d: ((2 * p + h) * 2 + d, 0, 0, 0))],
        out_shape=[jax.ShapeDtypeStruct((T, cf.HG * LANE), F32), jax.ShapeDtypeStruct((cf.HG * 2, nc, LANE, LANE), F32)],
        scratch_shapes=[pltpu.VMEM((LANE, LANE), F32)],
        compiler_params=_cp(("arbitrary", "arbitrary", "arbitrary")))(z, z, z, laf, lab)


def _gla_bwd(cf, z, laf, lab, sst, do):
    T, C = cf.T, GLA_CHUNK
    nc = T // C
    qs, ks, vs, las = _gla_specs(cf)

    def body(q_ref, k_ref, v_ref, laf_ref, lab_ref, sst_ref, do_ref, dq_ref, dk_ref, dv_ref, dlf_ref, dlb_ref, dst_scr):
        h, d = pl.program_id(1), pl.program_id(2)
        cm, allowed, vis = _gla_consts(d)
        lmask = (lax.broadcasted_iota(jnp.int32, (1, LANE), 1) // (LANE // 2) == h).astype(F32)
        dst_scr[...] = jnp.zeros_like(dst_scr)

        @pl.when(jnp.logical_and(h == 0, d == 0))
        def _():
            dq_ref[...] = jnp.zeros_like(dq_ref)
            dk_ref[...] = jnp.zeros_like(dk_ref)
            dlf_ref[...] = jnp.zeros_like(dlf_ref)
            dlb_ref[...] = jnp.zeros_like(dlb_ref)

        @pl.when(d == 0)
        def _():
            dv_ref[...] = jnp.zeros_like(dv_ref)

        isf = (d == 0).astype(F32)

        def loop(t, carry):
            cid = _gla_chunk_id(cf, nc - 1 - t, d)
            rows = pl.ds(pl.multiple_of(cid * C, C), C)
            la = jnp.where(d == 0, laf_ref[rows, :], lab_ref[rows, :])
            step = functools.partial(_gla_step, lmask=lmask, cm=cm, allowed=allowed, vis=vis)
            _, vjp_fn = jax.vjp(step, q_ref[rows, :], k_ref[rows, :], v_ref[rows, :], la, sst_ref[cid])
            dq, dk, dv, dla, dst = vjp_fn((do_ref[rows, :], dst_scr[...]))
            dst_scr[...] = dst
            dq_ref[rows, :] += dq
            dk_ref[rows, :] += dk
            dv_ref[rows, :] += dv
            dlf_ref[rows, :] += dla * isf
            dlb_ref[rows, :] += dla * (1.0 - isf)
            return carry

        lax.fori_loop(0, nc, loop, 0)

    pair = pl.BlockSpec((T, LANE), lambda p, h, d: (0, p))
    head = pl.BlockSpec((T, LANE), lambda p, h, d: (0, 2 * p + h))
    npair = cf.HG // 2
    return pl.pallas_call(
        body, name="gla_bwd", grid=(npair, 2, 2),
        in_specs=[qs, ks, vs, las, las, pl.BlockSpec((None, nc, LANE, LANE), lambda p, h, d: ((2 * p + h) * 2 + d, 0, 0, 0)), head],
        out_specs=[pair, pair, head, pair, pair],
        out_shape=[jax.ShapeDtypeStruct((T, npair * LANE), F32), jax.ShapeDtypeStruct((T, npair * LANE), F32),
                   jax.ShapeDtypeStruct((T, cf.HG * LANE), F32), jax.ShapeDtypeStruct((T, npair * LANE), F32),
                   jax.ShapeDtypeStruct((T, npair * LANE), F32)],
        scratch_shapes=[pltpu.VMEM((LANE, LANE), F32)],
        compiler_params=_cp(("arbitrary", "arbitrary", "arbitrary")))(z, z, z, laf, lab, sst, do)


def _conv_parts(cf, a, w_ref):
    T, LC = cf.T, cf.LC
    rid = lax.broadcasted_iota(jnp.int32, a.shape, 0)
    first = jnp.logical_or(rid == 0, rid == LC)
    last = jnp.logical_or(rid == LC - 1, rid == T - 1)
    ap = jnp.where(first, 0.0, pltpu.roll(a, 1, axis=0))
    an = jnp.where(last, 0.0, pltpu.roll(a, T - 1, axis=0))
    w0, w1, w2, b = w_ref[0:1, :], w_ref[1:2, :], w_ref[2:3, :], w_ref[3:4, :]
    ac = ap * w0 + a * w1 + an * w2 + b
    return ap, an, ac, first, last, (w0, w1, w2)


def _conv_fwd(cf, u, wb):
    T, Fd = cf.T, cf.F
    tc = _tile(Fd, 512)
    nj = Fd // tc

    def body(a_ref, v_ref, w_ref, t_ref):
        _, _, ac, _, _, _ = _conv_parts(cf, a_ref[...], w_ref)
        t_ref[...] = (_silu(ac) * v_ref[...]).astype(t_ref.dtype)

    return pl.pallas_call(
        body, name="conv_fwd", grid=(nj,),
        in_specs=[pl.BlockSpec((T, tc), lambda j: (0, j)), pl.BlockSpec((T, tc), lambda j: (0, nj + j)),
                  pl.BlockSpec((8, tc), lambda j: (0, j))],
        out_specs=pl.BlockSpec((T, tc), lambda j: (0, j)), out_shape=jax.ShapeDtypeStruct((T, Fd), BF16),
        compiler_params=_cp(("parallel",)))(u, u, wb)


def _conv_bwd(cf, u, wb, dt):
    T, Fd = cf.T, cf.F
    tc = _tile(Fd, 256)
    nj = Fd // tc

    def body(a_ref, v_ref, w_ref, dt_ref, da_ref, dv_ref, dw_ref):
        a, v, dtv = a_ref[...], v_ref[...], dt_ref[...].astype(F32)
        ap, an, ac, first, last, (w0, w1, w2) = _conv_parts(cf, a, w_ref)
        sg = _sigmoid(ac)
        dv_ref[...] = (dtv * ac * sg).astype(dv_ref.dtype)
        dac = dtv * v * (sg * (1.0 + ac * (1.0 - sg)))
        from_next = pltpu.roll(jnp.where(first, 0.0, dac), T - 1, axis=0)
        from_prev = pltpu.roll(jnp.where(last, 0.0, dac), 1, axis=0)
        da_ref[...] = (dac * w1 + from_next * w0 + from_prev * w2).astype(da_ref.dtype)
        rows = [jnp.sum(dac * ap, axis=0, keepdims=True), jnp.sum(dac * a, axis=0, keepdims=True),
                jnp.sum(dac * an, axis=0, keepdims=True), jnp.sum(dac, axis=0, keepdims=True)]
        rid = lax.broadcasted_iota(jnp.int32, (8, tc), 0)
        dw = jnp.zeros((8, tc), F32)
        for n_, rw in enumerate(rows):
            dw = jnp.where(rid == n_, rw, dw)
        dw_ref[...] = dw

    col = pl.BlockSpec((T, tc), lambda j: (0, j))
    return pl.pallas_call(
        body, name="conv_bwd", grid=(nj,),
        in_specs=[col, pl.BlockSpec((T, tc), lambda j: (0, nj + j)), pl.BlockSpec((8, tc), lambda j: (0, j)), col],
        out_specs=[col, col, pl.BlockSpec((8, tc), lambda j: (0, j))],
        out_shape=[jax.ShapeDtypeStruct((T, Fd), BF16), jax.ShapeDtypeStruct((T, Fd), BF16), jax.ShapeDtypeStruct((8, Fd), F32)],
        compiler_params=_cp(("parallel",)))(u, u, wb, dt)


def _me():
    x, y, c = lax.axis_index("x"), lax.axis_index("y"), lax.axis_index("c")
    return x, y, c, 4 * x + 2 * y + c


def _peer(x, y, c, k):
    px = 1 - x if (k >> 2) & 1 else x
    py = 1 - y if (k >> 1) & 1 else y
    pc = 1 - c if k & 1 else c
    return (px, py, pc), 4 * px + 2 * py + pc


def _rcopy(src, dst, ss, rs, tgt):
    return pltpu.make_async_remote_copy(src_ref=src, dst_ref=dst, send_sem=ss, recv_sem=rs, device_id=tgt,
                                        device_id_type=pl.DeviceIdType.MESH)


def _ag_small(name, v):
    R, Cc = v.shape

    def body(v_ref, o_ref, ssem, rsem, lsem):
        x, y, c, me = _me()
        loc = pltpu.make_async_copy(v_ref, o_ref.at[me], lsem)
        loc.start()
        sends = []
        for k in range(1, N_DEV):
            tgt, _ = _peer(x, y, c, k)
            cp = _rcopy(v_ref, o_ref.at[me], ssem.at[k - 1], rsem.at[k - 1], tgt)
            cp.start()
            sends.append(cp)
        for k in range(1, N_DEV):
            tgt, pi = _peer(x, y, c, k)
            _rcopy(v_ref, o_ref.at[pi], ssem.at[k - 1], rsem.at[k - 1], tgt).wait_recv()
        for cp in sends:
            cp.wait_send()
        loc.wait()

    vm = pl.BlockSpec(memory_space=pltpu.VMEM)
    return pl.pallas_call(
        body, name=name, in_specs=[vm], out_specs=vm, out_shape=jax.ShapeDtypeStruct((N_DEV, R, Cc), v.dtype),
        scratch_shapes=[pltpu.SemaphoreType.DMA((N_DEV - 1,)), pltpu.SemaphoreType.DMA((N_DEV - 1,)), pltpu.SemaphoreType.DMA],
        compiler_params=pltpu.CompilerParams(vmem_limit_bytes=VMEM_LIMIT))(v)


_HBM = pl.BlockSpec(memory_space=pltpu.HBM)
_SEM = pl.BlockSpec(memory_space=pltpu.SEMAPHORE)
_EFFECT = pltpu.SideEffectType.DATAFLOW_SIDE_EFFECTING
_KINDS = ['in', 'out', 'up', 'down']


def _hbm(a):
    return pltpu.with_memory_space_constraint(a, pltpu.HBM)


def _shard_shape(cf, kind):
    D, Fd = cf.D, cf.F
    return {'in': (D, cf.NINS), 'out': (D // N_DEV, D), 'up': (D, 2 * Fd // N_DEV), 'down': (Fd // N_DEV, D)}[kind]


def _whole_shape(cf, kind):
    D, Fd = cf.D, cf.F
    return {'in': (N_DEV, D, cf.NINS), 'out': (D, D), 'up': (D, 2 * Fd), 'down': (Fd, D)}[kind]


def _part(ref, cf, kind, idx):
    r, cdim = _shard_shape(cf, kind)
    if kind == 'in':
        return ref.at[idx]
    if kind == 'up':
        return ref.at[:, pl.ds(pl.multiple_of(idx * cdim, cdim), cdim)]
    return ref.at[pl.ds(pl.multiple_of(idx * r, r), r), :]


def _ag_start(cf, shards):
    npc = DEPTH * len(_KINDS)
    nio = len(_KINDS) + npc

    def body(*refs):
        srcs, lands = refs[:len(_KINDS)], refs[len(_KINDS):nio]
        ssems, rsems = refs[2 * nio:2 * nio + npc], refs[2 * nio + npc:2 * nio + 2 * npc]
        token, lsem = refs[2 * nio + 2 * npc], refs[2 * nio + 2 * npc + 1]
        x, y, c, me = _me()
        locs = []
        for l in range(DEPTH):
            for n, kind in enumerate(_KINDS):
                p = l * len(_KINDS) + n
                src, dst = srcs[n].at[l], _part(lands[p], cf, kind, me)
                locs.append(pltpu.make_async_copy(src, dst, lsem.at[p]))
                locs[-1].start()
                for k in range(1, N_DEV):
                    tgt, _ = _peer(x, y, c, k)
                    _rcopy(src, dst, ssems[p].at[k - 1], rsems[p].at[k - 1], tgt).start()
        for cp in locs:
            cp.wait()
        token[...] = jnp.zeros_like(token)

    land_shapes = [_whole_shape(cf, kind) for _ in range(DEPTH) for kind in _KINDS]
    out_shape = [pltpu.HBM(s.shape, BF16) for s in shards] + [pltpu.HBM(s, BF16) for s in land_shapes]
    out_shape += [pltpu.SemaphoreType.DMA((N_DEV - 1,))] * (2 * npc) + [jax.ShapeDtypeStruct((8, LANE), F32)]
    res = pl.pallas_call(
        body, name="ag_start", in_specs=[_HBM] * nio, out_specs=[_HBM] * nio + [_SEM] * (2 * npc) + [pl.BlockSpec(memory_space=pltpu.VMEM)],
        out_shape=out_shape, input_output_aliases={i: i for i in range(nio)}, scratch_shapes=[pltpu.SemaphoreType.DMA((npc,))],
        compiler_params=pltpu.CompilerParams(has_side_effects=_EFFECT))(
            *[_hbm(s) for s in shards], *[_hbm(lax.empty(s, BF16)) for s in land_shapes])
    return res[:len(_KINDS)], res[len(_KINDS):nio], res[nio:nio + npc], res[nio + npc:nio + 2 * npc], res[nio + 2 * npc]


def _ag_wait(cf, kind, l, src, land, ssem, rsem, after):
    def body(src_ref, land_ref, ssem_ref, rsem_ref, after_ref, src_out, land_out):
        x, y, c, me = _me()
        for k in range(1, N_DEV):
            tgt, pi = _peer(x, y, c, k)
            cp = _rcopy(src_ref.at[l], _part(land_ref, cf, kind, pi), ssem_ref.at[k - 1], rsem_ref.at[k - 1], tgt)
            cp.wait_send()
            cp.wait_recv()

    return pl.pallas_call(
        body, name="ag_wait_%s_%d" % (kind, l), in_specs=[_HBM, _HBM, _SEM, _SEM, pl.BlockSpec(memory_space=pl.ANY)],
        out_specs=[_HBM, _HBM], out_shape=[pltpu.HBM(src.shape, src.dtype), pltpu.HBM(land.shape, land.dtype)],
        input_output_aliases={0: 0, 1: 1}, compiler_params=pltpu.CompilerParams(has_side_effects=_EFFECT))(src, land, ssem, rsem, after)


def _rs_slab(ref, cf, kind, j):
    return ref.at[j] if kind in ('in', 'up') else _part(ref, cf, kind, j)


def _rs_start(cf, kind, l, g):
    def body(g_ref, recv_ref, g_out, recv_out, ssem, rsem, token, lsem):
        x, y, c, me = _me()
        loc = pltpu.make_async_copy(_rs_slab(g_ref, cf, kind, me), recv_ref.at[me], lsem)
        loc.start()
        for k in range(1, N_DEV):
            tgt, pi = _peer(x, y, c, k)
            _rcopy(_rs_slab(g_ref, cf, kind, pi), recv_ref.at[me], ssem.at[k - 1], rsem.at[k - 1], tgt).start()
        loc.wait()
        token[...] = jnp.zeros_like(token)

    rshape = (N_DEV,) + _shard_shape(cf, kind)
    sems = pltpu.SemaphoreType.DMA((N_DEV - 1,))
    return pl.pallas_call(
        body, name="rs_start_%s_%d" % (kind, l), in_specs=[_HBM, _HBM],
        out_specs=[_HBM, _HBM, _SEM, _SEM, pl.BlockSpec(memory_space=pltpu.VMEM)],
        out_shape=[pltpu.HBM(g.shape, BF16), pltpu.HBM(rshape, BF16), sems, sems, jax.ShapeDtypeStruct((8, LANE), F32)],
        input_output_aliases={0: 0, 1: 1}, scratch_shapes=[pltpu.SemaphoreType.DMA],
        compiler_params=pltpu.CompilerParams(has_side_effects=_EFFECT))(_hbm(g), _hbm(lax.empty(rshape, BF16)))


def _rs_wait(cf, kind, l, g, recv, ssem, rsem, after):
    def body(g_ref, recv_ref, ssem_ref, rsem_ref, after_ref, g_out, recv_out):
        x, y, c, me = _me()
        for k in range(1, N_DEV):
            tgt, pi = _peer(x, y, c, k)
            cp = _rcopy(_rs_slab(g_ref, cf, kind, pi), recv_ref.at[pi], ssem_ref.at[k - 1], rsem_ref.at[k - 1], tgt)
            cp.wait_send()
            cp.wait_recv()

    return pl.pallas_call(
        body, name="rs_wait_%s_%d" % (kind, l), in_specs=[_HBM, _HBM, _SEM, _SEM, pl.BlockSpec(memory_space=pl.ANY)],
        out_specs=[_HBM, _HBM], out_shape=[pltpu.HBM(g.shape, g.dtype), pltpu.HBM(recv.shape, recv.dtype)],
        input_output_aliases={0: 0, 1: 1}, compiler_params=pltpu.CompilerParams(has_side_effects=_EFFECT))(g, recv, ssem, rsem, after)[1]


def _adam_vals(w, g, m, v):
    m2 = ADAM_B1 * m + (1.0 - ADAM_B1) * g
    v2 = ADAM_B2 * v + (1.0 - ADAM_B2) * (g * g)
    mh = m2 / (1.0 - ADAM_B1 ** ADAM_STEP)
    vh = v2 / (1.0 - ADAM_B2 ** ADAM_STEP)
    return -ADAM_LR * (mh / (jnp.sqrt(vh) + ADAM_EPS) + ADAM_WD * w), m2, v2


def _row_tile(R, Cc, budget_elems):
    t = max(16, min(R, (budget_elems // max(Cc, 1)) // 16 * 16))
    while t > 16 and R % t:
        t -= 16
    return t if R % t == 0 else R


def _cast_bf16(name, w):
    Dp, R, Cc = w.shape
    tr = _row_tile(R, Cc, 1 << 20)

    def body(w_ref, o_ref):
        o_ref[...] = w_ref[...].astype(BF16)

    spec = pl.BlockSpec((None, tr, Cc), lambda l, i: (l, i, 0))
    return pl.pallas_call(body, name=name, grid=(Dp, R // tr), in_specs=[spec], out_specs=spec,
                          out_shape=jax.ShapeDtypeStruct(w.shape, BF16), compiler_params=_cp(("parallel", "parallel")))(w)


def _sum_adam(name, recvs, w, m, v):
    Dp, R, Cc = w.shape
    tr = _row_tile(R, Cc, 1 << 18)
    ni = R // tr

    def body(*refs):
        r_refs = refs[:Dp]
        w_ref, m_ref, v_ref, g_ref, d_ref, mo_ref, vo_ref = refs[Dp:]
        for layer in range(Dp):
            @pl.when(pl.program_id(0) == layer)
            def _():
                r_ref = r_refs[layer]
                g = r_ref[0].astype(F32)
                for s in range(1, N_DEV):
                    g = g + r_ref[s].astype(F32)
                dl, m2, v2 = _adam_vals(w_ref[...], g, m_ref[...], v_ref[...])
                g_ref[...] = g
                d_ref[...] = dl
                mo_ref[...] = m2
                vo_ref[...] = v2

    spec = pl.BlockSpec((None, tr, Cc), lambda l, i: (l, i, 0))
    rspecs = [pl.BlockSpec((N_DEV, tr, Cc), functools.partial(
        lambda l, i, layer: (0, jnp.where(l == layer, i, jnp.where(l > layer, ni - 1, 0)), 0), layer=layer)) for layer in range(Dp)]
    return pl.pallas_call(body, name=name, grid=(Dp, ni), in_specs=rspecs + [spec, spec, spec], out_specs=[spec] * 4,
                          out_shape=[jax.ShapeDtypeStruct(w.shape, F32)] * 4, compiler_params=_cp(("arbitrary", "arbitrary")))(*recvs, w, m, v)


def _adam(name, w, g, m, v):
    R, Cc = w.shape
    tr = _row_tile(R, Cc, 1 << 18)

    def body(w_ref, g_ref, m_ref, v_ref, d_ref, mo_ref, vo_ref):
        dl, m2, v2 = _adam_vals(w_ref[...], g_ref[...], m_ref[...], v_ref[...])
        d_ref[...] = dl
        mo_ref[...] = m2
        vo_ref[...] = v2

    spec = pl.BlockSpec((tr, Cc), lambda i: (i, 0))
    return pl.pallas_call(body, name=name, grid=(R // tr,), in_specs=[spec] * 4, out_specs=[spec] * 3,
                          out_shape=[jax.ShapeDtypeStruct(w.shape, F32)] * 3, compiler_params=_cp(("parallel",)))(w, g, m, v)


def _sum8(name, a):
    n, R, Cc = a.shape

    def body(a_ref, o_ref):
        s = a_ref[0]
        for k in range(1, n):
            s = s + a_ref[k]
        o_ref[...] = s

    return pl.pallas_call(body, name=name, in_specs=[pl.BlockSpec(memory_space=pltpu.VMEM)],
                          out_specs=pl.BlockSpec(memory_space=pltpu.VMEM), out_shape=jax.ShapeDtypeStruct((R, Cc), F32),
                          compiler_params=pltpu.CompilerParams(vmem_limit_bytes=VMEM_LIMIT))(a)


def _ada_fwd(cf, c9, ada_w):
    D = cf.D
    NS = ada_w.shape[-1]
    tk = _tile(D, 512)
    nk = D // tk

    def body(c_ref, w_ref, o_ref):
        kk = pl.program_id(1)
        s = _silu(c_ref[...]).astype(BF16)
        part = jnp.dot(s, w_ref[...].astype(BF16), preferred_element_type=F32)

        @pl.when(kk == 0)
        def _():
            o_ref[...] = part

        @pl.when(kk != 0)
        def _():
            o_ref[...] += part

    return pl.pallas_call(
        body, name="ada_fwd", grid=(DEPTH, nk),
        in_specs=[pl.BlockSpec((16, tk), lambda l, k: (0, k)), pl.BlockSpec((None, tk, NS), lambda l, k: (l, k, 0))],
        out_specs=pl.BlockSpec((None, 16, NS), lambda l, k: (l, 0, 0)),
        out_shape=jax.ShapeDtypeStruct((DEPTH, 16, NS), F32), compiler_params=_cp(("parallel", "arbitrary")))(c9, ada_w)


def _ada_bwd(cf, c9, ada_w, dm9):
    D = cf.D
    NS = ada_w.shape[-1]
    tk = _tile(D, 512)
    nk = D // tk

    def body(c_ref, w_ref, dm_ref, gw_ref, ds_ref):
        cv = c_ref[...]
        sg = _sigmoid(cv)
        dmb = dm_ref[...].astype(BF16)
        gw_ref[...] = lax.dot_general((cv * sg).astype(BF16), dmb, (_DN['tn'], ((), ())), preferred_element_type=F32)
        ds = lax.dot_general(dmb, w_ref[...].astype(BF16), (_DN['nt'], ((), ())), preferred_element_type=F32)
        ds_ref[...] = ds * (sg * (1.0 + cv * (1.0 - sg)))

    return pl.pallas_call(
        body, name="ada_bwd", grid=(DEPTH, nk),
        in_specs=[pl.BlockSpec((16, tk), lambda l, k: (0, k)), pl.BlockSpec((None, tk, NS), lambda l, k: (l, k, 0)),
                  pl.BlockSpec((None, 16, NS), lambda l, k: (l, 0, 0))],
        out_specs=[pl.BlockSpec((None, tk, NS), lambda l, k: (l, k, 0)), pl.BlockSpec((None, 16, tk), lambda l, k: (l, 0, k))],
        out_shape=[jax.ShapeDtypeStruct((DEPTH, D, NS), F32), jax.ShapeDtypeStruct((DEPTH, 16, D), F32)],
        compiler_params=_cp(("parallel", "parallel")))(c9, ada_w, dm9)


def _rope_tables(cf):
    L, LC = cf.L, cf.LC
    rows = L // GRID_W
    row = jnp.repeat(jnp.arange(rows, dtype=F32), GRID_W)
    col = jnp.tile(jnp.arange(GRID_W, dtype=F32), rows)
    nf = HEAD_DIM // 4
    inv = ROPE_THETA ** (-jnp.arange(nf, dtype=F32) / nf)
    ang = jnp.concatenate([row[:, None] * inv, col[:, None] * inv], axis=-1)
    cos, sin = jnp.cos(ang), jnp.sin(ang)
    cs = jnp.concatenate([jnp.ones((LC, HEAD_DIM), F32), jnp.concatenate([cos, cos], -1)], 0)
    sn = jnp.concatenate([jnp.zeros((LC, HEAD_DIM), F32), jnp.concatenate([-sin, sin], -1)], 0)
    return cs, sn


def _prep_tiles(cf, z, cs, sn, key):
    b = cf.OFF[key] // LANE
    return [(z, LANE, _col(b), True), (cs, LANE, _c0, False), (sn, LANE, _c0, False)]


_PREP = {'aq': _f_prep_norm, 'ak': _f_prep_norm, 'rq': _f_prep_plain, 'rk': _f_prep_scaled}


def _prep_fwd(cf, z, cs, sn, key, g):
    nh = cf.W[key] // LANE
    params = [(g, 'shared', True)] if g is not None else []
    return _row_fwd(cf, "prep_fwd_" + key, _PREP[key], _prep_tiles(cf, z, cs, sn, key), params,
                    [(LANE, _col(0), cf.W[key], BF16)], cf.TQ, nrep=nh)[0]


def _prep_bwd(cf, z, cs, sn, key, g, dt):
    nh = cf.W[key] // LANE
    params = [(g, 'shared', True)] if g is not None else []
    tg, pg = _row_bwd(cf, "prep_bwd_" + key, _PREP[key], _prep_tiles(cf, z, cs, sn, key), params,
                      [(dt, LANE, _col(0))], [(LANE, _col(0), cf.W[key], BF16)], cf.TQ, nrep=nh)
    return tg[0], (pg[0] if g is not None else None)


def _gate_params(cf, gup, gb):
    K = gup.shape[-1]
    gf = jnp.zeros((LANE, K), F32).at[0:GLA_RANK].set(gup[0])
    gbm = jnp.zeros((LANE, K), F32).at[GLA_RANK:2 * GLA_RANK].set(gup[1])
    return [(gf, 'shared', True), (gbm, 'shared', True), (gb[0:1], 'shared', True), (gb[1:2], 'shared', True)]


def _mix_tiles(cf, z, o, key):
    return [(o, LANE, _col(0), True), (z, LANE, _col(cf.OFF[key] // LANE), True)]


def _mid_io(cf, l, W, mod, x, y, zero=0.0):
    tiles = [(x, cf.D, _c0, True), (y, cf.D, _c0, True)]
    params = [(mod[2], 'stream', True), (W['norm2_g'][l] + zero, 'shared', True), (mod[3], 'stream', True), (mod[4], 'stream', True)]
    return tiles, params


class _BigWeights:
    def __init__(self, cf, srcs, lands, ssems, rsems):
        self.cf, self.srcs, self.lands, self.ssems, self.rsems = cf, list(srcs), lands, ssems, rsems
        self.whole = {}

    def get(self, kind, l, after=None):
        cf = self.cf
        if (kind, l) not in self.whole:
            n = _KINDS.index(kind)
            p = l * len(_KINDS) + n
            self.srcs[n], w = _ag_wait(cf, kind, l, self.srcs[n], self.lands[p], self.ssems[p], self.rsems[p], after)
            if kind == 'in':
                w = jnp.pad(_unshard_last(w), ((0, 0), (0, cf.NZ - cf.NIN)))
            self.whole[(kind, l)] = w
        return self.whole[(kind, l)]


def _layer_fwd(cf, l, W, big, mod, x, h, cs, sn):
    T, D, Fd = cf.T, cf.D, cf.F
    z = _mm("z_%d" % l, h, big.get('in', l, h), 'nn', T, cf.NZ, D, F32)
    qa = _prep_fwd(cf, z, cs, sn, 'aq', W['q_norm_g'][l])
    ka = _prep_fwd(cf, z, cs, sn, 'ak', W['k_norm_g'][l])
    qr = _prep_fwd(cf, z, cs, sn, 'rq', None)
    kr = _prep_fwd(cf, z, cs, sn, 'rk', None)
    o_att = _att_fwd(cf, qa, ka, z)
    o_ret = _ret_fwd(cf, qr, kr, z, W['ret_log_decay'][l])
    gates = _gate_params(cf, W['gla_gate_up'][l], W['gla_gate_b'][l])
    ga_tile = [(z, LANE, _col(cf.OFF['ga'] // LANE), True)]
    K = cf.HG * 64
    laf, lab = _row_fwd(cf, "gates_fwd_%d" % l, _f_gates, ga_tile, gates, [(K, _c0, K, F32), (K, _c0, K, F32)], cf.TQ)
    o_gla, sst = _gla_fwd(cf, z, laf, lab)
    cat_r = _row_fwd(cf, "mixr_fwd_%d" % l, _f_gated_norm, _mix_tiles(cf, z, o_ret, 'rg'), [(W['ret_norm_g'][l], 'shared', True)],
                     [(LANE, _col(0), cf.HR * LANE, BF16)], cf.TQ, nrep=cf.HR)[0]
    cat_g = _row_fwd(cf, "mixg_fwd_%d" % l, _f_gated_norm, _mix_tiles(cf, z, o_gla, 'gr'), [(W['gla_norm_g'][l], 'shared', True)],
                     [(LANE, _col(0), cf.HG * LANE, BF16)], cf.TQ, nrep=cf.HG)[0]
    cat = jnp.concatenate([o_att, cat_r, cat_g], axis=-1)
    y = _mm("y_%d" % l, cat, big.get('out', l, cat), 'nn', T, D, D, F32)
    tiles, params = _mid_io(cf, l, W, mod, x, y)
    x1, h2 = _row_fwd(cf, "mid_fwd_%d" % l, _f_resid_norm_mod, tiles, params, [(D, _c0, D, F32), (D, _c0, D, BF16)], cf.TM)
    u = _mm("u_%d" % l, h2, big.get('up', l, h2), 'nn', T, 2 * Fd, D, F32)
    t = _conv_fwd(cf, u, W['conv_wb'][l])
    yff = _mm("yff_%d" % l, t, big.get('down', l, t), 'nn', T, D, Fd, F32)
    return dict(x=x, h=h, z=z, qa=qa, ka=ka, qr=qr, kr=kr, laf=laf, lab=lab, sst=sst, o_ret=o_ret, o_gla=o_gla, cat=cat, y=y,
                x1=x1, h2=h2, u=u, t=t, yff=yff, gates=gates)


def _layer_bwd(cf, l, W, big, mod, sv, dx1, dyff, cs, sn):
    T, D, Fd = cf.T, cf.D, cf.F
    g, rs = {}, {}
    rs['down'] = _rs_start(cf, 'down', l, _mm("gwd_%d" % l, sv['t'], dyff, 'tn', Fd, D, T, BF16, tk=T))
    dt = _mm("dt_%d" % l, dyff, big.get('down', l), 'nt', T, Fd, D, BF16)
    da, dv, g['conv_wb'] = _conv_bwd(cf, sv['u'], W['conv_wb'][l] + rs['down'][4][0, 0], dt)
    du = jnp.concatenate([da, dv], axis=-1)
    cu = 2 * Fd // N_DEV
    rs['up'] = _rs_start(cf, 'up', l, _mm("gwu_%d" % l, sv['h2'], du, 'tn', D, 2 * Fd, T, BF16, tn=cu, tk=T, out_shape=(N_DEV, D, cu),
                                          out_spec=pl.BlockSpec((None, _tile(D, 768, 128), cu), lambda i, j, k: (j, i, 0))))
    dh2 = _mm("dh2_%d" % l, du, big.get('up', l), 'nt', T, D, 2 * Fd, BF16)
    tiles, params = _mid_io(cf, l, W, mod, sv['x'], sv['y'], rs['up'][4][0, 0])
    (dx, dy), (g['m2'], g['norm2_g'], g['m3'], g['m4']) = _row_bwd(
        cf, "mid_bwd_%d" % l, _f_resid_norm_mod, tiles, params, [(dx1, D, _c0), (dh2, D, _c0)],
        [(D, _c0, D, F32), (D, _c0, D, BF16)], cf.TM)
    rs['out'] = _rs_start(cf, 'out', l, _mm("gwo_%d" % l, sv['cat'], dy, 'tn', D, D, T, BF16, tk=T))
    dcat = _mm("dcat_%d" % l, dy, big.get('out', l), 'nt', T, D, D, BF16)
    z = sv['z']
    (do_ret, drg), (g['ret_norm_g'],) = _row_bwd(
        cf, "mixr_bwd_%d" % l, _f_gated_norm, _mix_tiles(cf, z, sv['o_ret'], 'rg'), [(W['ret_norm_g'][l] + rs['out'][4][0, 0], 'shared', True)],
        [(dcat, LANE, _col(cf.HQ))], [(LANE, _col(0), cf.HR * LANE, F32), (LANE, _col(0), cf.HR * LANE, BF16)], cf.TQ, nrep=cf.HR)
    (do_gla, dgr), (g['gla_norm_g'],) = _row_bwd(
        cf, "mixg_bwd_%d" % l, _f_gated_norm, _mix_tiles(cf, z, sv['o_gla'], 'gr'), [(W['gla_norm_g'][l], 'shared', True)],
        [(dcat, LANE, _col(cf.HQ + cf.HR))], [(LANE, _col(0), cf.HG * LANE, F32), (LANE, _col(0), cf.HG * LANE, BF16)], cf.TQ, nrep=cf.HG)
    dqa, dka, dav = _att_bwd(cf, sv['qa'], sv['ka'], z, dcat)
    dqr, dkr, drv, dlg = _ret_bwd(cf, sv['qr'], sv['kr'], z, W['ret_log_decay'][l], do_ret)
    g['ret_log_decay'] = dlg[:, 0:2, 0].T
    dgq, dgk, dgv, dlaf, dlab = _gla_bwd(cf, z, sv['laf'], sv['lab'], sv['sst'], do_gla)
    K = cf.HG * 64
    ga_tile = [(z, LANE, _col(cf.OFF['ga'] // LANE), True)]
    (dga,), (ggf, ggb, gbf, gbb) = _row_bwd(cf, "gates_bwd_%d" % l, _f_gates, ga_tile, sv['gates'],
                                             [(dlaf, K, _c0), (dlab, K, _c0)], [(LANE, _c0, LANE, BF16)], cf.TQ)
    g['gla_gate_up'] = jnp.stack([ggf[0:GLA_RANK], ggb[GLA_RANK:2 * GLA_RANK]])
    g['gla_gate_b'] = jnp.concatenate([gbf, gbb], axis=0)
    daq, g['q_norm_g'] = _prep_bwd(cf, z, cs, sn, 'aq', W['q_norm_g'][l], dqa)
    dak, g['k_norm_g'] = _prep_bwd(cf, z, cs, sn, 'ak', W['k_norm_g'][l], dka)
    drq, _ = _prep_bwd(cf, z, cs, sn, 'rq', None, dqr)
    drk, _ = _prep_bwd(cf, z, cs, sn, 'rk', None, dkr)
    pad = jnp.zeros((T, cf.NZ - cf.OFF['ga'] - LANE), BF16)
    dz = jnp.concatenate([daq, dak, dav.astype(BF16), drq, drk, drv.astype(BF16), drg, dgq.astype(BF16), dgk.astype(BF16),
                          dgv.astype(BF16), dgr, dga, pad], axis=-1)
    gwi = _mm("gwi_%d" % l, sv['h'], dz, 'tn', D, cf.NZ, T, BF16, tk=T)
    rs['in'] = _rs_start(cf, 'in', l, jnp.moveaxis(gwi[:, :cf.NIN].reshape(D, N_DEV, cf.NINS), 1, 0))
    dh = _mm("dh_%d" % l, dz, big.get('in', l), 'nt', T, D, cf.NZ, BF16)
    g['rs'] = rs
    return dx, dh, g, rs['in'][4][0, 0]


_WEIGHTS = ['c_ctx', 'ada_w', 'ada_b', 'norm1_g', 'w_in', 'q_norm_g', 'k_norm_g', 'ret_log_decay', 'ret_norm_g',
            'gla_gate_up', 'gla_gate_b', 'gla_norm_g', 'w_out', 'norm2_g', 'w_up', 'conv_w', 'conv_b', 'w_down', 'final_norm_g']
_BIG = ['w_in', 'w_out', 'w_up', 'w_down']
_SMALL = [n for n in _WEIGHTS if n not in _BIG and n != 'ada_w']
_COL_SHARDED = ['gla_gate_up', 'gla_gate_b', 'conv_w']


def _pack(arrs):
    flat = jnp.concatenate([a.reshape(-1) for a in arrs])
    n = flat.shape[0]
    tot = -(-n // (8 * LANE)) * (8 * LANE)
    return jnp.pad(flat, (0, tot - n)).reshape(tot // LANE, LANE)


def _unpack(flat, shapes):
    out, o = [], 0
    for s in shapes:
        n = int(np.prod(s))
        out.append(flat[..., o:o + n].reshape(flat.shape[:-1] + tuple(s)))
        o += n
    return out


def _unshard_last(a):
    return jnp.moveaxis(a, 0, -2).reshape(a.shape[1:-1] + (N_DEV * a.shape[-1],))


def _step(cf, x, c, ctx, loss_target, w, m, v):
    T, D, Fd, L, LC = cf.T, cf.D, cf.F, cf.L, cf.LC
    _, _, _, me = _me()
    NS = w['ada_w'].shape[-1]

    c_all = _ag_small("ag_c", jnp.pad(c, ((0, 7), (0, 0))))[:, 0, :]
    c9 = jnp.concatenate([c_all, w['c_ctx'][None], jnp.zeros((7, D), F32)], axis=0)
    pm = _ada_fwd(cf, c9, w['ada_w'])
    pm_all = _ag_small("ag_mod", pm.reshape(DEPTH * 16, NS)).reshape(N_DEV, DEPTH, 16, NS)
    mod_all = _unshard_last(pm_all) + w['ada_b'][:, None, :]
    mod_own = lax.dynamic_index_in_dim(mod_all, me, axis=1, keepdims=False)
    mods = []
    for l in range(DEPTH):
        mods.append([jnp.stack([mod_all[l, 8, k * D:(k + 1) * D], mod_own[l, k * D:(k + 1) * D]])[:, None, :] for k in range(N_MOD)])

    shard_shapes = [w[n].shape for n in _COL_SHARDED]
    got = _ag_small("ag_smallw", _pack([w[n] for n in _COL_SHARDED]))
    full = dict(zip(_COL_SHARDED, [_unshard_last(a) for a in _unpack(got.reshape(N_DEV, -1), shard_shapes)]))

    srcs, lands, ssems, rsems, ag_tok = _ag_start(cf, [_cast_bf16("cast_" + n, w[n]) for n in _BIG])
    big = _BigWeights(cf, srcs, lands, ssems, rsems)
    conv_wb = jnp.concatenate([full['conv_w'], w['conv_b'][:, None, :], jnp.zeros((DEPTH, 4, Fd), F32)], axis=1)
    W = dict(conv_wb=conv_wb, gla_gate_up=full['gla_gate_up'], gla_gate_b=full['gla_gate_b'], ret_log_decay=w['ret_log_decay'])
    for n in ['q_norm_g', 'k_norm_g', 'ret_norm_g', 'gla_norm_g', 'norm1_g', 'norm2_g']:
        W[n] = w[n][:, None, :]

    cs, sn = _rope_tables(cf)
    x0 = jnp.concatenate([ctx[0], x[0]], axis=0)
    pre_tiles = [(x0, D, _c0, True)]

    def pre_params(zero):
        return [(W['norm1_g'][0] + zero, 'shared', True), (mods[0][0], 'stream', True), (mods[0][1], 'stream', True)]

    def tr_params(zero):
        return [(mods[0][5], 'stream', True), (W['norm1_g'][1] + zero, 'shared', True), (mods[1][0], 'stream', True), (mods[1][1], 'stream', True)]

    h0 = _row_fwd(cf, "pre_fwd", _f_norm_mod, pre_tiles, pre_params(ag_tok[0, 0]), [(D, _c0, D, BF16)], cf.TM)[0]
    sv0 = _layer_fwd(cf, 0, W, big, mods[0], x0, h0, cs, sn)
    tr_tiles = [(sv0['x1'], D, _c0, True), (sv0['yff'], D, _c0, True)]
    xb, hb = _row_fwd(cf, "tr_fwd", _f_resid_norm_mod, tr_tiles, tr_params(0.0), [(D, _c0, D, F32), (D, _c0, D, BF16)], cf.TM)
    sv1 = _layer_fwd(cf, 1, W, big, mods[1], xb, hb, cs, sn)
    tgt = jnp.concatenate([jnp.zeros((LC, D), F32), loss_target[0]], axis=0)
    dx1, dyff, dm5_1, g_final, ls = _loss_grad(cf, sv1['x1'], sv1['yff'], mods[1][5], w['final_norm_g'][None], tgt)
    loss = lax.psum(ls[0, 0], ("x", "y", "c"))

    dxb, dhb, g1, zero1 = _layer_bwd(cf, 1, W, big, mods[1], sv1, dx1, dyff, cs, sn)
    (dx1_0, dyff_0), (dm5_0, gn1_1, dm0_1, dm1_1) = _row_bwd(
        cf, "tr_bwd", _f_resid_norm_mod, tr_tiles, tr_params(zero1), [(dxb, D, _c0), (dhb, D, _c0)],
        [(D, _c0, D, F32), (D, _c0, D, BF16)], cf.TM)
    dx0, dh0, g0, zero0 = _layer_bwd(cf, 0, W, big, mods[0], sv0, dx1_0, dyff_0, cs, sn)
    (dxa,), (gn1_0, dm0_0, dm1_0) = _row_bwd(cf, "pre_bwd", _f_first, pre_tiles, pre_params(zero0), [(dx0, D, _c0), (dh0, D, _c0)],
                                            [(D, _c0, D, F32)], cf.TM)
    grad_x = dxa[LC:][None]

    dmod = jnp.stack([jnp.concatenate([dm0_0, dm1_0, g0['m2'], g0['m3'], g0['m4'], dm5_0], axis=-1)[:, 0],
                      jnp.concatenate([dm0_1, dm1_1, g1['m2'], g1['m3'], g1['m4'], dm5_1], axis=-1)[:, 0]])
    dm_all = _ag_small("ag_dmod", jnp.pad(dmod.reshape(2 * DEPTH, N_MOD * D), ((0, 8 - 2 * DEPTH), (0, 0))))
    dm_all = dm_all[:, :2 * DEPTH].reshape(N_DEV, DEPTH, 2, N_MOD * D)
    dctx = _sum8("sum_dmodc", jnp.pad(dm_all[:, :, 0], ((0, 0), (0, 8 - DEPTH), (0, 0))))[:DEPTH]
    dm9 = jnp.concatenate([jnp.moveaxis(dm_all[:, :, 1], 0, 1), dctx[:, None]], axis=1)
    g_ada_b = _sum8("sum_adab", jnp.pad(jnp.moveaxis(dm9, 1, 0), ((0, 0), (0, 8 - DEPTH), (0, 0))))[:DEPTH]
    dm9s = lax.dynamic_slice_in_dim(jnp.pad(dm9, ((0, 0), (0, 7), (0, 0))), me * NS, NS, axis=2)
    g_ada_w, dsil = _ada_bwd(cf, c9, w['ada_w'], dm9s)
    g_cctx_part = dsil[0, 8]
    for l in range(1, DEPTH):
        g_cctx_part = g_cctx_part + dsil[l, 8]

    def both(key):
        return jnp.stack([g0[key], g1[key]])

    gsmall = dict(c_ctx=g_cctx_part, norm1_g=jnp.stack([gn1_0[0], gn1_1[0]]), q_norm_g=both('q_norm_g')[:, 0],
                  k_norm_g=both('k_norm_g')[:, 0], ret_log_decay=both('ret_log_decay'), ret_norm_g=both('ret_norm_g')[:, 0],
                  gla_gate_up=both('gla_gate_up'), gla_gate_b=both('gla_gate_b'), gla_norm_g=both('gla_norm_g')[:, 0],
                  norm2_g=both('norm2_g')[:, 0], conv_w=both('conv_wb')[:, 0:3], conv_b=both('conv_wb')[:, 3], final_norm_g=g_final[0])
    snames = [n for n in _SMALL if n != 'ada_b']
    sshapes = [gsmall[n].shape for n in snames]
    gs_all = _ag_small("ag_gsmall", _pack([gsmall[n] for n in snames]))
    gs = dict(zip(snames, _unpack(_sum8("sum_gsmall", gs_all).reshape(-1), sshapes)))
    gs['ada_b'] = g_ada_b
    for n in _COL_SHARDED:
        ns_ = w[n].shape[-1]
        gs[n] = lax.dynamic_slice_in_dim(gs[n], me * ns_, ns_, axis=gs[n].ndim - 1)

    out_g, out_d, out_m, out_v = {}, {}, {}, {}
    after = dxa
    for kind in reversed(_KINDS):
        n = 'w_' + kind
        recvs = [_rs_wait(cf, kind, l, *gl['rs'][kind][:4], after) for l, gl in enumerate((g0, g1))]
        out_g[n], out_d[n], out_m[n], out_v[n] = _sum_adam("adam_" + n, recvs, w[n], m[n], v[n])
        after = out_g[n]
    aw = [a.reshape(DEPTH * D, NS) for a in (w['ada_w'], g_ada_w, m['ada_w'], v['ada_w'])]
    out_g['ada_w'] = g_ada_w
    out_d['ada_w'], out_m['ada_w'], out_v['ada_w'] = [a.reshape(DEPTH, D, NS) for a in _adam("adam_ada_w", *aw)]
    shp = [w[n].shape for n in _SMALL]
    packed = [_pack([src[n] for n in _SMALL]) for src in (w, gs, m, v)]
    res = _adam("adam_small", *packed)
    for dst, pk in zip((out_d, out_m, out_v), res):
        dst.update(zip(_SMALL, _unpack(pk.reshape(-1), shp)))
    out_g.update({n: gs[n] for n in _SMALL})
    return (loss, grad_x, *[out_g[n] for n in _WEIGHTS], *[out_d[n] for n in _WEIGHTS], *[out_m[n] for n in _WEIGHTS],
            *[out_v[n] for n in _WEIGHTS])


def kernel(x, c, ctx, c_ctx, ada_w, ada_b, norm1_g, w_in, q_norm_g, k_norm_g, ret_log_decay, ret_norm_g, gla_gate_up, gla_gate_b, gla_norm_g, w_out, norm2_g, w_up, conv_w, conv_b, w_down, final_norm_g, loss_target, m_c_ctx, m_ada_w, m_ada_b, m_norm1_g, m_w_in, m_q_norm_g, m_k_norm_g, m_ret_log_decay, m_ret_norm_g, m_gla_gate_up, m_gla_gate_b, m_gla_norm_g, m_w_out, m_norm2_g, m_w_up, m_conv_w, m_conv_b, m_w_down, m_final_norm_g, v_c_ctx, v_ada_w, v_ada_b, v_norm1_g, v_w_in, v_q_norm_g, v_k_norm_g, v_ret_log_decay, v_ret_norm_g, v_gla_gate_up, v_gla_gate_b, v_gla_norm_g, v_w_out, v_norm2_g, v_w_up, v_conv_w, v_conv_b, v_w_down, v_final_norm_g):
    w = dict(c_ctx=c_ctx, ada_w=ada_w, ada_b=ada_b, norm1_g=norm1_g, w_in=w_in, q_norm_g=q_norm_g, k_norm_g=k_norm_g,
             ret_log_decay=ret_log_decay, ret_norm_g=ret_norm_g, gla_gate_up=gla_gate_up, gla_gate_b=gla_gate_b,
             gla_norm_g=gla_norm_g, w_out=w_out, norm2_g=norm2_g, w_up=w_up, conv_w=conv_w, conv_b=conv_b, w_down=w_down,
             final_norm_g=final_norm_g)
    m = dict(c_ctx=m_c_ctx, ada_w=m_ada_w, ada_b=m_ada_b, norm1_g=m_norm1_g, w_in=m_w_in, q_norm_g=m_q_norm_g,
             k_norm_g=m_k_norm_g, ret_log_decay=m_ret_log_decay, ret_norm_g=m_ret_norm_g, gla_gate_up=m_gla_gate_up,
             gla_gate_b=m_gla_gate_b, gla_norm_g=m_gla_norm_g, w_out=m_w_out, norm2_g=m_norm2_g, w_up=m_w_up,
             conv_w=m_conv_w, conv_b=m_conv_b, w_down=m_w_down, final_norm_g=m_final_norm_g)
    v = dict(c_ctx=v_c_ctx, ada_w=v_ada_w, ada_b=v_ada_b, norm1_g=v_norm1_g, w_in=v_w_in, q_norm_g=v_q_norm_g,
             k_norm_g=v_k_norm_g, ret_log_decay=v_ret_log_decay, ret_norm_g=v_ret_norm_g, gla_gate_up=v_gla_gate_up,
             gla_gate_b=v_gla_gate_b, gla_norm_g=v_gla_norm_g, w_out=v_w_out, norm2_g=v_norm2_g, w_up=v_w_up,
             conv_w=v_conv_w, conv_b=v_conv_b, w_down=v_w_down, final_norm_g=v_final_norm_g)
    return _step(_cfg(), x, c, ctx, loss_target, w, m, v)
```

```python
import functools
import math
import types

import jax
import jax.numpy as jnp
import numpy as np
from jax import lax
from jax.experimental import pallas as pl
from jax.experimental.pallas import tpu as pltpu

F32 = jnp.float32
BF16 = jnp.bfloat16
HI = lax.Precision.HIGHEST

D_MODEL = 2048
SEQ = 2048
CTX_LEN = 256
GRID_W = 64
D_FF = 5632
DEPTH = 2
N_DEV = 8
HEAD_DIM = 128
ROPE_THETA = 10000.0
GLA_TAU = 16.0
GLA_RANK = 16
GLA_CHUNK = 64
GLA_SUB = 16
EPS = 1e-6
N_MOD = 6
ADAM_LR = 0.001
ADAM_B1 = 0.9
ADAM_B2 = 0.999
ADAM_EPS = 1e-08
ADAM_WD = 0.01
ADAM_STEP = 10
LANE = 128
VMEM_LIMIT = 56 * 1024 * 1024
NEG = -1e30


def _cfg():
    d = types.SimpleNamespace()
    d.D, d.L, d.LC, d.F = D_MODEL, SEQ, CTX_LEN, D_FF
    d.T = d.L + d.LC
    nm = d.D // HEAD_DIM
    d.HQ, d.HKV, d.HR, d.HG = nm // 2, nm // 8, nm // 4, nm // 4
    d.G = d.HQ // d.HKV
    w = dict(aq=d.HQ * 128, ak=d.HKV * 128, av=d.HKV * 128, rq=d.HR * 128, rk=d.HR * 128, rv=d.HR * 128,
             rg=d.HR * 128, gq=d.HG * 64, gk=d.HG * 64, gv=d.HG * 128, gr=d.HG * 128, ga=2 * GLA_RANK)
    off, o = {}, 0
    for k, v in w.items():
        off[k] = o
        o += v
    d.W, d.OFF, d.NIN = w, off, o
    d.NZ = -(-(off['ga'] + LANE) // 256) * 256
    d.NINS = d.NIN // N_DEV
    d.TM = math.gcd(d.LC, 128)
    d.TQ = math.gcd(d.LC, 256)
    return d


def _cp(sem=None):
    return pltpu.CompilerParams(dimension_semantics=sem, vmem_limit_bytes=VMEM_LIMIT)


def _tile(n, target, mult=LANE):
    t = min(n, target)
    t -= t % mult
    while t > mult and n % t:
        t -= mult
    return t if t > 0 and n % t == 0 else n


_DN = {'nn': ((1,), (0,)), 'nt': ((1,), (1,)), 'tn': ((0,), (0,))}


def _mm(name, a, b, kind, M, N, K, out_dtype, tm=768, tn=768, tk=1024, a_spec=None, b_spec=None,
        out_shape=None, out_spec=None):
    tm, tn = _tile(M, tm, 128), _tile(N, tn, 128)
    tk = _tile(K, tk, 128)
    nk = K // tk

    def body(a_ref, b_ref, o_ref, acc):
        kk = pl.program_id(2)

        @pl.when(kk == 0)
        def _():
            acc[...] = jnp.zeros_like(acc)

        acc[...] += lax.dot_general(a_ref[...].astype(BF16), b_ref[...].astype(BF16), (_DN[kind], ((), ())),
                                    preferred_element_type=F32)

        @pl.when(kk == nk - 1)
        def _():
            o_ref[...] = acc[...].astype(o_ref.dtype)

    if a_spec is None:
        a_spec = pl.BlockSpec((tk, tm), lambda i, j, k: (k, i)) if kind == 'tn' else pl.BlockSpec((tm, tk), lambda i, j, k: (i, k))
    if b_spec is None:
        b_spec = pl.BlockSpec((tn, tk), lambda i, j, k: (j, k)) if kind == 'nt' else pl.BlockSpec((tk, tn), lambda i, j, k: (k, j))
    if out_spec is None:
        out_spec = pl.BlockSpec((tm, tn), lambda i, j, k: (i, j))
        out_shape = (M, N)
    return pl.pallas_call(
        body, name=name, grid=(M // tm, N // tn, nk), in_specs=[a_spec, b_spec], out_specs=out_spec,
        out_shape=jax.ShapeDtypeStruct(out_shape, out_dtype), scratch_shapes=[pltpu.VMEM((tm, tn), F32)],
        compiler_params=_cp(("parallel", "parallel", "arbitrary")))(a, b)


def _row_specs(cf, tm, tiles, params):
    nctx = cf.LC // tm
    specs = []
    for arr, w, colf, _ in tiles:
        specs.append(pl.BlockSpec((tm, w), functools.partial(lambda i, r, colf: (i, colf(r)), colf=colf)))
    for arr, kind, _ in params:
        nd = arr.ndim
        if kind == 'shared':
            specs.append(pl.BlockSpec(arr.shape, functools.partial(lambda i, r, nd: (0,) * nd, nd=nd)))
        else:
            specs.append(pl.BlockSpec((None,) + arr.shape[1:],
                                      functools.partial(lambda i, r, nd, nctx: (jnp.where(i >= nctx, 1, 0),) + (0,) * (nd - 1), nd=nd, nctx=nctx)))
    return specs


def _row_fwd(cf, name, f, tiles, params, outs, tm, nrep=1):
    nt, npar = len(tiles), len(params)

    def body(*refs):
        tv = [r[...].astype(F32) for r in refs[:nt]]
        pv = [r[...] for r in refs[nt:nt + npar]]
        res = f(*tv, *pv)
        for o, v in zip(refs[nt + npar:], res):
            o[...] = v.astype(o.dtype)

    out_specs = [pl.BlockSpec((tm, w), functools.partial(lambda i, r, colf: (i, colf(r)), colf=colf)) for w, colf, _, _ in outs]
    out_shape = [jax.ShapeDtypeStruct((cf.T, tw), dt) for _, _, tw, dt in outs]
    return pl.pallas_call(
        body, name=name, grid=(cf.T // tm, nrep), in_specs=_row_specs(cf, tm, tiles, params), out_specs=out_specs,
        out_shape=out_shape, compiler_params=_cp(("arbitrary", "arbitrary")))(*[t[0] for t in tiles], *[p[0] for p in params])


def _row_bwd(cf, name, f, tiles, params, cts, tgrads, tm, nrep=1):
    nt, npar, nc = len(tiles), len(params), len(cts)
    tdiff = [k for k, t in enumerate(tiles) if t[3]]
    pdiff = [k for k, p in enumerate(params) if p[2]]
    nctx = cf.LC // tm

    def body(*refs):
        i, r = pl.program_id(0), pl.program_id(1)
        tv = [x[...].astype(F32) for x in refs[:nt]]
        pv = [x[...] for x in refs[nt:nt + npar]]
        cv = tuple(x[...].astype(F32) for x in refs[nt + npar:nt + npar + nc])
        outs = refs[nt + npar + nc:]

        def g(*diff):
            tv2, pv2 = list(tv), list(pv)
            for k, v in zip(tdiff, diff[:len(tdiff)]):
                tv2[k] = v
            for k, v in zip(pdiff, diff[len(tdiff):]):
                pv2[k] = v
            return tuple(f(*tv2, *pv2))

        _, vjp_fn = jax.vjp(g, *[tv[k] for k in tdiff], *[pv[k] for k in pdiff])
        grads = vjp_fn(cv)
        for o, gv in zip(outs[:len(tdiff)], grads[:len(tdiff)]):
            o[...] = gv.astype(o.dtype)
        for n_, (o, gv) in enumerate(zip(outs[len(tdiff):], grads[len(tdiff):])):
            if params[pdiff[n_]][1] == 'shared':
                first = jnp.logical_and(i == 0, r == 0)
            else:
                first = jnp.logical_and(jnp.logical_or(i == 0, i == nctx), r == 0)

            @pl.when(first)
            def _():
                o[...] = gv

            @pl.when(jnp.logical_not(first))
            def _():
                o[...] += gv

    in_specs = _row_specs(cf, tm, tiles, params)
    in_specs += [pl.BlockSpec((tm, w), functools.partial(lambda i, r, colf: (i, colf(r)), colf=colf)) for _, w, colf in cts]
    out_specs = [pl.BlockSpec((tm, w), functools.partial(lambda i, r, colf: (i, colf(r)), colf=colf)) for w, colf, _, _ in tgrads]
    out_shape = [jax.ShapeDtypeStruct((cf.T, tw), dt) for _, _, tw, dt in tgrads]
    pspecs = _row_specs(cf, tm, [], [params[k] for k in pdiff])
    out_specs += pspecs
    out_shape += [jax.ShapeDtypeStruct(params[k][0].shape, F32) for k in pdiff]
    res = pl.pallas_call(
        body, name=name, grid=(cf.T // tm, nrep), in_specs=in_specs, out_specs=out_specs, out_shape=out_shape,
        compiler_params=_cp(("arbitrary", "arbitrary")))(*[t[0] for t in tiles], *[p[0] for p in params], *[c[0] for c in cts])
    return res[:len(tdiff)], res[len(tdiff):]


def _c0(r):
    return 0


def _col(base):
    return lambda r: base + r


def _rms(x, g):
    return x * lax.rsqrt(jnp.mean(x * x, axis=-1, keepdims=True) + EPS) * g


def _sigmoid(x):
    return 1.0 / (1.0 + jnp.exp(-x))


def _silu(x):
    return x * _sigmoid(x)


def _f_norm_mod(x, g, sh, sc):
    return (_rms(x, g) * (1 + sc) + sh,)


def _f_resid_norm_mod(x, y, gate, g, sh, sc):
    x1 = x + gate * y
    return (x1, _rms(x1, g) * (1 + sc) + sh)


@jax.custom_vjp
def _swap_halves(t):
    return pltpu.roll(t, HEAD_DIM // 2, axis=1)


def _swap_fwd(t):
    return _swap_halves(t), None


def _swap_bwd(_, g):
    return (pltpu.roll(g, HEAD_DIM // 2, axis=1),)


_swap_halves.defvjp(_swap_fwd, _swap_bwd)


def _rope(t, cs, sn):
    return t * cs + _swap_halves(t) * sn


def _f_prep_norm(t, cs, sn, g):
    return (_rope(_rms(t, g), cs, sn),)


def _f_prep_plain(t, cs, sn):
    return (_rope(t, cs, sn),)


def _f_prep_scaled(t, cs, sn):
    return (_rope(t * (HEAD_DIM ** -0.5), cs, sn),)


def _log_sigmoid(x):
    return jnp.minimum(x, 0.0) - jnp.log(1.0 + jnp.exp(-jnp.abs(x)))


def _f_gates(ga, gf, gb, bf, bb):
    gab = ga.astype(BF16)
    lf = jnp.dot(gab, gf.astype(BF16), preferred_element_type=F32) + bf
    lb = jnp.dot(gab, gb.astype(BF16), preferred_element_type=F32) + bb
    return (_log_sigmoid(lf) / GLA_TAU, _log_sigmoid(lb) / GLA_TAU)


def _f_gated_norm(o, g, n):
    return (_rms(o, n) * _silu(g),)


def _f_first(x, g, sh, sc):
    return (x, _rms(x, g) * (1 + sc) + sh)


def _loss_grad(cf, x1, yff, gate, gfin, tgt):
    tm, T, D = cf.TM, cf.T, cf.D
    nctx = cf.LC // tm

    def lossf(x1v, yv, gt, gf, tg):
        y = _rms(x1v + gt * yv, gf)
        e = y - tg
        return 0.5 * jnp.sum(jnp.mean(e * e, axis=-1, keepdims=True), axis=0, keepdims=True)

    def body(x1_ref, y_ref, gt_ref, gf_ref, tg_ref, dx_ref, dy_ref, dgt_ref, dgf_ref, ls_ref):
        i = pl.program_id(0)
        lat = (i >= nctx).astype(F32)
        val, vjp_fn = jax.vjp(lossf, x1_ref[...], y_ref[...].astype(F32), gt_ref[...], gf_ref[...], tg_ref[...])
        dx, dy, dgt, dgf, _ = vjp_fn(jnp.ones((1, 1), F32) * lat)
        dx_ref[...] = dx
        dy_ref[...] = dy.astype(dy_ref.dtype)
        first_s = jnp.logical_or(i == 0, i == nctx)

        @pl.when(first_s)
        def _():
            dgt_ref[...] = dgt

        @pl.when(jnp.logical_not(first_s))
        def _():
            dgt_ref[...] += dgt

        @pl.when(i == 0)
        def _():
            dgf_ref[...] = dgf
            ls_ref[...] = jnp.zeros_like(ls_ref) + val * lat

        @pl.when(i != 0)
        def _():
            dgf_ref[...] += dgf
            ls_ref[...] += val * lat

    row = pl.BlockSpec((tm, D), lambda i: (i, 0))
    strm = pl.BlockSpec((None, 1, D), lambda i: (jnp.where(i >= nctx, 1, 0), 0, 0))
    one = pl.BlockSpec((1, D), lambda i: (0, 0))
    return pl.pallas_call(
        body, name="loss_grad", grid=(T // tm,), in_specs=[row, row, strm, one, row],
        out_specs=[row, row, strm, one, pl.BlockSpec((8, LANE), lambda i: (0, 0))],
        out_shape=[jax.ShapeDtypeStruct((T, D), F32), jax.ShapeDtypeStruct((T, D), BF16),
                   jax.ShapeDtypeStruct((2, 1, D), F32), jax.ShapeDtypeStruct((1, D), F32),
                   jax.ShapeDtypeStruct((8, LANE), F32)],
        compiler_params=_cp(("arbitrary",)))(x1, yff, gate, gfin, tgt)


def _att_mask(cf, i, tq):
    col = lax.broadcasted_iota(jnp.int32, (tq, cf.T), 1)
    return jnp.logical_or(col < cf.LC, i >= cf.LC // tq)


def _att_probs(q, k, mask):
    s = lax.dot_general(q, k, (_DN['nt'], ((), ())), preferred_element_type=F32) * (HEAD_DIM ** -0.5)
    s = jnp.where(mask, s, NEG)
    e = jnp.exp(s - jnp.max(s, axis=-1, keepdims=True))
    return e / jnp.sum(e, axis=-1, keepdims=True)


def _att_fwd(cf, q, k, z):
    tq, T, G = cf.TQ, cf.T, cf.G
    vb = cf.OFF['av'] // LANE

    def body(q_ref, k_ref, v_ref, o_ref):
        mask = _att_mask(cf, pl.program_id(1), tq)
        kv, vv = k_ref[...], v_ref[...].astype(BF16)
        for j in range(G):
            p = _att_probs(q_ref[:, j * LANE:(j + 1) * LANE], kv, mask)
            o_ref[:, j * LANE:(j + 1) * LANE] = jnp.dot(p.astype(BF16), vv, preferred_element_type=F32).astype(o_ref.dtype)

    return pl.pallas_call(
        body, name="att_fwd", grid=(cf.HKV, T // tq),
        in_specs=[pl.BlockSpec((tq, G * LANE), lambda g, i: (i, g)), pl.BlockSpec((T, LANE), lambda g, i: (0, g)),
                  pl.BlockSpec((T, LANE), lambda g, i: (0, vb + g))],
        out_specs=pl.BlockSpec((tq, G * LANE), lambda g, i: (i, g)),
        out_shape=jax.ShapeDtypeStruct((T, cf.HQ * LANE), BF16), compiler_params=_cp(("arbitrary", "arbitrary")))(q, k, z)


def _att_bwd(cf, q, k, z, dcat):
    tq, T, G = cf.TQ, cf.T, cf.G
    vb = cf.OFF['av'] // LANE
    sc = HEAD_DIM ** -0.5

    def body(q_ref, k_ref, v_ref, do_ref, dq_ref, dk_ref, dv_ref):
        i = pl.program_id(1)
        mask = _att_mask(cf, i, tq)
        kv, vv = k_ref[...], v_ref[...].astype(BF16)
        dk = jnp.zeros((T, LANE), F32)
        dv = jnp.zeros((T, LANE), F32)
        for j in range(G):
            qj = q_ref[:, j * LANE:(j + 1) * LANE]
            do = do_ref[:, j * LANE:(j + 1) * LANE]
            p = _att_probs(qj, kv, mask)
            dv += lax.dot_general(p.astype(BF16), do, (_DN['tn'], ((), ())), preferred_element_type=F32)
            dp = lax.dot_general(do, vv, (_DN['nt'], ((), ())), preferred_element_type=F32)
            ds = p * (dp - jnp.sum(dp * p, axis=-1, keepdims=True)) * sc
            dsb = ds.astype(BF16)
            dq_ref[:, j * LANE:(j + 1) * LANE] = jnp.dot(dsb, kv, preferred_element_type=F32)
            dk += lax.dot_general(dsb, qj, (_DN['tn'], ((), ())), preferred_element_type=F32)

        @pl.when(i == 0)
        def _():
            dk_ref[...] = dk
            dv_ref[...] = dv

        @pl.when(i != 0)
        def _():
            dk_ref[...] += dk
            dv_ref[...] += dv

    qs = pl.BlockSpec((tq, G * LANE), lambda g, i: (i, g))
    ks = pl.BlockSpec((T, LANE), lambda g, i: (0, g))
    return pl.pallas_call(
        body, name="att_bwd", grid=(cf.HKV, T // tq),
        in_specs=[qs, ks, pl.BlockSpec((T, LANE), lambda g, i: (0, vb + g)), qs],
        out_specs=[qs, ks, ks],
        out_shape=[jax.ShapeDtypeStruct((T, cf.HQ * LANE), F32), jax.ShapeDtypeStruct((T, cf.HKV * LANE), F32),
                   jax.ShapeDtypeStruct((T, cf.HKV * LANE), F32)],
        compiler_params=_cp(("arbitrary", "arbitrary")))(q, k, z, dcat)


def _ret_masks(cf, i, tq, lgf, lgb):
    T, LC = cf.T, cf.LC
    row = lax.broadcasted_iota(jnp.int32, (tq, T), 0) + i * tq
    col = lax.broadcasted_iota(jnp.int32, (tq, T), 1)

    def pb(n):
        return jnp.where(n < LC, LC - 1 - n, T + LC - 1 - n)

    relf = row - col
    relb = pb(row) - pb(col)
    okf, okb = relf >= 0, relb >= 0
    rf = jnp.where(okf, relf, 0).astype(F32)
    rb = jnp.where(okb, relb, 0).astype(F32)
    mf = jnp.where(okf, jnp.exp(lgf * rf), 0.0)
    mb = jnp.where(okb, jnp.exp(lgb * rb), 0.0)
    return mf, mb, rf, rb


def _ret_fwd(cf, q, k, z, lg):
    tq, T = cf.TQ, cf.T
    vb = cf.OFF['rv'] // LANE

    def body(lg_ref, q_ref, k_ref, v_ref, o_ref):
        h, i = pl.program_id(0), pl.program_id(1)
        mf, mb, _, _ = _ret_masks(cf, i, tq, lg_ref[0, h], lg_ref[1, h])
        a = lax.dot_general(q_ref[...], k_ref[...], (_DN['nt'], ((), ())), preferred_element_type=F32)
        p = (a * (mf + mb)).astype(BF16)
        o_ref[...] = jnp.dot(p, v_ref[...].astype(BF16), preferred_element_type=F32)

    return pl.pallas_call(
        body, name="ret_fwd", grid=(cf.HR, T // tq),
        in_specs=[pl.BlockSpec(memory_space=pltpu.SMEM), pl.BlockSpec((tq, LANE), lambda h, i: (i, h)),
                  pl.BlockSpec((T, LANE), lambda h, i: (0, h)), pl.BlockSpec((T, LANE), lambda h, i: (0, vb + h))],
        out_specs=pl.BlockSpec((tq, LANE), lambda h, i: (i, h)),
        out_shape=jax.ShapeDtypeStruct((T, cf.HR * LANE), F32), compiler_params=_cp(("arbitrary", "arbitrary")))(lg, q, k, z)


def _ret_bwd(cf, q, k, z, lg, do):
    tq, T = cf.TQ, cf.T
    vb = cf.OFF['rv'] // LANE

    def body(lg_ref, q_ref, k_ref, v_ref, do_ref, dq_ref, dk_ref, dv_ref, dlg_ref):
        h, i = pl.program_id(0), pl.program_id(1)
        mf, mb, rf, rb = _ret_masks(cf, i, tq, lg_ref[0, h], lg_ref[1, h])
        qv, kv, vv = q_ref[...], k_ref[...], v_ref[...].astype(BF16)
        dob = do_ref[...].astype(BF16)
        a = lax.dot_general(qv, kv, (_DN['nt'], ((), ())), preferred_element_type=F32)
        m = mf + mb
        p = (a * m).astype(BF16)
        dv = lax.dot_general(p, dob, (_DN['tn'], ((), ())), preferred_element_type=F32)
        dp = lax.dot_general(dob, vv, (_DN['nt'], ((), ())), preferred_element_type=F32)
        da = (dp * m).astype(BF16)
        dq_ref[...] = jnp.dot(da, kv, preferred_element_type=F32)
        dk = lax.dot_general(da, qv, (_DN['tn'], ((), ())), preferred_element_type=F32)
        dm = dp * a
        dlf = jnp.sum(jnp.sum(dm * mf * rf, axis=-1, keepdims=True), axis=0, keepdims=True)
        dlb = jnp.sum(jnp.sum(dm * mb * rb, axis=-1, keepdims=True), axis=0, keepdims=True)
        rid = lax.broadcasted_iota(jnp.int32, (8, LANE), 0)
        dl = jnp.where(rid == 0, dlf, jnp.where(rid == 1, dlb, 0.0))

        @pl.when(i == 0)
        def _():
            dk_ref[...] = dk
            dv_ref[...] = dv
            dlg_ref[...] = dl

        @pl.when(i != 0)
        def _():
            dk_ref[...] += dk
            dv_ref[...] += dv
            dlg_ref[...] += dl

    qs = pl.BlockSpec((tq, LANE), lambda h, i: (i, h))
    ks = pl.BlockSpec((T, LANE), lambda h, i: (0, h))
    return pl.pallas_call(
        body, name="ret_bwd", grid=(cf.HR, T // tq),
        in_specs=[pl.BlockSpec(memory_space=pltpu.SMEM), qs, ks, pl.BlockSpec((T, LANE), lambda h, i: (0, vb + h)), qs],
        out_specs=[qs, ks, ks, pl.BlockSpec((None, 8, LANE), lambda h, i: (h, 0, 0))],
        out_shape=[jax.ShapeDtypeStruct((T, cf.HR * LANE), F32)] * 3 + [jax.ShapeDtypeStruct((cf.HR, 8, LANE), F32)],
        compiler_params=_cp(("arbitrary", "arbitrary")))(lg, q, k, z, do)


def _gla_consts(d):
    C, SB = GLA_CHUNK, GLA_SUB
    nsb = C // SB
    r = lax.broadcasted_iota(jnp.int32, (C, C), 0)
    m = lax.broadcasted_iota(jnp.int32, (C, C), 1)
    allowed = (m <= r) if d == 0 else (m >= r)
    blocks = [allowed]
    vis = []
    rr = lax.broadcasted_iota(jnp.int32, (C, LANE), 0)
    for b in range(nsb):
        blocks.append((m < SB * b) if d == 0 else (m >= SB * (b + 1)))
        vis.append((rr < SB * (b + 1)) if d == 0 else (rr >= SB * b))
    cm = jnp.concatenate([x.astype(F32) for x in blocks] + [jnp.ones((LANE, C), F32)], axis=0)
    return cm, allowed, vis


def _gla_step(q, k, v, la, st, lmask, cm, allowed, vis):
    C, SB = GLA_CHUNK, GLA_SUB
    nsb = C // SB
    qs = q * lmask * ((HEAD_DIM // 2) ** -0.5)
    ks = k * lmask
    cums = jnp.dot(cm, la, precision=HI, preferred_element_type=F32)
    cum = cums[0:C]
    tot = cums[(1 + nsb) * C:]
    rows = []
    for b in range(nsb):
        ref = cums[(1 + b) * C:(2 + b) * C]
        sl = slice(b * SB, (b + 1) * SB)
        qt = qs[sl] * jnp.exp(cum[sl] - ref[sl])
        kt = jnp.where(vis[b], ks * jnp.exp(jnp.where(vis[b], ref - cum, 0.0)), 0.0)
        rows.append(lax.dot_general(qt, kt, (_DN['nt'], ((), ())), precision=HI, preferred_element_type=F32))
    att = jnp.where(allowed, jnp.concatenate(rows, axis=0), 0.0)
    o = jnp.dot(att.astype(BF16), v.astype(BF16), preferred_element_type=F32)
    o += lax.dot_general((qs * jnp.exp(cum)).astype(BF16), st.astype(BF16), (_DN['nt'], ((), ())), preferred_element_type=F32)
    kd = (ks * jnp.exp(tot[0:C] - cum)).astype(BF16)
    st_new = st * jnp.exp(tot) + lax.dot_general(v.astype(BF16), kd, (_DN['tn'], ((), ())), preferred_element_type=F32)
    return o, st_new


def _gla_chunk_id(cf, s, d):
    if d == 0:
        return s
    nct, nc = cf.LC // GLA_CHUNK, cf.T // GLA_CHUNK
    return jnp.where(s < nct, nct - 1 - s, nc + nct - 1 - s)


def _gla_lmask(h):
    return (lax.broadcasted_iota(jnp.int32, (1, LANE), 1) // (LANE // 2) == h).astype(F32)


def _gla_fwd(cf, z, laf, lab):
    T, C = cf.T, GLA_CHUNK
    nc = T // C
    qb, kb, vb = cf.OFF['gq'] // LANE, cf.OFF['gk'] // LANE, cf.OFF['gv'] // (2 * LANE)
    chains = [(h, d) for h in range(2) for d in range(2)]

    def body(q_ref, k_ref, v_ref, laf_ref, lab_ref, o_ref, sst_ref, st_scr):
        consts = [_gla_consts(d) for d in range(2)]
        lmasks = [_gla_lmask(h) for h in range(2)]
        st_scr[...] = jnp.zeros_like(st_scr)
        o_ref[...] = jnp.zeros_like(o_ref)

        def loop(s, carry):
            for ci, (h, d) in enumerate(chains):
                cid = _gla_chunk_id(cf, s, d)
                rows = pl.ds(pl.multiple_of(cid * C, C), C)
                cols = slice(h * LANE, (h + 1) * LANE)
                la = (laf_ref if d == 0 else lab_ref)[rows, :]
                st = st_scr[ci]
                sst_ref[ci, cid] = st
                o, stn = _gla_step(q_ref[rows, :], k_ref[rows, :], v_ref[rows, cols], la, st, lmasks[h], *consts[d])
                st_scr[ci] = stn
                o_ref[rows, cols] += o
            return carry

        lax.fori_loop(0, nc, loop, 0)

    pair = lambda b: pl.BlockSpec((T, LANE), functools.partial(lambda p, b: (0, b + p), b=b))
    return pl.pallas_call(
        body, name="gla_fwd", grid=(cf.HG // 2,),
        in_specs=[pair(qb), pair(kb), pl.BlockSpec((T, 2 * LANE), lambda p: (0, vb + p)), pair(0), pair(0)],
        out_specs=[pl.BlockSpec((T, 2 * LANE), lambda p: (0, p)), pl.BlockSpec((4, nc, LANE, LANE), lambda p: (p, 0, 0, 0))],
        out_shape=[jax.ShapeDtypeStruct((T, cf.HG * LANE), F32), jax.ShapeDtypeStruct((cf.HG * 2, nc, LANE, LANE), F32)],
        scratch_shapes=[pltpu.VMEM((4, LANE, LANE), F32)],
        compiler_params=_cp(("arbitrary",)))(z, z, z, laf, lab)


def _gla_bwd(cf, z, laf, lab, sst, do):
    T, C = cf.T, GLA_CHUNK
    nc = T // C
    qb, kb, vb = cf.OFF['gq'] // LANE, cf.OFF['gk'] // LANE, cf.OFF['gv'] // LANE

    def body(q_ref, k_ref, v_ref, laf_ref, lab_ref, sst_ref, do_ref, dq_ref, dk_ref, dv_ref, dlf_ref, dlb_ref, dst_scr):
        h = pl.program_id(1)
        consts = [_gla_consts(d) for d in range(2)]
        lmask = (lax.broadcasted_iota(jnp.int32, (1, LANE), 1) // (LANE // 2) == h).astype(F32)
        dst_scr[...] = jnp.zeros_like(dst_scr)
        dv_ref[...] = jnp.zeros_like(dv_ref)

        @pl.when(h == 0)
        def _():
            dq_ref[...] = jnp.zeros_like(dq_ref)
            dk_ref[...] = jnp.zeros_like(dk_ref)
            dlf_ref[...] = jnp.zeros_like(dlf_ref)
            dlb_ref[...] = jnp.zeros_like(dlb_ref)

        def loop(t, carry):
            for d in range(2):
                cid = _gla_chunk_id(cf, nc - 1 - t, d)
                rows = pl.ds(pl.multiple_of(cid * C, C), C)
                la_ref, dl_ref = (laf_ref, dlf_ref) if d == 0 else (lab_ref, dlb_ref)
                cm, allowed, vis = consts[d]
                step = functools.partial(_gla_step, lmask=lmask, cm=cm, allowed=allowed, vis=vis)
                _, vjp_fn = jax.vjp(step, q_ref[rows, :], k_ref[rows, :], v_ref[rows, :], la_ref[rows, :], sst_ref[d, cid])
                dq, dk, dv, dla, dst = vjp_fn((do_ref[rows, :], dst_scr[d]))
                dst_scr[d] = dst
                dq_ref[rows, :] += dq
                dk_ref[rows, :] += dk
                dv_ref[rows, :] += dv
                dl_ref[rows, :] += dla
            return carry

        lax.fori_loop(0, nc, loop, 0)

    pair = lambda b: pl.BlockSpec((T, LANE), functools.partial(lambda p, h, b: (0, b + p), b=b))
    head = lambda b: pl.BlockSpec((T, LANE), functools.partial(lambda p, h, b: (0, b + 2 * p + h), b=b))
    npair = cf.HG // 2
    return pl.pallas_call(
        body, name="gla_bwd", grid=(npair, 2),
        in_specs=[pair(qb), pair(kb), head(vb), pair(0), pair(0),
                  pl.BlockSpec((2, nc, LANE, LANE), lambda p, h: (2 * p + h, 0, 0, 0)), head(0)],
        out_specs=[pair(0), pair(0), head(0), pair(0), pair(0)],
        out_shape=[jax.ShapeDtypeStruct((T, npair * LANE), F32), jax.ShapeDtypeStruct((T, npair * LANE), F32),
                   jax.ShapeDtypeStruct((T, cf.HG * LANE), F32), jax.ShapeDtypeStruct((T, npair * LANE), F32),
                   jax.ShapeDtypeStruct((T, npair * LANE), F32)],
        scratch_shapes=[pltpu.VMEM((2, LANE, LANE), F32)],
        compiler_params=_cp(("arbitrary", "arbitrary")))(z, z, z, laf, lab, sst, do)


def _conv_parts(cf, a, w_ref):
    T, LC = cf.T, cf.LC
    rid = lax.broadcasted_iota(jnp.int32, a.shape, 0)
    first = jnp.logical_or(rid == 0, rid == LC)
    last = jnp.logical_or(rid == LC - 1, rid == T - 1)
    ap = jnp.where(first, 0.0, pltpu.roll(a, 1, axis=0))
    an = jnp.where(last, 0.0, pltpu.roll(a, T - 1, axis=0))
    w0, w1, w2, b = w_ref[0:1, :], w_ref[1:2, :], w_ref[2:3, :], w_ref[3:4, :]
    ac = ap * w0 + a * w1 + an * w2 + b
    return ap, an, ac, first, last, (w0, w1, w2)


def _conv_fwd(cf, u, wb):
    T, Fd = cf.T, cf.F
    tc = _tile(Fd, 512)
    nj = Fd // tc

    def body(a_ref, v_ref, w_ref, t_ref):
        _, _, ac, _, _, _ = _conv_parts(cf, a_ref[...], w_ref)
        t_ref[...] = (_silu(ac) * v_ref[...]).astype(t_ref.dtype)

    return pl.pallas_call(
        body, name="conv_fwd", grid=(nj,),
        in_specs=[pl.BlockSpec((T, tc), lambda j: (0, j)), pl.BlockSpec((T, tc), lambda j: (0, nj + j)),
                  pl.BlockSpec((8, tc), lambda j: (0, j))],
        out_specs=pl.BlockSpec((T, tc), lambda j: (0, j)), out_shape=jax.ShapeDtypeStruct((T, Fd), BF16),
        compiler_params=_cp(("parallel",)))(u, u, wb)


def _conv_bwd(cf, u, wb, dt):
    T, Fd = cf.T, cf.F
    tc = _tile(Fd, 256)
    nj = Fd // tc

    def body(a_ref, v_ref, w_ref, dt_ref, da_ref, dv_ref, dw_ref):
        a, v, dtv = a_ref[...], v_ref[...], dt_ref[...].astype(F32)
        ap, an, ac, first, last, (w0, w1, w2) = _conv_parts(cf, a, w_ref)
        sg = _sigmoid(ac)
        dv_ref[...] = (dtv * ac * sg).astype(dv_ref.dtype)
        dac = dtv * v * (sg * (1.0 + ac * (1.0 - sg)))
        from_next = pltpu.roll(jnp.where(first, 0.0, dac), T - 1, axis=0)
        from_prev = pltpu.roll(jnp.where(last, 0.0, dac), 1, axis=0)
        da_ref[...] = (dac * w1 + from_next * w0 + from_prev * w2).astype(da_ref.dtype)
        rows = [jnp.sum(dac * ap, axis=0, keepdims=True), jnp.sum(dac * a, axis=0, keepdims=True),
                jnp.sum(dac * an, axis=0, keepdims=True), jnp.sum(dac, axis=0, keepdims=True)]
        rid = lax.broadcasted_iota(jnp.int32, (8, tc), 0)
        dw = jnp.zeros((8, tc), F32)
        for n_, rw in enumerate(rows):
            dw = jnp.where(rid == n_, rw, dw)
        dw_ref[...] = dw

    col = pl.BlockSpec((T, tc), lambda j: (0, j))
    return pl.pallas_call(
        body, name="conv_bwd", grid=(nj,),
        in_specs=[col, pl.BlockSpec((T, tc), lambda j: (0, nj + j)), pl.BlockSpec((8, tc), lambda j: (0, j)), col],
        out_specs=[col, col, pl.BlockSpec((8, tc), lambda j: (0, j))],
        out_shape=[jax.ShapeDtypeStruct((T, Fd), BF16), jax.ShapeDtypeStruct((T, Fd), BF16), jax.ShapeDtypeStruct((8, Fd), F32)],
        compiler_params=_cp(("parallel",)))(u, u, wb, dt)


def _me():
    x, y, c = lax.axis_index("x"), lax.axis_index("y"), lax.axis_index("c")
    return x, y, c, 4 * x + 2 * y + c


def _peer(x, y, c, k):
    px = 1 - x if (k >> 2) & 1 else x
    py = 1 - y if (k >> 1) & 1 else y
    pc = 1 - c if k & 1 else c
    return (px, py, pc), 4 * px + 2 * py + pc


def _rcopy(src, dst, ss, rs, tgt):
    return pltpu.make_async_remote_copy(src_ref=src, dst_ref=dst, send_sem=ss, recv_sem=rs, device_id=tgt,
                                        device_id_type=pl.DeviceIdType.MESH)


def _ag_small(name, v):
    R, Cc = v.shape

    def body(v_ref, o_ref, ssem, rsem, lsem):
        x, y, c, me = _me()
        loc = pltpu.make_async_copy(v_ref, o_ref.at[me], lsem)
        loc.start()
        sends = []
        for k in range(1, N_DEV):
            tgt, _ = _peer(x, y, c, k)
            cp = _rcopy(v_ref, o_ref.at[me], ssem.at[k - 1], rsem.at[k - 1], tgt)
            cp.start()
            sends.append(cp)
        for k in range(1, N_DEV):
            tgt, pi = _peer(x, y, c, k)
            _rcopy(v_ref, o_ref.at[pi], ssem.at[k - 1], rsem.at[k - 1], tgt).wait_recv()
        for cp in sends:
            cp.wait_send()
        loc.wait()

    vm = pl.BlockSpec(memory_space=pltpu.VMEM)
    return pl.pallas_call(
        body, name=name, in_specs=[vm], out_specs=vm, out_shape=jax.ShapeDtypeStruct((N_DEV, R, Cc), v.dtype),
        scratch_shapes=[pltpu.SemaphoreType.DMA((N_DEV - 1,)), pltpu.SemaphoreType.DMA((N_DEV - 1,)), pltpu.SemaphoreType.DMA],
        compiler_params=pltpu.CompilerParams(vmem_limit_bytes=VMEM_LIMIT))(v)


_HBM = pl.BlockSpec(memory_space=pltpu.HBM)
_SEM = pl.BlockSpec(memory_space=pltpu.SEMAPHORE)
_EFFECT = pltpu.SideEffectType.DATAFLOW_SIDE_EFFECTING
_KINDS = ['in', 'out', 'up', 'down']


def _hbm(a):
    return pltpu.with_memory_space_constraint(a, pltpu.HBM)


def _shard_shape(cf, kind):
    D, Fd = cf.D, cf.F
    return {'in': (D, cf.NINS), 'out': (D // N_DEV, D), 'up': (D, 2 * Fd // N_DEV), 'down': (Fd // N_DEV, D)}[kind]


def _whole_shape(cf, kind):
    D, Fd = cf.D, cf.F
    return {'in': (N_DEV, D, cf.NINS), 'out': (D, D), 'up': (D, 2 * Fd), 'down': (Fd, D)}[kind]


def _part(ref, cf, kind, idx):
    r, cdim = _shard_shape(cf, kind)
    if kind == 'in':
        return ref.at[idx]
    if kind == 'up':
        return ref.at[:, pl.ds(pl.multiple_of(idx * cdim, cdim), cdim)]
    return ref.at[pl.ds(pl.multiple_of(idx * r, r), r), :]


def _ag_start(cf, shards, after):
    npc = DEPTH * len(_KINDS)
    nio = len(_KINDS) + npc

    def body(*refs):
        srcs, lands = refs[:len(_KINDS)], refs[len(_KINDS):nio]
        ssems, rsems = refs[2 * nio + 1:2 * nio + 1 + npc], refs[2 * nio + 1 + npc:2 * nio + 1 + 2 * npc]
        token, lsem = refs[2 * nio + 1 + 2 * npc], refs[2 * nio + 2 + 2 * npc]
        x, y, c, me = _me()
        pieces = [(l * len(_KINDS) + n, srcs[n].at[l], kind) for l in range(DEPTH) for n, kind in enumerate(_KINDS)]
        locs = [pltpu.make_async_copy(src, _part(lands[p], cf, kind, me), lsem.at[p]) for p, src, kind in pieces]
        for cp in locs:
            cp.start()
        for cp in locs:
            cp.wait()
        for p, src, kind in pieces:
            for k in range(1, N_DEV):
                tgt, _ = _peer(x, y, c, k)
                _rcopy(src, _part(lands[p], cf, kind, me), ssems[p].at[k - 1], rsems[p].at[k - 1], tgt).start()
        token[...] = jnp.zeros_like(token)

    land_shapes = [_whole_shape(cf, kind) for _ in range(DEPTH) for kind in _KINDS]
    out_shape = [pltpu.HBM(s.shape, BF16) for s in shards] + [pltpu.HBM(s, BF16) for s in land_shapes]
    out_shape += [pltpu.SemaphoreType.DMA((N_DEV - 1,))] * (2 * npc) + [jax.ShapeDtypeStruct((8, LANE), F32)]
    res = pl.pallas_call(
        body, name="ag_start", in_specs=[_HBM] * nio + [pl.BlockSpec(memory_space=pl.ANY)],
        out_specs=[_HBM] * nio + [_SEM] * (2 * npc) + [pl.BlockSpec(memory_space=pltpu.VMEM)],
        out_shape=out_shape, input_output_aliases={i: i for i in range(nio)}, scratch_shapes=[pltpu.SemaphoreType.DMA((npc,))],
        compiler_params=pltpu.CompilerParams(has_side_effects=_EFFECT))(
            *[_hbm(s) for s in shards], *[_hbm(lax.empty(s, BF16)) for s in land_shapes], after)
    return res[:len(_KINDS)], res[len(_KINDS):nio], res[nio:nio + npc], res[nio + npc:nio + 2 * npc], res[nio + 2 * npc]


def _ag_wait(cf, kind, l, src, land, ssem, rsem, after):
    def body(src_ref, land_ref, ssem_ref, rsem_ref, after_ref, src_out, land_out):
        x, y, c, me = _me()
        for k in range(1, N_DEV):
            tgt, pi = _peer(x, y, c, k)
            cp = _rcopy(src_ref.at[l], _part(land_ref, cf, kind, pi), ssem_ref.at[k - 1], rsem_ref.at[k - 1], tgt)
            cp.wait_send()
            cp.wait_recv()

    return pl.pallas_call(
        body, name="ag_wait_%s_%d" % (kind, l), in_specs=[_HBM, _HBM, _SEM, _SEM, pl.BlockSpec(memory_space=pl.ANY)],
        out_specs=[_HBM, _HBM], out_shape=[pltpu.HBM(src.shape, src.dtype), pltpu.HBM(land.shape, land.dtype)],
        input_output_aliases={0: 0, 1: 1}, compiler_params=pltpu.CompilerParams(has_side_effects=_EFFECT))(src, land, ssem, rsem, after)


def _rs_slab(ref, cf, kind, j):
    return ref.at[j] if kind in ('in', 'up') else _part(ref, cf, kind, j)


def _rs_start(cf, kind, l, g, after):
    def body(g_ref, recv_ref, after_ref, g_out, recv_out, ssem, rsem, token, lsem):
        x, y, c, me = _me()
        loc = pltpu.make_async_copy(_rs_slab(g_ref, cf, kind, me), recv_ref.at[me], lsem)
        loc.start()
        loc.wait()
        for k in range(1, N_DEV):
            tgt, pi = _peer(x, y, c, k)
            _rcopy(_rs_slab(g_ref, cf, kind, pi), recv_ref.at[me], ssem.at[k - 1], rsem.at[k - 1], tgt).start()
        token[...] = jnp.zeros_like(token)

    rshape = (N_DEV,) + _shard_shape(cf, kind)
    sems = pltpu.SemaphoreType.DMA((N_DEV - 1,))
    return pl.pallas_call(
        body, name="rs_start_%s_%d" % (kind, l), in_specs=[_HBM, _HBM, pl.BlockSpec(memory_space=pl.ANY)],
        out_specs=[_HBM, _HBM, _SEM, _SEM, pl.BlockSpec(memory_space=pltpu.VMEM)],
        out_shape=[pltpu.HBM(g.shape, BF16), pltpu.HBM(rshape, BF16), sems, sems, jax.ShapeDtypeStruct((8, LANE), F32)],
        input_output_aliases={0: 0, 1: 1}, scratch_shapes=[pltpu.SemaphoreType.DMA],
        compiler_params=pltpu.CompilerParams(has_side_effects=_EFFECT))(_hbm(g), _hbm(lax.empty(rshape, BF16)), after)


def _rs_wait(cf, kind, l, g, recv, ssem, rsem, after):
    def body(g_ref, recv_ref, ssem_ref, rsem_ref, after_ref, g_out, recv_out):
        x, y, c, me = _me()
        for k in range(1, N_DEV):
            tgt, pi = _peer(x, y, c, k)
            cp = _rcopy(_rs_slab(g_ref, cf, kind, pi), recv_ref.at[pi], ssem_ref.at[k - 1], rsem_ref.at[k - 1], tgt)
            cp.wait_send()
            cp.wait_recv()

    return pl.pallas_call(
        body, name="rs_wait_%s_%d" % (kind, l), in_specs=[_HBM, _HBM, _SEM, _SEM, pl.BlockSpec(memory_space=pl.ANY)],
        out_specs=[_HBM, _HBM], out_shape=[pltpu.HBM(g.shape, g.dtype), pltpu.HBM(recv.shape, recv.dtype)],
        input_output_aliases={0: 0, 1: 1}, compiler_params=pltpu.CompilerParams(has_side_effects=_EFFECT))(g, recv, ssem, rsem, after)[1]


def _adam_vals(w, g, m, v):
    m2 = ADAM_B1 * m + (1.0 - ADAM_B1) * g
    v2 = ADAM_B2 * v + (1.0 - ADAM_B2) * (g * g)
    mh = m2 / (1.0 - ADAM_B1 ** ADAM_STEP)
    vh = v2 / (1.0 - ADAM_B2 ** ADAM_STEP)
    return -ADAM_LR * (mh / (jnp.sqrt(vh) + ADAM_EPS) + ADAM_WD * w), m2, v2


def _row_tile(R, Cc, budget_elems):
    t = max(16, min(R, (budget_elems // max(Cc, 1)) // 16 * 16))
    while t > 16 and R % t:
        t -= 16
    return t if R % t == 0 else R


def _cast_bf16(name, w):
    Dp, R, Cc = w.shape
    tr = _row_tile(R, Cc, 1 << 20)

    def body(w_ref, o_ref):
        o_ref[...] = w_ref[...].astype(BF16)

    spec = pl.BlockSpec((None, tr, Cc), lambda l, i: (l, i, 0))
    return pl.pallas_call(body, name=name, grid=(Dp, R // tr), in_specs=[spec], out_specs=spec,
                          out_shape=jax.ShapeDtypeStruct(w.shape, BF16), compiler_params=_cp(("parallel", "parallel")))(w)


def _sum_adam(name, recvs, w, m, v):
    Dp, R, Cc = w.shape
    tr = _row_tile(R, Cc, 1 << 18)
    ni = R // tr

    def body(*refs):
        r_refs = refs[:Dp]
        w_ref, m_ref, v_ref, g_ref, d_ref, mo_ref, vo_ref = refs[Dp:]
        for layer in range(Dp):
            @pl.when(pl.program_id(0) == layer)
            def _():
                r_ref = r_refs[layer]
                g = r_ref[0].astype(F32)
                for s in range(1, N_DEV):
                    g = g + r_ref[s].astype(F32)
                dl, m2, v2 = _adam_vals(w_ref[...], g, m_ref[...], v_ref[...])
                g_ref[...] = g
                d_ref[...] = dl
                mo_ref[...] = m2
                vo_ref[...] = v2

    spec = pl.BlockSpec((None, tr, Cc), lambda l, i: (l, i, 0))
    rspecs = [pl.BlockSpec((N_DEV, tr, Cc), functools.partial(
        lambda l, i, layer: (0, jnp.where(l == layer, i, jnp.where(l > layer, ni - 1, 0)), 0), layer=layer)) for layer in range(Dp)]
    return pl.pallas_call(body, name=name, grid=(Dp, ni), in_specs=rspecs + [spec, spec, spec], out_specs=[spec] * 4,
                          out_shape=[jax.ShapeDtypeStruct(w.shape, F32)] * 4, compiler_params=_cp(("arbitrary", "arbitrary")))(*recvs, w, m, v)


def _adam(name, w, g, m, v):
    R, Cc = w.shape
    tr = _row_tile(R, Cc, 1 << 18)

    def body(w_ref, g_ref, m_ref, v_ref, d_ref, mo_ref, vo_ref):
        dl, m2, v2 = _adam_vals(w_ref[...], g_ref[...], m_ref[...], v_ref[...])
        d_ref[...] = dl
        mo_ref[...] = m2
        vo_ref[...] = v2

    spec = pl.BlockSpec((tr, Cc), lambda i: (i, 0))
    return pl.pallas_call(body, name=name, grid=(R // tr,), in_specs=[spec] * 4, out_specs=[spec] * 3,
                          out_shape=[jax.ShapeDtypeStruct(w.shape, F32)] * 3, compiler_params=_cp(("parallel",)))(w, g, m, v)


def _sum8(name, a):
    n, R, Cc = a.shape

    def body(a_ref, o_ref):
        s = a_ref[0]
        for k in range(1, n):
            s = s + a_ref[k]
        o_ref[...] = s

    return pl.pallas_call(body, name=name, in_specs=[pl.BlockSpec(memory_space=pltpu.VMEM)],
                          out_specs=pl.BlockSpec(memory_space=pltpu.VMEM), out_shape=jax.ShapeDtypeStruct((R, Cc), F32),
                          compiler_params=pltpu.CompilerParams(vmem_limit_bytes=VMEM_LIMIT))(a)


def _ada_fwd(cf, c9, ada_w):
    D = cf.D
    NS = ada_w.shape[-1]
    tk = _tile(D, 512)
    nk = D // tk

    def body(c_ref, w_ref, o_ref):
        kk = pl.program_id(1)
        s = _silu(c_ref[...]).astype(BF16)
        part = jnp.dot(s, w_ref[...].astype(BF16), preferred_element_type=F32)

        @pl.when(kk == 0)
        def _():
            o_ref[...] = part

        @pl.when(kk != 0)
        def _():
            o_ref[...] += part

    return pl.pallas_call(
        body, name="ada_fwd", grid=(DEPTH, nk),
        in_specs=[pl.BlockSpec((16, tk), lambda l, k: (0, k)), pl.BlockSpec((None, tk, NS), lambda l, k: (l, k, 0))],
        out_specs=pl.BlockSpec((None, 16, NS), lambda l, k: (l, 0, 0)),
        out_shape=jax.ShapeDtypeStruct((DEPTH, 16, NS), F32), compiler_params=_cp(("parallel", "arbitrary")))(c9, ada_w)


def _ada_bwd(cf, c9, ada_w, dm9):
    D = cf.D
    NS = ada_w.shape[-1]
    tk = _tile(D, 512)
    nk = D // tk

    def body(c_ref, w_ref, dm_ref, gw_ref, ds_ref):
        cv = c_ref[...]
        sg = _sigmoid(cv)
        dmb = dm_ref[...].astype(BF16)
        gw_ref[...] = lax.dot_general((cv * sg).astype(BF16), dmb, (_DN['tn'], ((), ())), preferred_element_type=F32)
        ds = lax.dot_general(dmb, w_ref[...].astype(BF16), (_DN['nt'], ((), ())), preferred_element_type=F32)
        ds_ref[...] = ds * (sg * (1.0 + cv * (1.0 - sg)))

    return pl.pallas_call(
        body, name="ada_bwd", grid=(DEPTH, nk),
        in_specs=[pl.BlockSpec((16, tk), lambda l, k: (0, k)), pl.BlockSpec((None, tk, NS), lambda l, k: (l, k, 0)),
                  pl.BlockSpec((None, 16, NS), lambda l, k: (l, 0, 0))],
        out_specs=[pl.BlockSpec((None, tk, NS), lambda l, k: (l, k, 0)), pl.BlockSpec((None, 16, tk), lambda l, k: (l, 0, k))],
        out_shape=[jax.ShapeDtypeStruct((DEPTH, D, NS), F32), jax.ShapeDtypeStruct((DEPTH, 16, D), F32)],
        compiler_params=_cp(("parallel", "parallel")))(c9, ada_w, dm9)


def _rope_tables(cf):
    L, LC = cf.L, cf.LC
    rows = L // GRID_W
    row = jnp.repeat(jnp.arange(rows, dtype=F32), GRID_W)
    col = jnp.tile(jnp.arange(GRID_W, dtype=F32), rows)
    nf = HEAD_DIM // 4
    inv = ROPE_THETA ** (-jnp.arange(nf, dtype=F32) / nf)
    ang = jnp.concatenate([row[:, None] * inv, col[:, None] * inv], axis=-1)
    cos, sin = jnp.cos(ang), jnp.sin(ang)
    cs = jnp.concatenate([jnp.ones((LC, HEAD_DIM), F32), jnp.concatenate([cos, cos], -1)], 0)
    sn = jnp.concatenate([jnp.zeros((LC, HEAD_DIM), F32), jnp.concatenate([-sin, sin], -1)], 0)
    return cs, sn


def _prep_tiles(cf, z, cs, sn, key):
    b = cf.OFF[key] // LANE
    return [(z, LANE, _col(b), True), (cs, LANE, _c0, False), (sn, LANE, _c0, False)]


_PREP = {'aq': _f_prep_norm, 'ak': _f_prep_norm, 'rq': _f_prep_plain, 'rk': _f_prep_scaled}


def _prep_fwd(cf, z, cs, sn, key, g):
    nh = cf.W[key] // LANE
    params = [(g, 'shared', True)] if g is not None else []
    return _row_fwd(cf, "prep_fwd_" + key, _PREP[key], _prep_tiles(cf, z, cs, sn, key), params,
                    [(LANE, _col(0), cf.W[key], BF16)], cf.TQ, nrep=nh)[0]


def _prep_bwd(cf, z, cs, sn, key, g, dt):
    nh = cf.W[key] // LANE
    params = [(g, 'shared', True)] if g is not None else []
    tg, pg = _row_bwd(cf, "prep_bwd_" + key, _PREP[key], _prep_tiles(cf, z, cs, sn, key), params,
                      [(dt, LANE, _col(0))], [(LANE, _col(0), cf.W[key], BF16)], cf.TQ, nrep=nh)
    return tg[0], (pg[0] if g is not None else None)


def _gate_params(cf, gup, gb):
    K = gup.shape[-1]
    gf = jnp.zeros((LANE, K), F32).at[0:GLA_RANK].set(gup[0])
    gbm = jnp.zeros((LANE, K), F32).at[GLA_RANK:2 * GLA_RANK].set(gup[1])
    return [(gf, 'shared', True), (gbm, 'shared', True), (gb[0:1], 'shared', True), (gb[1:2], 'shared', True)]


def _mix_tiles(cf, z, o, key):
    return [(o, LANE, _col(0), True), (z, LANE, _col(cf.OFF[key] // LANE), True)]


def _mid_io(cf, l, W, mod, x, y, zero=0.0):
    tiles = [(x, cf.D, _c0, True), (y, cf.D, _c0, True)]
    params = [(mod[2], 'stream', True), (W['norm2_g'][l] + zero, 'shared', True), (mod[3], 'stream', True), (mod[4], 'stream', True)]
    return tiles, params


class _BigWeights:
    def __init__(self, cf, srcs, lands, ssems, rsems):
        self.cf, self.srcs, self.lands, self.ssems, self.rsems = cf, list(srcs), lands, ssems, rsems
        self.whole = {}

    def get(self, kind, l, after=None):
        cf = self.cf
        if (kind, l) not in self.whole:
            n = _KINDS.index(kind)
            p = l * len(_KINDS) + n
            self.srcs[n], w = _ag_wait(cf, kind, l, self.srcs[n], self.lands[p], self.ssems[p], self.rsems[p], after)
            if kind == 'in':
                w = jnp.pad(_unshard_last(w), ((0, 0), (0, cf.NZ - cf.NIN)))
            self.whole[(kind, l)] = w
        return self.whole[(kind, l)]


def _layer_fwd(cf, l, W, big, mod, x, h, cs, sn):
    T, D, Fd = cf.T, cf.D, cf.F
    z = _mm("z_%d" % l, h, big.get('in', l, h), 'nn', T, cf.NZ, D, F32)
    qa = _prep_fwd(cf, z, cs, sn, 'aq', W['q_norm_g'][l])
    ka = _prep_fwd(cf, z, cs, sn, 'ak', W['k_norm_g'][l])
    qr = _prep_fwd(cf, z, cs, sn, 'rq', None)
    kr = _prep_fwd(cf, z, cs, sn, 'rk', None)
    o_att = _att_fwd(cf, qa, ka, z)
    o_ret = _ret_fwd(cf, qr, kr, z, W['ret_log_decay'][l])
    gates = _gate_params(cf, W['gla_gate_up'][l], W['gla_gate_b'][l])
    ga_tile = [(z, LANE, _col(cf.OFF['ga'] // LANE), True)]
    K = cf.HG * 64
    laf, lab = _row_fwd(cf, "gates_fwd_%d" % l, _f_gates, ga_tile, gates, [(K, _c0, K, F32), (K, _c0, K, F32)], cf.TQ)
    o_gla, sst = _gla_fwd(cf, z, laf, lab)
    cat_r = _row_fwd(cf, "mixr_fwd_%d" % l, _f_gated_norm, _mix_tiles(cf, z, o_ret, 'rg'), [(W['ret_norm_g'][l], 'shared', True)],
                     [(LANE, _col(0), cf.HR * LANE, BF16)], cf.TQ, nrep=cf.HR)[0]
    cat_g = _row_fwd(cf, "mixg_fwd_%d" % l, _f_gated_norm, _mix_tiles(cf, z, o_gla, 'gr'), [(W['gla_norm_g'][l], 'shared', True)],
                     [(LANE, _col(0), cf.HG * LANE, BF16)], cf.TQ, nrep=cf.HG)[0]
    cat = jnp.concatenate([o_att, cat_r, cat_g], axis=-1)
    y = _mm("y_%d" % l, cat, big.get('out', l, cat), 'nn', T, D, D, F32)
    tiles, params = _mid_io(cf, l, W, mod, x, y)
    x1, h2 = _row_fwd(cf, "mid_fwd_%d" % l, _f_resid_norm_mod, tiles, params, [(D, _c0, D, F32), (D, _c0, D, BF16)], cf.TM)
    u = _mm("u_%d" % l, h2, big.get('up', l, h2), 'nn', T, 2 * Fd, D, F32)
    t = _conv_fwd(cf, u, W['conv_wb'][l])
    yff = _mm("yff_%d" % l, t, big.get('down', l, t), 'nn', T, D, Fd, F32)
    return dict(x=x, h=h, z=z, qa=qa, ka=ka, qr=qr, kr=kr, laf=laf, lab=lab, sst=sst, o_ret=o_ret, o_gla=o_gla, cat=cat, y=y,
                x1=x1, h2=h2, u=u, t=t, yff=yff, gates=gates)


def _layer_bwd(cf, l, W, big, mod, sv, dx1, dyff, cs, sn):
    T, D, Fd = cf.T, cf.D, cf.F
    g, rs = {}, {}
    rs['down'] = _rs_start(cf, 'down', l, _mm("gwd_%d" % l, sv['t'], dyff, 'tn', Fd, D, T, BF16, tk=T), cs)
    dt = _mm("dt_%d" % l, dyff, big.get('down', l), 'nt', T, Fd, D, BF16)
    da, dv, g['conv_wb'] = _conv_bwd(cf, sv['u'], W['conv_wb'][l] + rs['down'][4][0, 0], dt)
    du = jnp.concatenate([da, dv], axis=-1)
    cu = 2 * Fd // N_DEV
    rs['up'] = _rs_start(cf, 'up', l, _mm("gwu_%d" % l, sv['h2'], du, 'tn', D, 2 * Fd, T, BF16, tn=cu, tk=T, out_shape=(N_DEV, D, cu),
                                          out_spec=pl.BlockSpec((None, _tile(D, 768, 128), cu), lambda i, j, k: (j, i, 0))), cs)
    dh2 = _mm("dh2_%d" % l, du, big.get('up', l), 'nt', T, D, 2 * Fd, BF16)
    tiles, params = _mid_io(cf, l, W, mod, sv['x'], sv['y'], rs['up'][4][0, 0])
    (dx, dy), (g['m2'], g['norm2_g'], g['m3'], g['m4']) = _row_bwd(
        cf, "mid_bwd_%d" % l, _f_resid_norm_mod, tiles, params, [(dx1, D, _c0), (dh2, D, _c0)],
        [(D, _c0, D, F32), (D, _c0, D, BF16)], cf.TM)
    rs['out'] = _rs_start(cf, 'out', l, _mm("gwo_%d" % l, sv['cat'], dy, 'tn', D, D, T, BF16, tk=T), cs)
    dcat = _mm("dcat_%d" % l, dy, big.get('out', l), 'nt', T, D, D, BF16)
    z = sv['z']
    (do_ret, drg), (g['ret_norm_g'],) = _row_bwd(
        cf, "mixr_bwd_%d" % l, _f_gated_norm, _mix_tiles(cf, z, sv['o_ret'], 'rg'), [(W['ret_norm_g'][l] + rs['out'][4][0, 0], 'shared', True)],
        [(dcat, LANE, _col(cf.HQ))], [(LANE, _col(0), cf.HR * LANE, F32), (LANE, _col(0), cf.HR * LANE, BF16)], cf.TQ, nrep=cf.HR)
    (do_gla, dgr), (g['gla_norm_g'],) = _row_bwd(
        cf, "mixg_bwd_%d" % l, _f_gated_norm, _mix_tiles(cf, z, sv['o_gla'], 'gr'), [(W['gla_norm_g'][l], 'shared', True)],
        [(dcat, LANE, _col(cf.HQ + cf.HR))], [(LANE, _col(0), cf.HG * LANE, F32), (LANE, _col(0), cf.HG * LANE, BF16)], cf.TQ, nrep=cf.HG)
    dqa, dka, dav = _att_bwd(cf, sv['qa'], sv['ka'], z, dcat)
    dqr, dkr, drv, dlg = _ret_bwd(cf, sv['qr'], sv['kr'], z, W['ret_log_decay'][l], do_ret)
    g['ret_log_decay'] = dlg[:, 0:2, 0].T
    dgq, dgk, dgv, dlaf, dlab = _gla_bwd(cf, z, sv['laf'], sv['lab'], sv['sst'], do_gla)
    K = cf.HG * 64
    ga_tile = [(z, LANE, _col(cf.OFF['ga'] // LANE), True)]
    (dga,), (ggf, ggb, gbf, gbb) = _row_bwd(cf, "gates_bwd_%d" % l, _f_gates, ga_tile, sv['gates'],
                                             [(dlaf, K, _c0), (dlab, K, _c0)], [(LANE, _c0, LANE, BF16)], cf.TQ)
    g['gla_gate_up'] = jnp.stack([ggf[0:GLA_RANK], ggb[GLA_RANK:2 * GLA_RANK]])
    g['gla_gate_b'] = jnp.concatenate([gbf, gbb], axis=0)
    daq, g['q_norm_g'] = _prep_bwd(cf, z, cs, sn, 'aq', W['q_norm_g'][l], dqa)
    dak, g['k_norm_g'] = _prep_bwd(cf, z, cs, sn, 'ak', W['k_norm_g'][l], dka)
    drq, _ = _prep_bwd(cf, z, cs, sn, 'rq', None, dqr)
    drk, _ = _prep_bwd(cf, z, cs, sn, 'rk', None, dkr)
    pad = jnp.zeros((T, cf.NZ - cf.OFF['ga'] - LANE), BF16)
    dz = jnp.concatenate([daq, dak, dav.astype(BF16), drq, drk, drv.astype(BF16), drg, dgq.astype(BF16), dgk.astype(BF16),
                          dgv.astype(BF16), dgr, dga, pad], axis=-1)
    gwi = _mm("gwi_%d" % l, sv['h'], dz, 'tn', D, cf.NZ, T, BF16, tk=T)
    g['w_in_slabs'] = jnp.moveaxis(gwi[:, :cf.NIN].reshape(D, N_DEV, cf.NINS), 1, 0)
    dh = _mm("dh_%d" % l, dz, big.get('in', l), 'nt', T, D, cf.NZ, BF16)
    g['rs'] = rs
    return dx, dh, g


_WEIGHTS = ['c_ctx', 'ada_w', 'ada_b', 'norm1_g', 'w_in', 'q_norm_g', 'k_norm_g', 'ret_log_decay', 'ret_norm_g',
            'gla_gate_up', 'gla_gate_b', 'gla_norm_g', 'w_out', 'norm2_g', 'w_up', 'conv_w', 'conv_b', 'w_down', 'final_norm_g']
_BIG = ['w_in', 'w_out', 'w_up', 'w_down']
_SMALL = [n for n in _WEIGHTS if n not in _BIG and n != 'ada_w']
_COL_SHARDED = ['gla_gate_up', 'gla_gate_b', 'conv_w']


def _pack(arrs):
    flat = jnp.concatenate([a.reshape(-1) for a in arrs])
    n = flat.shape[0]
    tot = -(-n // (8 * LANE)) * (8 * LANE)
    return jnp.pad(flat, (0, tot - n)).reshape(tot // LANE, LANE)


def _unpack(flat, shapes):
    out, o = [], 0
    for s in shapes:
        n = int(np.prod(s))
        out.append(flat[..., o:o + n].reshape(flat.shape[:-1] + tuple(s)))
        o += n
    return out


def _unshard_last(a):
    return jnp.moveaxis(a, 0, -2).reshape(a.shape[1:-1] + (N_DEV * a.shape[-1],))


def _step(cf, x, c, ctx, loss_target, w, m, v):
    T, D, Fd, L, LC = cf.T, cf.D, cf.F, cf.L, cf.LC
    _, _, _, me = _me()
    NS = w['ada_w'].shape[-1]

    c_all = _ag_small("ag_c", jnp.pad(c, ((0, 7), (0, 0))))[:, 0, :]
    c9 = jnp.concatenate([c_all, w['c_ctx'][None], jnp.zeros((7, D), F32)], axis=0)
    pm = _ada_fwd(cf, c9, w['ada_w'])
    pm_all = _ag_small("ag_mod", pm.reshape(DEPTH * 16, NS)).reshape(N_DEV, DEPTH, 16, NS)
    mod_all = _unshard_last(pm_all) + w['ada_b'][:, None, :]
    mod_own = lax.dynamic_index_in_dim(mod_all, me, axis=1, keepdims=False)
    mods = []
    for l in range(DEPTH):
        mods.append([jnp.stack([mod_all[l, 8, k * D:(k + 1) * D], mod_own[l, k * D:(k + 1) * D]])[:, None, :] for k in range(N_MOD)])

    shard_shapes = [w[n].shape for n in _COL_SHARDED]
    got = _ag_small("ag_smallw", _pack([w[n] for n in _COL_SHARDED]))
    full = dict(zip(_COL_SHARDED, [_unshard_last(a) for a in _unpack(got.reshape(N_DEV, -1), shard_shapes)]))

    srcs, lands, ssems, rsems, ag_tok = _ag_start(cf, [_cast_bf16("cast_" + n, w[n]) for n in _BIG], full['conv_w'] + mod_all[0, 0, 0])
    big = _BigWeights(cf, srcs, lands, ssems, rsems)
    conv_wb = jnp.concatenate([full['conv_w'], w['conv_b'][:, None, :], jnp.zeros((DEPTH, 4, Fd), F32)], axis=1)
    W = dict(conv_wb=conv_wb, gla_gate_up=full['gla_gate_up'], gla_gate_b=full['gla_gate_b'], ret_log_decay=w['ret_log_decay'])
    for n in ['q_norm_g', 'k_norm_g', 'ret_norm_g', 'gla_norm_g', 'norm1_g', 'norm2_g']:
        W[n] = w[n][:, None, :]

    cs, sn = _rope_tables(cf)
    x0 = jnp.concatenate([ctx[0], x[0]], axis=0)
    pre_tiles = [(x0, D, _c0, True)]

    def pre_params(zero):
        return [(W['norm1_g'][0] + zero, 'shared', True), (mods[0][0], 'stream', True), (mods[0][1], 'stream', True)]

    def tr_params(zero):
        return [(mods[0][5], 'stream', True), (W['norm1_g'][1] + zero, 'shared', True), (mods[1][0], 'stream', True), (mods[1][1], 'stream', True)]

    h0 = _row_fwd(cf, "pre_fwd", _f_norm_mod, pre_tiles, pre_params(ag_tok[0, 0]), [(D, _c0, D, BF16)], cf.TM)[0]
    sv0 = _layer_fwd(cf, 0, W, big, mods[0], x0, h0, cs, sn)
    tr_tiles = [(sv0['x1'], D, _c0, True), (sv0['yff'], D, _c0, True)]
    xb, hb = _row_fwd(cf, "tr_fwd", _f_resid_norm_mod, tr_tiles, tr_params(0.0), [(D, _c0, D, F32), (D, _c0, D, BF16)], cf.TM)
    sv1 = _layer_fwd(cf, 1, W, big, mods[1], xb, hb, cs, sn)
    tgt = jnp.concatenate([jnp.zeros((LC, D), F32), loss_target[0]], axis=0)
    dx1, dyff, dm5_1, g_final, ls = _loss_grad(cf, sv1['x1'], sv1['yff'], mods[1][5], w['final_norm_g'][None], tgt)
    loss = lax.psum(ls[0, 0], ("x", "y", "c"))

    dxb, dhb, g1 = _layer_bwd(cf, 1, W, big, mods[1], sv1, dx1, dyff, cs, sn)
    g1['rs']['in'] = _rs_start(cf, 'in', 1, g1['w_in_slabs'], cs)
    (dx1_0, dyff_0), (dm5_0, gn1_1, dm0_1, dm1_1) = _row_bwd(
        cf, "tr_bwd", _f_resid_norm_mod, tr_tiles, tr_params(g1['rs']['in'][4][0, 0]), [(dxb, D, _c0), (dhb, D, _c0)],
        [(D, _c0, D, F32), (D, _c0, D, BF16)], cf.TM)
    dx0, dh0, g0 = _layer_bwd(cf, 0, W, big, mods[0], sv0, dx1_0, dyff_0, cs, sn)
    (dxa,), (gn1_0, dm0_0, dm1_0) = _row_bwd(cf, "pre_bwd", _f_first, pre_tiles, pre_params(0.0), [(dx0, D, _c0), (dh0, D, _c0)],
                                            [(D, _c0, D, F32)], cf.TM)
    grad_x = dxa[LC:][None]

    dmod = jnp.stack([jnp.concatenate([dm0_0, dm1_0, g0['m2'], g0['m3'], g0['m4'], dm5_0], axis=-1)[:, 0],
                      jnp.concatenate([dm0_1, dm1_1, g1['m2'], g1['m3'], g1['m4'], dm5_1], axis=-1)[:, 0]])
    dm_all = _ag_small("ag_dmod", jnp.pad(dmod.reshape(2 * DEPTH, N_MOD * D), ((0, 8 - 2 * DEPTH), (0, 0))))
    dm_all = dm_all[:, :2 * DEPTH].reshape(N_DEV, DEPTH, 2, N_MOD * D)
    dctx = _sum8("sum_dmodc", jnp.pad(dm_all[:, :, 0], ((0, 0), (0, 8 - DEPTH), (0, 0))))[:DEPTH]
    dm9 = jnp.concatenate([jnp.moveaxis(dm_all[:, :, 1], 0, 1), dctx[:, None]], axis=1)
    g_ada_b = _sum8("sum_adab", jnp.pad(jnp.moveaxis(dm9, 1, 0), ((0, 0), (0, 8 - DEPTH), (0, 0))))[:DEPTH]
    dm9s = lax.dynamic_slice_in_dim(jnp.pad(dm9, ((0, 0), (0, 7), (0, 0))), me * NS, NS, axis=2)
    g_ada_w, dsil = _ada_bwd(cf, c9, w['ada_w'], dm9s)
    g_cctx_part = dsil[0, 8]
    for l in range(1, DEPTH):
        g_cctx_part = g_cctx_part + dsil[l, 8]

    def both(key):
        return jnp.stack([g0[key], g1[key]])

    gsmall = dict(c_ctx=g_cctx_part, norm1_g=jnp.stack([gn1_0[0], gn1_1[0]]), q_norm_g=both('q_norm_g')[:, 0],
                  k_norm_g=both('k_norm_g')[:, 0], ret_log_decay=both('ret_log_decay'), ret_norm_g=both('ret_norm_g')[:, 0],
                  gla_gate_up=both('gla_gate_up'), gla_gate_b=both('gla_gate_b'), gla_norm_g=both('gla_norm_g')[:, 0],
                  norm2_g=both('norm2_g')[:, 0], conv_w=both('conv_wb')[:, 0:3], conv_b=both('conv_wb')[:, 3], final_norm_g=g_final[0])
    snames = [n for n in _SMALL if n != 'ada_b']
    sshapes = [gsmall[n].shape for n in snames]
    gs_all = _ag_small("ag_gsmall", _pack([gsmall[n] for n in snames]))
    gs = dict(zip(snames, _unpack(_sum8("sum_gsmall", gs_all).reshape(-1), sshapes)))
    gs['ada_b'] = g_ada_b
    for n in _COL_SHARDED:
        ns_ = w[n].shape[-1]
        gs[n] = lax.dynamic_slice_in_dim(gs[n], me * ns_, ns_, axis=gs[n].ndim - 1)

    g0['rs']['in'] = _rs_start(cf, 'in', 0, g0['w_in_slabs'], gs_all)
    out_g, out_d, out_m, out_v = {}, {}, {}, {}
    after = g0['rs']['in'][4]

    def update_big(kind, after):
        n = 'w_' + kind
        recvs = [_rs_wait(cf, kind, l, *gl['rs'][kind][:4], after) for l, gl in enumerate((g0, g1))]
        out_g[n], out_d[n], out_m[n], out_v[n] = _sum_adam("adam_" + n, recvs, w[n], m[n], v[n])
        return out_g[n]

    for kind in ['down', 'up', 'out']:
        after = update_big(kind, after)
    aw = [a.reshape(DEPTH * D, NS) for a in (w['ada_w'], g_ada_w, m['ada_w'], v['ada_w'])]
    out_g['ada_w'] = g_ada_w
    out_d['ada_w'], out_m['ada_w'], out_v['ada_w'] = [a.reshape(DEPTH, D, NS) for a in _adam("adam_ada_w", *aw)]
    shp = [w[n].shape for n in _SMALL]
    packed = [_pack([src[n] for n in _SMALL]) for src in (w, gs, m, v)]
    res = _adam("adam_small", *packed)
    for dst, pk in zip((out_d, out_m, out_v), res):
        dst.update(zip(_SMALL, _unpack(pk.reshape(-1), shp)))
    out_g.update({n: gs[n] for n in _SMALL})
    update_big('in', out_d['ada_w'])
    return (loss, grad_x, *[out_g[n] for n in _WEIGHTS], *[out_d[n] for n in _WEIGHTS], *[out_m[n] for n in _WEIGHTS],
            *[out_v[n] for n in _WEIGHTS])


def kernel(x, c, ctx, c_ctx, ada_w, ada_b, norm1_g, w_in, q_norm_g, k_norm_g, ret_log_decay, ret_norm_g, gla_gate_up, gla_gate_b, gla_norm_g, w_out, norm2_g, w_up, conv_w, conv_b, w_down, final_norm_g, loss_target, m_c_ctx, m_ada_w, m_ada_b, m_norm1_g, m_w_in, m_q_norm_g, m_k_norm_g, m_ret_log_decay, m_ret_norm_g, m_gla_gate_up, m_gla_gate_b, m_gla_norm_g, m_w_out, m_norm2_g, m_w_up, m_conv_w, m_conv_b, m_w_down, m_final_norm_g, v_c_ctx, v_ada_w, v_ada_b, v_norm1_g, v_w_in, v_q_norm_g, v_k_norm_g, v_ret_log_decay, v_ret_norm_g, v_gla_gate_up, v_gla_gate_b, v_gla_norm_g, v_w_out, v_norm2_g, v_w_up, v_conv_w, v_conv_b, v_w_down, v_final_norm_g):
    w = dict(c_ctx=c_ctx, ada_w=ada_w, ada_b=ada_b, norm1_g=norm1_g, w_in=w_in, q_norm_g=q_norm_g, k_norm_g=k_norm_g,
             ret_log_decay=ret_log_decay, ret_norm_g=ret_norm_g, gla_gate_up=gla_gate_up, gla_gate_b=gla_gate_b,
             gla_norm_g=gla_norm_g, w_out=w_out, norm2_g=norm2_g, w_up=w_up, conv_w=conv_w, conv_b=conv_b, w_down=w_down,
             final_norm_g=final_norm_g)
    m = dict(c_ctx=m_c_ctx, ada_w=m_ada_w, ada_b=m_ada_b, norm1_g=m_norm1_g, w_in=m_w_in, q_norm_g=m_q_norm_g,
             k_norm_g=m_k_norm_g, ret_log_decay=m_ret_log_decay, ret_norm_g=m_ret_norm_g, gla_gate_up=m_gla_gate_up,
             gla_gate_b=m_gla_gate_b, gla_norm_g=m_gla_norm_g, w_out=m_w_out, norm2_g=m_norm2_g, w_up=m_w_up,
             conv_w=m_conv_w, conv_b=m_conv_b, w_down=m_w_down, final_norm_g=m_final_norm_g)
    v = dict(c_ctx=v_c_ctx, ada_w=v_ada_w, ada_b=v_ada_b, norm1_g=v_norm1_g, w_in=v_w_in, q_norm_g=v_q_norm_g,
             k_norm_g=v_k_norm_g, ret_log_decay=v_ret_log_decay, ret_norm_g=v_ret_norm_g, gla_gate_up=v_gla_gate_up,
             gla_gate_b=v_gla_gate_b, gla_norm_g=v_gla_norm_g, w_out=v_w_out, norm2_g=v_norm2_g, w_up=v_w_up,
             conv_w=v_conv_w, conv_b=v_conv_b, w_down=v_w_down, final_norm_g=v_final_norm_g)
    return _step(_cfg(), x, c, ctx, loss_target, w, m, v)
```

```python
import functools
import math
import types

import jax
import jax.numpy as jnp
import numpy as np
from jax import lax
from jax.experimental import pallas as pl
from jax.experimental.pallas import tpu as pltpu

F32 = jnp.float32
BF16 = jnp.bfloat16
HI = lax.Precision.HIGHEST

D_MODEL = 2048
SEQ = 2048
CTX_LEN = 256
GRID_W = 64
D_FF = 5632
DEPTH = 2
N_DEV = 8
HEAD_DIM = 128
ROPE_THETA = 10000.0
GLA_TAU = 16.0
GLA_RANK = 16
GLA_CHUNK = 64
GLA_SUB = 16
EPS = 1e-6
N_MOD = 6
ADAM_LR = 0.001
ADAM_B1 = 0.9
ADAM_B2 = 0.999
ADAM_EPS = 1e-08
ADAM_WD = 0.01
ADAM_STEP = 10
LANE = 128
VMEM_LIMIT = 56 * 1024 * 1024
NEG = -1e30


def _cfg():
    d = types.SimpleNamespace()
    d.D, d.L, d.LC, d.F = D_MODEL, SEQ, CTX_LEN, D_FF
    d.T = d.L + d.LC
    nm = d.D // HEAD_DIM
    d.HQ, d.HKV, d.HR, d.HG = nm // 2, nm // 8, nm // 4, nm // 4
    d.G = d.HQ // d.HKV
    w = dict(aq=d.HQ * 128, ak=d.HKV * 128, av=d.HKV * 128, rq=d.HR * 128, rk=d.HR * 128, rv=d.HR * 128,
             rg=d.HR * 128, gq=d.HG * 64, gk=d.HG * 64, gv=d.HG * 128, gr=d.HG * 128, ga=2 * GLA_RANK)
    off, o = {}, 0
    for k, v in w.items():
        off[k] = o
        o += v
    d.W, d.OFF, d.NIN = w, off, o
    d.NZ = -(-(off['ga'] + LANE) // 256) * 256
    d.NINS = d.NIN // N_DEV
    d.TM = math.gcd(d.LC, 128)
    d.TQ = math.gcd(d.LC, 256)
    return d


def _cp(sem=None):
    return pltpu.CompilerParams(dimension_semantics=sem, vmem_limit_bytes=VMEM_LIMIT)


def _tile(n, target, mult=LANE):
    t = min(n, target)
    t -= t % mult
    while t > mult and n % t:
        t -= mult
    return t if t > 0 and n % t == 0 else n


_DN = {'nn': ((1,), (0,)), 'nt': ((1,), (1,)), 'tn': ((0,), (0,))}


def _mm(name, a, b, kind, M, N, K, out_dtype, tm=768, tn=768, tk=1024, a_spec=None, b_spec=None,
        out_shape=None, out_spec=None):
    tm, tn = _tile(M, tm, 128), _tile(N, tn, 128)
    tk = _tile(K, tk, 128)
    nk = K // tk

    def body(a_ref, b_ref, o_ref, acc):
        kk = pl.program_id(2)

        @pl.when(kk == 0)
        def _():
            acc[...] = jnp.zeros_like(acc)

        acc[...] += lax.dot_general(a_ref[...].astype(BF16), b_ref[...].astype(BF16), (_DN[kind], ((), ())),
                                    preferred_element_type=F32)

        @pl.when(kk == nk - 1)
        def _():
            o_ref[...] = acc[...].astype(o_ref.dtype)

    if a_spec is None:
        a_spec = pl.BlockSpec((tk, tm), lambda i, j, k: (k, i)) if kind == 'tn' else pl.BlockSpec((tm, tk), lambda i, j, k: (i, k))
    if b_spec is None:
        b_spec = pl.BlockSpec((tn, tk), lambda i, j, k: (j, k)) if kind == 'nt' else pl.BlockSpec((tk, tn), lambda i, j, k: (k, j))
    if out_spec is None:
        out_spec = pl.BlockSpec((tm, tn), lambda i, j, k: (i, j))
        out_shape = (M, N)
    return pl.pallas_call(
        body, name=name, grid=(M // tm, N // tn, nk), in_specs=[a_spec, b_spec], out_specs=out_spec,
        out_shape=jax.ShapeDtypeStruct(out_shape, out_dtype), scratch_shapes=[pltpu.VMEM((tm, tn), F32)],
        compiler_params=_cp(("parallel", "parallel", "arbitrary")))(a, b)


def _row_specs(cf, tm, tiles, params):
    nctx = cf.LC // tm
    specs = []
    for arr, w, colf, _ in tiles:
        specs.append(pl.BlockSpec((tm, w), functools.partial(lambda i, r, colf: (i, colf(r)), colf=colf)))
    for arr, kind, _ in params:
        nd = arr.ndim
        if kind == 'shared':
            specs.append(pl.BlockSpec(arr.shape, functools.partial(lambda i, r, nd: (0,) * nd, nd=nd)))
        else:
            specs.append(pl.BlockSpec((None,) + arr.shape[1:],
                                      functools.partial(lambda i, r, nd, nctx: (jnp.where(i >= nctx, 1, 0),) + (0,) * (nd - 1), nd=nd, nctx=nctx)))
    return specs


def _row_fwd(cf, name, f, tiles, params, outs, tm, nrep=1):
    nt, npar = len(tiles), len(params)

    def body(*refs):
        tv = [r[...].astype(F32) for r in refs[:nt]]
        pv = [r[...] for r in refs[nt:nt + npar]]
        res = f(*tv, *pv)
        for o, v in zip(refs[nt + npar:], res):
            o[...] = v.astype(o.dtype)

    out_specs = [pl.BlockSpec((tm, w), functools.partial(lambda i, r, colf: (i, colf(r)), colf=colf)) for w, colf, _, _ in outs]
    out_shape = [jax.ShapeDtypeStruct((cf.T, tw), dt) for _, _, tw, dt in outs]
    return pl.pallas_call(
        body, name=name, grid=(cf.T // tm, nrep), in_specs=_row_specs(cf, tm, tiles, params), out_specs=out_specs,
        out_shape=out_shape, compiler_params=_cp(("arbitrary", "arbitrary")))(*[t[0] for t in tiles], *[p[0] for p in params])


def _row_bwd(cf, name, f, tiles, params, cts, tgrads, tm, nrep=1):
    nt, npar, nc = len(tiles), len(params), len(cts)
    tdiff = [k for k, t in enumerate(tiles) if t[3]]
    pdiff = [k for k, p in enumerate(params) if p[2]]
    nctx = cf.LC // tm

    def body(*refs):
        i, r = pl.program_id(0), pl.program_id(1)
        tv = [x[...].astype(F32) for x in refs[:nt]]
        pv = [x[...] for x in refs[nt:nt + npar]]
        cv = tuple(x[...].astype(F32) for x in refs[nt + npar:nt + npar + nc])
        outs = refs[nt + npar + nc:]

        def g(*diff):
            tv2, pv2 = list(tv), list(pv)
            for k, v in zip(tdiff, diff[:len(tdiff)]):
                tv2[k] = v
            for k, v in zip(pdiff, diff[len(tdiff):]):
                pv2[k] = v
            return tuple(f(*tv2, *pv2))

        _, vjp_fn = jax.vjp(g, *[tv[k] for k in tdiff], *[pv[k] for k in pdiff])
        grads = vjp_fn(cv)
        for o, gv in zip(outs[:len(tdiff)], grads[:len(tdiff)]):
            o[...] = gv.astype(o.dtype)
        for n_, (o, gv) in enumerate(zip(outs[len(tdiff):], grads[len(tdiff):])):
            if params[pdiff[n_]][1] == 'shared':
                first = jnp.logical_and(i == 0, r == 0)
            else:
                first = jnp.logical_and(jnp.logical_or(i == 0, i == nctx), r == 0)

            @pl.when(first)
            def _():
                o[...] = gv

            @pl.when(jnp.logical_not(first))
            def _():
                o[...] += gv

    in_specs = _row_specs(cf, tm, tiles, params)
    in_specs += [pl.BlockSpec((tm, w), functools.partial(lambda i, r, colf: (i, colf(r)), colf=colf)) for _, w, colf in cts]
    out_specs = [pl.BlockSpec((tm, w), functools.partial(lambda i, r, colf: (i, colf(r)), colf=colf)) for w, colf, _, _ in tgrads]
    out_shape = [jax.ShapeDtypeStruct((cf.T, tw), dt) for _, _, tw, dt in tgrads]
    pspecs = _row_specs(cf, tm, [], [params[k] for k in pdiff])
    out_specs += pspecs
    out_shape += [jax.ShapeDtypeStruct(params[k][0].shape, F32) for k in pdiff]
    res = pl.pallas_call(
        body, name=name, grid=(cf.T // tm, nrep), in_specs=in_specs, out_specs=out_specs, out_shape=out_shape,
        compiler_params=_cp(("arbitrary", "arbitrary")))(*[t[0] for t in tiles], *[p[0] for p in params], *[c[0] for c in cts])
    return res[:len(tdiff)], res[len(tdiff):]


def _c0(r):
    return 0


def _col(base):
    return lambda r: base + r


def _rms(x, g):
    return x * lax.rsqrt(jnp.mean(x * x, axis=-1, keepdims=True) + EPS) * g


def _sigmoid(x):
    return 1.0 / (1.0 + jnp.exp(-x))


def _silu(x):
    return x * _sigmoid(x)


def _f_norm_mod(x, g, sh, sc):
    return (_rms(x, g) * (1 + sc) + sh,)


def _f_resid_norm_mod(x, y, gate, g, sh, sc):
    x1 = x + gate * y
    return (x1, _rms(x1, g) * (1 + sc) + sh)


@jax.custom_vjp
def _swap_halves(t):
    return pltpu.roll(t, HEAD_DIM // 2, axis=1)


def _swap_fwd(t):
    return _swap_halves(t), None


def _swap_bwd(_, g):
    return (pltpu.roll(g, HEAD_DIM // 2, axis=1),)


_swap_halves.defvjp(_swap_fwd, _swap_bwd)


def _rope(t, cs, sn):
    return t * cs + _swap_halves(t) * sn


def _f_prep_norm(t, cs, sn, g):
    return (_rope(_rms(t, g), cs, sn),)


def _f_prep_plain(t, cs, sn):
    return (_rope(t, cs, sn),)


def _f_prep_scaled(t, cs, sn):
    return (_rope(t * (HEAD_DIM ** -0.5), cs, sn),)


def _log_sigmoid(x):
    return jnp.minimum(x, 0.0) - jnp.log(1.0 + jnp.exp(-jnp.abs(x)))


N_DECAY = 8


def _gla_masks(d, width):
    C, SB = GLA_CHUNK, GLA_SUB
    r = lax.broadcasted_iota(jnp.int32, (C, C), 0)
    m = lax.broadcasted_iota(jnp.int32, (C, C), 1)
    rr = lax.broadcasted_iota(jnp.int32, (C, width), 0)
    allowed = (m <= r) if d == 0 else (m >= r)
    blocks, vis = [allowed], []
    for b in range(C // SB):
        blocks.append((m < SB * b) if d == 0 else (m >= SB * (b + 1)))
        vis.append((rr < SB * (b + 1)) if d == 0 else (rr >= SB * b))
    cm = jnp.concatenate([x.astype(F32) for x in blocks] + [jnp.ones((C, C), F32)], axis=0)
    return cm, allowed, vis


def _gla_decays(la, d):
    C, SB = GLA_CHUNK, GLA_SUB
    nsb = C // SB
    cm, _, vis = _gla_masks(d, la.shape[-1])
    cums = jnp.dot(cm, la, precision=HI, preferred_element_type=F32)
    cum, tot = cums[0:C], cums[(1 + nsb) * C:]
    refs = [cums[(1 + b) * C:(2 + b) * C] for b in range(nsb)]
    e1 = jnp.concatenate([jnp.exp(cum[b * SB:(b + 1) * SB] - refs[b][b * SB:(b + 1) * SB]) for b in range(nsb)], axis=0)
    e2 = [jnp.where(vis[b], jnp.exp(jnp.where(vis[b], refs[b] - cum, 0.0)), 0.0) for b in range(nsb)]
    return [e1] + e2 + [jnp.exp(cum), jnp.exp(tot - cum), jnp.exp(tot)]


def _f_gla_pre(ga, *per_pair):
    gab = ga.astype(BF16)
    outs = [[], []]
    for p in range(len(per_pair) // 4):
        gf, gb, bf, bb = per_pair[4 * p:4 * p + 4]
        for d, (gm, bm) in enumerate(((gf, bf), (gb, bb))):
            la = _log_sigmoid(jnp.dot(gab, gm.astype(BF16), preferred_element_type=F32) + bm) / GLA_TAU
            outs[d] += _gla_decays(la, d)
    return tuple(jnp.concatenate(o, axis=-1) for o in outs)


def _f_gated_norm(o, g, n):
    return (_rms(o, n) * _silu(g),)


def _f_first(x, g, sh, sc):
    return (x, _rms(x, g) * (1 + sc) + sh)


def _loss_grad(cf, x1, yff, gate, gfin, tgt):
    tm, T, D = cf.TM, cf.T, cf.D
    nctx = cf.LC // tm

    def lossf(x1v, yv, gt, gf, tg):
        y = _rms(x1v + gt * yv, gf)
        e = y - tg
        return 0.5 * jnp.sum(jnp.mean(e * e, axis=-1, keepdims=True), axis=0, keepdims=True)

    def body(x1_ref, y_ref, gt_ref, gf_ref, tg_ref, dx_ref, dy_ref, dgt_ref, dgf_ref, ls_ref):
        i = pl.program_id(0)
        lat = (i >= nctx).astype(F32)
        val, vjp_fn = jax.vjp(lossf, x1_ref[...], y_ref[...].astype(F32), gt_ref[...], gf_ref[...], tg_ref[...])
        dx, dy, dgt, dgf, _ = vjp_fn(jnp.ones((1, 1), F32) * lat)
        dx_ref[...] = dx
        dy_ref[...] = dy.astype(dy_ref.dtype)
        first_s = jnp.logical_or(i == 0, i == nctx)

        @pl.when(first_s)
        def _():
            dgt_ref[...] = dgt

        @pl.when(jnp.logical_not(first_s))
        def _():
            dgt_ref[...] += dgt

        @pl.when(i == 0)
        def _():
            dgf_ref[...] = dgf
            ls_ref[...] = jnp.zeros_like(ls_ref) + val * lat

        @pl.when(i != 0)
        def _():
            dgf_ref[...] += dgf
            ls_ref[...] += val * lat

    row = pl.BlockSpec((tm, D), lambda i: (i, 0))
    strm = pl.BlockSpec((None, 1, D), lambda i: (jnp.where(i >= nctx, 1, 0), 0, 0))
    one = pl.BlockSpec((1, D), lambda i: (0, 0))
    return pl.pallas_call(
        body, name="loss_grad", grid=(T // tm,), in_specs=[row, row, strm, one, row],
        out_specs=[row, row, strm, one, pl.BlockSpec((8, LANE), lambda i: (0, 0))],
        out_shape=[jax.ShapeDtypeStruct((T, D), F32), jax.ShapeDtypeStruct((T, D), BF16),
                   jax.ShapeDtypeStruct((2, 1, D), F32), jax.ShapeDtypeStruct((1, D), F32),
                   jax.ShapeDtypeStruct((8, LANE), F32)],
        compiler_params=_cp(("arbitrary",)))(x1, yff, gate, gfin, tgt)


def _att_mask(cf, i, tq):
    col = lax.broadcasted_iota(jnp.int32, (tq, cf.T), 1)
    return jnp.logical_or(col < cf.LC, i >= cf.LC // tq)


def _att_probs(q, k, mask):
    s = lax.dot_general(q, k, (_DN['nt'], ((), ())), preferred_element_type=F32) * (HEAD_DIM ** -0.5)
    s = jnp.where(mask, s, NEG)
    e = jnp.exp(s - jnp.max(s, axis=-1, keepdims=True))
    return e / jnp.sum(e, axis=-1, keepdims=True)


def _att_fwd(cf, q, k, z):
    tq, T, G = cf.TQ, cf.T, cf.G
    vb = cf.OFF['av'] // LANE

    def body(q_ref, k_ref, v_ref, o_ref):
        mask = _att_mask(cf, pl.program_id(1), tq)
        kv, vv = k_ref[...], v_ref[...].astype(BF16)
        for j in range(G):
            p = _att_probs(q_ref[:, j * LANE:(j + 1) * LANE], kv, mask)
            o_ref[:, j * LANE:(j + 1) * LANE] = jnp.dot(p.astype(BF16), vv, preferred_element_type=F32).astype(o_ref.dtype)

    return pl.pallas_call(
        body, name="att_fwd", grid=(cf.HKV, T // tq),
        in_specs=[pl.BlockSpec((tq, G * LANE), lambda g, i: (i, g)), pl.BlockSpec((T, LANE), lambda g, i: (0, g)),
                  pl.BlockSpec((T, LANE), lambda g, i: (0, vb + g))],
        out_specs=pl.BlockSpec((tq, G * LANE), lambda g, i: (i, g)),
        out_shape=jax.ShapeDtypeStruct((T, cf.HQ * LANE), BF16), compiler_params=_cp(("arbitrary", "arbitrary")))(q, k, z)


def _att_bwd(cf, q, k, z, dcat):
    tq, T, G = cf.TQ, cf.T, cf.G
    vb = cf.OFF['av'] // LANE
    sc = HEAD_DIM ** -0.5

    def body(q_ref, k_ref, v_ref, do_ref, dq_ref, dk_ref, dv_ref):
        i = pl.program_id(1)
        mask = _att_mask(cf, i, tq)
        kv, vv = k_ref[...], v_ref[...].astype(BF16)
        dk = jnp.zeros((T, LANE), F32)
        dv = jnp.zeros((T, LANE), F32)
        for j in range(G):
            qj = q_ref[:, j * LANE:(j + 1) * LANE]
            do = do_ref[:, j * LANE:(j + 1) * LANE]
            p = _att_probs(qj, kv, mask)
            dv += lax.dot_general(p.astype(BF16), do, (_DN['tn'], ((), ())), preferred_element_type=F32)
            dp = lax.dot_general(do, vv, (_DN['nt'], ((), ())), preferred_element_type=F32)
            ds = p * (dp - jnp.sum(dp * p, axis=-1, keepdims=True)) * sc
            dsb = ds.astype(BF16)
            dq_ref[:, j * LANE:(j + 1) * LANE] = jnp.dot(dsb, kv, preferred_element_type=F32)
            dk += lax.dot_general(dsb, qj, (_DN['tn'], ((), ())), preferred_element_type=F32)

        @pl.when(i == 0)
        def _():
            dk_ref[...] = dk
            dv_ref[...] = dv

        @pl.when(i != 0)
        def _():
            dk_ref[...] += dk
            dv_ref[...] += dv

    qs = pl.BlockSpec((tq, G * LANE), lambda g, i: (i, g))
    ks = pl.BlockSpec((T, LANE), lambda g, i: (0, g))
    return pl.pallas_call(
        body, name="att_bwd", grid=(cf.HKV, T // tq),
        in_specs=[qs, ks, pl.BlockSpec((T, LANE), lambda g, i: (0, vb + g)), qs],
        out_specs=[qs, ks, ks],
        out_shape=[jax.ShapeDtypeStruct((T, cf.HQ * LANE), F32), jax.ShapeDtypeStruct((T, cf.HKV * LANE), F32),
                   jax.ShapeDtypeStruct((T, cf.HKV * LANE), F32)],
        compiler_params=_cp(("arbitrary", "arbitrary")))(q, k, z, dcat)


def _ret_masks(cf, i, tq, lgf, lgb):
    T, LC = cf.T, cf.LC
    row = lax.broadcasted_iota(jnp.int32, (tq, T), 0) + i * tq
    col = lax.broadcasted_iota(jnp.int32, (tq, T), 1)

    def pb(n):
        return jnp.where(n < LC, LC - 1 - n, T + LC - 1 - n)

    relf = row - col
    relb = pb(row) - pb(col)
    okf, okb = relf >= 0, relb >= 0
    rf = jnp.where(okf, relf, 0).astype(F32)
    rb = jnp.where(okb, relb, 0).astype(F32)
    mf = jnp.where(okf, jnp.exp(lgf * rf), 0.0)
    mb = jnp.where(okb, jnp.exp(lgb * rb), 0.0)
    return mf, mb, rf, rb


def _ret_fwd(cf, q, k, z, lg):
    tq, T = cf.TQ, cf.T
    vb = cf.OFF['rv'] // LANE

    def body(lg_ref, q_ref, k_ref, v_ref, o_ref):
        h, i = pl.program_id(0), pl.program_id(1)
        mf, mb, _, _ = _ret_masks(cf, i, tq, lg_ref[0, h], lg_ref[1, h])
        a = lax.dot_general(q_ref[...], k_ref[...], (_DN['nt'], ((), ())), preferred_element_type=F32)
        p = (a * (mf + mb)).astype(BF16)
        o_ref[...] = jnp.dot(p, v_ref[...].astype(BF16), preferred_element_type=F32)

    return pl.pallas_call(
        body, name="ret_fwd", grid=(cf.HR, T // tq),
        in_specs=[pl.BlockSpec(memory_space=pltpu.SMEM), pl.BlockSpec((tq, LANE), lambda h, i: (i, h)),
                  pl.BlockSpec((T, LANE), lambda h, i: (0, h)), pl.BlockSpec((T, LANE), lambda h, i: (0, vb + h))],
        out_specs=pl.BlockSpec((tq, LANE), lambda h, i: (i, h)),
        out_shape=jax.ShapeDtypeStruct((T, cf.HR * LANE), F32), compiler_params=_cp(("arbitrary", "arbitrary")))(lg, q, k, z)


def _ret_bwd(cf, q, k, z, lg, do):
    tq, T = cf.TQ, cf.T
    vb = cf.OFF['rv'] // LANE

    def body(lg_ref, q_ref, k_ref, v_ref, do_ref, dq_ref, dk_ref, dv_ref, dlg_ref):
        h, i = pl.program_id(0), pl.program_id(1)
        mf, mb, rf, rb = _ret_masks(cf, i, tq, lg_ref[0, h], lg_ref[1, h])
        qv, kv, vv = q_ref[...], k_ref[...], v_ref[...].astype(BF16)
        dob = do_ref[...].astype(BF16)
        a = lax.dot_general(qv, kv, (_DN['nt'], ((), ())), preferred_element_type=F32)
        m = mf + mb
        p = (a * m).astype(BF16)
        dv = lax.dot_general(p, dob, (_DN['tn'], ((), ())), preferred_element_type=F32)
        dp = lax.dot_general(dob, vv, (_DN['nt'], ((), ())), preferred_element_type=F32)
        da = (dp * m).astype(BF16)
        dq_ref[...] = jnp.dot(da, kv, preferred_element_type=F32)
        dk = lax.dot_general(da, qv, (_DN['tn'], ((), ())), preferred_element_type=F32)
        dm = dp * a
        dlf = jnp.sum(jnp.sum(dm * mf * rf, axis=-1, keepdims=True), axis=0, keepdims=True)
        dlb = jnp.sum(jnp.sum(dm * mb * rb, axis=-1, keepdims=True), axis=0, keepdims=True)
        rid = lax.broadcasted_iota(jnp.int32, (8, LANE), 0)
        dl = jnp.where(rid == 0, dlf, jnp.where(rid == 1, dlb, 0.0))

        @pl.when(i == 0)
        def _():
            dk_ref[...] = dk
            dv_ref[...] = dv
            dlg_ref[...] = dl

        @pl.when(i != 0)
        def _():
            dk_ref[...] += dk
            dv_ref[...] += dv
            dlg_ref[...] += dl

    qs = pl.BlockSpec((tq, LANE), lambda h, i: (i, h))
    ks = pl.BlockSpec((T, LANE), lambda h, i: (0, h))
    return pl.pallas_call(
        body, name="ret_bwd", grid=(cf.HR, T // tq),
        in_specs=[pl.BlockSpec(memory_space=pltpu.SMEM), qs, ks, pl.BlockSpec((T, LANE), lambda h, i: (0, vb + h)), qs],
        out_specs=[qs, ks, ks, pl.BlockSpec((None, 8, LANE), lambda h, i: (h, 0, 0))],
        out_shape=[jax.ShapeDtypeStruct((T, cf.HR * LANE), F32)] * 3 + [jax.ShapeDtypeStruct((cf.HR, 8, LANE), F32)],
        compiler_params=_cp(("arbitrary", "arbitrary")))(lg, q, k, z, do)


def _gla_step(q, k, v, es, st, lmask, allowed):
    C, SB = GLA_CHUNK, GLA_SUB
    nsb = C // SB
    e1, e2, e3, e4, e5 = es[0], es[1:1 + nsb], es[1 + nsb], es[2 + nsb], es[3 + nsb]
    qs = q * lmask * ((HEAD_DIM // 2) ** -0.5)
    ks = k * lmask
    qt = qs * e1
    rows = [lax.dot_general(qt[b * SB:(b + 1) * SB], ks * e2[b], (_DN['nt'], ((), ())), precision=lax.Precision.HIGH,
                            preferred_element_type=F32) for b in range(nsb)]
    att = jnp.where(allowed, jnp.concatenate(rows, axis=0), 0.0)
    o = jnp.dot(att.astype(BF16), v.astype(BF16), preferred_element_type=F32)
    o += lax.dot_general((qs * e3).astype(BF16), st.astype(BF16), (_DN['nt'], ((), ())), preferred_element_type=F32)
    kd = (ks * e4).astype(BF16)
    st_new = st * jnp.concatenate([e5, e5], axis=0) + lax.dot_general(v.astype(BF16), kd, (_DN['tn'], ((), ())), preferred_element_type=F32)
    return o, st_new


def _gla_allowed(d):
    r = lax.broadcasted_iota(jnp.int32, (GLA_CHUNK, GLA_CHUNK), 0)
    m = lax.broadcasted_iota(jnp.int32, (GLA_CHUNK, GLA_CHUNK), 1)
    return (m <= r) if d == 0 else (m >= r)


def _gla_chunk_id(cf, s, d):
    if d == 0:
        return s
    nct, nc = cf.LC // GLA_CHUNK, cf.T // GLA_CHUNK
    return jnp.where(s < nct, nct - 1 - s, nc + nct - 1 - s)


def _gla_lmask(h):
    return (lax.broadcasted_iota(jnp.int32, (1, LANE), 1) // (LANE // 2) == h).astype(F32)


_GLA_CHAINS = [(h, d) for h in range(2) for d in range(2)]


def _gla_row_specs(cf, nc, reverse):
    def rowblk(s, d):
        return _gla_chunk_id(cf, nc - 1 - s if reverse else s, d)

    def spec(width, base, d, per_pair=1):
        return pl.BlockSpec((GLA_CHUNK, width), functools.partial(lambda p, s, base, d: (rowblk(s, d), base + per_pair * p), base=base, d=d))

    def state(d):
        return pl.BlockSpec((2, None, LANE, LANE), functools.partial(lambda p, s, d: (p, rowblk(s, d), 0, 0), d=d))

    return spec, state


def _gla_fwd(cf, z, ef, eb):
    T, C = cf.T, GLA_CHUNK
    nc = T // C
    qb, kb, vb = cf.OFF['gq'] // LANE, cf.OFF['gk'] // LANE, cf.OFF['gv'] // (2 * LANE)
    spec, state = _gla_row_specs(cf, nc, False)

    def body(qf, kf, vf, e_f, qb_, kb_, vb_, e_b, of, sf, ob, sb, st_scr):
        @pl.when(pl.program_id(1) == 0)
        def _():
            st_scr[...] = jnp.zeros_like(st_scr)

        io = [(qf, kf, vf, e_f, of, sf), (qb_, kb_, vb_, e_b, ob, sb)]
        for ci, (h, d) in enumerate(_GLA_CHAINS):
            q, k, v, e, o_ref, s_ref = io[d]
            cols = slice(h * LANE, (h + 1) * LANE)
            es = [e[:, n * LANE:(n + 1) * LANE] for n in range(N_DECAY)]
            st = st_scr[ci]
            s_ref[h] = st
            o, stn = _gla_step(q[...], k[...], v[:, cols], es, st, _gla_lmask(h), _gla_allowed(d))
            st_scr[ci] = stn
            o_ref[:, cols] = o

    ins, outs = [], []
    for d in range(2):
        ins += [spec(LANE, qb, d), spec(LANE, kb, d), spec(2 * LANE, vb, d), spec(N_DECAY * LANE, 0, d)]
        outs += [spec(2 * LANE, 0, d), state(d)]
    oshape = [jax.ShapeDtypeStruct((T, cf.HG * LANE), F32), jax.ShapeDtypeStruct((cf.HG, nc, LANE, LANE), F32)]
    return pl.pallas_call(
        body, name="gla_fwd", grid=(cf.HG // 2, nc), in_specs=ins, out_specs=outs, out_shape=oshape * 2,
        scratch_shapes=[pltpu.VMEM((4, LANE, LANE), F32)],
        compiler_params=_cp(("arbitrary", "arbitrary")))(z, z, z, ef, z, z, z, eb)


def _gla_bwd(cf, z, ef, eb, sf, sb, do):
    T, C = cf.T, GLA_CHUNK
    nc = T // C
    qb, kb, vb = cf.OFF['gq'] // LANE, cf.OFF['gk'] // LANE, cf.OFF['gv'] // (2 * LANE)
    spec, state = _gla_row_specs(cf, nc, True)

    def body(*refs):
        ins = [refs[0:6], refs[6:12]]
        outs = [refs[12:16], refs[16:20]]
        dst_scr = refs[20]

        @pl.when(pl.program_id(1) == 0)
        def _():
            dst_scr[...] = jnp.zeros_like(dst_scr)

        acc = [None, None]
        for ci, (h, d) in enumerate(_GLA_CHAINS):
            q, k, v, e, s_ref, do_ref = ins[d]
            cols = slice(h * LANE, (h + 1) * LANE)
            es = [e[:, n * LANE:(n + 1) * LANE] for n in range(N_DECAY)]
            step = functools.partial(_gla_step, lmask=_gla_lmask(h), allowed=_gla_allowed(d))
            _, vjp_fn = jax.vjp(step, q[...], k[...], v[:, cols], es, s_ref[h])
            dq, dk, dv, des, dst = vjp_fn((do_ref[:, cols], dst_scr[ci]))
            dst_scr[ci] = dst
            outs[d][2][:, cols] = dv
            part = [dq, dk] + list(des)
            acc[d] = part if acc[d] is None else [a + b for a, b in zip(acc[d], part)]
        for d in range(2):
            dq_ref, dk_ref, _, de_ref = outs[d]
            dq_ref[...] = acc[d][0]
            dk_ref[...] = acc[d][1]
            for n in range(N_DECAY):
                de_ref[:, n * LANE:(n + 1) * LANE] = acc[d][2 + n]

    in_specs, out_specs = [], []
    for d in range(2):
        in_specs += [spec(LANE, qb, d), spec(LANE, kb, d), spec(2 * LANE, vb, d), spec(N_DECAY * LANE, 0, d), state(d), spec(2 * LANE, 0, d)]
        out_specs += [spec(LANE, 0, d), spec(LANE, 0, d), spec(2 * LANE, 0, d), spec(N_DECAY * LANE, 0, d)]
    npair = cf.HG // 2
    oshape = [jax.ShapeDtypeStruct((T, npair * LANE), F32), jax.ShapeDtypeStruct((T, npair * LANE), F32),
              jax.ShapeDtypeStruct((T, cf.HG * LANE), F32), jax.ShapeDtypeStruct((T, npair * N_DECAY * LANE), F32)]
    return pl.pallas_call(
        body, name="gla_bwd", grid=(npair, nc), in_specs=in_specs, out_specs=out_specs, out_shape=oshape * 2,
        scratch_shapes=[pltpu.VMEM((4, LANE, LANE), F32)],
        compiler_params=_cp(("arbitrary", "arbitrary")))(z, z, z, ef, sf, do, z, z, z, eb, sb, do)


def _conv_parts(cf, a, w_ref):
    T, LC = cf.T, cf.LC
    rid = lax.broadcasted_iota(jnp.int32, a.shape, 0)
    first = jnp.logical_or(rid == 0, rid == LC)
    last = jnp.logical_or(rid == LC - 1, rid == T - 1)
    ap = jnp.where(first, 0.0, pltpu.roll(a, 1, axis=0))
    an = jnp.where(last, 0.0, pltpu.roll(a, T - 1, axis=0))
    w0, w1, w2, b = w_ref[0:1, :], w_ref[1:2, :], w_ref[2:3, :], w_ref[3:4, :]
    ac = ap * w0 + a * w1 + an * w2 + b
    return ap, an, ac, first, last, (w0, w1, w2)


def _conv_fwd(cf, u, wb):
    T, Fd = cf.T, cf.F
    tc = _tile(Fd, 512)
    nj = Fd // tc

    def body(a_ref, v_ref, w_ref, t_ref):
        _, _, ac, _, _, _ = _conv_parts(cf, a_ref[...], w_ref)
        t_ref[...] = (_silu(ac) * v_ref[...]).astype(t_ref.dtype)

    return pl.pallas_call(
        body, name="conv_fwd", grid=(nj,),
        in_specs=[pl.BlockSpec((T, tc), lambda j: (0, j)), pl.BlockSpec((T, tc), lambda j: (0, nj + j)),
                  pl.BlockSpec((8, tc), lambda j: (0, j))],
        out_specs=pl.BlockSpec((T, tc), lambda j: (0, j)), out_shape=jax.ShapeDtypeStruct((T, Fd), BF16),
        compiler_params=_cp(("parallel",)))(u, u, wb)


def _conv_bwd(cf, u, wb, dt):
    T, Fd = cf.T, cf.F
    tc = _tile(Fd, 256)
    nj = Fd // tc

    def body(a_ref, v_ref, w_ref, dt_ref, da_ref, dv_ref, dw_ref):
        a, v, dtv = a_ref[...], v_ref[...], dt_ref[...].astype(F32)
        ap, an, ac, first, last, (w0, w1, w2) = _conv_parts(cf, a, w_ref)
        sg = _sigmoid(ac)
        dv_ref[...] = (dtv * ac * sg).astype(dv_ref.dtype)
        dac = dtv * v * (sg * (1.0 + ac * (1.0 - sg)))
        from_next = pltpu.roll(jnp.where(first, 0.0, dac), T - 1, axis=0)
        from_prev = pltpu.roll(jnp.where(last, 0.0, dac), 1, axis=0)
        da_ref[...] = (dac * w1 + from_next * w0 + from_prev * w2).astype(da_ref.dtype)
        rows = [jnp.sum(dac * ap, axis=0, keepdims=True), jnp.sum(dac * a, axis=0, keepdims=True),
                jnp.sum(dac * an, axis=0, keepdims=True), jnp.sum(dac, axis=0, keepdims=True)]
        rid = lax.broadcasted_iota(jnp.int32, (8, tc), 0)
        dw = jnp.zeros((8, tc), F32)
        for n_, rw in enumerate(rows):
            dw = jnp.where(rid == n_, rw, dw)
        dw_ref[...] = dw

    col = pl.BlockSpec((T, tc), lambda j: (0, j))
    return pl.pallas_call(
        body, name="conv_bwd", grid=(nj,),
        in_specs=[col, pl.BlockSpec((T, tc), lambda j: (0, nj + j)), pl.BlockSpec((8, tc), lambda j: (0, j)), col],
        out_specs=[col, col, pl.BlockSpec((8, tc), lambda j: (0, j))],
        out_shape=[jax.ShapeDtypeStruct((T, Fd), BF16), jax.ShapeDtypeStruct((T, Fd), BF16), jax.ShapeDtypeStruct((8, Fd), F32)],
        compiler_params=_cp(("parallel",)))(u, u, wb, dt)


def _me():
    x, y, c = lax.axis_index("x"), lax.axis_index("y"), lax.axis_index("c")
    return x, y, c, 4 * x + 2 * y + c


def _peer(x, y, c, k):
    px = 1 - x if (k >> 2) & 1 else x
    py = 1 - y if (k >> 1) & 1 else y
    pc = 1 - c if k & 1 else c
    return (px, py, pc), 4 * px + 2 * py + pc


def _rcopy(src, dst, ss, rs, tgt):
    return pltpu.make_async_remote_copy(src_ref=src, dst_ref=dst, send_sem=ss, recv_sem=rs, device_id=tgt,
                                        device_id_type=pl.DeviceIdType.MESH)


def _ag_small(name, v):
    R, Cc = v.shape

    def body(v_ref, o_ref, ssem, rsem, lsem):
        x, y, c, me = _me()
        loc = pltpu.make_async_copy(v_ref, o_ref.at[me], lsem)
        loc.start()
        sends = []
        for k in range(1, N_DEV):
            tgt, _ = _peer(x, y, c, k)
            cp = _rcopy(v_ref, o_ref.at[me], ssem.at[k - 1], rsem.at[k - 1], tgt)
            cp.start()
            sends.append(cp)
        for k in range(1, N_DEV):
            tgt, pi = _peer(x, y, c, k)
            _rcopy(v_ref, o_ref.at[pi], ssem.at[k - 1], rsem.at[k - 1], tgt).wait_recv()
        for cp in sends:
            cp.wait_send()
        loc.wait()

    vm = pl.BlockSpec(memory_space=pltpu.VMEM)
    return pl.pallas_call(
        body, name=name, in_specs=[vm], out_specs=vm, out_shape=jax.ShapeDtypeStruct((N_DEV, R, Cc), v.dtype),
        scratch_shapes=[pltpu.SemaphoreType.DMA((N_DEV - 1,)), pltpu.SemaphoreType.DMA((N_DEV - 1,)), pltpu.SemaphoreType.DMA],
        compiler_params=pltpu.CompilerParams(vmem_limit_bytes=VMEM_LIMIT))(v)


_HBM = pl.BlockSpec(memory_space=pltpu.HBM)
_SEM = pl.BlockSpec(memory_space=pltpu.SEMAPHORE)
_EFFECT = pltpu.SideEffectType.DATAFLOW_SIDE_EFFECTING
_KINDS = ['in', 'out', 'up', 'down']


def _hbm(a):
    return pltpu.with_memory_space_constraint(a, pltpu.HBM)


def _shard_shape(cf, kind):
    D, Fd = cf.D, cf.F
    return {'in': (D, cf.NINS), 'out': (D // N_DEV, D), 'up': (D, 2 * Fd // N_DEV), 'down': (Fd // N_DEV, D)}[kind]


def _whole_shape(cf, kind):
    D, Fd = cf.D, cf.F
    return {'in': (N_DEV, D, cf.NINS), 'out': (D, D), 'up': (D, 2 * Fd), 'down': (Fd, D)}[kind]


def _part(ref, cf, kind, idx):
    r, cdim = _shard_shape(cf, kind)
    if kind == 'in':
        return ref.at[idx]
    if kind == 'up':
        return ref.at[:, pl.ds(pl.multiple_of(idx * cdim, cdim), cdim)]
    return ref.at[pl.ds(pl.multiple_of(idx * r, r), r), :]


def _ag_start(cf, shards, after):
    npc = DEPTH * len(_KINDS)
    nio = len(_KINDS) + npc

    def body(*refs):
        srcs, lands = refs[:len(_KINDS)], refs[len(_KINDS):nio]
        ssems, rsems = refs[2 * nio + 1:2 * nio + 1 + npc], refs[2 * nio + 1 + npc:2 * nio + 1 + 2 * npc]
        token, lsem = refs[2 * nio + 1 + 2 * npc], refs[2 * nio + 2 + 2 * npc]
        x, y, c, me = _me()
        pieces = [(l * len(_KINDS) + n, srcs[n].at[l], kind) for l in range(DEPTH) for n, kind in enumerate(_KINDS)]
        locs = [pltpu.make_async_copy(src, _part(lands[p], cf, kind, me), lsem.at[p]) for p, src, kind in pieces]
        for cp in locs:
            cp.start()
        for cp in locs:
            cp.wait()
        for p, src, kind in pieces:
            for k in range(1, N_DEV):
                tgt, _ = _peer(x, y, c, k)
                _rcopy(src, _part(lands[p], cf, kind, me), ssems[p].at[k - 1], rsems[p].at[k - 1], tgt).start()
        token[...] = jnp.zeros_like(token)

    land_shapes = [_whole_shape(cf, kind) for _ in range(DEPTH) for kind in _KINDS]
    out_shape = [pltpu.HBM(s.shape, BF16) for s in shards] + [pltpu.HBM(s, BF16) for s in land_shapes]
    out_shape += [pltpu.SemaphoreType.DMA((N_DEV - 1,))] * (2 * npc) + [jax.ShapeDtypeStruct((8, LANE), F32)]
    res = pl.pallas_call(
        body, name="ag_start", in_specs=[_HBM] * nio + [pl.BlockSpec(memory_space=pl.ANY)],
        out_specs=[_HBM] * nio + [_SEM] * (2 * npc) + [pl.BlockSpec(memory_space=pltpu.VMEM)],
        out_shape=out_shape, input_output_aliases={i: i for i in range(nio)}, scratch_shapes=[pltpu.SemaphoreType.DMA((npc,))],
        compiler_params=pltpu.CompilerParams(has_side_effects=_EFFECT))(
            *[_hbm(s) for s in shards], *[_hbm(lax.empty(s, BF16)) for s in land_shapes], after)
    return res[:len(_KINDS)], res[len(_KINDS):nio], res[nio:nio + npc], res[nio + npc:nio + 2 * npc], res[nio + 2 * npc]


def _ag_wait(cf, kind, l, src, land, ssem, rsem, after):
    def body(src_ref, land_ref, ssem_ref, rsem_ref, after_ref, src_out, land_out):
        x, y, c, me = _me()
        for k in range(1, N_DEV):
            tgt, pi = _peer(x, y, c, k)
            cp = _rcopy(src_ref.at[l], _part(land_ref, cf, kind, pi), ssem_ref.at[k - 1], rsem_ref.at[k - 1], tgt)
            cp.wait_send()
            cp.wait_recv()

    return pl.pallas_call(
        body, name="ag_wait_%s_%d" % (kind, l), in_specs=[_HBM, _HBM, _SEM, _SEM, pl.BlockSpec(memory_space=pl.ANY)],
        out_specs=[_HBM, _HBM], out_shape=[pltpu.HBM(src.shape, src.dtype), pltpu.HBM(land.shape, land.dtype)],
        input_output_aliases={0: 0, 1: 1}, compiler_params=pltpu.CompilerParams(has_side_effects=_EFFECT))(src, land, ssem, rsem, after)


def _rs_slab(ref, cf, kind, j):
    return ref.at[j] if kind in ('in', 'up') else _part(ref, cf, kind, j)


def _rs_start(cf, kind, l, g, after):
    def body(g_ref, recv_ref, after_ref, g_out, recv_out, ssem, rsem, token, lsem):
        x, y, c, me = _me()
        loc = pltpu.make_async_copy(_rs_slab(g_ref, cf, kind, me), recv_ref.at[me], lsem)
        loc.start()
        loc.wait()
        for k in range(1, N_DEV):
            tgt, pi = _peer(x, y, c, k)
            _rcopy(_rs_slab(g_ref, cf, kind, pi), recv_ref.at[me], ssem.at[k - 1], rsem.at[k - 1], tgt).start()
        token[...] = jnp.zeros_like(token)

    rshape = (N_DEV,) + _shard_shape(cf, kind)
    sems = pltpu.SemaphoreType.DMA((N_DEV - 1,))
    return pl.pallas_call(
        body, name="rs_start_%s_%d" % (kind, l), in_specs=[_HBM, _HBM, pl.BlockSpec(memory_space=pl.ANY)],
        out_specs=[_HBM, _HBM, _SEM, _SEM, pl.BlockSpec(memory_space=pltpu.VMEM)],
        out_shape=[pltpu.HBM(g.shape, BF16), pltpu.HBM(rshape, BF16), sems, sems, jax.ShapeDtypeStruct((8, LANE), F32)],
        input_output_aliases={0: 0, 1: 1}, scratch_shapes=[pltpu.SemaphoreType.DMA],
        compiler_params=pltpu.CompilerParams(has_side_effects=_EFFECT))(_hbm(g), _hbm(lax.empty(rshape, BF16)), after)


def _rs_wait(cf, kind, l, g, recv, ssem, rsem, after):
    def body(g_ref, recv_ref, ssem_ref, rsem_ref, after_ref, g_out, recv_out):
        x, y, c, me = _me()
        for k in range(1, N_DEV):
            tgt, pi = _peer(x, y, c, k)
            cp = _rcopy(_rs_slab(g_ref, cf, kind, pi), recv_ref.at[pi], ssem_ref.at[k - 1], rsem_ref.at[k - 1], tgt)
            cp.wait_send()
            cp.wait_recv()

    return pl.pallas_call(
        body, name="rs_wait_%s_%d" % (kind, l), in_specs=[_HBM, _HBM, _SEM, _SEM, pl.BlockSpec(memory_space=pl.ANY)],
        out_specs=[_HBM, _HBM], out_shape=[pltpu.HBM(g.shape, g.dtype), pltpu.HBM(recv.shape, recv.dtype)],
        input_output_aliases={0: 0, 1: 1}, compiler_params=pltpu.CompilerParams(has_side_effects=_EFFECT))(g, recv, ssem, rsem, after)[1]


def _adam_vals(w, g, m, v):
    m2 = ADAM_B1 * m + (1.0 - ADAM_B1) * g
    v2 = ADAM_B2 * v + (1.0 - ADAM_B2) * (g * g)
    mh = m2 / (1.0 - ADAM_B1 ** ADAM_STEP)
    vh = v2 / (1.0 - ADAM_B2 ** ADAM_STEP)
    return -ADAM_LR * (mh / (jnp.sqrt(vh) + ADAM_EPS) + ADAM_WD * w), m2, v2


def _row_tile(R, Cc, budget_elems):
    t = max(16, min(R, (budget_elems // max(Cc, 1)) // 16 * 16))
    while t > 16 and R % t:
        t -= 16
    return t if R % t == 0 else R


def _cast_bf16(name, w):
    Dp, R, Cc = w.shape
    tr = _row_tile(R, Cc, 1 << 20)

    def body(w_ref, o_ref):
        o_ref[...] = w_ref[...].astype(BF16)

    spec = pl.BlockSpec((None, tr, Cc), lambda l, i: (l, i, 0))
    return pl.pallas_call(body, name=name, grid=(Dp, R // tr), in_specs=[spec], out_specs=spec,
                          out_shape=jax.ShapeDtypeStruct(w.shape, BF16), compiler_params=_cp(("parallel", "parallel")))(w)


def _sum_adam(name, recvs, w, m, v):
    Dp, R, Cc = w.shape
    tr = _row_tile(R, Cc, 1 << 18)
    ni = R // tr

    def body(*refs):
        r_refs = refs[:Dp]
        w_ref, m_ref, v_ref, g_ref, d_ref, mo_ref, vo_ref = refs[Dp:]
        for layer in range(Dp):
            @pl.when(pl.program_id(0) == layer)
            def _():
                r_ref = r_refs[layer]
                g = r_ref[0].astype(F32)
                for s in range(1, N_DEV):
                    g = g + r_ref[s].astype(F32)
                dl, m2, v2 = _adam_vals(w_ref[...], g, m_ref[...], v_ref[...])
                g_ref[...] = g
                d_ref[...] = dl
                mo_ref[...] = m2
                vo_ref[...] = v2

    spec = pl.BlockSpec((None, tr, Cc), lambda l, i: (l, i, 0))
    rspecs = [pl.BlockSpec((N_DEV, tr, Cc), functools.partial(
        lambda l, i, layer: (0, jnp.where(l == layer, i, jnp.where(l > layer, ni - 1, 0)), 0), layer=layer)) for layer in range(Dp)]
    return pl.pallas_call(body, name=name, grid=(Dp, ni), in_specs=rspecs + [spec, spec, spec], out_specs=[spec] * 4,
                          out_shape=[jax.ShapeDtypeStruct(w.shape, F32)] * 4, compiler_params=_cp(("arbitrary", "arbitrary")))(*recvs, w, m, v)


def _adam(name, w, g, m, v):
    R, Cc = w.shape
    tr = _row_tile(R, Cc, 1 << 18)

    def body(w_ref, g_ref, m_ref, v_ref, d_ref, mo_ref, vo_ref):
        dl, m2, v2 = _adam_vals(w_ref[...], g_ref[...], m_ref[...], v_ref[...])
        d_ref[...] = dl
        mo_ref[...] = m2
        vo_ref[...] = v2

    spec = pl.BlockSpec((tr, Cc), lambda i: (i, 0))
    return pl.pallas_call(body, name=name, grid=(R // tr,), in_specs=[spec] * 4, out_specs=[spec] * 3,
                          out_shape=[jax.ShapeDtypeStruct(w.shape, F32)] * 3, compiler_params=_cp(("parallel",)))(w, g, m, v)


def _sum8(name, a):
    n, R, Cc = a.shape

    def body(a_ref, o_ref):
        s = a_ref[0]
        for k in range(1, n):
            s = s + a_ref[k]
        o_ref[...] = s

    return pl.pallas_call(body, name=name, in_specs=[pl.BlockSpec(memory_space=pltpu.VMEM)],
                          out_specs=pl.BlockSpec(memory_space=pltpu.VMEM), out_shape=jax.ShapeDtypeStruct((R, Cc), F32),
                          compiler_params=pltpu.CompilerParams(vmem_limit_bytes=VMEM_LIMIT))(a)


def _ada_fwd(cf, c9, ada_w):
    D = cf.D
    NS = ada_w.shape[-1]
    tk = _tile(D, 512)
    nk = D // tk

    def body(c_ref, w_ref, o_ref):
        kk = pl.program_id(1)
        s = _silu(c_ref[...]).astype(BF16)
        part = jnp.dot(s, w_ref[...].astype(BF16), preferred_element_type=F32)

        @pl.when(kk == 0)
        def _():
            o_ref[...] = part

        @pl.when(kk != 0)
        def _():
            o_ref[...] += part

    return pl.pallas_call(
        body, name="ada_fwd", grid=(DEPTH, nk),
        in_specs=[pl.BlockSpec((16, tk), lambda l, k: (0, k)), pl.BlockSpec((None, tk, NS), lambda l, k: (l, k, 0))],
        out_specs=pl.BlockSpec((None, 16, NS), lambda l, k: (l, 0, 0)),
        out_shape=jax.ShapeDtypeStruct((DEPTH, 16, NS), F32), compiler_params=_cp(("parallel", "arbitrary")))(c9, ada_w)


def _ada_bwd(cf, c9, ada_w, dm9):
    D = cf.D
    NS = ada_w.shape[-1]
    tk = _tile(D, 512)
    nk = D // tk

    def body(c_ref, w_ref, dm_ref, gw_ref, ds_ref):
        cv = c_ref[...]
        sg = _sigmoid(cv)
        dmb = dm_ref[...].astype(BF16)
        gw_ref[...] = lax.dot_general((cv * sg).astype(BF16), dmb, (_DN['tn'], ((), ())), preferred_element_type=F32)
        ds = lax.dot_general(dmb, w_ref[...].astype(BF16), (_DN['nt'], ((), ())), preferred_element_type=F32)
        ds_ref[...] = ds * (sg * (1.0 + cv * (1.0 - sg)))

    return pl.pallas_call(
        body, name="ada_bwd", grid=(DEPTH, nk),
        in_specs=[pl.BlockSpec((16, tk), lambda l, k: (0, k)), pl.BlockSpec((None, tk, NS), lambda l, k: (l, k, 0)),
                  pl.BlockSpec((None, 16, NS), lambda l, k: (l, 0, 0))],
        out_specs=[pl.BlockSpec((None, tk, NS), lambda l, k: (l, k, 0)), pl.BlockSpec((None, 16, tk), lambda l, k: (l, 0, k))],
        out_shape=[jax.ShapeDtypeStruct((DEPTH, D, NS), F32), jax.ShapeDtypeStruct((DEPTH, 16, D), F32)],
        compiler_params=_cp(("parallel", "parallel")))(c9, ada_w, dm9)


def _rope_tables(cf):
    L, LC = cf.L, cf.LC
    rows = L // GRID_W
    row = jnp.repeat(jnp.arange(rows, dtype=F32), GRID_W)
    col = jnp.tile(jnp.arange(GRID_W, dtype=F32), rows)
    nf = HEAD_DIM // 4
    inv = ROPE_THETA ** (-jnp.arange(nf, dtype=F32) / nf)
    ang = jnp.concatenate([row[:, None] * inv, col[:, None] * inv], axis=-1)
    cos, sin = jnp.cos(ang), jnp.sin(ang)
    cs = jnp.concatenate([jnp.ones((LC, HEAD_DIM), F32), jnp.concatenate([cos, cos], -1)], 0)
    sn = jnp.concatenate([jnp.zeros((LC, HEAD_DIM), F32), jnp.concatenate([-sin, sin], -1)], 0)
    return cs, sn


def _prep_tiles(cf, z, cs, sn, key):
    b = cf.OFF[key] // LANE
    return [(z, LANE, _col(b), True), (cs, LANE, _c0, False), (sn, LANE, _c0, False)]


_PREP = {'aq': _f_prep_norm, 'ak': _f_prep_norm, 'rq': _f_prep_plain, 'rk': _f_prep_scaled}


def _prep_fwd(cf, z, cs, sn, key, g):
    nh = cf.W[key] // LANE
    params = [(g, 'shared', True)] if g is not None else []
    return _row_fwd(cf, "prep_fwd_" + key, _PREP[key], _prep_tiles(cf, z, cs, sn, key), params,
                    [(LANE, _col(0), cf.W[key], BF16)], cf.TQ, nrep=nh)[0]


def _prep_bwd(cf, z, cs, sn, key, g, dt):
    nh = cf.W[key] // LANE
    params = [(g, 'shared', True)] if g is not None else []
    tg, pg = _row_bwd(cf, "prep_bwd_" + key, _PREP[key], _prep_tiles(cf, z, cs, sn, key), params,
                      [(dt, LANE, _col(0))], [(LANE, _col(0), cf.W[key], BF16)], cf.TQ, nrep=nh)
    return tg[0], (pg[0] if g is not None else None)


def _gate_params(cf, gup, gb):
    K = gup.shape[-1]
    gf = jnp.zeros((LANE, K), F32).at[0:GLA_RANK].set(gup[0])
    gbm = jnp.zeros((LANE, K), F32).at[GLA_RANK:2 * GLA_RANK].set(gup[1])
    out = []
    for p in range(K // LANE):
        cols = slice(p * LANE, (p + 1) * LANE)
        out += [(gf[:, cols], 'shared', True), (gbm[:, cols], 'shared', True), (gb[0:1, cols], 'shared', True), (gb[1:2, cols], 'shared', True)]
    return out


def _mix_tiles(cf, z, o, key):
    return [(o, LANE, _col(0), True), (z, LANE, _col(cf.OFF[key] // LANE), True)]


def _mid_io(cf, l, W, mod, x, y, zero=0.0):
    tiles = [(x, cf.D, _c0, True), (y, cf.D, _c0, True)]
    params = [(mod[2], 'stream', True), (W['norm2_g'][l] + zero, 'shared', True), (mod[3], 'stream', True), (mod[4], 'stream', True)]
    return tiles, params


class _BigWeights:
    def __init__(self, cf, srcs, lands, ssems, rsems):
        self.cf, self.srcs, self.lands, self.ssems, self.rsems = cf, list(srcs), lands, ssems, rsems
        self.whole = {}

    def get(self, kind, l, after=None):
        cf = self.cf
        if (kind, l) not in self.whole:
            n = _KINDS.index(kind)
            p = l * len(_KINDS) + n
            self.srcs[n], w = _ag_wait(cf, kind, l, self.srcs[n], self.lands[p], self.ssems[p], self.rsems[p], after)
            if kind == 'in':
                w = jnp.pad(_unshard_last(w), ((0, 0), (0, cf.NZ - cf.NIN)))
            self.whole[(kind, l)] = w
        return self.whole[(kind, l)]


def _layer_fwd(cf, l, W, big, mod, x, h, cs, sn):
    T, D, Fd = cf.T, cf.D, cf.F
    z = _mm("z_%d" % l, h, big.get('in', l, h), 'nn', T, cf.NZ, D, F32, tm=T, tn=768, tk=512)
    qa = _prep_fwd(cf, z, cs, sn, 'aq', W['q_norm_g'][l])
    ka = _prep_fwd(cf, z, cs, sn, 'ak', W['k_norm_g'][l])
    qr = _prep_fwd(cf, z, cs, sn, 'rq', None)
    kr = _prep_fwd(cf, z, cs, sn, 'rk', None)
    o_att = _att_fwd(cf, qa, ka, z)
    o_ret = _ret_fwd(cf, qr, kr, z, W['ret_log_decay'][l])
    gates = _gate_params(cf, W['gla_gate_up'][l], W['gla_gate_b'][l])
    ga_tile = [(z, LANE, _col(cf.OFF['ga'] // LANE), True)]
    we = (cf.HG // 2) * N_DECAY * LANE
    ef, eb = _row_fwd(cf, "gates_fwd_%d" % l, _f_gla_pre, ga_tile, gates, [(we, _c0, we, F32), (we, _c0, we, F32)], GLA_CHUNK)
    o_f, sf, o_b, sb = _gla_fwd(cf, z, ef, eb)
    o_gla = o_f + o_b
    cat_r = _row_fwd(cf, "mixr_fwd_%d" % l, _f_gated_norm, _mix_tiles(cf, z, o_ret, 'rg'), [(W['ret_norm_g'][l], 'shared', True)],
                     [(LANE, _col(0), cf.HR * LANE, BF16)], cf.TQ, nrep=cf.HR)[0]
    cat_g = _row_fwd(cf, "mixg_fwd_%d" % l, _f_gated_norm, _mix_tiles(cf, z, o_gla, 'gr'), [(W['gla_norm_g'][l], 'shared', True)],
                     [(LANE, _col(0), cf.HG * LANE, BF16)], cf.TQ, nrep=cf.HG)[0]
    cat = jnp.concatenate([o_att, cat_r, cat_g], axis=-1)
    y = _mm("y_%d" % l, cat, big.get('out', l, cat), 'nn', T, D, D, F32, tm=T, tn=1024, tk=512)
    tiles, params = _mid_io(cf, l, W, mod, x, y)
    x1, h2 = _row_fwd(cf, "mid_fwd_%d" % l, _f_resid_norm_mod, tiles, params, [(D, _c0, D, F32), (D, _c0, D, BF16)], cf.TM)
    u = _mm("u_%d" % l, h2, big.get('up', l, h2), 'nn', T, 2 * Fd, D, F32, tm=T, tn=1024, tk=512)
    t = _conv_fwd(cf, u, W['conv_wb'][l])
    yff = _mm("yff_%d" % l, t, big.get('down', l, t), 'nn', T, D, Fd, F32, tm=T, tn=1024, tk=512)
    return dict(x=x, h=h, z=z, qa=qa, ka=ka, qr=qr, kr=kr, ef=ef, eb=eb, sf=sf, sb=sb, o_ret=o_ret, o_gla=o_gla, cat=cat, y=y,
                x1=x1, h2=h2, u=u, t=t, yff=yff, gates=gates)


def _layer_bwd(cf, l, W, big, mod, sv, dx1, dyff, cs, sn):
    T, D, Fd = cf.T, cf.D, cf.F
    g, rs = {}, {}
    rs['down'] = _rs_start(cf, 'down', l, _mm("gwd_%d" % l, sv['t'], dyff, 'tn', Fd, D, T, BF16, tm=1408, tn=2048, tk=1152), cs)
    dt = _mm("dt_%d" % l, dyff, big.get('down', l), 'nt', T, Fd, D, BF16, tm=T, tn=1408, tk=512)
    da, dv, g['conv_wb'] = _conv_bwd(cf, sv['u'], W['conv_wb'][l] + rs['down'][4][0, 0], dt)
    du = jnp.concatenate([da, dv], axis=-1)
    cu = 2 * Fd // N_DEV
    rs['up'] = _rs_start(cf, 'up', l, _mm("gwu_%d" % l, sv['h2'], du, 'tn', D, 2 * Fd, T, BF16, tm=D, tn=cu, tk=1152, out_shape=(N_DEV, D, cu),
                                          out_spec=pl.BlockSpec((None, D, cu), lambda i, j, k: (j, i, 0))), cs)
    dh2 = _mm("dh2_%d" % l, du, big.get('up', l), 'nt', T, D, 2 * Fd, BF16, tm=T, tn=1024, tk=1024)
    tiles, params = _mid_io(cf, l, W, mod, sv['x'], sv['y'], rs['up'][4][0, 0])
    (dx, dy), (g['m2'], g['norm2_g'], g['m3'], g['m4']) = _row_bwd(
        cf, "mid_bwd_%d" % l, _f_resid_norm_mod, tiles, params, [(dx1, D, _c0), (dh2, D, _c0)],
        [(D, _c0, D, F32), (D, _c0, D, BF16)], cf.TM)
    rs['out'] = _rs_start(cf, 'out', l, _mm("gwo_%d" % l, sv['cat'], dy, 'tn', D, D, T, BF16, tm=D, tn=1024, tk=1152), cs)
    dcat = _mm("dcat_%d" % l, dy, big.get('out', l), 'nt', T, D, D, BF16, tm=T, tn=1024, tk=1024)
    z = sv['z']
    (do_ret, drg), (g['ret_norm_g'],) = _row_bwd(
        cf, "mixr_bwd_%d" % l, _f_gated_norm, _mix_tiles(cf, z, sv['o_ret'], 'rg'), [(W['ret_norm_g'][l] + rs['out'][4][0, 0], 'shared', True)],
        [(dcat, LANE, _col(cf.HQ))], [(LANE, _col(0), cf.HR * LANE, F32), (LANE, _col(0), cf.HR * LANE, BF16)], cf.TQ, nrep=cf.HR)
    (do_gla, dgr), (g['gla_norm_g'],) = _row_bwd(
        cf, "mixg_bwd_%d" % l, _f_gated_norm, _mix_tiles(cf, z, sv['o_gla'], 'gr'), [(W['gla_norm_g'][l], 'shared', True)],
        [(dcat, LANE, _col(cf.HQ + cf.HR))], [(LANE, _col(0), cf.HG * LANE, F32), (LANE, _col(0), cf.HG * LANE, BF16)], cf.TQ, nrep=cf.HG)
    dqa, dka, dav = _att_bwd(cf, sv['qa'], sv['ka'], z, dcat)
    dqr, dkr, drv, dlg = _ret_bwd(cf, sv['qr'], sv['kr'], z, W['ret_log_decay'][l], do_ret)
    g['ret_log_decay'] = dlg[:, 0:2, 0].T
    dq_f, dk_f, dv_f, def_, dq_b, dk_b, dv_b, deb = _gla_bwd(cf, z, sv['ef'], sv['eb'], sv['sf'], sv['sb'], do_gla)
    dgq, dgk, dgv = dq_f + dq_b, dk_f + dk_b, dv_f + dv_b
    we = (cf.HG // 2) * N_DECAY * LANE
    ga_tile = [(z, LANE, _col(cf.OFF['ga'] // LANE), True)]
    (dga,), gg = _row_bwd(cf, "gates_bwd_%d" % l, _f_gla_pre, ga_tile, sv['gates'],
                          [(def_, we, _c0), (deb, we, _c0)], [(LANE, _c0, LANE, BF16)], GLA_CHUNK)
    ggf, ggb, gbf, gbb = [jnp.concatenate(gg[n::4], axis=-1) for n in range(4)]
    g['gla_gate_up'] = jnp.stack([ggf[0:GLA_RANK], ggb[GLA_RANK:2 * GLA_RANK]])
    g['gla_gate_b'] = jnp.concatenate([gbf, gbb], axis=0)
    daq, g['q_norm_g'] = _prep_bwd(cf, z, cs, sn, 'aq', W['q_norm_g'][l], dqa)
    dak, g['k_norm_g'] = _prep_bwd(cf, z, cs, sn, 'ak', W['k_norm_g'][l], dka)
    drq, _ = _prep_bwd(cf, z, cs, sn, 'rq', None, dqr)
    drk, _ = _prep_bwd(cf, z, cs, sn, 'rk', None, dkr)
    pad = jnp.zeros((T, cf.NZ - cf.OFF['ga'] - LANE), BF16)
    dz = jnp.concatenate([daq, dak, dav.astype(BF16), drq, drk, drv.astype(BF16), drg, dgq.astype(BF16), dgk.astype(BF16),
                          dgv.astype(BF16), dgr, dga, pad], axis=-1)
    gwi = _mm("gwi_%d" % l, sv['h'], dz, 'tn', D, cf.NZ, T, BF16, tm=D, tn=768, tk=1152)
    g['w_in_slabs'] = jnp.moveaxis(gwi[:, :cf.NIN].reshape(D, N_DEV, cf.NINS), 1, 0)
    dh = _mm("dh_%d" % l, dz, big.get('in', l), 'nt', T, D, cf.NZ, BF16, tm=T, tn=1024, tk=768)
    g['rs'] = rs
    return dx, dh, g


_WEIGHTS = ['c_ctx', 'ada_w', 'ada_b', 'norm1_g', 'w_in', 'q_norm_g', 'k_norm_g', 'ret_log_decay', 'ret_norm_g',
            'gla_gate_up', 'gla_gate_b', 'gla_norm_g', 'w_out', 'norm2_g', 'w_up', 'conv_w', 'conv_b', 'w_down', 'final_norm_g']
_BIG = ['w_in', 'w_out', 'w_up', 'w_down']
_SMALL = [n for n in _WEIGHTS if n not in _BIG and n != 'ada_w']
_COL_SHARDED = ['gla_gate_up', 'gla_gate_b', 'conv_w']


def _pack(arrs):
    flat = jnp.concatenate([a.reshape(-1) for a in arrs])
    n = flat.shape[0]
    tot = -(-n // (8 * LANE)) * (8 * LANE)
    return jnp.pad(flat, (0, tot - n)).reshape(tot // LANE, LANE)


def _unpack(flat, shapes):
    out, o = [], 0
    for s in shapes:
        n = int(np.prod(s))
        out.append(flat[..., o:o + n].reshape(flat.shape[:-1] + tuple(s)))
        o += n
    return out


def _unshard_last(a):
    return jnp.moveaxis(a, 0, -2).reshape(a.shape[1:-1] + (N_DEV * a.shape[-1],))


def _step(cf, x, c, ctx, loss_target, w, m, v):
    T, D, Fd, L, LC = cf.T, cf.D, cf.F, cf.L, cf.LC
    _, _, _, me = _me()
    NS = w['ada_w'].shape[-1]

    c_all = _ag_small("ag_c", jnp.pad(c, ((0, 7), (0, 0))))[:, 0, :]
    c9 = jnp.concatenate([c_all, w['c_ctx'][None], jnp.zeros((7, D), F32)], axis=0)
    pm = _ada_fwd(cf, c9, w['ada_w'])
    pm_all = _ag_small("ag_mod", pm.reshape(DEPTH * 16, NS)).reshape(N_DEV, DEPTH, 16, NS)
    mod_all = _unshard_last(pm_all) + w['ada_b'][:, None, :]
    mod_own = lax.dynamic_index_in_dim(mod_all, me, axis=1, keepdims=False)
    mods = []
    for l in range(DEPTH):
        mods.append([jnp.stack([mod_all[l, 8, k * D:(k + 1) * D], mod_own[l, k * D:(k + 1) * D]])[:, None, :] for k in range(N_MOD)])

    shard_shapes = [w[n].shape for n in _COL_SHARDED]
    got = _ag_small("ag_smallw", _pack([w[n] for n in _COL_SHARDED]))
    full = dict(zip(_COL_SHARDED, [_unshard_last(a) for a in _unpack(got.reshape(N_DEV, -1), shard_shapes)]))

    srcs, lands, ssems, rsems, ag_tok = _ag_start(cf, [_cast_bf16("cast_" + n, w[n]) for n in _BIG], full['conv_w'] + mod_all[0, 0, 0])
    big = _BigWeights(cf, srcs, lands, ssems, rsems)
    conv_wb = jnp.concatenate([full['conv_w'], w['conv_b'][:, None, :], jnp.zeros((DEPTH, 4, Fd), F32)], axis=1)
    W = dict(conv_wb=conv_wb, gla_gate_up=full['gla_gate_up'], gla_gate_b=full['gla_gate_b'], ret_log_decay=w['ret_log_decay'])
    for n in ['q_norm_g', 'k_norm_g', 'ret_norm_g', 'gla_norm_g', 'norm1_g', 'norm2_g']:
        W[n] = w[n][:, None, :]

    cs, sn = _rope_tables(cf)
    x0 = jnp.concatenate([ctx[0], x[0]], axis=0)
    pre_tiles = [(x0, D, _c0, True)]

    def pre_params(zero):
        return [(W['norm1_g'][0] + zero, 'shared', True), (mods[0][0], 'stream', True), (mods[0][1], 'stream', True)]

    def tr_params(zero):
        return [(mods[0][5], 'stream', True), (W['norm1_g'][1] + zero, 'shared', True), (mods[1][0], 'stream', True), (mods[1][1], 'stream', True)]

    h0 = _row_fwd(cf, "pre_fwd", _f_norm_mod, pre_tiles, pre_params(ag_tok[0, 0]), [(D, _c0, D, BF16)], cf.TM)[0]
    sv0 = _layer_fwd(cf, 0, W, big, mods[0], x0, h0, cs, sn)
    tr_tiles = [(sv0['x1'], D, _c0, True), (sv0['yff'], D, _c0, True)]
    xb, hb = _row_fwd(cf, "tr_fwd", _f_resid_norm_mod, tr_tiles, tr_params(0.0), [(D, _c0, D, F32), (D, _c0, D, BF16)], cf.TM)
    sv1 = _layer_fwd(cf, 1, W, big, mods[1], xb, hb, cs, sn)
    tgt = jnp.concatenate([jnp.zeros((LC, D), F32), loss_target[0]], axis=0)
    dx1, dyff, dm5_1, g_final, ls = _loss_grad(cf, sv1['x1'], sv1['yff'], mods[1][5], w['final_norm_g'][None], tgt)
    loss = lax.psum(ls[0, 0], ("x", "y", "c"))

    dxb, dhb, g1 = _layer_bwd(cf, 1, W, big, mods[1], sv1, dx1, dyff, cs, sn)
    g1['rs']['in'] = _rs_start(cf, 'in', 1, g1['w_in_slabs'], cs)
    (dx1_0, dyff_0), (dm5_0, gn1_1, dm0_1, dm1_1) = _row_bwd(
        cf, "tr_bwd", _f_resid_norm_mod, tr_tiles, tr_params(g1['rs']['in'][4][0, 0]), [(dxb, D, _c0), (dhb, D, _c0)],
        [(D, _c0, D, F32), (D, _c0, D, BF16)], cf.TM)
    dx0, dh0, g0 = _layer_bwd(cf, 0, W, big, mods[0], sv0, dx1_0, dyff_0, cs, sn)
    (dxa,), (gn1_0, dm0_0, dm1_0) = _row_bwd(cf, "pre_bwd", _f_first, pre_tiles, pre_params(0.0), [(dx0, D, _c0), (dh0, D, _c0)],
                                            [(D, _c0, D, F32)], cf.TM)
    grad_x = dxa[LC:][None]

    dmod = jnp.stack([jnp.concatenate([dm0_0, dm1_0, g0['m2'], g0['m3'], g0['m4'], dm5_0], axis=-1)[:, 0],
                      jnp.concatenate([dm0_1, dm1_1, g1['m2'], g1['m3'], g1['m4'], dm5_1], axis=-1)[:, 0]])
    dm_all = _ag_small("ag_dmod", jnp.pad(dmod.reshape(2 * DEPTH, N_MOD * D), ((0, 8 - 2 * DEPTH), (0, 0))))
    dm_all = dm_all[:, :2 * DEPTH].reshape(N_DEV, DEPTH, 2, N_MOD * D)
    dctx = _sum8("sum_dmodc", jnp.pad(dm_all[:, :, 0], ((0, 0), (0, 8 - DEPTH), (0, 0))))[:DEPTH]
    dm9 = jnp.concatenate([jnp.moveaxis(dm_all[:, :, 1], 0, 1), dctx[:, None]], axis=1)
    g_ada_b = _sum8("sum_adab", jnp.pad(jnp.moveaxis(dm9, 1, 0), ((0, 0), (0, 8 - DEPTH), (0, 0))))[:DEPTH]
    dm9s = lax.dynamic_slice_in_dim(jnp.pad(dm9, ((0, 0), (0, 7), (0, 0))), me * NS, NS, axis=2)
    g_ada_w, dsil = _ada_bwd(cf, c9, w['ada_w'], dm9s)
    g_cctx_part = dsil[0, 8]
    for l in range(1, DEPTH):
        g_cctx_part = g_cctx_part + dsil[l, 8]

    def both(key):
        return jnp.stack([g0[key], g1[key]])

    gsmall = dict(c_ctx=g_cctx_part, norm1_g=jnp.stack([gn1_0[0], gn1_1[0]]), q_norm_g=both('q_norm_g')[:, 0],
                  k_norm_g=both('k_norm_g')[:, 0], ret_log_decay=both('ret_log_decay'), ret_norm_g=both('ret_norm_g')[:, 0],
                  gla_gate_up=both('gla_gate_up'), gla_gate_b=both('gla_gate_b'), gla_norm_g=both('gla_norm_g')[:, 0],
                  norm2_g=both('norm2_g')[:, 0], conv_w=both('conv_wb')[:, 0:3], conv_b=both('conv_wb')[:, 3], final_norm_g=g_final[0])
    snames = [n for n in _SMALL if n != 'ada_b']
    sshapes = [gsmall[n].shape for n in snames]
    gs_all = _ag_small("ag_gsmall", _pack([gsmall[n] for n in snames]))
    gs = dict(zip(snames, _unpack(_sum8("sum_gsmall", gs_all).reshape(-1), sshapes)))
    gs['ada_b'] = g_ada_b
    for n in _COL_SHARDED:
        ns_ = w[n].shape[-1]
        gs[n] = lax.dynamic_slice_in_dim(gs[n], me * ns_, ns_, axis=gs[n].ndim - 1)

    g0['rs']['in'] = _rs_start(cf, 'in', 0, g0['w_in_slabs'], gs_all)
    out_g, out_d, out_m, out_v = {}, {}, {}, {}
    after = g0['rs']['in'][4]

    def update_big(kind, after):
        n = 'w_' + kind
        recvs = [_rs_wait(cf, kind, l, *gl['rs'][kind][:4], after) for l, gl in enumerate((g0, g1))]
        out_g[n], out_d[n], out_m[n], out_v[n] = _sum_adam("adam_" + n, recvs, w[n], m[n], v[n])
        return out_g[n]

    for kind in ['down', 'up', 'out']:
        after = update_big(kind, after)
    aw = [a.reshape(DEPTH * D, NS) for a in (w['ada_w'], g_ada_w, m['ada_w'], v['ada_w'])]
    out_g['ada_w'] = g_ada_w
    out_d['ada_w'], out_m['ada_w'], out_v['ada_w'] = [a.reshape(DEPTH, D, NS) for a in _adam("adam_ada_w", *aw)]
    shp = [w[n].shape for n in _SMALL]
    packed = [_pack([src[n] for n in _SMALL]) for src in (w, gs, m, v)]
    res = _adam("adam_small", *packed)
    for dst, pk in zip((out_d, out_m, out_v), res):
        dst.update(zip(_SMALL, _unpack(pk.reshape(-1), shp)))
    out_g.update({n: gs[n] for n in _SMALL})
    update_big('in', out_d['ada_w'])
    return (loss, grad_x, *[out_g[n] for n in _WEIGHTS], *[out_d[n] for n in _WEIGHTS], *[out_m[n] for n in _WEIGHTS],
            *[out_v[n] for n in _WEIGHTS])


def kernel(x, c, ctx, c_ctx, ada_w, ada_b, norm1_g, w_in, q_norm_g, k_norm_g, ret_log_decay, ret_norm_g, gla_gate_up, gla_gate_b, gla_norm_g, w_out, norm2_g, w_up, conv_w, conv_b, w_down, final_norm_g, loss_target, m_c_ctx, m_ada_w, m_ada_b, m_norm1_g, m_w_in, m_q_norm_g, m_k_norm_g, m_ret_log_decay, m_ret_norm_g, m_gla_gate_up, m_gla_gate_b, m_gla_norm_g, m_w_out, m_norm2_g, m_w_up, m_conv_w, m_conv_b, m_w_down, m_final_norm_g, v_c_ctx, v_ada_w, v_ada_b, v_norm1_g, v_w_in, v_q_norm_g, v_k_norm_g, v_ret_log_decay, v_ret_norm_g, v_gla_gate_up, v_gla_gate_b, v_gla_norm_g, v_w_out, v_norm2_g, v_w_up, v_conv_w, v_conv_b, v_w_down, v_final_norm_g):
    w = dict(c_ctx=c_ctx, ada_w=ada_w, ada_b=ada_b, norm1_g=norm1_g, w_in=w_in, q_norm_g=q_norm_g, k_norm_g=k_norm_g,
             ret_log_decay=ret_log_decay, ret_norm_g=ret_norm_g, gla_gate_up=gla_gate_up, gla_gate_b=gla_gate_b,
             gla_norm_g=gla_norm_g, w_out=w_out, norm2_g=norm2_g, w_up=w_up, conv_w=conv_w, conv_b=conv_b, w_down=w_down,
             final_norm_g=final_norm_g)
    m = dict(c_ctx=m_c_ctx, ada_w=m_ada_w, ada_b=m_ada_b, norm1_g=m_norm1_g, w_in=m_w_in, q_norm_g=m_q_norm_g,
             k_norm_g=m_k_norm_g, ret_log_decay=m_ret_log_decay, ret_norm_g=m_ret_norm_g, gla_gate_up=m_gla_gate_up,
             gla_gate_b=m_gla_gate_b, gla_norm_g=m_gla_norm_g, w_out=m_w_out, norm2_g=m_norm2_g, w_up=m_w_up,
             conv_w=m_conv_w, conv_b=m_conv_b, w_down=m_w_down, final_norm_g=m_final_norm_g)
    v = dict(c_ctx=v_c_ctx, ada_w=v_ada_w, ada_b=v_ada_b, norm1_g=v_norm1_g, w_in=v_w_in, q_norm_g=v_q_norm_g,
             k_norm_g=v_k_norm_g, ret_log_decay=v_ret_log_decay, ret_norm_g=v_ret_norm_g, gla_gate_up=v_gla_gate_up,
             gla_gate_b=v_gla_gate_b, gla_norm_g=v_gla_norm_g, w_out=v_w_out, norm2_g=v_norm2_g, w_up=v_w_up,
             conv_w=v_conv_w, conv_b=v_conv_b, w_down=v_w_down, final_norm_g=v_final_norm_g)
    return _step(_cfg(), x, c, ctx, loss_target, w, m, v)
```

```python
import functools
import math
import types

import jax
import jax.numpy as jnp
import numpy as np
from jax import lax
from jax.experimental import pallas as pl
from jax.experimental.pallas import tpu as pltpu
from jax.experimental.pallas import tpu_sc as plsc

F32 = jnp.float32
BF16 = jnp.bfloat16
HI = lax.Precision.HIGHEST

D_MODEL = 2048
SEQ = 2048
CTX_LEN = 256
GRID_W = 64
D_FF = 5632
DEPTH = 2
N_DEV = 8
HEAD_DIM = 128
ROPE_THETA = 10000.0
GLA_TAU = 16.0
GLA_RANK = 16
GLA_CHUNK = 64
GLA_SUB = 16
EPS = 1e-6
N_MOD = 6
ADAM_LR = 0.001
ADAM_B1 = 0.9
ADAM_B2 = 0.999
ADAM_EPS = 1e-08
ADAM_WD = 0.01
ADAM_STEP = 10
LANE = 128
VMEM_LIMIT = 56 * 1024 * 1024
NEG = -1e30


def _cfg():
    d = types.SimpleNamespace()
    d.D, d.L, d.LC, d.F = D_MODEL, SEQ, CTX_LEN, D_FF
    d.T = d.L + d.LC
    nm = d.D // HEAD_DIM
    d.HQ, d.HKV, d.HR, d.HG = nm // 2, nm // 8, nm // 4, nm // 4
    d.G = d.HQ // d.HKV
    w = dict(aq=d.HQ * 128, ak=d.HKV * 128, av=d.HKV * 128, rq=d.HR * 128, rk=d.HR * 128, rv=d.HR * 128,
             rg=d.HR * 128, gq=d.HG * 64, gk=d.HG * 64, gv=d.HG * 128, gr=d.HG * 128, ga=2 * GLA_RANK)
    off, o = {}, 0
    for k, v in w.items():
        off[k] = o
        o += v
    d.W, d.OFF, d.NIN = w, off, o
    d.NZ = -(-(off['ga'] + LANE) // 256) * 256
    d.NINS = d.NIN // N_DEV
    d.TM = math.gcd(d.LC, 128)
    d.TQ = math.gcd(d.LC, 256)
    return d


def _cp(sem=None):
    return pltpu.CompilerParams(dimension_semantics=sem, vmem_limit_bytes=VMEM_LIMIT)


def _tile(n, target, mult=LANE):
    t = min(n, target)
    t -= t % mult
    while t > mult and n % t:
        t -= mult
    return t if t > 0 and n % t == 0 else n


_DN = {'nn': ((1,), (0,)), 'nt': ((1,), (1,)), 'tn': ((0,), (0,))}


def _mm(name, a, b, kind, M, N, K, out_dtype, tm=768, tn=768, tk=1024, a_spec=None, b_spec=None,
        out_shape=None, out_spec=None):
    tm, tn = _tile(M, tm, 128), _tile(N, tn, 128)
    tk = _tile(K, tk, 128)
    nk = K // tk

    def body(a_ref, b_ref, o_ref, acc):
        kk = pl.program_id(2)

        @pl.when(kk == 0)
        def _():
            acc[...] = jnp.zeros_like(acc)

        acc[...] += lax.dot_general(a_ref[...].astype(BF16), b_ref[...].astype(BF16), (_DN[kind], ((), ())),
                                    preferred_element_type=F32)

        @pl.when(kk == nk - 1)
        def _():
            o_ref[...] = acc[...].astype(o_ref.dtype)

    if a_spec is None:
        a_spec = pl.BlockSpec((tk, tm), lambda i, j, k: (k, i)) if kind == 'tn' else pl.BlockSpec((tm, tk), lambda i, j, k: (i, k))
    if b_spec is None:
        b_spec = pl.BlockSpec((tn, tk), lambda i, j, k: (j, k)) if kind == 'nt' else pl.BlockSpec((tk, tn), lambda i, j, k: (k, j))
    if out_spec is None:
        out_spec = pl.BlockSpec((tm, tn), lambda i, j, k: (i, j))
        out_shape = (M, N)
    return pl.pallas_call(
        body, name=name, grid=(M // tm, N // tn, nk), in_specs=[a_spec, b_spec], out_specs=out_spec,
        out_shape=jax.ShapeDtypeStruct(out_shape, out_dtype), scratch_shapes=[pltpu.VMEM((tm, tn), F32)],
        compiler_params=_cp(("parallel", "parallel", "arbitrary")))(a, b)


def _row_specs(cf, tm, tiles, params):
    nctx = cf.LC // tm
    specs = []
    for arr, w, colf, _ in tiles:
        specs.append(pl.BlockSpec((tm, w), functools.partial(lambda i, r, colf: (i, colf(r)), colf=colf)))
    for arr, kind, _ in params:
        nd = arr.ndim
        if kind == 'shared':
            specs.append(pl.BlockSpec(arr.shape, functools.partial(lambda i, r, nd: (0,) * nd, nd=nd)))
        else:
            specs.append(pl.BlockSpec((None,) + arr.shape[1:],
                                      functools.partial(lambda i, r, nd, nctx: (jnp.where(i >= nctx, 1, 0),) + (0,) * (nd - 1), nd=nd, nctx=nctx)))
    return specs


def _row_fwd(cf, name, f, tiles, params, outs, tm, nrep=1):
    nt, npar = len(tiles), len(params)

    def body(*refs):
        tv = [r[...].astype(F32) for r in refs[:nt]]
        pv = [r[...] for r in refs[nt:nt + npar]]
        res = f(*tv, *pv)
        for o, v in zip(refs[nt + npar:], res):
            o[...] = v.astype(o.dtype)

    out_specs = [pl.BlockSpec((tm, w), functools.partial(lambda i, r, colf: (i, colf(r)), colf=colf)) for w, colf, _, _ in outs]
    out_shape = [jax.ShapeDtypeStruct((cf.T, tw), dt) for _, _, tw, dt in outs]
    return pl.pallas_call(
        body, name=name, grid=(cf.T // tm, nrep), in_specs=_row_specs(cf, tm, tiles, params), out_specs=out_specs,
        out_shape=out_shape, compiler_params=_cp(("arbitrary", "arbitrary")))(*[t[0] for t in tiles], *[p[0] for p in params])


def _row_bwd(cf, name, f, tiles, params, cts, tgrads, tm, nrep=1):
    nt, npar, nc = len(tiles), len(params), len(cts)
    tdiff = [k for k, t in enumerate(tiles) if t[3]]
    pdiff = [k for k, p in enumerate(params) if p[2]]
    nctx = cf.LC // tm

    def body(*refs):
        i, r = pl.program_id(0), pl.program_id(1)
        tv = [x[...].astype(F32) for x in refs[:nt]]
        pv = [x[...] for x in refs[nt:nt + npar]]
        cv = tuple(x[...].astype(F32) for x in refs[nt + npar:nt + npar + nc])
        outs = refs[nt + npar + nc:]

        def g(*diff):
            tv2, pv2 = list(tv), list(pv)
            for k, v in zip(tdiff, diff[:len(tdiff)]):
                tv2[k] = v
            for k, v in zip(pdiff, diff[len(tdiff):]):
                pv2[k] = v
            return tuple(f(*tv2, *pv2))

        _, vjp_fn = jax.vjp(g, *[tv[k] for k in tdiff], *[pv[k] for k in pdiff])
        grads = vjp_fn(cv)
        for o, gv in zip(outs[:len(tdiff)], grads[:len(tdiff)]):
            o[...] = gv.astype(o.dtype)
        for n_, (o, gv) in enumerate(zip(outs[len(tdiff):], grads[len(tdiff):])):
            if params[pdiff[n_]][1] == 'shared':
                first = jnp.logical_and(i == 0, r == 0)
            else:
                first = jnp.logical_and(jnp.logical_or(i == 0, i == nctx), r == 0)

            @pl.when(first)
            def _():
                o[...] = gv

            @pl.when(jnp.logical_not(first))
            def _():
                o[...] += gv

    in_specs = _row_specs(cf, tm, tiles, params)
    in_specs += [pl.BlockSpec((tm, w), functools.partial(lambda i, r, colf: (i, colf(r)), colf=colf)) for _, w, colf in cts]
    out_specs = [pl.BlockSpec((tm, w), functools.partial(lambda i, r, colf: (i, colf(r)), colf=colf)) for w, colf, _, _ in tgrads]
    out_shape = [jax.ShapeDtypeStruct((cf.T, tw), dt) for _, _, tw, dt in tgrads]
    pspecs = _row_specs(cf, tm, [], [params[k] for k in pdiff])
    out_specs += pspecs
    out_shape += [jax.ShapeDtypeStruct(params[k][0].shape, F32) for k in pdiff]
    res = pl.pallas_call(
        body, name=name, grid=(cf.T // tm, nrep), in_specs=in_specs, out_specs=out_specs, out_shape=out_shape,
        compiler_params=_cp(("arbitrary", "arbitrary")))(*[t[0] for t in tiles], *[p[0] for p in params], *[c[0] for c in cts])
    return res[:len(tdiff)], res[len(tdiff):]


def _c0(r):
    return 0


def _col(base):
    return lambda r: base + r


def _rms(x, g):
    return x * lax.rsqrt(jnp.mean(x * x, axis=-1, keepdims=True) + EPS) * g


def _sigmoid(x):
    return 1.0 / (1.0 + jnp.exp(-x))


def _silu(x):
    return x * _sigmoid(x)


def _f_norm_mod(x, g, sh, sc):
    return (_rms(x, g) * (1 + sc) + sh,)


def _f_resid_norm_mod(x, y, gate, g, sh, sc):
    x1 = x + gate * y
    return (x1, _rms(x1, g) * (1 + sc) + sh)


@jax.custom_vjp
def _swap_halves(t):
    return pltpu.roll(t, HEAD_DIM // 2, axis=1)


def _swap_fwd(t):
    return _swap_halves(t), None


def _swap_bwd(_, g):
    return (pltpu.roll(g, HEAD_DIM // 2, axis=1),)


_swap_halves.defvjp(_swap_fwd, _swap_bwd)


def _rope(t, cs, sn):
    return t * cs + _swap_halves(t) * sn


def _f_prep_norm(t, cs, sn, g):
    return (_rope(_rms(t, g), cs, sn),)


def _f_prep_plain(t, cs, sn):
    return (_rope(t, cs, sn),)


def _f_prep_scaled(t, cs, sn):
    return (_rope(t * (HEAD_DIM ** -0.5), cs, sn),)


def _log_sigmoid(x):
    return jnp.minimum(x, 0.0) - jnp.log(1.0 + jnp.exp(-jnp.abs(x)))


N_DECAY = 8


def _gla_masks(d, width):
    C, SB = GLA_CHUNK, GLA_SUB
    r = lax.broadcasted_iota(jnp.int32, (C, C), 0)
    m = lax.broadcasted_iota(jnp.int32, (C, C), 1)
    rr = lax.broadcasted_iota(jnp.int32, (C, width), 0)
    allowed = (m <= r) if d == 0 else (m >= r)
    blocks, vis = [allowed], []
    for b in range(C // SB):
        blocks.append((m < SB * b) if d == 0 else (m >= SB * (b + 1)))
        vis.append((rr < SB * (b + 1)) if d == 0 else (rr >= SB * b))
    cm = jnp.concatenate([x.astype(F32) for x in blocks] + [jnp.ones((C, C), F32)], axis=0)
    return cm, allowed, vis


def _gla_decays(la, d):
    C, SB = GLA_CHUNK, GLA_SUB
    nsb = C // SB
    cm, _, vis = _gla_masks(d, la.shape[-1])
    cums = jnp.dot(cm, la, precision=HI, preferred_element_type=F32)
    cum, tot = cums[0:C], cums[(1 + nsb) * C:]
    refs = [cums[(1 + b) * C:(2 + b) * C] for b in range(nsb)]
    e1 = jnp.concatenate([jnp.exp(cum[b * SB:(b + 1) * SB] - refs[b][b * SB:(b + 1) * SB]) for b in range(nsb)], axis=0)
    e2 = [jnp.where(vis[b], jnp.exp(jnp.where(vis[b], refs[b] - cum, 0.0)), 0.0) for b in range(nsb)]
    return [e1] + e2 + [jnp.exp(cum), jnp.exp(tot - cum), jnp.exp(tot)]


def _f_gla_pre(ga, *per_pair):
    gab = ga.astype(BF16)
    outs = [[], []]
    for p in range(len(per_pair) // 4):
        gf, gb, bf, bb = per_pair[4 * p:4 * p + 4]
        for d, (gm, bm) in enumerate(((gf, bf), (gb, bb))):
            la = _log_sigmoid(jnp.dot(gab, gm.astype(BF16), preferred_element_type=F32) + bm) / GLA_TAU
            outs[d] += _gla_decays(la, d)
    return tuple(jnp.concatenate(o, axis=-1) for o in outs)


def _f_gated_norm(o, g, n):
    return (_rms(o, n) * _silu(g),)


def _f_first(x, g, sh, sc):
    return (x, _rms(x, g) * (1 + sc) + sh)


def _loss_grad(cf, x1, yff, gate, gfin, tgt):
    tm, T, D = cf.TM, cf.T, cf.D
    nctx = cf.LC // tm

    def lossf(x1v, yv, gt, gf, tg):
        y = _rms(x1v + gt * yv, gf)
        e = y - tg
        return 0.5 * jnp.sum(jnp.mean(e * e, axis=-1, keepdims=True), axis=0, keepdims=True)

    def body(x1_ref, y_ref, gt_ref, gf_ref, tg_ref, dx_ref, dy_ref, dgt_ref, dgf_ref, ls_ref):
        i = pl.program_id(0)
        lat = (i >= nctx).astype(F32)
        val, vjp_fn = jax.vjp(lossf, x1_ref[...], y_ref[...].astype(F32), gt_ref[...], gf_ref[...], tg_ref[...])
        dx, dy, dgt, dgf, _ = vjp_fn(jnp.ones((1, 1), F32) * lat)
        dx_ref[...] = dx
        dy_ref[...] = dy.astype(dy_ref.dtype)
        first_s = jnp.logical_or(i == 0, i == nctx)

        @pl.when(first_s)
        def _():
            dgt_ref[...] = dgt

        @pl.when(jnp.logical_not(first_s))
        def _():
            dgt_ref[...] += dgt

        @pl.when(i == 0)
        def _():
            dgf_ref[...] = dgf
            ls_ref[...] = jnp.zeros_like(ls_ref) + val * lat

        @pl.when(i != 0)
        def _():
            dgf_ref[...] += dgf
            ls_ref[...] += val * lat

    row = pl.BlockSpec((tm, D), lambda i: (i, 0))
    strm = pl.BlockSpec((None, 1, D), lambda i: (jnp.where(i >= nctx, 1, 0), 0, 0))
    one = pl.BlockSpec((1, D), lambda i: (0, 0))
    return pl.pallas_call(
        body, name="loss_grad", grid=(T // tm,), in_specs=[row, row, strm, one, row],
        out_specs=[row, row, strm, one, pl.BlockSpec((8, LANE), lambda i: (0, 0))],
        out_shape=[jax.ShapeDtypeStruct((T, D), F32), jax.ShapeDtypeStruct((T, D), BF16),
                   jax.ShapeDtypeStruct((2, 1, D), F32), jax.ShapeDtypeStruct((1, D), F32),
                   jax.ShapeDtypeStruct((8, LANE), F32)],
        compiler_params=_cp(("arbitrary",)))(x1, yff, gate, gfin, tgt)


def _att_mask(cf, i, tq):
    col = lax.broadcasted_iota(jnp.int32, (tq, cf.T), 1)
    return jnp.logical_or(col < cf.LC, i >= cf.LC // tq)


def _att_probs(q, k, mask):
    s = lax.dot_general(q, k, (_DN['nt'], ((), ())), preferred_element_type=F32) * (HEAD_DIM ** -0.5)
    s = jnp.where(mask, s, NEG)
    e = jnp.exp(s - jnp.max(s, axis=-1, keepdims=True))
    return e / jnp.sum(e, axis=-1, keepdims=True)


def _att_fwd(cf, q, k, z):
    tq, T, G = cf.TQ, cf.T, cf.G
    vb = cf.OFF['av'] // LANE

    def body(q_ref, k_ref, v_ref, o_ref):
        mask = _att_mask(cf, pl.program_id(1), tq)
        kv, vv = k_ref[...], v_ref[...].astype(BF16)
        for j in range(G):
            p = _att_probs(q_ref[:, j * LANE:(j + 1) * LANE], kv, mask)
            o_ref[:, j * LANE:(j + 1) * LANE] = jnp.dot(p.astype(BF16), vv, preferred_element_type=F32).astype(o_ref.dtype)

    return pl.pallas_call(
        body, name="att_fwd", grid=(cf.HKV, T // tq),
        in_specs=[pl.BlockSpec((tq, G * LANE), lambda g, i: (i, g)), pl.BlockSpec((T, LANE), lambda g, i: (0, g)),
                  pl.BlockSpec((T, LANE), lambda g, i: (0, vb + g))],
        out_specs=pl.BlockSpec((tq, G * LANE), lambda g, i: (i, g)),
        out_shape=jax.ShapeDtypeStruct((T, cf.HQ * LANE), BF16), compiler_params=_cp(("arbitrary", "arbitrary")))(q, k, z)


def _att_bwd(cf, q, k, z, dcat):
    tq, T, G = cf.TQ, cf.T, cf.G
    vb = cf.OFF['av'] // LANE
    sc = HEAD_DIM ** -0.5

    def body(q_ref, k_ref, v_ref, do_ref, dq_ref, dk_ref, dv_ref):
        i = pl.program_id(1)
        mask = _att_mask(cf, i, tq)
        kv, vv = k_ref[...], v_ref[...].astype(BF16)
        dk = jnp.zeros((T, LANE), F32)
        dv = jnp.zeros((T, LANE), F32)
        for j in range(G):
            qj = q_ref[:, j * LANE:(j + 1) * LANE]
            do = do_ref[:, j * LANE:(j + 1) * LANE]
            p = _att_probs(qj, kv, mask)
            dv += lax.dot_general(p.astype(BF16), do, (_DN['tn'], ((), ())), preferred_element_type=F32)
            dp = lax.dot_general(do, vv, (_DN['nt'], ((), ())), preferred_element_type=F32)
            ds = p * (dp - jnp.sum(dp * p, axis=-1, keepdims=True)) * sc
            dsb = ds.astype(BF16)
            dq_ref[:, j * LANE:(j + 1) * LANE] = jnp.dot(dsb, kv, preferred_element_type=F32)
            dk += lax.dot_general(dsb, qj, (_DN['tn'], ((), ())), preferred_element_type=F32)

        @pl.when(i == 0)
        def _():
            dk_ref[...] = dk
            dv_ref[...] = dv

        @pl.when(i != 0)
        def _():
            dk_ref[...] += dk
            dv_ref[...] += dv

    qs = pl.BlockSpec((tq, G * LANE), lambda g, i: (i, g))
    ks = pl.BlockSpec((T, LANE), lambda g, i: (0, g))
    return pl.pallas_call(
        body, name="att_bwd", grid=(cf.HKV, T // tq),
        in_specs=[qs, ks, pl.BlockSpec((T, LANE), lambda g, i: (0, vb + g)), qs],
        out_specs=[qs, ks, ks],
        out_shape=[jax.ShapeDtypeStruct((T, cf.HQ * LANE), F32), jax.ShapeDtypeStruct((T, cf.HKV * LANE), F32),
                   jax.ShapeDtypeStruct((T, cf.HKV * LANE), F32)],
        compiler_params=_cp(("arbitrary", "arbitrary")))(q, k, z, dcat)


def _ret_masks(cf, i, tq, lgf, lgb):
    T, LC = cf.T, cf.LC
    row = lax.broadcasted_iota(jnp.int32, (tq, T), 0) + i * tq
    col = lax.broadcasted_iota(jnp.int32, (tq, T), 1)

    def pb(n):
        return jnp.where(n < LC, LC - 1 - n, T + LC - 1 - n)

    relf = row - col
    relb = pb(row) - pb(col)
    okf, okb = relf >= 0, relb >= 0
    rf = jnp.where(okf, relf, 0).astype(F32)
    rb = jnp.where(okb, relb, 0).astype(F32)
    mf = jnp.where(okf, jnp.exp(lgf * rf), 0.0)
    mb = jnp.where(okb, jnp.exp(lgb * rb), 0.0)
    return mf, mb, rf, rb


def _ret_fwd(cf, q, k, z, lg):
    tq, T = cf.TQ, cf.T
    vb = cf.OFF['rv'] // LANE

    def body(lg_ref, q_ref, k_ref, v_ref, o_ref):
        h, i = pl.program_id(0), pl.program_id(1)
        mf, mb, _, _ = _ret_masks(cf, i, tq, lg_ref[0, h], lg_ref[1, h])
        a = lax.dot_general(q_ref[...], k_ref[...], (_DN['nt'], ((), ())), preferred_element_type=F32)
        p = (a * (mf + mb)).astype(BF16)
        o_ref[...] = jnp.dot(p, v_ref[...].astype(BF16), preferred_element_type=F32)

    return pl.pallas_call(
        body, name="ret_fwd", grid=(cf.HR, T // tq),
        in_specs=[pl.BlockSpec(memory_space=pltpu.SMEM), pl.BlockSpec((tq, LANE), lambda h, i: (i, h)),
                  pl.BlockSpec((T, LANE), lambda h, i: (0, h)), pl.BlockSpec((T, LANE), lambda h, i: (0, vb + h))],
        out_specs=pl.BlockSpec((tq, LANE), lambda h, i: (i, h)),
        out_shape=jax.ShapeDtypeStruct((T, cf.HR * LANE), F32), compiler_params=_cp(("arbitrary", "arbitrary")))(lg, q, k, z)


def _ret_bwd(cf, q, k, z, lg, do):
    tq, T = cf.TQ, cf.T
    vb = cf.OFF['rv'] // LANE

    def body(lg_ref, q_ref, k_ref, v_ref, do_ref, dq_ref, dk_ref, dv_ref, dlg_ref):
        h, i = pl.program_id(0), pl.program_id(1)
        mf, mb, rf, rb = _ret_masks(cf, i, tq, lg_ref[0, h], lg_ref[1, h])
        qv, kv, vv = q_ref[...], k_ref[...], v_ref[...].astype(BF16)
        dob = do_ref[...].astype(BF16)
        a = lax.dot_general(qv, kv, (_DN['nt'], ((), ())), preferred_element_type=F32)
        m = mf + mb
        p = (a * m).astype(BF16)
        dv = lax.dot_general(p, dob, (_DN['tn'], ((), ())), preferred_element_type=F32)
        dp = lax.dot_general(dob, vv, (_DN['nt'], ((), ())), preferred_element_type=F32)
        da = (dp * m).astype(BF16)
        dq_ref[...] = jnp.dot(da, kv, preferred_element_type=F32)
        dk = lax.dot_general(da, qv, (_DN['tn'], ((), ())), preferred_element_type=F32)
        dm = dp * a
        dlf = jnp.sum(jnp.sum(dm * mf * rf, axis=-1, keepdims=True), axis=0, keepdims=True)
        dlb = jnp.sum(jnp.sum(dm * mb * rb, axis=-1, keepdims=True), axis=0, keepdims=True)
        rid = lax.broadcasted_iota(jnp.int32, (8, LANE), 0)
        dl = jnp.where(rid == 0, dlf, jnp.where(rid == 1, dlb, 0.0))

        @pl.when(i == 0)
        def _():
            dk_ref[...] = dk
            dv_ref[...] = dv
            dlg_ref[...] = dl

        @pl.when(i != 0)
        def _():
            dk_ref[...] += dk
            dv_ref[...] += dv
            dlg_ref[...] += dl

    qs = pl.BlockSpec((tq, LANE), lambda h, i: (i, h))
    ks = pl.BlockSpec((T, LANE), lambda h, i: (0, h))
    return pl.pallas_call(
        body, name="ret_bwd", grid=(cf.HR, T // tq),
        in_specs=[pl.BlockSpec(memory_space=pltpu.SMEM), qs, ks, pl.BlockSpec((T, LANE), lambda h, i: (0, vb + h)), qs],
        out_specs=[qs, ks, ks, pl.BlockSpec((None, 8, LANE), lambda h, i: (h, 0, 0))],
        out_shape=[jax.ShapeDtypeStruct((T, cf.HR * LANE), F32)] * 3 + [jax.ShapeDtypeStruct((cf.HR, 8, LANE), F32)],
        compiler_params=_cp(("arbitrary", "arbitrary")))(lg, q, k, z, do)


def _gla_step(q, k, v, es, st, lmask, allowed):
    C, SB = GLA_CHUNK, GLA_SUB
    nsb = C // SB
    e1, e2, e3, e4, e5 = es[0], es[1:1 + nsb], es[1 + nsb], es[2 + nsb], es[3 + nsb]
    qs = q * lmask * ((HEAD_DIM // 2) ** -0.5)
    ks = k * lmask
    qt = qs * e1
    rows = [lax.dot_general(qt[b * SB:(b + 1) * SB], ks * e2[b], (_DN['nt'], ((), ())), precision=lax.Precision.HIGH,
                            preferred_element_type=F32) for b in range(nsb)]
    att = jnp.where(allowed, jnp.concatenate(rows, axis=0), 0.0)
    o = jnp.dot(att.astype(BF16), v.astype(BF16), preferred_element_type=F32)
    o += lax.dot_general((qs * e3).astype(BF16), st.astype(BF16), (_DN['nt'], ((), ())), preferred_element_type=F32)
    kd = (ks * e4).astype(BF16)
    st_new = st * jnp.concatenate([e5, e5], axis=0) + lax.dot_general(v.astype(BF16), kd, (_DN['tn'], ((), ())), preferred_element_type=F32)
    return o, st_new


def _gla_allowed(d):
    r = lax.broadcasted_iota(jnp.int32, (GLA_CHUNK, GLA_CHUNK), 0)
    m = lax.broadcasted_iota(jnp.int32, (GLA_CHUNK, GLA_CHUNK), 1)
    return (m <= r) if d == 0 else (m >= r)


def _gla_chunk_id(cf, s, d):
    if d == 0:
        return s
    nct, nc = cf.LC // GLA_CHUNK, cf.T // GLA_CHUNK
    return jnp.where(s < nct, nct - 1 - s, nc + nct - 1 - s)


def _gla_lmask(h):
    return (lax.broadcasted_iota(jnp.int32, (1, LANE), 1) // (LANE // 2) == h).astype(F32)


_GLA_CHAINS = [(h, d) for h in range(2) for d in range(2)]


def _gla_row_specs(cf, nc, reverse):
    def rowblk(s, d):
        return _gla_chunk_id(cf, nc - 1 - s if reverse else s, d)

    def spec(width, base, d, per_pair=1):
        return pl.BlockSpec((GLA_CHUNK, width), functools.partial(lambda p, s, base, d: (rowblk(s, d), base + per_pair * p), base=base, d=d))

    def state(d):
        return pl.BlockSpec((2, None, LANE, LANE), functools.partial(lambda p, s, d: (p, rowblk(s, d), 0, 0), d=d))

    return spec, state


def _gla_fwd(cf, z, ef, eb):
    T, C = cf.T, GLA_CHUNK
    nc = T // C
    qb, kb, vb = cf.OFF['gq'] // LANE, cf.OFF['gk'] // LANE, cf.OFF['gv'] // (2 * LANE)
    spec, state = _gla_row_specs(cf, nc, False)

    def body(qf, kf, vf, e_f, qb_, kb_, vb_, e_b, of, sf, ob, sb, st_scr):
        @pl.when(pl.program_id(1) == 0)
        def _():
            st_scr[...] = jnp.zeros_like(st_scr)

        io = [(qf, kf, vf, e_f, of, sf), (qb_, kb_, vb_, e_b, ob, sb)]
        for ci, (h, d) in enumerate(_GLA_CHAINS):
            q, k, v, e, o_ref, s_ref = io[d]
            cols = slice(h * LANE, (h + 1) * LANE)
            es = [e[:, n * LANE:(n + 1) * LANE] for n in range(N_DECAY)]
            st = st_scr[ci]
            s_ref[h] = st
            o, stn = _gla_step(q[...], k[...], v[:, cols], es, st, _gla_lmask(h), _gla_allowed(d))
            st_scr[ci] = stn
            o_ref[:, cols] = o

    ins, outs = [], []
    for d in range(2):
        ins += [spec(LANE, qb, d), spec(LANE, kb, d), spec(2 * LANE, vb, d), spec(N_DECAY * LANE, 0, d)]
        outs += [spec(2 * LANE, 0, d), state(d)]
    oshape = [jax.ShapeDtypeStruct((T, cf.HG * LANE), F32), jax.ShapeDtypeStruct((cf.HG, nc, LANE, LANE), F32)]
    return pl.pallas_call(
        body, name="gla_fwd", grid=(cf.HG // 2, nc), in_specs=ins, out_specs=outs, out_shape=oshape * 2,
        scratch_shapes=[pltpu.VMEM((4, LANE, LANE), F32)],
        compiler_params=_cp(("arbitrary", "arbitrary")))(z, z, z, ef, z, z, z, eb)


def _gla_bwd(cf, z, ef, eb, sf, sb, do):
    T, C = cf.T, GLA_CHUNK
    nc = T // C
    qb, kb, vb = cf.OFF['gq'] // LANE, cf.OFF['gk'] // LANE, cf.OFF['gv'] // (2 * LANE)
    spec, state = _gla_row_specs(cf, nc, True)

    def body(*refs):
        ins = [refs[0:6], refs[6:12]]
        outs = [refs[12:16], refs[16:20]]
        dst_scr = refs[20]

        @pl.when(pl.program_id(1) == 0)
        def _():
            dst_scr[...] = jnp.zeros_like(dst_scr)

        acc = [None, None]
        for ci, (h, d) in enumerate(_GLA_CHAINS):
            q, k, v, e, s_ref, do_ref = ins[d]
            cols = slice(h * LANE, (h + 1) * LANE)
            es = [e[:, n * LANE:(n + 1) * LANE] for n in range(N_DECAY)]
            step = functools.partial(_gla_step, lmask=_gla_lmask(h), allowed=_gla_allowed(d))
            _, vjp_fn = jax.vjp(step, q[...], k[...], v[:, cols], es, s_ref[h])
            dq, dk, dv, des, dst = vjp_fn((do_ref[:, cols], dst_scr[ci]))
            dst_scr[ci] = dst
            outs[d][2][:, cols] = dv
            part = [dq, dk] + list(des)
            acc[d] = part if acc[d] is None else [a + b for a, b in zip(acc[d], part)]
        for d in range(2):
            dq_ref, dk_ref, _, de_ref = outs[d]
            dq_ref[...] = acc[d][0]
            dk_ref[...] = acc[d][1]
            for n in range(N_DECAY):
                de_ref[:, n * LANE:(n + 1) * LANE] = acc[d][2 + n]

    in_specs, out_specs = [], []
    for d in range(2):
        in_specs += [spec(LANE, qb, d), spec(LANE, kb, d), spec(2 * LANE, vb, d), spec(N_DECAY * LANE, 0, d), state(d), spec(2 * LANE, 0, d)]
        out_specs += [spec(LANE, 0, d), spec(LANE, 0, d), spec(2 * LANE, 0, d), spec(N_DECAY * LANE, 0, d)]
    npair = cf.HG // 2
    oshape = [jax.ShapeDtypeStruct((T, npair * LANE), F32), jax.ShapeDtypeStruct((T, npair * LANE), F32),
              jax.ShapeDtypeStruct((T, cf.HG * LANE), F32), jax.ShapeDtypeStruct((T, npair * N_DECAY * LANE), F32)]
    return pl.pallas_call(
        body, name="gla_bwd", grid=(npair, nc), in_specs=in_specs, out_specs=out_specs, out_shape=oshape * 2,
        scratch_shapes=[pltpu.VMEM((4, LANE, LANE), F32)],
        compiler_params=_cp(("arbitrary", "arbitrary")))(z, z, z, ef, sf, do, z, z, z, eb, sb, do)


def _conv_parts(cf, a, w_ref):
    T, LC = cf.T, cf.LC
    rid = lax.broadcasted_iota(jnp.int32, a.shape, 0)
    first = jnp.logical_or(rid == 0, rid == LC)
    last = jnp.logical_or(rid == LC - 1, rid == T - 1)
    ap = jnp.where(first, 0.0, pltpu.roll(a, 1, axis=0))
    an = jnp.where(last, 0.0, pltpu.roll(a, T - 1, axis=0))
    w0, w1, w2, b = w_ref[0:1, :], w_ref[1:2, :], w_ref[2:3, :], w_ref[3:4, :]
    ac = ap * w0 + a * w1 + an * w2 + b
    return ap, an, ac, first, last, (w0, w1, w2)


def _conv_fwd(cf, u, wb):
    T, Fd = cf.T, cf.F
    tc = _tile(Fd, 512)
    nj = Fd // tc

    def body(a_ref, v_ref, w_ref, t_ref):
        _, _, ac, _, _, _ = _conv_parts(cf, a_ref[...], w_ref)
        t_ref[...] = (_silu(ac) * v_ref[...]).astype(t_ref.dtype)

    return pl.pallas_call(
        body, name="conv_fwd", grid=(nj,),
        in_specs=[pl.BlockSpec((T, tc), lambda j: (0, j)), pl.BlockSpec((T, tc), lambda j: (0, nj + j)),
                  pl.BlockSpec((8, tc), lambda j: (0, j))],
        out_specs=pl.BlockSpec((T, tc), lambda j: (0, j)), out_shape=jax.ShapeDtypeStruct((T, Fd), BF16),
        compiler_params=_cp(("parallel",)))(u, u, wb)


def _conv_bwd(cf, u, wb, dt):
    T, Fd = cf.T, cf.F
    tc = _tile(Fd, 256)
    nj = Fd // tc

    def body(a_ref, v_ref, w_ref, dt_ref, da_ref, dv_ref, dw_ref):
        a, v, dtv = a_ref[...], v_ref[...], dt_ref[...].astype(F32)
        ap, an, ac, first, last, (w0, w1, w2) = _conv_parts(cf, a, w_ref)
        sg = _sigmoid(ac)
        dv_ref[...] = (dtv * ac * sg).astype(dv_ref.dtype)
        dac = dtv * v * (sg * (1.0 + ac * (1.0 - sg)))
        from_next = pltpu.roll(jnp.where(first, 0.0, dac), T - 1, axis=0)
        from_prev = pltpu.roll(jnp.where(last, 0.0, dac), 1, axis=0)
        da_ref[...] = (dac * w1 + from_next * w0 + from_prev * w2).astype(da_ref.dtype)
        rows = [jnp.sum(dac * ap, axis=0, keepdims=True), jnp.sum(dac * a, axis=0, keepdims=True),
                jnp.sum(dac * an, axis=0, keepdims=True), jnp.sum(dac, axis=0, keepdims=True)]
        rid = lax.broadcasted_iota(jnp.int32, (8, tc), 0)
        dw = jnp.zeros((8, tc), F32)
        for n_, rw in enumerate(rows):
            dw = jnp.where(rid == n_, rw, dw)
        dw_ref[...] = dw

    col = pl.BlockSpec((T, tc), lambda j: (0, j))
    return pl.pallas_call(
        body, name="conv_bwd", grid=(nj,),
        in_specs=[col, pl.BlockSpec((T, tc), lambda j: (0, nj + j)), pl.BlockSpec((8, tc), lambda j: (0, j)), col],
        out_specs=[col, col, pl.BlockSpec((8, tc), lambda j: (0, j))],
        out_shape=[jax.ShapeDtypeStruct((T, Fd), BF16), jax.ShapeDtypeStruct((T, Fd), BF16), jax.ShapeDtypeStruct((8, Fd), F32)],
        compiler_params=_cp(("parallel",)))(u, u, wb, dt)


def _me():
    x, y, c = lax.axis_index("x"), lax.axis_index("y"), lax.axis_index("c")
    return x, y, c, 4 * x + 2 * y + c


def _peer(x, y, c, k):
    px = 1 - x if (k >> 2) & 1 else x
    py = 1 - y if (k >> 1) & 1 else y
    pc = 1 - c if k & 1 else c
    return (px, py, pc), 4 * px + 2 * py + pc


def _rcopy(src, dst, ss, rs, tgt):
    return pltpu.make_async_remote_copy(src_ref=src, dst_ref=dst, send_sem=ss, recv_sem=rs, device_id=tgt,
                                        device_id_type=pl.DeviceIdType.MESH)


def _ag_small(name, v):
    R, Cc = v.shape

    def body(v_ref, o_ref, ssem, rsem, lsem):
        x, y, c, me = _me()
        loc = pltpu.make_async_copy(v_ref, o_ref.at[me], lsem)
        loc.start()
        sends = []
        for k in range(1, N_DEV):
            tgt, _ = _peer(x, y, c, k)
            cp = _rcopy(v_ref, o_ref.at[me], ssem.at[k - 1], rsem.at[k - 1], tgt)
            cp.start()
            sends.append(cp)
        for k in range(1, N_DEV):
            tgt, pi = _peer(x, y, c, k)
            _rcopy(v_ref, o_ref.at[pi], ssem.at[k - 1], rsem.at[k - 1], tgt).wait_recv()
        for cp in sends:
            cp.wait_send()
        loc.wait()

    vm = pl.BlockSpec(memory_space=pltpu.VMEM)
    return pl.pallas_call(
        body, name=name, in_specs=[vm], out_specs=vm, out_shape=jax.ShapeDtypeStruct((N_DEV, R, Cc), v.dtype),
        scratch_shapes=[pltpu.SemaphoreType.DMA((N_DEV - 1,)), pltpu.SemaphoreType.DMA((N_DEV - 1,)), pltpu.SemaphoreType.DMA],
        compiler_params=pltpu.CompilerParams(vmem_limit_bytes=VMEM_LIMIT))(v)


_HBM = pl.BlockSpec(memory_space=pltpu.HBM)
_SEM = pl.BlockSpec(memory_space=pltpu.SEMAPHORE)
_EFFECT = pltpu.SideEffectType.DATAFLOW_SIDE_EFFECTING
_KINDS = ['in', 'out', 'up', 'down']


def _hbm(a):
    return pltpu.with_memory_space_constraint(a, pltpu.HBM)


def _shard_shape(cf, kind):
    D, Fd = cf.D, cf.F
    return {'in': (D, cf.NINS), 'out': (D // N_DEV, D), 'up': (D, 2 * Fd // N_DEV), 'down': (Fd // N_DEV, D)}[kind]


def _whole_shape(cf, kind):
    D, Fd = cf.D, cf.F
    return {'in': (N_DEV, D, cf.NINS), 'out': (D, D), 'up': (D, 2 * Fd), 'down': (Fd, D)}[kind]


def _part(ref, cf, kind, idx):
    r, cdim = _shard_shape(cf, kind)
    if kind == 'in':
        return ref.at[idx]
    if kind == 'up':
        return ref.at[:, pl.ds(pl.multiple_of(idx * cdim, cdim), cdim)]
    return ref.at[pl.ds(pl.multiple_of(idx * r, r), r), :]


def _ag_start(cf, shards, after):
    npc = DEPTH * len(_KINDS)
    nio = len(_KINDS) + npc

    def body(*refs):
        srcs, lands = refs[:len(_KINDS)], refs[len(_KINDS):nio]
        ssems, rsems = refs[2 * nio + 1:2 * nio + 1 + npc], refs[2 * nio + 1 + npc:2 * nio + 1 + 2 * npc]
        token, lsem = refs[2 * nio + 1 + 2 * npc], refs[2 * nio + 2 + 2 * npc]
        x, y, c, me = _me()
        pieces = [(l * len(_KINDS) + n, srcs[n].at[l], kind) for l in range(DEPTH) for n, kind in enumerate(_KINDS)]
        locs = [pltpu.make_async_copy(src, _part(lands[p], cf, kind, me), lsem.at[p]) for p, src, kind in pieces]
        for cp in locs:
            cp.start()
        for cp in locs:
            cp.wait()
        for p, src, kind in pieces:
            for k in range(1, N_DEV):
                tgt, _ = _peer(x, y, c, k)
                _rcopy(src, _part(lands[p], cf, kind, me), ssems[p].at[k - 1], rsems[p].at[k - 1], tgt).start()
        token[...] = jnp.zeros_like(token)

    land_shapes = [_whole_shape(cf, kind) for _ in range(DEPTH) for kind in _KINDS]
    out_shape = [pltpu.HBM(s.shape, BF16) for s in shards] + [pltpu.HBM(s, BF16) for s in land_shapes]
    out_shape += [pltpu.SemaphoreType.DMA((N_DEV - 1,))] * (2 * npc) + [jax.ShapeDtypeStruct((8, LANE), F32)]
    res = pl.pallas_call(
        body, name="ag_start", in_specs=[_HBM] * nio + [pl.BlockSpec(memory_space=pl.ANY)],
        out_specs=[_HBM] * nio + [_SEM] * (2 * npc) + [pl.BlockSpec(memory_space=pltpu.VMEM)],
        out_shape=out_shape, input_output_aliases={i: i for i in range(nio)}, scratch_shapes=[pltpu.SemaphoreType.DMA((npc,))],
        compiler_params=pltpu.CompilerParams(has_side_effects=_EFFECT))(
            *[_hbm(s) for s in shards], *[_hbm(lax.empty(s, BF16)) for s in land_shapes], after)
    return res[:len(_KINDS)], res[len(_KINDS):nio], res[nio:nio + npc], res[nio + npc:nio + 2 * npc], res[nio + 2 * npc]


def _ag_wait(cf, kind, l, src, land, ssem, rsem, after):
    def body(src_ref, land_ref, ssem_ref, rsem_ref, after_ref, src_out, land_out):
        x, y, c, me = _me()
        for k in range(1, N_DEV):
            tgt, pi = _peer(x, y, c, k)
            cp = _rcopy(src_ref.at[l], _part(land_ref, cf, kind, pi), ssem_ref.at[k - 1], rsem_ref.at[k - 1], tgt)
            cp.wait_send()
            cp.wait_recv()

    return pl.pallas_call(
        body, name="ag_wait_%s_%d" % (kind, l), in_specs=[_HBM, _HBM, _SEM, _SEM, pl.BlockSpec(memory_space=pl.ANY)],
        out_specs=[_HBM, _HBM], out_shape=[pltpu.HBM(src.shape, src.dtype), pltpu.HBM(land.shape, land.dtype)],
        input_output_aliases={0: 0, 1: 1}, compiler_params=pltpu.CompilerParams(has_side_effects=_EFFECT))(src, land, ssem, rsem, after)


N_BARRIER_IDS = 8


def _handshake(x, y, c):
    barrier = pltpu.get_barrier_semaphore()
    for k in range(1, N_DEV):
        pl.semaphore_signal(barrier, inc=1, device_id=_peer(x, y, c, k)[0], device_id_type=pl.DeviceIdType.MESH)
    pl.semaphore_wait(barrier, N_DEV - 1)


def _seq_kernel(body, name, seq, out_type):
    return pl.kernel(
        body, out_type=out_type, mesh=plsc.ScalarSubcoreMesh(axis_name="sq", num_cores=1), name=name,
        scratch_types=[pltpu.SemaphoreType.DMA((N_DEV - 1,)), pltpu.SemaphoreType.DMA((N_DEV - 1,)), pltpu.SemaphoreType.DMA],
        compiler_params=pltpu.CompilerParams(collective_id=seq % N_BARRIER_IDS))


def _seq_gather(cf, kind, l, seq, src):
    def body(src_ref, land_ref, ssem, rsem, lsem):
        x, y, c, me = _me()
        _handshake(x, y, c)
        sib = (x, y, 1 - c)
        chips = [(1 - x, y), (x, 1 - y), (1 - x, 1 - y)]

        def blk(px, py, pc):
            return _part(land_ref, cf, kind, 4 * px + 2 * py + pc)

        src, mine = src_ref.at[l], blk(x, y, c)
        loc = pltpu.make_async_copy(src, mine, lsem)
        loc.start()
        loc.wait()
        first = [_rcopy(src, mine, ssem.at[0], rsem.at[0], sib)]
        first += [_rcopy(src, mine, ssem.at[1 + j], rsem.at[1 + j], (*chip, c)) for j, chip in enumerate(chips)]
        for cp in first:
            cp.start()
        passed = [_rcopy(blk(*chip, c), blk(*chip, c), ssem.at[4 + j], rsem.at[4 + j], sib) for j, chip in enumerate(chips)]
        for j, chip in enumerate(chips):
            _rcopy(src, blk(*chip, c), ssem.at[1 + j], rsem.at[1 + j], (*chip, c)).wait_recv()
            passed[j].start()
        _rcopy(src, blk(x, y, 1 - c), ssem.at[0], rsem.at[0], sib).wait_recv()
        for j, chip in enumerate(chips):
            _rcopy(src, blk(*chip, 1 - c), ssem.at[4 + j], rsem.at[4 + j], sib).wait_recv()
        for cp in first + passed:
            cp.wait_send()

    return _seq_kernel(body, "seq_gather_%s_%d" % (kind, l), seq, jax.ShapeDtypeStruct(_whole_shape(cf, kind), BF16))(src)


def _seq_scatter(cf, kind, l, seq, g):
    def body(g_ref, recv_ref, ssem, rsem, lsem):
        x, y, c, me = _me()
        _handshake(x, y, c)
        loc = pltpu.make_async_copy(_rs_slab(g_ref, cf, kind, me), recv_ref.at[me], lsem)
        loc.start()
        loc.wait()
        sends = []
        for k in range(1, N_DEV):
            tgt, pi = _peer(x, y, c, k)
            sends.append(_rcopy(_rs_slab(g_ref, cf, kind, pi), recv_ref.at[me], ssem.at[k - 1], rsem.at[k - 1], tgt))
            sends[-1].start()
        for k in range(1, N_DEV):
            tgt, pi = _peer(x, y, c, k)
            _rcopy(_rs_slab(g_ref, cf, kind, pi), recv_ref.at[pi], ssem.at[k - 1], rsem.at[k - 1], tgt).wait_recv()
        for cp in sends:
            cp.wait_send()

    return _seq_kernel(body, "seq_scatter_%s_%d" % (kind, l), seq, jax.ShapeDtypeStruct((N_DEV,) + _shard_shape(cf, kind), BF16))(g)


def _rs_slab(ref, cf, kind, j):
    return ref.at[j] if kind in ('in', 'up') else _part(ref, cf, kind, j)


def _rs_start(cf, kind, l, g, after):
    def body(g_ref, recv_ref, after_ref, g_out, recv_out, ssem, rsem, token, lsem):
        x, y, c, me = _me()
        loc = pltpu.make_async_copy(_rs_slab(g_ref, cf, kind, me), recv_ref.at[me], lsem)
        loc.start()
        loc.wait()
        for k in range(1, N_DEV):
            tgt, pi = _peer(x, y, c, k)
            _rcopy(_rs_slab(g_ref, cf, kind, pi), recv_ref.at[me], ssem.at[k - 1], rsem.at[k - 1], tgt).start()
        token[...] = jnp.zeros_like(token)

    rshape = (N_DEV,) + _shard_shape(cf, kind)
    sems = pltpu.SemaphoreType.DMA((N_DEV - 1,))
    return pl.pallas_call(
        body, name="rs_start_%s_%d" % (kind, l), in_specs=[_HBM, _HBM, pl.BlockSpec(memory_space=pl.ANY)],
        out_specs=[_HBM, _HBM, _SEM, _SEM, pl.BlockSpec(memory_space=pltpu.VMEM)],
        out_shape=[pltpu.HBM(g.shape, BF16), pltpu.HBM(rshape, BF16), sems, sems, jax.ShapeDtypeStruct((8, LANE), F32)],
        input_output_aliases={0: 0, 1: 1}, scratch_shapes=[pltpu.SemaphoreType.DMA],
        compiler_params=pltpu.CompilerParams(has_side_effects=_EFFECT))(_hbm(g), _hbm(lax.empty(rshape, BF16)), after)


def _rs_wait(cf, kind, l, g, recv, ssem, rsem, after):
    def body(g_ref, recv_ref, ssem_ref, rsem_ref, after_ref, g_out, recv_out):
        x, y, c, me = _me()
        for k in range(1, N_DEV):
            tgt, pi = _peer(x, y, c, k)
            cp = _rcopy(_rs_slab(g_ref, cf, kind, pi), recv_ref.at[pi], ssem_ref.at[k - 1], rsem_ref.at[k - 1], tgt)
            cp.wait_send()
            cp.wait_recv()

    return pl.pallas_call(
        body, name="rs_wait_%s_%d" % (kind, l), in_specs=[_HBM, _HBM, _SEM, _SEM, pl.BlockSpec(memory_space=pl.ANY)],
        out_specs=[_HBM, _HBM], out_shape=[pltpu.HBM(g.shape, g.dtype), pltpu.HBM(recv.shape, recv.dtype)],
        input_output_aliases={0: 0, 1: 1}, compiler_params=pltpu.CompilerParams(has_side_effects=_EFFECT))(g, recv, ssem, rsem, after)[1]


def _adam_vals(w, g, m, v):
    m2 = ADAM_B1 * m + (1.0 - ADAM_B1) * g
    v2 = ADAM_B2 * v + (1.0 - ADAM_B2) * (g * g)
    mh = m2 / (1.0 - ADAM_B1 ** ADAM_STEP)
    vh = v2 / (1.0 - ADAM_B2 ** ADAM_STEP)
    return -ADAM_LR * (mh / (jnp.sqrt(vh) + ADAM_EPS) + ADAM_WD * w), m2, v2


def _row_tile(R, Cc, budget_elems):
    t = max(16, min(R, (budget_elems // max(Cc, 1)) // 16 * 16))
    while t > 16 and R % t:
        t -= 16
    return t if R % t == 0 else R


def _cast_bf16(name, w, after):
    Dp, R, Cc = w.shape
    tr = _row_tile(R, Cc, 1 << 20)

    def body(w_ref, after_ref, o_ref):
        o_ref[...] = w_ref[...].astype(BF16)

    spec = pl.BlockSpec((None, tr, Cc), lambda l, i: (l, i, 0))
    return pl.pallas_call(body, name=name, grid=(Dp, R // tr), in_specs=[spec, pl.BlockSpec(memory_space=pl.ANY)], out_specs=spec,
                          out_shape=jax.ShapeDtypeStruct(w.shape, BF16), compiler_params=_cp(("parallel", "parallel")))(w, after)


def _sum_adam(name, recvs, w, m, v):
    Dp, R, Cc = w.shape
    tr = _row_tile(R, Cc, 1 << 18)
    ni = R // tr

    def body(*refs):
        r_refs = refs[:Dp]
        w_ref, m_ref, v_ref, g_ref, d_ref, mo_ref, vo_ref = refs[Dp:]
        for layer in range(Dp):
            @pl.when(pl.program_id(0) == layer)
            def _():
                r_ref = r_refs[layer]
                g = r_ref[0].astype(F32)
                for s in range(1, N_DEV):
                    g = g + r_ref[s].astype(F32)
                dl, m2, v2 = _adam_vals(w_ref[...], g, m_ref[...], v_ref[...])
                g_ref[...] = g
                d_ref[...] = dl
                mo_ref[...] = m2
                vo_ref[...] = v2

    spec = pl.BlockSpec((None, tr, Cc), lambda l, i: (l, i, 0))
    rspecs = [pl.BlockSpec((N_DEV, tr, Cc), functools.partial(
        lambda l, i, layer: (0, jnp.where(l == layer, i, jnp.where(l > layer, ni - 1, 0)), 0), layer=layer)) for layer in range(Dp)]
    return pl.pallas_call(body, name=name, grid=(Dp, ni), in_specs=rspecs + [spec, spec, spec], out_specs=[spec] * 4,
                          out_shape=[jax.ShapeDtypeStruct(w.shape, F32)] * 4, compiler_params=_cp(("arbitrary", "arbitrary")))(*recvs, w, m, v)


def _adam(name, w, g, m, v):
    R, Cc = w.shape
    tr = _row_tile(R, Cc, 1 << 18)

    def body(w_ref, g_ref, m_ref, v_ref, d_ref, mo_ref, vo_ref):
        dl, m2, v2 = _adam_vals(w_ref[...], g_ref[...], m_ref[...], v_ref[...])
        d_ref[...] = dl
        mo_ref[...] = m2
        vo_ref[...] = v2

    spec = pl.BlockSpec((tr, Cc), lambda i: (i, 0))
    return pl.pallas_call(body, name=name, grid=(R // tr,), in_specs=[spec] * 4, out_specs=[spec] * 3,
                          out_shape=[jax.ShapeDtypeStruct(w.shape, F32)] * 3, compiler_params=_cp(("parallel",)))(w, g, m, v)


def _sum8(name, a):
    n, R, Cc = a.shape

    def body(a_ref, o_ref):
        s = a_ref[0]
        for k in range(1, n):
            s = s + a_ref[k]
        o_ref[...] = s

    return pl.pallas_call(body, name=name, in_specs=[pl.BlockSpec(memory_space=pltpu.VMEM)],
                          out_specs=pl.BlockSpec(memory_space=pltpu.VMEM), out_shape=jax.ShapeDtypeStruct((R, Cc), F32),
                          compiler_params=pltpu.CompilerParams(vmem_limit_bytes=VMEM_LIMIT))(a)


def _ada_fwd(cf, c9, ada_w):
    D = cf.D
    NS = ada_w.shape[-1]
    tk = _tile(D, 512)
    nk = D // tk

    def body(c_ref, w_ref, o_ref):
        kk = pl.program_id(1)
        s = _silu(c_ref[...]).astype(BF16)
        part = jnp.dot(s, w_ref[...].astype(BF16), preferred_element_type=F32)

        @pl.when(kk == 0)
        def _():
            o_ref[...] = part

        @pl.when(kk != 0)
        def _():
            o_ref[...] += part

    return pl.pallas_call(
        body, name="ada_fwd", grid=(DEPTH, nk),
        in_specs=[pl.BlockSpec((16, tk), lambda l, k: (0, k)), pl.BlockSpec((None, tk, NS), lambda l, k: (l, k, 0))],
        out_specs=pl.BlockSpec((None, 16, NS), lambda l, k: (l, 0, 0)),
        out_shape=jax.ShapeDtypeStruct((DEPTH, 16, NS), F32), compiler_params=_cp(("parallel", "arbitrary")))(c9, ada_w)


def _ada_bwd(cf, c9, ada_w, dm9):
    D = cf.D
    NS = ada_w.shape[-1]
    tk = _tile(D, 512)
    nk = D // tk

    def body(c_ref, w_ref, dm_ref, gw_ref, ds_ref):
        cv = c_ref[...]
        sg = _sigmoid(cv)
        dmb = dm_ref[...].astype(BF16)
        gw_ref[...] = lax.dot_general((cv * sg).astype(BF16), dmb, (_DN['tn'], ((), ())), preferred_element_type=F32)
        ds = lax.dot_general(dmb, w_ref[...].astype(BF16), (_DN['nt'], ((), ())), preferred_element_type=F32)
        ds_ref[...] = ds * (sg * (1.0 + cv * (1.0 - sg)))

    return pl.pallas_call(
        body, name="ada_bwd", grid=(DEPTH, nk),
        in_specs=[pl.BlockSpec((16, tk), lambda l, k: (0, k)), pl.BlockSpec((None, tk, NS), lambda l, k: (l, k, 0)),
                  pl.BlockSpec((None, 16, NS), lambda l, k: (l, 0, 0))],
        out_specs=[pl.BlockSpec((None, tk, NS), lambda l, k: (l, k, 0)), pl.BlockSpec((None, 16, tk), lambda l, k: (l, 0, k))],
        out_shape=[jax.ShapeDtypeStruct((DEPTH, D, NS), F32), jax.ShapeDtypeStruct((DEPTH, 16, D), F32)],
        compiler_params=_cp(("parallel", "parallel")))(c9, ada_w, dm9)


def _rope_tables(cf):
    L, LC = cf.L, cf.LC
    rows = L // GRID_W
    row = jnp.repeat(jnp.arange(rows, dtype=F32), GRID_W)
    col = jnp.tile(jnp.arange(GRID_W, dtype=F32), rows)
    nf = HEAD_DIM // 4
    inv = ROPE_THETA ** (-jnp.arange(nf, dtype=F32) / nf)
    ang = jnp.concatenate([row[:, None] * inv, col[:, None] * inv], axis=-1)
    cos, sin = jnp.cos(ang), jnp.sin(ang)
    cs = jnp.concatenate([jnp.ones((LC, HEAD_DIM), F32), jnp.concatenate([cos, cos], -1)], 0)
    sn = jnp.concatenate([jnp.zeros((LC, HEAD_DIM), F32), jnp.concatenate([-sin, sin], -1)], 0)
    return cs, sn


def _prep_tiles(cf, z, cs, sn, key):
    b = cf.OFF[key] // LANE
    return [(z, LANE, _col(b), True), (cs, LANE, _c0, False), (sn, LANE, _c0, False)]


_PREP = {'aq': _f_prep_norm, 'ak': _f_prep_norm, 'rq': _f_prep_plain, 'rk': _f_prep_scaled}


def _prep_fwd(cf, z, cs, sn, key, g):
    nh = cf.W[key] // LANE
    params = [(g, 'shared', True)] if g is not None else []
    return _row_fwd(cf, "prep_fwd_" + key, _PREP[key], _prep_tiles(cf, z, cs, sn, key), params,
                    [(LANE, _col(0), cf.W[key], BF16)], cf.TQ, nrep=nh)[0]


def _prep_bwd(cf, z, cs, sn, key, g, dt):
    nh = cf.W[key] // LANE
    params = [(g, 'shared', True)] if g is not None else []
    tg, pg = _row_bwd(cf, "prep_bwd_" + key, _PREP[key], _prep_tiles(cf, z, cs, sn, key), params,
                      [(dt, LANE, _col(0))], [(LANE, _col(0), cf.W[key], BF16)], cf.TQ, nrep=nh)
    return tg[0], (pg[0] if g is not None else None)


def _gate_params(cf, gup, gb):
    K = gup.shape[-1]
    gf = jnp.zeros((LANE, K), F32).at[0:GLA_RANK].set(gup[0])
    gbm = jnp.zeros((LANE, K), F32).at[GLA_RANK:2 * GLA_RANK].set(gup[1])
    out = []
    for p in range(K // LANE):
        cols = slice(p * LANE, (p + 1) * LANE)
        out += [(gf[:, cols], 'shared', True), (gbm[:, cols], 'shared', True), (gb[0:1, cols], 'shared', True), (gb[1:2, cols], 'shared', True)]
    return out


def _mix_tiles(cf, z, o, key):
    return [(o, LANE, _col(0), True), (z, LANE, _col(cf.OFF[key] // LANE), True)]


def _mid_io(cf, l, W, mod, x, y, zero=0.0):
    tiles = [(x, cf.D, _c0, True), (y, cf.D, _c0, True)]
    params = [(mod[2], 'stream', True), (W['norm2_g'][l] + zero, 'shared', True), (mod[3], 'stream', True), (mod[4], 'stream', True)]
    return tiles, params


class _BigWeights:
    def __init__(self, cf, shards):
        self.cf = cf
        self.whole = {(kind, l): _seq_gather(cf, kind, l, l * len(_KINDS) + n, shards[n])
                      for l in range(DEPTH) for n, kind in enumerate(_KINDS)}
        self.w_in = {}

    def get(self, kind, l, after=None):
        cf = self.cf
        if kind != 'in':
            return self.whole[(kind, l)]
        if l not in self.w_in:
            whole, _ = lax.optimization_barrier((self.whole[(kind, l)], after))
            self.w_in[l] = jnp.pad(_unshard_last(whole), ((0, 0), (0, cf.NZ - cf.NIN)))
        return self.w_in[l]


def _layer_fwd(cf, l, W, big, mod, x, h, cs, sn):
    T, D, Fd = cf.T, cf.D, cf.F
    z = _mm("z_%d" % l, h, big.get('in', l, h), 'nn', T, cf.NZ, D, F32, tm=T, tn=768, tk=512)
    qa = _prep_fwd(cf, z, cs, sn, 'aq', W['q_norm_g'][l])
    ka = _prep_fwd(cf, z, cs, sn, 'ak', W['k_norm_g'][l])
    qr = _prep_fwd(cf, z, cs, sn, 'rq', None)
    kr = _prep_fwd(cf, z, cs, sn, 'rk', None)
    o_att = _att_fwd(cf, qa, ka, z)
    o_ret = _ret_fwd(cf, qr, kr, z, W['ret_log_decay'][l])
    gates = _gate_params(cf, W['gla_gate_up'][l], W['gla_gate_b'][l])
    ga_tile = [(z, LANE, _col(cf.OFF['ga'] // LANE), True)]
    we = (cf.HG // 2) * N_DECAY * LANE
    ef, eb = _row_fwd(cf, "gates_fwd_%d" % l, _f_gla_pre, ga_tile, gates, [(we, _c0, we, F32), (we, _c0, we, F32)], GLA_CHUNK)
    o_f, sf, o_b, sb = _gla_fwd(cf, z, ef, eb)
    o_gla = o_f + o_b
    cat_r = _row_fwd(cf, "mixr_fwd_%d" % l, _f_gated_norm, _mix_tiles(cf, z, o_ret, 'rg'), [(W['ret_norm_g'][l], 'shared', True)],
                     [(LANE, _col(0), cf.HR * LANE, BF16)], cf.TQ, nrep=cf.HR)[0]
    cat_g = _row_fwd(cf, "mixg_fwd_%d" % l, _f_gated_norm, _mix_tiles(cf, z, o_gla, 'gr'), [(W['gla_norm_g'][l], 'shared', True)],
                     [(LANE, _col(0), cf.HG * LANE, BF16)], cf.TQ, nrep=cf.HG)[0]
    cat = jnp.concatenate([o_att, cat_r, cat_g], axis=-1)
    y = _mm("y_%d" % l, cat, big.get('out', l, cat), 'nn', T, D, D, F32, tm=T, tn=1024, tk=512)
    tiles, params = _mid_io(cf, l, W, mod, x, y)
    x1, h2 = _row_fwd(cf, "mid_fwd_%d" % l, _f_resid_norm_mod, tiles, params, [(D, _c0, D, F32), (D, _c0, D, BF16)], cf.TM)
    u = _mm("u_%d" % l, h2, big.get('up', l, h2), 'nn', T, 2 * Fd, D, F32, tm=T, tn=1024, tk=512)
    t = _conv_fwd(cf, u, W['conv_wb'][l])
    yff = _mm("yff_%d" % l, t, big.get('down', l, t), 'nn', T, D, Fd, F32, tm=T, tn=1024, tk=512)
    return dict(x=x, h=h, z=z, qa=qa, ka=ka, qr=qr, kr=kr, ef=ef, eb=eb, sf=sf, sb=sb, o_ret=o_ret, o_gla=o_gla, cat=cat, y=y,
                x1=x1, h2=h2, u=u, t=t, yff=yff, gates=gates)


def _layer_bwd(cf, l, W, big, mod, sv, dx1, dyff, cs, sn):
    T, D, Fd = cf.T, cf.D, cf.F
    g, rs = {}, {}
    sq = 2 * len(_KINDS) + (DEPTH - 1 - l) * len(_KINDS)
    rs['down'] = _rs_start(cf, 'down', l, _mm("gwd_%d" % l, sv['t'], dyff, 'tn', Fd, D, T, BF16, tm=1408, tn=2048, tk=1152), cs)
    wb = W['conv_wb'][l] + rs['down'][4][0, 0]
    dt = _mm("dt_%d" % l, dyff, big.get('down', l), 'nt', T, Fd, D, BF16, tm=T, tn=1408, tk=512)
    da, dv, g['conv_wb'] = _conv_bwd(cf, sv['u'], wb, dt)
    du = jnp.concatenate([da, dv], axis=-1)
    cu = 2 * Fd // N_DEV
    gwu = _mm("gwu_%d" % l, sv['h2'], du, 'tn', D, 2 * Fd, T, BF16, tm=D, tn=cu, tk=1152, out_shape=(N_DEV, D, cu),
              out_spec=pl.BlockSpec((None, D, cu), lambda i, j, k: (j, i, 0)))
    rs['up'] = _rs_start(cf, 'up', l, gwu, cs)
    dh2 = _mm("dh2_%d" % l, du, big.get('up', l), 'nt', T, D, 2 * Fd, BF16, tm=T, tn=1024, tk=1024)
    tiles, params = _mid_io(cf, l, W, mod, sv['x'], sv['y'], rs['up'][4][0, 0])
    (dx, dy), (g['m2'], g['norm2_g'], g['m3'], g['m4']) = _row_bwd(
        cf, "mid_bwd_%d" % l, _f_resid_norm_mod, tiles, params, [(dx1, D, _c0), (dh2, D, _c0)],
        [(D, _c0, D, F32), (D, _c0, D, BF16)], cf.TM)
    rs['out'] = _rs_start(cf, 'out', l, _mm("gwo_%d" % l, sv['cat'], dy, 'tn', D, D, T, BF16, tm=D, tn=1024, tk=1152), cs)
    dcat = _mm("dcat_%d" % l, dy, big.get('out', l), 'nt', T, D, D, BF16, tm=T, tn=1024, tk=1024)
    z = sv['z']
    (do_ret, drg), (g['ret_norm_g'],) = _row_bwd(
        cf, "mixr_bwd_%d" % l, _f_gated_norm, _mix_tiles(cf, z, sv['o_ret'], 'rg'), [(W['ret_norm_g'][l] + rs['out'][4][0, 0], 'shared', True)],
        [(dcat, LANE, _col(cf.HQ))], [(LANE, _col(0), cf.HR * LANE, F32), (LANE, _col(0), cf.HR * LANE, BF16)], cf.TQ, nrep=cf.HR)
    (do_gla, dgr), (g['gla_norm_g'],) = _row_bwd(
        cf, "mixg_bwd_%d" % l, _f_gated_norm, _mix_tiles(cf, z, sv['o_gla'], 'gr'), [(W['gla_norm_g'][l], 'shared', True)],
        [(dcat, LANE, _col(cf.HQ + cf.HR))], [(LANE, _col(0), cf.HG * LANE, F32), (LANE, _col(0), cf.HG * LANE, BF16)], cf.TQ, nrep=cf.HG)
    dqa, dka, dav = _att_bwd(cf, sv['qa'], sv['ka'], z, dcat)
    dqr, dkr, drv, dlg = _ret_bwd(cf, sv['qr'], sv['kr'], z, W['ret_log_decay'][l], do_ret)
    g['ret_log_decay'] = dlg[:, 0:2, 0].T
    dq_f, dk_f, dv_f, def_, dq_b, dk_b, dv_b, deb = _gla_bwd(cf, z, sv['ef'], sv['eb'], sv['sf'], sv['sb'], do_gla)
    dgq, dgk, dgv = dq_f + dq_b, dk_f + dk_b, dv_f + dv_b
    we = (cf.HG // 2) * N_DECAY * LANE
    ga_tile = [(z, LANE, _col(cf.OFF['ga'] // LANE), True)]
    (dga,), gg = _row_bwd(cf, "gates_bwd_%d" % l, _f_gla_pre, ga_tile, sv['gates'],
                          [(def_, we, _c0), (deb, we, _c0)], [(LANE, _c0, LANE, BF16)], GLA_CHUNK)
    ggf, ggb, gbf, gbb = [jnp.concatenate(gg[n::4], axis=-1) for n in range(4)]
    g['gla_gate_up'] = jnp.stack([ggf[0:GLA_RANK], ggb[GLA_RANK:2 * GLA_RANK]])
    g['gla_gate_b'] = jnp.concatenate([gbf, gbb], axis=0)
    daq, g['q_norm_g'] = _prep_bwd(cf, z, cs, sn, 'aq', W['q_norm_g'][l], dqa)
    dak, g['k_norm_g'] = _prep_bwd(cf, z, cs, sn, 'ak', W['k_norm_g'][l], dka)
    drq, _ = _prep_bwd(cf, z, cs, sn, 'rq', None, dqr)
    drk, _ = _prep_bwd(cf, z, cs, sn, 'rk', None, dkr)
    pad = jnp.zeros((T, cf.NZ - cf.OFF['ga'] - LANE), BF16)
    dz = jnp.concatenate([daq, dak, dav.astype(BF16), drq, drk, drv.astype(BF16), drg, dgq.astype(BF16), dgk.astype(BF16),
                          dgv.astype(BF16), dgr, dga, pad], axis=-1)
    gwi = _mm("gwi_%d" % l, sv['h'], dz, 'tn', D, cf.NZ, T, BF16, tm=D, tn=768, tk=1152)
    g['w_in_slabs'] = jnp.moveaxis(gwi[:, :cf.NIN].reshape(D, N_DEV, cf.NINS), 1, 0)
    g['zero'] = 0.0
    if l > 0:
        rs['in'] = _rs_start(cf, 'in', l, g['w_in_slabs'], cs)
        g['zero'] = rs['in'][4][0, 0]
    dh = _mm("dh_%d" % l, dz, big.get('in', l), 'nt', T, D, cf.NZ, BF16, tm=T, tn=1024, tk=768)
    g['rs'] = rs
    return dx, dh, g


_WEIGHTS = ['c_ctx', 'ada_w', 'ada_b', 'norm1_g', 'w_in', 'q_norm_g', 'k_norm_g', 'ret_log_decay', 'ret_norm_g',
            'gla_gate_up', 'gla_gate_b', 'gla_norm_g', 'w_out', 'norm2_g', 'w_up', 'conv_w', 'conv_b', 'w_down', 'final_norm_g']
_BIG = ['w_in', 'w_out', 'w_up', 'w_down']
_SMALL = [n for n in _WEIGHTS if n not in _BIG and n != 'ada_w']
_COL_SHARDED = ['gla_gate_up', 'gla_gate_b', 'conv_w']


def _pack(arrs):
    flat = jnp.concatenate([a.reshape(-1) for a in arrs])
    n = flat.shape[0]
    tot = -(-n // (8 * LANE)) * (8 * LANE)
    return jnp.pad(flat, (0, tot - n)).reshape(tot // LANE, LANE)


def _unpack(flat, shapes):
    out, o = [], 0
    for s in shapes:
        n = int(np.prod(s))
        out.append(flat[..., o:o + n].reshape(flat.shape[:-1] + tuple(s)))
        o += n
    return out


def _unshard_last(a):
    return jnp.moveaxis(a, 0, -2).reshape(a.shape[1:-1] + (N_DEV * a.shape[-1],))


def _step(cf, x, c, ctx, loss_target, w, m, v):
    T, D, Fd, L, LC = cf.T, cf.D, cf.F, cf.L, cf.LC
    _, _, _, me = _me()
    NS = w['ada_w'].shape[-1]

    c_all = _ag_small("ag_c", jnp.pad(c, ((0, 7), (0, 0))))[:, 0, :]
    c9 = jnp.concatenate([c_all, w['c_ctx'][None], jnp.zeros((7, D), F32)], axis=0)
    pm = _ada_fwd(cf, c9, w['ada_w'])
    pm_all = _ag_small("ag_mod", pm.reshape(DEPTH * 16, NS)).reshape(N_DEV, DEPTH, 16, NS)
    mod_all = _unshard_last(pm_all) + w['ada_b'][:, None, :]
    mod_own = lax.dynamic_index_in_dim(mod_all, me, axis=1, keepdims=False)
    mods = []
    for l in range(DEPTH):
        mods.append([jnp.stack([mod_all[l, 8, k * D:(k + 1) * D], mod_own[l, k * D:(k + 1) * D]])[:, None, :] for k in range(N_MOD)])

    shard_shapes = [w[n].shape for n in _COL_SHARDED]
    got = _ag_small("ag_smallw", _pack([w[n] for n in _COL_SHARDED]))
    full = dict(zip(_COL_SHARDED, [_unshard_last(a) for a in _unpack(got.reshape(N_DEV, -1), shard_shapes)]))

    small_done = full['conv_w'] + mod_all[0, 0, 0]
    big = _BigWeights(cf, [_cast_bf16("cast_" + n, w[n], small_done) for n in _BIG])
    conv_wb = jnp.concatenate([full['conv_w'], w['conv_b'][:, None, :], jnp.zeros((DEPTH, 4, Fd), F32)], axis=1)
    W = dict(conv_wb=conv_wb, gla_gate_up=full['gla_gate_up'], gla_gate_b=full['gla_gate_b'], ret_log_decay=w['ret_log_decay'])
    for n in ['q_norm_g', 'k_norm_g', 'ret_norm_g', 'gla_norm_g', 'norm1_g', 'norm2_g']:
        W[n] = w[n][:, None, :]

    cs, sn = _rope_tables(cf)
    x0 = jnp.concatenate([ctx[0], x[0]], axis=0)
    pre_tiles = [(x0, D, _c0, True)]

    def pre_params(zero):
        return [(W['norm1_g'][0] + zero, 'shared', True), (mods[0][0], 'stream', True), (mods[0][1], 'stream', True)]

    def tr_params(zero):
        return [(mods[0][5], 'stream', True), (W['norm1_g'][1] + zero, 'shared', True), (mods[1][0], 'stream', True), (mods[1][1], 'stream', True)]

    h0 = _row_fwd(cf, "pre_fwd", _f_norm_mod, pre_tiles, pre_params(0.0), [(D, _c0, D, BF16)], cf.TM)[0]
    sv0 = _layer_fwd(cf, 0, W, big, mods[0], x0, h0, cs, sn)
    tr_tiles = [(sv0['x1'], D, _c0, True), (sv0['yff'], D, _c0, True)]
    xb, hb = _row_fwd(cf, "tr_fwd", _f_resid_norm_mod, tr_tiles, tr_params(0.0), [(D, _c0, D, F32), (D, _c0, D, BF16)], cf.TM)
    sv1 = _layer_fwd(cf, 1, W, big, mods[1], xb, hb, cs, sn)
    tgt = jnp.concatenate([jnp.zeros((LC, D), F32), loss_target[0]], axis=0)
    dx1, dyff, dm5_1, g_final, ls = _loss_grad(cf, sv1['x1'], sv1['yff'], mods[1][5], w['final_norm_g'][None], tgt)
    loss = lax.psum(ls[0, 0], ("x", "y", "c"))

    dxb, dhb, g1 = _layer_bwd(cf, 1, W, big, mods[1], sv1, dx1, dyff, cs, sn)
    (dx1_0, dyff_0), (dm5_0, gn1_1, dm0_1, dm1_1) = _row_bwd(
        cf, "tr_bwd", _f_resid_norm_mod, tr_tiles, tr_params(g1['zero']), [(dxb, D, _c0), (dhb, D, _c0)],
        [(D, _c0, D, F32), (D, _c0, D, BF16)], cf.TM)
    dx0, dh0, g0 = _layer_bwd(cf, 0, W, big, mods[0], sv0, dx1_0, dyff_0, cs, sn)
    (dxa,), (gn1_0, dm0_0, dm1_0) = _row_bwd(cf, "pre_bwd", _f_first, pre_tiles, pre_params(0.0), [(dx0, D, _c0), (dh0, D, _c0)],
                                            [(D, _c0, D, F32)], cf.TM)
    grad_x = dxa[LC:][None]

    dmod = jnp.stack([jnp.concatenate([dm0_0, dm1_0, g0['m2'], g0['m3'], g0['m4'], dm5_0], axis=-1)[:, 0],
                      jnp.concatenate([dm0_1, dm1_1, g1['m2'], g1['m3'], g1['m4'], dm5_1], axis=-1)[:, 0]])
    dm_all = _ag_small("ag_dmod", jnp.pad(dmod.reshape(2 * DEPTH, N_MOD * D), ((0, 8 - 2 * DEPTH), (0, 0))))
    dm_all = dm_all[:, :2 * DEPTH].reshape(N_DEV, DEPTH, 2, N_MOD * D)
    dctx = _sum8("sum_dmodc", jnp.pad(dm_all[:, :, 0], ((0, 0), (0, 8 - DEPTH), (0, 0))))[:DEPTH]
    dm9 = jnp.concatenate([jnp.moveaxis(dm_all[:, :, 1], 0, 1), dctx[:, None]], axis=1)
    g_ada_b = _sum8("sum_adab", jnp.pad(jnp.moveaxis(dm9, 1, 0), ((0, 0), (0, 8 - DEPTH), (0, 0))))[:DEPTH]
    dm9s = lax.dynamic_slice_in_dim(jnp.pad(dm9, ((0, 0), (0, 7), (0, 0))), me * NS, NS, axis=2)
    g_ada_w, dsil = _ada_bwd(cf, c9, w['ada_w'], dm9s)
    g_cctx_part = dsil[0, 8]
    for l in range(1, DEPTH):
        g_cctx_part = g_cctx_part + dsil[l, 8]

    def both(key):
        return jnp.stack([g0[key], g1[key]])

    gsmall = dict(c_ctx=g_cctx_part, norm1_g=jnp.stack([gn1_0[0], gn1_1[0]]), q_norm_g=both('q_norm_g')[:, 0],
                  k_norm_g=both('k_norm_g')[:, 0], ret_log_decay=both('ret_log_decay'), ret_norm_g=both('ret_norm_g')[:, 0],
                  gla_gate_up=both('gla_gate_up'), gla_gate_b=both('gla_gate_b'), gla_norm_g=both('gla_norm_g')[:, 0],
                  norm2_g=both('norm2_g')[:, 0], conv_w=both('conv_wb')[:, 0:3], conv_b=both('conv_wb')[:, 3], final_norm_g=g_final[0])
    snames = [n for n in _SMALL if n != 'ada_b']
    sshapes = [gsmall[n].shape for n in snames]
    gs_all = _ag_small("ag_gsmall", _pack([gsmall[n] for n in snames]))
    gs = dict(zip(snames, _unpack(_sum8("sum_gsmall", gs_all).reshape(-1), sshapes)))
    gs['ada_b'] = g_ada_b
    for n in _COL_SHARDED:
        ns_ = w[n].shape[-1]
        gs[n] = lax.dynamic_slice_in_dim(gs[n], me * ns_, ns_, axis=gs[n].ndim - 1)

    g0['rs']['in'] = _rs_start(cf, 'in', 0, g0['w_in_slabs'], gs_all)
    out_g, out_d, out_m, out_v = {}, {}, {}, {}

    def update_big(kind, after):
        n = 'w_' + kind
        recvs = [_rs_wait(cf, kind, l, *gl['rs'][kind][:4], after) for l, gl in enumerate((g0, g1))]
        out_g[n], out_d[n], out_m[n], out_v[n] = _sum_adam("adam_" + n, recvs, w[n], m[n], v[n])
        return out_g[n]

    after = g0['rs']['in'][4]
    for kind in ['down', 'up', 'out']:
        after = update_big(kind, after)
    aw = [a.reshape(DEPTH * D, NS) for a in (w['ada_w'], g_ada_w, m['ada_w'], v['ada_w'])]
    out_g['ada_w'] = g_ada_w
    out_d['ada_w'], out_m['ada_w'], out_v['ada_w'] = [a.reshape(DEPTH, D, NS) for a in _adam("adam_ada_w", *aw)]
    shp = [w[n].shape for n in _SMALL]
    packed = [_pack([src[n] for n in _SMALL]) for src in (w, gs, m, v)]
    res = _adam("adam_small", *packed)
    for dst, pk in zip((out_d, out_m, out_v), res):
        dst.update(zip(_SMALL, _unpack(pk.reshape(-1), shp)))
    out_g.update({n: gs[n] for n in _SMALL})
    update_big('in', after)
    return (loss, grad_x, *[out_g[n] for n in _WEIGHTS], *[out_d[n] for n in _WEIGHTS], *[out_m[n] for n in _WEIGHTS],
            *[out_v[n] for n in _WEIGHTS])


def kernel(x, c, ctx, c_ctx, ada_w, ada_b, norm1_g, w_in, q_norm_g, k_norm_g, ret_log_decay, ret_norm_g, gla_gate_up, gla_gate_b, gla_norm_g, w_out, norm2_g, w_up, conv_w, conv_b, w_down, final_norm_g, loss_target, m_c_ctx, m_ada_w, m_ada_b, m_norm1_g, m_w_in, m_q_norm_g, m_k_norm_g, m_ret_log_decay, m_ret_norm_g, m_gla_gate_up, m_gla_gate_b, m_gla_norm_g, m_w_out, m_norm2_g, m_w_up, m_conv_w, m_conv_b, m_w_down, m_final_norm_g, v_c_ctx, v_ada_w, v_ada_b, v_norm1_g, v_w_in, v_q_norm_g, v_k_norm_g, v_ret_log_decay, v_ret_norm_g, v_gla_gate_up, v_gla_gate_b, v_gla_norm_g, v_w_out, v_norm2_g, v_w_up, v_conv_w, v_conv_b, v_w_down, v_final_norm_g):
    w = dict(c_ctx=c_ctx, ada_w=ada_w, ada_b=ada_b, norm1_g=norm1_g, w_in=w_in, q_norm_g=q_norm_g, k_norm_g=k_norm_g,
             ret_log_decay=ret_log_decay, ret_norm_g=ret_norm_g, gla_gate_up=gla_gate_up, gla_gate_b=gla_gate_b,
             gla_norm_g=gla_norm_g, w_out=w_out, norm2_g=norm2_g, w_up=w_up, conv_w=conv_w, conv_b=conv_b, w_down=w_down,
             final_norm_g=final_norm_g)
    m = dict(c_ctx=m_c_ctx, ada_w=m_ada_w, ada_b=m_ada_b, norm1_g=m_norm1_g, w_in=m_w_in, q_norm_g=m_q_norm_g,
             k_norm_g=m_k_norm_g, ret_log_decay=m_ret_log_decay, ret_norm_g=m_ret_norm_g, gla_gate_up=m_gla_gate_up,
             gla_gate_b=m_gla_gate_b, gla_norm_g=m_gla_norm_g, w_out=m_w_out, norm2_g=m_norm2_g, w_up=m_w_up,
             conv_w=m_conv_w, conv_b=m_conv_b, w_down=m_w_down, final_norm_g=m_final_norm_g)
    v = dict(c_ctx=v_c_ctx, ada_w=v_ada_w, ada_b=v_ada_b, norm1_g=v_norm1_g, w_in=v_w_in, q_norm_g=v_q_norm_g,
             k_norm_g=v_k_norm_g, ret_log_decay=v_ret_log_decay, ret_norm_g=v_ret_norm_g, gla_gate_up=v_gla_gate_up,
             gla_gate_b=v_gla_gate_b, gla_norm_g=v_gla_norm_g, w_out=v_w_out, norm2_g=v_norm2_g, w_up=v_w_up,
             conv_w=v_conv_w, conv_b=v_conv_b, w_down=v_w_down, final_norm_g=v_final_norm_g)
    return _step(_cfg(), x, c, ctx, loss_target, w, m, v)
```

```python
import functools
import math
import types

import jax
import jax.numpy as jnp
import numpy as np
from jax import lax
from jax.experimental import pallas as pl
from jax.experimental.pallas import tpu as pltpu
from jax.experimental.pallas import tpu_sc as plsc

F32 = jnp.float32
BF16 = jnp.bfloat16
HI = lax.Precision.HIGHEST

D_MODEL = 2048
SEQ = 2048
CTX_LEN = 256
GRID_W = 64
D_FF = 5632
DEPTH = 2
N_DEV = 8
HEAD_DIM = 128
ROPE_THETA = 10000.0
GLA_TAU = 16.0
GLA_RANK = 16
GLA_CHUNK = 64
GLA_SUB = 16
EPS = 1e-6
N_MOD = 6
ADAM_LR = 0.001
ADAM_B1 = 0.9
ADAM_B2 = 0.999
ADAM_EPS = 1e-08
ADAM_WD = 0.01
ADAM_STEP = 10
LANE = 128
VMEM_LIMIT = 56 * 1024 * 1024
NEG = -1e30


def _cfg():
    d = types.SimpleNamespace()
    d.D, d.L, d.LC, d.F = D_MODEL, SEQ, CTX_LEN, D_FF
    d.T = d.L + d.LC
    nm = d.D // HEAD_DIM
    d.HQ, d.HKV, d.HR, d.HG = nm // 2, nm // 8, nm // 4, nm // 4
    d.G = d.HQ // d.HKV
    w = dict(aq=d.HQ * 128, ak=d.HKV * 128, av=d.HKV * 128, rq=d.HR * 128, rk=d.HR * 128, rv=d.HR * 128,
             rg=d.HR * 128, gq=d.HG * 64, gk=d.HG * 64, gv=d.HG * 128, gr=d.HG * 128, ga=2 * GLA_RANK)
    off, o = {}, 0
    for k, v in w.items():
        off[k] = o
        o += v
    d.W, d.OFF, d.NIN = w, off, o
    d.NZ = -(-(off['ga'] + LANE) // 256) * 256
    d.NINS = d.NIN // N_DEV
    d.TM = math.gcd(d.LC, 128)
    d.TQ = math.gcd(d.LC, 256)
    return d


def _cp(sem=None):
    return pltpu.CompilerParams(dimension_semantics=sem, vmem_limit_bytes=VMEM_LIMIT)


def _tile(n, target, mult=LANE):
    t = min(n, target)
    t -= t % mult
    while t > mult and n % t:
        t -= mult
    return t if t > 0 and n % t == 0 else n


_DN = {'nn': ((1,), (0,)), 'nt': ((1,), (1,)), 'tn': ((0,), (0,))}


def _mm(name, a, b, kind, M, N, K, out_dtype, tm=768, tn=768, tk=1024, a_spec=None, b_spec=None,
        out_shape=None, out_spec=None, scatter=None):
    tm, tn = _tile(M, tm, 128), _tile(N, tn, 128)
    tk = _tile(K, tk, 128)
    nk = K // tk

    def body(a_ref, b_ref, *rest):
        if scatter is None:
            o_ref, acc = rest
        else:
            g_ref, recv_ref, o_ref, _, _, ssem, rsem, acc, lsem = rest
            first = jnp.logical_and(jnp.logical_and(pl.program_id(0) == 0, pl.program_id(1) == 0), pl.program_id(2) == 0)

            @pl.when(first)
            def _():
                _scatter_issue(scatter[0], scatter[1], g_ref, recv_ref, ssem, rsem, lsem)

        kk = pl.program_id(2)

        @pl.when(kk == 0)
        def _():
            acc[...] = jnp.zeros_like(acc)

        acc[...] += lax.dot_general(a_ref[...].astype(BF16), b_ref[...].astype(BF16), (_DN[kind], ((), ())),
                                    preferred_element_type=F32)

        @pl.when(kk == nk - 1)
        def _():
            o_ref[...] = acc[...].astype(o_ref.dtype)

    if a_spec is None:
        a_spec = pl.BlockSpec((tk, tm), lambda i, j, k: (k, i)) if kind == 'tn' else pl.BlockSpec((tm, tk), lambda i, j, k: (i, k))
    if b_spec is None:
        b_spec = pl.BlockSpec((tn, tk), lambda i, j, k: (j, k)) if kind == 'nt' else pl.BlockSpec((tk, tn), lambda i, j, k: (k, j))
    if out_spec is None:
        out_spec = pl.BlockSpec((tm, tn), lambda i, j, k: (i, j))
        out_shape = (M, N)
    grid = (M // tm, N // tn, nk)
    if scatter is None:
        return pl.pallas_call(
            body, name=name, grid=grid, in_specs=[a_spec, b_spec], out_specs=out_spec,
            out_shape=jax.ShapeDtypeStruct(out_shape, out_dtype), scratch_shapes=[pltpu.VMEM((tm, tn), F32)],
            compiler_params=_cp(("parallel", "parallel", "arbitrary")))(a, b)
    cf, skind, g = scatter
    rshape = (N_DEV,) + _shard_shape(cf, skind)
    sems = pltpu.SemaphoreType.DMA((N_DEV - 1,))
    return pl.pallas_call(
        body, name=name, grid=grid, in_specs=[a_spec, b_spec, _HBM, _HBM], out_specs=[out_spec, _HBM, _HBM, _SEM, _SEM],
        out_shape=[jax.ShapeDtypeStruct(out_shape, out_dtype), pltpu.HBM(g.shape, BF16), pltpu.HBM(rshape, BF16), sems, sems],
        input_output_aliases={2: 1, 3: 2}, scratch_shapes=[pltpu.VMEM((tm, tn), F32), pltpu.SemaphoreType.DMA],
        compiler_params=pltpu.CompilerParams(dimension_semantics=("arbitrary", "arbitrary", "arbitrary"), vmem_limit_bytes=VMEM_LIMIT,
                                             has_side_effects=_EFFECT))(a, b, _hbm(g), _hbm(lax.empty(rshape, BF16)))


def _tile_spec(tm, w, colf, nrep):
    if hasattr(colf, 'base'):
        assert colf.base % nrep == 0
        return pl.BlockSpec((tm, w * nrep), functools.partial(lambda i, b: (i, b), b=colf.base // nrep))
    return pl.BlockSpec((tm, w), lambda i: (i, 0))


def _head_cols(colf, w, r):
    return slice(r * w, (r + 1) * w) if hasattr(colf, 'base') else slice(None)


def _row_specs(cf, tm, tiles, params, nrep):
    nctx = cf.LC // tm
    specs = [_tile_spec(tm, w, colf, nrep) for arr, w, colf, _ in tiles]
    for arr, kind, _ in params:
        nd = arr.ndim
        if kind == 'shared':
            specs.append(pl.BlockSpec(arr.shape, functools.partial(lambda i, nd: (0,) * nd, nd=nd)))
        else:
            specs.append(pl.BlockSpec((None,) + arr.shape[1:],
                                      functools.partial(lambda i, nd, nctx: (jnp.where(i >= nctx, 1, 0),) + (0,) * (nd - 1), nd=nd, nctx=nctx)))
    return specs


def _row_fwd(cf, name, f, tiles, params, outs, tm, nrep=1):
    nt, npar = len(tiles), len(params)

    def body(*refs):
        pv = [r[...] for r in refs[nt:nt + npar]]
        for r in range(nrep):
            tv = [x[:, _head_cols(t[2], t[1], r)].astype(F32) for x, t in zip(refs[:nt], tiles)]
            res = f(*tv, *pv)
            for o, v, spec in zip(refs[nt + npar:], res, outs):
                o[:, _head_cols(spec[1], spec[0], r)] = v.astype(o.dtype)

    out_specs = [_tile_spec(tm, w, colf, nrep) for w, colf, _, _ in outs]
    out_shape = [jax.ShapeDtypeStruct((cf.T, tw), dt) for _, _, tw, dt in outs]
    return pl.pallas_call(
        body, name=name, grid=(cf.T // tm,), in_specs=_row_specs(cf, tm, tiles, params, nrep), out_specs=out_specs,
        out_shape=out_shape, compiler_params=_cp(("arbitrary",)))(*[t[0] for t in tiles], *[p[0] for p in params])


def _row_bwd(cf, name, f, tiles, params, cts, tgrads, tm, nrep=1):
    nt, npar, nc = len(tiles), len(params), len(cts)
    tdiff = [k for k, t in enumerate(tiles) if t[3]]
    pdiff = [k for k, p in enumerate(params) if p[2]]
    nctx = cf.LC // tm

    def body(*refs):
        i = pl.program_id(0)
        pv = [x[...] for x in refs[nt:nt + npar]]
        outs = refs[nt + npar + nc:]
        psum = None
        for r in range(nrep):
            tv = [x[:, _head_cols(t[2], t[1], r)].astype(F32) for x, t in zip(refs[:nt], tiles)]
            cv = tuple(x[:, _head_cols(c[2], c[1], r)].astype(F32) for x, c in zip(refs[nt + npar:nt + npar + nc], cts))

            def g(*diff, tv=tv):
                tv2, pv2 = list(tv), list(pv)
                for k, v in zip(tdiff, diff[:len(tdiff)]):
                    tv2[k] = v
                for k, v in zip(pdiff, diff[len(tdiff):]):
                    pv2[k] = v
                return tuple(f(*tv2, *pv2))

            _, vjp_fn = jax.vjp(g, *[tv[k] for k in tdiff], *[pv[k] for k in pdiff])
            grads = vjp_fn(cv)
            for o, gv, spec in zip(outs[:len(tdiff)], grads[:len(tdiff)], tgrads):
                o[:, _head_cols(spec[1], spec[0], r)] = gv.astype(o.dtype)
            pg = grads[len(tdiff):]
            psum = list(pg) if psum is None else [a + b for a, b in zip(psum, pg)]
        for n_, (o, gv) in enumerate(zip(outs[len(tdiff):], psum)):
            first = (i == 0) if params[pdiff[n_]][1] == 'shared' else jnp.logical_or(i == 0, i == nctx)

            @pl.when(first)
            def _():
                o[...] = gv

            @pl.when(jnp.logical_not(first))
            def _():
                o[...] += gv

    in_specs = _row_specs(cf, tm, tiles, params, nrep)
    in_specs += [_tile_spec(tm, w, colf, nrep) for _, w, colf in cts]
    out_specs = [_tile_spec(tm, w, colf, nrep) for w, colf, _, _ in tgrads]
    out_shape = [jax.ShapeDtypeStruct((cf.T, tw), dt) for _, _, tw, dt in tgrads]
    out_specs += _row_specs(cf, tm, [], [params[k] for k in pdiff], nrep)
    out_shape += [jax.ShapeDtypeStruct(params[k][0].shape, F32) for k in pdiff]
    res = pl.pallas_call(
        body, name=name, grid=(cf.T // tm,), in_specs=in_specs, out_specs=out_specs, out_shape=out_shape,
        compiler_params=_cp(("arbitrary",)))(*[t[0] for t in tiles], *[p[0] for p in params], *[c[0] for c in cts])
    return res[:len(tdiff)], res[len(tdiff):]


def _c0(r):
    return 0


def _col(base):
    def col(r):
        return base + r
    col.base = base
    return col


def _rms(x, g):
    return x * lax.rsqrt(jnp.mean(x * x, axis=-1, keepdims=True) + EPS) * g


def _sigmoid(x):
    return 1.0 / (1.0 + jnp.exp(-x))


def _silu(x):
    return x * _sigmoid(x)


def _f_norm_mod(x, g, sh, sc):
    return (_rms(x, g) * (1 + sc) + sh,)


def _f_resid_norm_mod(x, y, gate, g, sh, sc):
    x1 = x + gate * y
    return (x1, _rms(x1, g) * (1 + sc) + sh)


@jax.custom_vjp
def _swap_halves(t):
    return pltpu.roll(t, HEAD_DIM // 2, axis=1)


def _swap_fwd(t):
    return _swap_halves(t), None


def _swap_bwd(_, g):
    return (pltpu.roll(g, HEAD_DIM // 2, axis=1),)


_swap_halves.defvjp(_swap_fwd, _swap_bwd)


def _rope(t, cs, sn):
    return t * cs + _swap_halves(t) * sn


def _f_prep_norm(t, cs, sn, g):
    return (_rope(_rms(t, g), cs, sn),)


def _f_prep_plain(t, cs, sn):
    return (_rope(t, cs, sn),)


def _f_prep_scaled(t, cs, sn):
    return (_rope(t * (HEAD_DIM ** -0.5), cs, sn),)


def _log_sigmoid(x):
    return jnp.minimum(x, 0.0) - jnp.log(1.0 + jnp.exp(-jnp.abs(x)))


N_DECAY = 8


def _gla_masks(d, width):
    C, SB = GLA_CHUNK, GLA_SUB
    r = lax.broadcasted_iota(jnp.int32, (C, C), 0)
    m = lax.broadcasted_iota(jnp.int32, (C, C), 1)
    rr = lax.broadcasted_iota(jnp.int32, (C, width), 0)
    allowed = (m <= r) if d == 0 else (m >= r)
    blocks, vis = [allowed], []
    for b in range(C // SB):
        blocks.append((m < SB * b) if d == 0 else (m >= SB * (b + 1)))
        vis.append((rr < SB * (b + 1)) if d == 0 else (rr >= SB * b))
    cm = jnp.concatenate([x.astype(F32) for x in blocks] + [jnp.ones((C, C), F32)], axis=0)
    return cm, allowed, vis


def _gla_decays(la, d):
    C, SB = GLA_CHUNK, GLA_SUB
    nsb = C // SB
    cm, _, vis = _gla_masks(d, la.shape[-1])
    cums = jnp.dot(cm, la, precision=HI, preferred_element_type=F32)
    cum, tot = cums[0:C], cums[(1 + nsb) * C:]
    refs = [cums[(1 + b) * C:(2 + b) * C] for b in range(nsb)]
    e1 = jnp.concatenate([jnp.exp(cum[b * SB:(b + 1) * SB] - refs[b][b * SB:(b + 1) * SB]) for b in range(nsb)], axis=0)
    e2 = [jnp.where(vis[b], jnp.exp(jnp.where(vis[b], refs[b] - cum, 0.0)), 0.0) for b in range(nsb)]
    return [e1] + e2 + [jnp.exp(cum), jnp.exp(tot - cum), jnp.exp(tot)]


def _f_gla_pre(ga, *per_pair):
    gab = ga.astype(BF16)
    outs = [[], []]
    for p in range(len(per_pair) // 4):
        gf, gb, bf, bb = per_pair[4 * p:4 * p + 4]
        for d, (gm, bm) in enumerate(((gf, bf), (gb, bb))):
            la = _log_sigmoid(jnp.dot(gab, gm.astype(BF16), preferred_element_type=F32) + bm) / GLA_TAU
            outs[d] += _gla_decays(la, d)
    return tuple(jnp.concatenate(o, axis=-1) for o in outs)


def _f_gated_norm(o, g, n):
    return (_rms(o, n) * _silu(g),)


def _f_first(x, g, sh, sc):
    return (x, _rms(x, g) * (1 + sc) + sh)


def _loss_grad(cf, x1, yff, gate, gfin, tgt):
    tm, T, D = cf.TM, cf.T, cf.D
    nctx = cf.LC // tm

    def lossf(x1v, yv, gt, gf, tg):
        y = _rms(x1v + gt * yv, gf)
        e = y - tg
        return 0.5 * jnp.sum(jnp.mean(e * e, axis=-1, keepdims=True), axis=0, keepdims=True)

    def body(x1_ref, y_ref, gt_ref, gf_ref, tg_ref, dx_ref, dy_ref, dgt_ref, dgf_ref, ls_ref):
        i = pl.program_id(0)
        lat = (i >= nctx).astype(F32)
        val, vjp_fn = jax.vjp(lossf, x1_ref[...], y_ref[...].astype(F32), gt_ref[...], gf_ref[...], tg_ref[...])
        dx, dy, dgt, dgf, _ = vjp_fn(jnp.ones((1, 1), F32) * lat)
        dx_ref[...] = dx
        dy_ref[...] = dy.astype(dy_ref.dtype)
        first_s = jnp.logical_or(i == 0, i == nctx)

        @pl.when(first_s)
        def _():
            dgt_ref[...] = dgt

        @pl.when(jnp.logical_not(first_s))
        def _():
            dgt_ref[...] += dgt

        @pl.when(i == 0)
        def _():
            dgf_ref[...] = dgf
            ls_ref[...] = jnp.zeros_like(ls_ref) + val * lat

        @pl.when(i != 0)
        def _():
            dgf_ref[...] += dgf
            ls_ref[...] += val * lat

    row = pl.BlockSpec((tm, D), lambda i: (i, 0))
    strm = pl.BlockSpec((None, 1, D), lambda i: (jnp.where(i >= nctx, 1, 0), 0, 0))
    one = pl.BlockSpec((1, D), lambda i: (0, 0))
    return pl.pallas_call(
        body, name="loss_grad", grid=(T // tm,), in_specs=[row, row, strm, one, row],
        out_specs=[row, row, strm, one, pl.BlockSpec((8, LANE), lambda i: (0, 0))],
        out_shape=[jax.ShapeDtypeStruct((T, D), F32), jax.ShapeDtypeStruct((T, D), BF16),
                   jax.ShapeDtypeStruct((2, 1, D), F32), jax.ShapeDtypeStruct((1, D), F32),
                   jax.ShapeDtypeStruct((8, LANE), F32)],
        compiler_params=_cp(("arbitrary",)))(x1, yff, gate, gfin, tgt)


def _att_mask(cf, i, tq):
    col = lax.broadcasted_iota(jnp.int32, (tq, cf.T), 1)
    return jnp.logical_or(col < cf.LC, i >= cf.LC // tq)


def _att_probs(q, k, mask):
    s = lax.dot_general(q, k, (_DN['nt'], ((), ())), preferred_element_type=F32) * (HEAD_DIM ** -0.5)
    s = jnp.where(mask, s, NEG)
    e = jnp.exp(s - jnp.max(s, axis=-1, keepdims=True))
    return e / jnp.sum(e, axis=-1, keepdims=True)


def _att_fwd(cf, q, k, z):
    tq, T, G = cf.TQ, cf.T, cf.G
    vb = cf.OFF['av'] // LANE

    def body(q_ref, k_ref, v_ref, o_ref):
        mask = _att_mask(cf, pl.program_id(1), tq)
        kv, vv = k_ref[...], v_ref[...].astype(BF16)
        for j in range(G):
            p = _att_probs(q_ref[:, j * LANE:(j + 1) * LANE], kv, mask)
            o_ref[:, j * LANE:(j + 1) * LANE] = jnp.dot(p.astype(BF16), vv, preferred_element_type=F32).astype(o_ref.dtype)

    return pl.pallas_call(
        body, name="att_fwd", grid=(cf.HKV, T // tq),
        in_specs=[pl.BlockSpec((tq, G * LANE), lambda g, i: (i, g)), pl.BlockSpec((T, LANE), lambda g, i: (0, g)),
                  pl.BlockSpec((T, LANE), lambda g, i: (0, vb + g))],
        out_specs=pl.BlockSpec((tq, G * LANE), lambda g, i: (i, g)),
        out_shape=jax.ShapeDtypeStruct((T, cf.HQ * LANE), BF16), compiler_params=_cp(("arbitrary", "arbitrary")))(q, k, z)


def _att_bwd(cf, q, k, z, dcat):
    tq, T, G = cf.TQ, cf.T, cf.G
    vb = cf.OFF['av'] // LANE
    sc = HEAD_DIM ** -0.5

    def body(q_ref, k_ref, v_ref, do_ref, dq_ref, dk_ref, dv_ref):
        i = pl.program_id(1)
        mask = _att_mask(cf, i, tq)
        kv, vv = k_ref[...], v_ref[...].astype(BF16)
        dk = jnp.zeros((T, LANE), F32)
        dv = jnp.zeros((T, LANE), F32)
        for j in range(G):
            qj = q_ref[:, j * LANE:(j + 1) * LANE]
            do = do_ref[:, j * LANE:(j + 1) * LANE]
            p = _att_probs(qj, kv, mask)
            dv += lax.dot_general(p.astype(BF16), do, (_DN['tn'], ((), ())), preferred_element_type=F32)
            dp = lax.dot_general(do, vv, (_DN['nt'], ((), ())), preferred_element_type=F32)
            ds = p * (dp - jnp.sum(dp * p, axis=-1, keepdims=True)) * sc
            dsb = ds.astype(BF16)
            dq_ref[:, j * LANE:(j + 1) * LANE] = jnp.dot(dsb, kv, preferred_element_type=F32)
            dk += lax.dot_general(dsb, qj, (_DN['tn'], ((), ())), preferred_element_type=F32)

        @pl.when(i == 0)
        def _():
            dk_ref[...] = dk
            dv_ref[...] = dv

        @pl.when(i != 0)
        def _():
            dk_ref[...] += dk
            dv_ref[...] += dv

    qs = pl.BlockSpec((tq, G * LANE), lambda g, i: (i, g))
    ks = pl.BlockSpec((T, LANE), lambda g, i: (0, g))
    return pl.pallas_call(
        body, name="att_bwd", grid=(cf.HKV, T // tq),
        in_specs=[qs, ks, pl.BlockSpec((T, LANE), lambda g, i: (0, vb + g)), qs],
        out_specs=[qs, ks, ks],
        out_shape=[jax.ShapeDtypeStruct((T, cf.HQ * LANE), F32), jax.ShapeDtypeStruct((T, cf.HKV * LANE), F32),
                   jax.ShapeDtypeStruct((T, cf.HKV * LANE), F32)],
        compiler_params=_cp(("arbitrary", "arbitrary")))(q, k, z, dcat)


def _ret_masks(cf, i, tq, lgf, lgb):
    T, LC = cf.T, cf.LC
    row = lax.broadcasted_iota(jnp.int32, (tq, T), 0) + i * tq
    col = lax.broadcasted_iota(jnp.int32, (tq, T), 1)

    def pb(n):
        return jnp.where(n < LC, LC - 1 - n, T + LC - 1 - n)

    relf = row - col
    relb = pb(row) - pb(col)
    okf, okb = relf >= 0, relb >= 0
    rf = jnp.where(okf, relf, 0).astype(F32)
    rb = jnp.where(okb, relb, 0).astype(F32)
    mf = jnp.where(okf, jnp.exp(lgf * rf), 0.0)
    mb = jnp.where(okb, jnp.exp(lgb * rb), 0.0)
    return mf, mb, rf, rb


def _ret_fwd(cf, q, k, z, lg):
    tq, T = cf.TQ, cf.T
    vb = cf.OFF['rv'] // LANE

    def body(lg_ref, q_ref, k_ref, v_ref, o_ref):
        h, i = pl.program_id(0), pl.program_id(1)
        mf, mb, _, _ = _ret_masks(cf, i, tq, lg_ref[0, h], lg_ref[1, h])
        a = lax.dot_general(q_ref[...], k_ref[...], (_DN['nt'], ((), ())), preferred_element_type=F32)
        p = (a * (mf + mb)).astype(BF16)
        o_ref[...] = jnp.dot(p, v_ref[...].astype(BF16), preferred_element_type=F32)

    return pl.pallas_call(
        body, name="ret_fwd", grid=(cf.HR, T // tq),
        in_specs=[pl.BlockSpec(memory_space=pltpu.SMEM), pl.BlockSpec((tq, LANE), lambda h, i: (i, h)),
                  pl.BlockSpec((T, LANE), lambda h, i: (0, h)), pl.BlockSpec((T, LANE), lambda h, i: (0, vb + h))],
        out_specs=pl.BlockSpec((tq, LANE), lambda h, i: (i, h)),
        out_shape=jax.ShapeDtypeStruct((T, cf.HR * LANE), F32), compiler_params=_cp(("arbitrary", "arbitrary")))(lg, q, k, z)


def _ret_bwd(cf, q, k, z, lg, do):
    tq, T = cf.TQ, cf.T
    vb = cf.OFF['rv'] // LANE

    def body(lg_ref, q_ref, k_ref, v_ref, do_ref, dq_ref, dk_ref, dv_ref, dlg_ref):
        h, i = pl.program_id(0), pl.program_id(1)
        mf, mb, rf, rb = _ret_masks(cf, i, tq, lg_ref[0, h], lg_ref[1, h])
        qv, kv, vv = q_ref[...], k_ref[...], v_ref[...].astype(BF16)
        dob = do_ref[...].astype(BF16)
        a = lax.dot_general(qv, kv, (_DN['nt'], ((), ())), preferred_element_type=F32)
        m = mf + mb
        p = (a * m).astype(BF16)
        dv = lax.dot_general(p, dob, (_DN['tn'], ((), ())), preferred_element_type=F32)
        dp = lax.dot_general(dob, vv, (_DN['nt'], ((), ())), preferred_element_type=F32)
        da = (dp * m).astype(BF16)
        dq_ref[...] = jnp.dot(da, kv, preferred_element_type=F32)
        dk = lax.dot_general(da, qv, (_DN['tn'], ((), ())), preferred_element_type=F32)
        dm = dp * a
        dlf = jnp.sum(jnp.sum(dm * mf * rf, axis=-1, keepdims=True), axis=0, keepdims=True)
        dlb = jnp.sum(jnp.sum(dm * mb * rb, axis=-1, keepdims=True), axis=0, keepdims=True)
        rid = lax.broadcasted_iota(jnp.int32, (8, LANE), 0)
        dl = jnp.where(rid == 0, dlf, jnp.where(rid == 1, dlb, 0.0))

        @pl.when(i == 0)
        def _():
            dk_ref[...] = dk
            dv_ref[...] = dv
            dlg_ref[...] = dl

        @pl.when(i != 0)
        def _():
            dk_ref[...] += dk
            dv_ref[...] += dv
            dlg_ref[...] += dl

    qs = pl.BlockSpec((tq, LANE), lambda h, i: (i, h))
    ks = pl.BlockSpec((T, LANE), lambda h, i: (0, h))
    return pl.pallas_call(
        body, name="ret_bwd", grid=(cf.HR, T // tq),
        in_specs=[pl.BlockSpec(memory_space=pltpu.SMEM), qs, ks, pl.BlockSpec((T, LANE), lambda h, i: (0, vb + h)), qs],
        out_specs=[qs, ks, ks, pl.BlockSpec((None, 8, LANE), lambda h, i: (h, 0, 0))],
        out_shape=[jax.ShapeDtypeStruct((T, cf.HR * LANE), F32)] * 3 + [jax.ShapeDtypeStruct((cf.HR, 8, LANE), F32)],
        compiler_params=_cp(("arbitrary", "arbitrary")))(lg, q, k, z, do)


def _gla_step(q, k, v, es, st, lmask, allowed):
    C, SB = GLA_CHUNK, GLA_SUB
    nsb = C // SB
    e1, e2, e3, e4, e5 = es[0], es[1:1 + nsb], es[1 + nsb], es[2 + nsb], es[3 + nsb]
    qs = q * lmask * ((HEAD_DIM // 2) ** -0.5)
    ks = k * lmask
    qt = qs * e1
    rows = [lax.dot_general(qt[b * SB:(b + 1) * SB], ks * e2[b], (_DN['nt'], ((), ())), precision=lax.Precision.HIGH,
                            preferred_element_type=F32) for b in range(nsb)]
    att = jnp.where(allowed, jnp.concatenate(rows, axis=0), 0.0)
    o = jnp.dot(att.astype(BF16), v.astype(BF16), preferred_element_type=F32)
    o += lax.dot_general((qs * e3).astype(BF16), st.astype(BF16), (_DN['nt'], ((), ())), preferred_element_type=F32)
    kd = (ks * e4).astype(BF16)
    st_new = st * jnp.concatenate([e5, e5], axis=0) + lax.dot_general(v.astype(BF16), kd, (_DN['tn'], ((), ())), preferred_element_type=F32)
    return o, st_new


def _gla_allowed(d):
    r = lax.broadcasted_iota(jnp.int32, (GLA_CHUNK, GLA_CHUNK), 0)
    m = lax.broadcasted_iota(jnp.int32, (GLA_CHUNK, GLA_CHUNK), 1)
    return (m <= r) if d == 0 else (m >= r)


def _gla_chunk_id(cf, s, d):
    if d == 0:
        return s
    nct, nc = cf.LC // GLA_CHUNK, cf.T // GLA_CHUNK
    return jnp.where(s < nct, nct - 1 - s, nc + nct - 1 - s)


def _gla_lmask(h):
    return (lax.broadcasted_iota(jnp.int32, (1, LANE), 1) // (LANE // 2) == h).astype(F32)


_GLA_CHAINS = [(h, d) for h in range(2) for d in range(2)]


def _gla_row_specs(cf, nc, reverse):
    def rowblk(s, d):
        return _gla_chunk_id(cf, nc - 1 - s if reverse else s, d)

    def spec(width, base, d, per_pair=1):
        return pl.BlockSpec((GLA_CHUNK, width), functools.partial(lambda p, s, base, d: (rowblk(s, d), base + per_pair * p), base=base, d=d))

    def state(d):
        return pl.BlockSpec((2, None, LANE, LANE), functools.partial(lambda p, s, d: (p, rowblk(s, d), 0, 0), d=d))

    return spec, state


def _gla_fwd(cf, z, ef, eb):
    T, C = cf.T, GLA_CHUNK
    nc = T // C
    qb, kb, vb = cf.OFF['gq'] // LANE, cf.OFF['gk'] // LANE, cf.OFF['gv'] // (2 * LANE)
    spec, state = _gla_row_specs(cf, nc, False)

    def body(qf, kf, vf, e_f, qb_, kb_, vb_, e_b, of, sf, ob, sb, st_scr):
        @pl.when(pl.program_id(1) == 0)
        def _():
            st_scr[...] = jnp.zeros_like(st_scr)

        io = [(qf, kf, vf, e_f, of, sf), (qb_, kb_, vb_, e_b, ob, sb)]
        for ci, (h, d) in enumerate(_GLA_CHAINS):
            q, k, v, e, o_ref, s_ref = io[d]
            cols = slice(h * LANE, (h + 1) * LANE)
            es = [e[:, n * LANE:(n + 1) * LANE] for n in range(N_DECAY)]
            st = st_scr[ci]
            s_ref[h] = st
            o, stn = _gla_step(q[...], k[...], v[:, cols], es, st, _gla_lmask(h), _gla_allowed(d))
            st_scr[ci] = stn
            o_ref[:, cols] = o

    ins, outs = [], []
    for d in range(2):
        ins += [spec(LANE, qb, d), spec(LANE, kb, d), spec(2 * LANE, vb, d), spec(N_DECAY * LANE, 0, d)]
        outs += [spec(2 * LANE, 0, d), state(d)]
    oshape = [jax.ShapeDtypeStruct((T, cf.HG * LANE), F32), jax.ShapeDtypeStruct((cf.HG, nc, LANE, LANE), F32)]
    return pl.pallas_call(
        body, name="gla_fwd", grid=(cf.HG // 2, nc), in_specs=ins, out_specs=outs, out_shape=oshape * 2,
        scratch_shapes=[pltpu.VMEM((4, LANE, LANE), F32)],
        compiler_params=_cp(("arbitrary", "arbitrary")))(z, z, z, ef, z, z, z, eb)


def _gla_bwd(cf, z, ef, eb, sf, sb, do):
    T, C = cf.T, GLA_CHUNK
    nc = T // C
    qb, kb, vb = cf.OFF['gq'] // LANE, cf.OFF['gk'] // LANE, cf.OFF['gv'] // (2 * LANE)
    spec, state = _gla_row_specs(cf, nc, True)

    def body(*refs):
        ins = [refs[0:6], refs[6:12]]
        outs = [refs[12:16], refs[16:20]]
        dst_scr = refs[20]

        @pl.when(pl.program_id(1) == 0)
        def _():
            dst_scr[...] = jnp.zeros_like(dst_scr)

        acc = [None, None]
        for ci, (h, d) in enumerate(_GLA_CHAINS):
            q, k, v, e, s_ref, do_ref = ins[d]
            cols = slice(h * LANE, (h + 1) * LANE)
            es = [e[:, n * LANE:(n + 1) * LANE] for n in range(N_DECAY)]
            step = functools.partial(_gla_step, lmask=_gla_lmask(h), allowed=_gla_allowed(d))
            _, vjp_fn = jax.vjp(step, q[...], k[...], v[:, cols], es, s_ref[h])
            dq, dk, dv, des, dst = vjp_fn((do_ref[:, cols], dst_scr[ci]))
            dst_scr[ci] = dst
            outs[d][2][:, cols] = dv
            part = [dq, dk] + list(des)
            acc[d] = part if acc[d] is None else [a + b for a, b in zip(acc[d], part)]
        for d in range(2):
            dq_ref, dk_ref, _, de_ref = outs[d]
            dq_ref[...] = acc[d][0]
            dk_ref[...] = acc[d][1]
            for n in range(N_DECAY):
                de_ref[:, n * LANE:(n + 1) * LANE] = acc[d][2 + n]

    in_specs, out_specs = [], []
    for d in range(2):
        in_specs += [spec(LANE, qb, d), spec(LANE, kb, d), spec(2 * LANE, vb, d), spec(N_DECAY * LANE, 0, d), state(d), spec(2 * LANE, 0, d)]
        out_specs += [spec(LANE, 0, d), spec(LANE, 0, d), spec(2 * LANE, 0, d), spec(N_DECAY * LANE, 0, d)]
    npair = cf.HG // 2
    oshape = [jax.ShapeDtypeStruct((T, npair * LANE), F32), jax.ShapeDtypeStruct((T, npair * LANE), F32),
              jax.ShapeDtypeStruct((T, cf.HG * LANE), F32), jax.ShapeDtypeStruct((T, npair * N_DECAY * LANE), F32)]
    return pl.pallas_call(
        body, name="gla_bwd", grid=(npair, nc), in_specs=in_specs, out_specs=out_specs, out_shape=oshape * 2,
        scratch_shapes=[pltpu.VMEM((4, LANE, LANE), F32)],
        compiler_params=_cp(("arbitrary", "arbitrary")))(z, z, z, ef, sf, do, z, z, z, eb, sb, do)


def _conv_parts(cf, a, w_ref):
    T, LC = cf.T, cf.LC
    rid = lax.broadcasted_iota(jnp.int32, a.shape, 0)
    first = jnp.logical_or(rid == 0, rid == LC)
    last = jnp.logical_or(rid == LC - 1, rid == T - 1)
    ap = jnp.where(first, 0.0, pltpu.roll(a, 1, axis=0))
    an = jnp.where(last, 0.0, pltpu.roll(a, T - 1, axis=0))
    w0, w1, w2, b = w_ref[0:1, :], w_ref[1:2, :], w_ref[2:3, :], w_ref[3:4, :]
    ac = ap * w0 + a * w1 + an * w2 + b
    return ap, an, ac, first, last, (w0, w1, w2)


def _conv_fwd(cf, u, wb):
    T, Fd = cf.T, cf.F
    tc = _tile(Fd, 512)
    nj = Fd // tc

    def body(a_ref, v_ref, w_ref, t_ref):
        _, _, ac, _, _, _ = _conv_parts(cf, a_ref[...], w_ref)
        t_ref[...] = (_silu(ac) * v_ref[...]).astype(t_ref.dtype)

    return pl.pallas_call(
        body, name="conv_fwd", grid=(nj,),
        in_specs=[pl.BlockSpec((T, tc), lambda j: (0, j)), pl.BlockSpec((T, tc), lambda j: (0, nj + j)),
                  pl.BlockSpec((8, tc), lambda j: (0, j))],
        out_specs=pl.BlockSpec((T, tc), lambda j: (0, j)), out_shape=jax.ShapeDtypeStruct((T, Fd), BF16),
        compiler_params=_cp(("parallel",)))(u, u, wb)


def _conv_bwd(cf, u, wb, dt):
    T, Fd = cf.T, cf.F
    tc = _tile(Fd, 256)
    nj = Fd // tc

    def body(a_ref, v_ref, w_ref, dt_ref, da_ref, dv_ref, dw_ref):
        a, v, dtv = a_ref[...], v_ref[...], dt_ref[...].astype(F32)
        ap, an, ac, first, last, (w0, w1, w2) = _conv_parts(cf, a, w_ref)
        sg = _sigmoid(ac)
        dv_ref[...] = (dtv * ac * sg).astype(dv_ref.dtype)
        dac = dtv * v * (sg * (1.0 + ac * (1.0 - sg)))
        from_next = pltpu.roll(jnp.where(first, 0.0, dac), T - 1, axis=0)
        from_prev = pltpu.roll(jnp.where(last, 0.0, dac), 1, axis=0)
        da_ref[...] = (dac * w1 + from_next * w0 + from_prev * w2).astype(da_ref.dtype)
        rows = [jnp.sum(dac * ap, axis=0, keepdims=True), jnp.sum(dac * a, axis=0, keepdims=True),
                jnp.sum(dac * an, axis=0, keepdims=True), jnp.sum(dac, axis=0, keepdims=True)]
        rid = lax.broadcasted_iota(jnp.int32, (8, tc), 0)
        dw = jnp.zeros((8, tc), F32)
        for n_, rw in enumerate(rows):
            dw = jnp.where(rid == n_, rw, dw)
        dw_ref[...] = dw

    col = pl.BlockSpec((T, tc), lambda j: (0, j))
    return pl.pallas_call(
        body, name="conv_bwd", grid=(nj,),
        in_specs=[col, pl.BlockSpec((T, tc), lambda j: (0, nj + j)), pl.BlockSpec((8, tc), lambda j: (0, j)), col],
        out_specs=[col, col, pl.BlockSpec((8, tc), lambda j: (0, j))],
        out_shape=[jax.ShapeDtypeStruct((T, Fd), BF16), jax.ShapeDtypeStruct((T, Fd), BF16), jax.ShapeDtypeStruct((8, Fd), F32)],
        compiler_params=_cp(("parallel",)))(u, u, wb, dt)


def _me():
    x, y, c = lax.axis_index("x"), lax.axis_index("y"), lax.axis_index("c")
    return x, y, c, 4 * x + 2 * y + c


def _peer(x, y, c, k):
    px = 1 - x if (k >> 2) & 1 else x
    py = 1 - y if (k >> 1) & 1 else y
    pc = 1 - c if k & 1 else c
    return (px, py, pc), 4 * px + 2 * py + pc


def _rcopy(src, dst, ss, rs, tgt):
    return pltpu.make_async_remote_copy(src_ref=src, dst_ref=dst, send_sem=ss, recv_sem=rs, device_id=tgt,
                                        device_id_type=pl.DeviceIdType.MESH)


def _ag_small(name, v):
    R, Cc = v.shape

    def body(v_ref, o_ref, ssem, rsem, lsem):
        x, y, c, me = _me()
        loc = pltpu.make_async_copy(v_ref, o_ref.at[me], lsem)
        loc.start()
        sends = []
        for k in range(1, N_DEV):
            tgt, _ = _peer(x, y, c, k)
            cp = _rcopy(v_ref, o_ref.at[me], ssem.at[k - 1], rsem.at[k - 1], tgt)
            cp.start()
            sends.append(cp)
        for k in range(1, N_DEV):
            tgt, pi = _peer(x, y, c, k)
            _rcopy(v_ref, o_ref.at[pi], ssem.at[k - 1], rsem.at[k - 1], tgt).wait_recv()
        for cp in sends:
            cp.wait_send()
        loc.wait()

    vm = pl.BlockSpec(memory_space=pltpu.VMEM)
    return pl.pallas_call(
        body, name=name, in_specs=[vm], out_specs=vm, out_shape=jax.ShapeDtypeStruct((N_DEV, R, Cc), v.dtype),
        scratch_shapes=[pltpu.SemaphoreType.DMA((N_DEV - 1,)), pltpu.SemaphoreType.DMA((N_DEV - 1,)), pltpu.SemaphoreType.DMA],
        compiler_params=pltpu.CompilerParams(vmem_limit_bytes=VMEM_LIMIT))(v)


_HBM = pl.BlockSpec(memory_space=pltpu.HBM)
_SEM = pl.BlockSpec(memory_space=pltpu.SEMAPHORE)
_EFFECT = pltpu.SideEffectType.DATAFLOW_SIDE_EFFECTING
_KINDS = ['in', 'out', 'up', 'down']


def _hbm(a):
    return pltpu.with_memory_space_constraint(a, pltpu.HBM)


def _shard_shape(cf, kind):
    D, Fd = cf.D, cf.F
    return {'in': (D, cf.NINS), 'out': (D // N_DEV, D), 'up': (D, 2 * Fd // N_DEV), 'down': (Fd // N_DEV, D)}[kind]


def _whole_shape(cf, kind):
    D, Fd = cf.D, cf.F
    return {'in': (N_DEV, D, cf.NINS), 'out': (D, D), 'up': (D, 2 * Fd), 'down': (Fd, D)}[kind]


def _part(ref, cf, kind, idx):
    r, cdim = _shard_shape(cf, kind)
    if kind == 'in':
        return ref.at[idx]
    if kind == 'up':
        return ref.at[:, pl.ds(pl.multiple_of(idx * cdim, cdim), cdim)]
    return ref.at[pl.ds(pl.multiple_of(idx * r, r), r), :]


def _ag_start(cf, shards, after):
    npc = DEPTH * len(_KINDS)
    nio = len(_KINDS) + npc

    def body(*refs):
        srcs, lands = refs[:len(_KINDS)], refs[len(_KINDS):nio]
        ssems, rsems = refs[2 * nio + 1:2 * nio + 1 + npc], refs[2 * nio + 1 + npc:2 * nio + 1 + 2 * npc]
        token, lsem = refs[2 * nio + 1 + 2 * npc], refs[2 * nio + 2 + 2 * npc]
        x, y, c, me = _me()
        pieces = [(l * len(_KINDS) + n, srcs[n].at[l], kind) for l in range(DEPTH) for n, kind in enumerate(_KINDS)]
        locs = [pltpu.make_async_copy(src, _part(lands[p], cf, kind, me), lsem.at[p]) for p, src, kind in pieces]
        for cp in locs:
            cp.start()
        for cp in locs:
            cp.wait()
        for p, src, kind in pieces:
            for k in range(1, N_DEV):
                tgt, _ = _peer(x, y, c, k)
                _rcopy(src, _part(lands[p], cf, kind, me), ssems[p].at[k - 1], rsems[p].at[k - 1], tgt).start()
        token[...] = jnp.zeros_like(token)

    land_shapes = [_whole_shape(cf, kind) for _ in range(DEPTH) for kind in _KINDS]
    out_shape = [pltpu.HBM(s.shape, BF16) for s in shards] + [pltpu.HBM(s, BF16) for s in land_shapes]
    out_shape += [pltpu.SemaphoreType.DMA((N_DEV - 1,))] * (2 * npc) + [jax.ShapeDtypeStruct((8, LANE), F32)]
    res = pl.pallas_call(
        body, name="ag_start", in_specs=[_HBM] * nio + [pl.BlockSpec(memory_space=pl.ANY)],
        out_specs=[_HBM] * nio + [_SEM] * (2 * npc) + [pl.BlockSpec(memory_space=pltpu.VMEM)],
        out_shape=out_shape, input_output_aliases={i: i for i in range(nio)}, scratch_shapes=[pltpu.SemaphoreType.DMA((npc,))],
        compiler_params=pltpu.CompilerParams(has_side_effects=_EFFECT))(
            *[_hbm(s) for s in shards], *[_hbm(lax.empty(s, BF16)) for s in land_shapes], after)
    return res[:len(_KINDS)], res[len(_KINDS):nio], res[nio:nio + npc], res[nio + npc:nio + 2 * npc], res[nio + 2 * npc]


def _ag_wait(cf, kind, l, src, land, ssem, rsem, after):
    def body(src_ref, land_ref, ssem_ref, rsem_ref, after_ref, src_out, land_out):
        x, y, c, me = _me()
        for k in range(1, N_DEV):
            tgt, pi = _peer(x, y, c, k)
            cp = _rcopy(src_ref.at[l], _part(land_ref, cf, kind, pi), ssem_ref.at[k - 1], rsem_ref.at[k - 1], tgt)
            cp.wait_send()
            cp.wait_recv()

    return pl.pallas_call(
        body, name="ag_wait_%s_%d" % (kind, l), in_specs=[_HBM, _HBM, _SEM, _SEM, pl.BlockSpec(memory_space=pl.ANY)],
        out_specs=[_HBM, _HBM], out_shape=[pltpu.HBM(src.shape, src.dtype), pltpu.HBM(land.shape, land.dtype)],
        input_output_aliases={0: 0, 1: 1}, compiler_params=pltpu.CompilerParams(has_side_effects=_EFFECT))(src, land, ssem, rsem, after)


N_BARRIER_IDS = 8


def _handshake(x, y, c):
    barrier = pltpu.get_barrier_semaphore()
    for k in range(1, N_DEV):
        pl.semaphore_signal(barrier, inc=1, device_id=_peer(x, y, c, k)[0], device_id_type=pl.DeviceIdType.MESH)
    pl.semaphore_wait(barrier, N_DEV - 1)


def _seq_kernel(body, name, seq, out_type):
    return pl.kernel(
        body, out_type=out_type, mesh=plsc.ScalarSubcoreMesh(axis_name="sq", num_cores=1), name=name,
        scratch_types=[pltpu.SemaphoreType.DMA((N_DEV - 1,)), pltpu.SemaphoreType.DMA((N_DEV - 1,)), pltpu.SemaphoreType.DMA],
        compiler_params=pltpu.CompilerParams(collective_id=seq % N_BARRIER_IDS))


def _seq_gather(cf, kind, l, seq, src):
    def body(src_ref, land_ref, ssem, rsem, lsem):
        x, y, c, me = _me()
        _handshake(x, y, c)
        sib = (x, y, 1 - c)
        chips = [(1 - x, y), (x, 1 - y), (1 - x, 1 - y)]

        def blk(px, py, pc):
            return _part(land_ref, cf, kind, 4 * px + 2 * py + pc)

        src, mine = src_ref.at[l], blk(x, y, c)
        loc = pltpu.make_async_copy(src, mine, lsem)
        loc.start()
        loc.wait()
        first = [_rcopy(src, mine, ssem.at[0], rsem.at[0], sib)]
        first += [_rcopy(src, mine, ssem.at[1 + j], rsem.at[1 + j], (*chip, c)) for j, chip in enumerate(chips)]
        for cp in first:
            cp.start()
        passed = [_rcopy(blk(*chip, c), blk(*chip, c), ssem.at[4 + j], rsem.at[4 + j], sib) for j, chip in enumerate(chips)]
        for j, chip in enumerate(chips):
            _rcopy(src, blk(*chip, c), ssem.at[1 + j], rsem.at[1 + j], (*chip, c)).wait_recv()
            passed[j].start()
        _rcopy(src, blk(x, y, 1 - c), ssem.at[0], rsem.at[0], sib).wait_recv()
        for j, chip in enumerate(chips):
            _rcopy(src, blk(*chip, 1 - c), ssem.at[4 + j], rsem.at[4 + j], sib).wait_recv()
        for cp in first + passed:
            cp.wait_send()

    return _seq_kernel(body, "seq_gather_%s_%d" % (kind, l), seq, jax.ShapeDtypeStruct(_whole_shape(cf, kind), BF16))(src)


def _seq_scatter(cf, kind, l, seq, g):
    def body(g_ref, recv_ref, ssem, rsem, lsem):
        x, y, c, me = _me()
        _handshake(x, y, c)
        loc = pltpu.make_async_copy(_rs_slab(g_ref, cf, kind, me), recv_ref.at[me], lsem)
        loc.start()
        loc.wait()
        sends = []
        for k in range(1, N_DEV):
            tgt, pi = _peer(x, y, c, k)
            sends.append(_rcopy(_rs_slab(g_ref, cf, kind, pi), recv_ref.at[me], ssem.at[k - 1], rsem.at[k - 1], tgt))
            sends[-1].start()
        for k in range(1, N_DEV):
            tgt, pi = _peer(x, y, c, k)
            _rcopy(_rs_slab(g_ref, cf, kind, pi), recv_ref.at[pi], ssem.at[k - 1], rsem.at[k - 1], tgt).wait_recv()
        for cp in sends:
            cp.wait_send()

    return _seq_kernel(body, "seq_scatter_%s_%d" % (kind, l), seq, jax.ShapeDtypeStruct((N_DEV,) + _shard_shape(cf, kind), BF16))(g)


def _rs_slab(ref, cf, kind, j):
    return ref.at[j] if kind in ('in', 'up') else _part(ref, cf, kind, j)


def _scatter_issue(cf, kind, g_ref, recv_ref, ssem, rsem, lsem):
    x, y, c, me = _me()
    loc = pltpu.make_async_copy(_rs_slab(g_ref, cf, kind, me), recv_ref.at[me], lsem)
    loc.start()
    loc.wait()
    for k in range(1, N_DEV):
        tgt, pi = _peer(x, y, c, k)
        _rcopy(_rs_slab(g_ref, cf, kind, pi), recv_ref.at[me], ssem.at[k - 1], rsem.at[k - 1], tgt).start()


def _rs_start(cf, kind, l, g, after):
    def body(g_ref, recv_ref, after_ref, g_out, recv_out, ssem, rsem, token, lsem):
        _scatter_issue(cf, kind, g_ref, recv_ref, ssem, rsem, lsem)
        token[...] = jnp.zeros_like(token)

    rshape = (N_DEV,) + _shard_shape(cf, kind)
    sems = pltpu.SemaphoreType.DMA((N_DEV - 1,))
    return pl.pallas_call(
        body, name="rs_start_%s_%d" % (kind, l), in_specs=[_HBM, _HBM, pl.BlockSpec(memory_space=pl.ANY)],
        out_specs=[_HBM, _HBM, _SEM, _SEM, pl.BlockSpec(memory_space=pltpu.VMEM)],
        out_shape=[pltpu.HBM(g.shape, BF16), pltpu.HBM(rshape, BF16), sems, sems, jax.ShapeDtypeStruct((8, LANE), F32)],
        input_output_aliases={0: 0, 1: 1}, scratch_shapes=[pltpu.SemaphoreType.DMA],
        compiler_params=pltpu.CompilerParams(has_side_effects=_EFFECT))(_hbm(g), _hbm(lax.empty(rshape, BF16)), after)


def _rs_wait(cf, kind, l, g, recv, ssem, rsem, after):
    def body(g_ref, recv_ref, ssem_ref, rsem_ref, after_ref, g_out, recv_out):
        x, y, c, me = _me()
        for k in range(1, N_DEV):
            tgt, pi = _peer(x, y, c, k)
            cp = _rcopy(_rs_slab(g_ref, cf, kind, pi), recv_ref.at[pi], ssem_ref.at[k - 1], rsem_ref.at[k - 1], tgt)
            cp.wait_send()
            cp.wait_recv()

    return pl.pallas_call(
        body, name="rs_wait_%s_%d" % (kind, l), in_specs=[_HBM, _HBM, _SEM, _SEM, pl.BlockSpec(memory_space=pl.ANY)],
        out_specs=[_HBM, _HBM], out_shape=[pltpu.HBM(g.shape, g.dtype), pltpu.HBM(recv.shape, recv.dtype)],
        input_output_aliases={0: 0, 1: 1}, compiler_params=pltpu.CompilerParams(has_side_effects=_EFFECT))(g, recv, ssem, rsem, after)[1]


def _adam_vals(w, g, m, v):
    m2 = ADAM_B1 * m + (1.0 - ADAM_B1) * g
    v2 = ADAM_B2 * v + (1.0 - ADAM_B2) * (g * g)
    mh = m2 / (1.0 - ADAM_B1 ** ADAM_STEP)
    vh = v2 / (1.0 - ADAM_B2 ** ADAM_STEP)
    return -ADAM_LR * (mh / (jnp.sqrt(vh) + ADAM_EPS) + ADAM_WD * w), m2, v2


def _row_tile(R, Cc, budget_elems):
    t = max(16, min(R, (budget_elems // max(Cc, 1)) // 16 * 16))
    while t > 16 and R % t:
        t -= 16
    return t if R % t == 0 else R


def _cast_bf16(name, w, after):
    Dp, R, Cc = w.shape
    tr = _row_tile(R, Cc, 1 << 20)

    def body(w_ref, after_ref, o_ref):
        o_ref[...] = w_ref[...].astype(BF16)

    spec = pl.BlockSpec((None, tr, Cc), lambda l, i: (l, i, 0))
    return pl.pallas_call(body, name=name, grid=(Dp, R // tr), in_specs=[spec, pl.BlockSpec(memory_space=pl.ANY)], out_specs=spec,
                          out_shape=jax.ShapeDtypeStruct(w.shape, BF16), compiler_params=_cp(("parallel", "parallel")))(w, after)


def _sum_adam(name, recvs, w, m, v):
    Dp, R, Cc = w.shape
    tr = _row_tile(R, Cc, 1 << 18)
    ni = R // tr

    def body(*refs):
        r_refs = refs[:Dp]
        w_ref, m_ref, v_ref, g_ref, d_ref, mo_ref, vo_ref = refs[Dp:]
        for layer in range(Dp):
            @pl.when(pl.program_id(0) == layer)
            def _():
                r_ref = r_refs[layer]
                g = r_ref[0].astype(F32)
                for s in range(1, N_DEV):
                    g = g + r_ref[s].astype(F32)
                dl, m2, v2 = _adam_vals(w_ref[...], g, m_ref[...], v_ref[...])
                g_ref[...] = g
                d_ref[...] = dl
                mo_ref[...] = m2
                vo_ref[...] = v2

    spec = pl.BlockSpec((None, tr, Cc), lambda l, i: (l, i, 0))
    rspecs = [pl.BlockSpec((N_DEV, tr, Cc), functools.partial(
        lambda l, i, layer: (0, jnp.where(l == layer, i, jnp.where(l > layer, ni - 1, 0)), 0), layer=layer)) for layer in range(Dp)]
    return pl.pallas_call(body, name=name, grid=(Dp, ni), in_specs=rspecs + [spec, spec, spec], out_specs=[spec] * 4,
                          out_shape=[jax.ShapeDtypeStruct(w.shape, F32)] * 4, compiler_params=_cp(("arbitrary", "arbitrary")))(*recvs, w, m, v)


def _adam(name, w, g, m, v):
    R, Cc = w.shape
    tr = _row_tile(R, Cc, 1 << 18)

    def body(w_ref, g_ref, m_ref, v_ref, d_ref, mo_ref, vo_ref):
        dl, m2, v2 = _adam_vals(w_ref[...], g_ref[...], m_ref[...], v_ref[...])
        d_ref[...] = dl
        mo_ref[...] = m2
        vo_ref[...] = v2

    spec = pl.BlockSpec((tr, Cc), lambda i: (i, 0))
    return pl.pallas_call(body, name=name, grid=(R // tr,), in_specs=[spec] * 4, out_specs=[spec] * 3,
                          out_shape=[jax.ShapeDtypeStruct(w.shape, F32)] * 3, compiler_params=_cp(("parallel",)))(w, g, m, v)


def _sum8(name, a):
    n, R, Cc = a.shape

    def body(a_ref, o_ref):
        s = a_ref[0]
        for k in range(1, n):
            s = s + a_ref[k]
        o_ref[...] = s

    return pl.pallas_call(body, name=name, in_specs=[pl.BlockSpec(memory_space=pltpu.VMEM)],
                          out_specs=pl.BlockSpec(memory_space=pltpu.VMEM), out_shape=jax.ShapeDtypeStruct((R, Cc), F32),
                          compiler_params=pltpu.CompilerParams(vmem_limit_bytes=VMEM_LIMIT))(a)


def _ada_fwd(cf, c9, ada_w):
    D = cf.D
    NS = ada_w.shape[-1]
    tk = _tile(D, 512)
    nk = D // tk

    def body(c_ref, w_ref, o_ref):
        kk = pl.program_id(1)
        s = _silu(c_ref[...]).astype(BF16)
        part = jnp.dot(s, w_ref[...].astype(BF16), preferred_element_type=F32)

        @pl.when(kk == 0)
        def _():
            o_ref[...] = part

        @pl.when(kk != 0)
        def _():
            o_ref[...] += part

    return pl.pallas_call(
        body, name="ada_fwd", grid=(DEPTH, nk),
        in_specs=[pl.BlockSpec((16, tk), lambda l, k: (0, k)), pl.BlockSpec((None, tk, NS), lambda l, k: (l, k, 0))],
        out_specs=pl.BlockSpec((None, 16, NS), lambda l, k: (l, 0, 0)),
        out_shape=jax.ShapeDtypeStruct((DEPTH, 16, NS), F32), compiler_params=_cp(("parallel", "arbitrary")))(c9, ada_w)


def _ada_bwd(cf, c9, ada_w, dm9):
    D = cf.D
    NS = ada_w.shape[-1]
    tk = _tile(D, 512)
    nk = D // tk

    def body(c_ref, w_ref, dm_ref, gw_ref, ds_ref):
        cv = c_ref[...]
        sg = _sigmoid(cv)
        dmb = dm_ref[...].astype(BF16)
        gw_ref[...] = lax.dot_general((cv * sg).astype(BF16), dmb, (_DN['tn'], ((), ())), preferred_element_type=F32)
        ds = lax.dot_general(dmb, w_ref[...].astype(BF16), (_DN['nt'], ((), ())), preferred_element_type=F32)
        ds_ref[...] = ds * (sg * (1.0 + cv * (1.0 - sg)))

    return pl.pallas_call(
        body, name="ada_bwd", grid=(DEPTH, nk),
        in_specs=[pl.BlockSpec((16, tk), lambda l, k: (0, k)), pl.BlockSpec((None, tk, NS), lambda l, k: (l, k, 0)),
                  pl.BlockSpec((None, 16, NS), lambda l, k: (l, 0, 0))],
        out_specs=[pl.BlockSpec((None, tk, NS), lambda l, k: (l, k, 0)), pl.BlockSpec((None, 16, tk), lambda l, k: (l, 0, k))],
        out_shape=[jax.ShapeDtypeStruct((DEPTH, D, NS), F32), jax.ShapeDtypeStruct((DEPTH, 16, D), F32)],
        compiler_params=_cp(("parallel", "parallel")))(c9, ada_w, dm9)


def _rope_tables(cf):
    L, LC = cf.L, cf.LC
    rows = L // GRID_W
    row = jnp.repeat(jnp.arange(rows, dtype=F32), GRID_W)
    col = jnp.tile(jnp.arange(GRID_W, dtype=F32), rows)
    nf = HEAD_DIM // 4
    inv = ROPE_THETA ** (-jnp.arange(nf, dtype=F32) / nf)
    ang = jnp.concatenate([row[:, None] * inv, col[:, None] * inv], axis=-1)
    cos, sin = jnp.cos(ang), jnp.sin(ang)
    cs = jnp.concatenate([jnp.ones((LC, HEAD_DIM), F32), jnp.concatenate([cos, cos], -1)], 0)
    sn = jnp.concatenate([jnp.zeros((LC, HEAD_DIM), F32), jnp.concatenate([-sin, sin], -1)], 0)
    return cs, sn


def _prep_tiles(cf, z, cs, sn, key):
    b = cf.OFF[key] // LANE
    return [(z, LANE, _col(b), True), (cs, LANE, _c0, False), (sn, LANE, _c0, False)]


_PREP = {'aq': _f_prep_norm, 'ak': _f_prep_norm, 'rq': _f_prep_plain, 'rk': _f_prep_scaled}


def _prep_fwd(cf, z, cs, sn, key, g):
    nh = cf.W[key] // LANE
    params = [(g, 'shared', True)] if g is not None else []
    return _row_fwd(cf, "prep_fwd_" + key, _PREP[key], _prep_tiles(cf, z, cs, sn, key), params,
                    [(LANE, _col(0), cf.W[key], BF16)], cf.TQ, nrep=nh)[0]


def _prep_bwd(cf, z, cs, sn, key, g, dt):
    nh = cf.W[key] // LANE
    params = [(g, 'shared', True)] if g is not None else []
    tg, pg = _row_bwd(cf, "prep_bwd_" + key, _PREP[key], _prep_tiles(cf, z, cs, sn, key), params,
                      [(dt, LANE, _col(0))], [(LANE, _col(0), cf.W[key], BF16)], cf.TQ, nrep=nh)
    return tg[0], (pg[0] if g is not None else None)


def _gate_params(cf, gup, gb):
    K = gup.shape[-1]
    gf = jnp.zeros((LANE, K), F32).at[0:GLA_RANK].set(gup[0])
    gbm = jnp.zeros((LANE, K), F32).at[GLA_RANK:2 * GLA_RANK].set(gup[1])
    out = []
    for p in range(K // LANE):
        cols = slice(p * LANE, (p + 1) * LANE)
        out += [(gf[:, cols], 'shared', True), (gbm[:, cols], 'shared', True), (gb[0:1, cols], 'shared', True), (gb[1:2, cols], 'shared', True)]
    return out


def _mix_tiles(cf, z, o, key):
    return [(o, LANE, _col(0), True), (z, LANE, _col(cf.OFF[key] // LANE), True)]


def _mid_io(cf, l, W, mod, x, y, zero=0.0):
    tiles = [(x, cf.D, _c0, True), (y, cf.D, _c0, True)]
    params = [(mod[2], 'stream', True), (W['norm2_g'][l] + zero, 'shared', True), (mod[3], 'stream', True), (mod[4], 'stream', True)]
    return tiles, params


class _BigWeights:
    def __init__(self, cf, shards):
        self.cf = cf
        self.whole = {(kind, l): _seq_gather(cf, kind, l, l * len(_KINDS) + n, shards[n])
                      for l in range(DEPTH) for n, kind in enumerate(_KINDS)}
        self.w_in = {}

    def get(self, kind, l, after=None):
        cf = self.cf
        if kind != 'in':
            return self.whole[(kind, l)]
        if l not in self.w_in:
            whole, _ = lax.optimization_barrier((self.whole[(kind, l)], after))
            self.w_in[l] = jnp.pad(_unshard_last(whole), ((0, 0), (0, cf.NZ - cf.NIN)))
        return self.w_in[l]


def _layer_fwd(cf, l, W, big, mod, x, h, cs, sn):
    T, D, Fd = cf.T, cf.D, cf.F
    z = _mm("z_%d" % l, h, big.get('in', l, h), 'nn', T, cf.NZ, D, F32, tm=T, tn=768, tk=512)
    qa = _prep_fwd(cf, z, cs, sn, 'aq', W['q_norm_g'][l])
    ka = _prep_fwd(cf, z, cs, sn, 'ak', W['k_norm_g'][l])
    qr = _prep_fwd(cf, z, cs, sn, 'rq', None)
    kr = _prep_fwd(cf, z, cs, sn, 'rk', None)
    o_att = _att_fwd(cf, qa, ka, z)
    o_ret = _ret_fwd(cf, qr, kr, z, W['ret_log_decay'][l])
    gates = _gate_params(cf, W['gla_gate_up'][l], W['gla_gate_b'][l])
    ga_tile = [(z, LANE, _col(cf.OFF['ga'] // LANE), True)]
    we = (cf.HG // 2) * N_DECAY * LANE
    ef, eb = _row_fwd(cf, "gates_fwd_%d" % l, _f_gla_pre, ga_tile, gates, [(we, _c0, we, F32), (we, _c0, we, F32)], GLA_CHUNK)
    o_f, sf, o_b, sb = _gla_fwd(cf, z, ef, eb)
    o_gla = o_f + o_b
    cat_r = _row_fwd(cf, "mixr_fwd_%d" % l, _f_gated_norm, _mix_tiles(cf, z, o_ret, 'rg'), [(W['ret_norm_g'][l], 'shared', True)],
                     [(LANE, _col(0), cf.HR * LANE, BF16)], cf.TQ, nrep=cf.HR)[0]
    cat_g = _row_fwd(cf, "mixg_fwd_%d" % l, _f_gated_norm, _mix_tiles(cf, z, o_gla, 'gr'), [(W['gla_norm_g'][l], 'shared', True)],
                     [(LANE, _col(0), cf.HG * LANE, BF16)], cf.TQ, nrep=cf.HG)[0]
    cat = jnp.concatenate([o_att, cat_r, cat_g], axis=-1)
    y = _mm("y_%d" % l, cat, big.get('out', l, cat), 'nn', T, D, D, F32, tm=T, tn=1024, tk=512)
    tiles, params = _mid_io(cf, l, W, mod, x, y)
    x1, h2 = _row_fwd(cf, "mid_fwd_%d" % l, _f_resid_norm_mod, tiles, params, [(D, _c0, D, F32), (D, _c0, D, BF16)], cf.TM)
    u = _mm("u_%d" % l, h2, big.get('up', l, h2), 'nn', T, 2 * Fd, D, F32, tm=T, tn=1024, tk=512)
    t = _conv_fwd(cf, u, W['conv_wb'][l])
    yff = _mm("yff_%d" % l, t, big.get('down', l, t), 'nn', T, D, Fd, F32, tm=T, tn=1024, tk=512)
    return dict(x=x, h=h, z=z, qa=qa, ka=ka, qr=qr, kr=kr, ef=ef, eb=eb, sf=sf, sb=sb, o_ret=o_ret, o_gla=o_gla, cat=cat, y=y,
                x1=x1, h2=h2, u=u, t=t, yff=yff, gates=gates)


def _layer_bwd(cf, l, W, big, mod, sv, dx1, dyff, cs, sn):
    T, D, Fd = cf.T, cf.D, cf.F
    g, rs = {}, {}
    sq = 2 * len(_KINDS) + (DEPTH - 1 - l) * len(_KINDS)
    rs['down'] = _rs_start(cf, 'down', l, _mm("gwd_%d" % l, sv['t'], dyff, 'tn', Fd, D, T, BF16, tm=1408, tn=2048, tk=1152), cs)
    dt = _mm("dt_%d" % l, dyff, big.get('down', l), 'nt', T, Fd, D, BF16, tm=T, tn=1408, tk=512)
    da, dv, g['conv_wb'] = _conv_bwd(cf, sv['u'], W['conv_wb'][l] + rs['down'][4][0, 0], dt)
    du = jnp.concatenate([da, dv], axis=-1)
    cu = 2 * Fd // N_DEV
    gwu = _mm("gwu_%d" % l, sv['h2'], du, 'tn', D, 2 * Fd, T, BF16, tm=D, tn=cu, tk=1152, out_shape=(N_DEV, D, cu),
              out_spec=pl.BlockSpec((None, D, cu), lambda i, j, k: (j, i, 0)))
    rs['up'] = _rs_start(cf, 'up', l, gwu, cs)
    dh2 = _mm("dh2_%d" % l, du, big.get('up', l), 'nt', T, D, 2 * Fd, BF16, tm=T, tn=1024, tk=1024)
    tiles, params = _mid_io(cf, l, W, mod, sv['x'], sv['y'], rs['up'][4][0, 0])
    (dx, dy), (g['m2'], g['norm2_g'], g['m3'], g['m4']) = _row_bwd(
        cf, "mid_bwd_%d" % l, _f_resid_norm_mod, tiles, params, [(dx1, D, _c0), (dh2, D, _c0)],
        [(D, _c0, D, F32), (D, _c0, D, BF16)], cf.TM)
    rs['out'] = _rs_start(cf, 'out', l, _mm("gwo_%d" % l, sv['cat'], dy, 'tn', D, D, T, BF16, tm=D, tn=1024, tk=1152), cs)
    dcat = _mm("dcat_%d" % l, dy, big.get('out', l), 'nt', T, D, D, BF16, tm=T, tn=1024, tk=1024)
    z = sv['z']
    (do_ret, drg), (g['ret_norm_g'],) = _row_bwd(
        cf, "mixr_bwd_%d" % l, _f_gated_norm, _mix_tiles(cf, z, sv['o_ret'], 'rg'), [(W['ret_norm_g'][l] + rs['out'][4][0, 0], 'shared', True)],
        [(dcat, LANE, _col(cf.HQ))], [(LANE, _col(0), cf.HR * LANE, F32), (LANE, _col(0), cf.HR * LANE, BF16)], cf.TQ, nrep=cf.HR)
    (do_gla, dgr), (g['gla_norm_g'],) = _row_bwd(
        cf, "mixg_bwd_%d" % l, _f_gated_norm, _mix_tiles(cf, z, sv['o_gla'], 'gr'), [(W['gla_norm_g'][l], 'shared', True)],
        [(dcat, LANE, _col(cf.HQ + cf.HR))], [(LANE, _col(0), cf.HG * LANE, F32), (LANE, _col(0), cf.HG * LANE, BF16)], cf.TQ, nrep=cf.HG)
    dqa, dka, dav = _att_bwd(cf, sv['qa'], sv['ka'], z, dcat)
    dqr, dkr, drv, dlg = _ret_bwd(cf, sv['qr'], sv['kr'], z, W['ret_log_decay'][l], do_ret)
    g['ret_log_decay'] = dlg[:, 0:2, 0].T
    dq_f, dk_f, dv_f, def_, dq_b, dk_b, dv_b, deb = _gla_bwd(cf, z, sv['ef'], sv['eb'], sv['sf'], sv['sb'], do_gla)
    dgq, dgk, dgv = dq_f + dq_b, dk_f + dk_b, dv_f + dv_b
    we = (cf.HG // 2) * N_DECAY * LANE
    ga_tile = [(z, LANE, _col(cf.OFF['ga'] // LANE), True)]
    (dga,), gg = _row_bwd(cf, "gates_bwd_%d" % l, _f_gla_pre, ga_tile, sv['gates'],
                          [(def_, we, _c0), (deb, we, _c0)], [(LANE, _c0, LANE, BF16)], GLA_CHUNK)
    ggf, ggb, gbf, gbb = [jnp.concatenate(gg[n::4], axis=-1) for n in range(4)]
    g['gla_gate_up'] = jnp.stack([ggf[0:GLA_RANK], ggb[GLA_RANK:2 * GLA_RANK]])
    g['gla_gate_b'] = jnp.concatenate([gbf, gbb], axis=0)
    daq, g['q_norm_g'] = _prep_bwd(cf, z, cs, sn, 'aq', W['q_norm_g'][l], dqa)
    dak, g['k_norm_g'] = _prep_bwd(cf, z, cs, sn, 'ak', W['k_norm_g'][l], dka)
    drq, _ = _prep_bwd(cf, z, cs, sn, 'rq', None, dqr)
    drk, _ = _prep_bwd(cf, z, cs, sn, 'rk', None, dkr)
    pad = jnp.zeros((T, cf.NZ - cf.OFF['ga'] - LANE), BF16)
    dz = jnp.concatenate([daq, dak, dav.astype(BF16), drq, drk, drv.astype(BF16), drg, dgq.astype(BF16), dgk.astype(BF16),
                          dgv.astype(BF16), dgr, dga, pad], axis=-1)
    gwi = _mm("gwi_%d" % l, sv['h'], dz, 'tn', D, cf.NZ, T, BF16, tm=D, tn=768, tk=1152)
    g['w_in_slabs'] = jnp.moveaxis(gwi[:, :cf.NIN].reshape(D, N_DEV, cf.NINS), 1, 0)
    g['zero'] = 0.0
    if l > 0:
        rs['in'] = _rs_start(cf, 'in', l, g['w_in_slabs'], cs)
        g['zero'] = rs['in'][4][0, 0]
    dh = _mm("dh_%d" % l, dz, big.get('in', l), 'nt', T, D, cf.NZ, BF16, tm=T, tn=1024, tk=768)
    g['rs'] = rs
    return dx, dh, g


_WEIGHTS = ['c_ctx', 'ada_w', 'ada_b', 'norm1_g', 'w_in', 'q_norm_g', 'k_norm_g', 'ret_log_decay', 'ret_norm_g',
            'gla_gate_up', 'gla_gate_b', 'gla_norm_g', 'w_out', 'norm2_g', 'w_up', 'conv_w', 'conv_b', 'w_down', 'final_norm_g']
_BIG = ['w_in', 'w_out', 'w_up', 'w_down']
_SMALL = [n for n in _WEIGHTS if n not in _BIG and n != 'ada_w']
_COL_SHARDED = ['gla_gate_up', 'gla_gate_b', 'conv_w']


def _pack(arrs):
    flat = jnp.concatenate([a.reshape(-1) for a in arrs])
    n = flat.shape[0]
    tot = -(-n // (8 * LANE)) * (8 * LANE)
    return jnp.pad(flat, (0, tot - n)).reshape(tot // LANE, LANE)


def _unpack(flat, shapes):
    out, o = [], 0
    for s in shapes:
        n = int(np.prod(s))
        out.append(flat[..., o:o + n].reshape(flat.shape[:-1] + tuple(s)))
        o += n
    return out


def _unshard_last(a):
    return jnp.moveaxis(a, 0, -2).reshape(a.shape[1:-1] + (N_DEV * a.shape[-1],))


def _step(cf, x, c, ctx, loss_target, w, m, v):
    T, D, Fd, L, LC = cf.T, cf.D, cf.F, cf.L, cf.LC
    _, _, _, me = _me()
    NS = w['ada_w'].shape[-1]

    c_all = _ag_small("ag_c", jnp.pad(c, ((0, 7), (0, 0))))[:, 0, :]
    c9 = jnp.concatenate([c_all, w['c_ctx'][None], jnp.zeros((7, D), F32)], axis=0)
    pm = _ada_fwd(cf, c9, w['ada_w'])
    pm_all = _ag_small("ag_mod", pm.reshape(DEPTH * 16, NS)).reshape(N_DEV, DEPTH, 16, NS)
    mod_all = _unshard_last(pm_all) + w['ada_b'][:, None, :]
    mod_own = lax.dynamic_index_in_dim(mod_all, me, axis=1, keepdims=False)
    mods = []
    for l in range(DEPTH):
        mods.append([jnp.stack([mod_all[l, 8, k * D:(k + 1) * D], mod_own[l, k * D:(k + 1) * D]])[:, None, :] for k in range(N_MOD)])

    shard_shapes = [w[n].shape for n in _COL_SHARDED]
    got = _ag_small("ag_smallw", _pack([w[n] for n in _COL_SHARDED]))
    full = dict(zip(_COL_SHARDED, [_unshard_last(a) for a in _unpack(got.reshape(N_DEV, -1), shard_shapes)]))

    small_done = full['conv_w'] + mod_all[0, 0, 0]
    big = _BigWeights(cf, [_cast_bf16("cast_" + n, w[n], c if n == 'w_in' else small_done) for n in _BIG])
    conv_wb = jnp.concatenate([full['conv_w'], w['conv_b'][:, None, :], jnp.zeros((DEPTH, 4, Fd), F32)], axis=1)
    W = dict(conv_wb=conv_wb, gla_gate_up=full['gla_gate_up'], gla_gate_b=full['gla_gate_b'], ret_log_decay=w['ret_log_decay'])
    for n in ['q_norm_g', 'k_norm_g', 'ret_norm_g', 'gla_norm_g', 'norm1_g', 'norm2_g']:
        W[n] = w[n][:, None, :]

    cs, sn = _rope_tables(cf)
    x0 = jnp.concatenate([ctx[0], x[0]], axis=0)
    pre_tiles = [(x0, D, _c0, True)]

    def pre_params(zero):
        return [(W['norm1_g'][0] + zero, 'shared', True), (mods[0][0], 'stream', True), (mods[0][1], 'stream', True)]

    def tr_params(zero):
        return [(mods[0][5], 'stream', True), (W['norm1_g'][1] + zero, 'shared', True), (mods[1][0], 'stream', True), (mods[1][1], 'stream', True)]

    h0 = _row_fwd(cf, "pre_fwd", _f_norm_mod, pre_tiles, pre_params(0.0), [(D, _c0, D, BF16)], cf.TM)[0]
    sv0 = _layer_fwd(cf, 0, W, big, mods[0], x0, h0, cs, sn)
    tr_tiles = [(sv0['x1'], D, _c0, True), (sv0['yff'], D, _c0, True)]
    xb, hb = _row_fwd(cf, "tr_fwd", _f_resid_norm_mod, tr_tiles, tr_params(0.0), [(D, _c0, D, F32), (D, _c0, D, BF16)], cf.TM)
    sv1 = _layer_fwd(cf, 1, W, big, mods[1], xb, hb, cs, sn)
    tgt = jnp.concatenate([jnp.zeros((LC, D), F32), loss_target[0]], axis=0)
    dx1, dyff, dm5_1, g_final, ls = _loss_grad(cf, sv1['x1'], sv1['yff'], mods[1][5], w['final_norm_g'][None], tgt)
    loss = lax.psum(ls[0, 0], ("x", "y", "c"))

    dxb, dhb, g1 = _layer_bwd(cf, 1, W, big, mods[1], sv1, dx1, dyff, cs, sn)
    (dx1_0, dyff_0), (dm5_0, gn1_1, dm0_1, dm1_1) = _row_bwd(
        cf, "tr_bwd", _f_resid_norm_mod, tr_tiles, tr_params(g1['zero']), [(dxb, D, _c0), (dhb, D, _c0)],
        [(D, _c0, D, F32), (D, _c0, D, BF16)], cf.TM)
    dx0, dh0, g0 = _layer_bwd(cf, 0, W, big, mods[0], sv0, dx1_0, dyff_0, cs, sn)
    (dxa,), (gn1_0, dm0_0, dm1_0) = _row_bwd(cf, "pre_bwd", _f_first, pre_tiles, pre_params(0.0), [(dx0, D, _c0), (dh0, D, _c0)],
                                            [(D, _c0, D, F32)], cf.TM)
    grad_x = dxa[LC:][None]

    dmod = jnp.stack([jnp.concatenate([dm0_0, dm1_0, g0['m2'], g0['m3'], g0['m4'], dm5_0], axis=-1)[:, 0],
                      jnp.concatenate([dm0_1, dm1_1, g1['m2'], g1['m3'], g1['m4'], dm5_1], axis=-1)[:, 0]])
    dm_all = _ag_small("ag_dmod", jnp.pad(dmod.reshape(2 * DEPTH, N_MOD * D), ((0, 8 - 2 * DEPTH), (0, 0))))
    dm_all = dm_all[:, :2 * DEPTH].reshape(N_DEV, DEPTH, 2, N_MOD * D)
    dctx = _sum8("sum_dmodc", jnp.pad(dm_all[:, :, 0], ((0, 0), (0, 8 - DEPTH), (0, 0))))[:DEPTH]
    dm9 = jnp.concatenate([jnp.moveaxis(dm_all[:, :, 1], 0, 1), dctx[:, None]], axis=1)
    g_ada_b = _sum8("sum_adab", jnp.pad(jnp.moveaxis(dm9, 1, 0), ((0, 0), (0, 8 - DEPTH), (0, 0))))[:DEPTH]
    dm9s = lax.dynamic_slice_in_dim(jnp.pad(dm9, ((0, 0), (0, 7), (0, 0))), me * NS, NS, axis=2)
    g_ada_w, dsil = _ada_bwd(cf, c9, w['ada_w'], dm9s)
    g_cctx_part = dsil[0, 8]
    for l in range(1, DEPTH):
        g_cctx_part = g_cctx_part + dsil[l, 8]

    def both(key):
        return jnp.stack([g0[key], g1[key]])

    gsmall = dict(c_ctx=g_cctx_part, norm1_g=jnp.stack([gn1_0[0], gn1_1[0]]), q_norm_g=both('q_norm_g')[:, 0],
                  k_norm_g=both('k_norm_g')[:, 0], ret_log_decay=both('ret_log_decay'), ret_norm_g=both('ret_norm_g')[:, 0],
                  gla_gate_up=both('gla_gate_up'), gla_gate_b=both('gla_gate_b'), gla_norm_g=both('gla_norm_g')[:, 0],
                  norm2_g=both('norm2_g')[:, 0], conv_w=both('conv_wb')[:, 0:3], conv_b=both('conv_wb')[:, 3], final_norm_g=g_final[0])
    snames = [n for n in _SMALL if n != 'ada_b']
    sshapes = [gsmall[n].shape for n in snames]
    gs_all = _ag_small("ag_gsmall", _pack([gsmall[n] for n in snames]))
    gs = dict(zip(snames, _unpack(_sum8("sum_gsmall", gs_all).reshape(-1), sshapes)))
    gs['ada_b'] = g_ada_b
    for n in _COL_SHARDED:
        ns_ = w[n].shape[-1]
        gs[n] = lax.dynamic_slice_in_dim(gs[n], me * ns_, ns_, axis=gs[n].ndim - 1)

    g0['rs']['in'] = _rs_start(cf, 'in', 0, g0['w_in_slabs'], gs_all)
    out_g, out_d, out_m, out_v = {}, {}, {}, {}

    def update_big(kind, after):
        n = 'w_' + kind
        recvs = [_rs_wait(cf, kind, l, *gl['rs'][kind][:4], after) for l, gl in enumerate((g0, g1))]
        out_g[n], out_d[n], out_m[n], out_v[n] = _sum_adam("adam_" + n, recvs, w[n], m[n], v[n])
        return out_g[n]

    after = g0['rs']['in'][4]
    for kind in ['down', 'up', 'out']:
        after = update_big(kind, after)
    aw = [a.reshape(DEPTH * D, NS) for a in (w['ada_w'], g_ada_w, m['ada_w'], v['ada_w'])]
    out_g['ada_w'] = g_ada_w
    out_d['ada_w'], out_m['ada_w'], out_v['ada_w'] = [a.reshape(DEPTH, D, NS) for a in _adam("adam_ada_w", *aw)]
    shp = [w[n].shape for n in _SMALL]
    packed = [_pack([src[n] for n in _SMALL]) for src in (w, gs, m, v)]
    res = _adam("adam_small", *packed)
    for dst, pk in zip((out_d, out_m, out_v), res):
        dst.update(zip(_SMALL, _unpack(pk.reshape(-1), shp)))
    out_g.update({n: gs[n] for n in _SMALL})
    update_big('in', after)
    return (loss, grad_x, *[out_g[n] for n in _WEIGHTS], *[out_d[n] for n in _WEIGHTS], *[out_m[n] for n in _WEIGHTS],
            *[out_v[n] for n in _WEIGHTS])


def kernel(x, c, ctx, c_ctx, ada_w, ada_b, norm1_g, w_in, q_norm_g, k_norm_g, ret_log_decay, ret_norm_g, gla_gate_up, gla_gate_b, gla_norm_g, w_out, norm2_g, w_up, conv_w, conv_b, w_down, final_norm_g, loss_target, m_c_ctx, m_ada_w, m_ada_b, m_norm1_g, m_w_in, m_q_norm_g, m_k_norm_g, m_ret_log_decay, m_ret_norm_g, m_gla_gate_up, m_gla_gate_b, m_gla_norm_g, m_w_out, m_norm2_g, m_w_up, m_conv_w, m_conv_b, m_w_down, m_final_norm_g, v_c_ctx, v_ada_w, v_ada_b, v_norm1_g, v_w_in, v_q_norm_g, v_k_norm_g, v_ret_log_decay, v_ret_norm_g, v_gla_gate_up, v_gla_gate_b, v_gla_norm_g, v_w_out, v_norm2_g, v_w_up, v_conv_w, v_conv_b, v_w_down, v_final_norm_g):
    w = dict(c_ctx=c_ctx, ada_w=ada_w, ada_b=ada_b, norm1_g=norm1_g, w_in=w_in, q_norm_g=q_norm_g, k_norm_g=k_norm_g,
             ret_log_decay=ret_log_decay, ret_norm_g=ret_norm_g, gla_gate_up=gla_gate_up, gla_gate_b=gla_gate_b,
             gla_norm_g=gla_norm_g, w_out=w_out, norm2_g=norm2_g, w_up=w_up, conv_w=conv_w, conv_b=conv_b, w_down=w_down,
             final_norm_g=final_norm_g)
    m = dict(c_ctx=m_c_ctx, ada_w=m_ada_w, ada_b=m_ada_b, norm1_g=m_norm1_g, w_in=m_w_in, q_norm_g=m_q_norm_g,
             k_norm_g=m_k_norm_g, ret_log_decay=m_ret_log_decay, ret_norm_g=m_ret_norm_g, gla_gate_up=m_gla_gate_up,
             gla_gate_b=m_gla_gate_b, gla_norm_g=m_gla_norm_g, w_out=m_w_out, norm2_g=m_norm2_g, w_up=m_w_up,
             conv_w=m_conv_w, conv_b=m_conv_b, w_down=m_w_down, final_norm_g=m_final_norm_g)
    v = dict(c_ctx=v_c_ctx, ada_w=v_ada_w, ada_b=v_ada_b, norm1_g=v_norm1_g, w_in=v_w_in, q_norm_g=v_q_norm_g,
             k_norm_g=v_k_norm_g, ret_log_decay=v_ret_log_decay, ret_norm_g=v_ret_norm_g, gla_gate_up=v_gla_gate_up,
             gla_gate_b=v_gla_gate_b, gla_norm_g=v_gla_norm_g, w_out=v_w_out, norm2_g=v_norm2_g, w_up=v_w_up,
             conv_w=v_conv_w, conv_b=v_conv_b, w_down=v_w_down, final_norm_g=v_final_norm_g)
    return _step(_cfg(), x, c, ctx, loss_target, w, m, v)
```

```python
import functools
import math
import types

import jax
import jax.numpy as jnp
import numpy as np
from jax import lax
from jax.experimental import pallas as pl
from jax.experimental.pallas import tpu as pltpu
from jax.experimental.pallas import tpu_sc as plsc

F32 = jnp.float32
BF16 = jnp.bfloat16
HI = lax.Precision.HIGHEST

D_MODEL = 2048
SEQ = 2048
CTX_LEN = 256
GRID_W = 64
D_FF = 5632
DEPTH = 2
N_DEV = 8
HEAD_DIM = 128
ROPE_THETA = 10000.0
GLA_TAU = 16.0
GLA_RANK = 16
GLA_CHUNK = 64
GLA_SUB = 16
EPS = 1e-6
N_MOD = 6
ADAM_LR = 0.001
ADAM_B1 = 0.9
ADAM_B2 = 0.999
ADAM_EPS = 1e-08
ADAM_WD = 0.01
ADAM_STEP = 10
LANE = 128
VMEM_LIMIT = 56 * 1024 * 1024
NEG = -1e30


def _cfg():
    d = types.SimpleNamespace()
    d.D, d.L, d.LC, d.F = D_MODEL, SEQ, CTX_LEN, D_FF
    d.T = d.L + d.LC
    nm = d.D // HEAD_DIM
    d.HQ, d.HKV, d.HR, d.HG = nm // 2, nm // 8, nm // 4, nm // 4
    d.G = d.HQ // d.HKV
    w = dict(aq=d.HQ * 128, ak=d.HKV * 128, av=d.HKV * 128, rq=d.HR * 128, rk=d.HR * 128, rv=d.HR * 128,
             rg=d.HR * 128, gq=d.HG * 64, gk=d.HG * 64, gv=d.HG * 128, gr=d.HG * 128, ga=2 * GLA_RANK)
    off, o = {}, 0
    for k, v in w.items():
        off[k] = o
        o += v
    d.W, d.OFF, d.NIN = w, off, o
    d.NZ = -(-(off['ga'] + LANE) // 256) * 256
    d.NINS = d.NIN // N_DEV
    d.TM = math.gcd(d.LC, 128)
    d.TQ = math.gcd(d.LC, 256)
    return d


def _cp(sem=None):
    return pltpu.CompilerParams(dimension_semantics=sem, vmem_limit_bytes=VMEM_LIMIT)


def _tile(n, target, mult=LANE):
    t = min(n, target)
    t -= t % mult
    while t > mult and n % t:
        t -= mult
    return t if t > 0 and n % t == 0 else n


_DN = {'nn': ((1,), (0,)), 'nt': ((1,), (1,)), 'tn': ((0,), (0,))}


def _mm(name, a, b, kind, M, N, K, out_dtype, tm=768, tn=768, tk=1024, a_spec=None, b_spec=None,
        out_shape=None, out_spec=None, scatter=None):
    tm, tn = _tile(M, tm, 128), _tile(N, tn, 128)
    tk = _tile(K, tk, 128)
    nk = K // tk

    def body(a_ref, b_ref, *rest):
        if scatter is None:
            o_ref, acc = rest
        else:
            g_ref, recv_ref, o_ref, _, _, ssem, rsem, acc, lsem = rest
            first = jnp.logical_and(jnp.logical_and(pl.program_id(0) == 0, pl.program_id(1) == 0), pl.program_id(2) == 0)

            @pl.when(first)
            def _():
                _scatter_issue(scatter[0], scatter[1], g_ref, recv_ref, ssem, rsem, lsem)

        kk = pl.program_id(2)

        @pl.when(kk == 0)
        def _():
            acc[...] = jnp.zeros_like(acc)

        acc[...] += lax.dot_general(a_ref[...].astype(BF16), b_ref[...].astype(BF16), (_DN[kind], ((), ())),
                                    preferred_element_type=F32)

        @pl.when(kk == nk - 1)
        def _():
            o_ref[...] = acc[...].astype(o_ref.dtype)

    if a_spec is None:
        a_spec = pl.BlockSpec((tk, tm), lambda i, j, k: (k, i)) if kind == 'tn' else pl.BlockSpec((tm, tk), lambda i, j, k: (i, k))
    if b_spec is None:
        b_spec = pl.BlockSpec((tn, tk), lambda i, j, k: (j, k)) if kind == 'nt' else pl.BlockSpec((tk, tn), lambda i, j, k: (k, j))
    if out_spec is None:
        out_spec = pl.BlockSpec((tm, tn), lambda i, j, k: (i, j))
        out_shape = (M, N)
    grid = (M // tm, N // tn, nk)
    if scatter is None:
        return pl.pallas_call(
            body, name=name, grid=grid, in_specs=[a_spec, b_spec], out_specs=out_spec,
            out_shape=jax.ShapeDtypeStruct(out_shape, out_dtype), scratch_shapes=[pltpu.VMEM((tm, tn), F32)],
            compiler_params=_cp(("parallel", "parallel", "arbitrary")))(a, b)
    cf, skind, g = scatter
    rshape = (N_DEV,) + _shard_shape(cf, skind)
    sems = pltpu.SemaphoreType.DMA((N_DEV - 1,))
    return pl.pallas_call(
        body, name=name, grid=grid, in_specs=[a_spec, b_spec, _HBM, _HBM], out_specs=[out_spec, _HBM, _HBM, _SEM, _SEM],
        out_shape=[jax.ShapeDtypeStruct(out_shape, out_dtype), pltpu.HBM(g.shape, BF16), pltpu.HBM(rshape, BF16), sems, sems],
        input_output_aliases={2: 1, 3: 2}, scratch_shapes=[pltpu.VMEM((tm, tn), F32), pltpu.SemaphoreType.DMA],
        compiler_params=pltpu.CompilerParams(dimension_semantics=("arbitrary", "arbitrary", "arbitrary"), vmem_limit_bytes=VMEM_LIMIT,
                                             has_side_effects=_EFFECT))(a, b, _hbm(g), _hbm(lax.empty(rshape, BF16)))


def _tile_spec(tm, w, colf, nrep):
    if hasattr(colf, 'base'):
        assert colf.base % nrep == 0
        return pl.BlockSpec((tm, w * nrep), functools.partial(lambda i, b: (i, b), b=colf.base // nrep))
    return pl.BlockSpec((tm, w), lambda i: (i, 0))


def _head_cols(colf, w, r):
    return slice(r * w, (r + 1) * w) if hasattr(colf, 'base') else slice(None)


def _row_specs(cf, tm, tiles, params, nrep):
    nctx = cf.LC // tm
    specs = [_tile_spec(tm, w, colf, nrep) for arr, w, colf, _ in tiles]
    for arr, kind, _ in params:
        nd = arr.ndim
        if kind == 'shared':
            specs.append(pl.BlockSpec(arr.shape, functools.partial(lambda i, nd: (0,) * nd, nd=nd)))
        else:
            specs.append(pl.BlockSpec((None,) + arr.shape[1:],
                                      functools.partial(lambda i, nd, nctx: (jnp.where(i >= nctx, 1, 0),) + (0,) * (nd - 1), nd=nd, nctx=nctx)))
    return specs


def _row_fwd(cf, name, f, tiles, params, outs, tm, nrep=1):
    nt, npar = len(tiles), len(params)

    def body(*refs):
        pv = [r[...] for r in refs[nt:nt + npar]]
        for r in range(nrep):
            tv = [x[:, _head_cols(t[2], t[1], r)].astype(F32) for x, t in zip(refs[:nt], tiles)]
            res = f(*tv, *pv)
            for o, v, spec in zip(refs[nt + npar:], res, outs):
                o[:, _head_cols(spec[1], spec[0], r)] = v.astype(o.dtype)

    out_specs = [_tile_spec(tm, w, colf, nrep) for w, colf, _, _ in outs]
    out_shape = [jax.ShapeDtypeStruct((cf.T, tw), dt) for _, _, tw, dt in outs]
    return pl.pallas_call(
        body, name=name, grid=(cf.T // tm,), in_specs=_row_specs(cf, tm, tiles, params, nrep), out_specs=out_specs,
        out_shape=out_shape, compiler_params=_cp(("arbitrary",)))(*[t[0] for t in tiles], *[p[0] for p in params])


def _row_bwd(cf, name, f, tiles, params, cts, tgrads, tm, nrep=1):
    nt, npar, nc = len(tiles), len(params), len(cts)
    tdiff = [k for k, t in enumerate(tiles) if t[3]]
    pdiff = [k for k, p in enumerate(params) if p[2]]
    nctx = cf.LC // tm

    def body(*refs):
        i = pl.program_id(0)
        pv = [x[...] for x in refs[nt:nt + npar]]
        outs = refs[nt + npar + nc:]
        psum = None
        for r in range(nrep):
            tv = [x[:, _head_cols(t[2], t[1], r)].astype(F32) for x, t in zip(refs[:nt], tiles)]
            cv = tuple(x[:, _head_cols(c[2], c[1], r)].astype(F32) for x, c in zip(refs[nt + npar:nt + npar + nc], cts))

            def g(*diff, tv=tv):
                tv2, pv2 = list(tv), list(pv)
                for k, v in zip(tdiff, diff[:len(tdiff)]):
                    tv2[k] = v
                for k, v in zip(pdiff, diff[len(tdiff):]):
                    pv2[k] = v
                return tuple(f(*tv2, *pv2))

            _, vjp_fn = jax.vjp(g, *[tv[k] for k in tdiff], *[pv[k] for k in pdiff])
            grads = vjp_fn(cv)
            for o, gv, spec in zip(outs[:len(tdiff)], grads[:len(tdiff)], tgrads):
                o[:, _head_cols(spec[1], spec[0], r)] = gv.astype(o.dtype)
            pg = grads[len(tdiff):]
            psum = list(pg) if psum is None else [a + b for a, b in zip(psum, pg)]
        for n_, (o, gv) in enumerate(zip(outs[len(tdiff):], psum)):
            first = (i == 0) if params[pdiff[n_]][1] == 'shared' else jnp.logical_or(i == 0, i == nctx)

            @pl.when(first)
            def _():
                o[...] = gv

            @pl.when(jnp.logical_not(first))
            def _():
                o[...] += gv

    in_specs = _row_specs(cf, tm, tiles, params, nrep)
    in_specs += [_tile_spec(tm, w, colf, nrep) for _, w, colf in cts]
    out_specs = [_tile_spec(tm, w, colf, nrep) for w, colf, _, _ in tgrads]
    out_shape = [jax.ShapeDtypeStruct((cf.T, tw), dt) for _, _, tw, dt in tgrads]
    out_specs += _row_specs(cf, tm, [], [params[k] for k in pdiff], nrep)
    out_shape += [jax.ShapeDtypeStruct(params[k][0].shape, F32) for k in pdiff]
    res = pl.pallas_call(
        body, name=name, grid=(cf.T // tm,), in_specs=in_specs, out_specs=out_specs, out_shape=out_shape,
        compiler_params=_cp(("arbitrary",)))(*[t[0] for t in tiles], *[p[0] for p in params], *[c[0] for c in cts])
    return res[:len(tdiff)], res[len(tdiff):]


def _c0(r):
    return 0


def _col(base):
    def col(r):
        return base + r
    col.base = base
    return col


def _rms(x, g):
    return x * lax.rsqrt(jnp.mean(x * x, axis=-1, keepdims=True) + EPS) * g


def _sigmoid(x):
    return 1.0 / (1.0 + jnp.exp(-x))


def _silu(x):
    return x * _sigmoid(x)


def _f_norm_mod(x, g, sh, sc):
    return (_rms(x, g) * (1 + sc) + sh,)


def _f_resid_norm_mod(x, y, gate, g, sh, sc):
    x1 = x + gate * y
    return (x1, _rms(x1, g) * (1 + sc) + sh)


@jax.custom_vjp
def _swap_halves(t):
    return pltpu.roll(t, HEAD_DIM // 2, axis=1)


def _swap_fwd(t):
    return _swap_halves(t), None


def _swap_bwd(_, g):
    return (pltpu.roll(g, HEAD_DIM // 2, axis=1),)


_swap_halves.defvjp(_swap_fwd, _swap_bwd)


def _rope(t, cs, sn):
    return t * cs + _swap_halves(t) * sn


def _f_prep_norm(t, cs, sn, g):
    return (_rope(_rms(t, g), cs, sn),)


def _f_prep_plain(t, cs, sn):
    return (_rope(t, cs, sn),)


def _f_prep_scaled(t, cs, sn):
    return (_rope(t * (HEAD_DIM ** -0.5), cs, sn),)


def _log_sigmoid(x):
    return jnp.minimum(x, 0.0) - jnp.log(1.0 + jnp.exp(-jnp.abs(x)))


N_DECAY = 8


def _gla_masks(d, width):
    C, SB = GLA_CHUNK, GLA_SUB
    r = lax.broadcasted_iota(jnp.int32, (C, C), 0)
    m = lax.broadcasted_iota(jnp.int32, (C, C), 1)
    rr = lax.broadcasted_iota(jnp.int32, (C, width), 0)
    allowed = (m <= r) if d == 0 else (m >= r)
    blocks, vis = [allowed], []
    for b in range(C // SB):
        blocks.append((m < SB * b) if d == 0 else (m >= SB * (b + 1)))
        vis.append((rr < SB * (b + 1)) if d == 0 else (rr >= SB * b))
    cm = jnp.concatenate([x.astype(F32) for x in blocks] + [jnp.ones((C, C), F32)], axis=0)
    return cm, allowed, vis


def _gla_decays(la, d):
    C, SB = GLA_CHUNK, GLA_SUB
    nsb = C // SB
    cm, _, vis = _gla_masks(d, la.shape[-1])
    cums = jnp.dot(cm, la, precision=HI, preferred_element_type=F32)
    cum, tot = cums[0:C], cums[(1 + nsb) * C:]
    refs = [cums[(1 + b) * C:(2 + b) * C] for b in range(nsb)]
    e1 = jnp.concatenate([jnp.exp(cum[b * SB:(b + 1) * SB] - refs[b][b * SB:(b + 1) * SB]) for b in range(nsb)], axis=0)
    e2 = [jnp.where(vis[b], jnp.exp(jnp.where(vis[b], refs[b] - cum, 0.0)), 0.0) for b in range(nsb)]
    return [e1] + e2 + [jnp.exp(cum), jnp.exp(tot - cum), jnp.exp(tot)]


def _f_gla_pre(ga, *per_pair):
    gab = ga.astype(BF16)
    outs = [[], []]
    for p in range(len(per_pair) // 4):
        gf, gb, bf, bb = per_pair[4 * p:4 * p + 4]
        for d, (gm, bm) in enumerate(((gf, bf), (gb, bb))):
            la = _log_sigmoid(jnp.dot(gab, gm.astype(BF16), preferred_element_type=F32) + bm) / GLA_TAU
            outs[d] += _gla_decays(la, d)
    return tuple(jnp.concatenate(o, axis=-1) for o in outs)


def _f_gated_norm(o, g, n):
    return (_rms(o, n) * _silu(g),)


def _f_first(x, g, sh, sc):
    return (x, _rms(x, g) * (1 + sc) + sh)


def _loss_grad(cf, x1, yff, gate, gfin, tgt):
    tm, T, D = cf.TM, cf.T, cf.D
    nctx = cf.LC // tm

    def lossf(x1v, yv, gt, gf, tg):
        y = _rms(x1v + gt * yv, gf)
        e = y - tg
        return 0.5 * jnp.sum(jnp.mean(e * e, axis=-1, keepdims=True), axis=0, keepdims=True)

    def body(x1_ref, y_ref, gt_ref, gf_ref, tg_ref, dx_ref, dy_ref, dgt_ref, dgf_ref, ls_ref):
        i = pl.program_id(0)
        lat = (i >= nctx).astype(F32)
        val, vjp_fn = jax.vjp(lossf, x1_ref[...], y_ref[...].astype(F32), gt_ref[...], gf_ref[...], tg_ref[...])
        dx, dy, dgt, dgf, _ = vjp_fn(jnp.ones((1, 1), F32) * lat)
        dx_ref[...] = dx
        dy_ref[...] = dy.astype(dy_ref.dtype)
        first_s = jnp.logical_or(i == 0, i == nctx)

        @pl.when(first_s)
        def _():
            dgt_ref[...] = dgt

        @pl.when(jnp.logical_not(first_s))
        def _():
            dgt_ref[...] += dgt

        @pl.when(i == 0)
        def _():
            dgf_ref[...] = dgf
            ls_ref[...] = jnp.zeros_like(ls_ref) + val * lat

        @pl.when(i != 0)
        def _():
            dgf_ref[...] += dgf
            ls_ref[...] += val * lat

    row = pl.BlockSpec((tm, D), lambda i: (i, 0))
    strm = pl.BlockSpec((None, 1, D), lambda i: (jnp.where(i >= nctx, 1, 0), 0, 0))
    one = pl.BlockSpec((1, D), lambda i: (0, 0))
    return pl.pallas_call(
        body, name="loss_grad", grid=(T // tm,), in_specs=[row, row, strm, one, row],
        out_specs=[row, row, strm, one, pl.BlockSpec((8, LANE), lambda i: (0, 0))],
        out_shape=[jax.ShapeDtypeStruct((T, D), F32), jax.ShapeDtypeStruct((T, D), BF16),
                   jax.ShapeDtypeStruct((2, 1, D), F32), jax.ShapeDtypeStruct((1, D), F32),
                   jax.ShapeDtypeStruct((8, LANE), F32)],
        compiler_params=_cp(("arbitrary",)))(x1, yff, gate, gfin, tgt)


def _att_mask(cf, i, tq):
    col = lax.broadcasted_iota(jnp.int32, (tq, cf.T), 1)
    return jnp.logical_or(col < cf.LC, i >= cf.LC // tq)


def _att_probs(q, k, mask):
    s = lax.dot_general(q, k, (_DN['nt'], ((), ())), preferred_element_type=F32) * (HEAD_DIM ** -0.5)
    s = jnp.where(mask, s, NEG)
    e = jnp.exp(s - jnp.max(s, axis=-1, keepdims=True))
    return e / jnp.sum(e, axis=-1, keepdims=True)


def _att_fwd(cf, q, k, z):
    tq, T, G = cf.TQ, cf.T, cf.G
    vb = cf.OFF['av'] // LANE

    def body(q_ref, k_ref, v_ref, o_ref):
        mask = _att_mask(cf, pl.program_id(1), tq)
        kv, vv = k_ref[...], v_ref[...].astype(BF16)
        for j in range(G):
            p = _att_probs(q_ref[:, j * LANE:(j + 1) * LANE], kv, mask)
            o_ref[:, j * LANE:(j + 1) * LANE] = jnp.dot(p.astype(BF16), vv, preferred_element_type=F32).astype(o_ref.dtype)

    return pl.pallas_call(
        body, name="att_fwd", grid=(cf.HKV, T // tq),
        in_specs=[pl.BlockSpec((tq, G * LANE), lambda g, i: (i, g)), pl.BlockSpec((T, LANE), lambda g, i: (0, g)),
                  pl.BlockSpec((T, LANE), lambda g, i: (0, vb + g))],
        out_specs=pl.BlockSpec((tq, G * LANE), lambda g, i: (i, g)),
        out_shape=jax.ShapeDtypeStruct((T, cf.HQ * LANE), BF16), compiler_params=_cp(("arbitrary", "arbitrary")))(q, k, z)


def _att_bwd(cf, q, k, z, dcat):
    tq, T, G = cf.TQ, cf.T, cf.G
    vb = cf.OFF['av'] // LANE
    sc = HEAD_DIM ** -0.5

    def body(q_ref, k_ref, v_ref, do_ref, dq_ref, dk_ref, dv_ref):
        i = pl.program_id(1)
        mask = _att_mask(cf, i, tq)
        kv, vv = k_ref[...], v_ref[...].astype(BF16)
        dk = jnp.zeros((T, LANE), F32)
        dv = jnp.zeros((T, LANE), F32)
        for j in range(G):
            qj = q_ref[:, j * LANE:(j + 1) * LANE]
            do = do_ref[:, j * LANE:(j + 1) * LANE]
            p = _att_probs(qj, kv, mask)
            dv += lax.dot_general(p.astype(BF16), do, (_DN['tn'], ((), ())), preferred_element_type=F32)
            dp = lax.dot_general(do, vv, (_DN['nt'], ((), ())), preferred_element_type=F32)
            ds = p * (dp - jnp.sum(dp * p, axis=-1, keepdims=True)) * sc
            dsb = ds.astype(BF16)
            dq_ref[:, j * LANE:(j + 1) * LANE] = jnp.dot(dsb, kv, preferred_element_type=F32)
            dk += lax.dot_general(dsb, qj, (_DN['tn'], ((), ())), preferred_element_type=F32)

        @pl.when(i == 0)
        def _():
            dk_ref[...] = dk
            dv_ref[...] = dv

        @pl.when(i != 0)
        def _():
            dk_ref[...] += dk
            dv_ref[...] += dv

    qs = pl.BlockSpec((tq, G * LANE), lambda g, i: (i, g))
    ks = pl.BlockSpec((T, LANE), lambda g, i: (0, g))
    return pl.pallas_call(
        body, name="att_bwd", grid=(cf.HKV, T // tq),
        in_specs=[qs, ks, pl.BlockSpec((T, LANE), lambda g, i: (0, vb + g)), qs],
        out_specs=[qs, ks, ks],
        out_shape=[jax.ShapeDtypeStruct((T, cf.HQ * LANE), F32), jax.ShapeDtypeStruct((T, cf.HKV * LANE), F32),
                   jax.ShapeDtypeStruct((T, cf.HKV * LANE), F32)],
        compiler_params=_cp(("arbitrary", "arbitrary")))(q, k, z, dcat)


def _ret_masks(cf, i, tq, lgf, lgb):
    T, LC = cf.T, cf.LC
    row = lax.broadcasted_iota(jnp.int32, (tq, T), 0) + i * tq
    col = lax.broadcasted_iota(jnp.int32, (tq, T), 1)

    def pb(n):
        return jnp.where(n < LC, LC - 1 - n, T + LC - 1 - n)

    relf = row - col
    relb = pb(row) - pb(col)
    okf, okb = relf >= 0, relb >= 0
    rf = jnp.where(okf, relf, 0).astype(F32)
    rb = jnp.where(okb, relb, 0).astype(F32)
    mf = jnp.where(okf, jnp.exp(lgf * rf), 0.0)
    mb = jnp.where(okb, jnp.exp(lgb * rb), 0.0)
    return mf, mb, rf, rb


def _ret_fwd(cf, q, k, z, lg):
    tq, T = cf.TQ, cf.T
    vb = cf.OFF['rv'] // LANE

    def body(lg_ref, q_ref, k_ref, v_ref, o_ref):
        h, i = pl.program_id(0), pl.program_id(1)
        mf, mb, _, _ = _ret_masks(cf, i, tq, lg_ref[0, h], lg_ref[1, h])
        a = lax.dot_general(q_ref[...], k_ref[...], (_DN['nt'], ((), ())), preferred_element_type=F32)
        p = (a * (mf + mb)).astype(BF16)
        o_ref[...] = jnp.dot(p, v_ref[...].astype(BF16), preferred_element_type=F32)

    return pl.pallas_call(
        body, name="ret_fwd", grid=(cf.HR, T // tq),
        in_specs=[pl.BlockSpec(memory_space=pltpu.SMEM), pl.BlockSpec((tq, LANE), lambda h, i: (i, h)),
                  pl.BlockSpec((T, LANE), lambda h, i: (0, h)), pl.BlockSpec((T, LANE), lambda h, i: (0, vb + h))],
        out_specs=pl.BlockSpec((tq, LANE), lambda h, i: (i, h)),
        out_shape=jax.ShapeDtypeStruct((T, cf.HR * LANE), F32), compiler_params=_cp(("arbitrary", "arbitrary")))(lg, q, k, z)


def _ret_bwd(cf, q, k, z, lg, do):
    tq, T = cf.TQ, cf.T
    vb = cf.OFF['rv'] // LANE

    def body(lg_ref, q_ref, k_ref, v_ref, do_ref, dq_ref, dk_ref, dv_ref, dlg_ref):
        h, i = pl.program_id(0), pl.program_id(1)
        mf, mb, rf, rb = _ret_masks(cf, i, tq, lg_ref[0, h], lg_ref[1, h])
        qv, kv, vv = q_ref[...], k_ref[...], v_ref[...].astype(BF16)
        dob = do_ref[...].astype(BF16)
        a = lax.dot_general(qv, kv, (_DN['nt'], ((), ())), preferred_element_type=F32)
        m = mf + mb
        p = (a * m).astype(BF16)
        dv = lax.dot_general(p, dob, (_DN['tn'], ((), ())), preferred_element_type=F32)
        dp = lax.dot_general(dob, vv, (_DN['nt'], ((), ())), preferred_element_type=F32)
        da = (dp * m).astype(BF16)
        dq_ref[...] = jnp.dot(da, kv, preferred_element_type=F32)
        dk = lax.dot_general(da, qv, (_DN['tn'], ((), ())), preferred_element_type=F32)
        dm = dp * a
        dlf = jnp.sum(jnp.sum(dm * mf * rf, axis=-1, keepdims=True), axis=0, keepdims=True)
        dlb = jnp.sum(jnp.sum(dm * mb * rb, axis=-1, keepdims=True), axis=0, keepdims=True)
        rid = lax.broadcasted_iota(jnp.int32, (8, LANE), 0)
        dl = jnp.where(rid == 0, dlf, jnp.where(rid == 1, dlb, 0.0))

        @pl.when(i == 0)
        def _():
            dk_ref[...] = dk
            dv_ref[...] = dv
            dlg_ref[...] = dl

        @pl.when(i != 0)
        def _():
            dk_ref[...] += dk
            dv_ref[...] += dv
            dlg_ref[...] += dl

    qs = pl.BlockSpec((tq, LANE), lambda h, i: (i, h))
    ks = pl.BlockSpec((T, LANE), lambda h, i: (0, h))
    return pl.pallas_call(
        body, name="ret_bwd", grid=(cf.HR, T // tq),
        in_specs=[pl.BlockSpec(memory_space=pltpu.SMEM), qs, ks, pl.BlockSpec((T, LANE), lambda h, i: (0, vb + h)), qs],
        out_specs=[qs, ks, ks, pl.BlockSpec((None, 8, LANE), lambda h, i: (h, 0, 0))],
        out_shape=[jax.ShapeDtypeStruct((T, cf.HR * LANE), F32)] * 3 + [jax.ShapeDtypeStruct((cf.HR, 8, LANE), F32)],
        compiler_params=_cp(("arbitrary", "arbitrary")))(lg, q, k, z, do)


def _gla_step(q, k, v, es, st, lmask, allowed):
    C, SB = GLA_CHUNK, GLA_SUB
    nsb = C // SB
    e1, e2, e3, e4, e5 = es[0], es[1:1 + nsb], es[1 + nsb], es[2 + nsb], es[3 + nsb]
    qs = q * lmask * ((HEAD_DIM // 2) ** -0.5)
    ks = k * lmask
    qt = qs * e1
    rows = [lax.dot_general(qt[b * SB:(b + 1) * SB], ks * e2[b], (_DN['nt'], ((), ())), precision=lax.Precision.HIGH,
                            preferred_element_type=F32) for b in range(nsb)]
    att = jnp.where(allowed, jnp.concatenate(rows, axis=0), 0.0)
    o = jnp.dot(att.astype(BF16), v.astype(BF16), preferred_element_type=F32)
    o += lax.dot_general((qs * e3).astype(BF16), st.astype(BF16), (_DN['nt'], ((), ())), preferred_element_type=F32)
    kd = (ks * e4).astype(BF16)
    st_new = st * jnp.concatenate([e5, e5], axis=0) + lax.dot_general(v.astype(BF16), kd, (_DN['tn'], ((), ())), preferred_element_type=F32)
    return o, st_new


def _gla_allowed(d):
    r = lax.broadcasted_iota(jnp.int32, (GLA_CHUNK, GLA_CHUNK), 0)
    m = lax.broadcasted_iota(jnp.int32, (GLA_CHUNK, GLA_CHUNK), 1)
    return (m <= r) if d == 0 else (m >= r)


def _gla_chunk_id(cf, s, d):
    if d == 0:
        return s
    nct, nc = cf.LC // GLA_CHUNK, cf.T // GLA_CHUNK
    return jnp.where(s < nct, nct - 1 - s, nc + nct - 1 - s)


def _gla_lmask(h):
    return (lax.broadcasted_iota(jnp.int32, (1, LANE), 1) // (LANE // 2) == h).astype(F32)


_GLA_CHAINS = [(h, d) for h in range(2) for d in range(2)]


def _gla_row_specs(cf, nc, reverse):
    def rowblk(s, d):
        return _gla_chunk_id(cf, nc - 1 - s if reverse else s, d)

    def spec(width, base, d, per_pair=1):
        return pl.BlockSpec((GLA_CHUNK, width), functools.partial(lambda p, s, base, d: (rowblk(s, d), base + per_pair * p), base=base, d=d))

    def state(d):
        return pl.BlockSpec((2, None, LANE, LANE), functools.partial(lambda p, s, d: (p, rowblk(s, d), 0, 0), d=d))

    return spec, state


def _gla_fwd(cf, z, ef, eb):
    T, C = cf.T, GLA_CHUNK
    nc = T // C
    qb, kb, vb = cf.OFF['gq'] // LANE, cf.OFF['gk'] // LANE, cf.OFF['gv'] // (2 * LANE)
    spec, state = _gla_row_specs(cf, nc, False)

    def body(qf, kf, vf, e_f, qb_, kb_, vb_, e_b, of, sf, ob, sb, st_scr):
        @pl.when(pl.program_id(1) == 0)
        def _():
            st_scr[...] = jnp.zeros_like(st_scr)

        io = [(qf, kf, vf, e_f, of, sf), (qb_, kb_, vb_, e_b, ob, sb)]
        for ci, (h, d) in enumerate(_GLA_CHAINS):
            q, k, v, e, o_ref, s_ref = io[d]
            cols = slice(h * LANE, (h + 1) * LANE)
            es = [e[:, n * LANE:(n + 1) * LANE] for n in range(N_DECAY)]
            st = st_scr[ci]
            s_ref[h] = st
            o, stn = _gla_step(q[...], k[...], v[:, cols], es, st, _gla_lmask(h), _gla_allowed(d))
            st_scr[ci] = stn
            o_ref[:, cols] = o

    ins, outs = [], []
    for d in range(2):
        ins += [spec(LANE, qb, d), spec(LANE, kb, d), spec(2 * LANE, vb, d), spec(N_DECAY * LANE, 0, d)]
        outs += [spec(2 * LANE, 0, d), state(d)]
    oshape = [jax.ShapeDtypeStruct((T, cf.HG * LANE), F32), jax.ShapeDtypeStruct((cf.HG, nc, LANE, LANE), F32)]
    return pl.pallas_call(
        body, name="gla_fwd", grid=(cf.HG // 2, nc), in_specs=ins, out_specs=outs, out_shape=oshape * 2,
        scratch_shapes=[pltpu.VMEM((4, LANE, LANE), F32)],
        compiler_params=_cp(("arbitrary", "arbitrary")))(z, z, z, ef, z, z, z, eb)


def _gla_bwd(cf, z, ef, eb, sf, sb, do):
    T, C = cf.T, GLA_CHUNK
    nc = T // C
    qb, kb, vb = cf.OFF['gq'] // LANE, cf.OFF['gk'] // LANE, cf.OFF['gv'] // (2 * LANE)
    spec, state = _gla_row_specs(cf, nc, True)

    def body(*refs):
        ins = [refs[0:6], refs[6:12]]
        outs = [refs[12:16], refs[16:20]]
        dst_scr = refs[20]

        @pl.when(pl.program_id(1) == 0)
        def _():
            dst_scr[...] = jnp.zeros_like(dst_scr)

        acc = [None, None]
        for ci, (h, d) in enumerate(_GLA_CHAINS):
            q, k, v, e, s_ref, do_ref = ins[d]
            cols = slice(h * LANE, (h + 1) * LANE)
            es = [e[:, n * LANE:(n + 1) * LANE] for n in range(N_DECAY)]
            step = functools.partial(_gla_step, lmask=_gla_lmask(h), allowed=_gla_allowed(d))
            _, vjp_fn = jax.vjp(step, q[...], k[...], v[:, cols], es, s_ref[h])
            dq, dk, dv, des, dst = vjp_fn((do_ref[:, cols], dst_scr[ci]))
            dst_scr[ci] = dst
            outs[d][2][:, cols] = dv
            part = [dq, dk] + list(des)
            acc[d] = part if acc[d] is None else [a + b for a, b in zip(acc[d], part)]
        for d in range(2):
            dq_ref, dk_ref, _, de_ref = outs[d]
            dq_ref[...] = acc[d][0]
            dk_ref[...] = acc[d][1]
            for n in range(N_DECAY):
                de_ref[:, n * LANE:(n + 1) * LANE] = acc[d][2 + n]

    in_specs, out_specs = [], []
    for d in range(2):
        in_specs += [spec(LANE, qb, d), spec(LANE, kb, d), spec(2 * LANE, vb, d), spec(N_DECAY * LANE, 0, d), state(d), spec(2 * LANE, 0, d)]
        out_specs += [spec(LANE, 0, d), spec(LANE, 0, d), spec(2 * LANE, 0, d), spec(N_DECAY * LANE, 0, d)]
    npair = cf.HG // 2
    oshape = [jax.ShapeDtypeStruct((T, npair * LANE), F32), jax.ShapeDtypeStruct((T, npair * LANE), F32),
              jax.ShapeDtypeStruct((T, cf.HG * LANE), F32), jax.ShapeDtypeStruct((T, npair * N_DECAY * LANE), F32)]
    return pl.pallas_call(
        body, name="gla_bwd", grid=(npair, nc), in_specs=in_specs, out_specs=out_specs, out_shape=oshape * 2,
        scratch_shapes=[pltpu.VMEM((4, LANE, LANE), F32)],
        compiler_params=_cp(("arbitrary", "arbitrary")))(z, z, z, ef, sf, do, z, z, z, eb, sb, do)


def _conv_parts(cf, a, w_ref):
    T, LC = cf.T, cf.LC
    rid = lax.broadcasted_iota(jnp.int32, a.shape, 0)
    first = jnp.logical_or(rid == 0, rid == LC)
    last = jnp.logical_or(rid == LC - 1, rid == T - 1)
    ap = jnp.where(first, 0.0, pltpu.roll(a, 1, axis=0))
    an = jnp.where(last, 0.0, pltpu.roll(a, T - 1, axis=0))
    w0, w1, w2, b = w_ref[0:1, :], w_ref[1:2, :], w_ref[2:3, :], w_ref[3:4, :]
    ac = ap * w0 + a * w1 + an * w2 + b
    return ap, an, ac, first, last, (w0, w1, w2)


def _conv_fwd(cf, u, wb):
    T, Fd = cf.T, cf.F
    tc = _tile(Fd, 512)
    nj = Fd // tc

    def body(a_ref, v_ref, w_ref, t_ref):
        _, _, ac, _, _, _ = _conv_parts(cf, a_ref[...], w_ref)
        t_ref[...] = (_silu(ac) * v_ref[...]).astype(t_ref.dtype)

    return pl.pallas_call(
        body, name="conv_fwd", grid=(nj,),
        in_specs=[pl.BlockSpec((T, tc), lambda j: (0, j)), pl.BlockSpec((T, tc), lambda j: (0, nj + j)),
                  pl.BlockSpec((8, tc), lambda j: (0, j))],
        out_specs=pl.BlockSpec((T, tc), lambda j: (0, j)), out_shape=jax.ShapeDtypeStruct((T, Fd), BF16),
        compiler_params=_cp(("parallel",)))(u, u, wb)


def _conv_bwd(cf, u, wb, dt):
    T, Fd = cf.T, cf.F
    tc = _tile(Fd, 256)
    nj = Fd // tc

    def body(a_ref, v_ref, w_ref, dt_ref, da_ref, dv_ref, dw_ref):
        a, v, dtv = a_ref[...], v_ref[...], dt_ref[...].astype(F32)
        ap, an, ac, first, last, (w0, w1, w2) = _conv_parts(cf, a, w_ref)
        sg = _sigmoid(ac)
        dv_ref[...] = (dtv * ac * sg).astype(dv_ref.dtype)
        dac = dtv * v * (sg * (1.0 + ac * (1.0 - sg)))
        from_next = pltpu.roll(jnp.where(first, 0.0, dac), T - 1, axis=0)
        from_prev = pltpu.roll(jnp.where(last, 0.0, dac), 1, axis=0)
        da_ref[...] = (dac * w1 + from_next * w0 + from_prev * w2).astype(da_ref.dtype)
        rows = [jnp.sum(dac * ap, axis=0, keepdims=True), jnp.sum(dac * a, axis=0, keepdims=True),
                jnp.sum(dac * an, axis=0, keepdims=True), jnp.sum(dac, axis=0, keepdims=True)]
        rid = lax.broadcasted_iota(jnp.int32, (8, tc), 0)
        dw = jnp.zeros((8, tc), F32)
        for n_, rw in enumerate(rows):
            dw = jnp.where(rid == n_, rw, dw)
        dw_ref[...] = dw

    col = pl.BlockSpec((T, tc), lambda j: (0, j))
    return pl.pallas_call(
        body, name="conv_bwd", grid=(nj,),
        in_specs=[col, pl.BlockSpec((T, tc), lambda j: (0, nj + j)), pl.BlockSpec((8, tc), lambda j: (0, j)), col],
        out_specs=[col, col, pl.BlockSpec((8, tc), lambda j: (0, j))],
        out_shape=[jax.ShapeDtypeStruct((T, Fd), BF16), jax.ShapeDtypeStruct((T, Fd), BF16), jax.ShapeDtypeStruct((8, Fd), F32)],
        compiler_params=_cp(("parallel",)))(u, u, wb, dt)


def _me():
    x, y, c = lax.axis_index("x"), lax.axis_index("y"), lax.axis_index("c")
    return x, y, c, 4 * x + 2 * y + c


def _peer(x, y, c, k):
    px = 1 - x if (k >> 2) & 1 else x
    py = 1 - y if (k >> 1) & 1 else y
    pc = 1 - c if k & 1 else c
    return (px, py, pc), 4 * px + 2 * py + pc


def _rcopy(src, dst, ss, rs, tgt):
    return pltpu.make_async_remote_copy(src_ref=src, dst_ref=dst, send_sem=ss, recv_sem=rs, device_id=tgt,
                                        device_id_type=pl.DeviceIdType.MESH)


def _ag_small(name, v):
    R, Cc = v.shape

    def body(v_ref, o_ref, ssem, rsem, lsem):
        x, y, c, me = _me()
        loc = pltpu.make_async_copy(v_ref, o_ref.at[me], lsem)
        loc.start()
        sends = []
        for k in range(1, N_DEV):
            tgt, _ = _peer(x, y, c, k)
            cp = _rcopy(v_ref, o_ref.at[me], ssem.at[k - 1], rsem.at[k - 1], tgt)
            cp.start()
            sends.append(cp)
        for k in range(1, N_DEV):
            tgt, pi = _peer(x, y, c, k)
            _rcopy(v_ref, o_ref.at[pi], ssem.at[k - 1], rsem.at[k - 1], tgt).wait_recv()
        for cp in sends:
            cp.wait_send()
        loc.wait()

    vm = pl.BlockSpec(memory_space=pltpu.VMEM)
    return pl.pallas_call(
        body, name=name, in_specs=[vm], out_specs=vm, out_shape=jax.ShapeDtypeStruct((N_DEV, R, Cc), v.dtype),
        scratch_shapes=[pltpu.SemaphoreType.DMA((N_DEV - 1,)), pltpu.SemaphoreType.DMA((N_DEV - 1,)), pltpu.SemaphoreType.DMA],
        compiler_params=pltpu.CompilerParams(vmem_limit_bytes=VMEM_LIMIT))(v)


_HBM = pl.BlockSpec(memory_space=pltpu.HBM)
_SEM = pl.BlockSpec(memory_space=pltpu.SEMAPHORE)
_EFFECT = pltpu.SideEffectType.DATAFLOW_SIDE_EFFECTING
_KINDS = ['in', 'out', 'up', 'down']


def _hbm(a):
    return pltpu.with_memory_space_constraint(a, pltpu.HBM)


def _shard_shape(cf, kind):
    D, Fd = cf.D, cf.F
    return {'in': (D, cf.NINS), 'out': (D // N_DEV, D), 'up': (D, 2 * Fd // N_DEV), 'down': (Fd // N_DEV, D)}[kind]


def _whole_shape(cf, kind):
    D, Fd = cf.D, cf.F
    return {'in': (N_DEV, D, cf.NINS), 'out': (D, D), 'up': (D, 2 * Fd), 'down': (Fd, D)}[kind]


def _part(ref, cf, kind, idx):
    r, cdim = _shard_shape(cf, kind)
    if kind == 'in':
        return ref.at[idx]
    if kind == 'up':
        return ref.at[:, pl.ds(pl.multiple_of(idx * cdim, cdim), cdim)]
    return ref.at[pl.ds(pl.multiple_of(idx * r, r), r), :]


def _ag_start(cf, shards, after):
    npc = DEPTH * len(_KINDS)
    nio = len(_KINDS) + npc

    def body(*refs):
        srcs, lands = refs[:len(_KINDS)], refs[len(_KINDS):nio]
        ssems, rsems = refs[2 * nio + 1:2 * nio + 1 + npc], refs[2 * nio + 1 + npc:2 * nio + 1 + 2 * npc]
        token, lsem = refs[2 * nio + 1 + 2 * npc], refs[2 * nio + 2 + 2 * npc]
        x, y, c, me = _me()
        pieces = [(l * len(_KINDS) + n, srcs[n].at[l], kind) for l in range(DEPTH) for n, kind in enumerate(_KINDS)]
        locs = [pltpu.make_async_copy(src, _part(lands[p], cf, kind, me), lsem.at[p]) for p, src, kind in pieces]
        for cp in locs:
            cp.start()
        for cp in locs:
            cp.wait()
        for p, src, kind in pieces:
            for k in range(1, N_DEV):
                tgt, _ = _peer(x, y, c, k)
                _rcopy(src, _part(lands[p], cf, kind, me), ssems[p].at[k - 1], rsems[p].at[k - 1], tgt).start()
        token[...] = jnp.zeros_like(token)

    land_shapes = [_whole_shape(cf, kind) for _ in range(DEPTH) for kind in _KINDS]
    out_shape = [pltpu.HBM(s.shape, BF16) for s in shards] + [pltpu.HBM(s, BF16) for s in land_shapes]
    out_shape += [pltpu.SemaphoreType.DMA((N_DEV - 1,))] * (2 * npc) + [jax.ShapeDtypeStruct((8, LANE), F32)]
    res = pl.pallas_call(
        body, name="ag_start", in_specs=[_HBM] * nio + [pl.BlockSpec(memory_space=pl.ANY)],
        out_specs=[_HBM] * nio + [_SEM] * (2 * npc) + [pl.BlockSpec(memory_space=pltpu.VMEM)],
        out_shape=out_shape, input_output_aliases={i: i for i in range(nio)}, scratch_shapes=[pltpu.SemaphoreType.DMA((npc,))],
        compiler_params=pltpu.CompilerParams(has_side_effects=_EFFECT))(
            *[_hbm(s) for s in shards], *[_hbm(lax.empty(s, BF16)) for s in land_shapes], after)
    return res[:len(_KINDS)], res[len(_KINDS):nio], res[nio:nio + npc], res[nio + npc:nio + 2 * npc], res[nio + 2 * npc]


def _ag_wait(cf, kind, l, src, land, ssem, rsem, after):
    def body(src_ref, land_ref, ssem_ref, rsem_ref, after_ref, src_out, land_out):
        x, y, c, me = _me()
        for k in range(1, N_DEV):
            tgt, pi = _peer(x, y, c, k)
            cp = _rcopy(src_ref.at[l], _part(land_ref, cf, kind, pi), ssem_ref.at[k - 1], rsem_ref.at[k - 1], tgt)
            cp.wait_send()
            cp.wait_recv()

    return pl.pallas_call(
        body, name="ag_wait_%s_%d" % (kind, l), in_specs=[_HBM, _HBM, _SEM, _SEM, pl.BlockSpec(memory_space=pl.ANY)],
        out_specs=[_HBM, _HBM], out_shape=[pltpu.HBM(src.shape, src.dtype), pltpu.HBM(land.shape, land.dtype)],
        input_output_aliases={0: 0, 1: 1}, compiler_params=pltpu.CompilerParams(has_side_effects=_EFFECT))(src, land, ssem, rsem, after)


N_BARRIER_IDS = 8


def _handshake(x, y, c):
    barrier = pltpu.get_barrier_semaphore()
    for k in range(1, N_DEV):
        pl.semaphore_signal(barrier, inc=1, device_id=_peer(x, y, c, k)[0], device_id_type=pl.DeviceIdType.MESH)
    pl.semaphore_wait(barrier, N_DEV - 1)


def _seq_kernel(body, name, seq, out_type):
    return pl.kernel(
        body, out_type=out_type, mesh=plsc.ScalarSubcoreMesh(axis_name="sq", num_cores=1), name=name,
        scratch_types=[pltpu.SemaphoreType.DMA((N_DEV - 1,)), pltpu.SemaphoreType.DMA((N_DEV - 1,)), pltpu.SemaphoreType.DMA],
        compiler_params=pltpu.CompilerParams(collective_id=seq % N_BARRIER_IDS))


def _seq_gather(cf, kind, l, seq, src):
    def body(src_ref, land_ref, ssem, rsem, lsem):
        x, y, c, me = _me()
        _handshake(x, y, c)
        sib = (x, y, 1 - c)
        chips = [(1 - x, y), (x, 1 - y), (1 - x, 1 - y)]

        def blk(px, py, pc):
            return _part(land_ref, cf, kind, 4 * px + 2 * py + pc)

        src, mine = src_ref.at[l], blk(x, y, c)
        loc = pltpu.make_async_copy(src, mine, lsem)
        loc.start()
        loc.wait()
        first = [_rcopy(src, mine, ssem.at[0], rsem.at[0], sib)]
        first += [_rcopy(src, mine, ssem.at[1 + j], rsem.at[1 + j], (*chip, c)) for j, chip in enumerate(chips)]
        for cp in first:
            cp.start()
        passed = [_rcopy(blk(*chip, c), blk(*chip, c), ssem.at[4 + j], rsem.at[4 + j], sib) for j, chip in enumerate(chips)]
        for j, chip in enumerate(chips):
            _rcopy(src, blk(*chip, c), ssem.at[1 + j], rsem.at[1 + j], (*chip, c)).wait_recv()
            passed[j].start()
        _rcopy(src, blk(x, y, 1 - c), ssem.at[0], rsem.at[0], sib).wait_recv()
        for j, chip in enumerate(chips):
            _rcopy(src, blk(*chip, 1 - c), ssem.at[4 + j], rsem.at[4 + j], sib).wait_recv()
        for cp in first + passed:
            cp.wait_send()

    return _seq_kernel(body, "seq_gather_%s_%d" % (kind, l), seq, jax.ShapeDtypeStruct(_whole_shape(cf, kind), BF16))(src)


def _seq_scatter(cf, kind, l, seq, g):
    def body(g_ref, recv_ref, ssem, rsem, lsem):
        x, y, c, me = _me()
        _handshake(x, y, c)
        loc = pltpu.make_async_copy(_rs_slab(g_ref, cf, kind, me), recv_ref.at[me], lsem)
        loc.start()
        loc.wait()
        sends = []
        for k in range(1, N_DEV):
            tgt, pi = _peer(x, y, c, k)
            sends.append(_rcopy(_rs_slab(g_ref, cf, kind, pi), recv_ref.at[me], ssem.at[k - 1], rsem.at[k - 1], tgt))
            sends[-1].start()
        for k in range(1, N_DEV):
            tgt, pi = _peer(x, y, c, k)
            _rcopy(_rs_slab(g_ref, cf, kind, pi), recv_ref.at[pi], ssem.at[k - 1], rsem.at[k - 1], tgt).wait_recv()
        for cp in sends:
            cp.wait_send()

    return _seq_kernel(body, "seq_scatter_%s_%d" % (kind, l), seq, jax.ShapeDtypeStruct((N_DEV,) + _shard_shape(cf, kind), BF16))(g)


def _rs_slab(ref, cf, kind, j):
    return ref.at[j] if kind in ('in', 'up') else _part(ref, cf, kind, j)


def _scatter_issue(cf, kind, g_ref, recv_ref, ssem, rsem, lsem):
    x, y, c, me = _me()
    loc = pltpu.make_async_copy(_rs_slab(g_ref, cf, kind, me), recv_ref.at[me], lsem)
    loc.start()
    loc.wait()
    for k in range(1, N_DEV):
        tgt, pi = _peer(x, y, c, k)
        _rcopy(_rs_slab(g_ref, cf, kind, pi), recv_ref.at[me], ssem.at[k - 1], rsem.at[k - 1], tgt).start()


def _rs_start(cf, kind, l, g, after):
    def body(g_ref, recv_ref, after_ref, g_out, recv_out, ssem, rsem, token, lsem):
        _scatter_issue(cf, kind, g_ref, recv_ref, ssem, rsem, lsem)
        token[...] = jnp.zeros_like(token)

    rshape = (N_DEV,) + _shard_shape(cf, kind)
    sems = pltpu.SemaphoreType.DMA((N_DEV - 1,))
    return pl.pallas_call(
        body, name="rs_start_%s_%d" % (kind, l), in_specs=[_HBM, _HBM, pl.BlockSpec(memory_space=pl.ANY)],
        out_specs=[_HBM, _HBM, _SEM, _SEM, pl.BlockSpec(memory_space=pltpu.VMEM)],
        out_shape=[pltpu.HBM(g.shape, BF16), pltpu.HBM(rshape, BF16), sems, sems, jax.ShapeDtypeStruct((8, LANE), F32)],
        input_output_aliases={0: 0, 1: 1}, scratch_shapes=[pltpu.SemaphoreType.DMA],
        compiler_params=pltpu.CompilerParams(has_side_effects=_EFFECT))(_hbm(g), _hbm(lax.empty(rshape, BF16)), after)


def _rs_wait(cf, kind, l, g, recv, ssem, rsem, after):
    def body(g_ref, recv_ref, ssem_ref, rsem_ref, after_ref, g_out, recv_out):
        x, y, c, me = _me()
        for k in range(1, N_DEV):
            tgt, pi = _peer(x, y, c, k)
            cp = _rcopy(_rs_slab(g_ref, cf, kind, pi), recv_ref.at[pi], ssem_ref.at[k - 1], rsem_ref.at[k - 1], tgt)
            cp.wait_send()
            cp.wait_recv()

    return pl.pallas_call(
        body, name="rs_wait_%s_%d" % (kind, l), in_specs=[_HBM, _HBM, _SEM, _SEM, pl.BlockSpec(memory_space=pl.ANY)],
        out_specs=[_HBM, _HBM], out_shape=[pltpu.HBM(g.shape, g.dtype), pltpu.HBM(recv.shape, recv.dtype)],
        input_output_aliases={0: 0, 1: 1}, compiler_params=pltpu.CompilerParams(has_side_effects=_EFFECT))(g, recv, ssem, rsem, after)[1]


def _adam_vals(w, g, m, v):
    m2 = ADAM_B1 * m + (1.0 - ADAM_B1) * g
    v2 = ADAM_B2 * v + (1.0 - ADAM_B2) * (g * g)
    mh = m2 / (1.0 - ADAM_B1 ** ADAM_STEP)
    vh = v2 / (1.0 - ADAM_B2 ** ADAM_STEP)
    return -ADAM_LR * (mh / (jnp.sqrt(vh) + ADAM_EPS) + ADAM_WD * w), m2, v2


def _row_tile(R, Cc, budget_elems):
    t = max(16, min(R, (budget_elems // max(Cc, 1)) // 16 * 16))
    while t > 16 and R % t:
        t -= 16
    return t if R % t == 0 else R


def _cast_bf16(name, w, after):
    Dp, R, Cc = w.shape
    tr = _row_tile(R, Cc, 1 << 20)

    def body(w_ref, after_ref, o_ref):
        o_ref[...] = w_ref[...].astype(BF16)

    spec = pl.BlockSpec((None, tr, Cc), lambda l, i: (l, i, 0))
    return pl.pallas_call(body, name=name, grid=(Dp, R // tr), in_specs=[spec, pl.BlockSpec(memory_space=pl.ANY)], out_specs=spec,
                          out_shape=jax.ShapeDtypeStruct(w.shape, BF16), compiler_params=_cp(("parallel", "parallel")))(w, after)


def _unshard_in(cf, name, g):
    D, ns, nz = cf.D, cf.NINS, cf.NZ
    tr = _tile(D, 256, 16)

    def body(g_ref, o_ref):
        for j in range(N_DEV):
            o_ref[:, ns * j:ns * (j + 1)] = g_ref[j]
        o_ref[:, N_DEV * ns:] = jnp.zeros((tr, nz - N_DEV * ns), o_ref.dtype)

    return pl.pallas_call(body, name=name, grid=(D // tr,), in_specs=[pl.BlockSpec((N_DEV, tr, ns), lambda i: (0, i, 0))],
                          out_specs=pl.BlockSpec((tr, nz), lambda i: (i, 0)), out_shape=jax.ShapeDtypeStruct((D, nz), g.dtype),
                          compiler_params=_cp(("parallel",)))(g)


def _slabs_in(cf, name, gw):
    D, ns, nz = cf.D, cf.NINS, cf.NZ
    tr = _tile(D, 256, 16)

    def body(x_ref, o_ref):
        for j in range(N_DEV):
            o_ref[j] = x_ref[:, ns * j:ns * (j + 1)]

    return pl.pallas_call(body, name=name, grid=(D // tr,), in_specs=[pl.BlockSpec((tr, nz), lambda i: (i, 0))],
                          out_specs=pl.BlockSpec((N_DEV, tr, ns), lambda i: (0, i, 0)), out_shape=jax.ShapeDtypeStruct((N_DEV, D, ns), gw.dtype),
                          compiler_params=_cp(("parallel",)))(gw)


def _sum_adam(name, recv, w, m, v, layer, prev, after):
    Dp, R, Cc = w.shape
    tr = _row_tile(R, Cc, 1 << 18)

    def body(r_ref, w_ref, m_ref, v_ref, *rest):
        g_ref, d_ref, mo_ref, vo_ref = rest[-4:]
        g = r_ref[0].astype(F32)
        for s in range(1, N_DEV):
            g = g + r_ref[s].astype(F32)
        dl, m2, v2 = _adam_vals(w_ref[...], g, m_ref[...], v_ref[...])
        g_ref[...] = g
        d_ref[...] = dl
        mo_ref[...] = m2
        vo_ref[...] = v2

    spec = pl.BlockSpec((None, tr, Cc), lambda i: (layer, i, 0))
    anyspec = pl.BlockSpec(memory_space=pl.ANY)
    extra = [after] + (list(prev) if prev is not None else [])
    aliases = {5 + n: n for n in range(4)} if prev is not None else {}
    return pl.pallas_call(body, name=name, grid=(R // tr,), in_specs=[pl.BlockSpec((N_DEV, tr, Cc), lambda i: (0, i, 0)), spec, spec, spec] + [anyspec] * len(extra),
                          out_specs=[spec] * 4, out_shape=[jax.ShapeDtypeStruct(w.shape, F32)] * 4, input_output_aliases=aliases,
                          compiler_params=_cp(("parallel",)))(recv, w, m, v, *extra)


def _adam(name, w, g, m, v):
    R, Cc = w.shape
    tr = _row_tile(R, Cc, 1 << 18)

    def body(w_ref, g_ref, m_ref, v_ref, d_ref, mo_ref, vo_ref):
        dl, m2, v2 = _adam_vals(w_ref[...], g_ref[...], m_ref[...], v_ref[...])
        d_ref[...] = dl
        mo_ref[...] = m2
        vo_ref[...] = v2

    spec = pl.BlockSpec((tr, Cc), lambda i: (i, 0))
    return pl.pallas_call(body, name=name, grid=(R // tr,), in_specs=[spec] * 4, out_specs=[spec] * 3,
                          out_shape=[jax.ShapeDtypeStruct(w.shape, F32)] * 3, compiler_params=_cp(("parallel",)))(w, g, m, v)


def _sum8(name, a):
    n, R, Cc = a.shape

    def body(a_ref, o_ref):
        s = a_ref[0]
        for k in range(1, n):
            s = s + a_ref[k]
        o_ref[...] = s

    return pl.pallas_call(body, name=name, in_specs=[pl.BlockSpec(memory_space=pltpu.VMEM)],
                          out_specs=pl.BlockSpec(memory_space=pltpu.VMEM), out_shape=jax.ShapeDtypeStruct((R, Cc), F32),
                          compiler_params=pltpu.CompilerParams(vmem_limit_bytes=VMEM_LIMIT))(a)


def _ada_fwd(cf, c9, ada_w):
    D = cf.D
    NS = ada_w.shape[-1]
    tk = _tile(D, 512)
    nk = D // tk

    def body(c_ref, w_ref, o_ref):
        kk = pl.program_id(1)
        s = _silu(c_ref[...]).astype(BF16)
        part = jnp.dot(s, w_ref[...].astype(BF16), preferred_element_type=F32)

        @pl.when(kk == 0)
        def _():
            o_ref[...] = part

        @pl.when(kk != 0)
        def _():
            o_ref[...] += part

    return pl.pallas_call(
        body, name="ada_fwd", grid=(DEPTH, nk),
        in_specs=[pl.BlockSpec((16, tk), lambda l, k: (0, k)), pl.BlockSpec((None, tk, NS), lambda l, k: (l, k, 0))],
        out_specs=pl.BlockSpec((None, 16, NS), lambda l, k: (l, 0, 0)),
        out_shape=jax.ShapeDtypeStruct((DEPTH, 16, NS), F32), compiler_params=_cp(("parallel", "arbitrary")))(c9, ada_w)


def _ada_bwd(cf, c9, ada_w, dm9):
    D = cf.D
    NS = ada_w.shape[-1]
    tk = _tile(D, 512)
    nk = D // tk

    def body(c_ref, w_ref, dm_ref, gw_ref, ds_ref):
        cv = c_ref[...]
        sg = _sigmoid(cv)
        dmb = dm_ref[...].astype(BF16)
        gw_ref[...] = lax.dot_general((cv * sg).astype(BF16), dmb, (_DN['tn'], ((), ())), preferred_element_type=F32)
        ds = lax.dot_general(dmb, w_ref[...].astype(BF16), (_DN['nt'], ((), ())), preferred_element_type=F32)
        ds_ref[...] = ds * (sg * (1.0 + cv * (1.0 - sg)))

    return pl.pallas_call(
        body, name="ada_bwd", grid=(DEPTH, nk),
        in_specs=[pl.BlockSpec((16, tk), lambda l, k: (0, k)), pl.BlockSpec((None, tk, NS), lambda l, k: (l, k, 0)),
                  pl.BlockSpec((None, 16, NS), lambda l, k: (l, 0, 0))],
        out_specs=[pl.BlockSpec((None, tk, NS), lambda l, k: (l, k, 0)), pl.BlockSpec((None, 16, tk), lambda l, k: (l, 0, k))],
        out_shape=[jax.ShapeDtypeStruct((DEPTH, D, NS), F32), jax.ShapeDtypeStruct((DEPTH, 16, D), F32)],
        compiler_params=_cp(("parallel", "parallel")))(c9, ada_w, dm9)


def _rope_tables(cf):
    L, LC = cf.L, cf.LC
    rows = L // GRID_W
    row = jnp.repeat(jnp.arange(rows, dtype=F32), GRID_W)
    col = jnp.tile(jnp.arange(GRID_W, dtype=F32), rows)
    nf = HEAD_DIM // 4
    inv = ROPE_THETA ** (-jnp.arange(nf, dtype=F32) / nf)
    ang = jnp.concatenate([row[:, None] * inv, col[:, None] * inv], axis=-1)
    cos, sin = jnp.cos(ang), jnp.sin(ang)
    cs = jnp.concatenate([jnp.ones((LC, HEAD_DIM), F32), jnp.concatenate([cos, cos], -1)], 0)
    sn = jnp.concatenate([jnp.zeros((LC, HEAD_DIM), F32), jnp.concatenate([-sin, sin], -1)], 0)
    return cs, sn


def _prep_tiles(cf, z, cs, sn, key):
    b = cf.OFF[key] // LANE
    return [(z, LANE, _col(b), True), (cs, LANE, _c0, False), (sn, LANE, _c0, False)]


_PREP = {'aq': _f_prep_norm, 'ak': _f_prep_norm, 'rq': _f_prep_plain, 'rk': _f_prep_scaled}


def _prep_fwd(cf, z, cs, sn, key, g):
    nh = cf.W[key] // LANE
    params = [(g, 'shared', True)] if g is not None else []
    return _row_fwd(cf, "prep_fwd_" + key, _PREP[key], _prep_tiles(cf, z, cs, sn, key), params,
                    [(LANE, _col(0), cf.W[key], BF16)], cf.TQ, nrep=nh)[0]


def _prep_bwd(cf, z, cs, sn, key, g, dt):
    nh = cf.W[key] // LANE
    params = [(g, 'shared', True)] if g is not None else []
    tg, pg = _row_bwd(cf, "prep_bwd_" + key, _PREP[key], _prep_tiles(cf, z, cs, sn, key), params,
                      [(dt, LANE, _col(0))], [(LANE, _col(0), cf.W[key], BF16)], cf.TQ, nrep=nh)
    return tg[0], (pg[0] if g is not None else None)


def _gate_params(cf, gup, gb):
    K = gup.shape[-1]
    gf = jnp.zeros((LANE, K), F32).at[0:GLA_RANK].set(gup[0])
    gbm = jnp.zeros((LANE, K), F32).at[GLA_RANK:2 * GLA_RANK].set(gup[1])
    out = []
    for p in range(K // LANE):
        cols = slice(p * LANE, (p + 1) * LANE)
        out += [(gf[:, cols], 'shared', True), (gbm[:, cols], 'shared', True), (gb[0:1, cols], 'shared', True), (gb[1:2, cols], 'shared', True)]
    return out


def _mix_tiles(cf, z, o, key):
    return [(o, LANE, _col(0), True), (z, LANE, _col(cf.OFF[key] // LANE), True)]


def _mid_io(cf, l, W, mod, x, y, norm2_g=None):
    tiles = [(x, cf.D, _c0, True), (y, cf.D, _c0, True)]
    n2 = W['norm2_g'][l] if norm2_g is None else norm2_g
    params = [(mod[2], 'stream', True), (n2, 'shared', True), (mod[3], 'stream', True), (mod[4], 'stream', True)]
    return tiles, params


def _launch_scatter(cf, kind, l, seq, g, nxt):
    g, nxt = lax.optimization_barrier((g, nxt))
    return _seq_scatter(cf, kind, l, seq, g), nxt


class _BigWeights:
    def __init__(self, cf, shards):
        self.cf = cf
        self.whole = {(kind, l): _seq_gather(cf, kind, l, l * len(_KINDS) + n, shards[n])
                      for l in range(DEPTH) for n, kind in enumerate(_KINDS)}
        self.w_in = {}

    def get(self, kind, l, after=None):
        cf = self.cf
        if kind != 'in':
            return self.whole[(kind, l)]
        if l not in self.w_in:
            whole, _ = lax.optimization_barrier((self.whole[(kind, l)], after))
            self.w_in[l] = _unshard_in(cf, "unshard_in_%d" % l, whole)
        return self.w_in[l]


def _layer_fwd(cf, l, W, big, mod, x, h, cs, sn):
    T, D, Fd = cf.T, cf.D, cf.F
    z = _mm("z_%d" % l, h, big.get('in', l, h), 'nn', T, cf.NZ, D, F32, tm=T, tn=768, tk=512)
    qa = _prep_fwd(cf, z, cs, sn, 'aq', W['q_norm_g'][l])
    ka = _prep_fwd(cf, z, cs, sn, 'ak', W['k_norm_g'][l])
    qr = _prep_fwd(cf, z, cs, sn, 'rq', None)
    kr = _prep_fwd(cf, z, cs, sn, 'rk', None)
    o_att = _att_fwd(cf, qa, ka, z)
    o_ret = _ret_fwd(cf, qr, kr, z, W['ret_log_decay'][l])
    gates = _gate_params(cf, W['gla_gate_up'][l], W['gla_gate_b'][l])
    ga_tile = [(z, LANE, _col(cf.OFF['ga'] // LANE), True)]
    we = (cf.HG // 2) * N_DECAY * LANE
    ef, eb = _row_fwd(cf, "gates_fwd_%d" % l, _f_gla_pre, ga_tile, gates, [(we, _c0, we, F32), (we, _c0, we, F32)], GLA_CHUNK)
    o_f, sf, o_b, sb = _gla_fwd(cf, z, ef, eb)
    o_gla = o_f + o_b
    cat_r = _row_fwd(cf, "mixr_fwd_%d" % l, _f_gated_norm, _mix_tiles(cf, z, o_ret, 'rg'), [(W['ret_norm_g'][l], 'shared', True)],
                     [(LANE, _col(0), cf.HR * LANE, BF16)], cf.TQ, nrep=cf.HR)[0]
    cat_g = _row_fwd(cf, "mixg_fwd_%d" % l, _f_gated_norm, _mix_tiles(cf, z, o_gla, 'gr'), [(W['gla_norm_g'][l], 'shared', True)],
                     [(LANE, _col(0), cf.HG * LANE, BF16)], cf.TQ, nrep=cf.HG)[0]
    cat = jnp.concatenate([o_att, cat_r, cat_g], axis=-1)
    y = _mm("y_%d" % l, cat, big.get('out', l, cat), 'nn', T, D, D, F32, tm=T, tn=1024, tk=512)
    tiles, params = _mid_io(cf, l, W, mod, x, y)
    x1, h2 = _row_fwd(cf, "mid_fwd_%d" % l, _f_resid_norm_mod, tiles, params, [(D, _c0, D, F32), (D, _c0, D, BF16)], cf.TM)
    u = _mm("u_%d" % l, h2, big.get('up', l, h2), 'nn', T, 2 * Fd, D, F32, tm=T, tn=1024, tk=512)
    t = _conv_fwd(cf, u, W['conv_wb'][l])
    yff = _mm("yff_%d" % l, t, big.get('down', l, t), 'nn', T, D, Fd, F32, tm=T, tn=1024, tk=512)
    return dict(x=x, h=h, z=z, qa=qa, ka=ka, qr=qr, kr=kr, ef=ef, eb=eb, sf=sf, sb=sb, o_ret=o_ret, o_gla=o_gla, cat=cat, y=y,
                x1=x1, h2=h2, u=u, t=t, yff=yff, gates=gates)


def _layer_bwd(cf, l, W, big, mod, sv, dx1, dyff, cs, sn):
    T, D, Fd = cf.T, cf.D, cf.F
    g, rs = {}, {}
    sq = 2 * len(_KINDS) + (DEPTH - 1 - l) * len(_KINDS)
    gwd = _mm("gwd_%d" % l, sv['t'], dyff, 'tn', Fd, D, T, BF16, tm=1408, tn=2048, tk=1152)
    rs['down'], wb = _launch_scatter(cf, 'down', l, sq, gwd, W['conv_wb'][l])
    dt = _mm("dt_%d" % l, dyff, big.get('down', l), 'nt', T, Fd, D, BF16, tm=T, tn=1408, tk=512)
    da, dv, g['conv_wb'] = _conv_bwd(cf, sv['u'], wb, dt)
    du = jnp.concatenate([da, dv], axis=-1)
    cu = 2 * Fd // N_DEV
    gwu = _mm("gwu_%d" % l, sv['h2'], du, 'tn', D, 2 * Fd, T, BF16, tm=D, tn=cu, tk=1152, out_shape=(N_DEV, D, cu),
              out_spec=pl.BlockSpec((None, D, cu), lambda i, j, k: (j, i, 0)))
    rs['up'], n2 = _launch_scatter(cf, 'up', l, sq + 1, gwu, W['norm2_g'][l])
    dh2 = _mm("dh2_%d" % l, du, big.get('up', l), 'nt', T, D, 2 * Fd, BF16, tm=T, tn=1024, tk=1024)
    tiles, params = _mid_io(cf, l, W, mod, sv['x'], sv['y'], n2)
    (dx, dy), (g['m2'], g['norm2_g'], g['m3'], g['m4']) = _row_bwd(
        cf, "mid_bwd_%d" % l, _f_resid_norm_mod, tiles, params, [(dx1, D, _c0), (dh2, D, _c0)],
        [(D, _c0, D, F32), (D, _c0, D, BF16)], cf.TM)
    gwo = _mm("gwo_%d" % l, sv['cat'], dy, 'tn', D, D, T, BF16, tm=D, tn=1024, tk=1152)
    rs['out'], rn = _launch_scatter(cf, 'out', l, sq + 2, gwo, W['ret_norm_g'][l])
    dcat = _mm("dcat_%d" % l, dy, big.get('out', l), 'nt', T, D, D, BF16, tm=T, tn=1024, tk=1024)
    z = sv['z']
    (do_ret, drg), (g['ret_norm_g'],) = _row_bwd(
        cf, "mixr_bwd_%d" % l, _f_gated_norm, _mix_tiles(cf, z, sv['o_ret'], 'rg'), [(rn, 'shared', True)],
        [(dcat, LANE, _col(cf.HQ))], [(LANE, _col(0), cf.HR * LANE, F32), (LANE, _col(0), cf.HR * LANE, BF16)], cf.TQ, nrep=cf.HR)
    (do_gla, dgr), (g['gla_norm_g'],) = _row_bwd(
        cf, "mixg_bwd_%d" % l, _f_gated_norm, _mix_tiles(cf, z, sv['o_gla'], 'gr'), [(W['gla_norm_g'][l], 'shared', True)],
        [(dcat, LANE, _col(cf.HQ + cf.HR))], [(LANE, _col(0), cf.HG * LANE, F32), (LANE, _col(0), cf.HG * LANE, BF16)], cf.TQ, nrep=cf.HG)
    dqa, dka, dav = _att_bwd(cf, sv['qa'], sv['ka'], z, dcat)
    dqr, dkr, drv, dlg = _ret_bwd(cf, sv['qr'], sv['kr'], z, W['ret_log_decay'][l], do_ret)
    g['ret_log_decay'] = dlg[:, 0:2, 0].T
    dq_f, dk_f, dv_f, def_, dq_b, dk_b, dv_b, deb = _gla_bwd(cf, z, sv['ef'], sv['eb'], sv['sf'], sv['sb'], do_gla)
    dgq, dgk, dgv = dq_f + dq_b, dk_f + dk_b, dv_f + dv_b
    we = (cf.HG // 2) * N_DECAY * LANE
    ga_tile = [(z, LANE, _col(cf.OFF['ga'] // LANE), True)]
    (dga,), gg = _row_bwd(cf, "gates_bwd_%d" % l, _f_gla_pre, ga_tile, sv['gates'],
                          [(def_, we, _c0), (deb, we, _c0)], [(LANE, _c0, LANE, BF16)], GLA_CHUNK)
    ggf, ggb, gbf, gbb = [jnp.concatenate(gg[n::4], axis=-1) for n in range(4)]
    g['gla_gate_up'] = jnp.stack([ggf[0:GLA_RANK], ggb[GLA_RANK:2 * GLA_RANK]])
    g['gla_gate_b'] = jnp.concatenate([gbf, gbb], axis=0)
    daq, g['q_norm_g'] = _prep_bwd(cf, z, cs, sn, 'aq', W['q_norm_g'][l], dqa)
    dak, g['k_norm_g'] = _prep_bwd(cf, z, cs, sn, 'ak', W['k_norm_g'][l], dka)
    drq, _ = _prep_bwd(cf, z, cs, sn, 'rq', None, dqr)
    drk, _ = _prep_bwd(cf, z, cs, sn, 'rk', None, dkr)
    pad = jnp.zeros((T, cf.NZ - cf.OFF['ga'] - LANE), BF16)
    dz = jnp.concatenate([daq, dak, dav.astype(BF16), drq, drk, drv.astype(BF16), drg, dgq.astype(BF16), dgk.astype(BF16),
                          dgv.astype(BF16), dgr, dga, pad], axis=-1)
    gwi = _mm("gwi_%d" % l, sv['h'], dz, 'tn', D, cf.NZ, T, BF16, tm=D, tn=768, tk=1152)
    rs['in'], g['norm1_g_tied'] = _launch_scatter(cf, 'in', l, sq + 3, _slabs_in(cf, "slabs_in_%d" % l, gwi), W['norm1_g'][l])
    dh =_mm("dh_%d" % l, dz, big.get('in', l), 'nt', T, D, cf.NZ, BF16, tm=T, tn=1024, tk=768)
    g['rs'] = rs
    return dx, dh, g


_WEIGHTS = ['c_ctx', 'ada_w', 'ada_b', 'norm1_g', 'w_in', 'q_norm_g', 'k_norm_g', 'ret_log_decay', 'ret_norm_g',
            'gla_gate_up', 'gla_gate_b', 'gla_norm_g', 'w_out', 'norm2_g', 'w_up', 'conv_w', 'conv_b', 'w_down', 'final_norm_g']
_BIG = ['w_in', 'w_out', 'w_up', 'w_down']
_SMALL = [n for n in _WEIGHTS if n not in _BIG and n != 'ada_w']
_COL_SHARDED = ['gla_gate_up', 'gla_gate_b', 'conv_w']


def _pack(arrs):
    flat = jnp.concatenate([a.reshape(-1) for a in arrs])
    n = flat.shape[0]
    tot = -(-n // (8 * LANE)) * (8 * LANE)
    return jnp.pad(flat, (0, tot - n)).reshape(tot // LANE, LANE)


def _unpack(flat, shapes):
    out, o = [], 0
    for s in shapes:
        n = int(np.prod(s))
        out.append(flat[..., o:o + n].reshape(flat.shape[:-1] + tuple(s)))
        o += n
    return out


def _unshard_last(a):
    return jnp.moveaxis(a, 0, -2).reshape(a.shape[1:-1] + (N_DEV * a.shape[-1],))


def _step(cf, x, c, ctx, loss_target, w, m, v):
    T, D, Fd, L, LC = cf.T, cf.D, cf.F, cf.L, cf.LC
    _, _, _, me = _me()
    NS = w['ada_w'].shape[-1]

    c_all = _ag_small("ag_c", jnp.pad(c, ((0, 7), (0, 0))))[:, 0, :]
    c9 = jnp.concatenate([c_all, w['c_ctx'][None], jnp.zeros((7, D), F32)], axis=0)
    pm = _ada_fwd(cf, c9, w['ada_w'])
    pm_all = _ag_small("ag_mod", pm.reshape(DEPTH * 16, NS)).reshape(N_DEV, DEPTH, 16, NS)
    mod_all = _unshard_last(pm_all) + w['ada_b'][:, None, :]
    mod_own = lax.dynamic_index_in_dim(mod_all, me, axis=1, keepdims=False)
    mods = []
    for l in range(DEPTH):
        mods.append([jnp.stack([mod_all[l, 8, k * D:(k + 1) * D], mod_own[l, k * D:(k + 1) * D]])[:, None, :] for k in range(N_MOD)])

    shard_shapes = [w[n].shape for n in _COL_SHARDED]
    got = _ag_small("ag_smallw", _pack([w[n] for n in _COL_SHARDED]))
    full = dict(zip(_COL_SHARDED, [_unshard_last(a) for a in _unpack(got.reshape(N_DEV, -1), shard_shapes)]))

    small_done = full['conv_w'] + mod_all[0, 0, 0]
    big = _BigWeights(cf, [_cast_bf16("cast_" + n, w[n], c if n == 'w_in' else small_done) for n in _BIG])
    conv_wb = jnp.concatenate([full['conv_w'], w['conv_b'][:, None, :], jnp.zeros((DEPTH, 4, Fd), F32)], axis=1)
    W = dict(conv_wb=conv_wb, gla_gate_up=full['gla_gate_up'], gla_gate_b=full['gla_gate_b'], ret_log_decay=w['ret_log_decay'])
    for n in ['q_norm_g', 'k_norm_g', 'ret_norm_g', 'gla_norm_g', 'norm1_g', 'norm2_g']:
        W[n] = w[n][:, None, :]

    cs, sn = _rope_tables(cf)
    x0 = jnp.concatenate([ctx[0], x[0]], axis=0)
    pre_tiles = [(x0, D, _c0, True)]

    def pre_params(n1):
        return [(n1, 'shared', True), (mods[0][0], 'stream', True), (mods[0][1], 'stream', True)]

    def tr_params(n1):
        return [(mods[0][5], 'stream', True), (n1, 'shared', True), (mods[1][0], 'stream', True), (mods[1][1], 'stream', True)]

    h0 = _row_fwd(cf, "pre_fwd", _f_norm_mod, pre_tiles, pre_params(W['norm1_g'][0]), [(D, _c0, D, BF16)], cf.TM)[0]
    sv0 = _layer_fwd(cf, 0, W, big, mods[0], x0, h0, cs, sn)
    tr_tiles = [(sv0['x1'], D, _c0, True), (sv0['yff'], D, _c0, True)]
    xb, hb = _row_fwd(cf, "tr_fwd", _f_resid_norm_mod, tr_tiles, tr_params(W['norm1_g'][1]), [(D, _c0, D, F32), (D, _c0, D, BF16)], cf.TM)
    sv1 = _layer_fwd(cf, 1, W, big, mods[1], xb, hb, cs, sn)
    tgt = jnp.concatenate([jnp.zeros((LC, D), F32), loss_target[0]], axis=0)
    dx1, dyff, dm5_1, g_final, ls = _loss_grad(cf, sv1['x1'], sv1['yff'], mods[1][5], w['final_norm_g'][None], tgt)
    loss = lax.psum(ls[0, 0], ("x", "y", "c"))

    dxb, dhb, g1 = _layer_bwd(cf, 1, W, big, mods[1], sv1, dx1, dyff, cs, sn)
    (dx1_0, dyff_0), (dm5_0, gn1_1, dm0_1, dm1_1) = _row_bwd(
        cf, "tr_bwd", _f_resid_norm_mod, tr_tiles, tr_params(g1['norm1_g_tied']), [(dxb, D, _c0), (dhb, D, _c0)],
        [(D, _c0, D, F32), (D, _c0, D, BF16)], cf.TM)
    dx0, dh0, g0 = _layer_bwd(cf, 0, W, big, mods[0], sv0, dx1_0, dyff_0, cs, sn)
    (dxa,), (gn1_0, dm0_0, dm1_0) = _row_bwd(cf, "pre_bwd", _f_first, pre_tiles, pre_params(g0['norm1_g_tied']), [(dx0, D, _c0), (dh0, D, _c0)],
                                            [(D, _c0, D, F32)], cf.TM)
    grad_x = dxa[LC:][None]

    dmod = jnp.stack([jnp.concatenate([dm0_0, dm1_0, g0['m2'], g0['m3'], g0['m4'], dm5_0], axis=-1)[:, 0],
                      jnp.concatenate([dm0_1, dm1_1, g1['m2'], g1['m3'], g1['m4'], dm5_1], axis=-1)[:, 0]])
    dm_all = _ag_small("ag_dmod", jnp.pad(dmod.reshape(2 * DEPTH, N_MOD * D), ((0, 8 - 2 * DEPTH), (0, 0))))
    dm_all = dm_all[:, :2 * DEPTH].reshape(N_DEV, DEPTH, 2, N_MOD * D)
    dctx = _sum8("sum_dmodc", jnp.pad(dm_all[:, :, 0], ((0, 0), (0, 8 - DEPTH), (0, 0))))[:DEPTH]
    dm9 = jnp.concatenate([jnp.moveaxis(dm_all[:, :, 1], 0, 1), dctx[:, None]], axis=1)
    g_ada_b = _sum8("sum_adab", jnp.pad(jnp.moveaxis(dm9, 1, 0), ((0, 0), (0, 8 - DEPTH), (0, 0))))[:DEPTH]
    dm9s = lax.dynamic_slice_in_dim(jnp.pad(dm9, ((0, 0), (0, 7), (0, 0))), me * NS, NS, axis=2)
    g_ada_w, dsil = _ada_bwd(cf, c9, w['ada_w'], dm9s)
    g_cctx_part = dsil[0, 8]
    for l in range(1, DEPTH):
        g_cctx_part = g_cctx_part + dsil[l, 8]

    def both(key):
        return jnp.stack([g0[key], g1[key]])

    gsmall = dict(c_ctx=g_cctx_part, norm1_g=jnp.stack([gn1_0[0], gn1_1[0]]), q_norm_g=both('q_norm_g')[:, 0],
                  k_norm_g=both('k_norm_g')[:, 0], ret_log_decay=both('ret_log_decay'), ret_norm_g=both('ret_norm_g')[:, 0],
                  gla_gate_up=both('gla_gate_up'), gla_gate_b=both('gla_gate_b'), gla_norm_g=both('gla_norm_g')[:, 0],
                  norm2_g=both('norm2_g')[:, 0], conv_w=both('conv_wb')[:, 0:3], conv_b=both('conv_wb')[:, 3], final_norm_g=g_final[0])
    snames = [n for n in _SMALL if n != 'ada_b']
    sshapes = [gsmall[n].shape for n in snames]
    gs_all = _ag_small("ag_gsmall", _pack([gsmall[n] for n in snames]))
    gs = dict(zip(snames, _unpack(_sum8("sum_gsmall", gs_all).reshape(-1), sshapes)))
    gs['ada_b'] = g_ada_b
    for n in _COL_SHARDED:
        ns_ = w[n].shape[-1]
        gs[n] = lax.dynamic_slice_in_dim(gs[n], me * ns_, ns_, axis=gs[n].ndim - 1)

    out_g, out_d, out_m, out_v = {}, {}, {}, {}

    after, done = gs_all, {}
    for l, gl in ((1, g1), (0, g0)):
        for kind in reversed(_KINDS):
            n = 'w_' + kind
            done[n] = _sum_adam("adam_%s_%d" % (n, l), gl['rs'][kind], w[n], m[n], v[n], l, done.get(n), after)
            after = done[n][0]
    for n in _BIG:
        out_g[n], out_d[n], out_m[n], out_v[n] = done[n]
    aw = [a.reshape(DEPTH * D, NS) for a in (w['ada_w'], g_ada_w, m['ada_w'], v['ada_w'])]
    out_g['ada_w'] = g_ada_w
    out_d['ada_w'], out_m['ada_w'], out_v['ada_w'] = [a.reshape(DEPTH, D, NS) for a in _adam("adam_ada_w", *aw)]
    shp = [w[n].shape for n in _SMALL]
    packed = [_pack([src[n] for n in _SMALL]) for src in (w, gs, m, v)]
    res = _adam("adam_small", *packed)
    for dst, pk in zip((out_d, out_m, out_v), res):
        dst.update(zip(_SMALL, _unpack(pk.reshape(-1), shp)))
    out_g.update({n: gs[n] for n in _SMALL})
    return (loss, grad_x, *[out_g[n] for n in _WEIGHTS], *[out_d[n] for n in _WEIGHTS], *[out_m[n] for n in _WEIGHTS],
            *[out_v[n] for n in _WEIGHTS])


def kernel(x, c, ctx, c_ctx, ada_w, ada_b, norm1_g, w_in, q_norm_g, k_norm_g, ret_log_decay, ret_norm_g, gla_gate_up, gla_gate_b, gla_norm_g, w_out, norm2_g, w_up, conv_w, conv_b, w_down, final_norm_g, loss_target, m_c_ctx, m_ada_w, m_ada_b, m_norm1_g, m_w_in, m_q_norm_g, m_k_norm_g, m_ret_log_decay, m_ret_norm_g, m_gla_gate_up, m_gla_gate_b, m_gla_norm_g, m_w_out, m_norm2_g, m_w_up, m_conv_w, m_conv_b, m_w_down, m_final_norm_g, v_c_ctx, v_ada_w, v_ada_b, v_norm1_g, v_w_in, v_q_norm_g, v_k_norm_g, v_ret_log_decay, v_ret_norm_g, v_gla_gate_up, v_gla_gate_b, v_gla_norm_g, v_w_out, v_norm2_g, v_w_up, v_conv_w, v_conv_b, v_w_down, v_final_norm_g):
    w = dict(c_ctx=c_ctx, ada_w=ada_w, ada_b=ada_b, norm1_g=norm1_g, w_in=w_in, q_norm_g=q_norm_g, k_norm_g=k_norm_g,
             ret_log_decay=ret_log_decay, ret_norm_g=ret_norm_g, gla_gate_up=gla_gate_up, gla_gate_b=gla_gate_b,
             gla_norm_g=gla_norm_g, w_out=w_out, norm2_g=norm2_g, w_up=w_up, conv_w=conv_w, conv_b=conv_b, w_down=w_down,
             final_norm_g=final_norm_g)
    m = dict(c_ctx=m_c_ctx, ada_w=m_ada_w, ada_b=m_ada_b, norm1_g=m_norm1_g, w_in=m_w_in, q_norm_g=m_q_norm_g,
             k_norm_g=m_k_norm_g, ret_log_decay=m_ret_log_decay, ret_norm_g=m_ret_norm_g, gla_gate_up=m_gla_gate_up,
             gla_gate_b=m_gla_gate_b, gla_norm_g=m_gla_norm_g, w_out=m_w_out, norm2_g=m_norm2_g, w_up=m_w_up,
             conv_w=m_conv_w, conv_b=m_conv_b, w_down=m_w_down, final_norm_g=m_final_norm_g)
    v = dict(c_ctx=v_c_ctx, ada_w=v_ada_w, ada_b=v_ada_b, norm1_g=v_norm1_g, w_in=v_w_in, q_norm_g=v_q_norm_g,
             k_norm_g=v_k_norm_g, ret_log_decay=v_ret_log_decay, ret_norm_g=v_ret_norm_g, gla_gate_up=v_gla_gate_up,
             gla_gate_b=v_gla_gate_b, gla_norm_g=v_gla_norm_g, w_out=v_w_out, norm2_g=v_norm2_g, w_up=v_w_up,
             conv_w=v_conv_w, conv_b=v_conv_b, w_down=v_w_down, final_norm_g=v_final_norm_g)
    return _step(_cfg(), x, c, ctx, loss_target, w, m, v)
```

```python
import functools
import math
import types

import jax
import jax.numpy as jnp
import numpy as np
from jax import lax
from jax.experimental import pallas as pl
from jax.experimental.pallas import tpu as pltpu
from jax.experimental.pallas import tpu_sc as plsc

F32 = jnp.float32
BF16 = jnp.bfloat16

D_MODEL = 2048
SEQ = 2048
CTX_LEN = 256
GRID_W = 64
D_FF = 5632
DEPTH = 2
N_DEV = 8
HEAD_DIM = 128
ROPE_THETA = 10000.0
GLA_TAU = 16.0
GLA_RANK = 16
GLA_CHUNK = 64
GLA_SUB = 16
EPS = 1e-6
N_MOD = 6
ADAM_LR = 0.001
ADAM_B1 = 0.9
ADAM_B2 = 0.999
ADAM_EPS = 1e-08
ADAM_WD = 0.01
ADAM_STEP = 10
LANE = 128
VMEM_LIMIT = 56 * 1024 * 1024
NEG = -1e30


def _cfg():
    d = types.SimpleNamespace()
    d.D, d.L, d.LC, d.F = D_MODEL, SEQ, CTX_LEN, D_FF
    d.T = d.L + d.LC
    nm = d.D // HEAD_DIM
    d.HQ, d.HKV, d.HR, d.HG = nm // 2, nm // 8, nm // 4, nm // 4
    d.G = d.HQ // d.HKV
    w = dict(aq=d.HQ * 128, ak=d.HKV * 128, av=d.HKV * 128, rq=d.HR * 128, rk=d.HR * 128, rv=d.HR * 128,
             rg=d.HR * 128, gq=d.HG * 64, gk=d.HG * 64, gv=d.HG * 128, gr=d.HG * 128, ga=2 * GLA_RANK)
    off, o = {}, 0
    for k, v in w.items():
        off[k] = o
        o += v
    d.W, d.OFF, d.NIN = w, off, o
    d.NZ = -(-(off['ga'] + LANE) // 256) * 256
    d.NINS = d.NIN // N_DEV
    d.TM = math.gcd(d.LC, 128)
    d.TQ = math.gcd(d.LC, 256)
    return d


def _cp(sem=None):
    return pltpu.CompilerParams(dimension_semantics=sem, vmem_limit_bytes=VMEM_LIMIT)


def _tile(n, target, mult=LANE):
    t = min(n, target)
    t -= t % mult
    while t > mult and n % t:
        t -= mult
    return t if t > 0 and n % t == 0 else n


_DN = {'nn': ((1,), (0,)), 'nt': ((1,), (1,)), 'tn': ((0,), (0,))}


def _mm(name, a, b, kind, M, N, K, out_dtype, tm=768, tn=768, tk=1024, a_spec=None, b_spec=None,
        out_shape=None, out_spec=None):
    tm, tn = _tile(M, tm, 128), _tile(N, tn, 128)
    tk = _tile(K, tk, 128)
    nk = K // tk

    def dot(a_ref, b_ref):
        return lax.dot_general(a_ref[...].astype(BF16), b_ref[...].astype(BF16), (_DN[kind], ((), ())), preferred_element_type=F32)

    def body_one(a_ref, b_ref, o_ref):
        o_ref[...] = dot(a_ref, b_ref).astype(o_ref.dtype)

    def body(a_ref, b_ref, o_ref, acc):
        kk = pl.program_id(2)

        @pl.when(kk == 0)
        def _():
            acc[...] = jnp.zeros_like(acc)

        acc[...] += dot(a_ref, b_ref)

        @pl.when(kk == nk - 1)
        def _():
            o_ref[...] = acc[...].astype(o_ref.dtype)

    if a_spec is None:
        a_spec = pl.BlockSpec((tk, tm), lambda i, j, k: (k, i)) if kind == 'tn' else pl.BlockSpec((tm, tk), lambda i, j, k: (i, k))
    if b_spec is None:
        b_spec = pl.BlockSpec((tn, tk), lambda i, j, k: (j, k)) if kind == 'nt' else pl.BlockSpec((tk, tn), lambda i, j, k: (k, j))
    if out_spec is None:
        out_spec = pl.BlockSpec((tm, tn), lambda i, j, k: (i, j))
        out_shape = (M, N)
    return pl.pallas_call(
        body_one if nk == 1 else body, name=name, grid=(M // tm, N // tn, nk), in_specs=[a_spec, b_spec], out_specs=out_spec,
        out_shape=jax.ShapeDtypeStruct(out_shape, out_dtype), scratch_shapes=[] if nk == 1 else [pltpu.VMEM((tm, tn), F32)],
        compiler_params=_cp(("parallel", "parallel", "arbitrary")))(a, b)


def _tile_spec(tm, w, colf, nrep):
    if hasattr(colf, 'base'):
        assert colf.base % nrep == 0
        return pl.BlockSpec((tm, w * nrep), functools.partial(lambda i, b: (i, b), b=colf.base // nrep))
    return pl.BlockSpec((tm, w), lambda i: (i, 0))


def _head_cols(colf, w, r):
    return slice(r * w, (r + 1) * w) if hasattr(colf, 'base') else slice(None)


def _row_specs(cf, tm, tiles, params, nrep):
    nctx = cf.LC // tm
    specs = [_tile_spec(tm, w, colf, nrep) for arr, w, colf, _ in tiles]
    for arr, kind, _ in params:
        nd = arr.ndim
        if kind == 'shared':
            specs.append(pl.BlockSpec(arr.shape, functools.partial(lambda i, nd: (0,) * nd, nd=nd)))
        else:
            specs.append(pl.BlockSpec((None,) + arr.shape[1:],
                                      functools.partial(lambda i, nd, nctx: (jnp.where(i >= nctx, 1, 0),) + (0,) * (nd - 1), nd=nd, nctx=nctx)))
    return specs


def _row_fwd(cf, name, f, tiles, params, outs, tm, nrep=1):
    nt, npar = len(tiles), len(params)

    def body(*refs):
        pv = [r[...] for r in refs[nt:nt + npar]]
        for r in range(nrep):
            tv = [x[:, _head_cols(t[2], t[1], r)].astype(F32) for x, t in zip(refs[:nt], tiles)]
            res = f(*tv, *pv)
            for o, v, spec in zip(refs[nt + npar:], res, outs):
                o[:, _head_cols(spec[1], spec[0], r)] = v.astype(o.dtype)

    out_specs = [_tile_spec(tm, w, colf, nrep) for w, colf, _, _ in outs]
    out_shape = [jax.ShapeDtypeStruct((cf.T, tw), dt) for _, _, tw, dt in outs]
    return pl.pallas_call(
        body, name=name, grid=(cf.T // tm,), in_specs=_row_specs(cf, tm, tiles, params, nrep), out_specs=out_specs,
        out_shape=out_shape, compiler_params=_cp(("arbitrary",)))(*[t[0] for t in tiles], *[p[0] for p in params])


def _row_bwd(cf, name, f, tiles, params, cts, tgrads, tm, nrep=1):
    nt, npar, nc = len(tiles), len(params), len(cts)
    tdiff = [k for k, t in enumerate(tiles) if t[3]]
    pdiff = [k for k, p in enumerate(params) if p[2]]
    nctx = cf.LC // tm

    def body(*refs):
        i = pl.program_id(0)
        pv = [x[...] for x in refs[nt:nt + npar]]
        outs = refs[nt + npar + nc:]
        psum = None
        for r in range(nrep):
            tv = [x[:, _head_cols(t[2], t[1], r)].astype(F32) for x, t in zip(refs[:nt], tiles)]
            cv = tuple(x[:, _head_cols(c[2], c[1], r)].astype(F32) for x, c in zip(refs[nt + npar:nt + npar + nc], cts))

            def g(*diff, tv=tv):
                tv2, pv2 = list(tv), list(pv)
                for k, v in zip(tdiff, diff[:len(tdiff)]):
                    tv2[k] = v
                for k, v in zip(pdiff, diff[len(tdiff):]):
                    pv2[k] = v
                return tuple(f(*tv2, *pv2))

            _, vjp_fn = jax.vjp(g, *[tv[k] for k in tdiff], *[pv[k] for k in pdiff])
            grads = vjp_fn(cv)
            for o, gv, spec in zip(outs[:len(tdiff)], grads[:len(tdiff)], tgrads):
                o[:, _head_cols(spec[1], spec[0], r)] = gv.astype(o.dtype)
            pg = grads[len(tdiff):]
            psum = list(pg) if psum is None else [a + b for a, b in zip(psum, pg)]
        for n_, (o, gv) in enumerate(zip(outs[len(tdiff):], psum)):
            first = (i == 0) if params[pdiff[n_]][1] == 'shared' else jnp.logical_or(i == 0, i == nctx)

            @pl.when(first)
            def _():
                o[...] = gv

            @pl.when(jnp.logical_not(first))
            def _():
                o[...] += gv

    in_specs = _row_specs(cf, tm, tiles, params, nrep)
    in_specs += [_tile_spec(tm, w, colf, nrep) for _, w, colf in cts]
    out_specs = [_tile_spec(tm, w, colf, nrep) for w, colf, _, _ in tgrads]
    out_shape = [jax.ShapeDtypeStruct((cf.T, tw), dt) for _, _, tw, dt in tgrads]
    out_specs += _row_specs(cf, tm, [], [params[k] for k in pdiff], nrep)
    out_shape += [jax.ShapeDtypeStruct(params[k][0].shape, F32) for k in pdiff]
    res = pl.pallas_call(
        body, name=name, grid=(cf.T // tm,), in_specs=in_specs, out_specs=out_specs, out_shape=out_shape,
        compiler_params=_cp(("arbitrary",)))(*[t[0] for t in tiles], *[p[0] for p in params], *[c[0] for c in cts])
    return res[:len(tdiff)], res[len(tdiff):]


def _c0(r):
    return 0


def _col(base):
    def col(r):
        return base + r
    col.base = base
    return col


def _rms(x, g):
    return x * lax.rsqrt(jnp.mean(x * x, axis=-1, keepdims=True) + EPS) * g


def _sigmoid(x):
    return 1.0 / (1.0 + jnp.exp(-x))


def _silu(x):
    return x * _sigmoid(x)


def _f_norm_mod(x, g, sh, sc):
    return (_rms(x, g) * (1 + sc) + sh,)


def _f_resid_norm_mod(x, y, gate, g, sh, sc):
    x1 = x + gate * y
    return (x1, _rms(x1, g) * (1 + sc) + sh)


@jax.custom_vjp
def _swap_halves(t):
    return pltpu.roll(t, HEAD_DIM // 2, axis=1)


def _swap_fwd(t):
    return _swap_halves(t), None


def _swap_bwd(_, g):
    return (pltpu.roll(g, HEAD_DIM // 2, axis=1),)


_swap_halves.defvjp(_swap_fwd, _swap_bwd)


def _rope(t, cs, sn):
    return t * cs + _swap_halves(t) * sn


def _f_prep_norm(t, cs, sn, g):
    return (_rope(_rms(t, g), cs, sn),)


def _f_prep_plain(t, cs, sn):
    return (_rope(t, cs, sn),)


def _f_prep_scaled(t, cs, sn):
    return (_rope(t * (HEAD_DIM ** -0.5), cs, sn),)


def _log_sigmoid(x):
    return jnp.minimum(x, 0.0) - jnp.log(1.0 + jnp.exp(-jnp.abs(x)))


N_DECAY = 8


def _gla_masks(d, width):
    C, SB = GLA_CHUNK, GLA_SUB
    r = lax.broadcasted_iota(jnp.int32, (C, C), 0)
    m = lax.broadcasted_iota(jnp.int32, (C, C), 1)
    rr = lax.broadcasted_iota(jnp.int32, (C, width), 0)
    allowed = (m <= r) if d == 0 else (m >= r)
    blocks, vis = [allowed], []
    for b in range(C // SB):
        blocks.append((m < SB * b) if d == 0 else (m >= SB * (b + 1)))
        vis.append((rr < SB * (b + 1)) if d == 0 else (rr >= SB * b))
    cm = jnp.concatenate([x.astype(F32) for x in blocks] + [jnp.ones((C, C), F32)], axis=0)
    return cm, allowed, vis


def _gla_decays(la, d):
    C, SB = GLA_CHUNK, GLA_SUB
    nsb = C // SB
    cm, _, vis = _gla_masks(d, la.shape[-1])
    cums = jnp.dot(cm, la, precision=lax.Precision.HIGH, preferred_element_type=F32)
    cum, tot = cums[0:C], cums[(1 + nsb) * C:]
    refs = [cums[(1 + b) * C:(2 + b) * C] for b in range(nsb)]
    e1 = jnp.concatenate([jnp.exp(cum[b * SB:(b + 1) * SB] - refs[b][b * SB:(b + 1) * SB]) for b in range(nsb)], axis=0)
    e2 = [jnp.where(vis[b], jnp.exp(jnp.where(vis[b], refs[b] - cum, 0.0)), 0.0) for b in range(nsb)]
    return [e1] + e2 + [jnp.exp(cum), jnp.exp(tot - cum), jnp.exp(tot)]


def _f_gla_pre(ga, *per_pair):
    gab = ga.astype(BF16)
    outs = [[], []]
    for p in range(len(per_pair) // 4):
        gf, gb, bf, bb = per_pair[4 * p:4 * p + 4]
        for d, (gm, bm) in enumerate(((gf, bf), (gb, bb))):
            la = _log_sigmoid(jnp.dot(gab, gm.astype(BF16), preferred_element_type=F32) + bm) / GLA_TAU
            outs[d] += _gla_decays(la, d)
    return tuple(jnp.concatenate(o, axis=-1) for o in outs)


def _f_gated_norm(o, g, n):
    return (_rms(o, n) * _silu(g),)


def _f_first(x, g, sh, sc):
    return (x, _rms(x, g) * (1 + sc) + sh)


def _loss_grad(cf, x1, yff, gate, gfin, tgt):
    tm, T, D = cf.TM, cf.T, cf.D
    nctx = cf.LC // tm

    def lossf(x1v, yv, gt, gf, tg):
        y = _rms(x1v + gt * yv, gf)
        e = y - tg
        return 0.5 * jnp.sum(jnp.mean(e * e, axis=-1, keepdims=True), axis=0, keepdims=True)

    def body(x1_ref, y_ref, gt_ref, gf_ref, tg_ref, dx_ref, dy_ref, dgt_ref, dgf_ref, ls_ref):
        i = pl.program_id(0)
        lat = (i >= nctx).astype(F32)
        val, vjp_fn = jax.vjp(lossf, x1_ref[...], y_ref[...].astype(F32), gt_ref[...], gf_ref[...], tg_ref[...])
        dx, dy, dgt, dgf, _ = vjp_fn(jnp.ones((1, 1), F32) * lat)
        dx_ref[...] = dx
        dy_ref[...] = dy.astype(dy_ref.dtype)
        first_s = jnp.logical_or(i == 0, i == nctx)

        @pl.when(first_s)
        def _():
            dgt_ref[...] = dgt

        @pl.when(jnp.logical_not(first_s))
        def _():
            dgt_ref[...] += dgt

        @pl.when(i == 0)
        def _():
            dgf_ref[...] = dgf
            ls_ref[...] = jnp.zeros_like(ls_ref) + val * lat

        @pl.when(i != 0)
        def _():
            dgf_ref[...] += dgf
            ls_ref[...] += val * lat

    row = pl.BlockSpec((tm, D), lambda i: (i, 0))
    strm = pl.BlockSpec((None, 1, D), lambda i: (jnp.where(i >= nctx, 1, 0), 0, 0))
    one = pl.BlockSpec((1, D), lambda i: (0, 0))
    return pl.pallas_call(
        body, name="loss_grad", grid=(T // tm,), in_specs=[row, row, strm, one, row],
        out_specs=[row, row, strm, one, pl.BlockSpec((8, LANE), lambda i: (0, 0))],
        out_shape=[jax.ShapeDtypeStruct((T, D), F32), jax.ShapeDtypeStruct((T, D), BF16),
                   jax.ShapeDtypeStruct((2, 1, D), F32), jax.ShapeDtypeStruct((1, D), F32),
                   jax.ShapeDtypeStruct((8, LANE), F32)],
        compiler_params=_cp(("arbitrary",)))(x1, yff, gate, gfin, tgt)


def _att_mask(cf, i, tq):
    col = lax.broadcasted_iota(jnp.int32, (tq, cf.T), 1)
    return jnp.logical_or(col < cf.LC, i >= cf.LC // tq)


def _att_probs(q, k, mask):
    s = lax.dot_general(q, k, (_DN['nt'], ((), ())), preferred_element_type=F32) * (HEAD_DIM ** -0.5)
    s = jnp.where(mask, s, NEG)
    e = jnp.exp(s - jnp.max(s, axis=-1, keepdims=True))
    return e / jnp.sum(e, axis=-1, keepdims=True)


def _att_fwd(cf, q, k, z):
    tq, T, G = cf.TQ, cf.T, cf.G
    vb = cf.OFF['av'] // LANE

    def body(q_ref, k_ref, v_ref, o_ref):
        mask = _att_mask(cf, pl.program_id(1), tq)
        kv, vv = k_ref[...], v_ref[...].astype(BF16)
        for j in range(G):
            p = _att_probs(q_ref[:, j * LANE:(j + 1) * LANE], kv, mask)
            o_ref[:, j * LANE:(j + 1) * LANE] = jnp.dot(p.astype(BF16), vv, preferred_element_type=F32).astype(o_ref.dtype)

    return pl.pallas_call(
        body, name="att_fwd", grid=(cf.HKV, T // tq),
        in_specs=[pl.BlockSpec((tq, G * LANE), lambda g, i: (i, g)), pl.BlockSpec((T, LANE), lambda g, i: (0, g)),
                  pl.BlockSpec((T, LANE), lambda g, i: (0, vb + g))],
        out_specs=pl.BlockSpec((tq, G * LANE), lambda g, i: (i, g)),
        out_shape=jax.ShapeDtypeStruct((T, cf.HQ * LANE), BF16), compiler_params=_cp(("arbitrary", "arbitrary")))(q, k, z)


def _att_bwd(cf, q, k, z, dcat):
    tq, T, G = cf.TQ, cf.T, cf.G
    vb = cf.OFF['av'] // LANE
    sc = HEAD_DIM ** -0.5

    def body(q_ref, k_ref, v_ref, do_ref, dq_ref, dk_ref, dv_ref):
        i = pl.program_id(1)
        mask = _att_mask(cf, i, tq)
        kv, vv = k_ref[...], v_ref[...].astype(BF16)
        dk = jnp.zeros((T, LANE), F32)
        dv = jnp.zeros((T, LANE), F32)
        for j in range(G):
            qj = q_ref[:, j * LANE:(j + 1) * LANE]
            do = do_ref[:, j * LANE:(j + 1) * LANE]
            p = _att_probs(qj, kv, mask)
            dv += lax.dot_general(p.astype(BF16), do, (_DN['tn'], ((), ())), preferred_element_type=F32)
            dp = lax.dot_general(do, vv, (_DN['nt'], ((), ())), preferred_element_type=F32)
            ds = p * (dp - jnp.sum(dp * p, axis=-1, keepdims=True)) * sc
            dsb = ds.astype(BF16)
            dq_ref[:, j * LANE:(j + 1) * LANE] = jnp.dot(dsb, kv, preferred_element_type=F32)
            dk += lax.dot_general(dsb, qj, (_DN['tn'], ((), ())), preferred_element_type=F32)

        @pl.when(i == 0)
        def _():
            dk_ref[...] = dk
            dv_ref[...] = dv

        @pl.when(i != 0)
        def _():
            dk_ref[...] += dk
            dv_ref[...] += dv

    qs = pl.BlockSpec((tq, G * LANE), lambda g, i: (i, g))
    ks = pl.BlockSpec((T, LANE), lambda g, i: (0, g))
    return pl.pallas_call(
        body, name="att_bwd", grid=(cf.HKV, T // tq),
        in_specs=[qs, ks, pl.BlockSpec((T, LANE), lambda g, i: (0, vb + g)), qs],
        out_specs=[qs, ks, ks],
        out_shape=[jax.ShapeDtypeStruct((T, cf.HQ * LANE), F32), jax.ShapeDtypeStruct((T, cf.HKV * LANE), F32),
                   jax.ShapeDtypeStruct((T, cf.HKV * LANE), F32)],
        compiler_params=_cp(("arbitrary", "arbitrary")))(q, k, z, dcat)


def _ret_masks(cf, i, tq, lgf, lgb):
    T, LC = cf.T, cf.LC
    row = lax.broadcasted_iota(jnp.int32, (tq, T), 0) + i * tq
    col = lax.broadcasted_iota(jnp.int32, (tq, T), 1)

    def pb(n):
        return jnp.where(n < LC, LC - 1 - n, T + LC - 1 - n)

    relf = row - col
    relb = pb(row) - pb(col)
    okf, okb = relf >= 0, relb >= 0
    rf = jnp.where(okf, relf, 0).astype(F32)
    rb = jnp.where(okb, relb, 0).astype(F32)
    mf = jnp.where(okf, jnp.exp(lgf * rf), 0.0)
    mb = jnp.where(okb, jnp.exp(lgb * rb), 0.0)
    return mf, mb, rf, rb


def _ret_fwd(cf, q, k, z, lg):
    tq, T = cf.TQ, cf.T
    vb = cf.OFF['rv'] // LANE

    def body(lg_ref, q_ref, k_ref, v_ref, o_ref):
        h, i = pl.program_id(0), pl.program_id(1)
        mf, mb, _, _ = _ret_masks(cf, i, tq, lg_ref[0, h], lg_ref[1, h])
        a = lax.dot_general(q_ref[...], k_ref[...], (_DN['nt'], ((), ())), preferred_element_type=F32)
        p = (a * (mf + mb)).astype(BF16)
        o_ref[...] = jnp.dot(p, v_ref[...].astype(BF16), preferred_element_type=F32)

    return pl.pallas_call(
        body, name="ret_fwd", grid=(cf.HR, T // tq),
        in_specs=[pl.BlockSpec(memory_space=pltpu.SMEM), pl.BlockSpec((tq, LANE), lambda h, i: (i, h)),
                  pl.BlockSpec((T, LANE), lambda h, i: (0, h)), pl.BlockSpec((T, LANE), lambda h, i: (0, vb + h))],
        out_specs=pl.BlockSpec((tq, LANE), lambda h, i: (i, h)),
        out_shape=jax.ShapeDtypeStruct((T, cf.HR * LANE), F32), compiler_params=_cp(("arbitrary", "arbitrary")))(lg, q, k, z)


def _ret_bwd(cf, q, k, z, lg, do):
    tq, T = cf.TQ, cf.T
    vb = cf.OFF['rv'] // LANE

    def body(lg_ref, q_ref, k_ref, v_ref, do_ref, dq_ref, dk_ref, dv_ref, dlg_ref):
        h, i = pl.program_id(0), pl.program_id(1)
        mf, mb, rf, rb = _ret_masks(cf, i, tq, lg_ref[0, h], lg_ref[1, h])
        qv, kv, vv = q_ref[...], k_ref[...], v_ref[...].astype(BF16)
        dob = do_ref[...].astype(BF16)
        a = lax.dot_general(qv, kv, (_DN['nt'], ((), ())), preferred_element_type=F32)
        m = mf + mb
        p = (a * m).astype(BF16)
        dv = lax.dot_general(p, dob, (_DN['tn'], ((), ())), preferred_element_type=F32)
        dp = lax.dot_general(dob, vv, (_DN['nt'], ((), ())), preferred_element_type=F32)
        da = (dp * m).astype(BF16)
        dq_ref[...] = jnp.dot(da, kv, preferred_element_type=F32)
        dk = lax.dot_general(da, qv, (_DN['tn'], ((), ())), preferred_element_type=F32)
        dm = dp * a
        dlf = jnp.sum(jnp.sum(dm * mf * rf, axis=-1, keepdims=True), axis=0, keepdims=True)
        dlb = jnp.sum(jnp.sum(dm * mb * rb, axis=-1, keepdims=True), axis=0, keepdims=True)
        rid = lax.broadcasted_iota(jnp.int32, (8, LANE), 0)
        dl = jnp.where(rid == 0, dlf, jnp.where(rid == 1, dlb, 0.0))

        @pl.when(i == 0)
        def _():
            dk_ref[...] = dk
            dv_ref[...] = dv
            dlg_ref[...] = dl

        @pl.when(i != 0)
        def _():
            dk_ref[...] += dk
            dv_ref[...] += dv
            dlg_ref[...] += dl

    qs = pl.BlockSpec((tq, LANE), lambda h, i: (i, h))
    ks = pl.BlockSpec((T, LANE), lambda h, i: (0, h))
    return pl.pallas_call(
        body, name="ret_bwd", grid=(cf.HR, T // tq),
        in_specs=[pl.BlockSpec(memory_space=pltpu.SMEM), qs, ks, pl.BlockSpec((T, LANE), lambda h, i: (0, vb + h)), qs],
        out_specs=[qs, ks, ks, pl.BlockSpec((None, 8, LANE), lambda h, i: (h, 0, 0))],
        out_shape=[jax.ShapeDtypeStruct((T, cf.HR * LANE), F32)] * 3 + [jax.ShapeDtypeStruct((cf.HR, 8, LANE), F32)],
        compiler_params=_cp(("arbitrary", "arbitrary")))(lg, q, k, z, do)


def _gla_step(q, k, v, es, st, lmask, allowed):
    C, SB = GLA_CHUNK, GLA_SUB
    nsb = C // SB
    e1, e2, e3, e4, e5 = es[0], es[1:1 + nsb], es[1 + nsb], es[2 + nsb], es[3 + nsb]
    qs = q * lmask * ((HEAD_DIM // 2) ** -0.5)
    ks = k * lmask
    qt = qs * e1
    rows = [lax.dot_general(qt[b * SB:(b + 1) * SB], ks * e2[b], (_DN['nt'], ((), ())), precision=lax.Precision.HIGH,
                            preferred_element_type=F32) for b in range(nsb)]
    att = jnp.where(allowed, jnp.concatenate(rows, axis=0), 0.0)
    o = jnp.dot(att.astype(BF16), v.astype(BF16), preferred_element_type=F32)
    o += lax.dot_general((qs * e3).astype(BF16), st.astype(BF16), (_DN['nt'], ((), ())), preferred_element_type=F32)
    kd = (ks * e4).astype(BF16)
    st_new = st * jnp.concatenate([e5, e5], axis=0) + lax.dot_general(v.astype(BF16), kd, (_DN['tn'], ((), ())), preferred_element_type=F32)
    return o, st_new


def _gla_allowed(d):
    r = lax.broadcasted_iota(jnp.int32, (GLA_CHUNK, GLA_CHUNK), 0)
    m = lax.broadcasted_iota(jnp.int32, (GLA_CHUNK, GLA_CHUNK), 1)
    return (m <= r) if d == 0 else (m >= r)


def _gla_chunk_id(cf, s, d):
    if d == 0:
        return s
    nct, nc = cf.LC // GLA_CHUNK, cf.T // GLA_CHUNK
    return jnp.where(s < nct, nct - 1 - s, nc + nct - 1 - s)


def _gla_lmask(h):
    return (lax.broadcasted_iota(jnp.int32, (1, LANE), 1) // (LANE // 2) == h).astype(F32)


_GLA_CHAINS = [(h, d) for h in range(2) for d in range(2)]


def _gla_row_specs(cf, nc, reverse):
    def rowblk(s, d):
        return _gla_chunk_id(cf, nc - 1 - s if reverse else s, d)

    def spec(width, base, d, per_pair=1):
        return pl.BlockSpec((GLA_CHUNK, width), functools.partial(lambda p, s, base, d: (rowblk(s, d), base + per_pair * p), base=base, d=d))

    def state(d):
        return pl.BlockSpec((2, None, LANE, LANE), functools.partial(lambda p, s, d: (p, rowblk(s, d), 0, 0), d=d))

    return spec, state


def _gla_fwd(cf, z, ef, eb):
    T, C = cf.T, GLA_CHUNK
    nc = T // C
    qb, kb, vb = cf.OFF['gq'] // LANE, cf.OFF['gk'] // LANE, cf.OFF['gv'] // (2 * LANE)
    spec, state = _gla_row_specs(cf, nc, False)

    def body(qf, kf, vf, e_f, qb_, kb_, vb_, e_b, of, sf, ob, sb, st_scr):
        @pl.when(pl.program_id(1) == 0)
        def _():
            st_scr[...] = jnp.zeros_like(st_scr)

        io = [(qf, kf, vf, e_f, of, sf), (qb_, kb_, vb_, e_b, ob, sb)]
        for ci, (h, d) in enumerate(_GLA_CHAINS):
            q, k, v, e, o_ref, s_ref = io[d]
            cols = slice(h * LANE, (h + 1) * LANE)
            es = [e[:, n * LANE:(n + 1) * LANE] for n in range(N_DECAY)]
            st = st_scr[ci]
            s_ref[h] = st
            o, stn = _gla_step(q[...], k[...], v[:, cols], es, st, _gla_lmask(h), _gla_allowed(d))
            st_scr[ci] = stn
            o_ref[:, cols] = o

    ins, outs = [], []
    for d in range(2):
        ins += [spec(LANE, qb, d), spec(LANE, kb, d), spec(2 * LANE, vb, d), spec(N_DECAY * LANE, 0, d)]
        outs += [spec(2 * LANE, 0, d), state(d)]
    oshape = [jax.ShapeDtypeStruct((T, cf.HG * LANE), F32), jax.ShapeDtypeStruct((cf.HG, nc, LANE, LANE), F32)]
    return pl.pallas_call(
        body, name="gla_fwd", grid=(cf.HG // 2, nc), in_specs=ins, out_specs=outs, out_shape=oshape * 2,
        scratch_shapes=[pltpu.VMEM((4, LANE, LANE), F32)],
        compiler_params=_cp(("arbitrary", "arbitrary")))(z, z, z, ef, z, z, z, eb)


def _gla_bwd(cf, z, ef, eb, sf, sb, do):
    T, C = cf.T, GLA_CHUNK
    nc = T // C
    qb, kb, vb = cf.OFF['gq'] // LANE, cf.OFF['gk'] // LANE, cf.OFF['gv'] // (2 * LANE)
    spec, state = _gla_row_specs(cf, nc, True)

    def body(*refs):
        ins = [refs[0:6], refs[6:12]]
        outs = [refs[12:16], refs[16:20]]
        dst_scr = refs[20]

        @pl.when(pl.program_id(1) == 0)
        def _():
            dst_scr[...] = jnp.zeros_like(dst_scr)

        acc = [None, None]
        for ci, (h, d) in enumerate(_GLA_CHAINS):
            q, k, v, e, s_ref, do_ref = ins[d]
            cols = slice(h * LANE, (h + 1) * LANE)
            es = [e[:, n * LANE:(n + 1) * LANE] for n in range(N_DECAY)]
            step = functools.partial(_gla_step, lmask=_gla_lmask(h), allowed=_gla_allowed(d))
            _, vjp_fn = jax.vjp(step, q[...], k[...], v[:, cols], es, s_ref[h])
            dq, dk, dv, des, dst = vjp_fn((do_ref[:, cols], dst_scr[ci]))
            dst_scr[ci] = dst
            outs[d][2][:, cols] = dv
            part = [dq, dk] + list(des)
            acc[d] = part if acc[d] is None else [a + b for a, b in zip(acc[d], part)]
        for d in range(2):
            dq_ref, dk_ref, _, de_ref = outs[d]
            dq_ref[...] = acc[d][0]
            dk_ref[...] = acc[d][1]
            for n in range(N_DECAY):
                de_ref[:, n * LANE:(n + 1) * LANE] = acc[d][2 + n]

    in_specs, out_specs = [], []
    for d in range(2):
        in_specs += [spec(LANE, qb, d), spec(LANE, kb, d), spec(2 * LANE, vb, d), spec(N_DECAY * LANE, 0, d), state(d), spec(2 * LANE, 0, d)]
        out_specs += [spec(LANE, 0, d), spec(LANE, 0, d), spec(2 * LANE, 0, d), spec(N_DECAY * LANE, 0, d)]
    npair = cf.HG // 2
    oshape = [jax.ShapeDtypeStruct((T, npair * LANE), F32), jax.ShapeDtypeStruct((T, npair * LANE), F32),
              jax.ShapeDtypeStruct((T, cf.HG * LANE), F32), jax.ShapeDtypeStruct((T, npair * N_DECAY * LANE), F32)]
    return pl.pallas_call(
        body, name="gla_bwd", grid=(npair, nc), in_specs=in_specs, out_specs=out_specs, out_shape=oshape * 2,
        scratch_shapes=[pltpu.VMEM((4, LANE, LANE), F32)],
        compiler_params=_cp(("arbitrary", "arbitrary")))(z, z, z, ef, sf, do, z, z, z, eb, sb, do)


def _conv_parts(cf, a, w_ref):
    T, LC = cf.T, cf.LC
    rid = lax.broadcasted_iota(jnp.int32, a.shape, 0)
    first = jnp.logical_or(rid == 0, rid == LC)
    last = jnp.logical_or(rid == LC - 1, rid == T - 1)
    ap = jnp.where(first, 0.0, pltpu.roll(a, 1, axis=0))
    an = jnp.where(last, 0.0, pltpu.roll(a, T - 1, axis=0))
    w0, w1, w2, b = w_ref[0:1, :], w_ref[1:2, :], w_ref[2:3, :], w_ref[3:4, :]
    ac = ap * w0 + a * w1 + an * w2 + b
    return ap, an, ac, first, last, (w0, w1, w2)


def _conv_fwd(cf, u, wb):
    T, Fd = cf.T, cf.F
    tc = _tile(Fd, 512)
    nj = Fd // tc

    def body(a_ref, v_ref, w_ref, t_ref):
        _, _, ac, _, _, _ = _conv_parts(cf, a_ref[...], w_ref)
        t_ref[...] = (_silu(ac) * v_ref[...]).astype(t_ref.dtype)

    return pl.pallas_call(
        body, name="conv_fwd", grid=(nj,),
        in_specs=[pl.BlockSpec((T, tc), lambda j: (0, j)), pl.BlockSpec((T, tc), lambda j: (0, nj + j)),
                  pl.BlockSpec((8, tc), lambda j: (0, j))],
        out_specs=pl.BlockSpec((T, tc), lambda j: (0, j)), out_shape=jax.ShapeDtypeStruct((T, Fd), BF16),
        compiler_params=_cp(("parallel",)))(u, u, wb)


def _conv_bwd(cf, u, wb, dt):
    T, Fd = cf.T, cf.F
    tc = _tile(Fd, 256)
    nj = Fd // tc

    def body(a_ref, v_ref, w_ref, dt_ref, du_ref, dw_ref):
        a, v, dtv = a_ref[...], v_ref[...], dt_ref[...].astype(F32)
        ap, an, ac, first, last, (w0, w1, w2) = _conv_parts(cf, a, w_ref)
        sg = _sigmoid(ac)
        du_ref[1] = (dtv * ac * sg).astype(du_ref.dtype)
        dac = dtv * v * (sg * (1.0 + ac * (1.0 - sg)))
        from_next = pltpu.roll(jnp.where(first, 0.0, dac), T - 1, axis=0)
        from_prev = pltpu.roll(jnp.where(last, 0.0, dac), 1, axis=0)
        du_ref[0] = (dac * w1 + from_next * w0 + from_prev * w2).astype(du_ref.dtype)
        rows = [jnp.sum(dac * ap, axis=0, keepdims=True), jnp.sum(dac * a, axis=0, keepdims=True),
                jnp.sum(dac * an, axis=0, keepdims=True), jnp.sum(dac, axis=0, keepdims=True)]
        rid = lax.broadcasted_iota(jnp.int32, (8, tc), 0)
        dw = jnp.zeros((8, tc), F32)
        for n_, rw in enumerate(rows):
            dw = jnp.where(rid == n_, rw, dw)
        dw_ref[...] = dw

    col = pl.BlockSpec((T, tc), lambda j: (0, j))
    return pl.pallas_call(
        body, name="conv_bwd", grid=(nj,),
        in_specs=[col, pl.BlockSpec((T, tc), lambda j: (0, nj + j)), pl.BlockSpec((8, tc), lambda j: (0, j)), col],
        out_specs=[pl.BlockSpec((2, T, tc), lambda j: (0, 0, j)), pl.BlockSpec((8, tc), lambda j: (0, j))],
        out_shape=[jax.ShapeDtypeStruct((2, T, Fd), BF16), jax.ShapeDtypeStruct((8, Fd), F32)],
        compiler_params=_cp(("parallel",)))(u, u, wb, dt)


def _me():
    x, y, c = lax.axis_index("x"), lax.axis_index("y"), lax.axis_index("c")
    return x, y, c, 4 * x + 2 * y + c


def _peer(x, y, c, k):
    px = 1 - x if (k >> 2) & 1 else x
    py = 1 - y if (k >> 1) & 1 else y
    pc = 1 - c if k & 1 else c
    return (px, py, pc), 4 * px + 2 * py + pc


def _rcopy(src, dst, ss, rs, tgt):
    return pltpu.make_async_remote_copy(src_ref=src, dst_ref=dst, send_sem=ss, recv_sem=rs, device_id=tgt,
                                        device_id_type=pl.DeviceIdType.MESH)


def _ag_small(name, v):
    R, Cc = v.shape

    def body(v_ref, o_ref, ssem, rsem, lsem):
        x, y, c, me = _me()
        loc = pltpu.make_async_copy(v_ref, o_ref.at[me], lsem)
        loc.start()
        sends = []
        for k in range(1, N_DEV):
            tgt, _ = _peer(x, y, c, k)
            cp = _rcopy(v_ref, o_ref.at[me], ssem.at[k - 1], rsem.at[k - 1], tgt)
            cp.start()
            sends.append(cp)
        for k in range(1, N_DEV):
            tgt, pi = _peer(x, y, c, k)
            _rcopy(v_ref, o_ref.at[pi], ssem.at[k - 1], rsem.at[k - 1], tgt).wait_recv()
        for cp in sends:
            cp.wait_send()
        loc.wait()

    vm = pl.BlockSpec(memory_space=pltpu.VMEM)
    return pl.pallas_call(
        body, name=name, in_specs=[vm], out_specs=vm, out_shape=jax.ShapeDtypeStruct((N_DEV, R, Cc), v.dtype),
        scratch_shapes=[pltpu.SemaphoreType.DMA((N_DEV - 1,)), pltpu.SemaphoreType.DMA((N_DEV - 1,)), pltpu.SemaphoreType.DMA],
        compiler_params=pltpu.CompilerParams(vmem_limit_bytes=VMEM_LIMIT))(v)


_KINDS = ['in', 'out', 'up', 'down']


def _shard_shape(cf, kind):
    D, Fd = cf.D, cf.F
    return {'in': (D, cf.NINS), 'out': (D // N_DEV, D), 'up': (D, 2 * Fd // N_DEV), 'down': (Fd // N_DEV, D)}[kind]


def _whole_shape(cf, kind):
    D, Fd = cf.D, cf.F
    return {'in': (N_DEV, D, cf.NINS), 'out': (D, D), 'up': (D, 2 * Fd), 'down': (Fd, D)}[kind]


def _part(ref, cf, kind, idx):
    r, cdim = _shard_shape(cf, kind)
    if kind == 'in':
        return ref.at[idx]
    if kind == 'up':
        return ref.at[:, pl.ds(pl.multiple_of(idx * cdim, cdim), cdim)]
    return ref.at[pl.ds(pl.multiple_of(idx * r, r), r), :]


N_BARRIER_IDS = 8


def _handshake(x, y, c):
    barrier = pltpu.get_barrier_semaphore()
    for k in range(1, N_DEV):
        pl.semaphore_signal(barrier, inc=1, device_id=_peer(x, y, c, k)[0], device_id_type=pl.DeviceIdType.MESH)
    pl.semaphore_wait(barrier, N_DEV - 1)


def _seq_kernel(body, name, seq, out_type):
    return pl.kernel(
        body, out_type=out_type, mesh=plsc.ScalarSubcoreMesh(axis_name="sq", num_cores=1), name=name,
        scratch_types=[pltpu.SemaphoreType.DMA((N_DEV - 1,)), pltpu.SemaphoreType.DMA((N_DEV - 1,)), pltpu.SemaphoreType.DMA],
        compiler_params=pltpu.CompilerParams(collective_id=seq % N_BARRIER_IDS))


def _seq_gather(cf, kind, l, seq, src):
    def body(src_ref, land_ref, ssem, rsem, lsem):
        x, y, c, me = _me()
        _handshake(x, y, c)
        sib = (x, y, 1 - c)
        chips = [(1 - x, y), (x, 1 - y), (1 - x, 1 - y)]

        def blk(px, py, pc):
            return _part(land_ref, cf, kind, 4 * px + 2 * py + pc)

        src, mine = src_ref.at[l], blk(x, y, c)
        loc = pltpu.make_async_copy(src, mine, lsem)
        loc.start()
        loc.wait()
        first = [_rcopy(src, mine, ssem.at[0], rsem.at[0], sib)]
        first += [_rcopy(src, mine, ssem.at[1 + j], rsem.at[1 + j], (*chip, c)) for j, chip in enumerate(chips)]
        for cp in first:
            cp.start()
        passed = [_rcopy(blk(*chip, c), blk(*chip, c), ssem.at[4 + j], rsem.at[4 + j], sib) for j, chip in enumerate(chips)]
        for j, chip in enumerate(chips):
            _rcopy(src, blk(*chip, c), ssem.at[1 + j], rsem.at[1 + j], (*chip, c)).wait_recv()
            passed[j].start()
        _rcopy(src, blk(x, y, 1 - c), ssem.at[0], rsem.at[0], sib).wait_recv()
        for j, chip in enumerate(chips):
            _rcopy(src, blk(*chip, 1 - c), ssem.at[4 + j], rsem.at[4 + j], sib).wait_recv()
        for cp in first + passed:
            cp.wait_send()

    return _seq_kernel(body, "seq_gather_%s_%d" % (kind, l), seq, jax.ShapeDtypeStruct(_whole_shape(cf, kind), BF16))(src)


def _seq_scatter(cf, kind, l, seq, g):
    def body(g_ref, recv_ref, ssem, rsem, lsem):
        x, y, c, me = _me()
        _handshake(x, y, c)
        loc = pltpu.make_async_copy(_rs_slab(g_ref, cf, kind, me), recv_ref.at[me], lsem)
        loc.start()
        loc.wait()
        sends = []
        for k in range(1, N_DEV):
            tgt, pi = _peer(x, y, c, k)
            sends.append(_rcopy(_rs_slab(g_ref, cf, kind, pi), recv_ref.at[me], ssem.at[k - 1], rsem.at[k - 1], tgt))
            sends[-1].start()
        for k in range(1, N_DEV):
            tgt, pi = _peer(x, y, c, k)
            _rcopy(_rs_slab(g_ref, cf, kind, pi), recv_ref.at[pi], ssem.at[k - 1], rsem.at[k - 1], tgt).wait_recv()
        for cp in sends:
            cp.wait_send()

    return _seq_kernel(body, "seq_scatter_%s_%d" % (kind, l), seq, jax.ShapeDtypeStruct((N_DEV,) + _shard_shape(cf, kind), BF16))(g)


def _rs_slab(ref, cf, kind, j):
    return ref.at[j] if kind in ('in', 'up') else _part(ref, cf, kind, j)


def _adam_vals(w, g, m, v):
    m2 = ADAM_B1 * m + (1.0 - ADAM_B1) * g
    v2 = ADAM_B2 * v + (1.0 - ADAM_B2) * (g * g)
    mh = m2 / (1.0 - ADAM_B1 ** ADAM_STEP)
    vh = v2 / (1.0 - ADAM_B2 ** ADAM_STEP)
    return -ADAM_LR * (mh / (jnp.sqrt(vh) + ADAM_EPS) + ADAM_WD * w), m2, v2


def _row_tile(R, Cc, budget_elems):
    t = max(16, min(R, (budget_elems // max(Cc, 1)) // 16 * 16))
    while t > 16 and R % t:
        t -= 16
    return t if R % t == 0 else R


def _cast_bf16(name, w, after):
    Dp, R, Cc = w.shape
    tr = _row_tile(R, Cc, 1 << 20)

    def body(w_ref, after_ref, o_ref):
        o_ref[...] = w_ref[...].astype(BF16)

    spec = pl.BlockSpec((None, tr, Cc), lambda l, i: (l, i, 0))
    return pl.pallas_call(body, name=name, grid=(Dp, R // tr), in_specs=[spec, pl.BlockSpec(memory_space=pl.ANY)], out_specs=spec,
                          out_shape=jax.ShapeDtypeStruct(w.shape, BF16), compiler_params=_cp(("parallel", "parallel")))(w, after)


def _unshard_in(cf, name, g):
    D, ns, nz = cf.D, cf.NINS, cf.NZ
    tr = _tile(D, 256, 16)

    def body(g_ref, o_ref):
        for j in range(N_DEV):
            o_ref[:, ns * j:ns * (j + 1)] = g_ref[j]
        o_ref[:, N_DEV * ns:] = jnp.zeros((tr, nz - N_DEV * ns), o_ref.dtype)

    return pl.pallas_call(body, name=name, grid=(D // tr,), in_specs=[pl.BlockSpec((N_DEV, tr, ns), lambda i: (0, i, 0))],
                          out_specs=pl.BlockSpec((tr, nz), lambda i: (i, 0)), out_shape=jax.ShapeDtypeStruct((D, nz), g.dtype),
                          compiler_params=_cp(("parallel",)))(g)


def _slabs_in(cf, name, gw):
    D, ns, nz = cf.D, cf.NINS, cf.NZ
    tr = _tile(D, 256, 16)

    def body(x_ref, o_ref):
        for j in range(N_DEV):
            o_ref[j] = x_ref[:, ns * j:ns * (j + 1)]

    return pl.pallas_call(body, name=name, grid=(D // tr,), in_specs=[pl.BlockSpec((tr, nz), lambda i: (i, 0))],
                          out_specs=pl.BlockSpec((N_DEV, tr, ns), lambda i: (0, i, 0)), out_shape=jax.ShapeDtypeStruct((N_DEV, D, ns), gw.dtype),
                          compiler_params=_cp(("parallel",)))(gw)


def _sum_adam(name, recv, w, m, v, layer, prev, after):
    Dp, R, Cc = w.shape
    tr = _row_tile(R, Cc, 1 << 18)

    def body(r_ref, w_ref, m_ref, v_ref, *rest):
        g_ref, d_ref, mo_ref, vo_ref = rest[-4:]
        g = r_ref[0].astype(F32)
        for s in range(1, N_DEV):
            g = g + r_ref[s].astype(F32)
        dl, m2, v2 = _adam_vals(w_ref[...], g, m_ref[...], v_ref[...])
        g_ref[...] = g
        d_ref[...] = dl
        mo_ref[...] = m2
        vo_ref[...] = v2

    spec = pl.BlockSpec((None, tr, Cc), lambda i: (layer, i, 0))
    anyspec = pl.BlockSpec(memory_space=pl.ANY)
    extra = [after] + (list(prev) if prev is not None else [])
    aliases = {5 + n: n for n in range(4)} if prev is not None else {}
    return pl.pallas_call(body, name=name, grid=(R // tr,), in_specs=[pl.BlockSpec((N_DEV, tr, Cc), lambda i: (0, i, 0)), spec, spec, spec] + [anyspec] * len(extra),
                          out_specs=[spec] * 4, out_shape=[jax.ShapeDtypeStruct(w.shape, F32)] * 4, input_output_aliases=aliases,
                          compiler_params=_cp(("parallel",)))(recv, w, m, v, *extra)


def _adam(name, w, g, m, v):
    R, Cc = w.shape
    tr = _row_tile(R, Cc, 1 << 18)

    def body(w_ref, g_ref, m_ref, v_ref, d_ref, mo_ref, vo_ref):
        dl, m2, v2 = _adam_vals(w_ref[...], g_ref[...], m_ref[...], v_ref[...])
        d_ref[...] = dl
        mo_ref[...] = m2
        vo_ref[...] = v2

    spec = pl.BlockSpec((tr, Cc), lambda i: (i, 0))
    return pl.pallas_call(body, name=name, grid=(R // tr,), in_specs=[spec] * 4, out_specs=[spec] * 3,
                          out_shape=[jax.ShapeDtypeStruct(w.shape, F32)] * 3, compiler_params=_cp(("parallel",)))(w, g, m, v)


def _sum8(name, a):
    n, R, Cc = a.shape

    def body(a_ref, o_ref):
        s = a_ref[0]
        for k in range(1, n):
            s = s + a_ref[k]
        o_ref[...] = s

    return pl.pallas_call(body, name=name, in_specs=[pl.BlockSpec(memory_space=pltpu.VMEM)],
                          out_specs=pl.BlockSpec(memory_space=pltpu.VMEM), out_shape=jax.ShapeDtypeStruct((R, Cc), F32),
                          compiler_params=pltpu.CompilerParams(vmem_limit_bytes=VMEM_LIMIT))(a)


def _ada_fwd(cf, c9, ada_w):
    D = cf.D
    NS = ada_w.shape[-1]
    tk = _tile(D, 512)
    nk = D // tk

    def body(c_ref, w_ref, o_ref):
        kk = pl.program_id(1)
        s = _silu(c_ref[...]).astype(BF16)
        part = jnp.dot(s, w_ref[...].astype(BF16), preferred_element_type=F32)

        @pl.when(kk == 0)
        def _():
            o_ref[...] = part

        @pl.when(kk != 0)
        def _():
            o_ref[...] += part

    return pl.pallas_call(
        body, name="ada_fwd", grid=(DEPTH, nk),
        in_specs=[pl.BlockSpec((16, tk), lambda l, k: (0, k)), pl.BlockSpec((None, tk, NS), lambda l, k: (l, k, 0))],
        out_specs=pl.BlockSpec((None, 16, NS), lambda l, k: (l, 0, 0)),
        out_shape=jax.ShapeDtypeStruct((DEPTH, 16, NS), F32), compiler_params=_cp(("parallel", "arbitrary")))(c9, ada_w)


def _ada_bwd(cf, c9, ada_w, dm9):
    D = cf.D
    NS = ada_w.shape[-1]
    tk = _tile(D, 512)
    nk = D // tk

    def body(c_ref, w_ref, dm_ref, gw_ref, ds_ref):
        cv = c_ref[...]
        sg = _sigmoid(cv)
        dmb = dm_ref[...].astype(BF16)
        gw_ref[...] = lax.dot_general((cv * sg).astype(BF16), dmb, (_DN['tn'], ((), ())), preferred_element_type=F32)
        ds = lax.dot_general(dmb, w_ref[...].astype(BF16), (_DN['nt'], ((), ())), preferred_element_type=F32)
        ds_ref[...] = ds * (sg * (1.0 + cv * (1.0 - sg)))

    return pl.pallas_call(
        body, name="ada_bwd", grid=(DEPTH, nk),
        in_specs=[pl.BlockSpec((16, tk), lambda l, k: (0, k)), pl.BlockSpec((None, tk, NS), lambda l, k: (l, k, 0)),
                  pl.BlockSpec((None, 16, NS), lambda l, k: (l, 0, 0))],
        out_specs=[pl.BlockSpec((None, tk, NS), lambda l, k: (l, k, 0)), pl.BlockSpec((None, 16, tk), lambda l, k: (l, 0, k))],
        out_shape=[jax.ShapeDtypeStruct((DEPTH, D, NS), F32), jax.ShapeDtypeStruct((DEPTH, 16, D), F32)],
        compiler_params=_cp(("parallel", "parallel")))(c9, ada_w, dm9)


def _rope_tables(cf):
    L, LC = cf.L, cf.LC
    rows = L // GRID_W
    row = jnp.repeat(jnp.arange(rows, dtype=F32), GRID_W)
    col = jnp.tile(jnp.arange(GRID_W, dtype=F32), rows)
    nf = HEAD_DIM // 4
    inv = ROPE_THETA ** (-jnp.arange(nf, dtype=F32) / nf)
    ang = jnp.concatenate([row[:, None] * inv, col[:, None] * inv], axis=-1)
    cos, sin = jnp.cos(ang), jnp.sin(ang)
    cs = jnp.concatenate([jnp.ones((LC, HEAD_DIM), F32), jnp.concatenate([cos, cos], -1)], 0)
    sn = jnp.concatenate([jnp.zeros((LC, HEAD_DIM), F32), jnp.concatenate([-sin, sin], -1)], 0)
    return cs, sn


def _prep_tiles(cf, z, cs, sn, key):
    b = cf.OFF[key] // LANE
    return [(z, LANE, _col(b), True), (cs, LANE, _c0, False), (sn, LANE, _c0, False)]


_PREP = {'aq': _f_prep_norm, 'ak': _f_prep_norm, 'rq': _f_prep_plain, 'rk': _f_prep_scaled}


def _prep_fwd(cf, z, cs, sn, key, g):
    nh = cf.W[key] // LANE
    params = [(g, 'shared', True)] if g is not None else []
    return _row_fwd(cf, "prep_fwd_" + key, _PREP[key], _prep_tiles(cf, z, cs, sn, key), params,
                    [(LANE, _col(0), cf.W[key], BF16)], cf.TQ, nrep=nh)[0]


def _prep_bwd(cf, z, cs, sn, key, g, dt):
    nh = cf.W[key] // LANE
    params = [(g, 'shared', True)] if g is not None else []
    tg, pg = _row_bwd(cf, "prep_bwd_" + key, _PREP[key], _prep_tiles(cf, z, cs, sn, key), params,
                      [(dt, LANE, _col(0))], [(LANE, _col(0), cf.W[key], BF16)], cf.TQ, nrep=nh)
    return tg[0], (pg[0] if g is not None else None)


def _gate_params(cf, gup, gb):
    K = gup.shape[-1]
    gf = jnp.zeros((LANE, K), F32).at[0:GLA_RANK].set(gup[0])
    gbm = jnp.zeros((LANE, K), F32).at[GLA_RANK:2 * GLA_RANK].set(gup[1])
    out = []
    for p in range(K // LANE):
        cols = slice(p * LANE, (p + 1) * LANE)
        out += [(gf[:, cols], 'shared', True), (gbm[:, cols], 'shared', True), (gb[0:1, cols], 'shared', True), (gb[1:2, cols], 'shared', True)]
    return out


def _mix_tiles(cf, z, o, key):
    return [(o, LANE, _col(0), True), (z, LANE, _col(cf.OFF[key] // LANE), True)]


def _mid_io(cf, l, W, mod, x, y, norm2_g=None):
    tiles = [(x, cf.D, _c0, True), (y, cf.D, _c0, True)]
    n2 = W['norm2_g'][l] if norm2_g is None else norm2_g
    params = [(mod[2], 'stream', True), (n2, 'shared', True), (mod[3], 'stream', True), (mod[4], 'stream', True)]
    return tiles, params


def _launch_scatter(cf, kind, l, seq, g, nxt):
    g, nxt = lax.optimization_barrier((g, nxt))
    return _seq_scatter(cf, kind, l, seq, g), nxt


class _BigWeights:
    def __init__(self, cf, shards):
        self.cf = cf
        self.whole = {(kind, l): _seq_gather(cf, kind, l, l * len(_KINDS) + n, shards[n])
                      for l in range(DEPTH) for n, kind in enumerate(_KINDS)}
        self.w_in = {}

    def get(self, kind, l, after=None):
        cf = self.cf
        if kind != 'in':
            return self.whole[(kind, l)]
        if l not in self.w_in:
            whole, _ = lax.optimization_barrier((self.whole[(kind, l)], after))
            self.w_in[l] = _unshard_in(cf, "unshard_in_%d" % l, whole)
        return self.w_in[l]


def _layer_fwd(cf, l, W, big, mod, x, h, cs, sn):
    T, D, Fd = cf.T, cf.D, cf.F
    z = _mm("z_%d" % l, h, big.get('in', l, h), 'nn', T, cf.NZ, D, F32, tm=T, tn=768, tk=D)
    qa = _prep_fwd(cf, z, cs, sn, 'aq', W['q_norm_g'][l])
    ka = _prep_fwd(cf, z, cs, sn, 'ak', W['k_norm_g'][l])
    qr = _prep_fwd(cf, z, cs, sn, 'rq', None)
    kr = _prep_fwd(cf, z, cs, sn, 'rk', None)
    o_att = _att_fwd(cf, qa, ka, z)
    o_ret = _ret_fwd(cf, qr, kr, z, W['ret_log_decay'][l])
    gates = _gate_params(cf, W['gla_gate_up'][l], W['gla_gate_b'][l])
    ga_tile = [(z, LANE, _col(cf.OFF['ga'] // LANE), True)]
    we = (cf.HG // 2) * N_DECAY * LANE
    ef, eb = _row_fwd(cf, "gates_fwd_%d" % l, _f_gla_pre, ga_tile, gates, [(we, _c0, we, F32), (we, _c0, we, F32)], GLA_CHUNK)
    o_f, sf, o_b, sb = _gla_fwd(cf, z, ef, eb)
    o_gla = o_f + o_b
    cat_r = _row_fwd(cf, "mixr_fwd_%d" % l, _f_gated_norm, _mix_tiles(cf, z, o_ret, 'rg'), [(W['ret_norm_g'][l], 'shared', True)],
                     [(LANE, _col(0), cf.HR * LANE, BF16)], cf.TQ, nrep=cf.HR)[0]
    cat_g = _row_fwd(cf, "mixg_fwd_%d" % l, _f_gated_norm, _mix_tiles(cf, z, o_gla, 'gr'), [(W['gla_norm_g'][l], 'shared', True)],
                     [(LANE, _col(0), cf.HG * LANE, BF16)], cf.TQ, nrep=cf.HG)[0]
    cat = jnp.concatenate([o_att, cat_r, cat_g], axis=-1)
    y = _mm("y_%d" % l, cat, big.get('out', l, cat), 'nn', T, D, D, F32, tm=T, tn=512, tk=D)
    tiles, params = _mid_io(cf, l, W, mod, x, y)
    x1, h2 = _row_fwd(cf, "mid_fwd_%d" % l, _f_resid_norm_mod, tiles, params, [(D, _c0, D, F32), (D, _c0, D, BF16)], cf.TM)
    u = _mm("u_%d" % l, h2, big.get('up', l, h2), 'nn', T, 2 * Fd, D, F32, tm=T, tn=512, tk=D)
    t = _conv_fwd(cf, u, W['conv_wb'][l])
    yff = _mm("yff_%d" % l, t, big.get('down', l, t), 'nn', T, D, Fd, F32, tm=T, tn=1024, tk=512)
    return dict(x=x, h=h, z=z, qa=qa, ka=ka, qr=qr, kr=kr, ef=ef, eb=eb, sf=sf, sb=sb, o_ret=o_ret, o_gla=o_gla, cat=cat, y=y,
                x1=x1, h2=h2, u=u, t=t, yff=yff, gates=gates)


def _layer_bwd(cf, l, W, big, mod, sv, dx1, dyff, cs, sn):
    T, D, Fd = cf.T, cf.D, cf.F
    g, rs = {}, {}
    sq = 2 * len(_KINDS) + (DEPTH - 1 - l) * len(_KINDS)
    gwd = _mm("gwd_%d" % l, sv['t'], dyff, 'tn', Fd, D, T, BF16, tm=1408, tn=2048, tk=T)
    rs['down'], wb = _launch_scatter(cf, 'down', l, sq, gwd, W['conv_wb'][l])
    dt = _mm("dt_%d" % l, dyff, big.get('down', l), 'nt', T, Fd, D, BF16, tm=T, tn=1408, tk=D)
    du, g['conv_wb'] = _conv_bwd(cf, sv['u'], wb, dt)
    cu = 2 * Fd // N_DEV
    half = Fd // cu
    gwu = _mm("gwu_%d" % l, sv['h2'], du, 'tn', D, 2 * Fd, T, BF16, tm=D, tn=cu, tk=T, out_shape=(N_DEV, D, cu),
              b_spec=pl.BlockSpec((None, T, cu), lambda i, j, k: (j // half, 0, j % half)),
              out_spec=pl.BlockSpec((None, D, cu), lambda i, j, k: (j, i, 0)))
    rs['up'], n2 = _launch_scatter(cf, 'up', l, sq + 1, gwu, W['norm2_g'][l])
    dh2 = _mm("dh2_%d" % l, du, big.get('up', l), 'nt', T, D, 2 * Fd, BF16, tm=T, tn=1024, tk=cu,
              a_spec=pl.BlockSpec((None, T, cu), lambda i, j, k: (k // half, 0, k % half)))
    tiles, params = _mid_io(cf, l, W, mod, sv['x'], sv['y'], n2)
    (dx, dy), (g['m2'], g['norm2_g'], g['m3'], g['m4']) = _row_bwd(
        cf, "mid_bwd_%d" % l, _f_resid_norm_mod, tiles, params, [(dx1, D, _c0), (dh2, D, _c0)],
        [(D, _c0, D, F32), (D, _c0, D, BF16)], cf.TM)
    gwo = _mm("gwo_%d" % l, sv['cat'], dy, 'tn', D, D, T, BF16, tm=D, tn=1024, tk=T)
    rs['out'], rn = _launch_scatter(cf, 'out', l, sq + 2, gwo, W['ret_norm_g'][l])
    dcat = _mm("dcat_%d" % l, dy, big.get('out', l), 'nt', T, D, D, BF16, tm=T, tn=1024, tk=D)
    z = sv['z']
    (do_ret, drg), (g['ret_norm_g'],) = _row_bwd(
        cf, "mixr_bwd_%d" % l, _f_gated_norm, _mix_tiles(cf, z, sv['o_ret'], 'rg'), [(rn, 'shared', True)],
        [(dcat, LANE, _col(cf.HQ))], [(LANE, _col(0), cf.HR * LANE, F32), (LANE, _col(0), cf.HR * LANE, BF16)], cf.TQ, nrep=cf.HR)
    (do_gla, dgr), (g['gla_norm_g'],) = _row_bwd(
        cf, "mixg_bwd_%d" % l, _f_gated_norm, _mix_tiles(cf, z, sv['o_gla'], 'gr'), [(W['gla_norm_g'][l], 'shared', True)],
        [(dcat, LANE, _col(cf.HQ + cf.HR))], [(LANE, _col(0), cf.HG * LANE, F32), (LANE, _col(0), cf.HG * LANE, BF16)], cf.TQ, nrep=cf.HG)
    dqa, dka, dav = _att_bwd(cf, sv['qa'], sv['ka'], z, dcat)
    dqr, dkr, drv, dlg = _ret_bwd(cf, sv['qr'], sv['kr'], z, W['ret_log_decay'][l], do_ret)
    g['ret_log_decay'] = dlg[:, 0:2, 0].T
    dq_f, dk_f, dv_f, def_, dq_b, dk_b, dv_b, deb = _gla_bwd(cf, z, sv['ef'], sv['eb'], sv['sf'], sv['sb'], do_gla)
    dgq, dgk, dgv = dq_f + dq_b, dk_f + dk_b, dv_f + dv_b
    we = (cf.HG // 2) * N_DECAY * LANE
    ga_tile = [(z, LANE, _col(cf.OFF['ga'] // LANE), True)]
    (dga,), gg = _row_bwd(cf, "gates_bwd_%d" % l, _f_gla_pre, ga_tile, sv['gates'],
                          [(def_, we, _c0), (deb, we, _c0)], [(LANE, _c0, LANE, BF16)], GLA_CHUNK)
    ggf, ggb, gbf, gbb = [jnp.concatenate(gg[n::4], axis=-1) for n in range(4)]
    g['gla_gate_up'] = jnp.stack([ggf[0:GLA_RANK], ggb[GLA_RANK:2 * GLA_RANK]])
    g['gla_gate_b'] = jnp.concatenate([gbf, gbb], axis=0)
    daq, g['q_norm_g'] = _prep_bwd(cf, z, cs, sn, 'aq', W['q_norm_g'][l], dqa)
    dak, g['k_norm_g'] = _prep_bwd(cf, z, cs, sn, 'ak', W['k_norm_g'][l], dka)
    drq, _ = _prep_bwd(cf, z, cs, sn, 'rq', None, dqr)
    drk, _ = _prep_bwd(cf, z, cs, sn, 'rk', None, dkr)
    pad = jnp.zeros((T, cf.NZ - cf.OFF['ga'] - LANE), BF16)
    dz = jnp.concatenate([daq, dak, dav.astype(BF16), drq, drk, drv.astype(BF16), drg, dgq.astype(BF16), dgk.astype(BF16),
                          dgv.astype(BF16), dgr, dga, pad], axis=-1)
    gwi = _mm("gwi_%d" % l, sv['h'], dz, 'tn', D, cf.NZ, T, BF16, tm=D, tn=768, tk=T)
    rs['in'], g['norm1_g_tied'] = _launch_scatter(cf, 'in', l, sq + 3, _slabs_in(cf, "slabs_in_%d" % l, gwi), W['norm1_g'][l])
    dh =_mm("dh_%d" % l, dz, big.get('in', l), 'nt', T, D, cf.NZ, BF16, tm=T, tn=1024, tk=1792)
    g['rs'] = rs
    return dx, dh, g


_WEIGHTS = ['c_ctx', 'ada_w', 'ada_b', 'norm1_g', 'w_in', 'q_norm_g', 'k_norm_g', 'ret_log_decay', 'ret_norm_g',
            'gla_gate_up', 'gla_gate_b', 'gla_norm_g', 'w_out', 'norm2_g', 'w_up', 'conv_w', 'conv_b', 'w_down', 'final_norm_g']
_BIG = ['w_in', 'w_out', 'w_up', 'w_down']
_SMALL = [n for n in _WEIGHTS if n not in _BIG and n != 'ada_w']
_COL_SHARDED = ['gla_gate_up', 'gla_gate_b', 'conv_w']


def _pack(arrs):
    rows = []
    for a in arrs:
        flat = a.reshape(-1)
        n = flat.shape[0]
        rows.append(jnp.pad(flat, (0, -n % LANE)).reshape(-1, LANE))
    packed = jnp.concatenate(rows, axis=0)
    return jnp.pad(packed, ((0, -packed.shape[0] % 8), (0, 0)))


def _unpack(packed, shapes):
    lead = packed.shape[:-2]
    out, r = [], 0
    for s in shapes:
        n = int(np.prod(s))
        nr = -(-n // LANE)
        out.append(packed[..., r:r + nr, :].reshape(lead + (nr * LANE,))[..., :n].reshape(lead + tuple(s)))
        r += nr
    return out


def _unshard_last(a):
    return jnp.moveaxis(a, 0, -2).reshape(a.shape[1:-1] + (N_DEV * a.shape[-1],))


def _step(cf, x, c, ctx, loss_target, w, m, v):
    T, D, Fd, L, LC = cf.T, cf.D, cf.F, cf.L, cf.LC
    _, _, _, me = _me()
    NS = w['ada_w'].shape[-1]

    c_all = _ag_small("ag_c", jnp.pad(c, ((0, 7), (0, 0))))[:, 0, :]
    c9 = jnp.concatenate([c_all, w['c_ctx'][None], jnp.zeros((7, D), F32)], axis=0)
    pm = _ada_fwd(cf, c9, w['ada_w'])
    pm_all = _ag_small("ag_mod", pm.reshape(DEPTH * 16, NS)).reshape(N_DEV, DEPTH, 16, NS)
    mod_all = _unshard_last(pm_all) + w['ada_b'][:, None, :]
    mod_own = lax.dynamic_index_in_dim(mod_all, me, axis=1, keepdims=False)
    mods = []
    for l in range(DEPTH):
        mods.append([jnp.stack([mod_all[l, 8, k * D:(k + 1) * D], mod_own[l, k * D:(k + 1) * D]])[:, None, :] for k in range(N_MOD)])

    shard_shapes = [w[n].shape for n in _COL_SHARDED]
    got = _ag_small("ag_smallw", _pack([w[n] for n in _COL_SHARDED]))
    full = dict(zip(_COL_SHARDED, [_unshard_last(a) for a in _unpack(got, shard_shapes)]))

    small_done = full['conv_w'] + mod_all[0, 0, 0]
    big = _BigWeights(cf, [_cast_bf16("cast_" + n, w[n], c if n == 'w_in' else small_done) for n in _BIG])
    conv_wb = jnp.concatenate([full['conv_w'], w['conv_b'][:, None, :], jnp.zeros((DEPTH, 4, Fd), F32)], axis=1)
    W = dict(conv_wb=conv_wb, gla_gate_up=full['gla_gate_up'], gla_gate_b=full['gla_gate_b'], ret_log_decay=w['ret_log_decay'])
    for n in ['q_norm_g', 'k_norm_g', 'ret_norm_g', 'gla_norm_g', 'norm1_g', 'norm2_g']:
        W[n] = w[n][:, None, :]

    cs, sn = _rope_tables(cf)
    x0 = jnp.concatenate([ctx[0], x[0]], axis=0)
    pre_tiles = [(x0, D, _c0, True)]

    def pre_params(n1):
        return [(n1, 'shared', True), (mods[0][0], 'stream', True), (mods[0][1], 'stream', True)]

    def tr_params(n1):
        return [(mods[0][5], 'stream', True), (n1, 'shared', True), (mods[1][0], 'stream', True), (mods[1][1], 'stream', True)]

    h0 = _row_fwd(cf, "pre_fwd", _f_norm_mod, pre_tiles, pre_params(W['norm1_g'][0]), [(D, _c0, D, BF16)], cf.TM)[0]
    sv0 = _layer_fwd(cf, 0, W, big, mods[0], x0, h0, cs, sn)
    tr_tiles = [(sv0['x1'], D, _c0, True), (sv0['yff'], D, _c0, True)]
    xb, hb = _row_fwd(cf, "tr_fwd", _f_resid_norm_mod, tr_tiles, tr_params(W['norm1_g'][1]), [(D, _c0, D, F32), (D, _c0, D, BF16)], cf.TM)
    sv1 = _layer_fwd(cf, 1, W, big, mods[1], xb, hb, cs, sn)
    tgt = jnp.concatenate([jnp.zeros((LC, D), F32), loss_target[0]], axis=0)
    dx1, dyff, dm5_1, g_final, ls = _loss_grad(cf, sv1['x1'], sv1['yff'], mods[1][5], w['final_norm_g'][None], tgt)
    loss = lax.psum(ls[0, 0], ("x", "y", "c"))

    dxb, dhb, g1 = _layer_bwd(cf, 1, W, big, mods[1], sv1, dx1, dyff, cs, sn)
    (dx1_0, dyff_0), (dm5_0, gn1_1, dm0_1, dm1_1) = _row_bwd(
        cf, "tr_bwd", _f_resid_norm_mod, tr_tiles, tr_params(g1['norm1_g_tied']), [(dxb, D, _c0), (dhb, D, _c0)],
        [(D, _c0, D, F32), (D, _c0, D, BF16)], cf.TM)
    dx0, dh0, g0 = _layer_bwd(cf, 0, W, big, mods[0], sv0, dx1_0, dyff_0, cs, sn)
    (dxa,), (gn1_0, dm0_0, dm1_0) = _row_bwd(cf, "pre_bwd", _f_first, pre_tiles, pre_params(g0['norm1_g_tied']), [(dx0, D, _c0), (dh0, D, _c0)],
                                            [(D, _c0, D, F32)], cf.TM)
    grad_x = dxa[LC:][None]

    dmod = jnp.stack([jnp.concatenate([dm0_0, dm1_0, g0['m2'], g0['m3'], g0['m4'], dm5_0], axis=-1)[:, 0],
                      jnp.concatenate([dm0_1, dm1_1, g1['m2'], g1['m3'], g1['m4'], dm5_1], axis=-1)[:, 0]])
    dm_all = _ag_small("ag_dmod", jnp.pad(dmod.reshape(2 * DEPTH, N_MOD * D), ((0, 8 - 2 * DEPTH), (0, 0))))
    dm_all = dm_all[:, :2 * DEPTH].reshape(N_DEV, DEPTH, 2, N_MOD * D)
    dctx = _sum8("sum_dmodc", jnp.pad(dm_all[:, :, 0], ((0, 0), (0, 8 - DEPTH), (0, 0))))[:DEPTH]
    dm9 = jnp.concatenate([jnp.moveaxis(dm_all[:, :, 1], 0, 1), dctx[:, None]], axis=1)
    g_ada_b = _sum8("sum_adab", jnp.pad(jnp.moveaxis(dm9, 1, 0), ((0, 0), (0, 8 - DEPTH), (0, 0))))[:DEPTH]
    dm9s = lax.dynamic_slice_in_dim(jnp.pad(dm9, ((0, 0), (0, 7), (0, 0))), me * NS, NS, axis=2)
    g_ada_w, dsil = _ada_bwd(cf, c9, w['ada_w'], dm9s)
    g_cctx_part = dsil[0, 8]
    for l in range(1, DEPTH):
        g_cctx_part = g_cctx_part + dsil[l, 8]

    def both(key):
        return jnp.stack([g0[key], g1[key]])

    gsmall = dict(c_ctx=g_cctx_part, norm1_g=jnp.stack([gn1_0[0], gn1_1[0]]), q_norm_g=both('q_norm_g')[:, 0],
                  k_norm_g=both('k_norm_g')[:, 0], ret_log_decay=both('ret_log_decay'), ret_norm_g=both('ret_norm_g')[:, 0],
                  gla_gate_up=both('gla_gate_up'), gla_gate_b=both('gla_gate_b'), gla_norm_g=both('gla_norm_g')[:, 0],
                  norm2_g=both('norm2_g')[:, 0], conv_w=both('conv_wb')[:, 0:3], conv_b=both('conv_wb')[:, 3], final_norm_g=g_final[0])
    snames = [n for n in _SMALL if n != 'ada_b']
    sshapes = [gsmall[n].shape for n in snames]
    gs_all = _ag_small("ag_gsmall", _pack([gsmall[n] for n in snames]))
    gs = dict(zip(snames, _unpack(_sum8("sum_gsmall", gs_all), sshapes)))
    gs['ada_b'] = g_ada_b
    for n in _COL_SHARDED:
        ns_ = w[n].shape[-1]
        gs[n] = lax.dynamic_slice_in_dim(gs[n], me * ns_, ns_, axis=gs[n].ndim - 1)

    out_g, out_d, out_m, out_v = {}, {}, {}, {}

    after, done = gs_all, {}
    for l, gl in ((1, g1), (0, g0)):
        for kind in reversed(_KINDS):
            n = 'w_' + kind
            done[n] = _sum_adam("adam_%s_%d" % (n, l), gl['rs'][kind], w[n], m[n], v[n], l, done.get(n), after)
            after = done[n][0]
    for n in _BIG:
        out_g[n], out_d[n], out_m[n], out_v[n] = done[n]
    aw = [a.reshape(DEPTH * D, NS) for a in (w['ada_w'], g_ada_w, m['ada_w'], v['ada_w'])]
    out_g['ada_w'] = g_ada_w
    out_d['ada_w'], out_m['ada_w'], out_v['ada_w'] = [a.reshape(DEPTH, D, NS) for a in _adam("adam_ada_w", *aw)]
    shp = [w[n].shape for n in _SMALL]
    packed = [_pack([src[n] for n in _SMALL]) for src in (w, gs, m, v)]
    res = _adam("adam_small", *packed)
    for dst, pk in zip((out_d, out_m, out_v), res):
        dst.update(zip(_SMALL, _unpack(pk, shp)))
    out_g.update({n: gs[n] for n in _SMALL})
    return (loss, grad_x, *[out_g[n] for n in _WEIGHTS], *[out_d[n] for n in _WEIGHTS], *[out_m[n] for n in _WEIGHTS],
            *[out_v[n] for n in _WEIGHTS])


def kernel(x, c, ctx, c_ctx, ada_w, ada_b, norm1_g, w_in, q_norm_g, k_norm_g, ret_log_decay, ret_norm_g, gla_gate_up, gla_gate_b, gla_norm_g, w_out, norm2_g, w_up, conv_w, conv_b, w_down, final_norm_g, loss_target, m_c_ctx, m_ada_w, m_ada_b, m_norm1_g, m_w_in, m_q_norm_g, m_k_norm_g, m_ret_log_decay, m_ret_norm_g, m_gla_gate_up, m_gla_gate_b, m_gla_norm_g, m_w_out, m_norm2_g, m_w_up, m_conv_w, m_conv_b, m_w_down, m_final_norm_g, v_c_ctx, v_ada_w, v_ada_b, v_norm1_g, v_w_in, v_q_norm_g, v_k_norm_g, v_ret_log_decay, v_ret_norm_g, v_gla_gate_up, v_gla_gate_b, v_gla_norm_g, v_w_out, v_norm2_g, v_w_up, v_conv_w, v_conv_b, v_w_down, v_final_norm_g):
    w = dict(c_ctx=c_ctx, ada_w=ada_w, ada_b=ada_b, norm1_g=norm1_g, w_in=w_in, q_norm_g=q_norm_g, k_norm_g=k_norm_g,
             ret_log_decay=ret_log_decay, ret_norm_g=ret_norm_g, gla_gate_up=gla_gate_up, gla_gate_b=gla_gate_b,
             gla_norm_g=gla_norm_g, w_out=w_out, norm2_g=norm2_g, w_up=w_up, conv_w=conv_w, conv_b=conv_b, w_down=w_down,
             final_norm_g=final_norm_g)
    m = dict(c_ctx=m_c_ctx, ada_w=m_ada_w, ada_b=m_ada_b, norm1_g=m_norm1_g, w_in=m_w_in, q_norm_g=m_q_norm_g,
             k_norm_g=m_k_norm_g, ret_log_decay=m_ret_log_decay, ret_norm_g=m_ret_norm_g, gla_gate_up=m_gla_gate_up,
             gla_gate_b=m_gla_gate_b, gla_norm_g=m_gla_norm_g, w_out=m_w_out, norm2_g=m_norm2_g, w_up=m_w_up,
             conv_w=m_conv_w, conv_b=m_conv_b, w_down=m_w_down, final_norm_g=m_final_norm_g)
    v = dict(c_ctx=v_c_ctx, ada_w=v_ada_w, ada_b=v_ada_b, norm1_g=v_norm1_g, w_in=v_w_in, q_norm_g=v_q_norm_g,
             k_norm_g=v_k_norm_g, ret_log_decay=v_ret_log_decay, ret_norm_g=v_ret_norm_g, gla_gate_up=v_gla_gate_up,
             gla_gate_b=v_gla_gate_b, gla_norm_g=v_gla_norm_g, w_out=v_w_out, norm2_g=v_norm2_g, w_up=v_w_up,
             conv_w=v_conv_w, conv_b=v_conv_b, w_down=v_w_down, final_norm_g=v_final_norm_g)
    return _step(_cfg(), x, c, ctx, loss_target, w, m, v)
```

```python
import functools
import math
import types

import jax
import jax.numpy as jnp
import numpy as np
from jax import lax
from jax.experimental import pallas as pl
from jax.experimental.pallas import tpu as pltpu
from jax.experimental.pallas import tpu_sc as plsc

F32 = jnp.float32
BF16 = jnp.bfloat16

D_MODEL = 2048
SEQ = 2048
CTX_LEN = 256
GRID_W = 64
D_FF = 5632
DEPTH = 2
N_DEV = 8
HEAD_DIM = 128
ROPE_THETA = 10000.0
GLA_TAU = 16.0
GLA_RANK = 16
GLA_CHUNK = 64
GLA_SUB = 16
EPS = 1e-6
N_MOD = 6
ADAM_LR = 0.001
ADAM_B1 = 0.9
ADAM_B2 = 0.999
ADAM_EPS = 1e-08
ADAM_WD = 0.01
ADAM_STEP = 10
LANE = 128
VMEM_LIMIT = 56 * 1024 * 1024
NEG = -1e30


def _cfg():
    d = types.SimpleNamespace()
    d.D, d.L, d.LC, d.F = D_MODEL, SEQ, CTX_LEN, D_FF
    d.T = d.L + d.LC
    nm = d.D // HEAD_DIM
    d.HQ, d.HKV, d.HR, d.HG = nm // 2, nm // 8, nm // 4, nm // 4
    d.G = d.HQ // d.HKV
    w = dict(aq=d.HQ * 128, ak=d.HKV * 128, av=d.HKV * 128, rq=d.HR * 128, rk=d.HR * 128, rv=d.HR * 128,
             rg=d.HR * 128, gq=d.HG * 64, gk=d.HG * 64, gv=d.HG * 128, gr=d.HG * 128, ga=2 * GLA_RANK)
    off, o = {}, 0
    for k, v in w.items():
        off[k] = o
        o += v
    d.W, d.OFF, d.NIN = w, off, o
    d.NZ = -(-(off['ga'] + LANE) // 256) * 256
    d.NINS = d.NIN // N_DEV
    d.TM = math.gcd(d.LC, 128)
    d.TQ = math.gcd(d.LC, 256)
    return d


def _cp(sem=None):
    return pltpu.CompilerParams(dimension_semantics=sem, vmem_limit_bytes=VMEM_LIMIT)


def _tile(n, target, mult=LANE):
    t = min(n, target)
    t -= t % mult
    while t > mult and n % t:
        t -= mult
    return t if t > 0 and n % t == 0 else n


_DN = {'nn': ((1,), (0,)), 'nt': ((1,), (1,)), 'tn': ((0,), (0,))}


def _mm(name, a, b, kind, M, N, K, out_dtype, tm=768, tn=768, tk=1024, a_spec=None, b_spec=None,
        out_shape=None, out_spec=None):
    tm, tn = _tile(M, tm, 128), _tile(N, tn, 128)
    tk = _tile(K, tk, 128)
    nk = K // tk

    def dot(a_ref, b_ref):
        return lax.dot_general(a_ref[...].astype(BF16), b_ref[...].astype(BF16), (_DN[kind], ((), ())), preferred_element_type=F32)

    def body_one(a_ref, b_ref, o_ref):
        o_ref[...] = dot(a_ref, b_ref).astype(o_ref.dtype)

    def body(a_ref, b_ref, o_ref, acc):
        kk = pl.program_id(2)

        @pl.when(kk == 0)
        def _():
            acc[...] = jnp.zeros_like(acc)

        acc[...] += dot(a_ref, b_ref)

        @pl.when(kk == nk - 1)
        def _():
            o_ref[...] = acc[...].astype(o_ref.dtype)

    if a_spec is None:
        a_spec = pl.BlockSpec((tk, tm), lambda i, j, k: (k, i)) if kind == 'tn' else pl.BlockSpec((tm, tk), lambda i, j, k: (i, k))
    if b_spec is None:
        b_spec = pl.BlockSpec((tn, tk), lambda i, j, k: (j, k)) if kind == 'nt' else pl.BlockSpec((tk, tn), lambda i, j, k: (k, j))
    if out_spec is None:
        out_spec = pl.BlockSpec((tm, tn), lambda i, j, k: (i, j))
        out_shape = (M, N)
    return pl.pallas_call(
        body_one if nk == 1 else body, name=name, grid=(M // tm, N // tn, nk), in_specs=[a_spec, b_spec], out_specs=out_spec,
        out_shape=jax.ShapeDtypeStruct(out_shape, out_dtype), scratch_shapes=[] if nk == 1 else [pltpu.VMEM((tm, tn), F32)],
        compiler_params=_cp(("parallel", "parallel", "arbitrary")))(a, b)


def _tile_spec(tm, w, colf, nrep):
    if hasattr(colf, 'base'):
        assert colf.base % nrep == 0
        return pl.BlockSpec((tm, w * nrep), functools.partial(lambda i, b: (i, b), b=colf.base // nrep))
    return pl.BlockSpec((tm, w), lambda i: (i, 0))


def _head_cols(colf, w, r):
    return slice(r * w, (r + 1) * w) if hasattr(colf, 'base') else slice(None)


def _row_specs(cf, tm, tiles, params, nrep):
    nctx = cf.LC // tm
    specs = [_tile_spec(tm, w, colf, nrep) for arr, w, colf, _ in tiles]
    for arr, kind, _ in params:
        nd = arr.ndim
        if kind == 'shared':
            specs.append(pl.BlockSpec(arr.shape, functools.partial(lambda i, nd: (0,) * nd, nd=nd)))
        else:
            specs.append(pl.BlockSpec((None,) + arr.shape[1:],
                                      functools.partial(lambda i, nd, nctx: (jnp.where(i >= nctx, 1, 0),) + (0,) * (nd - 1), nd=nd, nctx=nctx)))
    return specs


def _row_fwd(cf, name, f, tiles, params, outs, tm, nrep=1):
    nt, npar = len(tiles), len(params)

    def body(*refs):
        pv = [r[...] for r in refs[nt:nt + npar]]
        for r in range(nrep):
            tv = [x[:, _head_cols(t[2], t[1], r)].astype(F32) for x, t in zip(refs[:nt], tiles)]
            res = f(*tv, *pv)
            for o, v, spec in zip(refs[nt + npar:], res, outs):
                o[:, _head_cols(spec[1], spec[0], r)] = v.astype(o.dtype)

    out_specs = [_tile_spec(tm, w, colf, nrep) for w, colf, _, _ in outs]
    out_shape = [jax.ShapeDtypeStruct((cf.T, tw), dt) for _, _, tw, dt in outs]
    return pl.pallas_call(
        body, name=name, grid=(cf.T // tm,), in_specs=_row_specs(cf, tm, tiles, params, nrep), out_specs=out_specs,
        out_shape=out_shape, compiler_params=_cp(("arbitrary",)))(*[t[0] for t in tiles], *[p[0] for p in params])


def _row_bwd(cf, name, f, tiles, params, cts, tgrads, tm, nrep=1):
    nt, npar, nc = len(tiles), len(params), len(cts)
    tdiff = [k for k, t in enumerate(tiles) if t[3]]
    pdiff = [k for k, p in enumerate(params) if p[2]]
    nctx = cf.LC // tm

    def body(*refs):
        i = pl.program_id(0)
        pv = [x[...] for x in refs[nt:nt + npar]]
        outs = refs[nt + npar + nc:]
        psum = None
        for r in range(nrep):
            tv = [x[:, _head_cols(t[2], t[1], r)].astype(F32) for x, t in zip(refs[:nt], tiles)]
            cv = tuple(x[:, _head_cols(c[2], c[1], r)].astype(F32) for x, c in zip(refs[nt + npar:nt + npar + nc], cts))

            def g(*diff, tv=tv):
                tv2, pv2 = list(tv), list(pv)
                for k, v in zip(tdiff, diff[:len(tdiff)]):
                    tv2[k] = v
                for k, v in zip(pdiff, diff[len(tdiff):]):
                    pv2[k] = v
                return tuple(f(*tv2, *pv2))

            _, vjp_fn = jax.vjp(g, *[tv[k] for k in tdiff], *[pv[k] for k in pdiff])
            grads = vjp_fn(cv)
            for o, gv, spec in zip(outs[:len(tdiff)], grads[:len(tdiff)], tgrads):
                o[:, _head_cols(spec[1], spec[0], r)] = gv.astype(o.dtype)
            pg = grads[len(tdiff):]
            psum = list(pg) if psum is None else [a + b for a, b in zip(psum, pg)]
        for n_, (o, gv) in enumerate(zip(outs[len(tdiff):], psum)):
            first = (i == 0) if params[pdiff[n_]][1] == 'shared' else jnp.logical_or(i == 0, i == nctx)

            @pl.when(first)
            def _():
                o[...] = gv

            @pl.when(jnp.logical_not(first))
            def _():
                o[...] += gv

    in_specs = _row_specs(cf, tm, tiles, params, nrep)
    in_specs += [_tile_spec(tm, w, colf, nrep) for _, w, colf in cts]
    out_specs = [_tile_spec(tm, w, colf, nrep) for w, colf, _, _ in tgrads]
    out_shape = [jax.ShapeDtypeStruct((cf.T, tw), dt) for _, _, tw, dt in tgrads]
    out_specs += _row_specs(cf, tm, [], [params[k] for k in pdiff], nrep)
    out_shape += [jax.ShapeDtypeStruct(params[k][0].shape, F32) for k in pdiff]
    res = pl.pallas_call(
        body, name=name, grid=(cf.T // tm,), in_specs=in_specs, out_specs=out_specs, out_shape=out_shape,
        compiler_params=_cp(("arbitrary",)))(*[t[0] for t in tiles], *[p[0] for p in params], *[c[0] for c in cts])
    return res[:len(tdiff)], res[len(tdiff):]


def _c0(r):
    return 0


def _col(base):
    def col(r):
        return base + r
    col.base = base
    return col


def _rms(x, g):
    return x * lax.rsqrt(jnp.mean(x * x, axis=-1, keepdims=True) + EPS) * g


def _sigmoid(x):
    return 1.0 / (1.0 + jnp.exp(-x))


def _silu(x):
    return x * _sigmoid(x)


def _f_norm_mod(x, g, sh, sc):
    return (_rms(x, g) * (1 + sc) + sh,)


def _f_resid_norm_mod(x, y, gate, g, sh, sc):
    x1 = x + gate * y
    return (x1, _rms(x1, g) * (1 + sc) + sh)


@jax.custom_vjp
def _swap_halves(t):
    return pltpu.roll(t, HEAD_DIM // 2, axis=1)


def _swap_fwd(t):
    return _swap_halves(t), None


def _swap_bwd(_, g):
    return (pltpu.roll(g, HEAD_DIM // 2, axis=1),)


_swap_halves.defvjp(_swap_fwd, _swap_bwd)


def _rope(t, cs, sn):
    return t * cs + _swap_halves(t) * sn


def _f_prep_norm(t, cs, sn, g):
    return (_rope(_rms(t, g), cs, sn),)


def _f_prep_plain(t, cs, sn):
    return (_rope(t, cs, sn),)


def _f_prep_scaled(t, cs, sn):
    return (_rope(t * (HEAD_DIM ** -0.5), cs, sn),)


def _log_sigmoid(x):
    return jnp.minimum(x, 0.0) - jnp.log(1.0 + jnp.exp(-jnp.abs(x)))


N_DECAY = 8


def _gla_masks(d, width):
    C, SB = GLA_CHUNK, GLA_SUB
    r = lax.broadcasted_iota(jnp.int32, (C, C), 0)
    m = lax.broadcasted_iota(jnp.int32, (C, C), 1)
    rr = lax.broadcasted_iota(jnp.int32, (C, width), 0)
    allowed = (m <= r) if d == 0 else (m >= r)
    blocks, vis = [allowed], []
    for b in range(C // SB):
        blocks.append((m < SB * b) if d == 0 else (m >= SB * (b + 1)))
        vis.append((rr < SB * (b + 1)) if d == 0 else (rr >= SB * b))
    cm = jnp.concatenate([x.astype(F32) for x in blocks] + [jnp.ones((C, C), F32)], axis=0)
    return cm, allowed, vis


def _gla_decays(la, d):
    C, SB = GLA_CHUNK, GLA_SUB
    nsb = C // SB
    cm, _, vis = _gla_masks(d, la.shape[-1])
    cums = jnp.dot(cm, la, precision=lax.Precision.HIGH, preferred_element_type=F32)
    cum, tot = cums[0:C], cums[(1 + nsb) * C:]
    refs = [cums[(1 + b) * C:(2 + b) * C] for b in range(nsb)]
    e1 = jnp.concatenate([jnp.exp(cum[b * SB:(b + 1) * SB] - refs[b][b * SB:(b + 1) * SB]) for b in range(nsb)], axis=0)
    e2 = [jnp.where(vis[b], jnp.exp(jnp.where(vis[b], refs[b] - cum, 0.0)), 0.0) for b in range(nsb)]
    return [e1] + e2 + [jnp.exp(cum), jnp.exp(tot - cum), jnp.exp(tot)]


def _f_gla_pre(ga, *per_pair):
    gab = ga.astype(BF16)
    outs = [[], []]
    for p in range(len(per_pair) // 4):
        gf, gb, bf, bb = per_pair[4 * p:4 * p + 4]
        for d, (gm, bm) in enumerate(((gf, bf), (gb, bb))):
            la = _log_sigmoid(jnp.dot(gab, gm.astype(BF16), preferred_element_type=F32) + bm) / GLA_TAU
            outs[d] += _gla_decays(la, d)
    return tuple(jnp.concatenate(o, axis=-1) for o in outs)


def _f_gated_norm(o, g, n):
    return (_rms(o, n) * _silu(g),)


def _f_first(x, g, sh, sc):
    return (x, _rms(x, g) * (1 + sc) + sh)


def _loss_grad(cf, x1, yff, gate, gfin, tgt):
    tm, T, D = cf.TM, cf.T, cf.D
    nctx = cf.LC // tm

    def lossf(x1v, yv, gt, gf, tg):
        y = _rms(x1v + gt * yv, gf)
        e = y - tg
        return 0.5 * jnp.sum(jnp.mean(e * e, axis=-1, keepdims=True), axis=0, keepdims=True)

    def body(x1_ref, y_ref, gt_ref, gf_ref, tg_ref, dx_ref, dy_ref, dgt_ref, dgf_ref, ls_ref):
        i = pl.program_id(0)
        lat = (i >= nctx).astype(F32)
        val, vjp_fn = jax.vjp(lossf, x1_ref[...], y_ref[...].astype(F32), gt_ref[...], gf_ref[...], tg_ref[...])
        dx, dy, dgt, dgf, _ = vjp_fn(jnp.ones((1, 1), F32) * lat)
        dx_ref[...] = dx
        dy_ref[...] = dy.astype(dy_ref.dtype)
        first_s = jnp.logical_or(i == 0, i == nctx)

        @pl.when(first_s)
        def _():
            dgt_ref[...] = dgt

        @pl.when(jnp.logical_not(first_s))
        def _():
            dgt_ref[...] += dgt

        @pl.when(i == 0)
        def _():
            dgf_ref[...] = dgf
            ls_ref[...] = jnp.zeros_like(ls_ref) + val * lat

        @pl.when(i != 0)
        def _():
            dgf_ref[...] += dgf
            ls_ref[...] += val * lat

    row = pl.BlockSpec((tm, D), lambda i: (i, 0))
    strm = pl.BlockSpec((None, 1, D), lambda i: (jnp.where(i >= nctx, 1, 0), 0, 0))
    one = pl.BlockSpec((1, D), lambda i: (0, 0))
    return pl.pallas_call(
        body, name="loss_grad", grid=(T // tm,), in_specs=[row, row, strm, one, row],
        out_specs=[row, row, strm, one, pl.BlockSpec((8, LANE), lambda i: (0, 0))],
        out_shape=[jax.ShapeDtypeStruct((T, D), F32), jax.ShapeDtypeStruct((T, D), BF16),
                   jax.ShapeDtypeStruct((2, 1, D), F32), jax.ShapeDtypeStruct((1, D), F32),
                   jax.ShapeDtypeStruct((8, LANE), F32)],
        compiler_params=_cp(("arbitrary",)))(x1, yff, gate, gfin, tgt)


def _att_mask(cf, is_latent, rows):
    col = lax.broadcasted_iota(jnp.int32, (rows, cf.T), 1)
    return jnp.logical_or(col < cf.LC, is_latent)


def _stack_heads(ref, G):
    return jnp.concatenate([ref[:, j * LANE:(j + 1) * LANE] for j in range(G)], axis=0)


def _att_probs(q, k, mask):
    s = lax.dot_general(q, k, (_DN['nt'], ((), ())), preferred_element_type=F32) * (HEAD_DIM ** -0.5)
    s = jnp.where(mask, s, NEG)
    e = jnp.exp(s - jnp.max(s, axis=-1, keepdims=True))
    return e / jnp.sum(e, axis=-1, keepdims=True)


def _att_fwd(cf, q, k, z):
    tq, T, G = cf.TQ, cf.T, cf.G
    vb = cf.OFF['av'] // LANE

    def body(q_ref, k_ref, v_ref, o_ref):
        mask = _att_mask(cf, pl.program_id(1) >= cf.LC // tq, tq)
        kv, vv = k_ref[...], v_ref[...].astype(BF16)
        for j in range(G):
            p = _att_probs(q_ref[:, j * LANE:(j + 1) * LANE], kv, mask)
            o_ref[:, j * LANE:(j + 1) * LANE] = jnp.dot(p.astype(BF16), vv, preferred_element_type=F32).astype(o_ref.dtype)

    return pl.pallas_call(
        body, name="att_fwd", grid=(cf.HKV, T // tq),
        in_specs=[pl.BlockSpec((tq, G * LANE), lambda g, i: (i, g)), pl.BlockSpec((T, LANE), lambda g, i: (0, g)),
                  pl.BlockSpec((T, LANE), lambda g, i: (0, vb + g))],
        out_specs=pl.BlockSpec((tq, G * LANE), lambda g, i: (i, g)),
        out_shape=jax.ShapeDtypeStruct((T, cf.HQ * LANE), BF16), compiler_params=_cp(("arbitrary", "arbitrary")))(q, k, z)


def _att_bwd(cf, q, k, z, dcat):
    tq, T, G = cf.TM, cf.T, cf.G
    vb = cf.OFF['av'] // LANE
    sc = HEAD_DIM ** -0.5

    def body(q_ref, k_ref, v_ref, do_ref, dq_ref, dk_ref, dv_ref):
        i = pl.program_id(1)
        mask = _att_mask(cf, i >= cf.LC // tq, G * tq)
        kv, vv = k_ref[...], v_ref[...].astype(BF16)
        q4, do4 = _stack_heads(q_ref, G), _stack_heads(do_ref, G)
        p = _att_probs(q4, kv, mask)
        dv = lax.dot_general(p.astype(BF16), do4, (_DN['tn'], ((), ())), preferred_element_type=F32)
        dp = lax.dot_general(do4, vv, (_DN['nt'], ((), ())), preferred_element_type=F32)
        dsb = (p * (dp - jnp.sum(dp * p, axis=-1, keepdims=True)) * sc).astype(BF16)
        dq = jnp.dot(dsb, kv, preferred_element_type=F32)
        dk = lax.dot_general(dsb, q4, (_DN['tn'], ((), ())), preferred_element_type=F32)
        for j in range(G):
            dq_ref[:, j * LANE:(j + 1) * LANE] = dq[j * tq:(j + 1) * tq]

        @pl.when(i == 0)
        def _():
            dk_ref[...] = dk
            dv_ref[...] = dv

        @pl.when(i != 0)
        def _():
            dk_ref[...] += dk
            dv_ref[...] += dv

    qs = pl.BlockSpec((tq, G * LANE), lambda g, i: (i, g))
    ks = pl.BlockSpec((T, LANE), lambda g, i: (0, g))
    return pl.pallas_call(
        body, name="att_bwd", grid=(cf.HKV, T // tq),
        in_specs=[qs, ks, pl.BlockSpec((T, LANE), lambda g, i: (0, vb + g)), qs],
        out_specs=[qs, ks, ks],
        out_shape=[jax.ShapeDtypeStruct((T, cf.HQ * LANE), F32), jax.ShapeDtypeStruct((T, cf.HKV * LANE), F32),
                   jax.ShapeDtypeStruct((T, cf.HKV * LANE), F32)],
        compiler_params=_cp(("arbitrary", "arbitrary")))(q, k, z, dcat)


def _ret_masks(cf, i, tq, lgf, lgb):
    T, LC = cf.T, cf.LC
    row = (lax.broadcasted_iota(jnp.int32, (tq, 1), 0) + i * tq)
    col = lax.broadcasted_iota(jnp.int32, (1, T), 1)

    def pb(n):
        return jnp.where(n < LC, LC - 1 - n, T + LC - 1 - n).astype(F32)

    relf = row.astype(F32) - col.astype(F32)
    relb = pb(row) - pb(col)
    rf, rb = jnp.maximum(relf, 0.0), jnp.maximum(relb, 0.0)
    mf = jnp.where(relf >= 0, jnp.exp(lgf * rf), 0.0)
    mb = jnp.where(relb >= 0, jnp.exp(lgb * rb), 0.0)
    return mf, mb, rf, rb


def _ret_fwd(cf, q, k, z, lg):
    tq, T = cf.TQ, cf.T
    vb = cf.OFF['rv'] // LANE

    def body(lg_ref, q_ref, k_ref, v_ref, o_ref):
        h, i = pl.program_id(0), pl.program_id(1)
        mf, mb, _, _ = _ret_masks(cf, i, tq, lg_ref[0, h], lg_ref[1, h])
        a = lax.dot_general(q_ref[...], k_ref[...], (_DN['nt'], ((), ())), preferred_element_type=F32)
        p = (a * (mf + mb)).astype(BF16)
        o_ref[...] = jnp.dot(p, v_ref[...].astype(BF16), preferred_element_type=F32)

    return pl.pallas_call(
        body, name="ret_fwd", grid=(cf.HR, T // tq),
        in_specs=[pl.BlockSpec(memory_space=pltpu.SMEM), pl.BlockSpec((tq, LANE), lambda h, i: (i, h)),
                  pl.BlockSpec((T, LANE), lambda h, i: (0, h)), pl.BlockSpec((T, LANE), lambda h, i: (0, vb + h))],
        out_specs=pl.BlockSpec((tq, LANE), lambda h, i: (i, h)),
        out_shape=jax.ShapeDtypeStruct((T, cf.HR * LANE), F32), compiler_params=_cp(("arbitrary", "arbitrary")))(lg, q, k, z)


def _ret_bwd(cf, q, k, z, lg, do):
    tq, T = cf.TQ, cf.T
    vb = cf.OFF['rv'] // LANE

    def body(lg_ref, q_ref, k_ref, v_ref, do_ref, dq_ref, dk_ref, dv_ref, dlg_ref):
        h, i = pl.program_id(0), pl.program_id(1)
        mf, mb, rf, rb = _ret_masks(cf, i, tq, lg_ref[0, h], lg_ref[1, h])
        qv, kv, vv = q_ref[...], k_ref[...], v_ref[...].astype(BF16)
        dob = do_ref[...].astype(BF16)
        a = lax.dot_general(qv, kv, (_DN['nt'], ((), ())), preferred_element_type=F32)
        m = mf + mb
        p = (a * m).astype(BF16)
        dv = lax.dot_general(p, dob, (_DN['tn'], ((), ())), preferred_element_type=F32)
        dp = lax.dot_general(dob, vv, (_DN['nt'], ((), ())), preferred_element_type=F32)
        da = (dp * m).astype(BF16)
        dq_ref[...] = jnp.dot(da, kv, preferred_element_type=F32)
        dk = lax.dot_general(da, qv, (_DN['tn'], ((), ())), preferred_element_type=F32)
        dm = dp * a
        dlf = jnp.sum(jnp.sum(dm * mf * rf, axis=-1, keepdims=True), axis=0, keepdims=True)
        dlb = jnp.sum(jnp.sum(dm * mb * rb, axis=-1, keepdims=True), axis=0, keepdims=True)
        rid = lax.broadcasted_iota(jnp.int32, (8, LANE), 0)
        dl = jnp.where(rid == 0, dlf, jnp.where(rid == 1, dlb, 0.0))

        @pl.when(i == 0)
        def _():
            dk_ref[...] = dk
            dv_ref[...] = dv
            dlg_ref[...] = dl

        @pl.when(i != 0)
        def _():
            dk_ref[...] += dk
            dv_ref[...] += dv
            dlg_ref[...] += dl

    qs = pl.BlockSpec((tq, LANE), lambda h, i: (i, h))
    ks = pl.BlockSpec((T, LANE), lambda h, i: (0, h))
    return pl.pallas_call(
        body, name="ret_bwd", grid=(cf.HR, T // tq),
        in_specs=[pl.BlockSpec(memory_space=pltpu.SMEM), qs, ks, pl.BlockSpec((T, LANE), lambda h, i: (0, vb + h)), qs],
        out_specs=[qs, ks, ks, pl.BlockSpec((None, 8, LANE), lambda h, i: (h, 0, 0))],
        out_shape=[jax.ShapeDtypeStruct((T, cf.HR * LANE), F32)] * 3 + [jax.ShapeDtypeStruct((cf.HR, 8, LANE), F32)],
        compiler_params=_cp(("arbitrary", "arbitrary")))(lg, q, k, z, do)


def _gla_step(q, k, v, es, st, lmask, allowed):
    C, SB = GLA_CHUNK, GLA_SUB
    nsb = C // SB
    e1, e2, e3, e4, e5 = es[0], es[1:1 + nsb], es[1 + nsb], es[2 + nsb], es[3 + nsb]
    qs = q * lmask * ((HEAD_DIM // 2) ** -0.5)
    ks = k * lmask
    qt = qs * e1
    rows = [lax.dot_general(qt[b * SB:(b + 1) * SB], ks * e2[b], (_DN['nt'], ((), ())), precision=lax.Precision.HIGH,
                            preferred_element_type=F32) for b in range(nsb)]
    att = jnp.where(allowed, jnp.concatenate(rows, axis=0), 0.0)
    o = jnp.dot(att.astype(BF16), v.astype(BF16), preferred_element_type=F32)
    o += lax.dot_general((qs * e3).astype(BF16), st.astype(BF16), (_DN['nt'], ((), ())), preferred_element_type=F32)
    kd = (ks * e4).astype(BF16)
    st_new = st * jnp.concatenate([e5, e5], axis=0) + lax.dot_general(v.astype(BF16), kd, (_DN['tn'], ((), ())), preferred_element_type=F32)
    return o, st_new


def _gla_allowed(d):
    r = lax.broadcasted_iota(jnp.int32, (GLA_CHUNK, GLA_CHUNK), 0)
    m = lax.broadcasted_iota(jnp.int32, (GLA_CHUNK, GLA_CHUNK), 1)
    return (m <= r) if d == 0 else (m >= r)


def _gla_chunk_id(cf, s, d):
    if d == 0:
        return s
    nct, nc = cf.LC // GLA_CHUNK, cf.T // GLA_CHUNK
    return jnp.where(s < nct, nct - 1 - s, nc + nct - 1 - s)


def _gla_lmask(h):
    return (lax.broadcasted_iota(jnp.int32, (1, LANE), 1) // (LANE // 2) == h).astype(F32)


_GLA_CHAINS = [(h, d) for h in range(2) for d in range(2)]


def _gla_row_specs(cf, nc, reverse):
    def rowblk(s, d):
        return _gla_chunk_id(cf, nc - 1 - s if reverse else s, d)

    def spec(width, base, d, per_pair=1):
        return pl.BlockSpec((GLA_CHUNK, width), functools.partial(lambda p, s, base, d: (rowblk(s, d), base + per_pair * p), base=base, d=d))

    def state(d):
        return pl.BlockSpec((2, None, LANE, LANE), functools.partial(lambda p, s, d: (p, rowblk(s, d), 0, 0), d=d))

    return spec, state


def _gla_fwd(cf, z, ef, eb):
    T, C = cf.T, GLA_CHUNK
    nc = T // C
    qb, kb, vb = cf.OFF['gq'] // LANE, cf.OFF['gk'] // LANE, cf.OFF['gv'] // (2 * LANE)
    spec, state = _gla_row_specs(cf, nc, False)

    def body(qf, kf, vf, e_f, qb_, kb_, vb_, e_b, of, sf, ob, sb, st_scr):
        @pl.when(pl.program_id(1) == 0)
        def _():
            st_scr[...] = jnp.zeros_like(st_scr)

        io = [(qf, kf, vf, e_f, of, sf), (qb_, kb_, vb_, e_b, ob, sb)]
        for ci, (h, d) in enumerate(_GLA_CHAINS):
            q, k, v, e, o_ref, s_ref = io[d]
            cols = slice(h * LANE, (h + 1) * LANE)
            es = [e[:, n * LANE:(n + 1) * LANE] for n in range(N_DECAY)]
            st = st_scr[ci]
            s_ref[h] = st
            o, stn = _gla_step(q[...], k[...], v[:, cols], es, st, _gla_lmask(h), _gla_allowed(d))
            st_scr[ci] = stn
            o_ref[:, cols] = o

    ins, outs = [], []
    for d in range(2):
        ins += [spec(LANE, qb, d), spec(LANE, kb, d), spec(2 * LANE, vb, d), spec(N_DECAY * LANE, 0, d)]
        outs += [spec(2 * LANE, 0, d), state(d)]
    oshape = [jax.ShapeDtypeStruct((T, cf.HG * LANE), F32), jax.ShapeDtypeStruct((cf.HG, nc, LANE, LANE), F32)]
    return pl.pallas_call(
        body, name="gla_fwd", grid=(cf.HG // 2, nc), in_specs=ins, out_specs=outs, out_shape=oshape * 2,
        scratch_shapes=[pltpu.VMEM((4, LANE, LANE), F32)],
        compiler_params=_cp(("arbitrary", "arbitrary")))(z, z, z, ef, z, z, z, eb)


def _gla_bwd(cf, z, ef, eb, sf, sb, do):
    T, C = cf.T, GLA_CHUNK
    nc = T // C
    qb, kb, vb = cf.OFF['gq'] // LANE, cf.OFF['gk'] // LANE, cf.OFF['gv'] // (2 * LANE)
    spec, state = _gla_row_specs(cf, nc, True)

    def body(*refs):
        ins = [refs[0:6], refs[6:12]]
        outs = [refs[12:16], refs[16:20]]
        dst_scr = refs[20]

        @pl.when(pl.program_id(1) == 0)
        def _():
            dst_scr[...] = jnp.zeros_like(dst_scr)

        acc = [None, None]
        for ci, (h, d) in enumerate(_GLA_CHAINS):
            q, k, v, e, s_ref, do_ref = ins[d]
            cols = slice(h * LANE, (h + 1) * LANE)
            es = [e[:, n * LANE:(n + 1) * LANE] for n in range(N_DECAY)]
            step = functools.partial(_gla_step, lmask=_gla_lmask(h), allowed=_gla_allowed(d))
            _, vjp_fn = jax.vjp(step, q[...], k[...], v[:, cols], es, s_ref[h])
            dq, dk, dv, des, dst = vjp_fn((do_ref[:, cols], dst_scr[ci]))
            dst_scr[ci] = dst
            outs[d][2][:, cols] = dv
            part = [dq, dk] + list(des)
            acc[d] = part if acc[d] is None else [a + b for a, b in zip(acc[d], part)]
        for d in range(2):
            dq_ref, dk_ref, _, de_ref = outs[d]
            dq_ref[...] = acc[d][0]
            dk_ref[...] = acc[d][1]
            for n in range(N_DECAY):
                de_ref[:, n * LANE:(n + 1) * LANE] = acc[d][2 + n]

    in_specs, out_specs = [], []
    for d in range(2):
        in_specs += [spec(LANE, qb, d), spec(LANE, kb, d), spec(2 * LANE, vb, d), spec(N_DECAY * LANE, 0, d), state(d), spec(2 * LANE, 0, d)]
        out_specs += [spec(LANE, 0, d), spec(LANE, 0, d), spec(2 * LANE, 0, d), spec(N_DECAY * LANE, 0, d)]
    npair = cf.HG // 2
    oshape = [jax.ShapeDtypeStruct((T, npair * LANE), F32), jax.ShapeDtypeStruct((T, npair * LANE), F32),
              jax.ShapeDtypeStruct((T, cf.HG * LANE), F32), jax.ShapeDtypeStruct((T, npair * N_DECAY * LANE), F32)]
    return pl.pallas_call(
        body, name="gla_bwd", grid=(npair, nc), in_specs=in_specs, out_specs=out_specs, out_shape=oshape * 2,
        scratch_shapes=[pltpu.VMEM((4, LANE, LANE), F32)],
        compiler_params=_cp(("arbitrary", "arbitrary")))(z, z, z, ef, sf, do, z, z, z, eb, sb, do)


def _conv_parts(cf, a, w_ref):
    T, LC = cf.T, cf.LC
    rid = lax.broadcasted_iota(jnp.int32, a.shape, 0)
    first = jnp.logical_or(rid == 0, rid == LC)
    last = jnp.logical_or(rid == LC - 1, rid == T - 1)
    ap = jnp.where(first, 0.0, pltpu.roll(a, 1, axis=0))
    an = jnp.where(last, 0.0, pltpu.roll(a, T - 1, axis=0))
    w0, w1, w2, b = w_ref[0:1, :], w_ref[1:2, :], w_ref[2:3, :], w_ref[3:4, :]
    ac = ap * w0 + a * w1 + an * w2 + b
    return ap, an, ac, first, last, (w0, w1, w2)


def _conv_fwd(cf, u, wb):
    T, Fd = cf.T, cf.F
    tc = _tile(Fd, 512)
    nj = Fd // tc

    def body(a_ref, v_ref, w_ref, t_ref):
        _, _, ac, _, _, _ = _conv_parts(cf, a_ref[...], w_ref)
        t_ref[...] = (_silu(ac) * v_ref[...]).astype(t_ref.dtype)

    return pl.pallas_call(
        body, name="conv_fwd", grid=(nj,),
        in_specs=[pl.BlockSpec((T, tc), lambda j: (0, j)), pl.BlockSpec((T, tc), lambda j: (0, nj + j)),
                  pl.BlockSpec((8, tc), lambda j: (0, j))],
        out_specs=pl.BlockSpec((T, tc), lambda j: (0, j)), out_shape=jax.ShapeDtypeStruct((T, Fd), BF16),
        compiler_params=_cp(("parallel",)))(u, u, wb)


def _conv_bwd(cf, u, wb, dt):
    T, Fd = cf.T, cf.F
    tc = _tile(Fd, 256)
    nj = Fd // tc

    def body(a_ref, v_ref, w_ref, dt_ref, du_ref, dw_ref):
        a, v, dtv = a_ref[...], v_ref[...], dt_ref[...].astype(F32)
        ap, an, ac, first, last, (w0, w1, w2) = _conv_parts(cf, a, w_ref)
        sg = _sigmoid(ac)
        du_ref[1] = (dtv * ac * sg).astype(du_ref.dtype)
        dac = dtv * v * (sg * (1.0 + ac * (1.0 - sg)))
        from_next = pltpu.roll(jnp.where(first, 0.0, dac), T - 1, axis=0)
        from_prev = pltpu.roll(jnp.where(last, 0.0, dac), 1, axis=0)
        du_ref[0] = (dac * w1 + from_next * w0 + from_prev * w2).astype(du_ref.dtype)
        rows = [jnp.sum(dac * ap, axis=0, keepdims=True), jnp.sum(dac * a, axis=0, keepdims=True),
                jnp.sum(dac * an, axis=0, keepdims=True), jnp.sum(dac, axis=0, keepdims=True)]
        rid = lax.broadcasted_iota(jnp.int32, (8, tc), 0)
        dw = jnp.zeros((8, tc), F32)
        for n_, rw in enumerate(rows):
            dw = jnp.where(rid == n_, rw, dw)
        dw_ref[...] = dw

    col = pl.BlockSpec((T, tc), lambda j: (0, j))
    return pl.pallas_call(
        body, name="conv_bwd", grid=(nj,),
        in_specs=[col, pl.BlockSpec((T, tc), lambda j: (0, nj + j)), pl.BlockSpec((8, tc), lambda j: (0, j)), col],
        out_specs=[pl.BlockSpec((2, T, tc), lambda j: (0, 0, j)), pl.BlockSpec((8, tc), lambda j: (0, j))],
        out_shape=[jax.ShapeDtypeStruct((2, T, Fd), BF16), jax.ShapeDtypeStruct((8, Fd), F32)],
        compiler_params=_cp(("parallel",)))(u, u, wb, dt)


def _me():
    x, y, c = lax.axis_index("x"), lax.axis_index("y"), lax.axis_index("c")
    return x, y, c, 4 * x + 2 * y + c


def _peer(x, y, c, k):
    px = 1 - x if (k >> 2) & 1 else x
    py = 1 - y if (k >> 1) & 1 else y
    pc = 1 - c if k & 1 else c
    return (px, py, pc), 4 * px + 2 * py + pc


def _rcopy(src, dst, ss, rs, tgt):
    return pltpu.make_async_remote_copy(src_ref=src, dst_ref=dst, send_sem=ss, recv_sem=rs, device_id=tgt,
                                        device_id_type=pl.DeviceIdType.MESH)


def _ag_small(name, v):
    R, Cc = v.shape

    def body(v_ref, o_ref, ssem, rsem, lsem):
        x, y, c, me = _me()
        loc = pltpu.make_async_copy(v_ref, o_ref.at[me], lsem)
        loc.start()
        sends = []
        for k in range(1, N_DEV):
            tgt, _ = _peer(x, y, c, k)
            cp = _rcopy(v_ref, o_ref.at[me], ssem.at[k - 1], rsem.at[k - 1], tgt)
            cp.start()
            sends.append(cp)
        for k in range(1, N_DEV):
            tgt, pi = _peer(x, y, c, k)
            _rcopy(v_ref, o_ref.at[pi], ssem.at[k - 1], rsem.at[k - 1], tgt).wait_recv()
        for cp in sends:
            cp.wait_send()
        loc.wait()

    vm = pl.BlockSpec(memory_space=pltpu.VMEM)
    return pl.pallas_call(
        body, name=name, in_specs=[vm], out_specs=vm, out_shape=jax.ShapeDtypeStruct((N_DEV, R, Cc), v.dtype),
        scratch_shapes=[pltpu.SemaphoreType.DMA((N_DEV - 1,)), pltpu.SemaphoreType.DMA((N_DEV - 1,)), pltpu.SemaphoreType.DMA],
        compiler_params=pltpu.CompilerParams(vmem_limit_bytes=VMEM_LIMIT))(v)


_KINDS = ['in', 'out', 'up', 'down']


def _shard_shape(cf, kind):
    D, Fd = cf.D, cf.F
    return {'in': (D, cf.NINS), 'out': (D // N_DEV, D), 'up': (D, 2 * Fd // N_DEV), 'down': (Fd // N_DEV, D)}[kind]


def _whole_shape(cf, kind):
    D, Fd = cf.D, cf.F
    return {'in': (N_DEV, D, cf.NINS), 'out': (D, D), 'up': (D, 2 * Fd), 'down': (Fd, D)}[kind]


def _part(ref, cf, kind, idx):
    r, cdim = _shard_shape(cf, kind)
    if kind == 'in':
        return ref.at[idx]
    if kind == 'up':
        return ref.at[:, pl.ds(pl.multiple_of(idx * cdim, cdim), cdim)]
    return ref.at[pl.ds(pl.multiple_of(idx * r, r), r), :]


N_BARRIER_IDS = 8


def _handshake(x, y, c):
    barrier = pltpu.get_barrier_semaphore()
    for k in range(1, N_DEV):
        pl.semaphore_signal(barrier, inc=1, device_id=_peer(x, y, c, k)[0], device_id_type=pl.DeviceIdType.MESH)
    pl.semaphore_wait(barrier, N_DEV - 1)


def _seq_kernel(body, name, seq, out_type):
    return pl.kernel(
        body, out_type=out_type, mesh=plsc.ScalarSubcoreMesh(axis_name="sq", num_cores=1), name=name,
        scratch_types=[pltpu.SemaphoreType.DMA((N_DEV - 1,)), pltpu.SemaphoreType.DMA((N_DEV - 1,)), pltpu.SemaphoreType.DMA],
        compiler_params=pltpu.CompilerParams(collective_id=seq % N_BARRIER_IDS))


def _seq_gather(cf, kind, l, seq, src):
    def body(src_ref, land_ref, ssem, rsem, lsem):
        x, y, c, me = _me()
        _handshake(x, y, c)
        sib = (x, y, 1 - c)
        chips = [(1 - x, y), (x, 1 - y), (1 - x, 1 - y)]

        def blk(px, py, pc):
            return _part(land_ref, cf, kind, 4 * px + 2 * py + pc)

        src, mine = src_ref.at[l], blk(x, y, c)
        loc = pltpu.make_async_copy(src, mine, lsem)
        loc.start()
        loc.wait()
        first = [_rcopy(src, mine, ssem.at[0], rsem.at[0], sib)]
        first += [_rcopy(src, mine, ssem.at[1 + j], rsem.at[1 + j], (*chip, c)) for j, chip in enumerate(chips)]
        for cp in first:
            cp.start()
        passed = [_rcopy(blk(*chip, c), blk(*chip, c), ssem.at[4 + j], rsem.at[4 + j], sib) for j, chip in enumerate(chips)]
        for j, chip in enumerate(chips):
            _rcopy(src, blk(*chip, c), ssem.at[1 + j], rsem.at[1 + j], (*chip, c)).wait_recv()
            passed[j].start()
        _rcopy(src, blk(x, y, 1 - c), ssem.at[0], rsem.at[0], sib).wait_recv()
        for j, chip in enumerate(chips):
            _rcopy(src, blk(*chip, 1 - c), ssem.at[4 + j], rsem.at[4 + j], sib).wait_recv()
        for cp in first + passed:
            cp.wait_send()

    return _seq_kernel(body, "seq_gather_%s_%d" % (kind, l), seq, jax.ShapeDtypeStruct(_whole_shape(cf, kind), BF16))(src)


def _seq_scatter(cf, kind, l, seq, g):
    def body(g_ref, recv_ref, ssem, rsem, lsem):
        x, y, c, me = _me()
        _handshake(x, y, c)
        loc = pltpu.make_async_copy(_rs_slab(g_ref, cf, kind, me), recv_ref.at[me], lsem)
        loc.start()
        loc.wait()
        sends = []
        for k in range(1, N_DEV):
            tgt, pi = _peer(x, y, c, k)
            sends.append(_rcopy(_rs_slab(g_ref, cf, kind, pi), recv_ref.at[me], ssem.at[k - 1], rsem.at[k - 1], tgt))
            sends[-1].start()
        for k in range(1, N_DEV):
            tgt, pi = _peer(x, y, c, k)
            _rcopy(_rs_slab(g_ref, cf, kind, pi), recv_ref.at[pi], ssem.at[k - 1], rsem.at[k - 1], tgt).wait_recv()
        for cp in sends:
            cp.wait_send()

    return _seq_kernel(body, "seq_scatter_%s_%d" % (kind, l), seq, jax.ShapeDtypeStruct((N_DEV,) + _shard_shape(cf, kind), BF16))(g)


def _rs_slab(ref, cf, kind, j):
    return ref.at[j] if kind in ('in', 'up') else _part(ref, cf, kind, j)


def _adam_vals(w, g, m, v):
    m2 = ADAM_B1 * m + (1.0 - ADAM_B1) * g
    v2 = ADAM_B2 * v + (1.0 - ADAM_B2) * (g * g)
    mh = m2 / (1.0 - ADAM_B1 ** ADAM_STEP)
    vh = v2 / (1.0 - ADAM_B2 ** ADAM_STEP)
    return -ADAM_LR * (mh / (jnp.sqrt(vh) + ADAM_EPS) + ADAM_WD * w), m2, v2


def _row_tile(R, Cc, budget_elems):
    t = max(16, min(R, (budget_elems // max(Cc, 1)) // 16 * 16))
    while t > 16 and R % t:
        t -= 16
    return t if R % t == 0 else R


def _cast_bf16(name, w, after):
    Dp, R, Cc = w.shape
    tr = _row_tile(R, Cc, 1 << 20)

    def body(w_ref, after_ref, o_ref):
        o_ref[...] = w_ref[...].astype(BF16)

    spec = pl.BlockSpec((None, tr, Cc), lambda l, i: (l, i, 0))
    return pl.pallas_call(body, name=name, grid=(Dp, R // tr), in_specs=[spec, pl.BlockSpec(memory_space=pl.ANY)], out_specs=spec,
                          out_shape=jax.ShapeDtypeStruct(w.shape, BF16), compiler_params=_cp(("parallel", "parallel")))(w, after)


def _unshard_in(cf, name, g):
    D, ns, nz = cf.D, cf.NINS, cf.NZ
    tr = _tile(D, 256, 16)

    def body(g_ref, o_ref):
        for j in range(N_DEV):
            o_ref[:, ns * j:ns * (j + 1)] = g_ref[j]
        o_ref[:, N_DEV * ns:] = jnp.zeros((tr, nz - N_DEV * ns), o_ref.dtype)

    return pl.pallas_call(body, name=name, grid=(D // tr,), in_specs=[pl.BlockSpec((N_DEV, tr, ns), lambda i: (0, i, 0))],
                          out_specs=pl.BlockSpec((tr, nz), lambda i: (i, 0)), out_shape=jax.ShapeDtypeStruct((D, nz), g.dtype),
                          compiler_params=_cp(("parallel",)))(g)


def _slabs_in(cf, name, gw):
    D, ns, nz = cf.D, cf.NINS, cf.NZ
    tr = _tile(D, 256, 16)

    def body(x_ref, o_ref):
        for j in range(N_DEV):
            o_ref[j] = x_ref[:, ns * j:ns * (j + 1)]

    return pl.pallas_call(body, name=name, grid=(D // tr,), in_specs=[pl.BlockSpec((tr, nz), lambda i: (i, 0))],
                          out_specs=pl.BlockSpec((N_DEV, tr, ns), lambda i: (0, i, 0)), out_shape=jax.ShapeDtypeStruct((N_DEV, D, ns), gw.dtype),
                          compiler_params=_cp(("parallel",)))(gw)


def _sum_adam(name, recv, w, m, v, layer, prev, after):
    Dp, R, Cc = w.shape
    tr = _row_tile(R, Cc, 1 << 18)

    def body(r_ref, w_ref, m_ref, v_ref, *rest):
        g_ref, d_ref, mo_ref, vo_ref = rest[-4:]
        g = r_ref[0].astype(F32)
        for s in range(1, N_DEV):
            g = g + r_ref[s].astype(F32)
        dl, m2, v2 = _adam_vals(w_ref[...], g, m_ref[...], v_ref[...])
        g_ref[...] = g
        d_ref[...] = dl
        mo_ref[...] = m2
        vo_ref[...] = v2

    spec = pl.BlockSpec((None, tr, Cc), lambda i: (layer, i, 0))
    anyspec = pl.BlockSpec(memory_space=pl.ANY)
    extra = [after] + (list(prev) if prev is not None else [])
    aliases = {5 + n: n for n in range(4)} if prev is not None else {}
    return pl.pallas_call(body, name=name, grid=(R // tr,), in_specs=[pl.BlockSpec((N_DEV, tr, Cc), lambda i: (0, i, 0)), spec, spec, spec] + [anyspec] * len(extra),
                          out_specs=[spec] * 4, out_shape=[jax.ShapeDtypeStruct(w.shape, F32)] * 4, input_output_aliases=aliases,
                          compiler_params=_cp(("parallel",)))(recv, w, m, v, *extra)


def _adam(name, w, g, m, v):
    R, Cc = w.shape
    tr = _row_tile(R, Cc, 1 << 18)

    def body(w_ref, g_ref, m_ref, v_ref, d_ref, mo_ref, vo_ref):
        dl, m2, v2 = _adam_vals(w_ref[...], g_ref[...], m_ref[...], v_ref[...])
        d_ref[...] = dl
        mo_ref[...] = m2
        vo_ref[...] = v2

    spec = pl.BlockSpec((tr, Cc), lambda i: (i, 0))
    return pl.pallas_call(body, name=name, grid=(R // tr,), in_specs=[spec] * 4, out_specs=[spec] * 3,
                          out_shape=[jax.ShapeDtypeStruct(w.shape, F32)] * 3, compiler_params=_cp(("parallel",)))(w, g, m, v)


def _sum8(name, a):
    n, R, Cc = a.shape

    def body(a_ref, o_ref):
        s = a_ref[0]
        for k in range(1, n):
            s = s + a_ref[k]
        o_ref[...] = s

    return pl.pallas_call(body, name=name, in_specs=[pl.BlockSpec(memory_space=pltpu.VMEM)],
                          out_specs=pl.BlockSpec(memory_space=pltpu.VMEM), out_shape=jax.ShapeDtypeStruct((R, Cc), F32),
                          compiler_params=pltpu.CompilerParams(vmem_limit_bytes=VMEM_LIMIT))(a)


def _ada_fwd(cf, c9, ada_w):
    D = cf.D
    NS = ada_w.shape[-1]
    tk = _tile(D, 512)
    nk = D // tk

    def body(c_ref, w_ref, o_ref):
        kk = pl.program_id(1)
        s = _silu(c_ref[...]).astype(BF16)
        part = jnp.dot(s, w_ref[...].astype(BF16), preferred_element_type=F32)

        @pl.when(kk == 0)
        def _():
            o_ref[...] = part

        @pl.when(kk != 0)
        def _():
            o_ref[...] += part

    return pl.pallas_call(
        body, name="ada_fwd", grid=(DEPTH, nk),
        in_specs=[pl.BlockSpec((16, tk), lambda l, k: (0, k)), pl.BlockSpec((None, tk, NS), lambda l, k: (l, k, 0))],
        out_specs=pl.BlockSpec((None, 16, NS), lambda l, k: (l, 0, 0)),
        out_shape=jax.ShapeDtypeStruct((DEPTH, 16, NS), F32), compiler_params=_cp(("parallel", "arbitrary")))(c9, ada_w)


def _ada_bwd(cf, c9, ada_w, dm9):
    D = cf.D
    NS = ada_w.shape[-1]
    tk = _tile(D, 512)
    nk = D // tk

    def body(c_ref, w_ref, dm_ref, gw_ref, ds_ref):
        cv = c_ref[...]
        sg = _sigmoid(cv)
        dmb = dm_ref[...].astype(BF16)
        gw_ref[...] = lax.dot_general((cv * sg).astype(BF16), dmb, (_DN['tn'], ((), ())), preferred_element_type=F32)
        ds = lax.dot_general(dmb, w_ref[...].astype(BF16), (_DN['nt'], ((), ())), preferred_element_type=F32)
        ds_ref[...] = ds * (sg * (1.0 + cv * (1.0 - sg)))

    return pl.pallas_call(
        body, name="ada_bwd", grid=(DEPTH, nk),
        in_specs=[pl.BlockSpec((16, tk), lambda l, k: (0, k)), pl.BlockSpec((None, tk, NS), lambda l, k: (l, k, 0)),
                  pl.BlockSpec((None, 16, NS), lambda l, k: (l, 0, 0))],
        out_specs=[pl.BlockSpec((None, tk, NS), lambda l, k: (l, k, 0)), pl.BlockSpec((None, 16, tk), lambda l, k: (l, 0, k))],
        out_shape=[jax.ShapeDtypeStruct((DEPTH, D, NS), F32), jax.ShapeDtypeStruct((DEPTH, 16, D), F32)],
        compiler_params=_cp(("parallel", "parallel")))(c9, ada_w, dm9)


def _rope_tables(cf):
    L, LC = cf.L, cf.LC
    rows = L // GRID_W
    row = jnp.repeat(jnp.arange(rows, dtype=F32), GRID_W)
    col = jnp.tile(jnp.arange(GRID_W, dtype=F32), rows)
    nf = HEAD_DIM // 4
    inv = ROPE_THETA ** (-jnp.arange(nf, dtype=F32) / nf)
    ang = jnp.concatenate([row[:, None] * inv, col[:, None] * inv], axis=-1)
    cos, sin = jnp.cos(ang), jnp.sin(ang)
    cs = jnp.concatenate([jnp.ones((LC, HEAD_DIM), F32), jnp.concatenate([cos, cos], -1)], 0)
    sn = jnp.concatenate([jnp.zeros((LC, HEAD_DIM), F32), jnp.concatenate([-sin, sin], -1)], 0)
    return cs, sn


def _prep_tiles(cf, z, cs, sn, key):
    b = cf.OFF[key] // LANE
    return [(z, LANE, _col(b), True), (cs, LANE, _c0, False), (sn, LANE, _c0, False)]


_PREP = {'aq': _f_prep_norm, 'ak': _f_prep_norm, 'rq': _f_prep_plain, 'rk': _f_prep_scaled}


def _prep_fwd(cf, z, cs, sn, key, g):
    nh = cf.W[key] // LANE
    params = [(g, 'shared', True)] if g is not None else []
    return _row_fwd(cf, "prep_fwd_" + key, _PREP[key], _prep_tiles(cf, z, cs, sn, key), params,
                    [(LANE, _col(0), cf.W[key], BF16)], cf.TQ, nrep=nh)[0]


def _prep_bwd(cf, z, cs, sn, key, g, dt):
    nh = cf.W[key] // LANE
    params = [(g, 'shared', True)] if g is not None else []
    tg, pg = _row_bwd(cf, "prep_bwd_" + key, _PREP[key], _prep_tiles(cf, z, cs, sn, key), params,
                      [(dt, LANE, _col(0))], [(LANE, _col(0), cf.W[key], BF16)], cf.TQ, nrep=nh)
    return tg[0], (pg[0] if g is not None else None)


def _gate_params(cf, gup, gb):
    K = gup.shape[-1]
    gf = jnp.zeros((LANE, K), F32).at[0:GLA_RANK].set(gup[0])
    gbm = jnp.zeros((LANE, K), F32).at[GLA_RANK:2 * GLA_RANK].set(gup[1])
    out = []
    for p in range(K // LANE):
        cols = slice(p * LANE, (p + 1) * LANE)
        out += [(gf[:, cols], 'shared', True), (gbm[:, cols], 'shared', True), (gb[0:1, cols], 'shared', True), (gb[1:2, cols], 'shared', True)]
    return out


def _mix_tiles(cf, z, o, key):
    return [(o, LANE, _col(0), True), (z, LANE, _col(cf.OFF[key] // LANE), True)]


def _mid_io(cf, l, W, mod, x, y, norm2_g=None):
    tiles = [(x, cf.D, _c0, True), (y, cf.D, _c0, True)]
    n2 = W['norm2_g'][l] if norm2_g is None else norm2_g
    params = [(mod[2], 'stream', True), (n2, 'shared', True), (mod[3], 'stream', True), (mod[4], 'stream', True)]
    return tiles, params


def _launch_scatter(cf, kind, l, seq, g, nxt):
    g, nxt = lax.optimization_barrier((g, nxt))
    return _seq_scatter(cf, kind, l, seq, g), nxt


class _BigWeights:
    def __init__(self, cf, shards):
        self.cf = cf
        self.whole = {(kind, l): _seq_gather(cf, kind, l, l * len(_KINDS) + n, shards[n])
                      for l in range(DEPTH) for n, kind in enumerate(_KINDS)}
        self.w_in = {}

    def early(self, l, x):
        whole, x = lax.optimization_barrier((self.whole[('in', l)], x))
        self.w_in[l], x = lax.optimization_barrier((_unshard_in(self.cf, "unshard_in_%d" % l, whole), x))
        return x

    def get(self, kind, l, after=None):
        cf = self.cf
        if kind != 'in':
            return self.whole[(kind, l)]
        if l not in self.w_in:
            whole, _ = lax.optimization_barrier((self.whole[(kind, l)], after))
            self.w_in[l] = _unshard_in(cf, "unshard_in_%d" % l, whole)
        return self.w_in[l]


def _layer_fwd(cf, l, W, big, mod, x, h, cs, sn):
    T, D, Fd = cf.T, cf.D, cf.F
    z = _mm("z_%d" % l, h, big.get('in', l, h), 'nn', T, cf.NZ, D, F32, tm=T, tn=768, tk=D)
    qa = _prep_fwd(cf, z, cs, sn, 'aq', W['q_norm_g'][l])
    ka = _prep_fwd(cf, z, cs, sn, 'ak', W['k_norm_g'][l])
    qr = _prep_fwd(cf, z, cs, sn, 'rq', None)
    kr = _prep_fwd(cf, z, cs, sn, 'rk', None)
    o_att = _att_fwd(cf, qa, ka, z)
    o_ret = _ret_fwd(cf, qr, kr, z, W['ret_log_decay'][l])
    gates = _gate_params(cf, W['gla_gate_up'][l], W['gla_gate_b'][l])
    ga_tile = [(z, LANE, _col(cf.OFF['ga'] // LANE), True)]
    we = (cf.HG // 2) * N_DECAY * LANE
    ef, eb = _row_fwd(cf, "gates_fwd_%d" % l, _f_gla_pre, ga_tile, gates, [(we, _c0, we, F32), (we, _c0, we, F32)], GLA_CHUNK)
    o_f, sf, o_b, sb = _gla_fwd(cf, z, ef, eb)
    o_gla = o_f + o_b
    cat_r = _row_fwd(cf, "mixr_fwd_%d" % l, _f_gated_norm, _mix_tiles(cf, z, o_ret, 'rg'), [(W['ret_norm_g'][l], 'shared', True)],
                     [(LANE, _col(0), cf.HR * LANE, BF16)], cf.TQ, nrep=cf.HR)[0]
    cat_g = _row_fwd(cf, "mixg_fwd_%d" % l, _f_gated_norm, _mix_tiles(cf, z, o_gla, 'gr'), [(W['gla_norm_g'][l], 'shared', True)],
                     [(LANE, _col(0), cf.HG * LANE, BF16)], cf.TQ, nrep=cf.HG)[0]
    cat = jnp.concatenate([o_att, cat_r, cat_g], axis=-1)
    y = _mm("y_%d" % l, cat, big.get('out', l, cat), 'nn', T, D, D, F32, tm=T, tn=512, tk=D)
    tiles, params = _mid_io(cf, l, W, mod, x, y)
    x1, h2 = _row_fwd(cf, "mid_fwd_%d" % l, _f_resid_norm_mod, tiles, params, [(D, _c0, D, F32), (D, _c0, D, BF16)], cf.TM)
    u = _mm("u_%d" % l, h2, big.get('up', l, h2), 'nn', T, 2 * Fd, D, F32, tm=T, tn=512, tk=D)
    if l + 1 < DEPTH:
        u = big.early(l + 1, u)
    t = _conv_fwd(cf, u, W['conv_wb'][l])
    yff = _mm("yff_%d" % l, t, big.get('down', l, t), 'nn', T, D, Fd, F32, tm=T, tn=1024, tk=512)
    return dict(x=x, h=h, z=z, qa=qa, ka=ka, qr=qr, kr=kr, ef=ef, eb=eb, sf=sf, sb=sb, o_ret=o_ret, o_gla=o_gla, cat=cat, y=y,
                x1=x1, h2=h2, u=u, t=t, yff=yff, gates=gates)


def _layer_bwd(cf, l, W, big, mod, sv, dx1, dyff, cs, sn):
    T, D, Fd = cf.T, cf.D, cf.F
    g, rs = {}, {}
    sq = 2 * len(_KINDS) + (DEPTH - 1 - l) * len(_KINDS)
    gwd = _mm("gwd_%d" % l, sv['t'], dyff, 'tn', Fd, D, T, BF16, tm=1408, tn=2048, tk=T)
    rs['down'], wb = _launch_scatter(cf, 'down', l, sq, gwd, W['conv_wb'][l])
    dt = _mm("dt_%d" % l, dyff, big.get('down', l), 'nt', T, Fd, D, BF16, tm=T, tn=1408, tk=D)
    du, g['conv_wb'] = _conv_bwd(cf, sv['u'], wb, dt)
    cu = 2 * Fd // N_DEV
    half = Fd // cu
    gwu = _mm("gwu_%d" % l, sv['h2'], du, 'tn', D, 2 * Fd, T, BF16, tm=D, tn=cu, tk=T, out_shape=(N_DEV, D, cu),
              b_spec=pl.BlockSpec((None, T, cu), lambda i, j, k: (j // half, 0, j % half)),
              out_spec=pl.BlockSpec((None, D, cu), lambda i, j, k: (j, i, 0)))
    rs['up'], n2 = _launch_scatter(cf, 'up', l, sq + 1, gwu, W['norm2_g'][l])
    dh2 = _mm("dh2_%d" % l, du, big.get('up', l), 'nt', T, D, 2 * Fd, BF16, tm=T, tn=1024, tk=cu,
              a_spec=pl.BlockSpec((None, T, cu), lambda i, j, k: (k // half, 0, k % half)))
    tiles, params = _mid_io(cf, l, W, mod, sv['x'], sv['y'], n2)
    (dx, dy), (g['m2'], g['norm2_g'], g['m3'], g['m4']) = _row_bwd(
        cf, "mid_bwd_%d" % l, _f_resid_norm_mod, tiles, params, [(dx1, D, _c0), (dh2, D, _c0)],
        [(D, _c0, D, F32), (D, _c0, D, BF16)], cf.TM)
    gwo = _mm("gwo_%d" % l, sv['cat'], dy, 'tn', D, D, T, BF16, tm=D, tn=1024, tk=T)
    rs['out'], rn = _launch_scatter(cf, 'out', l, sq + 2, gwo, W['ret_norm_g'][l])
    dcat = _mm("dcat_%d" % l, dy, big.get('out', l), 'nt', T, D, D, BF16, tm=T, tn=1024, tk=D)
    z = sv['z']
    (do_ret, drg), (g['ret_norm_g'],) = _row_bwd(
        cf, "mixr_bwd_%d" % l, _f_gated_norm, _mix_tiles(cf, z, sv['o_ret'], 'rg'), [(rn, 'shared', True)],
        [(dcat, LANE, _col(cf.HQ))], [(LANE, _col(0), cf.HR * LANE, F32), (LANE, _col(0), cf.HR * LANE, BF16)], cf.TQ, nrep=cf.HR)
    (do_gla, dgr), (g['gla_norm_g'],) = _row_bwd(
        cf, "mixg_bwd_%d" % l, _f_gated_norm, _mix_tiles(cf, z, sv['o_gla'], 'gr'), [(W['gla_norm_g'][l], 'shared', True)],
        [(dcat, LANE, _col(cf.HQ + cf.HR))], [(LANE, _col(0), cf.HG * LANE, F32), (LANE, _col(0), cf.HG * LANE, BF16)], cf.TQ, nrep=cf.HG)
    dqa, dka, dav = _att_bwd(cf, sv['qa'], sv['ka'], z, dcat)
    dqr, dkr, drv, dlg = _ret_bwd(cf, sv['qr'], sv['kr'], z, W['ret_log_decay'][l], do_ret)
    g['ret_log_decay'] = dlg[:, 0:2, 0].T
    dq_f, dk_f, dv_f, def_, dq_b, dk_b, dv_b, deb = _gla_bwd(cf, z, sv['ef'], sv['eb'], sv['sf'], sv['sb'], do_gla)
    dgq, dgk, dgv = dq_f + dq_b, dk_f + dk_b, dv_f + dv_b
    we = (cf.HG // 2) * N_DECAY * LANE
    ga_tile = [(z, LANE, _col(cf.OFF['ga'] // LANE), True)]
    (dga,), gg = _row_bwd(cf, "gates_bwd_%d" % l, _f_gla_pre, ga_tile, sv['gates'],
                          [(def_, we, _c0), (deb, we, _c0)], [(LANE, _c0, LANE, BF16)], GLA_CHUNK)
    ggf, ggb, gbf, gbb = [jnp.concatenate(gg[n::4], axis=-1) for n in range(4)]
    g['gla_gate_up'] = jnp.stack([ggf[0:GLA_RANK], ggb[GLA_RANK:2 * GLA_RANK]])
    g['gla_gate_b'] = jnp.concatenate([gbf, gbb], axis=0)
    daq, g['q_norm_g'] = _prep_bwd(cf, z, cs, sn, 'aq', W['q_norm_g'][l], dqa)
    dak, g['k_norm_g'] = _prep_bwd(cf, z, cs, sn, 'ak', W['k_norm_g'][l], dka)
    drq, _ = _prep_bwd(cf, z, cs, sn, 'rq', None, dqr)
    drk, _ = _prep_bwd(cf, z, cs, sn, 'rk', None, dkr)
    pad = jnp.zeros((T, cf.NZ - cf.OFF['ga'] - LANE), BF16)
    dz = jnp.concatenate([daq, dak, dav.astype(BF16), drq, drk, drv.astype(BF16), drg, dgq.astype(BF16), dgk.astype(BF16),
                          dgv.astype(BF16), dgr, dga, pad], axis=-1)
    gwi = _mm("gwi_%d" % l, sv['h'], dz, 'tn', D, cf.NZ, T, BF16, tm=D, tn=768, tk=T)
    rs['in'], g['norm1_g_tied'] = _launch_scatter(cf, 'in', l, sq + 3, _slabs_in(cf, "slabs_in_%d" % l, gwi), W['norm1_g'][l])
    dh =_mm("dh_%d" % l, dz, big.get('in', l), 'nt', T, D, cf.NZ, BF16, tm=T, tn=1024, tk=1792)
    g['rs'] = rs
    return dx, dh, g


_WEIGHTS = ['c_ctx', 'ada_w', 'ada_b', 'norm1_g', 'w_in', 'q_norm_g', 'k_norm_g', 'ret_log_decay', 'ret_norm_g',
            'gla_gate_up', 'gla_gate_b', 'gla_norm_g', 'w_out', 'norm2_g', 'w_up', 'conv_w', 'conv_b', 'w_down', 'final_norm_g']
_BIG = ['w_in', 'w_out', 'w_up', 'w_down']
_SMALL = [n for n in _WEIGHTS if n not in _BIG and n != 'ada_w']
_COL_SHARDED = ['gla_gate_up', 'gla_gate_b', 'conv_w']


def _pack(arrs):
    rows = []
    for a in arrs:
        flat = a.reshape(-1)
        n = flat.shape[0]
        rows.append(jnp.pad(flat, (0, -n % LANE)).reshape(-1, LANE))
    packed = jnp.concatenate(rows, axis=0)
    return jnp.pad(packed, ((0, -packed.shape[0] % 8), (0, 0)))


def _unpack(packed, shapes):
    lead = packed.shape[:-2]
    out, r = [], 0
    for s in shapes:
        n = int(np.prod(s))
        nr = -(-n // LANE)
        out.append(packed[..., r:r + nr, :].reshape(lead + (nr * LANE,))[..., :n].reshape(lead + tuple(s)))
        r += nr
    return out


def _unshard_last(a):
    return jnp.moveaxis(a, 0, -2).reshape(a.shape[1:-1] + (N_DEV * a.shape[-1],))


def _step(cf, x, c, ctx, loss_target, w, m, v):
    T, D, Fd, L, LC = cf.T, cf.D, cf.F, cf.L, cf.LC
    _, _, _, me = _me()
    NS = w['ada_w'].shape[-1]

    c_all = _ag_small("ag_c", jnp.pad(c, ((0, 7), (0, 0))))[:, 0, :]
    c9 = jnp.concatenate([c_all, w['c_ctx'][None], jnp.zeros((7, D), F32)], axis=0)
    pm = _ada_fwd(cf, c9, w['ada_w'])
    pm_all = _ag_small("ag_mod", pm.reshape(DEPTH * 16, NS)).reshape(N_DEV, DEPTH, 16, NS)
    mod_all = _unshard_last(pm_all) + w['ada_b'][:, None, :]
    mod_own = lax.dynamic_index_in_dim(mod_all, me, axis=1, keepdims=False)
    mods = []
    for l in range(DEPTH):
        mods.append([jnp.stack([mod_all[l, 8, k * D:(k + 1) * D], mod_own[l, k * D:(k + 1) * D]])[:, None, :] for k in range(N_MOD)])

    shard_shapes = [w[n].shape for n in _COL_SHARDED]
    got = _ag_small("ag_smallw", _pack([w[n] for n in _COL_SHARDED]))
    full = dict(zip(_COL_SHARDED, [_unshard_last(a) for a in _unpack(got, shard_shapes)]))

    small_done = full['conv_w'] + mod_all[0, 0, 0]
    big = _BigWeights(cf, [_cast_bf16("cast_" + n, w[n], c if n == 'w_in' else small_done) for n in _BIG])
    conv_wb = jnp.concatenate([full['conv_w'], w['conv_b'][:, None, :], jnp.zeros((DEPTH, 4, Fd), F32)], axis=1)
    W = dict(conv_wb=conv_wb, gla_gate_up=full['gla_gate_up'], gla_gate_b=full['gla_gate_b'], ret_log_decay=w['ret_log_decay'])
    for n in ['q_norm_g', 'k_norm_g', 'ret_norm_g', 'gla_norm_g', 'norm1_g', 'norm2_g']:
        W[n] = w[n][:, None, :]

    cs, sn = _rope_tables(cf)
    x0 = jnp.concatenate([ctx[0], x[0]], axis=0)
    pre_tiles = [(x0, D, _c0, True)]

    def pre_params(n1):
        return [(n1, 'shared', True), (mods[0][0], 'stream', True), (mods[0][1], 'stream', True)]

    def tr_params(n1):
        return [(mods[0][5], 'stream', True), (n1, 'shared', True), (mods[1][0], 'stream', True), (mods[1][1], 'stream', True)]

    h0 = _row_fwd(cf, "pre_fwd", _f_norm_mod, pre_tiles, pre_params(W['norm1_g'][0]), [(D, _c0, D, BF16)], cf.TM)[0]
    sv0 = _layer_fwd(cf, 0, W, big, mods[0], x0, h0, cs, sn)
    tr_tiles = [(sv0['x1'], D, _c0, True), (sv0['yff'], D, _c0, True)]
    xb, hb = _row_fwd(cf, "tr_fwd", _f_resid_norm_mod, tr_tiles, tr_params(W['norm1_g'][1]), [(D, _c0, D, F32), (D, _c0, D, BF16)], cf.TM)
    sv1 = _layer_fwd(cf, 1, W, big, mods[1], xb, hb, cs, sn)
    tgt = jnp.concatenate([jnp.zeros((LC, D), F32), loss_target[0]], axis=0)
    dx1, dyff, dm5_1, g_final, ls = _loss_grad(cf, sv1['x1'], sv1['yff'], mods[1][5], w['final_norm_g'][None], tgt)
    loss = lax.psum(ls[0, 0], ("x", "y", "c"))

    dxb, dhb, g1 = _layer_bwd(cf, 1, W, big, mods[1], sv1, dx1, dyff, cs, sn)
    (dx1_0, dyff_0), (dm5_0, gn1_1, dm0_1, dm1_1) = _row_bwd(
        cf, "tr_bwd", _f_resid_norm_mod, tr_tiles, tr_params(g1['norm1_g_tied']), [(dxb, D, _c0), (dhb, D, _c0)],
        [(D, _c0, D, F32), (D, _c0, D, BF16)], cf.TM)
    dx0, dh0, g0 = _layer_bwd(cf, 0, W, big, mods[0], sv0, dx1_0, dyff_0, cs, sn)
    (dxa,), (gn1_0, dm0_0, dm1_0) = _row_bwd(cf, "pre_bwd", _f_first, pre_tiles, pre_params(g0['norm1_g_tied']), [(dx0, D, _c0), (dh0, D, _c0)],
                                            [(D, _c0, D, F32)], cf.TM)
    grad_x = dxa[LC:][None]

    dmod = jnp.stack([jnp.concatenate([dm0_0, dm1_0, g0['m2'], g0['m3'], g0['m4'], dm5_0], axis=-1)[:, 0],
                      jnp.concatenate([dm0_1, dm1_1, g1['m2'], g1['m3'], g1['m4'], dm5_1], axis=-1)[:, 0]])
    dm_all = _ag_small("ag_dmod", jnp.pad(dmod.reshape(2 * DEPTH, N_MOD * D), ((0, 8 - 2 * DEPTH), (0, 0))))
    dm_all = dm_all[:, :2 * DEPTH].reshape(N_DEV, DEPTH, 2, N_MOD * D)
    dctx = _sum8("sum_dmodc", jnp.pad(dm_all[:, :, 0], ((0, 0), (0, 8 - DEPTH), (0, 0))))[:DEPTH]
    dm9 = jnp.concatenate([jnp.moveaxis(dm_all[:, :, 1], 0, 1), dctx[:, None]], axis=1)
    g_ada_b = _sum8("sum_adab", jnp.pad(jnp.moveaxis(dm9, 1, 0), ((0, 0), (0, 8 - DEPTH), (0, 0))))[:DEPTH]
    dm9s = lax.dynamic_slice_in_dim(jnp.pad(dm9, ((0, 0), (0, 7), (0, 0))), me * NS, NS, axis=2)
    g_ada_w, dsil = _ada_bwd(cf, c9, w['ada_w'], dm9s)
    g_cctx_part = dsil[0, 8]
    for l in range(1, DEPTH):
        g_cctx_part = g_cctx_part + dsil[l, 8]

    def both(key):
        return jnp.stack([g0[key], g1[key]])

    gsmall = dict(c_ctx=g_cctx_part, norm1_g=jnp.stack([gn1_0[0], gn1_1[0]]), q_norm_g=both('q_norm_g')[:, 0],
                  k_norm_g=both('k_norm_g')[:, 0], ret_log_decay=both('ret_log_decay'), ret_norm_g=both('ret_norm_g')[:, 0],
                  gla_gate_up=both('gla_gate_up'), gla_gate_b=both('gla_gate_b'), gla_norm_g=both('gla_norm_g')[:, 0],
                  norm2_g=both('norm2_g')[:, 0], conv_w=both('conv_wb')[:, 0:3], conv_b=both('conv_wb')[:, 3], final_norm_g=g_final[0])
    snames = [n for n in _SMALL if n != 'ada_b']
    sshapes = [gsmall[n].shape for n in snames]
    gs_all = _ag_small("ag_gsmall", _pack([gsmall[n] for n in snames]))
    gs = dict(zip(snames, _unpack(_sum8("sum_gsmall", gs_all), sshapes)))
    gs['ada_b'] = g_ada_b
    for n in _COL_SHARDED:
        ns_ = w[n].shape[-1]
        gs[n] = lax.dynamic_slice_in_dim(gs[n], me * ns_, ns_, axis=gs[n].ndim - 1)

    out_g, out_d, out_m, out_v = {}, {}, {}, {}

    after, done = gs_all, {}
    for l, gl in ((1, g1), (0, g0)):
        for kind in reversed(_KINDS):
            n = 'w_' + kind
            done[n] = _sum_adam("adam_%s_%d" % (n, l), gl['rs'][kind], w[n], m[n], v[n], l, done.get(n), after)
            after = done[n][0]
    for n in _BIG:
        out_g[n], out_d[n], out_m[n], out_v[n] = done[n]
    aw = [a.reshape(DEPTH * D, NS) for a in (w['ada_w'], g_ada_w, m['ada_w'], v['ada_w'])]
    out_g['ada_w'] = g_ada_w
    out_d['ada_w'], out_m['ada_w'], out_v['ada_w'] = [a.reshape(DEPTH, D, NS) for a in _adam("adam_ada_w", *aw)]
    shp = [w[n].shape for n in _SMALL]
    packed = [_pack([src[n] for n in _SMALL]) for src in (w, gs, m, v)]
    res = _adam("adam_small", *packed)
    for dst, pk in zip((out_d, out_m, out_v), res):
        dst.update(zip(_SMALL, _unpack(pk, shp)))
    out_g.update({n: gs[n] for n in _SMALL})
    return (loss, grad_x, *[out_g[n] for n in _WEIGHTS], *[out_d[n] for n in _WEIGHTS], *[out_m[n] for n in _WEIGHTS],
            *[out_v[n] for n in _WEIGHTS])


def kernel(x, c, ctx, c_ctx, ada_w, ada_b, norm1_g, w_in, q_norm_g, k_norm_g, ret_log_decay, ret_norm_g, gla_gate_up, gla_gate_b, gla_norm_g, w_out, norm2_g, w_up, conv_w, conv_b, w_down, final_norm_g, loss_target, m_c_ctx, m_ada_w, m_ada_b, m_norm1_g, m_w_in, m_q_norm_g, m_k_norm_g, m_ret_log_decay, m_ret_norm_g, m_gla_gate_up, m_gla_gate_b, m_gla_norm_g, m_w_out, m_norm2_g, m_w_up, m_conv_w, m_conv_b, m_w_down, m_final_norm_g, v_c_ctx, v_ada_w, v_ada_b, v_norm1_g, v_w_in, v_q_norm_g, v_k_norm_g, v_ret_log_decay, v_ret_norm_g, v_gla_gate_up, v_gla_gate_b, v_gla_norm_g, v_w_out, v_norm2_g, v_w_up, v_conv_w, v_conv_b, v_w_down, v_final_norm_g):
    w = dict(c_ctx=c_ctx, ada_w=ada_w, ada_b=ada_b, norm1_g=norm1_g, w_in=w_in, q_norm_g=q_norm_g, k_norm_g=k_norm_g,
             ret_log_decay=ret_log_decay, ret_norm_g=ret_norm_g, gla_gate_up=gla_gate_up, gla_gate_b=gla_gate_b,
             gla_norm_g=gla_norm_g, w_out=w_out, norm2_g=norm2_g, w_up=w_up, conv_w=conv_w, conv_b=conv_b, w_down=w_down,
             final_norm_g=final_norm_g)
    m = dict(c_ctx=m_c_ctx, ada_w=m_ada_w, ada_b=m_ada_b, norm1_g=m_norm1_g, w_in=m_w_in, q_norm_g=m_q_norm_g,
             k_norm_g=m_k_norm_g, ret_log_decay=m_ret_log_decay, ret_norm_g=m_ret_norm_g, gla_gate_up=m_gla_gate_up,
             gla_gate_b=m_gla_gate_b, gla_norm_g=m_gla_norm_g, w_out=m_w_out, norm2_g=m_norm2_g, w_up=m_w_up,
             conv_w=m_conv_w, conv_b=m_conv_b, w_down=m_w_down, final_norm_g=m_final_norm_g)
    v = dict(c_ctx=v_c_ctx, ada_w=v_ada_w, ada_b=v_ada_b, norm1_g=v_norm1_g, w_in=v_w_in, q_norm_g=v_q_norm_g,
             k_norm_g=v_k_norm_g, ret_log_decay=v_ret_log_decay, ret_norm_g=v_ret_norm_g, gla_gate_up=v_gla_gate_up,
             gla_gate_b=v_gla_gate_b, gla_norm_g=v_gla_norm_g, w_out=v_w_out, norm2_g=v_norm2_g, w_up=v_w_up,
             conv_w=v_conv_w, conv_b=v_conv_b, w_down=v_w_down, final_norm_g=v_final_norm_g)
    return _step(_cfg(), x, c, ctx, loss_target, w, m, v)
```

```python
import functools
import math
import types

import jax
import jax.numpy as jnp
import numpy as np
from jax import lax
from jax.experimental import pallas as pl
from jax.experimental.pallas import tpu as pltpu
from jax.experimental.pallas import tpu_sc as plsc

F32 = jnp.float32
BF16 = jnp.bfloat16

D_MODEL = 2048
SEQ = 2048
CTX_LEN = 256
GRID_W = 64
D_FF = 5632
DEPTH = 2
N_DEV = 8
HEAD_DIM = 128
ROPE_THETA = 10000.0
GLA_TAU = 16.0
GLA_RANK = 16
GLA_CHUNK = 64
GLA_SUB = 16
EPS = 1e-6
N_MOD = 6
ADAM_LR = 0.001
ADAM_B1 = 0.9
ADAM_B2 = 0.999
ADAM_EPS = 1e-08
ADAM_WD = 0.01
ADAM_STEP = 10
LANE = 128
VMEM_LIMIT = 56 * 1024 * 1024
NEG = -1e30


def _cfg():
    d = types.SimpleNamespace()
    d.D, d.L, d.LC, d.F = D_MODEL, SEQ, CTX_LEN, D_FF
    d.T = d.L + d.LC
    nm = d.D // HEAD_DIM
    d.HQ, d.HKV, d.HR, d.HG = nm // 2, nm // 8, nm // 4, nm // 4
    d.G = d.HQ // d.HKV
    w = dict(aq=d.HQ * 128, ak=d.HKV * 128, av=d.HKV * 128, rq=d.HR * 128, rk=d.HR * 128, rv=d.HR * 128,
             rg=d.HR * 128, gq=d.HG * 64, gk=d.HG * 64, gv=d.HG * 128, gr=d.HG * 128, ga=2 * GLA_RANK)
    off, o = {}, 0
    for k, v in w.items():
        off[k] = o
        o += v
    d.W, d.OFF, d.NIN = w, off, o
    d.NZ = -(-(off['ga'] + LANE) // 256) * 256
    d.NINS = d.NIN // N_DEV
    d.TM = math.gcd(d.LC, 128)
    d.TQ = math.gcd(d.LC, 256)
    return d


def _cp(sem=None):
    return pltpu.CompilerParams(dimension_semantics=sem, vmem_limit_bytes=VMEM_LIMIT)


def _tile(n, target, mult=LANE):
    t = min(n, target)
    t -= t % mult
    while t > mult and n % t:
        t -= mult
    return t if t > 0 and n % t == 0 else n


_DN = {'nn': ((1,), (0,)), 'nt': ((1,), (1,)), 'tn': ((0,), (0,))}


def _mm(name, a, b, kind, M, N, K, out_dtype, tm=768, tn=768, tk=1024, a_spec=None, b_spec=None,
        out_shape=None, out_spec=None):
    tm, tn = _tile(M, tm, 128), _tile(N, tn, 128)
    tk = _tile(K, tk, 128)
    nk = K // tk

    def dot(a_ref, b_ref):
        return lax.dot_general(a_ref[...].astype(BF16), b_ref[...].astype(BF16), (_DN[kind], ((), ())), preferred_element_type=F32)

    def body_one(a_ref, b_ref, o_ref):
        o_ref[...] = dot(a_ref, b_ref).astype(o_ref.dtype)

    def body(a_ref, b_ref, o_ref, acc):
        kk = pl.program_id(2)

        @pl.when(kk == 0)
        def _():
            acc[...] = jnp.zeros_like(acc)

        acc[...] += dot(a_ref, b_ref)

        @pl.when(kk == nk - 1)
        def _():
            o_ref[...] = acc[...].astype(o_ref.dtype)

    if a_spec is None:
        a_spec = pl.BlockSpec((tk, tm), lambda i, j, k: (k, i)) if kind == 'tn' else pl.BlockSpec((tm, tk), lambda i, j, k: (i, k))
    if b_spec is None:
        b_spec = pl.BlockSpec((tn, tk), lambda i, j, k: (j, k)) if kind == 'nt' else pl.BlockSpec((tk, tn), lambda i, j, k: (k, j))
    if out_spec is None:
        out_spec = pl.BlockSpec((tm, tn), lambda i, j, k: (i, j))
        out_shape = (M, N)
    return pl.pallas_call(
        body_one if nk == 1 else body, name=name, grid=(M // tm, N // tn, nk), in_specs=[a_spec, b_spec], out_specs=out_spec,
        out_shape=jax.ShapeDtypeStruct(out_shape, out_dtype), scratch_shapes=[] if nk == 1 else [pltpu.VMEM((tm, tn), F32)],
        compiler_params=_cp(("parallel", "parallel", "arbitrary")))(a, b)


def _tile_spec(tm, w, colf, nrep):
    if hasattr(colf, 'base'):
        assert colf.base % nrep == 0
        return pl.BlockSpec((tm, w * nrep), functools.partial(lambda i, b: (i, b), b=colf.base // nrep))
    return pl.BlockSpec((tm, w), lambda i: (i, 0))


def _head_cols(colf, w, r):
    return slice(r * w, (r + 1) * w) if hasattr(colf, 'base') else slice(None)


def _row_specs(cf, tm, tiles, params, nrep):
    nctx = cf.LC // tm
    specs = [_tile_spec(tm, w, colf, nrep) for arr, w, colf, _ in tiles]
    for arr, kind, _ in params:
        nd = arr.ndim
        if kind == 'shared':
            specs.append(pl.BlockSpec(arr.shape, functools.partial(lambda i, nd: (0,) * nd, nd=nd)))
        else:
            specs.append(pl.BlockSpec((None,) + arr.shape[1:],
                                      functools.partial(lambda i, nd, nctx: (jnp.where(i >= nctx, 1, 0),) + (0,) * (nd - 1), nd=nd, nctx=nctx)))
    return specs


def _row_fwd(cf, name, f, tiles, params, outs, tm, nrep=1):
    nt, npar = len(tiles), len(params)

    def body(*refs):
        pv = [r[...] for r in refs[nt:nt + npar]]
        for r in range(nrep):
            tv = [x[:, _head_cols(t[2], t[1], r)].astype(F32) for x, t in zip(refs[:nt], tiles)]
            res = f(*tv, *pv)
            for o, v, spec in zip(refs[nt + npar:], res, outs):
                o[:, _head_cols(spec[1], spec[0], r)] = v.astype(o.dtype)

    out_specs = [_tile_spec(tm, w, colf, nrep) for w, colf, _, _ in outs]
    out_shape = [jax.ShapeDtypeStruct((cf.T, tw), dt) for _, _, tw, dt in outs]
    return pl.pallas_call(
        body, name=name, grid=(cf.T // tm,), in_specs=_row_specs(cf, tm, tiles, params, nrep), out_specs=out_specs,
        out_shape=out_shape, compiler_params=_cp(("arbitrary",)))(*[t[0] for t in tiles], *[p[0] for p in params])


def _row_bwd(cf, name, f, tiles, params, cts, tgrads, tm, nrep=1):
    nt, npar, nc = len(tiles), len(params), len(cts)
    tdiff = [k for k, t in enumerate(tiles) if t[3]]
    pdiff = [k for k, p in enumerate(params) if p[2]]
    nctx = cf.LC // tm

    def body(*refs):
        i = pl.program_id(0)
        pv = [x[...] for x in refs[nt:nt + npar]]
        outs = refs[nt + npar + nc:]
        psum = None
        for r in range(nrep):
            tv = [x[:, _head_cols(t[2], t[1], r)].astype(F32) for x, t in zip(refs[:nt], tiles)]
            cv = tuple(x[:, _head_cols(c[2], c[1], r)].astype(F32) for x, c in zip(refs[nt + npar:nt + npar + nc], cts))

            def g(*diff, tv=tv):
                tv2, pv2 = list(tv), list(pv)
                for k, v in zip(tdiff, diff[:len(tdiff)]):
                    tv2[k] = v
                for k, v in zip(pdiff, diff[len(tdiff):]):
                    pv2[k] = v
                return tuple(f(*tv2, *pv2))

            _, vjp_fn = jax.vjp(g, *[tv[k] for k in tdiff], *[pv[k] for k in pdiff])
            grads = vjp_fn(cv)
            for o, gv, spec in zip(outs[:len(tdiff)], grads[:len(tdiff)], tgrads):
                o[:, _head_cols(spec[1], spec[0], r)] = gv.astype(o.dtype)
            pg = grads[len(tdiff):]
            psum = list(pg) if psum is None else [a + b for a, b in zip(psum, pg)]
        for n_, (o, gv) in enumerate(zip(outs[len(tdiff):], psum)):
            first = (i == 0) if params[pdiff[n_]][1] == 'shared' else jnp.logical_or(i == 0, i == nctx)

            @pl.when(first)
            def _():
                o[...] = gv

            @pl.when(jnp.logical_not(first))
            def _():
                o[...] += gv

    in_specs = _row_specs(cf, tm, tiles, params, nrep)
    in_specs += [_tile_spec(tm, w, colf, nrep) for _, w, colf in cts]
    out_specs = [_tile_spec(tm, w, colf, nrep) for w, colf, _, _ in tgrads]
    out_shape = [jax.ShapeDtypeStruct((cf.T, tw), dt) for _, _, tw, dt in tgrads]
    out_specs += _row_specs(cf, tm, [], [params[k] for k in pdiff], nrep)
    out_shape += [jax.ShapeDtypeStruct(params[k][0].shape, F32) for k in pdiff]
    res = pl.pallas_call(
        body, name=name, grid=(cf.T // tm,), in_specs=in_specs, out_specs=out_specs, out_shape=out_shape,
        compiler_params=_cp(("arbitrary",)))(*[t[0] for t in tiles], *[p[0] for p in params], *[c[0] for c in cts])
    return res[:len(tdiff)], res[len(tdiff):]


def _c0(r):
    return 0


def _col(base):
    def col(r):
        return base + r
    col.base = base
    return col


def _rms(x, g):
    return x * lax.rsqrt(jnp.mean(x * x, axis=-1, keepdims=True) + EPS) * g


def _sigmoid(x):
    return 1.0 / (1.0 + jnp.exp(-x))


def _silu(x):
    return x * _sigmoid(x)


def _f_norm_mod(x, g, sh, sc):
    return (_rms(x, g) * (1 + sc) + sh,)


def _f_resid_norm_mod(x, y, gate, g, sh, sc):
    x1 = x + gate * y
    return (x1, _rms(x1, g) * (1 + sc) + sh)


@jax.custom_vjp
def _swap_halves(t):
    return pltpu.roll(t, HEAD_DIM // 2, axis=1)


def _swap_fwd(t):
    return _swap_halves(t), None


def _swap_bwd(_, g):
    return (pltpu.roll(g, HEAD_DIM // 2, axis=1),)


_swap_halves.defvjp(_swap_fwd, _swap_bwd)


def _rope(t, cs, sn):
    return t * cs + _swap_halves(t) * sn


def _f_prep_norm(t, cs, sn, g):
    return (_rope(_rms(t, g), cs, sn),)


def _f_prep_plain(t, cs, sn):
    return (_rope(t, cs, sn),)


def _f_prep_scaled(t, cs, sn):
    return (_rope(t * (HEAD_DIM ** -0.5), cs, sn),)


def _log_sigmoid(x):
    return jnp.minimum(x, 0.0) - jnp.log(1.0 + jnp.exp(-jnp.abs(x)))


N_DECAY = 8


def _gla_masks(d, width):
    C, SB = GLA_CHUNK, GLA_SUB
    r = lax.broadcasted_iota(jnp.int32, (C, C), 0)
    m = lax.broadcasted_iota(jnp.int32, (C, C), 1)
    rr = lax.broadcasted_iota(jnp.int32, (C, width), 0)
    allowed = (m <= r) if d == 0 else (m >= r)
    blocks, vis = [allowed], []
    for b in range(C // SB):
        blocks.append((m < SB * b) if d == 0 else (m >= SB * (b + 1)))
        vis.append((rr < SB * (b + 1)) if d == 0 else (rr >= SB * b))
    cm = jnp.concatenate([x.astype(F32) for x in blocks] + [jnp.ones((C, C), F32)], axis=0)
    return cm, allowed, vis


def _gla_decays(la, d):
    C, SB = GLA_CHUNK, GLA_SUB
    nsb = C // SB
    cm, _, vis = _gla_masks(d, la.shape[-1])
    cums = jnp.dot(cm, la, precision=lax.Precision.HIGH, preferred_element_type=F32)
    cum, tot = cums[0:C], cums[(1 + nsb) * C:]
    refs = [cums[(1 + b) * C:(2 + b) * C] for b in range(nsb)]
    e1 = jnp.concatenate([jnp.exp(cum[b * SB:(b + 1) * SB] - refs[b][b * SB:(b + 1) * SB]) for b in range(nsb)], axis=0)
    e2 = [jnp.where(vis[b], jnp.exp(jnp.where(vis[b], refs[b] - cum, 0.0)), 0.0) for b in range(nsb)]
    return [e1] + e2 + [jnp.exp(cum), jnp.exp(tot - cum), jnp.exp(tot)]


def _f_gla_pre(ga, *per_pair):
    gab = ga.astype(BF16)
    outs = [[], []]
    for p in range(len(per_pair) // 4):
        gf, gb, bf, bb = per_pair[4 * p:4 * p + 4]
        for d, (gm, bm) in enumerate(((gf, bf), (gb, bb))):
            la = _log_sigmoid(jnp.dot(gab, gm.astype(BF16), preferred_element_type=F32) + bm) / GLA_TAU
            outs[d] += _gla_decays(la, d)
    return tuple(jnp.concatenate(o, axis=-1) for o in outs)


def _f_gated_norm(o, g, n):
    return (_rms(o, n) * _silu(g),)


def _f_first(x, g, sh, sc):
    return (x, _rms(x, g) * (1 + sc) + sh)


def _loss_grad(cf, x1, yff, gate, gfin, tgt):
    tm, T, D = cf.TM, cf.T, cf.D
    nctx = cf.LC // tm

    def lossf(x1v, yv, gt, gf, tg):
        y = _rms(x1v + gt * yv, gf)
        e = y - tg
        return 0.5 * jnp.sum(jnp.mean(e * e, axis=-1, keepdims=True), axis=0, keepdims=True)

    def body(x1_ref, y_ref, gt_ref, gf_ref, tg_ref, dx_ref, dy_ref, dgt_ref, dgf_ref, ls_ref):
        i = pl.program_id(0)
        lat = (i >= nctx).astype(F32)
        val, vjp_fn = jax.vjp(lossf, x1_ref[...], y_ref[...].astype(F32), gt_ref[...], gf_ref[...], tg_ref[...])
        dx, dy, dgt, dgf, _ = vjp_fn(jnp.ones((1, 1), F32) * lat)
        dx_ref[...] = dx
        dy_ref[...] = dy.astype(dy_ref.dtype)
        first_s = jnp.logical_or(i == 0, i == nctx)

        @pl.when(first_s)
        def _():
            dgt_ref[...] = dgt

        @pl.when(jnp.logical_not(first_s))
        def _():
            dgt_ref[...] += dgt

        @pl.when(i == 0)
        def _():
            dgf_ref[...] = dgf
            ls_ref[...] = jnp.zeros_like(ls_ref) + val * lat

        @pl.when(i != 0)
        def _():
            dgf_ref[...] += dgf
            ls_ref[...] += val * lat

    row = pl.BlockSpec((tm, D), lambda i: (i, 0))
    strm = pl.BlockSpec((None, 1, D), lambda i: (jnp.where(i >= nctx, 1, 0), 0, 0))
    one = pl.BlockSpec((1, D), lambda i: (0, 0))
    return pl.pallas_call(
        body, name="loss_grad", grid=(T // tm,), in_specs=[row, row, strm, one, row],
        out_specs=[row, row, strm, one, pl.BlockSpec((8, LANE), lambda i: (0, 0))],
        out_shape=[jax.ShapeDtypeStruct((T, D), F32), jax.ShapeDtypeStruct((T, D), BF16),
                   jax.ShapeDtypeStruct((2, 1, D), F32), jax.ShapeDtypeStruct((1, D), F32),
                   jax.ShapeDtypeStruct((8, LANE), F32)],
        compiler_params=_cp(("arbitrary",)))(x1, yff, gate, gfin, tgt)


def _att_mask(cf, is_latent, rows):
    col = lax.broadcasted_iota(jnp.int32, (rows, cf.T), 1)
    return jnp.logical_or(col < cf.LC, is_latent)


def _stack_heads(ref, G):
    return jnp.concatenate([ref[:, j * LANE:(j + 1) * LANE] for j in range(G)], axis=0)


def _att_probs(q, k, mask):
    s = lax.dot_general(q, k, (_DN['nt'], ((), ())), preferred_element_type=F32) * (HEAD_DIM ** -0.5)
    s = jnp.where(mask, s, NEG)
    e = jnp.exp(s - jnp.max(s, axis=-1, keepdims=True))
    return e / jnp.sum(e, axis=-1, keepdims=True)


def _att_fwd(cf, q, k, z):
    tq, T, G = cf.TQ, cf.T, cf.G
    vb = cf.OFF['av'] // LANE

    def body(q_ref, k_ref, v_ref, o_ref):
        mask = _att_mask(cf, pl.program_id(1) >= cf.LC // tq, tq)
        kv, vv = k_ref[...], v_ref[...].astype(BF16)
        for j in range(G):
            p = _att_probs(q_ref[:, j * LANE:(j + 1) * LANE], kv, mask)
            o_ref[:, j * LANE:(j + 1) * LANE] = jnp.dot(p.astype(BF16), vv, preferred_element_type=F32).astype(o_ref.dtype)

    return pl.pallas_call(
        body, name="att_fwd", grid=(cf.HKV, T // tq),
        in_specs=[pl.BlockSpec((tq, G * LANE), lambda g, i: (i, g)), pl.BlockSpec((T, LANE), lambda g, i: (0, g)),
                  pl.BlockSpec((T, LANE), lambda g, i: (0, vb + g))],
        out_specs=pl.BlockSpec((tq, G * LANE), lambda g, i: (i, g)),
        out_shape=jax.ShapeDtypeStruct((T, cf.HQ * LANE), BF16), compiler_params=_cp(("arbitrary", "arbitrary")))(q, k, z)


def _att_bwd(cf, q, k, z, dcat):
    tq, T, G = cf.TM, cf.T, cf.G
    vb = cf.OFF['av'] // LANE
    sc = HEAD_DIM ** -0.5

    def body(q_ref, k_ref, v_ref, do_ref, dq_ref, dk_ref, dv_ref):
        i = pl.program_id(1)
        mask = _att_mask(cf, i >= cf.LC // tq, G * tq)
        kv, vv = k_ref[...], v_ref[...].astype(BF16)
        q4, do4 = _stack_heads(q_ref, G), _stack_heads(do_ref, G)
        p = _att_probs(q4, kv, mask)
        dv = lax.dot_general(p.astype(BF16), do4, (_DN['tn'], ((), ())), preferred_element_type=F32)
        dp = lax.dot_general(do4, vv, (_DN['nt'], ((), ())), preferred_element_type=F32)
        dsb = (p * (dp - jnp.sum(dp * p, axis=-1, keepdims=True)) * sc).astype(BF16)
        dq = jnp.dot(dsb, kv, preferred_element_type=F32)
        dk = lax.dot_general(dsb, q4, (_DN['tn'], ((), ())), preferred_element_type=F32)
        for j in range(G):
            dq_ref[:, j * LANE:(j + 1) * LANE] = dq[j * tq:(j + 1) * tq]

        @pl.when(i == 0)
        def _():
            dk_ref[...] = dk
            dv_ref[...] = dv

        @pl.when(i != 0)
        def _():
            dk_ref[...] += dk
            dv_ref[...] += dv

    qs = pl.BlockSpec((tq, G * LANE), lambda g, i: (i, g))
    ks = pl.BlockSpec((T, LANE), lambda g, i: (0, g))
    return pl.pallas_call(
        body, name="att_bwd", grid=(cf.HKV, T // tq),
        in_specs=[qs, ks, pl.BlockSpec((T, LANE), lambda g, i: (0, vb + g)), qs],
        out_specs=[qs, ks, ks],
        out_shape=[jax.ShapeDtypeStruct((T, cf.HQ * LANE), F32), jax.ShapeDtypeStruct((T, cf.HKV * LANE), F32),
                   jax.ShapeDtypeStruct((T, cf.HKV * LANE), F32)],
        compiler_params=_cp(("arbitrary", "arbitrary")))(q, k, z, dcat)


def _ret_masks(cf, i, tq, lgf, lgb):
    T, LC = cf.T, cf.LC
    row = (lax.broadcasted_iota(jnp.int32, (tq, 1), 0) + i * tq)
    col = lax.broadcasted_iota(jnp.int32, (1, T), 1)

    def pb(n):
        return jnp.where(n < LC, LC - 1 - n, T + LC - 1 - n).astype(F32)

    relf = row.astype(F32) - col.astype(F32)
    relb = pb(row) - pb(col)
    rf, rb = jnp.maximum(relf, 0.0), jnp.maximum(relb, 0.0)
    mf = jnp.where(relf >= 0, jnp.exp(lgf * rf), 0.0)
    mb = jnp.where(relb >= 0, jnp.exp(lgb * rb), 0.0)
    return mf, mb, rf, rb


def _ret_fwd(cf, q, k, z, lg):
    tq, T = cf.TQ, cf.T
    vb = cf.OFF['rv'] // LANE

    def body(lg_ref, q_ref, k_ref, v_ref, o_ref):
        h, i = pl.program_id(0), pl.program_id(1)
        mf, mb, _, _ = _ret_masks(cf, i, tq, lg_ref[0, h], lg_ref[1, h])
        a = lax.dot_general(q_ref[...], k_ref[...], (_DN['nt'], ((), ())), preferred_element_type=F32)
        p = (a * (mf + mb)).astype(BF16)
        o_ref[...] = jnp.dot(p, v_ref[...].astype(BF16), preferred_element_type=F32)

    return pl.pallas_call(
        body, name="ret_fwd", grid=(cf.HR, T // tq),
        in_specs=[pl.BlockSpec(memory_space=pltpu.SMEM), pl.BlockSpec((tq, LANE), lambda h, i: (i, h)),
                  pl.BlockSpec((T, LANE), lambda h, i: (0, h)), pl.BlockSpec((T, LANE), lambda h, i: (0, vb + h))],
        out_specs=pl.BlockSpec((tq, LANE), lambda h, i: (i, h)),
        out_shape=jax.ShapeDtypeStruct((T, cf.HR * LANE), F32), compiler_params=_cp(("arbitrary", "arbitrary")))(lg, q, k, z)


def _ret_bwd(cf, q, k, z, lg, do):
    tq, T = cf.TQ, cf.T
    vb = cf.OFF['rv'] // LANE

    def body(lg_ref, q_ref, k_ref, v_ref, do_ref, dq_ref, dk_ref, dv_ref, dlg_ref):
        h, i = pl.program_id(0), pl.program_id(1)
        mf, mb, rf, rb = _ret_masks(cf, i, tq, lg_ref[0, h], lg_ref[1, h])
        qv, kv, vv = q_ref[...], k_ref[...], v_ref[...].astype(BF16)
        dob = do_ref[...].astype(BF16)
        a = lax.dot_general(qv, kv, (_DN['nt'], ((), ())), preferred_element_type=F32)
        m = mf + mb
        p = (a * m).astype(BF16)
        dv = lax.dot_general(p, dob, (_DN['tn'], ((), ())), preferred_element_type=F32)
        dp = lax.dot_general(dob, vv, (_DN['nt'], ((), ())), preferred_element_type=F32)
        da = (dp * m).astype(BF16)
        dq_ref[...] = jnp.dot(da, kv, preferred_element_type=F32)
        dk = lax.dot_general(da, qv, (_DN['tn'], ((), ())), preferred_element_type=F32)
        dm = dp * a
        dlf = jnp.sum(jnp.sum(dm * mf * rf, axis=-1, keepdims=True), axis=0, keepdims=True)
        dlb = jnp.sum(jnp.sum(dm * mb * rb, axis=-1, keepdims=True), axis=0, keepdims=True)
        rid = lax.broadcasted_iota(jnp.int32, (8, LANE), 0)
        dl = jnp.where(rid == 0, dlf, jnp.where(rid == 1, dlb, 0.0))

        @pl.when(i == 0)
        def _():
            dk_ref[...] = dk
            dv_ref[...] = dv
            dlg_ref[...] = dl

        @pl.when(i != 0)
        def _():
            dk_ref[...] += dk
            dv_ref[...] += dv
            dlg_ref[...] += dl

    qs = pl.BlockSpec((tq, LANE), lambda h, i: (i, h))
    ks = pl.BlockSpec((T, LANE), lambda h, i: (0, h))
    return pl.pallas_call(
        body, name="ret_bwd", grid=(cf.HR, T // tq),
        in_specs=[pl.BlockSpec(memory_space=pltpu.SMEM), qs, ks, pl.BlockSpec((T, LANE), lambda h, i: (0, vb + h)), qs],
        out_specs=[qs, ks, ks, pl.BlockSpec((None, 8, LANE), lambda h, i: (h, 0, 0))],
        out_shape=[jax.ShapeDtypeStruct((T, cf.HR * LANE), F32)] * 3 + [jax.ShapeDtypeStruct((cf.HR, 8, LANE), F32)],
        compiler_params=_cp(("arbitrary", "arbitrary")))(lg, q, k, z, do)


def _gla_step(q, k, v, es, st, lmask, allowed):
    C, SB = GLA_CHUNK, GLA_SUB
    nsb = C // SB
    e1, e2, e3, e4, e5 = es[0], es[1:1 + nsb], es[1 + nsb], es[2 + nsb], es[3 + nsb]
    qs = q * lmask * ((HEAD_DIM // 2) ** -0.5)
    ks = k * lmask
    qt = qs * e1
    rows = [lax.dot_general(qt[b * SB:(b + 1) * SB], ks * e2[b], (_DN['nt'], ((), ())), precision=lax.Precision.HIGH,
                            preferred_element_type=F32) for b in range(nsb)]
    att = jnp.where(allowed, jnp.concatenate(rows, axis=0), 0.0)
    o = jnp.dot(att.astype(BF16), v.astype(BF16), preferred_element_type=F32)
    o += lax.dot_general((qs * e3).astype(BF16), st.astype(BF16), (_DN['nt'], ((), ())), preferred_element_type=F32)
    kd = (ks * e4).astype(BF16)
    st_new = st * jnp.concatenate([e5, e5], axis=0) + lax.dot_general(v.astype(BF16), kd, (_DN['tn'], ((), ())), preferred_element_type=F32)
    return o, st_new


def _gla_allowed(d):
    r = lax.broadcasted_iota(jnp.int32, (GLA_CHUNK, GLA_CHUNK), 0)
    m = lax.broadcasted_iota(jnp.int32, (GLA_CHUNK, GLA_CHUNK), 1)
    return (m <= r) if d == 0 else (m >= r)


def _gla_chunk_id(cf, s, d):
    if d == 0:
        return s
    nct, nc = cf.LC // GLA_CHUNK, cf.T // GLA_CHUNK
    return jnp.where(s < nct, nct - 1 - s, nc + nct - 1 - s)


def _gla_lmask(h):
    return (lax.broadcasted_iota(jnp.int32, (1, LANE), 1) // (LANE // 2) == h).astype(F32)


_GLA_CHAINS = [(h, d) for h in range(2) for d in range(2)]


def _gla_row_specs(cf, nc, reverse):
    def rowblk(s, d):
        return _gla_chunk_id(cf, nc - 1 - s if reverse else s, d)

    def spec(width, base, d, per_pair=1):
        return pl.BlockSpec((GLA_CHUNK, width), functools.partial(lambda p, s, base, d: (rowblk(s, d), base + per_pair * p), base=base, d=d))

    def state(d):
        return pl.BlockSpec((2, None, LANE, LANE), functools.partial(lambda p, s, d: (p, rowblk(s, d), 0, 0), d=d))

    return spec, state


def _gla_fwd(cf, z, ef, eb):
    T, C = cf.T, GLA_CHUNK
    nc = T // C
    qb, kb, vb = cf.OFF['gq'] // LANE, cf.OFF['gk'] // LANE, cf.OFF['gv'] // (2 * LANE)
    spec, state = _gla_row_specs(cf, nc, False)

    def body(qf, kf, vf, e_f, qb_, kb_, vb_, e_b, of, sf, ob, sb, st_scr):
        @pl.when(pl.program_id(1) == 0)
        def _():
            st_scr[...] = jnp.zeros_like(st_scr)

        io = [(qf, kf, vf, e_f, of, sf), (qb_, kb_, vb_, e_b, ob, sb)]
        for ci, (h, d) in enumerate(_GLA_CHAINS):
            q, k, v, e, o_ref, s_ref = io[d]
            cols = slice(h * LANE, (h + 1) * LANE)
            es = [e[:, n * LANE:(n + 1) * LANE] for n in range(N_DECAY)]
            st = st_scr[ci]
            s_ref[h] = st
            o, stn = _gla_step(q[...], k[...], v[:, cols], es, st, _gla_lmask(h), _gla_allowed(d))
            st_scr[ci] = stn
            o_ref[:, cols] = o

    ins, outs = [], []
    for d in range(2):
        ins += [spec(LANE, qb, d), spec(LANE, kb, d), spec(2 * LANE, vb, d), spec(N_DECAY * LANE, 0, d)]
        outs += [spec(2 * LANE, 0, d), state(d)]
    oshape = [jax.ShapeDtypeStruct((T, cf.HG * LANE), F32), jax.ShapeDtypeStruct((cf.HG, nc, LANE, LANE), F32)]
    return pl.pallas_call(
        body, name="gla_fwd", grid=(cf.HG // 2, nc), in_specs=ins, out_specs=outs, out_shape=oshape * 2,
        scratch_shapes=[pltpu.VMEM((4, LANE, LANE), F32)],
        compiler_params=_cp(("arbitrary", "arbitrary")))(z, z, z, ef, z, z, z, eb)


def _gla_bwd(cf, z, ef, eb, sf, sb, do):
    T, C = cf.T, GLA_CHUNK
    nc = T // C
    qb, kb, vb = cf.OFF['gq'] // LANE, cf.OFF['gk'] // LANE, cf.OFF['gv'] // (2 * LANE)
    spec, state = _gla_row_specs(cf, nc, True)

    def body(*refs):
        ins = [refs[0:6], refs[6:12]]
        outs = [refs[12:16], refs[16:20]]
        dst_scr = refs[20]

        @pl.when(pl.program_id(1) == 0)
        def _():
            dst_scr[...] = jnp.zeros_like(dst_scr)

        acc = [None, None]
        for ci, (h, d) in enumerate(_GLA_CHAINS):
            q, k, v, e, s_ref, do_ref = ins[d]
            cols = slice(h * LANE, (h + 1) * LANE)
            es = [e[:, n * LANE:(n + 1) * LANE] for n in range(N_DECAY)]
            step = functools.partial(_gla_step, lmask=_gla_lmask(h), allowed=_gla_allowed(d))
            _, vjp_fn = jax.vjp(step, q[...], k[...], v[:, cols], es, s_ref[h])
            dq, dk, dv, des, dst = vjp_fn((do_ref[:, cols], dst_scr[ci]))
            dst_scr[ci] = dst
            outs[d][2][:, cols] = dv
            part = [dq, dk] + list(des)
            acc[d] = part if acc[d] is None else [a + b for a, b in zip(acc[d], part)]
        for d in range(2):
            dq_ref, dk_ref, _, de_ref = outs[d]
            dq_ref[...] = acc[d][0]
            dk_ref[...] = acc[d][1]
            for n in range(N_DECAY):
                de_ref[:, n * LANE:(n + 1) * LANE] = acc[d][2 + n]

    in_specs, out_specs = [], []
    for d in range(2):
        in_specs += [spec(LANE, qb, d), spec(LANE, kb, d), spec(2 * LANE, vb, d), spec(N_DECAY * LANE, 0, d), state(d), spec(2 * LANE, 0, d)]
        out_specs += [spec(LANE, 0, d), spec(LANE, 0, d), spec(2 * LANE, 0, d), spec(N_DECAY * LANE, 0, d)]
    npair = cf.HG // 2
    oshape = [jax.ShapeDtypeStruct((T, npair * LANE), F32), jax.ShapeDtypeStruct((T, npair * LANE), F32),
              jax.ShapeDtypeStruct((T, cf.HG * LANE), F32), jax.ShapeDtypeStruct((T, npair * N_DECAY * LANE), F32)]
    return pl.pallas_call(
        body, name="gla_bwd", grid=(npair, nc), in_specs=in_specs, out_specs=out_specs, out_shape=oshape * 2,
        scratch_shapes=[pltpu.VMEM((4, LANE, LANE), F32)],
        compiler_params=_cp(("arbitrary", "arbitrary")))(z, z, z, ef, sf, do, z, z, z, eb, sb, do)


def _conv_parts(cf, a, w_ref):
    T, LC = cf.T, cf.LC
    rid = lax.broadcasted_iota(jnp.int32, a.shape, 0)
    first = jnp.logical_or(rid == 0, rid == LC)
    last = jnp.logical_or(rid == LC - 1, rid == T - 1)
    ap = jnp.where(first, 0.0, pltpu.roll(a, 1, axis=0))
    an = jnp.where(last, 0.0, pltpu.roll(a, T - 1, axis=0))
    w0, w1, w2, b = w_ref[0:1, :], w_ref[1:2, :], w_ref[2:3, :], w_ref[3:4, :]
    ac = ap * w0 + a * w1 + an * w2 + b
    return ap, an, ac, first, last, (w0, w1, w2)


def _conv_fwd(cf, u, wb):
    T, Fd = cf.T, cf.F
    tc = _tile(Fd, 512)
    nj = Fd // tc

    def body(a_ref, v_ref, w_ref, t_ref):
        _, _, ac, _, _, _ = _conv_parts(cf, a_ref[...], w_ref)
        t_ref[...] = (_silu(ac) * v_ref[...]).astype(t_ref.dtype)

    return pl.pallas_call(
        body, name="conv_fwd", grid=(nj,),
        in_specs=[pl.BlockSpec((T, tc), lambda j: (0, j)), pl.BlockSpec((T, tc), lambda j: (0, nj + j)),
                  pl.BlockSpec((8, tc), lambda j: (0, j))],
        out_specs=pl.BlockSpec((T, tc), lambda j: (0, j)), out_shape=jax.ShapeDtypeStruct((T, Fd), BF16),
        compiler_params=_cp(("parallel",)))(u, u, wb)


def _conv_bwd(cf, u, wb, dt):
    T, Fd = cf.T, cf.F
    tc = _tile(Fd, 256)
    nj = Fd // tc

    def body(a_ref, v_ref, w_ref, dt_ref, du_ref, dw_ref):
        a, v, dtv = a_ref[...], v_ref[...], dt_ref[...].astype(F32)
        ap, an, ac, first, last, (w0, w1, w2) = _conv_parts(cf, a, w_ref)
        sg = _sigmoid(ac)
        du_ref[1] = (dtv * ac * sg).astype(du_ref.dtype)
        dac = dtv * v * (sg * (1.0 + ac * (1.0 - sg)))
        from_next = pltpu.roll(jnp.where(first, 0.0, dac), T - 1, axis=0)
        from_prev = pltpu.roll(jnp.where(last, 0.0, dac), 1, axis=0)
        du_ref[0] = (dac * w1 + from_next * w0 + from_prev * w2).astype(du_ref.dtype)
        rows = [jnp.sum(dac * ap, axis=0, keepdims=True), jnp.sum(dac * a, axis=0, keepdims=True),
                jnp.sum(dac * an, axis=0, keepdims=True), jnp.sum(dac, axis=0, keepdims=True)]
        rid = lax.broadcasted_iota(jnp.int32, (8, tc), 0)
        dw = jnp.zeros((8, tc), F32)
        for n_, rw in enumerate(rows):
            dw = jnp.where(rid == n_, rw, dw)
        dw_ref[...] = dw

    col = pl.BlockSpec((T, tc), lambda j: (0, j))
    return pl.pallas_call(
        body, name="conv_bwd", grid=(nj,),
        in_specs=[col, pl.BlockSpec((T, tc), lambda j: (0, nj + j)), pl.BlockSpec((8, tc), lambda j: (0, j)), col],
        out_specs=[pl.BlockSpec((2, T, tc), lambda j: (0, 0, j)), pl.BlockSpec((8, tc), lambda j: (0, j))],
        out_shape=[jax.ShapeDtypeStruct((2, T, Fd), BF16), jax.ShapeDtypeStruct((8, Fd), F32)],
        compiler_params=_cp(("parallel",)))(u, u, wb, dt)


def _me():
    x, y, c = lax.axis_index("x"), lax.axis_index("y"), lax.axis_index("c")
    return x, y, c, 4 * x + 2 * y + c


def _peer(x, y, c, k):
    px = 1 - x if (k >> 2) & 1 else x
    py = 1 - y if (k >> 1) & 1 else y
    pc = 1 - c if k & 1 else c
    return (px, py, pc), 4 * px + 2 * py + pc


def _rcopy(src, dst, ss, rs, tgt):
    return pltpu.make_async_remote_copy(src_ref=src, dst_ref=dst, send_sem=ss, recv_sem=rs, device_id=tgt,
                                        device_id_type=pl.DeviceIdType.MESH)


def _ag_small(name, v):
    R, Cc = v.shape

    def body(v_ref, o_ref, ssem, rsem, lsem):
        x, y, c, me = _me()
        loc = pltpu.make_async_copy(v_ref, o_ref.at[me], lsem)
        loc.start()
        sends = []
        for k in range(1, N_DEV):
            tgt, _ = _peer(x, y, c, k)
            cp = _rcopy(v_ref, o_ref.at[me], ssem.at[k - 1], rsem.at[k - 1], tgt)
            cp.start()
            sends.append(cp)
        for k in range(1, N_DEV):
            tgt, pi = _peer(x, y, c, k)
            _rcopy(v_ref, o_ref.at[pi], ssem.at[k - 1], rsem.at[k - 1], tgt).wait_recv()
        for cp in sends:
            cp.wait_send()
        loc.wait()

    vm = pl.BlockSpec(memory_space=pltpu.VMEM)
    return pl.pallas_call(
        body, name=name, in_specs=[vm], out_specs=vm, out_shape=jax.ShapeDtypeStruct((N_DEV, R, Cc), v.dtype),
        scratch_shapes=[pltpu.SemaphoreType.DMA((N_DEV - 1,)), pltpu.SemaphoreType.DMA((N_DEV - 1,)), pltpu.SemaphoreType.DMA],
        compiler_params=pltpu.CompilerParams(vmem_limit_bytes=VMEM_LIMIT))(v)


_KINDS = ['in', 'out', 'up', 'down']


def _shard_shape(cf, kind):
    D, Fd = cf.D, cf.F
    return {'in': (D, cf.NINS), 'out': (D // N_DEV, D), 'up': (D, 2 * Fd // N_DEV), 'down': (Fd // N_DEV, D)}[kind]


def _whole_shape(cf, kind):
    D, Fd = cf.D, cf.F
    return {'in': (N_DEV, D, cf.NINS), 'out': (D, D), 'up': (D, 2 * Fd), 'down': (Fd, D)}[kind]


def _part(ref, cf, kind, idx):
    r, cdim = _shard_shape(cf, kind)
    if kind == 'in':
        return ref.at[idx]
    if kind == 'up':
        return ref.at[:, pl.ds(pl.multiple_of(idx * cdim, cdim), cdim)]
    return ref.at[pl.ds(pl.multiple_of(idx * r, r), r), :]


N_BARRIER_IDS = 8


def _handshake(x, y, c):
    barrier = pltpu.get_barrier_semaphore()
    for k in range(1, N_DEV):
        pl.semaphore_signal(barrier, inc=1, device_id=_peer(x, y, c, k)[0], device_id_type=pl.DeviceIdType.MESH)
    pl.semaphore_wait(barrier, N_DEV - 1)


def _seq_kernel(body, name, seq, out_type, nsem=N_DEV - 1):
    return pl.kernel(
        body, out_type=out_type, mesh=plsc.ScalarSubcoreMesh(axis_name="sq", num_cores=1), name=name,
        scratch_types=[pltpu.SemaphoreType.DMA((nsem,)), pltpu.SemaphoreType.DMA((nsem,)), pltpu.SemaphoreType.DMA],
        compiler_params=pltpu.CompilerParams(collective_id=seq % N_BARRIER_IDS))


def _seq_gather(cf, kind, l, seq, src):
    rows = _shard_shape(cf, kind)[0]

    def body(src_ref, land_ref, ssem, rsem, lsem):
        x, y, c, me = _me()
        _handshake(x, y, c)
        sib, xn, yn, dg = (x, y, 1 - c), (1 - x, y, c), (x, 1 - y, c), (1 - x, 1 - y, c)

        def blk(dev, half=None):
            part = _part(land_ref, cf, kind, 4 * dev[0] + 2 * dev[1] + dev[2])
            return part if half is None else part.at[pl.ds(half * (rows // 2), rows // 2)]

        def copy(n, src, dst, to):
            return _rcopy(src, dst, ssem.at[n], rsem.at[n], to)

        src, mine = src_ref.at[l], blk((x, y, c))
        loc = pltpu.make_async_copy(src, mine, lsem)
        loc.start()
        loc.wait()
        sends = [copy(0, src, mine, sib), copy(1, src, mine, xn), copy(2, src, mine, yn)]
        for cp in sends:
            cp.start()

        def arrived(n, got):
            copy(n, got, got, sib).wait_recv()

        def pass_on(n, part, to):
            sends.append(copy(n, part, part, to))
            sends[-1].start()

        arrived(1, blk(xn))
        pass_on(3, blk(xn, 0), yn)
        pass_on(5, blk(xn), sib)
        arrived(2, blk(yn))
        pass_on(4, blk(yn, 1), xn)
        pass_on(6, blk(yn), sib)
        arrived(3, blk(dg, 0))
        pass_on(7, blk(dg, 0), sib)
        arrived(4, blk(dg, 1))
        pass_on(8, blk(dg, 1), sib)
        other = lambda d: (d[0], d[1], 1 - c)
        for n, got in [(0, blk(other((x, y, c)))), (5, blk(other(xn))), (6, blk(other(yn))), (7, blk(other(dg), 0)), (8, blk(other(dg), 1))]:
            copy(n, got, got, sib).wait_recv()
        for cp in sends:
            cp.wait_send()

    return _seq_kernel(body, "seq_gather_%s_%d" % (kind, l), seq, jax.ShapeDtypeStruct(_whole_shape(cf, kind), BF16), nsem=9)(src)


def _seq_scatter(cf, kind, l, seq, g):
    def body(g_ref, recv_ref, ssem, rsem, lsem):
        x, y, c, me = _me()
        _handshake(x, y, c)
        loc = pltpu.make_async_copy(_rs_slab(g_ref, cf, kind, me), recv_ref.at[me], lsem)
        loc.start()
        loc.wait()
        sends = []
        for k in range(1, N_DEV):
            tgt, pi = _peer(x, y, c, k)
            sends.append(_rcopy(_rs_slab(g_ref, cf, kind, pi), recv_ref.at[me], ssem.at[k - 1], rsem.at[k - 1], tgt))
            sends[-1].start()
        for k in range(1, N_DEV):
            tgt, pi = _peer(x, y, c, k)
            _rcopy(_rs_slab(g_ref, cf, kind, pi), recv_ref.at[pi], ssem.at[k - 1], rsem.at[k - 1], tgt).wait_recv()
        for cp in sends:
            cp.wait_send()

    return _seq_kernel(body, "seq_scatter_%s_%d" % (kind, l), seq, jax.ShapeDtypeStruct((N_DEV,) + _shard_shape(cf, kind), BF16))(g)


def _rs_slab(ref, cf, kind, j):
    return ref.at[j] if kind in ('in', 'up') else _part(ref, cf, kind, j)


def _adam_vals(w, g, m, v):
    m2 = ADAM_B1 * m + (1.0 - ADAM_B1) * g
    v2 = ADAM_B2 * v + (1.0 - ADAM_B2) * (g * g)
    mh = m2 / (1.0 - ADAM_B1 ** ADAM_STEP)
    vh = v2 / (1.0 - ADAM_B2 ** ADAM_STEP)
    return -ADAM_LR * (mh / (jnp.sqrt(vh) + ADAM_EPS) + ADAM_WD * w), m2, v2


def _row_tile(R, Cc, budget_elems):
    t = max(16, min(R, (budget_elems // max(Cc, 1)) // 16 * 16))
    while t > 16 and R % t:
        t -= 16
    return t if R % t == 0 else R


def _cast_bf16(name, w, after):
    Dp, R, Cc = w.shape
    tr = _row_tile(R, Cc, 1 << 20)

    def body(w_ref, after_ref, o_ref):
        o_ref[...] = w_ref[...].astype(BF16)

    spec = pl.BlockSpec((None, tr, Cc), lambda l, i: (l, i, 0))
    return pl.pallas_call(body, name=name, grid=(Dp, R // tr), in_specs=[spec, pl.BlockSpec(memory_space=pl.ANY)], out_specs=spec,
                          out_shape=jax.ShapeDtypeStruct(w.shape, BF16), compiler_params=_cp(("parallel", "parallel")))(w, after)


def _unshard_in(cf, name, g):
    D, ns, nz = cf.D, cf.NINS, cf.NZ
    tr = _tile(D, 256, 16)

    def body(g_ref, o_ref):
        for j in range(N_DEV):
            o_ref[:, ns * j:ns * (j + 1)] = g_ref[j]
        o_ref[:, N_DEV * ns:] = jnp.zeros((tr, nz - N_DEV * ns), o_ref.dtype)

    return pl.pallas_call(body, name=name, grid=(D // tr,), in_specs=[pl.BlockSpec((N_DEV, tr, ns), lambda i: (0, i, 0))],
                          out_specs=pl.BlockSpec((tr, nz), lambda i: (i, 0)), out_shape=jax.ShapeDtypeStruct((D, nz), g.dtype),
                          compiler_params=_cp(("parallel",)))(g)


def _slabs_in(cf, name, gw):
    D, ns, nz = cf.D, cf.NINS, cf.NZ
    tr = _tile(D, 256, 16)

    def body(x_ref, o_ref):
        for j in range(N_DEV):
            o_ref[j] = x_ref[:, ns * j:ns * (j + 1)]

    return pl.pallas_call(body, name=name, grid=(D // tr,), in_specs=[pl.BlockSpec((tr, nz), lambda i: (i, 0))],
                          out_specs=pl.BlockSpec((N_DEV, tr, ns), lambda i: (0, i, 0)), out_shape=jax.ShapeDtypeStruct((N_DEV, D, ns), gw.dtype),
                          compiler_params=_cp(("parallel",)))(gw)


def _sum_adam(name, recv, w, m, v, layer, prev, after):
    Dp, R, Cc = w.shape
    tr = _row_tile(R, Cc, 1 << 18)

    def body(r_ref, w_ref, m_ref, v_ref, *rest):
        g_ref, d_ref, mo_ref, vo_ref = rest[-4:]
        g = r_ref[0].astype(F32)
        for s in range(1, N_DEV):
            g = g + r_ref[s].astype(F32)
        dl, m2, v2 = _adam_vals(w_ref[...], g, m_ref[...], v_ref[...])
        g_ref[...] = g
        d_ref[...] = dl
        mo_ref[...] = m2
        vo_ref[...] = v2

    spec = pl.BlockSpec((None, tr, Cc), lambda i: (layer, i, 0))
    anyspec = pl.BlockSpec(memory_space=pl.ANY)
    extra = [after] + (list(prev) if prev is not None else [])
    aliases = {5 + n: n for n in range(4)} if prev is not None else {}
    return pl.pallas_call(body, name=name, grid=(R // tr,), in_specs=[pl.BlockSpec((N_DEV, tr, Cc), lambda i: (0, i, 0)), spec, spec, spec] + [anyspec] * len(extra),
                          out_specs=[spec] * 4, out_shape=[jax.ShapeDtypeStruct(w.shape, F32)] * 4, input_output_aliases=aliases,
                          compiler_params=_cp(("parallel",)))(recv, w, m, v, *extra)


def _adam(name, w, g, m, v):
    R, Cc = w.shape
    tr = _row_tile(R, Cc, 1 << 18)

    def body(w_ref, g_ref, m_ref, v_ref, d_ref, mo_ref, vo_ref):
        dl, m2, v2 = _adam_vals(w_ref[...], g_ref[...], m_ref[...], v_ref[...])
        d_ref[...] = dl
        mo_ref[...] = m2
        vo_ref[...] = v2

    spec = pl.BlockSpec((tr, Cc), lambda i: (i, 0))
    return pl.pallas_call(body, name=name, grid=(R // tr,), in_specs=[spec] * 4, out_specs=[spec] * 3,
                          out_shape=[jax.ShapeDtypeStruct(w.shape, F32)] * 3, compiler_params=_cp(("parallel",)))(w, g, m, v)


def _sum8(name, a):
    n, R, Cc = a.shape

    def body(a_ref, o_ref):
        s = a_ref[0]
        for k in range(1, n):
            s = s + a_ref[k]
        o_ref[...] = s

    return pl.pallas_call(body, name=name, in_specs=[pl.BlockSpec(memory_space=pltpu.VMEM)],
                          out_specs=pl.BlockSpec(memory_space=pltpu.VMEM), out_shape=jax.ShapeDtypeStruct((R, Cc), F32),
                          compiler_params=pltpu.CompilerParams(vmem_limit_bytes=VMEM_LIMIT))(a)


def _ada_fwd(cf, c9, ada_w):
    D = cf.D
    NS = ada_w.shape[-1]
    tk = _tile(D, 512)
    nk = D // tk

    def body(c_ref, w_ref, o_ref):
        kk = pl.program_id(1)
        s = _silu(c_ref[...]).astype(BF16)
        part = jnp.dot(s, w_ref[...].astype(BF16), preferred_element_type=F32)

        @pl.when(kk == 0)
        def _():
            o_ref[...] = part

        @pl.when(kk != 0)
        def _():
            o_ref[...] += part

    return pl.pallas_call(
        body, name="ada_fwd", grid=(DEPTH, nk),
        in_specs=[pl.BlockSpec((16, tk), lambda l, k: (0, k)), pl.BlockSpec((None, tk, NS), lambda l, k: (l, k, 0))],
        out_specs=pl.BlockSpec((None, 16, NS), lambda l, k: (l, 0, 0)),
        out_shape=jax.ShapeDtypeStruct((DEPTH, 16, NS), F32), compiler_params=_cp(("parallel", "arbitrary")))(c9, ada_w)


def _ada_bwd(cf, c9, ada_w, dm9):
    D = cf.D
    NS = ada_w.shape[-1]
    tk = _tile(D, 512)
    nk = D // tk

    def body(c_ref, w_ref, dm_ref, gw_ref, ds_ref):
        cv = c_ref[...]
        sg = _sigmoid(cv)
        dmb = dm_ref[...].astype(BF16)
        gw_ref[...] = lax.dot_general((cv * sg).astype(BF16), dmb, (_DN['tn'], ((), ())), preferred_element_type=F32)
        ds = lax.dot_general(dmb, w_ref[...].astype(BF16), (_DN['nt'], ((), ())), preferred_element_type=F32)
        ds_ref[...] = ds * (sg * (1.0 + cv * (1.0 - sg)))

    return pl.pallas_call(
        body, name="ada_bwd", grid=(DEPTH, nk),
        in_specs=[pl.BlockSpec((16, tk), lambda l, k: (0, k)), pl.BlockSpec((None, tk, NS), lambda l, k: (l, k, 0)),
                  pl.BlockSpec((None, 16, NS), lambda l, k: (l, 0, 0))],
        out_specs=[pl.BlockSpec((None, tk, NS), lambda l, k: (l, k, 0)), pl.BlockSpec((None, 16, tk), lambda l, k: (l, 0, k))],
        out_shape=[jax.ShapeDtypeStruct((DEPTH, D, NS), F32), jax.ShapeDtypeStruct((DEPTH, 16, D), F32)],
        compiler_params=_cp(("parallel", "parallel")))(c9, ada_w, dm9)


def _rope_tables(cf):
    L, LC = cf.L, cf.LC
    rows = L // GRID_W
    row = jnp.repeat(jnp.arange(rows, dtype=F32), GRID_W)
    col = jnp.tile(jnp.arange(GRID_W, dtype=F32), rows)
    nf = HEAD_DIM // 4
    inv = ROPE_THETA ** (-jnp.arange(nf, dtype=F32) / nf)
    ang = jnp.concatenate([row[:, None] * inv, col[:, None] * inv], axis=-1)
    cos, sin = jnp.cos(ang), jnp.sin(ang)
    cs = jnp.concatenate([jnp.ones((LC, HEAD_DIM), F32), jnp.concatenate([cos, cos], -1)], 0)
    sn = jnp.concatenate([jnp.zeros((LC, HEAD_DIM), F32), jnp.concatenate([-sin, sin], -1)], 0)
    return cs, sn


def _prep_tiles(cf, z, cs, sn, key):
    b = cf.OFF[key] // LANE
    return [(z, LANE, _col(b), True), (cs, LANE, _c0, False), (sn, LANE, _c0, False)]


_PREP = {'aq': _f_prep_norm, 'ak': _f_prep_norm, 'rq': _f_prep_plain, 'rk': _f_prep_scaled}


def _prep_fwd(cf, z, cs, sn, key, g):
    nh = cf.W[key] // LANE
    params = [(g, 'shared', True)] if g is not None else []
    return _row_fwd(cf, "prep_fwd_" + key, _PREP[key], _prep_tiles(cf, z, cs, sn, key), params,
                    [(LANE, _col(0), cf.W[key], BF16)], cf.TQ, nrep=nh)[0]


def _prep_bwd(cf, z, cs, sn, key, g, dt):
    nh = cf.W[key] // LANE
    params = [(g, 'shared', True)] if g is not None else []
    tg, pg = _row_bwd(cf, "prep_bwd_" + key, _PREP[key], _prep_tiles(cf, z, cs, sn, key), params,
                      [(dt, LANE, _col(0))], [(LANE, _col(0), cf.W[key], BF16)], cf.TQ, nrep=nh)
    return tg[0], (pg[0] if g is not None else None)


def _gate_params(cf, gup, gb):
    K = gup.shape[-1]
    gf = jnp.zeros((LANE, K), F32).at[0:GLA_RANK].set(gup[0])
    gbm = jnp.zeros((LANE, K), F32).at[GLA_RANK:2 * GLA_RANK].set(gup[1])
    out = []
    for p in range(K // LANE):
        cols = slice(p * LANE, (p + 1) * LANE)
        out += [(gf[:, cols], 'shared', True), (gbm[:, cols], 'shared', True), (gb[0:1, cols], 'shared', True), (gb[1:2, cols], 'shared', True)]
    return out


def _mix_tiles(cf, z, o, key):
    return [(o, LANE, _col(0), True), (z, LANE, _col(cf.OFF[key] // LANE), True)]


def _mid_io(cf, l, W, mod, x, y, norm2_g=None):
    tiles = [(x, cf.D, _c0, True), (y, cf.D, _c0, True)]
    n2 = W['norm2_g'][l] if norm2_g is None else norm2_g
    params = [(mod[2], 'stream', True), (n2, 'shared', True), (mod[3], 'stream', True), (mod[4], 'stream', True)]
    return tiles, params


def _launch_scatter(cf, kind, l, seq, g, nxt):
    g, nxt = lax.optimization_barrier((g, nxt))
    return _seq_scatter(cf, kind, l, seq, g), nxt


class _BigWeights:
    def __init__(self, cf, shards):
        self.cf = cf
        self.whole = {(kind, l): _seq_gather(cf, kind, l, l * len(_KINDS) + n, shards[n])
                      for l in range(DEPTH) for n, kind in enumerate(_KINDS)}
        self.w_in = {}

    def get(self, kind, l, after=None):
        cf = self.cf
        if kind != 'in':
            return self.whole[(kind, l)]
        if l not in self.w_in:
            whole, _ = lax.optimization_barrier((self.whole[(kind, l)], after))
            self.w_in[l] = _unshard_in(cf, "unshard_in_%d" % l, whole)
        return self.w_in[l]


def _layer_fwd(cf, l, W, big, mod, x, h, cs, sn):
    T, D, Fd = cf.T, cf.D, cf.F
    z = _mm("z_%d" % l, h, big.get('in', l, h), 'nn', T, cf.NZ, D, F32, tm=T, tn=768, tk=D)
    qa = _prep_fwd(cf, z, cs, sn, 'aq', W['q_norm_g'][l])
    ka = _prep_fwd(cf, z, cs, sn, 'ak', W['k_norm_g'][l])
    qr = _prep_fwd(cf, z, cs, sn, 'rq', None)
    kr = _prep_fwd(cf, z, cs, sn, 'rk', None)
    o_att = _att_fwd(cf, qa, ka, z)
    o_ret = _ret_fwd(cf, qr, kr, z, W['ret_log_decay'][l])
    gates = _gate_params(cf, W['gla_gate_up'][l], W['gla_gate_b'][l])
    ga_tile = [(z, LANE, _col(cf.OFF['ga'] // LANE), True)]
    we = (cf.HG // 2) * N_DECAY * LANE
    ef, eb = _row_fwd(cf, "gates_fwd_%d" % l, _f_gla_pre, ga_tile, gates, [(we, _c0, we, F32), (we, _c0, we, F32)], GLA_CHUNK)
    o_f, sf, o_b, sb = _gla_fwd(cf, z, ef, eb)
    o_gla = o_f + o_b
    cat_r = _row_fwd(cf, "mixr_fwd_%d" % l, _f_gated_norm, _mix_tiles(cf, z, o_ret, 'rg'), [(W['ret_norm_g'][l], 'shared', True)],
                     [(LANE, _col(0), cf.HR * LANE, BF16)], cf.TQ, nrep=cf.HR)[0]
    cat_g = _row_fwd(cf, "mixg_fwd_%d" % l, _f_gated_norm, _mix_tiles(cf, z, o_gla, 'gr'), [(W['gla_norm_g'][l], 'shared', True)],
                     [(LANE, _col(0), cf.HG * LANE, BF16)], cf.TQ, nrep=cf.HG)[0]
    cat = jnp.concatenate([o_att, cat_r, cat_g], axis=-1)
    y = _mm("y_%d" % l, cat, big.get('out', l, cat), 'nn', T, D, D, F32, tm=T, tn=512, tk=D)
    tiles, params = _mid_io(cf, l, W, mod, x, y)
    x1, h2 = _row_fwd(cf, "mid_fwd_%d" % l, _f_resid_norm_mod, tiles, params, [(D, _c0, D, F32), (D, _c0, D, BF16)], cf.TM)
    u = _mm("u_%d" % l, h2, big.get('up', l, h2), 'nn', T, 2 * Fd, D, F32, tm=T, tn=512, tk=D)
    t = _conv_fwd(cf, u, W['conv_wb'][l])
    yff = _mm("yff_%d" % l, t, big.get('down', l, t), 'nn', T, D, Fd, F32, tm=T, tn=1024, tk=512)
    return dict(x=x, h=h, z=z, qa=qa, ka=ka, qr=qr, kr=kr, ef=ef, eb=eb, sf=sf, sb=sb, o_ret=o_ret, o_gla=o_gla, cat=cat, y=y,
                x1=x1, h2=h2, u=u, t=t, yff=yff, gates=gates)


def _layer_bwd(cf, l, W, big, mod, sv, dx1, dyff, cs, sn):
    T, D, Fd = cf.T, cf.D, cf.F
    g, rs = {}, {}
    sq = 2 * len(_KINDS) + (DEPTH - 1 - l) * len(_KINDS)
    gwd = _mm("gwd_%d" % l, sv['t'], dyff, 'tn', Fd, D, T, BF16, tm=1408, tn=2048, tk=T)
    rs['down'], wb = _launch_scatter(cf, 'down', l, sq, gwd, W['conv_wb'][l])
    dt = _mm("dt_%d" % l, dyff, big.get('down', l), 'nt', T, Fd, D, BF16, tm=T, tn=1408, tk=D)
    du, g['conv_wb'] = _conv_bwd(cf, sv['u'], wb, dt)
    cu = 2 * Fd // N_DEV
    half = Fd // cu
    gwu = _mm("gwu_%d" % l, sv['h2'], du, 'tn', D, 2 * Fd, T, BF16, tm=D, tn=cu, tk=T, out_shape=(N_DEV, D, cu),
              b_spec=pl.BlockSpec((None, T, cu), lambda i, j, k: (j // half, 0, j % half)),
              out_spec=pl.BlockSpec((None, D, cu), lambda i, j, k: (j, i, 0)))
    rs['up'], n2 = _launch_scatter(cf, 'up', l, sq + 1, gwu, W['norm2_g'][l])
    dh2 = _mm("dh2_%d" % l, du, big.get('up', l), 'nt', T, D, 2 * Fd, BF16, tm=T, tn=1024, tk=cu,
              a_spec=pl.BlockSpec((None, T, cu), lambda i, j, k: (k // half, 0, k % half)))
    tiles, params = _mid_io(cf, l, W, mod, sv['x'], sv['y'], n2)
    (dx, dy), (g['m2'], g['norm2_g'], g['m3'], g['m4']) = _row_bwd(
        cf, "mid_bwd_%d" % l, _f_resid_norm_mod, tiles, params, [(dx1, D, _c0), (dh2, D, _c0)],
        [(D, _c0, D, F32), (D, _c0, D, BF16)], cf.TM)
    gwo = _mm("gwo_%d" % l, sv['cat'], dy, 'tn', D, D, T, BF16, tm=D, tn=1024, tk=T)
    rs['out'], rn = _launch_scatter(cf, 'out', l, sq + 2, gwo, W['ret_norm_g'][l])
    dcat = _mm("dcat_%d" % l, dy, big.get('out', l), 'nt', T, D, D, BF16, tm=T, tn=1024, tk=D)
    z = sv['z']
    (do_ret, drg), (g['ret_norm_g'],) = _row_bwd(
        cf, "mixr_bwd_%d" % l, _f_gated_norm, _mix_tiles(cf, z, sv['o_ret'], 'rg'), [(rn, 'shared', True)],
        [(dcat, LANE, _col(cf.HQ))], [(LANE, _col(0), cf.HR * LANE, F32), (LANE, _col(0), cf.HR * LANE, BF16)], cf.TQ, nrep=cf.HR)
    (do_gla, dgr), (g['gla_norm_g'],) = _row_bwd(
        cf, "mixg_bwd_%d" % l, _f_gated_norm, _mix_tiles(cf, z, sv['o_gla'], 'gr'), [(W['gla_norm_g'][l], 'shared', True)],
        [(dcat, LANE, _col(cf.HQ + cf.HR))], [(LANE, _col(0), cf.HG * LANE, F32), (LANE, _col(0), cf.HG * LANE, BF16)], cf.TQ, nrep=cf.HG)
    dqa, dka, dav = _att_bwd(cf, sv['qa'], sv['ka'], z, dcat)
    dqr, dkr, drv, dlg = _ret_bwd(cf, sv['qr'], sv['kr'], z, W['ret_log_decay'][l], do_ret)
    g['ret_log_decay'] = dlg[:, 0:2, 0].T
    dq_f, dk_f, dv_f, def_, dq_b, dk_b, dv_b, deb = _gla_bwd(cf, z, sv['ef'], sv['eb'], sv['sf'], sv['sb'], do_gla)
    dgq, dgk, dgv = dq_f + dq_b, dk_f + dk_b, dv_f + dv_b
    we = (cf.HG // 2) * N_DECAY * LANE
    ga_tile = [(z, LANE, _col(cf.OFF['ga'] // LANE), True)]
    (dga,), gg = _row_bwd(cf, "gates_bwd_%d" % l, _f_gla_pre, ga_tile, sv['gates'],
                          [(def_, we, _c0), (deb, we, _c0)], [(LANE, _c0, LANE, BF16)], GLA_CHUNK)
    ggf, ggb, gbf, gbb = [jnp.concatenate(gg[n::4], axis=-1) for n in range(4)]
    g['gla_gate_up'] = jnp.stack([ggf[0:GLA_RANK], ggb[GLA_RANK:2 * GLA_RANK]])
    g['gla_gate_b'] = jnp.concatenate([gbf, gbb], axis=0)
    daq, g['q_norm_g'] = _prep_bwd(cf, z, cs, sn, 'aq', W['q_norm_g'][l], dqa)
    dak, g['k_norm_g'] = _prep_bwd(cf, z, cs, sn, 'ak', W['k_norm_g'][l], dka)
    drq, _ = _prep_bwd(cf, z, cs, sn, 'rq', None, dqr)
    drk, _ = _prep_bwd(cf, z, cs, sn, 'rk', None, dkr)
    pad = jnp.zeros((T, cf.NZ - cf.OFF['ga'] - LANE), BF16)
    dz = jnp.concatenate([daq, dak, dav.astype(BF16), drq, drk, drv.astype(BF16), drg, dgq.astype(BF16), dgk.astype(BF16),
                          dgv.astype(BF16), dgr, dga, pad], axis=-1)
    gwi = _mm("gwi_%d" % l, sv['h'], dz, 'tn', D, cf.NZ, T, BF16, tm=D, tn=768, tk=T)
    rs['in'], g['norm1_g_tied'] = _launch_scatter(cf, 'in', l, sq + 3, _slabs_in(cf, "slabs_in_%d" % l, gwi), W['norm1_g'][l])
    dh =_mm("dh_%d" % l, dz, big.get('in', l), 'nt', T, D, cf.NZ, BF16, tm=T, tn=1024, tk=1792)
    g['rs'] = rs
    return dx, dh, g


_WEIGHTS = ['c_ctx', 'ada_w', 'ada_b', 'norm1_g', 'w_in', 'q_norm_g', 'k_norm_g', 'ret_log_decay', 'ret_norm_g',
            'gla_gate_up', 'gla_gate_b', 'gla_norm_g', 'w_out', 'norm2_g', 'w_up', 'conv_w', 'conv_b', 'w_down', 'final_norm_g']
_BIG = ['w_in', 'w_out', 'w_up', 'w_down']
_SMALL = [n for n in _WEIGHTS if n not in _BIG and n != 'ada_w']
_COL_SHARDED = ['gla_gate_up', 'gla_gate_b', 'conv_w']


def _pack(arrs):
    rows = []
    for a in arrs:
        flat = a.reshape(-1)
        n = flat.shape[0]
        rows.append(jnp.pad(flat, (0, -n % LANE)).reshape(-1, LANE))
    packed = jnp.concatenate(rows, axis=0)
    return jnp.pad(packed, ((0, -packed.shape[0] % 8), (0, 0)))


def _unpack(packed, shapes):
    lead = packed.shape[:-2]
    out, r = [], 0
    for s in shapes:
        n = int(np.prod(s))
        nr = -(-n // LANE)
        out.append(packed[..., r:r + nr, :].reshape(lead + (nr * LANE,))[..., :n].reshape(lead + tuple(s)))
        r += nr
    return out


def _unshard_last(a):
    return jnp.moveaxis(a, 0, -2).reshape(a.shape[1:-1] + (N_DEV * a.shape[-1],))


def _step(cf, x, c, ctx, loss_target, w, m, v):
    T, D, Fd, L, LC = cf.T, cf.D, cf.F, cf.L, cf.LC
    _, _, _, me = _me()
    NS = w['ada_w'].shape[-1]

    c_all = _ag_small("ag_c", jnp.pad(c, ((0, 7), (0, 0))))[:, 0, :]
    c9 = jnp.concatenate([c_all, w['c_ctx'][None], jnp.zeros((7, D), F32)], axis=0)
    pm = _ada_fwd(cf, c9, w['ada_w'])
    pm_all = _ag_small("ag_mod", pm.reshape(DEPTH * 16, NS)).reshape(N_DEV, DEPTH, 16, NS)
    mod_all = _unshard_last(pm_all) + w['ada_b'][:, None, :]
    mod_own = lax.dynamic_index_in_dim(mod_all, me, axis=1, keepdims=False)
    mods = []
    for l in range(DEPTH):
        mods.append([jnp.stack([mod_all[l, 8, k * D:(k + 1) * D], mod_own[l, k * D:(k + 1) * D]])[:, None, :] for k in range(N_MOD)])

    shard_shapes = [w[n].shape for n in _COL_SHARDED]
    got = _ag_small("ag_smallw", _pack([w[n] for n in _COL_SHARDED]))
    full = dict(zip(_COL_SHARDED, [_unshard_last(a) for a in _unpack(got, shard_shapes)]))

    small_done = full['conv_w'] + mod_all[0, 0, 0]
    big = _BigWeights(cf, [_cast_bf16("cast_" + n, w[n], c if n == 'w_in' else small_done) for n in _BIG])
    conv_wb = jnp.concatenate([full['conv_w'], w['conv_b'][:, None, :], jnp.zeros((DEPTH, 4, Fd), F32)], axis=1)
    W = dict(conv_wb=conv_wb, gla_gate_up=full['gla_gate_up'], gla_gate_b=full['gla_gate_b'], ret_log_decay=w['ret_log_decay'])
    for n in ['q_norm_g', 'k_norm_g', 'ret_norm_g', 'gla_norm_g', 'norm1_g', 'norm2_g']:
        W[n] = w[n][:, None, :]

    cs, sn = _rope_tables(cf)
    x0 = jnp.concatenate([ctx[0], x[0]], axis=0)
    pre_tiles = [(x0, D, _c0, True)]

    def pre_params(n1):
        return [(n1, 'shared', True), (mods[0][0], 'stream', True), (mods[0][1], 'stream', True)]

    def tr_params(n1):
        return [(mods[0][5], 'stream', True), (n1, 'shared', True), (mods[1][0], 'stream', True), (mods[1][1], 'stream', True)]

    h0 = _row_fwd(cf, "pre_fwd", _f_norm_mod, pre_tiles, pre_params(W['norm1_g'][0]), [(D, _c0, D, BF16)], cf.TM)[0]
    sv0 = _layer_fwd(cf, 0, W, big, mods[0], x0, h0, cs, sn)
    tr_tiles = [(sv0['x1'], D, _c0, True), (sv0['yff'], D, _c0, True)]
    xb, hb = _row_fwd(cf, "tr_fwd", _f_resid_norm_mod, tr_tiles, tr_params(W['norm1_g'][1]), [(D, _c0, D, F32), (D, _c0, D, BF16)], cf.TM)
    sv1 = _layer_fwd(cf, 1, W, big, mods[1], xb, hb, cs, sn)
    tgt = jnp.concatenate([jnp.zeros((LC, D), F32), loss_target[0]], axis=0)
    dx1, dyff, dm5_1, g_final, ls = _loss_grad(cf, sv1['x1'], sv1['yff'], mods[1][5], w['final_norm_g'][None], tgt)
    loss = lax.psum(ls[0, 0], ("x", "y", "c"))

    dxb, dhb, g1 = _layer_bwd(cf, 1, W, big, mods[1], sv1, dx1, dyff, cs, sn)
    (dx1_0, dyff_0), (dm5_0, gn1_1, dm0_1, dm1_1) = _row_bwd(
        cf, "tr_bwd", _f_resid_norm_mod, tr_tiles, tr_params(g1['norm1_g_tied']), [(dxb, D, _c0), (dhb, D, _c0)],
        [(D, _c0, D, F32), (D, _c0, D, BF16)], cf.TM)
    dx0, dh0, g0 = _layer_bwd(cf, 0, W, big, mods[0], sv0, dx1_0, dyff_0, cs, sn)
    (dxa,), (gn1_0, dm0_0, dm1_0) = _row_bwd(cf, "pre_bwd", _f_first, pre_tiles, pre_params(g0['norm1_g_tied']), [(dx0, D, _c0), (dh0, D, _c0)],
                                            [(D, _c0, D, F32)], cf.TM)
    grad_x = dxa[LC:][None]

    dmod = jnp.stack([jnp.concatenate([dm0_0, dm1_0, g0['m2'], g0['m3'], g0['m4'], dm5_0], axis=-1)[:, 0],
                      jnp.concatenate([dm0_1, dm1_1, g1['m2'], g1['m3'], g1['m4'], dm5_1], axis=-1)[:, 0]])
    dm_all = _ag_small("ag_dmod", jnp.pad(dmod.reshape(2 * DEPTH, N_MOD * D), ((0, 8 - 2 * DEPTH), (0, 0))))
    dm_all = dm_all[:, :2 * DEPTH].reshape(N_DEV, DEPTH, 2, N_MOD * D)
    dctx = _sum8("sum_dmodc", jnp.pad(dm_all[:, :, 0], ((0, 0), (0, 8 - DEPTH), (0, 0))))[:DEPTH]
    dm9 = jnp.concatenate([jnp.moveaxis(dm_all[:, :, 1], 0, 1), dctx[:, None]], axis=1)
    g_ada_b = _sum8("sum_adab", jnp.pad(jnp.moveaxis(dm9, 1, 0), ((0, 0), (0, 8 - DEPTH), (0, 0))))[:DEPTH]
    dm9s = lax.dynamic_slice_in_dim(jnp.pad(dm9, ((0, 0), (0, 7), (0, 0))), me * NS, NS, axis=2)
    g_ada_w, dsil = _ada_bwd(cf, c9, w['ada_w'], dm9s)
    g_cctx_part = dsil[0, 8]
    for l in range(1, DEPTH):
        g_cctx_part = g_cctx_part + dsil[l, 8]

    def both(key):
        return jnp.stack([g0[key], g1[key]])

    gsmall = dict(c_ctx=g_cctx_part, norm1_g=jnp.stack([gn1_0[0], gn1_1[0]]), q_norm_g=both('q_norm_g')[:, 0],
                  k_norm_g=both('k_norm_g')[:, 0], ret_log_decay=both('ret_log_decay'), ret_norm_g=both('ret_norm_g')[:, 0],
                  gla_gate_up=both('gla_gate_up'), gla_gate_b=both('gla_gate_b'), gla_norm_g=both('gla_norm_g')[:, 0],
                  norm2_g=both('norm2_g')[:, 0], conv_w=both('conv_wb')[:, 0:3], conv_b=both('conv_wb')[:, 3], final_norm_g=g_final[0])
    snames = [n for n in _SMALL if n != 'ada_b']
    sshapes = [gsmall[n].shape for n in snames]
    gs_all = _ag_small("ag_gsmall", _pack([gsmall[n] for n in snames]))
    gs = dict(zip(snames, _unpack(_sum8("sum_gsmall", gs_all), sshapes)))
    gs['ada_b'] = g_ada_b
    for n in _COL_SHARDED:
        ns_ = w[n].shape[-1]
        gs[n] = lax.dynamic_slice_in_dim(gs[n], me * ns_, ns_, axis=gs[n].ndim - 1)

    out_g, out_d, out_m, out_v = {}, {}, {}, {}

    after, done = gs_all, {}
    for l, gl in ((1, g1), (0, g0)):
        for kind in reversed(_KINDS):
            n = 'w_' + kind
            done[n] = _sum_adam("adam_%s_%d" % (n, l), gl['rs'][kind], w[n], m[n], v[n], l, done.get(n), after)
            after = done[n][0]
    for n in _BIG:
        out_g[n], out_d[n], out_m[n], out_v[n] = done[n]
    aw = [a.reshape(DEPTH * D, NS) for a in (w['ada_w'], g_ada_w, m['ada_w'], v['ada_w'])]
    out_g['ada_w'] = g_ada_w
    out_d['ada_w'], out_m['ada_w'], out_v['ada_w'] = [a.reshape(DEPTH, D, NS) for a in _adam("adam_ada_w", *aw)]
    shp = [w[n].shape for n in _SMALL]
    packed = [_pack([src[n] for n in _SMALL]) for src in (w, gs, m, v)]
    res = _adam("adam_small", *packed)
    for dst, pk in zip((out_d, out_m, out_v), res):
        dst.update(zip(_SMALL, _unpack(pk, shp)))
    out_g.update({n: gs[n] for n in _SMALL})
    return (loss, grad_x, *[out_g[n] for n in _WEIGHTS], *[out_d[n] for n in _WEIGHTS], *[out_m[n] for n in _WEIGHTS],
            *[out_v[n] for n in _WEIGHTS])


def kernel(x, c, ctx, c_ctx, ada_w, ada_b, norm1_g, w_in, q_norm_g, k_norm_g, ret_log_decay, ret_norm_g, gla_gate_up, gla_gate_b, gla_norm_g, w_out, norm2_g, w_up, conv_w, conv_b, w_down, final_norm_g, loss_target, m_c_ctx, m_ada_w, m_ada_b, m_norm1_g, m_w_in, m_q_norm_g, m_k_norm_g, m_ret_log_decay, m_ret_norm_g, m_gla_gate_up, m_gla_gate_b, m_gla_norm_g, m_w_out, m_norm2_g, m_w_up, m_conv_w, m_conv_b, m_w_down, m_final_norm_g, v_c_ctx, v_ada_w, v_ada_b, v_norm1_g, v_w_in, v_q_norm_g, v_k_norm_g, v_ret_log_decay, v_ret_norm_g, v_gla_gate_up, v_gla_gate_b, v_gla_norm_g, v_w_out, v_norm2_g, v_w_up, v_conv_w, v_conv_b, v_w_down, v_final_norm_g):
    w = dict(c_ctx=c_ctx, ada_w=ada_w, ada_b=ada_b, norm1_g=norm1_g, w_in=w_in, q_norm_g=q_norm_g, k_norm_g=k_norm_g,
             ret_log_decay=ret_log_decay, ret_norm_g=ret_norm_g, gla_gate_up=gla_gate_up, gla_gate_b=gla_gate_b,
             gla_norm_g=gla_norm_g, w_out=w_out, norm2_g=norm2_g, w_up=w_up, conv_w=conv_w, conv_b=conv_b, w_down=w_down,
             final_norm_g=final_norm_g)
    m = dict(c_ctx=m_c_ctx, ada_w=m_ada_w, ada_b=m_ada_b, norm1_g=m_norm1_g, w_in=m_w_in, q_norm_g=m_q_norm_g,
             k_norm_g=m_k_norm_g, ret_log_decay=m_ret_log_decay, ret_norm_g=m_ret_norm_g, gla_gate_up=m_gla_gate_up,
             gla_gate_b=m_gla_gate_b, gla_norm_g=m_gla_norm_g, w_out=m_w_out, norm2_g=m_norm2_g, w_up=m_w_up,
             conv_w=m_conv_w, conv_b=m_conv_b, w_down=m_w_down, final_norm_g=m_final_norm_g)
    v = dict(c_ctx=v_c_ctx, ada_w=v_ada_w, ada_b=v_ada_b, norm1_g=v_norm1_g, w_in=v_w_in, q_norm_g=v_q_norm_g,
             k_norm_g=v_k_norm_g, ret_log_decay=v_ret_log_decay, ret_norm_g=v_ret_norm_g, gla_gate_up=v_gla_gate_up,
             gla_gate_b=v_gla_gate_b, gla_norm_g=v_gla_norm_g, w_out=v_w_out, norm2_g=v_norm2_g, w_up=v_w_up,
             conv_w=v_conv_w, conv_b=v_conv_b, w_down=v_w_down, final_norm_g=v_final_norm_g)
    return _step(_cfg(), x, c, ctx, loss_target, w, m, v)
```

```python
import functools
import math
import types

import jax
import jax.numpy as jnp
import numpy as np
from jax import lax
from jax.experimental import pallas as pl
from jax.experimental.pallas import tpu as pltpu
from jax.experimental.pallas import tpu_sc as plsc

F32 = jnp.float32
BF16 = jnp.bfloat16

D_MODEL = 2048
SEQ = 2048
CTX_LEN = 256
GRID_W = 64
D_FF = 5632
DEPTH = 2
N_DEV = 8
HEAD_DIM = 128
ROPE_THETA = 10000.0
GLA_TAU = 16.0
GLA_RANK = 16
GLA_CHUNK = 64
GLA_SUB = 16
EPS = 1e-6
N_MOD = 6
ADAM_LR = 0.001
ADAM_B1 = 0.9
ADAM_B2 = 0.999
ADAM_EPS = 1e-08
ADAM_WD = 0.01
ADAM_STEP = 10
LANE = 128
VMEM_LIMIT = 56 * 1024 * 1024
NEG = -1e30


def _cfg():
    d = types.SimpleNamespace()
    d.D, d.L, d.LC, d.F = D_MODEL, SEQ, CTX_LEN, D_FF
    d.T = d.L + d.LC
    nm = d.D // HEAD_DIM
    d.HQ, d.HKV, d.HR, d.HG = nm // 2, nm // 8, nm // 4, nm // 4
    d.G = d.HQ // d.HKV
    w = dict(aq=d.HQ * 128, ak=d.HKV * 128, av=d.HKV * 128, rq=d.HR * 128, rk=d.HR * 128, rv=d.HR * 128,
             rg=d.HR * 128, gq=d.HG * 64, gk=d.HG * 64, gv=d.HG * 128, gr=d.HG * 128, ga=2 * GLA_RANK)
    off, o = {}, 0
    for k, v in w.items():
        off[k] = o
        o += v
    d.W, d.OFF, d.NIN = w, off, o
    d.NZ = -(-(off['ga'] + LANE) // 256) * 256
    d.NINS = d.NIN // N_DEV
    d.TM = math.gcd(d.LC, 128)
    d.TQ = math.gcd(d.LC, 256)
    return d


def _cp(sem=None):
    return pltpu.CompilerParams(dimension_semantics=sem, vmem_limit_bytes=VMEM_LIMIT)


def _tile(n, target, mult=LANE):
    t = min(n, target)
    t -= t % mult
    while t > mult and n % t:
        t -= mult
    return t if t > 0 and n % t == 0 else n


_DN = {'nn': ((1,), (0,)), 'nt': ((1,), (1,)), 'tn': ((0,), (0,))}


def _mm(name, a, b, kind, M, N, K, out_dtype, tm=768, tn=768, tk=1024, a_spec=None, b_spec=None,
        out_shape=None, out_spec=None):
    tm, tn = _tile(M, tm, 128), _tile(N, tn, 128)
    tk = _tile(K, tk, 128)
    nk = K // tk

    def dot(a_ref, b_ref):
        return lax.dot_general(a_ref[...].astype(BF16), b_ref[...].astype(BF16), (_DN[kind], ((), ())), preferred_element_type=F32)

    def body_one(a_ref, b_ref, o_ref):
        o_ref[...] = dot(a_ref, b_ref).astype(o_ref.dtype)

    def body(a_ref, b_ref, o_ref, acc):
        kk = pl.program_id(2)

        @pl.when(kk == 0)
        def _():
            acc[...] = jnp.zeros_like(acc)

        acc[...] += dot(a_ref, b_ref)

        @pl.when(kk == nk - 1)
        def _():
            o_ref[...] = acc[...].astype(o_ref.dtype)

    if a_spec is None:
        a_spec = pl.BlockSpec((tk, tm), lambda i, j, k: (k, i)) if kind == 'tn' else pl.BlockSpec((tm, tk), lambda i, j, k: (i, k))
    if b_spec is None:
        b_spec = pl.BlockSpec((tn, tk), lambda i, j, k: (j, k)) if kind == 'nt' else pl.BlockSpec((tk, tn), lambda i, j, k: (k, j))
    if out_spec is None:
        out_spec = pl.BlockSpec((tm, tn), lambda i, j, k: (i, j))
        out_shape = (M, N)
    return pl.pallas_call(
        body_one if nk == 1 else body, name=name, grid=(M // tm, N // tn, nk), in_specs=[a_spec, b_spec], out_specs=out_spec,
        out_shape=jax.ShapeDtypeStruct(out_shape, out_dtype), scratch_shapes=[] if nk == 1 else [pltpu.VMEM((tm, tn), F32)],
        compiler_params=_cp(("parallel", "parallel", "arbitrary")))(a, b)


def _tile_spec(tm, w, colf, nrep):
    if hasattr(colf, 'base'):
        assert colf.base % nrep == 0
        return pl.BlockSpec((tm, w * nrep), functools.partial(lambda i, b: (i, b), b=colf.base // nrep))
    return pl.BlockSpec((tm, w), lambda i: (i, 0))


def _head_cols(colf, w, r):
    return slice(r * w, (r + 1) * w) if hasattr(colf, 'base') else slice(None)


def _row_specs(cf, tm, tiles, params, nrep):
    nctx = cf.LC // tm
    specs = [_tile_spec(tm, w, colf, nrep) for arr, w, colf, _ in tiles]
    for arr, kind, _ in params:
        nd = arr.ndim
        if kind == 'shared':
            specs.append(pl.BlockSpec(arr.shape, functools.partial(lambda i, nd: (0,) * nd, nd=nd)))
        else:
            specs.append(pl.BlockSpec((None,) + arr.shape[1:],
                                      functools.partial(lambda i, nd, nctx: (jnp.where(i >= nctx, 1, 0),) + (0,) * (nd - 1), nd=nd, nctx=nctx)))
    return specs


def _row_fwd(cf, name, f, tiles, params, outs, tm, nrep=1):
    nt, npar = len(tiles), len(params)

    def body(*refs):
        pv = [r[...] for r in refs[nt:nt + npar]]
        for r in range(nrep):
            tv = [x[:, _head_cols(t[2], t[1], r)].astype(F32) for x, t in zip(refs[:nt], tiles)]
            res = f(*tv, *pv)
            for o, v, spec in zip(refs[nt + npar:], res, outs):
                o[:, _head_cols(spec[1], spec[0], r)] = v.astype(o.dtype)

    out_specs = [_tile_spec(tm, w, colf, nrep) for w, colf, _, _ in outs]
    out_shape = [jax.ShapeDtypeStruct((cf.T, tw), dt) for _, _, tw, dt in outs]
    return pl.pallas_call(
        body, name=name, grid=(cf.T // tm,), in_specs=_row_specs(cf, tm, tiles, params, nrep), out_specs=out_specs,
        out_shape=out_shape, compiler_params=_cp(("arbitrary",)))(*[t[0] for t in tiles], *[p[0] for p in params])


def _row_bwd(cf, name, f, tiles, params, cts, tgrads, tm, nrep=1):
    nt, npar, nc = len(tiles), len(params), len(cts)
    tdiff = [k for k, t in enumerate(tiles) if t[3]]
    pdiff = [k for k, p in enumerate(params) if p[2]]
    nctx = cf.LC // tm

    def body(*refs):
        i = pl.program_id(0)
        pv = [x[...] for x in refs[nt:nt + npar]]
        outs = refs[nt + npar + nc:]
        psum = None
        for r in range(nrep):
            tv = [x[:, _head_cols(t[2], t[1], r)].astype(F32) for x, t in zip(refs[:nt], tiles)]
            cv = tuple(x[:, _head_cols(c[2], c[1], r)].astype(F32) for x, c in zip(refs[nt + npar:nt + npar + nc], cts))

            def g(*diff, tv=tv):
                tv2, pv2 = list(tv), list(pv)
                for k, v in zip(tdiff, diff[:len(tdiff)]):
                    tv2[k] = v
                for k, v in zip(pdiff, diff[len(tdiff):]):
                    pv2[k] = v
                return tuple(f(*tv2, *pv2))

            _, vjp_fn = jax.vjp(g, *[tv[k] for k in tdiff], *[pv[k] for k in pdiff])
            grads = vjp_fn(cv)
            for o, gv, spec in zip(outs[:len(tdiff)], grads[:len(tdiff)], tgrads):
                o[:, _head_cols(spec[1], spec[0], r)] = gv.astype(o.dtype)
            pg = grads[len(tdiff):]
            psum = list(pg) if psum is None else [a + b for a, b in zip(psum, pg)]
        for n_, (o, gv) in enumerate(zip(outs[len(tdiff):], psum)):
            first = (i == 0) if params[pdiff[n_]][1] == 'shared' else jnp.logical_or(i == 0, i == nctx)

            @pl.when(first)
            def _():
                o[...] = gv

            @pl.when(jnp.logical_not(first))
            def _():
                o[...] += gv

    in_specs = _row_specs(cf, tm, tiles, params, nrep)
    in_specs += [_tile_spec(tm, w, colf, nrep) for _, w, colf in cts]
    out_specs = [_tile_spec(tm, w, colf, nrep) for w, colf, _, _ in tgrads]
    out_shape = [jax.ShapeDtypeStruct((cf.T, tw), dt) for _, _, tw, dt in tgrads]
    out_specs += _row_specs(cf, tm, [], [params[k] for k in pdiff], nrep)
    out_shape += [jax.ShapeDtypeStruct(params[k][0].shape, F32) for k in pdiff]
    res = pl.pallas_call(
        body, name=name, grid=(cf.T // tm,), in_specs=in_specs, out_specs=out_specs, out_shape=out_shape,
        compiler_params=_cp(("arbitrary",)))(*[t[0] for t in tiles], *[p[0] for p in params], *[c[0] for c in cts])
    return res[:len(tdiff)], res[len(tdiff):]


def _c0(r):
    return 0


def _col(base):
    def col(r):
        return base + r
    col.base = base
    return col


def _rms(x, g):
    return x * lax.rsqrt(jnp.mean(x * x, axis=-1, keepdims=True) + EPS) * g


def _sigmoid(x):
    return 1.0 / (1.0 + jnp.exp(-x))


def _silu(x):
    return x * _sigmoid(x)


def _f_norm_mod(x, g, sh, sc):
    return (_rms(x, g) * (1 + sc) + sh,)


def _f_resid_norm_mod(x, y, gate, g, sh, sc):
    x1 = x + gate * y
    return (x1, _rms(x1, g) * (1 + sc) + sh)


@jax.custom_vjp
def _swap_halves(t):
    return pltpu.roll(t, HEAD_DIM // 2, axis=1)


def _swap_fwd(t):
    return _swap_halves(t), None


def _swap_bwd(_, g):
    return (pltpu.roll(g, HEAD_DIM // 2, axis=1),)


_swap_halves.defvjp(_swap_fwd, _swap_bwd)


def _rope(t, cs, sn):
    return t * cs + _swap_halves(t) * sn


def _f_prep_norm(t, cs, sn, g):
    return (_rope(_rms(t, g), cs, sn),)


def _f_prep_plain(t, cs, sn):
    return (_rope(t, cs, sn),)


def _f_prep_scaled(t, cs, sn):
    return (_rope(t * (HEAD_DIM ** -0.5), cs, sn),)


def _log_sigmoid(x):
    return jnp.minimum(x, 0.0) - jnp.log(1.0 + jnp.exp(-jnp.abs(x)))


N_DECAY = 8


def _gla_masks(d, width):
    C, SB = GLA_CHUNK, GLA_SUB
    r = lax.broadcasted_iota(jnp.int32, (C, C), 0)
    m = lax.broadcasted_iota(jnp.int32, (C, C), 1)
    rr = lax.broadcasted_iota(jnp.int32, (C, width), 0)
    allowed = (m <= r) if d == 0 else (m >= r)
    blocks, vis = [allowed], []
    for b in range(C // SB):
        blocks.append((m < SB * b) if d == 0 else (m >= SB * (b + 1)))
        vis.append((rr < SB * (b + 1)) if d == 0 else (rr >= SB * b))
    cm = jnp.concatenate([x.astype(F32) for x in blocks] + [jnp.ones((C, C), F32)], axis=0)
    return cm, allowed, vis


def _gla_decays(la, d):
    C, SB = GLA_CHUNK, GLA_SUB
    nsb = C // SB
    cm, _, vis = _gla_masks(d, la.shape[-1])
    cums = jnp.dot(cm, la, precision=lax.Precision.HIGH, preferred_element_type=F32)
    cum, tot = cums[0:C], cums[(1 + nsb) * C:]
    refs = [cums[(1 + b) * C:(2 + b) * C] for b in range(nsb)]
    e1 = jnp.concatenate([jnp.exp(cum[b * SB:(b + 1) * SB] - refs[b][b * SB:(b + 1) * SB]) for b in range(nsb)], axis=0)
    e2 = [jnp.where(vis[b], jnp.exp(jnp.where(vis[b], refs[b] - cum, 0.0)), 0.0) for b in range(nsb)]
    return [e1] + e2 + [jnp.exp(cum), jnp.exp(tot - cum), jnp.exp(tot)]


def _f_gla_pre(ga, *per_pair):
    gab = ga.astype(BF16)
    outs = [[], []]
    for p in range(len(per_pair) // 4):
        gf, gb, bf, bb = per_pair[4 * p:4 * p + 4]
        for d, (gm, bm) in enumerate(((gf, bf), (gb, bb))):
            la = _log_sigmoid(jnp.dot(gab, gm.astype(BF16), preferred_element_type=F32) + bm) / GLA_TAU
            outs[d] += _gla_decays(la, d)
    return tuple(jnp.concatenate(o, axis=-1) for o in outs)


def _f_gated_norm(o, g, n):
    return (_rms(o, n) * _silu(g),)


def _f_first(x, g, sh, sc):
    return (x, _rms(x, g) * (1 + sc) + sh)


def _loss_grad(cf, x1, yff, gate, gfin, tgt):
    tm, T, D = cf.TM, cf.T, cf.D
    nctx = cf.LC // tm

    def lossf(x1v, yv, gt, gf, tg):
        y = _rms(x1v + gt * yv, gf)
        e = y - tg
        return 0.5 * jnp.sum(jnp.mean(e * e, axis=-1, keepdims=True), axis=0, keepdims=True)

    def body(x1_ref, y_ref, gt_ref, gf_ref, tg_ref, dx_ref, dy_ref, dgt_ref, dgf_ref, ls_ref):
        i = pl.program_id(0)
        lat = (i >= nctx).astype(F32)
        val, vjp_fn = jax.vjp(lossf, x1_ref[...], y_ref[...].astype(F32), gt_ref[...], gf_ref[...], tg_ref[...])
        dx, dy, dgt, dgf, _ = vjp_fn(jnp.ones((1, 1), F32) * lat)
        dx_ref[...] = dx
        dy_ref[...] = dy.astype(dy_ref.dtype)
        first_s = jnp.logical_or(i == 0, i == nctx)

        @pl.when(first_s)
        def _():
            dgt_ref[...] = dgt

        @pl.when(jnp.logical_not(first_s))
        def _():
            dgt_ref[...] += dgt

        @pl.when(i == 0)
        def _():
            dgf_ref[...] = dgf
            ls_ref[...] = jnp.zeros_like(ls_ref) + val * lat

        @pl.when(i != 0)
        def _():
            dgf_ref[...] += dgf
            ls_ref[...] += val * lat

    row = pl.BlockSpec((tm, D), lambda i: (i, 0))
    strm = pl.BlockSpec((None, 1, D), lambda i: (jnp.where(i >= nctx, 1, 0), 0, 0))
    one = pl.BlockSpec((1, D), lambda i: (0, 0))
    return pl.pallas_call(
        body, name="loss_grad", grid=(T // tm,), in_specs=[row, row, strm, one, row],
        out_specs=[row, row, strm, one, pl.BlockSpec((8, LANE), lambda i: (0, 0))],
        out_shape=[jax.ShapeDtypeStruct((T, D), F32), jax.ShapeDtypeStruct((T, D), BF16),
                   jax.ShapeDtypeStruct((2, 1, D), F32), jax.ShapeDtypeStruct((1, D), F32),
                   jax.ShapeDtypeStruct((8, LANE), F32)],
        compiler_params=_cp(("arbitrary",)))(x1, yff, gate, gfin, tgt)


def _att_mask(cf, is_latent, rows):
    col = lax.broadcasted_iota(jnp.int32, (rows, cf.T), 1)
    return jnp.logical_or(col < cf.LC, is_latent)


def _stack_heads(ref, G):
    return jnp.concatenate([ref[:, j * LANE:(j + 1) * LANE] for j in range(G)], axis=0)


def _att_probs(q, k, mask):
    s = lax.dot_general(q, k, (_DN['nt'], ((), ())), preferred_element_type=F32) * (HEAD_DIM ** -0.5)
    s = jnp.where(mask, s, NEG)
    e = jnp.exp(s - jnp.max(s, axis=-1, keepdims=True))
    return e / jnp.sum(e, axis=-1, keepdims=True)


def _att_fwd(cf, q, k, z):
    tq, T, G = cf.TQ, cf.T, cf.G
    vb = cf.OFF['av'] // LANE

    def body(q_ref, k_ref, v_ref, o_ref):
        mask = _att_mask(cf, pl.program_id(1) >= cf.LC // tq, tq)
        kv, vv = k_ref[...], v_ref[...].astype(BF16)
        for j in range(G):
            p = _att_probs(q_ref[:, j * LANE:(j + 1) * LANE], kv, mask)
            o_ref[:, j * LANE:(j + 1) * LANE] = jnp.dot(p.astype(BF16), vv, preferred_element_type=F32).astype(o_ref.dtype)

    return pl.pallas_call(
        body, name="att_fwd", grid=(cf.HKV, T // tq),
        in_specs=[pl.BlockSpec((tq, G * LANE), lambda g, i: (i, g)), pl.BlockSpec((T, LANE), lambda g, i: (0, g)),
                  pl.BlockSpec((T, LANE), lambda g, i: (0, vb + g))],
        out_specs=pl.BlockSpec((tq, G * LANE), lambda g, i: (i, g)),
        out_shape=jax.ShapeDtypeStruct((T, cf.HQ * LANE), BF16), compiler_params=_cp(("arbitrary", "arbitrary")))(q, k, z)


def _att_bwd(cf, q, k, z, dcat):
    tq, T, G = cf.TM, cf.T, cf.G
    vb = cf.OFF['av'] // LANE
    sc = HEAD_DIM ** -0.5

    def body(q_ref, k_ref, v_ref, do_ref, dq_ref, dk_ref, dv_ref):
        i = pl.program_id(1)
        mask = _att_mask(cf, i >= cf.LC // tq, G * tq)
        kv, vv = k_ref[...], v_ref[...].astype(BF16)
        q4, do4 = _stack_heads(q_ref, G), _stack_heads(do_ref, G)
        p = _att_probs(q4, kv, mask)
        dv = lax.dot_general(p.astype(BF16), do4, (_DN['tn'], ((), ())), preferred_element_type=F32)
        dp = lax.dot_general(do4, vv, (_DN['nt'], ((), ())), preferred_element_type=F32)
        dsb = (p * (dp - jnp.sum(dp * p, axis=-1, keepdims=True)) * sc).astype(BF16)
        dq = jnp.dot(dsb, kv, preferred_element_type=F32)
        dk = lax.dot_general(dsb, q4, (_DN['tn'], ((), ())), preferred_element_type=F32)
        for j in range(G):
            dq_ref[:, j * LANE:(j + 1) * LANE] = dq[j * tq:(j + 1) * tq]

        @pl.when(i == 0)
        def _():
            dk_ref[...] = dk
            dv_ref[...] = dv

        @pl.when(i != 0)
        def _():
            dk_ref[...] += dk
            dv_ref[...] += dv

    qs = pl.BlockSpec((tq, G * LANE), lambda g, i: (i, g))
    ks = pl.BlockSpec((T, LANE), lambda g, i: (0, g))
    return pl.pallas_call(
        body, name="att_bwd", grid=(cf.HKV, T // tq),
        in_specs=[qs, ks, pl.BlockSpec((T, LANE), lambda g, i: (0, vb + g)), qs],
        out_specs=[qs, ks, ks],
        out_shape=[jax.ShapeDtypeStruct((T, cf.HQ * LANE), F32), jax.ShapeDtypeStruct((T, cf.HKV * LANE), F32),
                   jax.ShapeDtypeStruct((T, cf.HKV * LANE), F32)],
        compiler_params=_cp(("arbitrary", "arbitrary")))(q, k, z, dcat)


def _ret_masks(cf, i, tq, lgf, lgb):
    T, LC = cf.T, cf.LC
    row = (lax.broadcasted_iota(jnp.int32, (tq, 1), 0) + i * tq)
    col = lax.broadcasted_iota(jnp.int32, (1, T), 1)

    def pb(n):
        return jnp.where(n < LC, LC - 1 - n, T + LC - 1 - n).astype(F32)

    relf = row.astype(F32) - col.astype(F32)
    relb = pb(row) - pb(col)
    rf, rb = jnp.maximum(relf, 0.0), jnp.maximum(relb, 0.0)
    mf = jnp.where(relf >= 0, jnp.exp(lgf * rf), 0.0)
    mb = jnp.where(relb >= 0, jnp.exp(lgb * rb), 0.0)
    return mf, mb, rf, rb


def _ret_fwd(cf, q, k, z, lg):
    tq, T = cf.TQ, cf.T
    vb = cf.OFF['rv'] // LANE

    def body(lg_ref, q_ref, k_ref, v_ref, o_ref):
        h, i = pl.program_id(0), pl.program_id(1)
        mf, mb, _, _ = _ret_masks(cf, i, tq, lg_ref[0, h], lg_ref[1, h])
        a = lax.dot_general(q_ref[...], k_ref[...], (_DN['nt'], ((), ())), preferred_element_type=F32)
        p = (a * (mf + mb)).astype(BF16)
        o_ref[...] = jnp.dot(p, v_ref[...].astype(BF16), preferred_element_type=F32)

    return pl.pallas_call(
        body, name="ret_fwd", grid=(cf.HR, T // tq),
        in_specs=[pl.BlockSpec(memory_space=pltpu.SMEM), pl.BlockSpec((tq, LANE), lambda h, i: (i, h)),
                  pl.BlockSpec((T, LANE), lambda h, i: (0, h)), pl.BlockSpec((T, LANE), lambda h, i: (0, vb + h))],
        out_specs=pl.BlockSpec((tq, LANE), lambda h, i: (i, h)),
        out_shape=jax.ShapeDtypeStruct((T, cf.HR * LANE), F32), compiler_params=_cp(("arbitrary", "arbitrary")))(lg, q, k, z)


def _ret_bwd(cf, q, k, z, lg, do):
    tq, T = cf.TQ, cf.T
    vb = cf.OFF['rv'] // LANE

    def body(lg_ref, q_ref, k_ref, v_ref, do_ref, dq_ref, dk_ref, dv_ref, dlg_ref):
        h, i = pl.program_id(0), pl.program_id(1)
        mf, mb, rf, rb = _ret_masks(cf, i, tq, lg_ref[0, h], lg_ref[1, h])
        qv, kv, vv = q_ref[...], k_ref[...], v_ref[...].astype(BF16)
        dob = do_ref[...].astype(BF16)
        a = lax.dot_general(qv, kv, (_DN['nt'], ((), ())), preferred_element_type=F32)
        m = mf + mb
        p = (a * m).astype(BF16)
        dv = lax.dot_general(p, dob, (_DN['tn'], ((), ())), preferred_element_type=F32)
        dp = lax.dot_general(dob, vv, (_DN['nt'], ((), ())), preferred_element_type=F32)
        da = (dp * m).astype(BF16)
        dq_ref[...] = jnp.dot(da, kv, preferred_element_type=F32)
        dk = lax.dot_general(da, qv, (_DN['tn'], ((), ())), preferred_element_type=F32)
        dm = dp * a
        dlf = jnp.sum(jnp.sum(dm * mf * rf, axis=-1, keepdims=True), axis=0, keepdims=True)
        dlb = jnp.sum(jnp.sum(dm * mb * rb, axis=-1, keepdims=True), axis=0, keepdims=True)
        rid = lax.broadcasted_iota(jnp.int32, (8, LANE), 0)
        dl = jnp.where(rid == 0, dlf, jnp.where(rid == 1, dlb, 0.0))

        @pl.when(i == 0)
        def _():
            dk_ref[...] = dk
            dv_ref[...] = dv
            dlg_ref[...] = dl

        @pl.when(i != 0)
        def _():
            dk_ref[...] += dk
            dv_ref[...] += dv
            dlg_ref[...] += dl

    qs = pl.BlockSpec((tq, LANE), lambda h, i: (i, h))
    ks = pl.BlockSpec((T, LANE), lambda h, i: (0, h))
    return pl.pallas_call(
        body, name="ret_bwd", grid=(cf.HR, T // tq),
        in_specs=[pl.BlockSpec(memory_space=pltpu.SMEM), qs, ks, pl.BlockSpec((T, LANE), lambda h, i: (0, vb + h)), qs],
        out_specs=[qs, ks, ks, pl.BlockSpec((None, 8, LANE), lambda h, i: (h, 0, 0))],
        out_shape=[jax.ShapeDtypeStruct((T, cf.HR * LANE), F32)] * 3 + [jax.ShapeDtypeStruct((cf.HR, 8, LANE), F32)],
        compiler_params=_cp(("arbitrary", "arbitrary")))(lg, q, k, z, do)


def _gla_step(q, k, v, es, st, lmask, allowed):
    C, SB = GLA_CHUNK, GLA_SUB
    nsb = C // SB
    e1, e2, e3, e4, e5 = es[0], es[1:1 + nsb], es[1 + nsb], es[2 + nsb], es[3 + nsb]
    qs = q * lmask * ((HEAD_DIM // 2) ** -0.5)
    ks = k * lmask
    qt = qs * e1
    rows = [lax.dot_general(qt[b * SB:(b + 1) * SB], ks * e2[b], (_DN['nt'], ((), ())), precision=lax.Precision.HIGH,
                            preferred_element_type=F32) for b in range(nsb)]
    att = jnp.where(allowed, jnp.concatenate(rows, axis=0), 0.0)
    o = jnp.dot(att.astype(BF16), v.astype(BF16), preferred_element_type=F32)
    o += lax.dot_general((qs * e3).astype(BF16), st.astype(BF16), (_DN['nt'], ((), ())), preferred_element_type=F32)
    kd = (ks * e4).astype(BF16)
    st_new = st * jnp.concatenate([e5, e5], axis=0) + lax.dot_general(v.astype(BF16), kd, (_DN['tn'], ((), ())), preferred_element_type=F32)
    return o, st_new


def _gla_allowed(d):
    r = lax.broadcasted_iota(jnp.int32, (GLA_CHUNK, GLA_CHUNK), 0)
    m = lax.broadcasted_iota(jnp.int32, (GLA_CHUNK, GLA_CHUNK), 1)
    return (m <= r) if d == 0 else (m >= r)


def _gla_chunk_id(cf, s, d):
    if d == 0:
        return s
    nct, nc = cf.LC // GLA_CHUNK, cf.T // GLA_CHUNK
    return jnp.where(s < nct, nct - 1 - s, nc + nct - 1 - s)


def _gla_lmask(h):
    return (lax.broadcasted_iota(jnp.int32, (1, LANE), 1) // (LANE // 2) == h).astype(F32)


_GLA_CHAINS = [(h, d) for h in range(2) for d in range(2)]


def _gla_row_specs(cf, nc, reverse):
    def rowblk(s, d):
        return _gla_chunk_id(cf, nc - 1 - s if reverse else s, d)

    def spec(width, base, d, per_pair=1):
        return pl.BlockSpec((GLA_CHUNK, width), functools.partial(lambda p, s, base, d: (rowblk(s, d), base + per_pair * p), base=base, d=d))

    def state(d):
        return pl.BlockSpec((2, None, LANE, LANE), functools.partial(lambda p, s, d: (p, rowblk(s, d), 0, 0), d=d))

    return spec, state


def _gla_fwd(cf, z, ef, eb):
    T, C = cf.T, GLA_CHUNK
    nc = T // C
    qb, kb, vb = cf.OFF['gq'] // LANE, cf.OFF['gk'] // LANE, cf.OFF['gv'] // (2 * LANE)
    spec, state = _gla_row_specs(cf, nc, False)

    def body(qf, kf, vf, e_f, qb_, kb_, vb_, e_b, of, sf, ob, sb, st_scr):
        @pl.when(pl.program_id(1) == 0)
        def _():
            st_scr[...] = jnp.zeros_like(st_scr)

        io = [(qf, kf, vf, e_f, of, sf), (qb_, kb_, vb_, e_b, ob, sb)]
        for ci, (h, d) in enumerate(_GLA_CHAINS):
            q, k, v, e, o_ref, s_ref = io[d]
            cols = slice(h * LANE, (h + 1) * LANE)
            es = [e[:, n * LANE:(n + 1) * LANE] for n in range(N_DECAY)]
            st = st_scr[ci]
            s_ref[h] = st
            o, stn = _gla_step(q[...], k[...], v[:, cols], es, st, _gla_lmask(h), _gla_allowed(d))
            st_scr[ci] = stn
            o_ref[:, cols] = o

    ins, outs = [], []
    for d in range(2):
        ins += [spec(LANE, qb, d), spec(LANE, kb, d), spec(2 * LANE, vb, d), spec(N_DECAY * LANE, 0, d)]
        outs += [spec(2 * LANE, 0, d), state(d)]
    oshape = [jax.ShapeDtypeStruct((T, cf.HG * LANE), F32), jax.ShapeDtypeStruct((cf.HG, nc, LANE, LANE), F32)]
    return pl.pallas_call(
        body, name="gla_fwd", grid=(cf.HG // 2, nc), in_specs=ins, out_specs=outs, out_shape=oshape * 2,
        scratch_shapes=[pltpu.VMEM((4, LANE, LANE), F32)],
        compiler_params=_cp(("arbitrary", "arbitrary")))(z, z, z, ef, z, z, z, eb)


def _gla_bwd(cf, z, ef, eb, sf, sb, do):
    T, C = cf.T, GLA_CHUNK
    nc = T // C
    qb, kb, vb = cf.OFF['gq'] // LANE, cf.OFF['gk'] // LANE, cf.OFF['gv'] // (2 * LANE)
    spec, state = _gla_row_specs(cf, nc, True)

    def body(*refs):
        ins = [refs[0:6], refs[6:12]]
        outs = [refs[12:16], refs[16:20]]
        dst_scr = refs[20]

        @pl.when(pl.program_id(1) == 0)
        def _():
            dst_scr[...] = jnp.zeros_like(dst_scr)

        acc = [None, None]
        for ci, (h, d) in enumerate(_GLA_CHAINS):
            q, k, v, e, s_ref, do_ref = ins[d]
            cols = slice(h * LANE, (h + 1) * LANE)
            es = [e[:, n * LANE:(n + 1) * LANE] for n in range(N_DECAY)]
            step = functools.partial(_gla_step, lmask=_gla_lmask(h), allowed=_gla_allowed(d))
            _, vjp_fn = jax.vjp(step, q[...], k[...], v[:, cols], es, s_ref[h])
            dq, dk, dv, des, dst = vjp_fn((do_ref[:, cols], dst_scr[ci]))
            dst_scr[ci] = dst
            outs[d][2][:, cols] = dv
            part = [dq, dk] + list(des)
            acc[d] = part if acc[d] is None else [a + b for a, b in zip(acc[d], part)]
        for d in range(2):
            dq_ref, dk_ref, _, de_ref = outs[d]
            dq_ref[...] = acc[d][0]
            dk_ref[...] = acc[d][1]
            for n in range(N_DECAY):
                de_ref[:, n * LANE:(n + 1) * LANE] = acc[d][2 + n]

    in_specs, out_specs = [], []
    for d in range(2):
        in_specs += [spec(LANE, qb, d), spec(LANE, kb, d), spec(2 * LANE, vb, d), spec(N_DECAY * LANE, 0, d), state(d), spec(2 * LANE, 0, d)]
        out_specs += [spec(LANE, 0, d), spec(LANE, 0, d), spec(2 * LANE, 0, d), spec(N_DECAY * LANE, 0, d)]
    npair = cf.HG // 2
    oshape = [jax.ShapeDtypeStruct((T, npair * LANE), F32), jax.ShapeDtypeStruct((T, npair * LANE), F32),
              jax.ShapeDtypeStruct((T, cf.HG * LANE), F32), jax.ShapeDtypeStruct((T, npair * N_DECAY * LANE), F32)]
    return pl.pallas_call(
        body, name="gla_bwd", grid=(npair, nc), in_specs=in_specs, out_specs=out_specs, out_shape=oshape * 2,
        scratch_shapes=[pltpu.VMEM((4, LANE, LANE), F32)],
        compiler_params=_cp(("arbitrary", "arbitrary")))(z, z, z, ef, sf, do, z, z, z, eb, sb, do)


def _conv_parts(cf, a, w_ref):
    T, LC = cf.T, cf.LC
    rid = lax.broadcasted_iota(jnp.int32, a.shape, 0)
    first = jnp.logical_or(rid == 0, rid == LC)
    last = jnp.logical_or(rid == LC - 1, rid == T - 1)
    ap = jnp.where(first, 0.0, pltpu.roll(a, 1, axis=0))
    an = jnp.where(last, 0.0, pltpu.roll(a, T - 1, axis=0))
    w0, w1, w2, b = w_ref[0:1, :], w_ref[1:2, :], w_ref[2:3, :], w_ref[3:4, :]
    ac = ap * w0 + a * w1 + an * w2 + b
    return ap, an, ac, first, last, (w0, w1, w2)


def _conv_fwd(cf, u, wb):
    T, Fd = cf.T, cf.F
    tc = _tile(Fd, 512)
    nj = Fd // tc

    def body(a_ref, v_ref, w_ref, t_ref):
        _, _, ac, _, _, _ = _conv_parts(cf, a_ref[...].astype(F32), w_ref)
        t_ref[...] = (_silu(ac) * v_ref[...].astype(F32)).astype(t_ref.dtype)

    return pl.pallas_call(
        body, name="conv_fwd", grid=(nj,),
        in_specs=[pl.BlockSpec((T, tc), lambda j: (0, j)), pl.BlockSpec((T, tc), lambda j: (0, nj + j)),
                  pl.BlockSpec((8, tc), lambda j: (0, j))],
        out_specs=pl.BlockSpec((T, tc), lambda j: (0, j)), out_shape=jax.ShapeDtypeStruct((T, Fd), BF16),
        compiler_params=_cp(("parallel",)))(u, u, wb)


def _conv_bwd(cf, u, wb, dt):
    T, Fd = cf.T, cf.F
    tc = _tile(Fd, 256)
    nj = Fd // tc

    def body(a_ref, v_ref, w_ref, dt_ref, du_ref, dw_ref):
        a, v, dtv = a_ref[...].astype(F32), v_ref[...].astype(F32), dt_ref[...].astype(F32)
        ap, an, ac, first, last, (w0, w1, w2) = _conv_parts(cf, a, w_ref)
        sg = _sigmoid(ac)
        du_ref[1] = (dtv * ac * sg).astype(du_ref.dtype)
        dac = dtv * v * (sg * (1.0 + ac * (1.0 - sg)))
        from_next = pltpu.roll(jnp.where(first, 0.0, dac), T - 1, axis=0)
        from_prev = pltpu.roll(jnp.where(last, 0.0, dac), 1, axis=0)
        du_ref[0] = (dac * w1 + from_next * w0 + from_prev * w2).astype(du_ref.dtype)
        rows = [jnp.sum(dac * ap, axis=0, keepdims=True), jnp.sum(dac * a, axis=0, keepdims=True),
                jnp.sum(dac * an, axis=0, keepdims=True), jnp.sum(dac, axis=0, keepdims=True)]
        rid = lax.broadcasted_iota(jnp.int32, (8, tc), 0)
        dw = jnp.zeros((8, tc), F32)
        for n_, rw in enumerate(rows):
            dw = jnp.where(rid == n_, rw, dw)
        dw_ref[...] = dw

    col = pl.BlockSpec((T, tc), lambda j: (0, j))
    return pl.pallas_call(
        body, name="conv_bwd", grid=(nj,),
        in_specs=[col, pl.BlockSpec((T, tc), lambda j: (0, nj + j)), pl.BlockSpec((8, tc), lambda j: (0, j)), col],
        out_specs=[pl.BlockSpec((2, T, tc), lambda j: (0, 0, j)), pl.BlockSpec((8, tc), lambda j: (0, j))],
        out_shape=[jax.ShapeDtypeStruct((2, T, Fd), BF16), jax.ShapeDtypeStruct((8, Fd), F32)],
        compiler_params=_cp(("parallel",)))(u, u, wb, dt)


def _me():
    x, y, c = lax.axis_index("x"), lax.axis_index("y"), lax.axis_index("c")
    return x, y, c, 4 * x + 2 * y + c


def _peer(x, y, c, k):
    px = 1 - x if (k >> 2) & 1 else x
    py = 1 - y if (k >> 1) & 1 else y
    pc = 1 - c if k & 1 else c
    return (px, py, pc), 4 * px + 2 * py + pc


def _rcopy(src, dst, ss, rs, tgt):
    return pltpu.make_async_remote_copy(src_ref=src, dst_ref=dst, send_sem=ss, recv_sem=rs, device_id=tgt,
                                        device_id_type=pl.DeviceIdType.MESH)


def _ag_small(name, v):
    R, Cc = v.shape

    def body(v_ref, o_ref, ssem, rsem, lsem):
        x, y, c, me = _me()
        loc = pltpu.make_async_copy(v_ref, o_ref.at[me], lsem)
        loc.start()
        sends = []
        for k in range(1, N_DEV):
            tgt, _ = _peer(x, y, c, k)
            cp = _rcopy(v_ref, o_ref.at[me], ssem.at[k - 1], rsem.at[k - 1], tgt)
            cp.start()
            sends.append(cp)
        for k in range(1, N_DEV):
            tgt, pi = _peer(x, y, c, k)
            _rcopy(v_ref, o_ref.at[pi], ssem.at[k - 1], rsem.at[k - 1], tgt).wait_recv()
        for cp in sends:
            cp.wait_send()
        loc.wait()

    vm = pl.BlockSpec(memory_space=pltpu.VMEM)
    return pl.pallas_call(
        body, name=name, in_specs=[vm], out_specs=vm, out_shape=jax.ShapeDtypeStruct((N_DEV, R, Cc), v.dtype),
        scratch_shapes=[pltpu.SemaphoreType.DMA((N_DEV - 1,)), pltpu.SemaphoreType.DMA((N_DEV - 1,)), pltpu.SemaphoreType.DMA],
        compiler_params=pltpu.CompilerParams(vmem_limit_bytes=VMEM_LIMIT))(v)


_KINDS = ['in', 'out', 'up', 'down']


def _shard_shape(cf, kind):
    D, Fd = cf.D, cf.F
    return {'in': (D, cf.NINS), 'out': (D // N_DEV, D), 'up': (D, 2 * Fd // N_DEV), 'down': (Fd // N_DEV, D)}[kind]


def _whole_shape(cf, kind):
    D, Fd = cf.D, cf.F
    return {'in': (N_DEV, D, cf.NINS), 'out': (D, D), 'up': (D, 2 * Fd), 'down': (Fd, D)}[kind]


def _part(ref, cf, kind, idx):
    r, cdim = _shard_shape(cf, kind)
    if kind == 'in':
        return ref.at[idx]
    if kind == 'up':
        return ref.at[:, pl.ds(pl.multiple_of(idx * cdim, cdim), cdim)]
    return ref.at[pl.ds(pl.multiple_of(idx * r, r), r), :]


N_BARRIER_IDS = 8


def _handshake(x, y, c):
    barrier = pltpu.get_barrier_semaphore()
    for k in range(1, N_DEV):
        pl.semaphore_signal(barrier, inc=1, device_id=_peer(x, y, c, k)[0], device_id_type=pl.DeviceIdType.MESH)
    pl.semaphore_wait(barrier, N_DEV - 1)


def _seq_kernel(body, name, seq, out_type, nsem=N_DEV - 1):
    return pl.kernel(
        body, out_type=out_type, mesh=plsc.ScalarSubcoreMesh(axis_name="sq", num_cores=1), name=name,
        scratch_types=[pltpu.SemaphoreType.DMA((nsem,)), pltpu.SemaphoreType.DMA((nsem,)), pltpu.SemaphoreType.DMA],
        compiler_params=pltpu.CompilerParams(collective_id=seq % N_BARRIER_IDS))


def _seq_gather(cf, kind, l, seq, src):
    rows = _shard_shape(cf, kind)[0]

    def body(src_ref, land_ref, ssem, rsem, lsem):
        x, y, c, me = _me()
        _handshake(x, y, c)
        sib, xn, yn, dg = (x, y, 1 - c), (1 - x, y, c), (x, 1 - y, c), (1 - x, 1 - y, c)

        def blk(dev, half=None):
            part = _part(land_ref, cf, kind, 4 * dev[0] + 2 * dev[1] + dev[2])
            return part if half is None else part.at[pl.ds(half * (rows // 2), rows // 2)]

        def copy(n, src, dst, to):
            return _rcopy(src, dst, ssem.at[n], rsem.at[n], to)

        src, mine = src_ref.at[l], blk((x, y, c))
        loc = pltpu.make_async_copy(src, mine, lsem)
        loc.start()
        loc.wait()
        sends = [copy(0, src, mine, sib), copy(1, src, mine, xn), copy(2, src, mine, yn)]
        for cp in sends:
            cp.start()

        def arrived(n, got):
            copy(n, got, got, sib).wait_recv()

        def pass_on(n, part, to):
            sends.append(copy(n, part, part, to))
            sends[-1].start()

        arrived(1, blk(xn))
        pass_on(3, blk(xn, 0), yn)
        pass_on(5, blk(xn), sib)
        arrived(2, blk(yn))
        pass_on(4, blk(yn, 1), xn)
        pass_on(6, blk(yn), sib)
        arrived(3, blk(dg, 0))
        pass_on(7, blk(dg, 0), sib)
        arrived(4, blk(dg, 1))
        pass_on(8, blk(dg, 1), sib)
        other = lambda d: (d[0], d[1], 1 - c)
        for n, got in [(0, blk(other((x, y, c)))), (5, blk(other(xn))), (6, blk(other(yn))), (7, blk(other(dg), 0)), (8, blk(other(dg), 1))]:
            copy(n, got, got, sib).wait_recv()
        for cp in sends:
            cp.wait_send()

    return _seq_kernel(body, "seq_gather_%s_%d" % (kind, l), seq, jax.ShapeDtypeStruct(_whole_shape(cf, kind), BF16), nsem=9)(src)


def _seq_scatter(cf, kind, l, seq, g):
    def body(g_ref, recv_ref, ssem, rsem, lsem):
        x, y, c, me = _me()
        _handshake(x, y, c)
        loc = pltpu.make_async_copy(_rs_slab(g_ref, cf, kind, me), recv_ref.at[me], lsem)
        loc.start()
        loc.wait()
        sends = []
        for k in range(1, N_DEV):
            tgt, pi = _peer(x, y, c, k)
            sends.append(_rcopy(_rs_slab(g_ref, cf, kind, pi), recv_ref.at[me], ssem.at[k - 1], rsem.at[k - 1], tgt))
            sends[-1].start()
        for k in range(1, N_DEV):
            tgt, pi = _peer(x, y, c, k)
            _rcopy(_rs_slab(g_ref, cf, kind, pi), recv_ref.at[pi], ssem.at[k - 1], rsem.at[k - 1], tgt).wait_recv()
        for cp in sends:
            cp.wait_send()

    return _seq_kernel(body, "seq_scatter_%s_%d" % (kind, l), seq, jax.ShapeDtypeStruct((N_DEV,) + _shard_shape(cf, kind), BF16))(g)


def _rs_slab(ref, cf, kind, j):
    return ref.at[j] if kind in ('in', 'up') else _part(ref, cf, kind, j)


def _adam_vals(w, g, m, v):
    m2 = ADAM_B1 * m + (1.0 - ADAM_B1) * g
    v2 = ADAM_B2 * v + (1.0 - ADAM_B2) * (g * g)
    mh = m2 / (1.0 - ADAM_B1 ** ADAM_STEP)
    vh = v2 / (1.0 - ADAM_B2 ** ADAM_STEP)
    return -ADAM_LR * (mh / (jnp.sqrt(vh) + ADAM_EPS) + ADAM_WD * w), m2, v2


def _row_tile(R, Cc, budget_elems):
    t = max(16, min(R, (budget_elems // max(Cc, 1)) // 16 * 16))
    while t > 16 and R % t:
        t -= 16
    return t if R % t == 0 else R


def _cast_bf16(name, w, after):
    Dp, R, Cc = w.shape
    tr = _row_tile(R, Cc, 1 << 20)

    def body(w_ref, after_ref, o_ref):
        o_ref[...] = w_ref[...].astype(BF16)

    spec = pl.BlockSpec((None, tr, Cc), lambda l, i: (l, i, 0))
    return pl.pallas_call(body, name=name, grid=(Dp, R // tr), in_specs=[spec, pl.BlockSpec(memory_space=pl.ANY)], out_specs=spec,
                          out_shape=jax.ShapeDtypeStruct(w.shape, BF16), compiler_params=_cp(("parallel", "parallel")))(w, after)


def _unshard_in(cf, name, g):
    D, ns, nz = cf.D, cf.NINS, cf.NZ
    tr = _tile(D, 256, 16)

    def body(g_ref, o_ref):
        for j in range(N_DEV):
            o_ref[:, ns * j:ns * (j + 1)] = g_ref[j]
        o_ref[:, N_DEV * ns:] = jnp.zeros((tr, nz - N_DEV * ns), o_ref.dtype)

    return pl.pallas_call(body, name=name, grid=(D // tr,), in_specs=[pl.BlockSpec((N_DEV, tr, ns), lambda i: (0, i, 0))],
                          out_specs=pl.BlockSpec((tr, nz), lambda i: (i, 0)), out_shape=jax.ShapeDtypeStruct((D, nz), g.dtype),
                          compiler_params=_cp(("parallel",)))(g)


def _slabs_in(cf, name, gw):
    D, ns, nz = cf.D, cf.NINS, cf.NZ
    tr = _tile(D, 256, 16)

    def body(x_ref, o_ref):
        for j in range(N_DEV):
            o_ref[j] = x_ref[:, ns * j:ns * (j + 1)]

    return pl.pallas_call(body, name=name, grid=(D // tr,), in_specs=[pl.BlockSpec((tr, nz), lambda i: (i, 0))],
                          out_specs=pl.BlockSpec((N_DEV, tr, ns), lambda i: (0, i, 0)), out_shape=jax.ShapeDtypeStruct((N_DEV, D, ns), gw.dtype),
                          compiler_params=_cp(("parallel",)))(gw)


def _sum_adam(name, recv, w, m, v, layer, prev, after):
    Dp, R, Cc = w.shape
    tr = _row_tile(R, Cc, 1 << 18)

    def body(r_ref, w_ref, m_ref, v_ref, *rest):
        g_ref, d_ref, mo_ref, vo_ref = rest[-4:]
        g = r_ref[0].astype(F32)
        for s in range(1, N_DEV):
            g = g + r_ref[s].astype(F32)
        dl, m2, v2 = _adam_vals(w_ref[...], g, m_ref[...], v_ref[...])
        g_ref[...] = g
        d_ref[...] = dl
        mo_ref[...] = m2
        vo_ref[...] = v2

    spec = pl.BlockSpec((None, tr, Cc), lambda i: (layer, i, 0))
    anyspec = pl.BlockSpec(memory_space=pl.ANY)
    extra = [after] + (list(prev) if prev is not None else [])
    aliases = {5 + n: n for n in range(4)} if prev is not None else {}
    return pl.pallas_call(body, name=name, grid=(R // tr,), in_specs=[pl.BlockSpec((N_DEV, tr, Cc), lambda i: (0, i, 0)), spec, spec, spec] + [anyspec] * len(extra),
                          out_specs=[spec] * 4, out_shape=[jax.ShapeDtypeStruct(w.shape, F32)] * 4, input_output_aliases=aliases,
                          compiler_params=_cp(("parallel",)))(recv, w, m, v, *extra)


def _adam(name, w, g, m, v):
    R, Cc = w.shape
    tr = _row_tile(R, Cc, 1 << 18)

    def body(w_ref, g_ref, m_ref, v_ref, d_ref, mo_ref, vo_ref):
        dl, m2, v2 = _adam_vals(w_ref[...], g_ref[...], m_ref[...], v_ref[...])
        d_ref[...] = dl
        mo_ref[...] = m2
        vo_ref[...] = v2

    spec = pl.BlockSpec((tr, Cc), lambda i: (i, 0))
    return pl.pallas_call(body, name=name, grid=(R // tr,), in_specs=[spec] * 4, out_specs=[spec] * 3,
                          out_shape=[jax.ShapeDtypeStruct(w.shape, F32)] * 3, compiler_params=_cp(("parallel",)))(w, g, m, v)


def _sum8(name, a):
    n, R, Cc = a.shape

    def body(a_ref, o_ref):
        s = a_ref[0]
        for k in range(1, n):
            s = s + a_ref[k]
        o_ref[...] = s

    return pl.pallas_call(body, name=name, in_specs=[pl.BlockSpec(memory_space=pltpu.VMEM)],
                          out_specs=pl.BlockSpec(memory_space=pltpu.VMEM), out_shape=jax.ShapeDtypeStruct((R, Cc), F32),
                          compiler_params=pltpu.CompilerParams(vmem_limit_bytes=VMEM_LIMIT))(a)


def _ada_fwd(cf, c9, ada_w):
    D = cf.D
    NS = ada_w.shape[-1]
    tk = _tile(D, 512)
    nk = D // tk

    def body(c_ref, w_ref, o_ref):
        kk = pl.program_id(1)
        s = _silu(c_ref[...]).astype(BF16)
        part = jnp.dot(s, w_ref[...].astype(BF16), preferred_element_type=F32)

        @pl.when(kk == 0)
        def _():
            o_ref[...] = part

        @pl.when(kk != 0)
        def _():
            o_ref[...] += part

    return pl.pallas_call(
        body, name="ada_fwd", grid=(DEPTH, nk),
        in_specs=[pl.BlockSpec((16, tk), lambda l, k: (0, k)), pl.BlockSpec((None, tk, NS), lambda l, k: (l, k, 0))],
        out_specs=pl.BlockSpec((None, 16, NS), lambda l, k: (l, 0, 0)),
        out_shape=jax.ShapeDtypeStruct((DEPTH, 16, NS), F32), compiler_params=_cp(("parallel", "arbitrary")))(c9, ada_w)


def _ada_bwd(cf, c9, ada_w, dm9):
    D = cf.D
    NS = ada_w.shape[-1]
    tk = _tile(D, 512)
    nk = D // tk

    def body(c_ref, w_ref, dm_ref, gw_ref, ds_ref):
        cv = c_ref[...]
        sg = _sigmoid(cv)
        dmb = dm_ref[...].astype(BF16)
        gw_ref[...] = lax.dot_general((cv * sg).astype(BF16), dmb, (_DN['tn'], ((), ())), preferred_element_type=F32)
        ds = lax.dot_general(dmb, w_ref[...].astype(BF16), (_DN['nt'], ((), ())), preferred_element_type=F32)
        ds_ref[...] = ds * (sg * (1.0 + cv * (1.0 - sg)))

    return pl.pallas_call(
        body, name="ada_bwd", grid=(DEPTH, nk),
        in_specs=[pl.BlockSpec((16, tk), lambda l, k: (0, k)), pl.BlockSpec((None, tk, NS), lambda l, k: (l, k, 0)),
                  pl.BlockSpec((None, 16, NS), lambda l, k: (l, 0, 0))],
        out_specs=[pl.BlockSpec((None, tk, NS), lambda l, k: (l, k, 0)), pl.BlockSpec((None, 16, tk), lambda l, k: (l, 0, k))],
        out_shape=[jax.ShapeDtypeStruct((DEPTH, D, NS), F32), jax.ShapeDtypeStruct((DEPTH, 16, D), F32)],
        compiler_params=_cp(("parallel", "parallel")))(c9, ada_w, dm9)


def _rope_tables(cf):
    L, LC = cf.L, cf.LC
    rows = L // GRID_W
    row = jnp.repeat(jnp.arange(rows, dtype=F32), GRID_W)
    col = jnp.tile(jnp.arange(GRID_W, dtype=F32), rows)
    nf = HEAD_DIM // 4
    inv = ROPE_THETA ** (-jnp.arange(nf, dtype=F32) / nf)
    ang = jnp.concatenate([row[:, None] * inv, col[:, None] * inv], axis=-1)
    cos, sin = jnp.cos(ang), jnp.sin(ang)
    cs = jnp.concatenate([jnp.ones((LC, HEAD_DIM), F32), jnp.concatenate([cos, cos], -1)], 0)
    sn = jnp.concatenate([jnp.zeros((LC, HEAD_DIM), F32), jnp.concatenate([-sin, sin], -1)], 0)
    return cs, sn


def _prep_tiles(cf, z, cs, sn, key):
    b = cf.OFF[key] // LANE
    return [(z, LANE, _col(b), True), (cs, LANE, _c0, False), (sn, LANE, _c0, False)]


_PREP = {'aq': _f_prep_norm, 'ak': _f_prep_norm, 'rq': _f_prep_plain, 'rk': _f_prep_scaled}


def _prep_fwd(cf, z, cs, sn, key, g):
    nh = cf.W[key] // LANE
    params = [(g, 'shared', True)] if g is not None else []
    return _row_fwd(cf, "prep_fwd_" + key, _PREP[key], _prep_tiles(cf, z, cs, sn, key), params,
                    [(LANE, _col(0), cf.W[key], BF16)], cf.TQ, nrep=nh)[0]


def _prep_bwd(cf, z, cs, sn, key, g, dt):
    nh = cf.W[key] // LANE
    params = [(g, 'shared', True)] if g is not None else []
    tg, pg = _row_bwd(cf, "prep_bwd_" + key, _PREP[key], _prep_tiles(cf, z, cs, sn, key), params,
                      [(dt, LANE, _col(0))], [(LANE, _col(0), cf.W[key], BF16)], cf.TQ, nrep=nh)
    return tg[0], (pg[0] if g is not None else None)


def _gate_params(cf, gup, gb):
    K = gup.shape[-1]
    gf = jnp.zeros((LANE, K), F32).at[0:GLA_RANK].set(gup[0])
    gbm = jnp.zeros((LANE, K), F32).at[GLA_RANK:2 * GLA_RANK].set(gup[1])
    out = []
    for p in range(K // LANE):
        cols = slice(p * LANE, (p + 1) * LANE)
        out += [(gf[:, cols], 'shared', True), (gbm[:, cols], 'shared', True), (gb[0:1, cols], 'shared', True), (gb[1:2, cols], 'shared', True)]
    return out


def _mix_tiles(cf, z, o, key):
    return [(o, LANE, _col(0), True), (z, LANE, _col(cf.OFF[key] // LANE), True)]


def _mid_io(cf, l, W, mod, x, y, norm2_g=None):
    tiles = [(x, cf.D, _c0, True), (y, cf.D, _c0, True)]
    n2 = W['norm2_g'][l] if norm2_g is None else norm2_g
    params = [(mod[2], 'stream', True), (n2, 'shared', True), (mod[3], 'stream', True), (mod[4], 'stream', True)]
    return tiles, params


def _launch_scatter(cf, kind, l, seq, g, nxt):
    g, nxt = lax.optimization_barrier((g, nxt))
    return _seq_scatter(cf, kind, l, seq, g), nxt


class _BigWeights:
    def __init__(self, cf, shards):
        self.cf = cf
        self.whole = {(kind, l): _seq_gather(cf, kind, l, l * len(_KINDS) + n, shards[n])
                      for l in range(DEPTH) for n, kind in enumerate(_KINDS)}
        self.w_in = {}

    def get(self, kind, l, after=None):
        cf = self.cf
        if kind != 'in':
            return self.whole[(kind, l)]
        if l not in self.w_in:
            whole, _ = lax.optimization_barrier((self.whole[(kind, l)], after))
            self.w_in[l] = _unshard_in(cf, "unshard_in_%d" % l, whole)
        return self.w_in[l]


def _layer_fwd(cf, l, W, big, mod, x, h, cs, sn):
    T, D, Fd = cf.T, cf.D, cf.F
    z = _mm("z_%d" % l, h, big.get('in', l, h), 'nn', T, cf.NZ, D, F32, tm=T, tn=768, tk=D)
    qa = _prep_fwd(cf, z, cs, sn, 'aq', W['q_norm_g'][l])
    ka = _prep_fwd(cf, z, cs, sn, 'ak', W['k_norm_g'][l])
    qr = _prep_fwd(cf, z, cs, sn, 'rq', None)
    kr = _prep_fwd(cf, z, cs, sn, 'rk', None)
    o_att = _att_fwd(cf, qa, ka, z)
    o_ret = _ret_fwd(cf, qr, kr, z, W['ret_log_decay'][l])
    gates = _gate_params(cf, W['gla_gate_up'][l], W['gla_gate_b'][l])
    ga_tile = [(z, LANE, _col(cf.OFF['ga'] // LANE), True)]
    we = (cf.HG // 2) * N_DECAY * LANE
    ef, eb = _row_fwd(cf, "gates_fwd_%d" % l, _f_gla_pre, ga_tile, gates, [(we, _c0, we, F32), (we, _c0, we, F32)], GLA_CHUNK)
    o_f, sf, o_b, sb = _gla_fwd(cf, z, ef, eb)
    o_gla = o_f + o_b
    cat_r = _row_fwd(cf, "mixr_fwd_%d" % l, _f_gated_norm, _mix_tiles(cf, z, o_ret, 'rg'), [(W['ret_norm_g'][l], 'shared', True)],
                     [(LANE, _col(0), cf.HR * LANE, BF16)], cf.TQ, nrep=cf.HR)[0]
    cat_g = _row_fwd(cf, "mixg_fwd_%d" % l, _f_gated_norm, _mix_tiles(cf, z, o_gla, 'gr'), [(W['gla_norm_g'][l], 'shared', True)],
                     [(LANE, _col(0), cf.HG * LANE, BF16)], cf.TQ, nrep=cf.HG)[0]
    cat = jnp.concatenate([o_att, cat_r, cat_g], axis=-1)
    y = _mm("y_%d" % l, cat, big.get('out', l, cat), 'nn', T, D, D, F32, tm=T, tn=512, tk=D)
    tiles, params = _mid_io(cf, l, W, mod, x, y)
    x1, h2 = _row_fwd(cf, "mid_fwd_%d" % l, _f_resid_norm_mod, tiles, params, [(D, _c0, D, F32), (D, _c0, D, BF16)], cf.TM)
    u = _mm("u_%d" % l, h2, big.get('up', l, h2), 'nn', T, 2 * Fd, D, BF16, tm=T, tn=512, tk=D)
    t = _conv_fwd(cf, u, W['conv_wb'][l])
    yff = _mm("yff_%d" % l, t, big.get('down', l, t), 'nn', T, D, Fd, F32, tm=T, tn=1024, tk=512)
    return dict(x=x, h=h, z=z, qa=qa, ka=ka, qr=qr, kr=kr, ef=ef, eb=eb, sf=sf, sb=sb, o_ret=o_ret, o_gla=o_gla, cat=cat, y=y,
                x1=x1, h2=h2, u=u, t=t, yff=yff, gates=gates)


def _layer_bwd(cf, l, W, big, mod, sv, dx1, dyff, cs, sn):
    T, D, Fd = cf.T, cf.D, cf.F
    g, rs = {}, {}
    sq = 2 * len(_KINDS) + (DEPTH - 1 - l) * len(_KINDS)
    gwd = _mm("gwd_%d" % l, sv['t'], dyff, 'tn', Fd, D, T, BF16, tm=1408, tn=2048, tk=T)
    rs['down'], wb = _launch_scatter(cf, 'down', l, sq, gwd, W['conv_wb'][l])
    dt = _mm("dt_%d" % l, dyff, big.get('down', l), 'nt', T, Fd, D, BF16, tm=T, tn=1408, tk=D)
    du, g['conv_wb'] = _conv_bwd(cf, sv['u'], wb, dt)
    cu = 2 * Fd // N_DEV
    half = Fd // cu
    gwu = _mm("gwu_%d" % l, sv['h2'], du, 'tn', D, 2 * Fd, T, BF16, tm=D, tn=cu, tk=T, out_shape=(N_DEV, D, cu),
              b_spec=pl.BlockSpec((None, T, cu), lambda i, j, k: (j // half, 0, j % half)),
              out_spec=pl.BlockSpec((None, D, cu), lambda i, j, k: (j, i, 0)))
    rs['up'], n2 = _launch_scatter(cf, 'up', l, sq + 1, gwu, W['norm2_g'][l])
    dh2 = _mm("dh2_%d" % l, du, big.get('up', l), 'nt', T, D, 2 * Fd, BF16, tm=T, tn=1024, tk=cu,
              a_spec=pl.BlockSpec((None, T, cu), lambda i, j, k: (k // half, 0, k % half)))
    tiles, params = _mid_io(cf, l, W, mod, sv['x'], sv['y'], n2)
    (dx, dy), (g['m2'], g['norm2_g'], g['m3'], g['m4']) = _row_bwd(
        cf, "mid_bwd_%d" % l, _f_resid_norm_mod, tiles, params, [(dx1, D, _c0), (dh2, D, _c0)],
        [(D, _c0, D, F32), (D, _c0, D, BF16)], cf.TM)
    gwo = _mm("gwo_%d" % l, sv['cat'], dy, 'tn', D, D, T, BF16, tm=D, tn=1024, tk=T)
    rs['out'], rn = _launch_scatter(cf, 'out', l, sq + 2, gwo, W['ret_norm_g'][l])
    dcat = _mm("dcat_%d" % l, dy, big.get('out', l), 'nt', T, D, D, BF16, tm=T, tn=1024, tk=D)
    z = sv['z']
    (do_ret, drg), (g['ret_norm_g'],) = _row_bwd(
        cf, "mixr_bwd_%d" % l, _f_gated_norm, _mix_tiles(cf, z, sv['o_ret'], 'rg'), [(rn, 'shared', True)],
        [(dcat, LANE, _col(cf.HQ))], [(LANE, _col(0), cf.HR * LANE, F32), (LANE, _col(0), cf.HR * LANE, BF16)], cf.TQ, nrep=cf.HR)
    (do_gla, dgr), (g['gla_norm_g'],) = _row_bwd(
        cf, "mixg_bwd_%d" % l, _f_gated_norm, _mix_tiles(cf, z, sv['o_gla'], 'gr'), [(W['gla_norm_g'][l], 'shared', True)],
        [(dcat, LANE, _col(cf.HQ + cf.HR))], [(LANE, _col(0), cf.HG * LANE, F32), (LANE, _col(0), cf.HG * LANE, BF16)], cf.TQ, nrep=cf.HG)
    dqa, dka, dav = _att_bwd(cf, sv['qa'], sv['ka'], z, dcat)
    dqr, dkr, drv, dlg = _ret_bwd(cf, sv['qr'], sv['kr'], z, W['ret_log_decay'][l], do_ret)
    g['ret_log_decay'] = dlg[:, 0:2, 0].T
    dq_f, dk_f, dv_f, def_, dq_b, dk_b, dv_b, deb = _gla_bwd(cf, z, sv['ef'], sv['eb'], sv['sf'], sv['sb'], do_gla)
    dgq, dgk, dgv = dq_f + dq_b, dk_f + dk_b, dv_f + dv_b
    we = (cf.HG // 2) * N_DECAY * LANE
    ga_tile = [(z, LANE, _col(cf.OFF['ga'] // LANE), True)]
    (dga,), gg = _row_bwd(cf, "gates_bwd_%d" % l, _f_gla_pre, ga_tile, sv['gates'],
                          [(def_, we, _c0), (deb, we, _c0)], [(LANE, _c0, LANE, BF16)], GLA_CHUNK)
    ggf, ggb, gbf, gbb = [jnp.concatenate(gg[n::4], axis=-1) for n in range(4)]
    g['gla_gate_up'] = jnp.stack([ggf[0:GLA_RANK], ggb[GLA_RANK:2 * GLA_RANK]])
    g['gla_gate_b'] = jnp.concatenate([gbf, gbb], axis=0)
    daq, g['q_norm_g'] = _prep_bwd(cf, z, cs, sn, 'aq', W['q_norm_g'][l], dqa)
    dak, g['k_norm_g'] = _prep_bwd(cf, z, cs, sn, 'ak', W['k_norm_g'][l], dka)
    drq, _ = _prep_bwd(cf, z, cs, sn, 'rq', None, dqr)
    drk, _ = _prep_bwd(cf, z, cs, sn, 'rk', None, dkr)
    pad = jnp.zeros((T, cf.NZ - cf.OFF['ga'] - LANE), BF16)
    dz = jnp.concatenate([daq, dak, dav.astype(BF16), drq, drk, drv.astype(BF16), drg, dgq.astype(BF16), dgk.astype(BF16),
                          dgv.astype(BF16), dgr, dga, pad], axis=-1)
    gwi = _mm("gwi_%d" % l, sv['h'], dz, 'tn', D, cf.NZ, T, BF16, tm=D, tn=768, tk=T)
    rs['in'], g['norm1_g_tied'] = _launch_scatter(cf, 'in', l, sq + 3, _slabs_in(cf, "slabs_in_%d" % l, gwi), W['norm1_g'][l])
    dh =_mm("dh_%d" % l, dz, big.get('in', l), 'nt', T, D, cf.NZ, BF16, tm=T, tn=1024, tk=1792)
    g['rs'] = rs
    return dx, dh, g


_WEIGHTS = ['c_ctx', 'ada_w', 'ada_b', 'norm1_g', 'w_in', 'q_norm_g', 'k_norm_g', 'ret_log_decay', 'ret_norm_g',
            'gla_gate_up', 'gla_gate_b', 'gla_norm_g', 'w_out', 'norm2_g', 'w_up', 'conv_w', 'conv_b', 'w_down', 'final_norm_g']
_BIG = ['w_in', 'w_out', 'w_up', 'w_down']
_SMALL = [n for n in _WEIGHTS if n not in _BIG and n != 'ada_w']
_COL_SHARDED = ['gla_gate_up', 'gla_gate_b', 'conv_w']


def _pack(arrs):
    rows = []
    for a in arrs:
        flat = a.reshape(-1)
        n = flat.shape[0]
        rows.append(jnp.pad(flat, (0, -n % LANE)).reshape(-1, LANE))
    packed = jnp.concatenate(rows, axis=0)
    return jnp.pad(packed, ((0, -packed.shape[0] % 8), (0, 0)))


def _unpack(packed, shapes):
    lead = packed.shape[:-2]
    out, r = [], 0
    for s in shapes:
        n = int(np.prod(s))
        nr = -(-n // LANE)
        out.append(packed[..., r:r + nr, :].reshape(lead + (nr * LANE,))[..., :n].reshape(lead + tuple(s)))
        r += nr
    return out


def _unshard_last(a):
    return jnp.moveaxis(a, 0, -2).reshape(a.shape[1:-1] + (N_DEV * a.shape[-1],))


def _step(cf, x, c, ctx, loss_target, w, m, v):
    T, D, Fd, L, LC = cf.T, cf.D, cf.F, cf.L, cf.LC
    _, _, _, me = _me()
    NS = w['ada_w'].shape[-1]

    c_all = _ag_small("ag_c", jnp.pad(c, ((0, 7), (0, 0))))[:, 0, :]
    c9 = jnp.concatenate([c_all, w['c_ctx'][None], jnp.zeros((7, D), F32)], axis=0)
    pm = _ada_fwd(cf, c9, w['ada_w'])
    pm_all = _ag_small("ag_mod", pm.reshape(DEPTH * 16, NS)).reshape(N_DEV, DEPTH, 16, NS)
    mod_all = _unshard_last(pm_all) + w['ada_b'][:, None, :]
    mod_own = lax.dynamic_index_in_dim(mod_all, me, axis=1, keepdims=False)
    mods = []
    for l in range(DEPTH):
        mods.append([jnp.stack([mod_all[l, 8, k * D:(k + 1) * D], mod_own[l, k * D:(k + 1) * D]])[:, None, :] for k in range(N_MOD)])

    shard_shapes = [w[n].shape for n in _COL_SHARDED]
    got = _ag_small("ag_smallw", _pack([w[n] for n in _COL_SHARDED]))
    full = dict(zip(_COL_SHARDED, [_unshard_last(a) for a in _unpack(got, shard_shapes)]))

    small_done = full['conv_w'] + mod_all[0, 0, 0]
    big = _BigWeights(cf, [_cast_bf16("cast_" + n, w[n], c if n == 'w_in' else small_done) for n in _BIG])
    conv_wb = jnp.concatenate([full['conv_w'], w['conv_b'][:, None, :], jnp.zeros((DEPTH, 4, Fd), F32)], axis=1)
    W = dict(conv_wb=conv_wb, gla_gate_up=full['gla_gate_up'], gla_gate_b=full['gla_gate_b'], ret_log_decay=w['ret_log_decay'])
    for n in ['q_norm_g', 'k_norm_g', 'ret_norm_g', 'gla_norm_g', 'norm1_g', 'norm2_g']:
        W[n] = w[n][:, None, :]

    cs, sn = _rope_tables(cf)
    x0 = jnp.concatenate([ctx[0], x[0]], axis=0)
    pre_tiles = [(x0, D, _c0, True)]

    def pre_params(n1):
        return [(n1, 'shared', True), (mods[0][0], 'stream', True), (mods[0][1], 'stream', True)]

    def tr_params(n1):
        return [(mods[0][5], 'stream', True), (n1, 'shared', True), (mods[1][0], 'stream', True), (mods[1][1], 'stream', True)]

    h0 = _row_fwd(cf, "pre_fwd", _f_norm_mod, pre_tiles, pre_params(W['norm1_g'][0]), [(D, _c0, D, BF16)], cf.TM)[0]
    sv0 = _layer_fwd(cf, 0, W, big, mods[0], x0, h0, cs, sn)
    tr_tiles = [(sv0['x1'], D, _c0, True), (sv0['yff'], D, _c0, True)]
    xb, hb = _row_fwd(cf, "tr_fwd", _f_resid_norm_mod, tr_tiles, tr_params(W['norm1_g'][1]), [(D, _c0, D, F32), (D, _c0, D, BF16)], cf.TM)
    sv1 = _layer_fwd(cf, 1, W, big, mods[1], xb, hb, cs, sn)
    tgt = jnp.concatenate([jnp.zeros((LC, D), F32), loss_target[0]], axis=0)
    dx1, dyff, dm5_1, g_final, ls = _loss_grad(cf, sv1['x1'], sv1['yff'], mods[1][5], w['final_norm_g'][None], tgt)
    loss = lax.psum(ls[0, 0], ("x", "y", "c"))

    dxb, dhb, g1 = _layer_bwd(cf, 1, W, big, mods[1], sv1, dx1, dyff, cs, sn)
    (dx1_0, dyff_0), (dm5_0, gn1_1, dm0_1, dm1_1) = _row_bwd(
        cf, "tr_bwd", _f_resid_norm_mod, tr_tiles, tr_params(g1['norm1_g_tied']), [(dxb, D, _c0), (dhb, D, _c0)],
        [(D, _c0, D, F32), (D, _c0, D, BF16)], cf.TM)
    dx0, dh0, g0 = _layer_bwd(cf, 0, W, big, mods[0], sv0, dx1_0, dyff_0, cs, sn)
    (dxa,), (gn1_0, dm0_0, dm1_0) = _row_bwd(cf, "pre_bwd", _f_first, pre_tiles, pre_params(g0['norm1_g_tied']), [(dx0, D, _c0), (dh0, D, _c0)],
                                            [(D, _c0, D, F32)], cf.TM)
    grad_x = dxa[LC:][None]

    dmod = jnp.stack([jnp.concatenate([dm0_0, dm1_0, g0['m2'], g0['m3'], g0['m4'], dm5_0], axis=-1)[:, 0],
                      jnp.concatenate([dm0_1, dm1_1, g1['m2'], g1['m3'], g1['m4'], dm5_1], axis=-1)[:, 0]])
    dm_all = _ag_small("ag_dmod", jnp.pad(dmod.reshape(2 * DEPTH, N_MOD * D), ((0, 8 - 2 * DEPTH), (0, 0))))
    dm_all = dm_all[:, :2 * DEPTH].reshape(N_DEV, DEPTH, 2, N_MOD * D)
    dctx = _sum8("sum_dmodc", jnp.pad(dm_all[:, :, 0], ((0, 0), (0, 8 - DEPTH), (0, 0))))[:DEPTH]
    dm9 = jnp.concatenate([jnp.moveaxis(dm_all[:, :, 1], 0, 1), dctx[:, None]], axis=1)
    g_ada_b = _sum8("sum_adab", jnp.pad(jnp.moveaxis(dm9, 1, 0), ((0, 0), (0, 8 - DEPTH), (0, 0))))[:DEPTH]
    dm9s = lax.dynamic_slice_in_dim(jnp.pad(dm9, ((0, 0), (0, 7), (0, 0))), me * NS, NS, axis=2)
    g_ada_w, dsil = _ada_bwd(cf, c9, w['ada_w'], dm9s)
    g_cctx_part = dsil[0, 8]
    for l in range(1, DEPTH):
        g_cctx_part = g_cctx_part + dsil[l, 8]

    def both(key):
        return jnp.stack([g0[key], g1[key]])

    gsmall = dict(c_ctx=g_cctx_part, norm1_g=jnp.stack([gn1_0[0], gn1_1[0]]), q_norm_g=both('q_norm_g')[:, 0],
                  k_norm_g=both('k_norm_g')[:, 0], ret_log_decay=both('ret_log_decay'), ret_norm_g=both('ret_norm_g')[:, 0],
                  gla_gate_up=both('gla_gate_up'), gla_gate_b=both('gla_gate_b'), gla_norm_g=both('gla_norm_g')[:, 0],
                  norm2_g=both('norm2_g')[:, 0], conv_w=both('conv_wb')[:, 0:3], conv_b=both('conv_wb')[:, 3], final_norm_g=g_final[0])
    snames = [n for n in _SMALL if n != 'ada_b']
    sshapes = [gsmall[n].shape for n in snames]
    gs_all = _ag_small("ag_gsmall", _pack([gsmall[n] for n in snames]))
    gs = dict(zip(snames, _unpack(_sum8("sum_gsmall", gs_all), sshapes)))
    gs['ada_b'] = g_ada_b
    for n in _COL_SHARDED:
        ns_ = w[n].shape[-1]
        gs[n] = lax.dynamic_slice_in_dim(gs[n], me * ns_, ns_, axis=gs[n].ndim - 1)

    out_g, out_d, out_m, out_v = {}, {}, {}, {}

    after, done = gs_all, {}
    for l, gl in ((1, g1), (0, g0)):
        for kind in reversed(_KINDS):
            n = 'w_' + kind
            done[n] = _sum_adam("adam_%s_%d" % (n, l), gl['rs'][kind], w[n], m[n], v[n], l, done.get(n), after)
            after = done[n][0]
    for n in _BIG:
        out_g[n], out_d[n], out_m[n], out_v[n] = done[n]
    aw = [a.reshape(DEPTH * D, NS) for a in (w['ada_w'], g_ada_w, m['ada_w'], v['ada_w'])]
    out_g['ada_w'] = g_ada_w
    out_d['ada_w'], out_m['ada_w'], out_v['ada_w'] = [a.reshape(DEPTH, D, NS) for a in _adam("adam_ada_w", *aw)]
    shp = [w[n].shape for n in _SMALL]
    packed = [_pack([src[n] for n in _SMALL]) for src in (w, gs, m, v)]
    res = _adam("adam_small", *packed)
    for dst, pk in zip((out_d, out_m, out_v), res):
        dst.update(zip(_SMALL, _unpack(pk, shp)))
    out_g.update({n: gs[n] for n in _SMALL})
    return (loss, grad_x, *[out_g[n] for n in _WEIGHTS], *[out_d[n] for n in _WEIGHTS], *[out_m[n] for n in _WEIGHTS],
            *[out_v[n] for n in _WEIGHTS])


def kernel(x, c, ctx, c_ctx, ada_w, ada_b, norm1_g, w_in, q_norm_g, k_norm_g, ret_log_decay, ret_norm_g, gla_gate_up, gla_gate_b, gla_norm_g, w_out, norm2_g, w_up, conv_w, conv_b, w_down, final_norm_g, loss_target, m_c_ctx, m_ada_w, m_ada_b, m_norm1_g, m_w_in, m_q_norm_g, m_k_norm_g, m_ret_log_decay, m_ret_norm_g, m_gla_gate_up, m_gla_gate_b, m_gla_norm_g, m_w_out, m_norm2_g, m_w_up, m_conv_w, m_conv_b, m_w_down, m_final_norm_g, v_c_ctx, v_ada_w, v_ada_b, v_norm1_g, v_w_in, v_q_norm_g, v_k_norm_g, v_ret_log_decay, v_ret_norm_g, v_gla_gate_up, v_gla_gate_b, v_gla_norm_g, v_w_out, v_norm2_g, v_w_up, v_conv_w, v_conv_b, v_w_down, v_final_norm_g):
    w = dict(c_ctx=c_ctx, ada_w=ada_w, ada_b=ada_b, norm1_g=norm1_g, w_in=w_in, q_norm_g=q_norm_g, k_norm_g=k_norm_g,
             ret_log_decay=ret_log_decay, ret_norm_g=ret_norm_g, gla_gate_up=gla_gate_up, gla_gate_b=gla_gate_b,
             gla_norm_g=gla_norm_g, w_out=w_out, norm2_g=norm2_g, w_up=w_up, conv_w=conv_w, conv_b=conv_b, w_down=w_down,
             final_norm_g=final_norm_g)
    m = dict(c_ctx=m_c_ctx, ada_w=m_ada_w, ada_b=m_ada_b, norm1_g=m_norm1_g, w_in=m_w_in, q_norm_g=m_q_norm_g,
             k_norm_g=m_k_norm_g, ret_log_decay=m_ret_log_decay, ret_norm_g=m_ret_norm_g, gla_gate_up=m_gla_gate_up,
             gla_gate_b=m_gla_gate_b, gla_norm_g=m_gla_norm_g, w_out=m_w_out, norm2_g=m_norm2_g, w_up=m_w_up,
             conv_w=m_conv_w, conv_b=m_conv_b, w_down=m_w_down, final_norm_g=m_final_norm_g)
    v = dict(c_ctx=v_c_ctx, ada_w=v_ada_w, ada_b=v_ada_b, norm1_g=v_norm1_g, w_in=v_w_in, q_norm_g=v_q_norm_g,
             k_norm_g=v_k_norm_g, ret_log_decay=v_ret_log_decay, ret_norm_g=v_ret_norm_g, gla_gate_up=v_gla_gate_up,
             gla_gate_b=v_gla_gate_b, gla_norm_g=v_gla_norm_g, w_out=v_w_out, norm2_g=v_norm2_g, w_up=v_w_up,
             conv_w=v_conv_w, conv_b=v_conv_b, w_down=v_w_down, final_norm_g=v_final_norm_g)
    return _step(_cfg(), x, c, ctx, loss_target, w, m, v)
```

```python
import functools
import math
import types

import jax
import jax.numpy as jnp
import numpy as np
from jax import lax
from jax.experimental import pallas as pl
from jax.experimental.pallas import tpu as pltpu
from jax.experimental.pallas import tpu_sc as plsc

F32 = jnp.float32
BF16 = jnp.bfloat16

D_MODEL = 2048
SEQ = 2048
CTX_LEN = 256
GRID_W = 64
D_FF = 5632
DEPTH = 2
N_DEV = 8
HEAD_DIM = 128
ROPE_THETA = 10000.0
GLA_TAU = 16.0
GLA_RANK = 16
GLA_CHUNK = 64
GLA_SUB = 16
EPS = 1e-6
N_MOD = 6
ADAM_LR = 0.001
ADAM_B1 = 0.9
ADAM_B2 = 0.999
ADAM_EPS = 1e-08
ADAM_WD = 0.01
ADAM_STEP = 10
LANE = 128
VMEM_LIMIT = 56 * 1024 * 1024
NEG = -1e30


def _cfg():
    d = types.SimpleNamespace()
    d.D, d.L, d.LC, d.F = D_MODEL, SEQ, CTX_LEN, D_FF
    d.T = d.L + d.LC
    nm = d.D // HEAD_DIM
    d.HQ, d.HKV, d.HR, d.HG = nm // 2, nm // 8, nm // 4, nm // 4
    d.G = d.HQ // d.HKV
    w = dict(aq=d.HQ * 128, ak=d.HKV * 128, av=d.HKV * 128, rq=d.HR * 128, rk=d.HR * 128, rv=d.HR * 128,
             rg=d.HR * 128, gq=d.HG * 64, gk=d.HG * 64, gv=d.HG * 128, gr=d.HG * 128, ga=2 * GLA_RANK)
    off, o = {}, 0
    for k, v in w.items():
        off[k] = o
        o += v
    d.W, d.OFF, d.NIN = w, off, o
    d.NZ = -(-(off['ga'] + LANE) // 256) * 256
    d.NINS = d.NIN // N_DEV
    d.TM = math.gcd(d.LC, 128)
    d.TQ = math.gcd(d.LC, 256)
    return d


def _cp(sem=None):
    return pltpu.CompilerParams(dimension_semantics=sem, vmem_limit_bytes=VMEM_LIMIT)


def _tile(n, target, mult=LANE):
    t = min(n, target)
    t -= t % mult
    while t > mult and n % t:
        t -= mult
    return t if t > 0 and n % t == 0 else n


_DN = {'nn': ((1,), (0,)), 'nt': ((1,), (1,)), 'tn': ((0,), (0,))}


def _mm(name, a, b, kind, M, N, K, out_dtype, tm=768, tn=768, tk=1024, a_spec=None, b_spec=None,
        out_shape=None, out_spec=None):
    tm, tn = _tile(M, tm, 128), _tile(N, tn, 128)
    tk = _tile(K, tk, 128)
    nk = K // tk

    def dot(a_ref, b_ref):
        return lax.dot_general(a_ref[...].astype(BF16), b_ref[...].astype(BF16), (_DN[kind], ((), ())), preferred_element_type=F32)

    def body_one(a_ref, b_ref, o_ref):
        o_ref[...] = dot(a_ref, b_ref).astype(o_ref.dtype)

    def body(a_ref, b_ref, o_ref, acc):
        kk = pl.program_id(2)

        @pl.when(kk == 0)
        def _():
            acc[...] = jnp.zeros_like(acc)

        acc[...] += dot(a_ref, b_ref)

        @pl.when(kk == nk - 1)
        def _():
            o_ref[...] = acc[...].astype(o_ref.dtype)

    if a_spec is None:
        a_spec = pl.BlockSpec((tk, tm), lambda i, j, k: (k, i)) if kind == 'tn' else pl.BlockSpec((tm, tk), lambda i, j, k: (i, k))
    if b_spec is None:
        b_spec = pl.BlockSpec((tn, tk), lambda i, j, k: (j, k)) if kind == 'nt' else pl.BlockSpec((tk, tn), lambda i, j, k: (k, j))
    if out_spec is None:
        out_spec = pl.BlockSpec((tm, tn), lambda i, j, k: (i, j))
        out_shape = (M, N)
    return pl.pallas_call(
        body_one if nk == 1 else body, name=name, grid=(M // tm, N // tn, nk), in_specs=[a_spec, b_spec], out_specs=out_spec,
        out_shape=jax.ShapeDtypeStruct(out_shape, out_dtype), scratch_shapes=[] if nk == 1 else [pltpu.VMEM((tm, tn), F32)],
        compiler_params=_cp(("parallel", "parallel", "arbitrary")))(a, b)


def _tile_spec(tm, w, colf, nrep):
    if hasattr(colf, 'base'):
        assert colf.base % nrep == 0
        return pl.BlockSpec((tm, w * nrep), functools.partial(lambda i, b: (i, b), b=colf.base // nrep))
    return pl.BlockSpec((tm, w), lambda i: (i, 0))


def _head_cols(colf, w, r):
    return slice(r * w, (r + 1) * w) if hasattr(colf, 'base') else slice(None)


def _row_specs(cf, tm, tiles, params, nrep):
    nctx = cf.LC // tm
    specs = [_tile_spec(tm, w, colf, nrep) for arr, w, colf, _ in tiles]
    for arr, kind, _ in params:
        nd = arr.ndim
        if kind == 'shared':
            specs.append(pl.BlockSpec(arr.shape, functools.partial(lambda i, nd: (0,) * nd, nd=nd)))
        else:
            specs.append(pl.BlockSpec((None,) + arr.shape[1:],
                                      functools.partial(lambda i, nd, nctx: (jnp.where(i >= nctx, 1, 0),) + (0,) * (nd - 1), nd=nd, nctx=nctx)))
    return specs


def _row_fwd(cf, name, f, tiles, params, outs, tm, nrep=1):
    nt, npar = len(tiles), len(params)

    def body(*refs):
        pv = [r[...] for r in refs[nt:nt + npar]]
        for r in range(nrep):
            tv = [x[:, _head_cols(t[2], t[1], r)].astype(F32) for x, t in zip(refs[:nt], tiles)]
            res = f(*tv, *pv)
            for o, v, spec in zip(refs[nt + npar:], res, outs):
                o[:, _head_cols(spec[1], spec[0], r)] = v.astype(o.dtype)

    out_specs = [_tile_spec(tm, w, colf, nrep) for w, colf, _, _ in outs]
    out_shape = [jax.ShapeDtypeStruct((cf.T, tw), dt) for _, _, tw, dt in outs]
    return pl.pallas_call(
        body, name=name, grid=(cf.T // tm,), in_specs=_row_specs(cf, tm, tiles, params, nrep), out_specs=out_specs,
        out_shape=out_shape, compiler_params=_cp(("arbitrary",)))(*[t[0] for t in tiles], *[p[0] for p in params])


def _row_bwd(cf, name, f, tiles, params, cts, tgrads, tm, nrep=1):
    nt, npar, nc = len(tiles), len(params), len(cts)
    tdiff = [k for k, t in enumerate(tiles) if t[3]]
    pdiff = [k for k, p in enumerate(params) if p[2]]
    nctx = cf.LC // tm

    def body(*refs):
        i = pl.program_id(0)
        pv = [x[...] for x in refs[nt:nt + npar]]
        outs = refs[nt + npar + nc:]
        psum = None
        for r in range(nrep):
            tv = [x[:, _head_cols(t[2], t[1], r)].astype(F32) for x, t in zip(refs[:nt], tiles)]
            cv = tuple(x[:, _head_cols(c[2], c[1], r)].astype(F32) for x, c in zip(refs[nt + npar:nt + npar + nc], cts))

            def g(*diff, tv=tv):
                tv2, pv2 = list(tv), list(pv)
                for k, v in zip(tdiff, diff[:len(tdiff)]):
                    tv2[k] = v
                for k, v in zip(pdiff, diff[len(tdiff):]):
                    pv2[k] = v
                return tuple(f(*tv2, *pv2))

            _, vjp_fn = jax.vjp(g, *[tv[k] for k in tdiff], *[pv[k] for k in pdiff])
            grads = vjp_fn(cv)
            for o, gv, spec in zip(outs[:len(tdiff)], grads[:len(tdiff)], tgrads):
                o[:, _head_cols(spec[1], spec[0], r)] = gv.astype(o.dtype)
            pg = grads[len(tdiff):]
            psum = list(pg) if psum is None else [a + b for a, b in zip(psum, pg)]
        for n_, (o, gv) in enumerate(zip(outs[len(tdiff):], psum)):
            first = (i == 0) if params[pdiff[n_]][1] == 'shared' else jnp.logical_or(i == 0, i == nctx)

            @pl.when(first)
            def _():
                o[...] = gv

            @pl.when(jnp.logical_not(first))
            def _():
                o[...] += gv

    in_specs = _row_specs(cf, tm, tiles, params, nrep)
    in_specs += [_tile_spec(tm, w, colf, nrep) for _, w, colf in cts]
    out_specs = [_tile_spec(tm, w, colf, nrep) for w, colf, _, _ in tgrads]
    out_shape = [jax.ShapeDtypeStruct((cf.T, tw), dt) for _, _, tw, dt in tgrads]
    out_specs += _row_specs(cf, tm, [], [params[k] for k in pdiff], nrep)
    out_shape += [jax.ShapeDtypeStruct(params[k][0].shape, F32) for k in pdiff]
    res = pl.pallas_call(
        body, name=name, grid=(cf.T // tm,), in_specs=in_specs, out_specs=out_specs, out_shape=out_shape,
        compiler_params=_cp(("arbitrary",)))(*[t[0] for t in tiles], *[p[0] for p in params], *[c[0] for c in cts])
    return res[:len(tdiff)], res[len(tdiff):]


def _c0(r):
    return 0


def _col(base):
    def col(r):
        return base + r
    col.base = base
    return col


def _rms(x, g):
    return x * lax.rsqrt(jnp.mean(x * x, axis=-1, keepdims=True) + EPS) * g


def _sigmoid(x):
    return 1.0 / (1.0 + jnp.exp(-x))


def _silu(x):
    return x * _sigmoid(x)


def _f_norm_mod(x, g, sh, sc):
    return (_rms(x, g) * (1 + sc) + sh,)


def _f_resid_norm_mod(x, y, gate, g, sh, sc):
    x1 = x + gate * y
    return (x1, _rms(x1, g) * (1 + sc) + sh)


@jax.custom_vjp
def _swap_halves(t):
    return pltpu.roll(t, HEAD_DIM // 2, axis=1)


def _swap_fwd(t):
    return _swap_halves(t), None


def _swap_bwd(_, g):
    return (pltpu.roll(g, HEAD_DIM // 2, axis=1),)


_swap_halves.defvjp(_swap_fwd, _swap_bwd)


def _rope(t, cs, sn):
    return t * cs + _swap_halves(t) * sn


def _f_prep_norm(t, cs, sn, g):
    return (_rope(_rms(t, g), cs, sn),)


def _f_prep_plain(t, cs, sn):
    return (_rope(t, cs, sn),)


def _f_prep_scaled(t, cs, sn):
    return (_rope(t * (HEAD_DIM ** -0.5), cs, sn),)


def _log_sigmoid(x):
    return jnp.minimum(x, 0.0) - jnp.log(1.0 + jnp.exp(-jnp.abs(x)))


N_DECAY = 8


def _gla_masks(d, width):
    C, SB = GLA_CHUNK, GLA_SUB
    r = lax.broadcasted_iota(jnp.int32, (C, C), 0)
    m = lax.broadcasted_iota(jnp.int32, (C, C), 1)
    rr = lax.broadcasted_iota(jnp.int32, (C, width), 0)
    allowed = (m <= r) if d == 0 else (m >= r)
    blocks, vis = [allowed], []
    for b in range(C // SB):
        blocks.append((m < SB * b) if d == 0 else (m >= SB * (b + 1)))
        vis.append((rr < SB * (b + 1)) if d == 0 else (rr >= SB * b))
    cm = jnp.concatenate([x.astype(F32) for x in blocks] + [jnp.ones((C, C), F32)], axis=0)
    return cm, allowed, vis


def _gla_decays(la, d):
    C, SB = GLA_CHUNK, GLA_SUB
    nsb = C // SB
    cm, _, vis = _gla_masks(d, la.shape[-1])
    cums = jnp.dot(cm, la, precision=lax.Precision.HIGH, preferred_element_type=F32)
    cum, tot = cums[0:C], cums[(1 + nsb) * C:]
    refs = [cums[(1 + b) * C:(2 + b) * C] for b in range(nsb)]
    e1 = jnp.concatenate([jnp.exp(cum[b * SB:(b + 1) * SB] - refs[b][b * SB:(b + 1) * SB]) for b in range(nsb)], axis=0)
    e2 = [jnp.where(vis[b], jnp.exp(jnp.where(vis[b], refs[b] - cum, 0.0)), 0.0) for b in range(nsb)]
    return [e1] + e2 + [jnp.exp(cum), jnp.exp(tot - cum), jnp.exp(tot)]


def _f_gla_pre(ga, *per_pair):
    gab = ga.astype(BF16)
    outs = [[], []]
    for p in range(len(per_pair) // 4):
        gf, gb, bf, bb = per_pair[4 * p:4 * p + 4]
        for d, (gm, bm) in enumerate(((gf, bf), (gb, bb))):
            la = _log_sigmoid(jnp.dot(gab, gm.astype(BF16), preferred_element_type=F32) + bm) / GLA_TAU
            outs[d] += _gla_decays(la, d)
    return tuple(jnp.concatenate(o, axis=-1) for o in outs)


def _f_gated_norm(o, g, n):
    return (_rms(o, n) * _silu(g),)


def _f_first(x, g, sh, sc):
    return (x, _rms(x, g) * (1 + sc) + sh)


def _loss_grad(cf, x1, yff, gate, gfin, tgt):
    tm, T, D = cf.TM, cf.T, cf.D
    nctx = cf.LC // tm

    def lossf(x1v, yv, gt, gf, tg):
        y = _rms(x1v + gt * yv, gf)
        e = y - tg
        return 0.5 * jnp.sum(jnp.mean(e * e, axis=-1, keepdims=True), axis=0, keepdims=True)

    def body(x1_ref, y_ref, gt_ref, gf_ref, tg_ref, dx_ref, dy_ref, dgt_ref, dgf_ref, ls_ref):
        i = pl.program_id(0)
        lat = (i >= nctx).astype(F32)
        val, vjp_fn = jax.vjp(lossf, x1_ref[...], y_ref[...].astype(F32), gt_ref[...], gf_ref[...], tg_ref[...])
        dx, dy, dgt, dgf, _ = vjp_fn(jnp.ones((1, 1), F32) * lat)
        dx_ref[...] = dx
        dy_ref[...] = dy.astype(dy_ref.dtype)
        first_s = jnp.logical_or(i == 0, i == nctx)

        @pl.when(first_s)
        def _():
            dgt_ref[...] = dgt

        @pl.when(jnp.logical_not(first_s))
        def _():
            dgt_ref[...] += dgt

        @pl.when(i == 0)
        def _():
            dgf_ref[...] = dgf
            ls_ref[...] = jnp.zeros_like(ls_ref) + val * lat

        @pl.when(i != 0)
        def _():
            dgf_ref[...] += dgf
            ls_ref[...] += val * lat

    row = pl.BlockSpec((tm, D), lambda i: (i, 0))
    strm = pl.BlockSpec((None, 1, D), lambda i: (jnp.where(i >= nctx, 1, 0), 0, 0))
    one = pl.BlockSpec((1, D), lambda i: (0, 0))
    return pl.pallas_call(
        body, name="loss_grad", grid=(T // tm,), in_specs=[row, row, strm, one, row],
        out_specs=[row, row, strm, one, pl.BlockSpec((8, LANE), lambda i: (0, 0))],
        out_shape=[jax.ShapeDtypeStruct((T, D), F32), jax.ShapeDtypeStruct((T, D), BF16),
                   jax.ShapeDtypeStruct((2, 1, D), F32), jax.ShapeDtypeStruct((1, D), F32),
                   jax.ShapeDtypeStruct((8, LANE), F32)],
        compiler_params=_cp(("arbitrary",)))(x1, yff, gate, gfin, tgt)


def _att_mask(cf, is_latent, rows):
    col = lax.broadcasted_iota(jnp.int32, (rows, cf.T), 1)
    return jnp.logical_or(col < cf.LC, is_latent)


def _stack_heads(ref, G):
    return jnp.concatenate([ref[:, j * LANE:(j + 1) * LANE] for j in range(G)], axis=0)


def _att_probs(q, k, mask):
    s = lax.dot_general(q, k, (_DN['nt'], ((), ())), preferred_element_type=F32) * (HEAD_DIM ** -0.5)
    s = jnp.where(mask, s, NEG)
    e = jnp.exp(s - jnp.max(s, axis=-1, keepdims=True))
    return e / jnp.sum(e, axis=-1, keepdims=True)


def _att_fwd(cf, q, k, z):
    tq, T, G = cf.TQ, cf.T, cf.G
    vb = cf.OFF['av'] // LANE

    def body(q_ref, k_ref, v_ref, o_ref):
        mask = _att_mask(cf, pl.program_id(1) >= cf.LC // tq, tq)
        kv, vv = k_ref[...], v_ref[...].astype(BF16)
        for j in range(G):
            p = _att_probs(q_ref[:, j * LANE:(j + 1) * LANE], kv, mask)
            o_ref[:, j * LANE:(j + 1) * LANE] = jnp.dot(p.astype(BF16), vv, preferred_element_type=F32).astype(o_ref.dtype)

    return pl.pallas_call(
        body, name="att_fwd", grid=(cf.HKV, T // tq),
        in_specs=[pl.BlockSpec((tq, G * LANE), lambda g, i: (i, g)), pl.BlockSpec((T, LANE), lambda g, i: (0, g)),
                  pl.BlockSpec((T, LANE), lambda g, i: (0, vb + g))],
        out_specs=pl.BlockSpec((tq, G * LANE), lambda g, i: (i, g)),
        out_shape=jax.ShapeDtypeStruct((T, cf.HQ * LANE), BF16), compiler_params=_cp(("arbitrary", "arbitrary")))(q, k, z)


def _att_bwd(cf, q, k, z, dcat):
    tq, T, G = cf.TM, cf.T, cf.G
    vb = cf.OFF['av'] // LANE
    sc = HEAD_DIM ** -0.5

    def body(q_ref, k_ref, v_ref, do_ref, dq_ref, dk_ref, dv_ref):
        i = pl.program_id(1)
        mask = _att_mask(cf, i >= cf.LC // tq, G * tq)
        kv, vv = k_ref[...], v_ref[...].astype(BF16)
        q4, do4 = _stack_heads(q_ref, G), _stack_heads(do_ref, G)
        p = _att_probs(q4, kv, mask)
        dv = lax.dot_general(p.astype(BF16), do4, (_DN['tn'], ((), ())), preferred_element_type=F32)
        dp = lax.dot_general(do4, vv, (_DN['nt'], ((), ())), preferred_element_type=F32)
        dsb = (p * (dp - jnp.sum(dp * p, axis=-1, keepdims=True)) * sc).astype(BF16)
        dq = jnp.dot(dsb, kv, preferred_element_type=F32)
        dk = lax.dot_general(dsb, q4, (_DN['tn'], ((), ())), preferred_element_type=F32)
        for j in range(G):
            dq_ref[:, j * LANE:(j + 1) * LANE] = dq[j * tq:(j + 1) * tq]

        @pl.when(i == 0)
        def _():
            dk_ref[...] = dk
            dv_ref[...] = dv

        @pl.when(i != 0)
        def _():
            dk_ref[...] += dk
            dv_ref[...] += dv

    qs = pl.BlockSpec((tq, G * LANE), lambda g, i: (i, g))
    ks = pl.BlockSpec((T, LANE), lambda g, i: (0, g))
    return pl.pallas_call(
        body, name="att_bwd", grid=(cf.HKV, T // tq),
        in_specs=[qs, ks, pl.BlockSpec((T, LANE), lambda g, i: (0, vb + g)), qs],
        out_specs=[qs, ks, ks],
        out_shape=[jax.ShapeDtypeStruct((T, cf.HQ * LANE), F32), jax.ShapeDtypeStruct((T, cf.HKV * LANE), F32),
                   jax.ShapeDtypeStruct((T, cf.HKV * LANE), F32)],
        compiler_params=_cp(("arbitrary", "arbitrary")))(q, k, z, dcat)


def _ret_masks(cf, i, tq, lgf, lgb):
    T, LC = cf.T, cf.LC
    row = (lax.broadcasted_iota(jnp.int32, (tq, 1), 0) + i * tq)
    col = lax.broadcasted_iota(jnp.int32, (1, T), 1)

    def pb(n):
        return jnp.where(n < LC, LC - 1 - n, T + LC - 1 - n).astype(F32)

    relf = row.astype(F32) - col.astype(F32)
    relb = pb(row) - pb(col)
    rf, rb = jnp.maximum(relf, 0.0), jnp.maximum(relb, 0.0)
    mf = jnp.where(relf >= 0, jnp.exp(lgf * rf), 0.0)
    mb = jnp.where(relb >= 0, jnp.exp(lgb * rb), 0.0)
    return mf, mb, rf, rb


def _ret_fwd(cf, q, k, z, lg):
    tq, T = cf.TQ, cf.T
    vb = cf.OFF['rv'] // LANE

    def body(lg_ref, q_ref, k_ref, v_ref, o_ref):
        h, i = pl.program_id(0), pl.program_id(1)
        mf, mb, _, _ = _ret_masks(cf, i, tq, lg_ref[0, h], lg_ref[1, h])
        a = lax.dot_general(q_ref[...], k_ref[...], (_DN['nt'], ((), ())), preferred_element_type=F32)
        p = (a * (mf + mb)).astype(BF16)
        o_ref[...] = jnp.dot(p, v_ref[...].astype(BF16), preferred_element_type=F32)

    return pl.pallas_call(
        body, name="ret_fwd", grid=(cf.HR, T // tq),
        in_specs=[pl.BlockSpec(memory_space=pltpu.SMEM), pl.BlockSpec((tq, LANE), lambda h, i: (i, h)),
                  pl.BlockSpec((T, LANE), lambda h, i: (0, h)), pl.BlockSpec((T, LANE), lambda h, i: (0, vb + h))],
        out_specs=pl.BlockSpec((tq, LANE), lambda h, i: (i, h)),
        out_shape=jax.ShapeDtypeStruct((T, cf.HR * LANE), F32), compiler_params=_cp(("arbitrary", "arbitrary")))(lg, q, k, z)


def _ret_bwd(cf, q, k, z, lg, do):
    tq, T = cf.TQ, cf.T
    vb = cf.OFF['rv'] // LANE

    def body(lg_ref, q_ref, k_ref, v_ref, do_ref, dq_ref, dk_ref, dv_ref, dlg_ref):
        h, i = pl.program_id(0), pl.program_id(1)
        mf, mb, rf, rb = _ret_masks(cf, i, tq, lg_ref[0, h], lg_ref[1, h])
        qv, kv, vv = q_ref[...], k_ref[...], v_ref[...].astype(BF16)
        dob = do_ref[...].astype(BF16)
        a = lax.dot_general(qv, kv, (_DN['nt'], ((), ())), preferred_element_type=F32)
        m = mf + mb
        p = (a * m).astype(BF16)
        dv = lax.dot_general(p, dob, (_DN['tn'], ((), ())), preferred_element_type=F32)
        dp = lax.dot_general(dob, vv, (_DN['nt'], ((), ())), preferred_element_type=F32)
        da = (dp * m).astype(BF16)
        dq_ref[...] = jnp.dot(da, kv, preferred_element_type=F32)
        dk = lax.dot_general(da, qv, (_DN['tn'], ((), ())), preferred_element_type=F32)
        dm = dp * a
        dlf = jnp.sum(jnp.sum(dm * mf * rf, axis=-1, keepdims=True), axis=0, keepdims=True)
        dlb = jnp.sum(jnp.sum(dm * mb * rb, axis=-1, keepdims=True), axis=0, keepdims=True)
        rid = lax.broadcasted_iota(jnp.int32, (8, LANE), 0)
        dl = jnp.where(rid == 0, dlf, jnp.where(rid == 1, dlb, 0.0))

        @pl.when(i == 0)
        def _():
            dk_ref[...] = dk
            dv_ref[...] = dv
            dlg_ref[...] = dl

        @pl.when(i != 0)
        def _():
            dk_ref[...] += dk
            dv_ref[...] += dv
            dlg_ref[...] += dl

    qs = pl.BlockSpec((tq, LANE), lambda h, i: (i, h))
    ks = pl.BlockSpec((T, LANE), lambda h, i: (0, h))
    return pl.pallas_call(
        body, name="ret_bwd", grid=(cf.HR, T // tq),
        in_specs=[pl.BlockSpec(memory_space=pltpu.SMEM), qs, ks, pl.BlockSpec((T, LANE), lambda h, i: (0, vb + h)), qs],
        out_specs=[qs, ks, ks, pl.BlockSpec((None, 8, LANE), lambda h, i: (h, 0, 0))],
        out_shape=[jax.ShapeDtypeStruct((T, cf.HR * LANE), F32)] * 3 + [jax.ShapeDtypeStruct((cf.HR, 8, LANE), F32)],
        compiler_params=_cp(("arbitrary", "arbitrary")))(lg, q, k, z, do)


def _gla_step(q, k, v, es, st, lmask, allowed):
    C, SB = GLA_CHUNK, GLA_SUB
    nsb = C // SB
    e1, e2, e3, e4, e5 = es[0], es[1:1 + nsb], es[1 + nsb], es[2 + nsb], es[3 + nsb]
    qs = q * lmask * ((HEAD_DIM // 2) ** -0.5)
    ks = k * lmask
    qt = qs * e1
    rows = [lax.dot_general(qt[b * SB:(b + 1) * SB], ks * e2[b], (_DN['nt'], ((), ())), precision=lax.Precision.HIGH,
                            preferred_element_type=F32) for b in range(nsb)]
    att = jnp.where(allowed, jnp.concatenate(rows, axis=0), 0.0)
    o = jnp.dot(att.astype(BF16), v.astype(BF16), preferred_element_type=F32)
    o += lax.dot_general((qs * e3).astype(BF16), st.astype(BF16), (_DN['nt'], ((), ())), preferred_element_type=F32)
    kd = (ks * e4).astype(BF16)
    st_new = st * jnp.concatenate([e5, e5], axis=0) + lax.dot_general(v.astype(BF16), kd, (_DN['tn'], ((), ())), preferred_element_type=F32)
    return o, st_new


def _gla_allowed(d):
    r = lax.broadcasted_iota(jnp.int32, (GLA_CHUNK, GLA_CHUNK), 0)
    m = lax.broadcasted_iota(jnp.int32, (GLA_CHUNK, GLA_CHUNK), 1)
    return (m <= r) if d == 0 else (m >= r)


def _gla_chunk_id(cf, s, d):
    if d == 0:
        return s
    nct, nc = cf.LC // GLA_CHUNK, cf.T // GLA_CHUNK
    return jnp.where(s < nct, nct - 1 - s, nc + nct - 1 - s)


def _gla_lmask(h):
    return (lax.broadcasted_iota(jnp.int32, (1, LANE), 1) // (LANE // 2) == h).astype(F32)


_GLA_CHAINS = [(h, d) for h in range(2) for d in range(2)]


def _gla_row_specs(cf, nc, reverse):
    def rowblk(s, d):
        return _gla_chunk_id(cf, nc - 1 - s if reverse else s, d)

    def spec(width, base, d, per_pair=1):
        return pl.BlockSpec((GLA_CHUNK, width), functools.partial(lambda p, s, base, d: (rowblk(s, d), base + per_pair * p), base=base, d=d))

    def state(d):
        return pl.BlockSpec((2, None, LANE, LANE), functools.partial(lambda p, s, d: (p, rowblk(s, d), 0, 0), d=d))

    return spec, state


def _gla_fwd(cf, z, ef, eb):
    T, C = cf.T, GLA_CHUNK
    nc = T // C
    qb, kb, vb = cf.OFF['gq'] // LANE, cf.OFF['gk'] // LANE, cf.OFF['gv'] // (2 * LANE)
    spec, state = _gla_row_specs(cf, nc, False)

    def body(qf, kf, vf, e_f, qb_, kb_, vb_, e_b, of, sf, ob, sb, st_scr):
        @pl.when(pl.program_id(1) == 0)
        def _():
            st_scr[...] = jnp.zeros_like(st_scr)

        io = [(qf, kf, vf, e_f, of, sf), (qb_, kb_, vb_, e_b, ob, sb)]
        for ci, (h, d) in enumerate(_GLA_CHAINS):
            q, k, v, e, o_ref, s_ref = io[d]
            cols = slice(h * LANE, (h + 1) * LANE)
            es = [e[:, n * LANE:(n + 1) * LANE] for n in range(N_DECAY)]
            st = st_scr[ci]
            s_ref[h] = st
            o, stn = _gla_step(q[...], k[...], v[:, cols], es, st, _gla_lmask(h), _gla_allowed(d))
            st_scr[ci] = stn
            o_ref[:, cols] = o

    ins, outs = [], []
    for d in range(2):
        ins += [spec(LANE, qb, d), spec(LANE, kb, d), spec(2 * LANE, vb, d), spec(N_DECAY * LANE, 0, d)]
        outs += [spec(2 * LANE, 0, d), state(d)]
    oshape = [jax.ShapeDtypeStruct((T, cf.HG * LANE), F32), jax.ShapeDtypeStruct((cf.HG, nc, LANE, LANE), F32)]
    return pl.pallas_call(
        body, name="gla_fwd", grid=(cf.HG // 2, nc), in_specs=ins, out_specs=outs, out_shape=oshape * 2,
        scratch_shapes=[pltpu.VMEM((4, LANE, LANE), F32)],
        compiler_params=_cp(("arbitrary", "arbitrary")))(z, z, z, ef, z, z, z, eb)


def _gla_bwd(cf, z, ef, eb, sf, sb, do):
    T, C = cf.T, GLA_CHUNK
    nc = T // C
    qb, kb, vb = cf.OFF['gq'] // LANE, cf.OFF['gk'] // LANE, cf.OFF['gv'] // (2 * LANE)
    spec, state = _gla_row_specs(cf, nc, True)

    def body(*refs):
        ins = [refs[0:6], refs[6:12]]
        outs = [refs[12:16], refs[16:20]]
        dst_scr = refs[20]

        @pl.when(pl.program_id(1) == 0)
        def _():
            dst_scr[...] = jnp.zeros_like(dst_scr)

        acc = [None, None]
        for ci, (h, d) in enumerate(_GLA_CHAINS):
            q, k, v, e, s_ref, do_ref = ins[d]
            cols = slice(h * LANE, (h + 1) * LANE)
            es = [e[:, n * LANE:(n + 1) * LANE] for n in range(N_DECAY)]
            step = functools.partial(_gla_step, lmask=_gla_lmask(h), allowed=_gla_allowed(d))
            _, vjp_fn = jax.vjp(step, q[...], k[...], v[:, cols], es, s_ref[h])
            dq, dk, dv, des, dst = vjp_fn((do_ref[:, cols], dst_scr[ci]))
            dst_scr[ci] = dst
            outs[d][2][:, cols] = dv
            part = [dq, dk] + list(des)
            acc[d] = part if acc[d] is None else [a + b for a, b in zip(acc[d], part)]
        for d in range(2):
            dq_ref, dk_ref, _, de_ref = outs[d]
            dq_ref[...] = acc[d][0]
            dk_ref[...] = acc[d][1]
            for n in range(N_DECAY):
                de_ref[:, n * LANE:(n + 1) * LANE] = acc[d][2 + n]

    in_specs, out_specs = [], []
    for d in range(2):
        in_specs += [spec(LANE, qb, d), spec(LANE, kb, d), spec(2 * LANE, vb, d), spec(N_DECAY * LANE, 0, d), state(d), spec(2 * LANE, 0, d)]
        out_specs += [spec(LANE, 0, d), spec(LANE, 0, d), spec(2 * LANE, 0, d), spec(N_DECAY * LANE, 0, d)]
    npair = cf.HG // 2
    oshape = [jax.ShapeDtypeStruct((T, npair * LANE), F32), jax.ShapeDtypeStruct((T, npair * LANE), F32),
              jax.ShapeDtypeStruct((T, cf.HG * LANE), F32), jax.ShapeDtypeStruct((T, npair * N_DECAY * LANE), F32)]
    return pl.pallas_call(
        body, name="gla_bwd", grid=(npair, nc), in_specs=in_specs, out_specs=out_specs, out_shape=oshape * 2,
        scratch_shapes=[pltpu.VMEM((4, LANE, LANE), F32)],
        compiler_params=_cp(("arbitrary", "arbitrary")))(z, z, z, ef, sf, do, z, z, z, eb, sb, do)


def _conv_parts(cf, a, w_ref):
    T, LC = cf.T, cf.LC
    rid = lax.broadcasted_iota(jnp.int32, a.shape, 0)
    first = jnp.logical_or(rid == 0, rid == LC)
    last = jnp.logical_or(rid == LC - 1, rid == T - 1)
    ap = jnp.where(first, 0.0, pltpu.roll(a, 1, axis=0))
    an = jnp.where(last, 0.0, pltpu.roll(a, T - 1, axis=0))
    w0, w1, w2, b = w_ref[0:1, :], w_ref[1:2, :], w_ref[2:3, :], w_ref[3:4, :]
    ac = ap * w0 + a * w1 + an * w2 + b
    return ap, an, ac, first, last, (w0, w1, w2)


def _conv_fwd(cf, u, wb):
    T, Fd = cf.T, cf.F
    tc = _tile(Fd, 512)
    nj = Fd // tc

    def body(a_ref, v_ref, w_ref, t_ref):
        _, _, ac, _, _, _ = _conv_parts(cf, a_ref[...], w_ref)
        t_ref[...] = (_silu(ac) * v_ref[...]).astype(t_ref.dtype)

    return pl.pallas_call(
        body, name="conv_fwd", grid=(nj,),
        in_specs=[pl.BlockSpec((T, tc), lambda j: (0, j)), pl.BlockSpec((T, tc), lambda j: (0, nj + j)),
                  pl.BlockSpec((8, tc), lambda j: (0, j))],
        out_specs=pl.BlockSpec((T, tc), lambda j: (0, j)), out_shape=jax.ShapeDtypeStruct((T, Fd), BF16),
        compiler_params=_cp(("parallel",)))(u, u, wb)


def _conv_bwd(cf, u, wb, dt):
    T, Fd = cf.T, cf.F
    tc = _tile(Fd, 256)
    nj = Fd // tc

    def body(a_ref, v_ref, w_ref, dt_ref, du_ref, dw_ref):
        a, v, dtv = a_ref[...], v_ref[...], dt_ref[...].astype(F32)
        ap, an, ac, first, last, (w0, w1, w2) = _conv_parts(cf, a, w_ref)
        sg = _sigmoid(ac)
        du_ref[1] = (dtv * ac * sg).astype(du_ref.dtype)
        dac = dtv * v * (sg * (1.0 + ac * (1.0 - sg)))
        from_next = pltpu.roll(jnp.where(first, 0.0, dac), T - 1, axis=0)
        from_prev = pltpu.roll(jnp.where(last, 0.0, dac), 1, axis=0)
        du_ref[0] = (dac * w1 + from_next * w0 + from_prev * w2).astype(du_ref.dtype)
        rows = [jnp.sum(dac * ap, axis=0, keepdims=True), jnp.sum(dac * a, axis=0, keepdims=True),
                jnp.sum(dac * an, axis=0, keepdims=True), jnp.sum(dac, axis=0, keepdims=True)]
        rid = lax.broadcasted_iota(jnp.int32, (8, tc), 0)
        dw = jnp.zeros((8, tc), F32)
        for n_, rw in enumerate(rows):
            dw = jnp.where(rid == n_, rw, dw)
        dw_ref[...] = dw

    col = pl.BlockSpec((T, tc), lambda j: (0, j))
    return pl.pallas_call(
        body, name="conv_bwd", grid=(nj,),
        in_specs=[col, pl.BlockSpec((T, tc), lambda j: (0, nj + j)), pl.BlockSpec((8, tc), lambda j: (0, j)), col],
        out_specs=[pl.BlockSpec((2, T, tc), lambda j: (0, 0, j)), pl.BlockSpec((8, tc), lambda j: (0, j))],
        out_shape=[jax.ShapeDtypeStruct((2, T, Fd), BF16), jax.ShapeDtypeStruct((8, Fd), F32)],
        compiler_params=_cp(("parallel",)))(u, u, wb, dt)


def _me():
    x, y, c = lax.axis_index("x"), lax.axis_index("y"), lax.axis_index("c")
    return x, y, c, 4 * x + 2 * y + c


def _peer(x, y, c, k):
    px = 1 - x if (k >> 2) & 1 else x
    py = 1 - y if (k >> 1) & 1 else y
    pc = 1 - c if k & 1 else c
    return (px, py, pc), 4 * px + 2 * py + pc


def _rcopy(src, dst, ss, rs, tgt):
    return pltpu.make_async_remote_copy(src_ref=src, dst_ref=dst, send_sem=ss, recv_sem=rs, device_id=tgt,
                                        device_id_type=pl.DeviceIdType.MESH)


def _ag_small(name, v):
    R, Cc = v.shape

    def body(v_ref, o_ref, ssem, rsem, lsem):
        x, y, c, me = _me()
        loc = pltpu.make_async_copy(v_ref, o_ref.at[me], lsem)
        loc.start()
        sends = []
        for k in range(1, N_DEV):
            tgt, _ = _peer(x, y, c, k)
            cp = _rcopy(v_ref, o_ref.at[me], ssem.at[k - 1], rsem.at[k - 1], tgt)
            cp.start()
            sends.append(cp)
        for k in range(1, N_DEV):
            tgt, pi = _peer(x, y, c, k)
            _rcopy(v_ref, o_ref.at[pi], ssem.at[k - 1], rsem.at[k - 1], tgt).wait_recv()
        for cp in sends:
            cp.wait_send()
        loc.wait()

    vm = pl.BlockSpec(memory_space=pltpu.VMEM)
    return pl.pallas_call(
        body, name=name, in_specs=[vm], out_specs=vm, out_shape=jax.ShapeDtypeStruct((N_DEV, R, Cc), v.dtype),
        scratch_shapes=[pltpu.SemaphoreType.DMA((N_DEV - 1,)), pltpu.SemaphoreType.DMA((N_DEV - 1,)), pltpu.SemaphoreType.DMA],
        compiler_params=pltpu.CompilerParams(vmem_limit_bytes=VMEM_LIMIT))(v)


_KINDS = ['in', 'out', 'up', 'down']


def _shard_shape(cf, kind):
    D, Fd = cf.D, cf.F
    return {'in': (D, cf.NINS), 'out': (D // N_DEV, D), 'up': (D, 2 * Fd // N_DEV), 'down': (Fd // N_DEV, D)}[kind]


def _whole_shape(cf, kind):
    D, Fd = cf.D, cf.F
    return {'in': (N_DEV, D, cf.NINS), 'out': (D, D), 'up': (D, 2 * Fd), 'down': (Fd, D)}[kind]


def _part(ref, cf, kind, idx):
    r, cdim = _shard_shape(cf, kind)
    if kind == 'in':
        return ref.at[idx]
    if kind == 'up':
        return ref.at[:, pl.ds(pl.multiple_of(idx * cdim, cdim), cdim)]
    return ref.at[pl.ds(pl.multiple_of(idx * r, r), r), :]


N_BARRIER_IDS = 8


def _handshake(x, y, c):
    barrier = pltpu.get_barrier_semaphore()
    for k in range(1, N_DEV):
        pl.semaphore_signal(barrier, inc=1, device_id=_peer(x, y, c, k)[0], device_id_type=pl.DeviceIdType.MESH)
    pl.semaphore_wait(barrier, N_DEV - 1)


def _seq_kernel(body, name, seq, out_type, nsem=N_DEV - 1, num_cores=1):
    return pl.kernel(
        body, out_type=out_type, mesh=plsc.ScalarSubcoreMesh(axis_name="sq", num_cores=num_cores), name=name,
        scratch_types=[pltpu.SemaphoreType.DMA((nsem,)), pltpu.SemaphoreType.DMA((nsem,)), pltpu.SemaphoreType.DMA],
        compiler_params=pltpu.CompilerParams(collective_id=seq % N_BARRIER_IDS))


def _seq_gather(cf, kind, l, seq, src):
    rows = _shard_shape(cf, kind)[0] // 2

    def body(src_ref, land_ref, ssem, rsem, lsem):
        mine_of = pl.ds(pl.multiple_of(lax.axis_index("sq") * rows, rows), rows)
        x, y, c, me = _me()
        _handshake(x, y, c)
        sib, xn, yn, dg = (x, y, 1 - c), (1 - x, y, c), (x, 1 - y, c), (1 - x, 1 - y, c)

        def blk(dev, half=None):
            part = _part(land_ref, cf, kind, 4 * dev[0] + 2 * dev[1] + dev[2]).at[mine_of]
            return part if half is None else part.at[pl.ds(half * (rows // 2), rows // 2)]

        def copy(n, src, dst, to):
            return _rcopy(src, dst, ssem.at[n], rsem.at[n], to)

        src, mine = src_ref.at[l].at[mine_of], blk((x, y, c))
        loc = pltpu.make_async_copy(src, mine, lsem)
        loc.start()
        loc.wait()
        sends = [copy(0, src, mine, sib), copy(1, src, mine, xn), copy(2, src, mine, yn)]
        for cp in sends:
            cp.start()

        def arrived(n, got):
            copy(n, got, got, sib).wait_recv()

        def pass_on(n, part, to):
            sends.append(copy(n, part, part, to))
            sends[-1].start()

        arrived(1, blk(xn))
        pass_on(3, blk(xn, 0), yn)
        pass_on(5, blk(xn), sib)
        arrived(2, blk(yn))
        pass_on(4, blk(yn, 1), xn)
        pass_on(6, blk(yn), sib)
        arrived(3, blk(dg, 0))
        pass_on(7, blk(dg, 0), sib)
        arrived(4, blk(dg, 1))
        pass_on(8, blk(dg, 1), sib)
        other = lambda d: (d[0], d[1], 1 - c)
        for n, got in [(0, blk(other((x, y, c)))), (5, blk(other(xn))), (6, blk(other(yn))), (7, blk(other(dg), 0)), (8, blk(other(dg), 1))]:
            copy(n, got, got, sib).wait_recv()
        for cp in sends:
            cp.wait_send()

    return _seq_kernel(body, "seq_gather_%s_%d" % (kind, l), seq, jax.ShapeDtypeStruct(_whole_shape(cf, kind), BF16), nsem=9, num_cores=2)(src)


def _seq_scatter(cf, kind, l, seq, g):
    def body(g_ref, recv_ref, ssem, rsem, lsem):
        x, y, c, me = _me()
        _handshake(x, y, c)
        loc = pltpu.make_async_copy(_rs_slab(g_ref, cf, kind, me), recv_ref.at[me], lsem)
        loc.start()
        loc.wait()
        sends = []
        for k in range(1, N_DEV):
            tgt, pi = _peer(x, y, c, k)
            sends.append(_rcopy(_rs_slab(g_ref, cf, kind, pi), recv_ref.at[me], ssem.at[k - 1], rsem.at[k - 1], tgt))
            sends[-1].start()
        for k in range(1, N_DEV):
            tgt, pi = _peer(x, y, c, k)
            _rcopy(_rs_slab(g_ref, cf, kind, pi), recv_ref.at[pi], ssem.at[k - 1], rsem.at[k - 1], tgt).wait_recv()
        for cp in sends:
            cp.wait_send()

    return _seq_kernel(body, "seq_scatter_%s_%d" % (kind, l), seq, jax.ShapeDtypeStruct((N_DEV,) + _shard_shape(cf, kind), BF16))(g)


def _rs_slab(ref, cf, kind, j):
    return ref.at[j] if kind in ('in', 'up') else _part(ref, cf, kind, j)


def _adam_vals(w, g, m, v):
    m2 = ADAM_B1 * m + (1.0 - ADAM_B1) * g
    v2 = ADAM_B2 * v + (1.0 - ADAM_B2) * (g * g)
    mh = m2 / (1.0 - ADAM_B1 ** ADAM_STEP)
    vh = v2 / (1.0 - ADAM_B2 ** ADAM_STEP)
    return -ADAM_LR * (mh / (jnp.sqrt(vh) + ADAM_EPS) + ADAM_WD * w), m2, v2


def _row_tile(R, Cc, budget_elems):
    t = max(16, min(R, (budget_elems // max(Cc, 1)) // 16 * 16))
    while t > 16 and R % t:
        t -= 16
    return t if R % t == 0 else R


def _cast_bf16(name, w, after):
    Dp, R, Cc = w.shape
    tr = _row_tile(R, Cc, 1 << 20)

    def body(w_ref, after_ref, o_ref):
        o_ref[...] = w_ref[...].astype(BF16)

    spec = pl.BlockSpec((None, tr, Cc), lambda l, i: (l, i, 0))
    return pl.pallas_call(body, name=name, grid=(Dp, R // tr), in_specs=[spec, pl.BlockSpec(memory_space=pl.ANY)], out_specs=spec,
                          out_shape=jax.ShapeDtypeStruct(w.shape, BF16), compiler_params=_cp(("parallel", "parallel")))(w, after)


def _unshard_in(cf, name, g):
    D, ns, nz = cf.D, cf.NINS, cf.NZ
    tr = _tile(D, 256, 16)

    def body(g_ref, o_ref):
        for j in range(N_DEV):
            o_ref[:, ns * j:ns * (j + 1)] = g_ref[j]
        o_ref[:, N_DEV * ns:] = jnp.zeros((tr, nz - N_DEV * ns), o_ref.dtype)

    return pl.pallas_call(body, name=name, grid=(D // tr,), in_specs=[pl.BlockSpec((N_DEV, tr, ns), lambda i: (0, i, 0))],
                          out_specs=pl.BlockSpec((tr, nz), lambda i: (i, 0)), out_shape=jax.ShapeDtypeStruct((D, nz), g.dtype),
                          compiler_params=_cp(("parallel",)))(g)


def _slabs_in(cf, name, gw):
    D, ns, nz = cf.D, cf.NINS, cf.NZ
    tr = _tile(D, 256, 16)

    def body(x_ref, o_ref):
        for j in range(N_DEV):
            o_ref[j] = x_ref[:, ns * j:ns * (j + 1)]

    return pl.pallas_call(body, name=name, grid=(D // tr,), in_specs=[pl.BlockSpec((tr, nz), lambda i: (i, 0))],
                          out_specs=pl.BlockSpec((N_DEV, tr, ns), lambda i: (0, i, 0)), out_shape=jax.ShapeDtypeStruct((N_DEV, D, ns), gw.dtype),
                          compiler_params=_cp(("parallel",)))(gw)


def _sum_adam(name, recv, w, m, v, layer, prev, after):
    Dp, R, Cc = w.shape
    tr = _row_tile(R, Cc, 1 << 18)

    def body(r_ref, w_ref, m_ref, v_ref, *rest):
        g_ref, d_ref, mo_ref, vo_ref = rest[-4:]
        g = r_ref[0].astype(F32)
        for s in range(1, N_DEV):
            g = g + r_ref[s].astype(F32)
        dl, m2, v2 = _adam_vals(w_ref[...], g, m_ref[...], v_ref[...])
        g_ref[...] = g
        d_ref[...] = dl
        mo_ref[...] = m2
        vo_ref[...] = v2

    spec = pl.BlockSpec((None, tr, Cc), lambda i: (layer, i, 0))
    anyspec = pl.BlockSpec(memory_space=pl.ANY)
    extra = [after] + (list(prev) if prev is not None else [])
    aliases = {5 + n: n for n in range(4)} if prev is not None else {}
    return pl.pallas_call(body, name=name, grid=(R // tr,), in_specs=[pl.BlockSpec((N_DEV, tr, Cc), lambda i: (0, i, 0)), spec, spec, spec] + [anyspec] * len(extra),
                          out_specs=[spec] * 4, out_shape=[jax.ShapeDtypeStruct(w.shape, F32)] * 4, input_output_aliases=aliases,
                          compiler_params=_cp(("parallel",)))(recv, w, m, v, *extra)


def _adam(name, w, g, m, v):
    R, Cc = w.shape
    tr = _row_tile(R, Cc, 1 << 18)

    def body(w_ref, g_ref, m_ref, v_ref, d_ref, mo_ref, vo_ref):
        dl, m2, v2 = _adam_vals(w_ref[...], g_ref[...], m_ref[...], v_ref[...])
        d_ref[...] = dl
        mo_ref[...] = m2
        vo_ref[...] = v2

    spec = pl.BlockSpec((tr, Cc), lambda i: (i, 0))
    return pl.pallas_call(body, name=name, grid=(R // tr,), in_specs=[spec] * 4, out_specs=[spec] * 3,
                          out_shape=[jax.ShapeDtypeStruct(w.shape, F32)] * 3, compiler_params=_cp(("parallel",)))(w, g, m, v)


def _sum8(name, a):
    n, R, Cc = a.shape

    def body(a_ref, o_ref):
        s = a_ref[0]
        for k in range(1, n):
            s = s + a_ref[k]
        o_ref[...] = s

    return pl.pallas_call(body, name=name, in_specs=[pl.BlockSpec(memory_space=pltpu.VMEM)],
                          out_specs=pl.BlockSpec(memory_space=pltpu.VMEM), out_shape=jax.ShapeDtypeStruct((R, Cc), F32),
                          compiler_params=pltpu.CompilerParams(vmem_limit_bytes=VMEM_LIMIT))(a)


def _ada_fwd(cf, c9, ada_w):
    D = cf.D
    NS = ada_w.shape[-1]
    tk = _tile(D, 512)
    nk = D // tk

    def body(c_ref, w_ref, o_ref):
        kk = pl.program_id(1)
        s = _silu(c_ref[...]).astype(BF16)
        part = jnp.dot(s, w_ref[...].astype(BF16), preferred_element_type=F32)

        @pl.when(kk == 0)
        def _():
            o_ref[...] = part

        @pl.when(kk != 0)
        def _():
            o_ref[...] += part

    return pl.pallas_call(
        body, name="ada_fwd", grid=(DEPTH, nk),
        in_specs=[pl.BlockSpec((16, tk), lambda l, k: (0, k)), pl.BlockSpec((None, tk, NS), lambda l, k: (l, k, 0))],
        out_specs=pl.BlockSpec((None, 16, NS), lambda l, k: (l, 0, 0)),
        out_shape=jax.ShapeDtypeStruct((DEPTH, 16, NS), F32), compiler_params=_cp(("parallel", "arbitrary")))(c9, ada_w)


def _ada_bwd(cf, c9, ada_w, dm9):
    D = cf.D
    NS = ada_w.shape[-1]
    tk = _tile(D, 512)
    nk = D // tk

    def body(c_ref, w_ref, dm_ref, gw_ref, ds_ref):
        cv = c_ref[...]
        sg = _sigmoid(cv)
        dmb = dm_ref[...].astype(BF16)
        gw_ref[...] = lax.dot_general((cv * sg).astype(BF16), dmb, (_DN['tn'], ((), ())), preferred_element_type=F32)
        ds = lax.dot_general(dmb, w_ref[...].astype(BF16), (_DN['nt'], ((), ())), preferred_element_type=F32)
        ds_ref[...] = ds * (sg * (1.0 + cv * (1.0 - sg)))

    return pl.pallas_call(
        body, name="ada_bwd", grid=(DEPTH, nk),
        in_specs=[pl.BlockSpec((16, tk), lambda l, k: (0, k)), pl.BlockSpec((None, tk, NS), lambda l, k: (l, k, 0)),
                  pl.BlockSpec((None, 16, NS), lambda l, k: (l, 0, 0))],
        out_specs=[pl.BlockSpec((None, tk, NS), lambda l, k: (l, k, 0)), pl.BlockSpec((None, 16, tk), lambda l, k: (l, 0, k))],
        out_shape=[jax.ShapeDtypeStruct((DEPTH, D, NS), F32), jax.ShapeDtypeStruct((DEPTH, 16, D), F32)],
        compiler_params=_cp(("parallel", "parallel")))(c9, ada_w, dm9)


def _rope_tables(cf):
    L, LC = cf.L, cf.LC
    rows = L // GRID_W
    row = jnp.repeat(jnp.arange(rows, dtype=F32), GRID_W)
    col = jnp.tile(jnp.arange(GRID_W, dtype=F32), rows)
    nf = HEAD_DIM // 4
    inv = ROPE_THETA ** (-jnp.arange(nf, dtype=F32) / nf)
    ang = jnp.concatenate([row[:, None] * inv, col[:, None] * inv], axis=-1)
    cos, sin = jnp.cos(ang), jnp.sin(ang)
    cs = jnp.concatenate([jnp.ones((LC, HEAD_DIM), F32), jnp.concatenate([cos, cos], -1)], 0)
    sn = jnp.concatenate([jnp.zeros((LC, HEAD_DIM), F32), jnp.concatenate([-sin, sin], -1)], 0)
    return cs, sn


def _prep_tiles(cf, z, cs, sn, key):
    b = cf.OFF[key] // LANE
    return [(z, LANE, _col(b), True), (cs, LANE, _c0, False), (sn, LANE, _c0, False)]


_PREP = {'aq': _f_prep_norm, 'ak': _f_prep_norm, 'rq': _f_prep_plain, 'rk': _f_prep_scaled}


def _prep_fwd(cf, z, cs, sn, key, g):
    nh = cf.W[key] // LANE
    params = [(g, 'shared', True)] if g is not None else []
    return _row_fwd(cf, "prep_fwd_" + key, _PREP[key], _prep_tiles(cf, z, cs, sn, key), params,
                    [(LANE, _col(0), cf.W[key], BF16)], cf.TQ, nrep=nh)[0]


def _prep_bwd(cf, z, cs, sn, key, g, dt):
    nh = cf.W[key] // LANE
    params = [(g, 'shared', True)] if g is not None else []
    tg, pg = _row_bwd(cf, "prep_bwd_" + key, _PREP[key], _prep_tiles(cf, z, cs, sn, key), params,
                      [(dt, LANE, _col(0))], [(LANE, _col(0), cf.W[key], BF16)], cf.TQ, nrep=nh)
    return tg[0], (pg[0] if g is not None else None)


def _gate_params(cf, gup, gb):
    K = gup.shape[-1]
    gf = jnp.zeros((LANE, K), F32).at[0:GLA_RANK].set(gup[0])
    gbm = jnp.zeros((LANE, K), F32).at[GLA_RANK:2 * GLA_RANK].set(gup[1])
    out = []
    for p in range(K // LANE):
        cols = slice(p * LANE, (p + 1) * LANE)
        out += [(gf[:, cols], 'shared', True), (gbm[:, cols], 'shared', True), (gb[0:1, cols], 'shared', True), (gb[1:2, cols], 'shared', True)]
    return out


def _mix_tiles(cf, z, o, key):
    return [(o, LANE, _col(0), True), (z, LANE, _col(cf.OFF[key] // LANE), True)]


def _mid_io(cf, l, W, mod, x, y, norm2_g=None):
    tiles = [(x, cf.D, _c0, True), (y, cf.D, _c0, True)]
    n2 = W['norm2_g'][l] if norm2_g is None else norm2_g
    params = [(mod[2], 'stream', True), (n2, 'shared', True), (mod[3], 'stream', True), (mod[4], 'stream', True)]
    return tiles, params


def _launch_scatter(cf, kind, l, seq, g, nxt):
    g, nxt = lax.optimization_barrier((g, nxt))
    return _seq_scatter(cf, kind, l, seq, g), nxt


class _BigWeights:
    def __init__(self, cf, shards):
        self.cf = cf
        self.whole = {(kind, l): _seq_gather(cf, kind, l, l * len(_KINDS) + n, shards[n])
                      for l in range(DEPTH) for n, kind in enumerate(_KINDS)}
        self.w_in = {}

    def get(self, kind, l, after=None):
        cf = self.cf
        if kind != 'in':
            return self.whole[(kind, l)]
        if l not in self.w_in:
            whole, _ = lax.optimization_barrier((self.whole[(kind, l)], after))
            self.w_in[l] = _unshard_in(cf, "unshard_in_%d" % l, whole)
        return self.w_in[l]


def _layer_fwd(cf, l, W, big, mod, x, h, cs, sn):
    T, D, Fd = cf.T, cf.D, cf.F
    z = _mm("z_%d" % l, h, big.get('in', l, h), 'nn', T, cf.NZ, D, F32, tm=T, tn=768, tk=D)
    qa = _prep_fwd(cf, z, cs, sn, 'aq', W['q_norm_g'][l])
    ka = _prep_fwd(cf, z, cs, sn, 'ak', W['k_norm_g'][l])
    qr = _prep_fwd(cf, z, cs, sn, 'rq', None)
    kr = _prep_fwd(cf, z, cs, sn, 'rk', None)
    o_att = _att_fwd(cf, qa, ka, z)
    o_ret = _ret_fwd(cf, qr, kr, z, W['ret_log_decay'][l])
    gates = _gate_params(cf, W['gla_gate_up'][l], W['gla_gate_b'][l])
    ga_tile = [(z, LANE, _col(cf.OFF['ga'] // LANE), True)]
    we = (cf.HG // 2) * N_DECAY * LANE
    ef, eb = _row_fwd(cf, "gates_fwd_%d" % l, _f_gla_pre, ga_tile, gates, [(we, _c0, we, F32), (we, _c0, we, F32)], GLA_CHUNK)
    o_f, sf, o_b, sb = _gla_fwd(cf, z, ef, eb)
    o_gla = o_f + o_b
    cat_r = _row_fwd(cf, "mixr_fwd_%d" % l, _f_gated_norm, _mix_tiles(cf, z, o_ret, 'rg'), [(W['ret_norm_g'][l], 'shared', True)],
                     [(LANE, _col(0), cf.HR * LANE, BF16)], cf.TQ, nrep=cf.HR)[0]
    cat_g = _row_fwd(cf, "mixg_fwd_%d" % l, _f_gated_norm, _mix_tiles(cf, z, o_gla, 'gr'), [(W['gla_norm_g'][l], 'shared', True)],
                     [(LANE, _col(0), cf.HG * LANE, BF16)], cf.TQ, nrep=cf.HG)[0]
    cat = jnp.concatenate([o_att, cat_r, cat_g], axis=-1)
    y = _mm("y_%d" % l, cat, big.get('out', l, cat), 'nn', T, D, D, F32, tm=T, tn=512, tk=D)
    tiles, params = _mid_io(cf, l, W, mod, x, y)
    x1, h2 = _row_fwd(cf, "mid_fwd_%d" % l, _f_resid_norm_mod, tiles, params, [(D, _c0, D, F32), (D, _c0, D, BF16)], cf.TM)
    u = _mm("u_%d" % l, h2, big.get('up', l, h2), 'nn', T, 2 * Fd, D, F32, tm=T, tn=512, tk=D)
    t = _conv_fwd(cf, u, W['conv_wb'][l])
    yff = _mm("yff_%d" % l, t, big.get('down', l, t), 'nn', T, D, Fd, F32, tm=T, tn=1024, tk=512)
    return dict(x=x, h=h, z=z, qa=qa, ka=ka, qr=qr, kr=kr, ef=ef, eb=eb, sf=sf, sb=sb, o_ret=o_ret, o_gla=o_gla, cat=cat, y=y,
                x1=x1, h2=h2, u=u, t=t, yff=yff, gates=gates)


def _layer_bwd(cf, l, W, big, mod, sv, dx1, dyff, cs, sn):
    T, D, Fd = cf.T, cf.D, cf.F
    g, rs = {}, {}
    sq = 2 * len(_KINDS) + (DEPTH - 1 - l) * len(_KINDS)
    gwd = _mm("gwd_%d" % l, sv['t'], dyff, 'tn', Fd, D, T, BF16, tm=1408, tn=2048, tk=T)
    rs['down'], wb = _launch_scatter(cf, 'down', l, sq, gwd, W['conv_wb'][l])
    dt = _mm("dt_%d" % l, dyff, big.get('down', l), 'nt', T, Fd, D, BF16, tm=T, tn=1408, tk=D)
    du, g['conv_wb'] = _conv_bwd(cf, sv['u'], wb, dt)
    cu = 2 * Fd // N_DEV
    half = Fd // cu
    gwu = _mm("gwu_%d" % l, sv['h2'], du, 'tn', D, 2 * Fd, T, BF16, tm=D, tn=cu, tk=T, out_shape=(N_DEV, D, cu),
              b_spec=pl.BlockSpec((None, T, cu), lambda i, j, k: (j // half, 0, j % half)),
              out_spec=pl.BlockSpec((None, D, cu), lambda i, j, k: (j, i, 0)))
    rs['up'], n2 = _launch_scatter(cf, 'up', l, sq + 1, gwu, W['norm2_g'][l])
    dh2 = _mm("dh2_%d" % l, du, big.get('up', l), 'nt', T, D, 2 * Fd, BF16, tm=T, tn=1024, tk=cu,
              a_spec=pl.BlockSpec((None, T, cu), lambda i, j, k: (k // half, 0, k % half)))
    tiles, params = _mid_io(cf, l, W, mod, sv['x'], sv['y'], n2)
    (dx, dy), (g['m2'], g['norm2_g'], g['m3'], g['m4']) = _row_bwd(
        cf, "mid_bwd_%d" % l, _f_resid_norm_mod, tiles, params, [(dx1, D, _c0), (dh2, D, _c0)],
        [(D, _c0, D, F32), (D, _c0, D, BF16)], cf.TM)
    gwo = _mm("gwo_%d" % l, sv['cat'], dy, 'tn', D, D, T, BF16, tm=D, tn=1024, tk=T)
    rs['out'], rn = _launch_scatter(cf, 'out', l, sq + 2, gwo, W['ret_norm_g'][l])
    dcat = _mm("dcat_%d" % l, dy, big.get('out', l), 'nt', T, D, D, BF16, tm=T, tn=1024, tk=D)
    z = sv['z']
    (do_ret, drg), (g['ret_norm_g'],) = _row_bwd(
        cf, "mixr_bwd_%d" % l, _f_gated_norm, _mix_tiles(cf, z, sv['o_ret'], 'rg'), [(rn, 'shared', True)],
        [(dcat, LANE, _col(cf.HQ))], [(LANE, _col(0), cf.HR * LANE, F32), (LANE, _col(0), cf.HR * LANE, BF16)], cf.TQ, nrep=cf.HR)
    (do_gla, dgr), (g['gla_norm_g'],) = _row_bwd(
        cf, "mixg_bwd_%d" % l, _f_gated_norm, _mix_tiles(cf, z, sv['o_gla'], 'gr'), [(W['gla_norm_g'][l], 'shared', True)],
        [(dcat, LANE, _col(cf.HQ + cf.HR))], [(LANE, _col(0), cf.HG * LANE, F32), (LANE, _col(0), cf.HG * LANE, BF16)], cf.TQ, nrep=cf.HG)
    dqa, dka, dav = _att_bwd(cf, sv['qa'], sv['ka'], z, dcat)
    dqr, dkr, drv, dlg = _ret_bwd(cf, sv['qr'], sv['kr'], z, W['ret_log_decay'][l], do_ret)
    g['ret_log_decay'] = dlg[:, 0:2, 0].T
    dq_f, dk_f, dv_f, def_, dq_b, dk_b, dv_b, deb = _gla_bwd(cf, z, sv['ef'], sv['eb'], sv['sf'], sv['sb'], do_gla)
    dgq, dgk, dgv = dq_f + dq_b, dk_f + dk_b, dv_f + dv_b
    we = (cf.HG // 2) * N_DECAY * LANE
    ga_tile = [(z, LANE, _col(cf.OFF['ga'] // LANE), True)]
    (dga,), gg = _row_bwd(cf, "gates_bwd_%d" % l, _f_gla_pre, ga_tile, sv['gates'],
                          [(def_, we, _c0), (deb, we, _c0)], [(LANE, _c0, LANE, BF16)], GLA_CHUNK)
    ggf, ggb, gbf, gbb = [jnp.concatenate(gg[n::4], axis=-1) for n in range(4)]
    g['gla_gate_up'] = jnp.stack([ggf[0:GLA_RANK], ggb[GLA_RANK:2 * GLA_RANK]])
    g['gla_gate_b'] = jnp.concatenate([gbf, gbb], axis=0)
    daq, g['q_norm_g'] = _prep_bwd(cf, z, cs, sn, 'aq', W['q_norm_g'][l], dqa)
    dak, g['k_norm_g'] = _prep_bwd(cf, z, cs, sn, 'ak', W['k_norm_g'][l], dka)
    drq, _ = _prep_bwd(cf, z, cs, sn, 'rq', None, dqr)
    drk, _ = _prep_bwd(cf, z, cs, sn, 'rk', None, dkr)
    pad = jnp.zeros((T, cf.NZ - cf.OFF['ga'] - LANE), BF16)
    dz = jnp.concatenate([daq, dak, dav.astype(BF16), drq, drk, drv.astype(BF16), drg, dgq.astype(BF16), dgk.astype(BF16),
                          dgv.astype(BF16), dgr, dga, pad], axis=-1)
    gwi = _mm("gwi_%d" % l, sv['h'], dz, 'tn', D, cf.NZ, T, BF16, tm=D, tn=768, tk=T)
    rs['in'], g['norm1_g_tied'] = _launch_scatter(cf, 'in', l, sq + 3, _slabs_in(cf, "slabs_in_%d" % l, gwi), W['norm1_g'][l])
    dh =_mm("dh_%d" % l, dz, big.get('in', l), 'nt', T, D, cf.NZ, BF16, tm=T, tn=1024, tk=1792)
    g['rs'] = rs
    return dx, dh, g


_WEIGHTS = ['c_ctx', 'ada_w', 'ada_b', 'norm1_g', 'w_in', 'q_norm_g', 'k_norm_g', 'ret_log_decay', 'ret_norm_g',
            'gla_gate_up', 'gla_gate_b', 'gla_norm_g', 'w_out', 'norm2_g', 'w_up', 'conv_w', 'conv_b', 'w_down', 'final_norm_g']
_BIG = ['w_in', 'w_out', 'w_up', 'w_down']
_SMALL = [n for n in _WEIGHTS if n not in _BIG and n != 'ada_w']
_COL_SHARDED = ['gla_gate_up', 'gla_gate_b', 'conv_w']


def _pack(arrs):
    rows = []
    for a in arrs:
        flat = a.reshape(-1)
        n = flat.shape[0]
        rows.append(jnp.pad(flat, (0, -n % LANE)).reshape(-1, LANE))
    packed = jnp.concatenate(rows, axis=0)
    return jnp.pad(packed, ((0, -packed.shape[0] % 8), (0, 0)))


def _unpack(packed, shapes):
    lead = packed.shape[:-2]
    out, r = [], 0
    for s in shapes:
        n = int(np.prod(s))
        nr = -(-n // LANE)
        out.append(packed[..., r:r + nr, :].reshape(lead + (nr * LANE,))[..., :n].reshape(lead + tuple(s)))
        r += nr
    return out


def _unshard_last(a):
    return jnp.moveaxis(a, 0, -2).reshape(a.shape[1:-1] + (N_DEV * a.shape[-1],))


def _step(cf, x, c, ctx, loss_target, w, m, v):
    T, D, Fd, L, LC = cf.T, cf.D, cf.F, cf.L, cf.LC
    _, _, _, me = _me()
    NS = w['ada_w'].shape[-1]

    c_all = _ag_small("ag_c", jnp.pad(c, ((0, 7), (0, 0))))[:, 0, :]
    c9 = jnp.concatenate([c_all, w['c_ctx'][None], jnp.zeros((7, D), F32)], axis=0)
    pm = _ada_fwd(cf, c9, w['ada_w'])
    pm_all = _ag_small("ag_mod", pm.reshape(DEPTH * 16, NS)).reshape(N_DEV, DEPTH, 16, NS)
    mod_all = _unshard_last(pm_all) + w['ada_b'][:, None, :]
    mod_own = lax.dynamic_index_in_dim(mod_all, me, axis=1, keepdims=False)
    mods = []
    for l in range(DEPTH):
        mods.append([jnp.stack([mod_all[l, 8, k * D:(k + 1) * D], mod_own[l, k * D:(k + 1) * D]])[:, None, :] for k in range(N_MOD)])

    shard_shapes = [w[n].shape for n in _COL_SHARDED]
    got = _ag_small("ag_smallw", _pack([w[n] for n in _COL_SHARDED]))
    full = dict(zip(_COL_SHARDED, [_unshard_last(a) for a in _unpack(got, shard_shapes)]))

    small_done = full['conv_w'] + mod_all[0, 0, 0]
    big = _BigWeights(cf, [_cast_bf16("cast_" + n, w[n], c if n == 'w_in' else small_done) for n in _BIG])
    conv_wb = jnp.concatenate([full['conv_w'], w['conv_b'][:, None, :], jnp.zeros((DEPTH, 4, Fd), F32)], axis=1)
    W = dict(conv_wb=conv_wb, gla_gate_up=full['gla_gate_up'], gla_gate_b=full['gla_gate_b'], ret_log_decay=w['ret_log_decay'])
    for n in ['q_norm_g', 'k_norm_g', 'ret_norm_g', 'gla_norm_g', 'norm1_g', 'norm2_g']:
        W[n] = w[n][:, None, :]

    cs, sn = _rope_tables(cf)
    x0 = jnp.concatenate([ctx[0], x[0]], axis=0)
    pre_tiles = [(x0, D, _c0, True)]

    def pre_params(n1):
        return [(n1, 'shared', True), (mods[0][0], 'stream', True), (mods[0][1], 'stream', True)]

    def tr_params(n1):
        return [(mods[0][5], 'stream', True), (n1, 'shared', True), (mods[1][0], 'stream', True), (mods[1][1], 'stream', True)]

    h0 = _row_fwd(cf, "pre_fwd", _f_norm_mod, pre_tiles, pre_params(W['norm1_g'][0]), [(D, _c0, D, BF16)], cf.TM)[0]
    sv0 = _layer_fwd(cf, 0, W, big, mods[0], x0, h0, cs, sn)
    tr_tiles = [(sv0['x1'], D, _c0, True), (sv0['yff'], D, _c0, True)]
    xb, hb = _row_fwd(cf, "tr_fwd", _f_resid_norm_mod, tr_tiles, tr_params(W['norm1_g'][1]), [(D, _c0, D, F32), (D, _c0, D, BF16)], cf.TM)
    sv1 = _layer_fwd(cf, 1, W, big, mods[1], xb, hb, cs, sn)
    tgt = jnp.concatenate([jnp.zeros((LC, D), F32), loss_target[0]], axis=0)
    dx1, dyff, dm5_1, g_final, ls = _loss_grad(cf, sv1['x1'], sv1['yff'], mods[1][5], w['final_norm_g'][None], tgt)
    loss = lax.psum(ls[0, 0], ("x", "y", "c"))

    dxb, dhb, g1 = _layer_bwd(cf, 1, W, big, mods[1], sv1, dx1, dyff, cs, sn)
    (dx1_0, dyff_0), (dm5_0, gn1_1, dm0_1, dm1_1) = _row_bwd(
        cf, "tr_bwd", _f_resid_norm_mod, tr_tiles, tr_params(g1['norm1_g_tied']), [(dxb, D, _c0), (dhb, D, _c0)],
        [(D, _c0, D, F32), (D, _c0, D, BF16)], cf.TM)
    dx0, dh0, g0 = _layer_bwd(cf, 0, W, big, mods[0], sv0, dx1_0, dyff_0, cs, sn)
    (dxa,), (gn1_0, dm0_0, dm1_0) = _row_bwd(cf, "pre_bwd", _f_first, pre_tiles, pre_params(g0['norm1_g_tied']), [(dx0, D, _c0), (dh0, D, _c0)],
                                            [(D, _c0, D, F32)], cf.TM)
    grad_x = dxa[LC:][None]

    dmod = jnp.stack([jnp.concatenate([dm0_0, dm1_0, g0['m2'], g0['m3'], g0['m4'], dm5_0], axis=-1)[:, 0],
                      jnp.concatenate([dm0_1, dm1_1, g1['m2'], g1['m3'], g1['m4'], dm5_1], axis=-1)[:, 0]])
    dm_all = _ag_small("ag_dmod", jnp.pad(dmod.reshape(2 * DEPTH, N_MOD * D), ((0, 8 - 2 * DEPTH), (0, 0))))
    dm_all = dm_all[:, :2 * DEPTH].reshape(N_DEV, DEPTH, 2, N_MOD * D)
    dctx = _sum8("sum_dmodc", jnp.pad(dm_all[:, :, 0], ((0, 0), (0, 8 - DEPTH), (0, 0))))[:DEPTH]
    dm9 = jnp.concatenate([jnp.moveaxis(dm_all[:, :, 1], 0, 1), dctx[:, None]], axis=1)
    g_ada_b = _sum8("sum_adab", jnp.pad(jnp.moveaxis(dm9, 1, 0), ((0, 0), (0, 8 - DEPTH), (0, 0))))[:DEPTH]
    dm9s = lax.dynamic_slice_in_dim(jnp.pad(dm9, ((0, 0), (0, 7), (0, 0))), me * NS, NS, axis=2)
    g_ada_w, dsil = _ada_bwd(cf, c9, w['ada_w'], dm9s)
    g_cctx_part = dsil[0, 8]
    for l in range(1, DEPTH):
        g_cctx_part = g_cctx_part + dsil[l, 8]

    def both(key):
        return jnp.stack([g0[key], g1[key]])

    gsmall = dict(c_ctx=g_cctx_part, norm1_g=jnp.stack([gn1_0[0], gn1_1[0]]), q_norm_g=both('q_norm_g')[:, 0],
                  k_norm_g=both('k_norm_g')[:, 0], ret_log_decay=both('ret_log_decay'), ret_norm_g=both('ret_norm_g')[:, 0],
                  gla_gate_up=both('gla_gate_up'), gla_gate_b=both('gla_gate_b'), gla_norm_g=both('gla_norm_g')[:, 0],
                  norm2_g=both('norm2_g')[:, 0], conv_w=both('conv_wb')[:, 0:3], conv_b=both('conv_wb')[:, 3], final_norm_g=g_final[0])
    snames = [n for n in _SMALL if n != 'ada_b']
    sshapes = [gsmall[n].shape for n in snames]
    gs_all = _ag_small("ag_gsmall", _pack([gsmall[n] for n in snames]))
    gs = dict(zip(snames, _unpack(_sum8("sum_gsmall", gs_all), sshapes)))
    gs['ada_b'] = g_ada_b
    for n in _COL_SHARDED:
        ns_ = w[n].shape[-1]
        gs[n] = lax.dynamic_slice_in_dim(gs[n], me * ns_, ns_, axis=gs[n].ndim - 1)

    out_g, out_d, out_m, out_v = {}, {}, {}, {}

    after, done = gs_all, {}
    for l, gl in ((1, g1), (0, g0)):
        for kind in reversed(_KINDS):
            n = 'w_' + kind
            done[n] = _sum_adam("adam_%s_%d" % (n, l), gl['rs'][kind], w[n], m[n], v[n], l, done.get(n), after)
            after = done[n][0]
    for n in _BIG:
        out_g[n], out_d[n], out_m[n], out_v[n] = done[n]
    aw = [a.reshape(DEPTH * D, NS) for a in (w['ada_w'], g_ada_w, m['ada_w'], v['ada_w'])]
    out_g['ada_w'] = g_ada_w
    out_d['ada_w'], out_m['ada_w'], out_v['ada_w'] = [a.reshape(DEPTH, D, NS) for a in _adam("adam_ada_w", *aw)]
    shp = [w[n].shape for n in _SMALL]
    packed = [_pack([src[n] for n in _SMALL]) for src in (w, gs, m, v)]
    res = _adam("adam_small", *packed)
    for dst, pk in zip((out_d, out_m, out_v), res):
        dst.update(zip(_SMALL, _unpack(pk, shp)))
    out_g.update({n: gs[n] for n in _SMALL})
    return (loss, grad_x, *[out_g[n] for n in _WEIGHTS], *[out_d[n] for n in _WEIGHTS], *[out_m[n] for n in _WEIGHTS],
            *[out_v[n] for n in _WEIGHTS])


def kernel(x, c, ctx, c_ctx, ada_w, ada_b, norm1_g, w_in, q_norm_g, k_norm_g, ret_log_decay, ret_norm_g, gla_gate_up, gla_gate_b, gla_norm_g, w_out, norm2_g, w_up, conv_w, conv_b, w_down, final_norm_g, loss_target, m_c_ctx, m_ada_w, m_ada_b, m_norm1_g, m_w_in, m_q_norm_g, m_k_norm_g, m_ret_log_decay, m_ret_norm_g, m_gla_gate_up, m_gla_gate_b, m_gla_norm_g, m_w_out, m_norm2_g, m_w_up, m_conv_w, m_conv_b, m_w_down, m_final_norm_g, v_c_ctx, v_ada_w, v_ada_b, v_norm1_g, v_w_in, v_q_norm_g, v_k_norm_g, v_ret_log_decay, v_ret_norm_g, v_gla_gate_up, v_gla_gate_b, v_gla_norm_g, v_w_out, v_norm2_g, v_w_up, v_conv_w, v_conv_b, v_w_down, v_final_norm_g):
    w = dict(c_ctx=c_ctx, ada_w=ada_w, ada_b=ada_b, norm1_g=norm1_g, w_in=w_in, q_norm_g=q_norm_g, k_norm_g=k_norm_g,
             ret_log_decay=ret_log_decay, ret_norm_g=ret_norm_g, gla_gate_up=gla_gate_up, gla_gate_b=gla_gate_b,
             gla_norm_g=gla_norm_g, w_out=w_out, norm2_g=norm2_g, w_up=w_up, conv_w=conv_w, conv_b=conv_b, w_down=w_down,
             final_norm_g=final_norm_g)
    m = dict(c_ctx=m_c_ctx, ada_w=m_ada_w, ada_b=m_ada_b, norm1_g=m_norm1_g, w_in=m_w_in, q_norm_g=m_q_norm_g,
             k_norm_g=m_k_norm_g, ret_log_decay=m_ret_log_decay, ret_norm_g=m_ret_norm_g, gla_gate_up=m_gla_gate_up,
             gla_gate_b=m_gla_gate_b, gla_norm_g=m_gla_norm_g, w_out=m_w_out, norm2_g=m_norm2_g, w_up=m_w_up,
             conv_w=m_conv_w, conv_b=m_conv_b, w_down=m_w_down, final_norm_g=m_final_norm_g)
    v = dict(c_ctx=v_c_ctx, ada_w=v_ada_w, ada_b=v_ada_b, norm1_g=v_norm1_g, w_in=v_w_in, q_norm_g=v_q_norm_g,
             k_norm_g=v_k_norm_g, ret_log_decay=v_ret_log_decay, ret_norm_g=v_ret_norm_g, gla_gate_up=v_gla_gate_up,
             gla_gate_b=v_gla_gate_b, gla_norm_g=v_gla_norm_g, w_out=v_w_out, norm2_g=v_norm2_g, w_up=v_w_up,
             conv_w=v_conv_w, conv_b=v_conv_b, w_down=v_w_down, final_norm_g=v_final_norm_g)
    return _step(_cfg(), x, c, ctx, loss_target, w, m, v)
```

```python
import functools
import math
import types

import jax
import jax.numpy as jnp
import numpy as np
from jax import lax
from jax.experimental import pallas as pl
from jax.experimental.pallas import tpu as pltpu
from jax.experimental.pallas import tpu_sc as plsc

F32 = jnp.float32
BF16 = jnp.bfloat16

D_MODEL = 2048
SEQ = 2048
CTX_LEN = 256
GRID_W = 64
D_FF = 5632
DEPTH = 2
N_DEV = 8
HEAD_DIM = 128
ROPE_THETA = 10000.0
GLA_TAU = 16.0
GLA_RANK = 16
GLA_CHUNK = 64
GLA_SUB = 16
EPS = 1e-6
N_MOD = 6
ADAM_LR = 0.001
ADAM_B1 = 0.9
ADAM_B2 = 0.999
ADAM_EPS = 1e-08
ADAM_WD = 0.01
ADAM_STEP = 10
LANE = 128
VMEM_LIMIT = 56 * 1024 * 1024
NEG = -1e30


def _cfg():
    d = types.SimpleNamespace()
    d.D, d.L, d.LC, d.F = D_MODEL, SEQ, CTX_LEN, D_FF
    d.T = d.L + d.LC
    nm = d.D // HEAD_DIM
    d.HQ, d.HKV, d.HR, d.HG = nm // 2, nm // 8, nm // 4, nm // 4
    d.G = d.HQ // d.HKV
    w = dict(aq=d.HQ * 128, ak=d.HKV * 128, av=d.HKV * 128, rq=d.HR * 128, rk=d.HR * 128, rv=d.HR * 128,
             rg=d.HR * 128, gq=d.HG * 64, gk=d.HG * 64, gv=d.HG * 128, gr=d.HG * 128, ga=2 * GLA_RANK)
    off, o = {}, 0
    for k, v in w.items():
        off[k] = o
        o += v
    d.W, d.OFF, d.NIN = w, off, o
    d.NZ = -(-(off['ga'] + LANE) // 256) * 256
    d.NINS = d.NIN // N_DEV
    d.TM = math.gcd(d.LC, 128)
    d.TQ = math.gcd(d.LC, 256)
    return d


def _cp(sem=None):
    return pltpu.CompilerParams(dimension_semantics=sem, vmem_limit_bytes=VMEM_LIMIT)


def _tile(n, target, mult=LANE):
    t = min(n, target)
    t -= t % mult
    while t > mult and n % t:
        t -= mult
    return t if t > 0 and n % t == 0 else n


_DN = {'nn': ((1,), (0,)), 'nt': ((1,), (1,)), 'tn': ((0,), (0,))}


def _mm(name, a, b, kind, M, N, K, out_dtype, tm=768, tn=768, tk=1024, a_spec=None, b_spec=None,
        out_shape=None, out_spec=None):
    tm, tn = _tile(M, tm, 128), _tile(N, tn, 128)
    tk = _tile(K, tk, 128)
    nk = K // tk

    def dot(a_ref, b_ref):
        return lax.dot_general(a_ref[...].astype(BF16), b_ref[...].astype(BF16), (_DN[kind], ((), ())), preferred_element_type=F32)

    def body_one(a_ref, b_ref, o_ref):
        o_ref[...] = dot(a_ref, b_ref).astype(o_ref.dtype)

    def body(a_ref, b_ref, o_ref, acc):
        kk = pl.program_id(2)

        @pl.when(kk == 0)
        def _():
            acc[...] = jnp.zeros_like(acc)

        acc[...] += dot(a_ref, b_ref)

        @pl.when(kk == nk - 1)
        def _():
            o_ref[...] = acc[...].astype(o_ref.dtype)

    if a_spec is None:
        a_spec = pl.BlockSpec((tk, tm), lambda i, j, k: (k, i)) if kind == 'tn' else pl.BlockSpec((tm, tk), lambda i, j, k: (i, k))
    if b_spec is None:
        b_spec = pl.BlockSpec((tn, tk), lambda i, j, k: (j, k)) if kind == 'nt' else pl.BlockSpec((tk, tn), lambda i, j, k: (k, j))
    if out_spec is None:
        out_spec = pl.BlockSpec((tm, tn), lambda i, j, k: (i, j))
        out_shape = (M, N)
    return pl.pallas_call(
        body_one if nk == 1 else body, name=name, grid=(M // tm, N // tn, nk), in_specs=[a_spec, b_spec], out_specs=out_spec,
        out_shape=jax.ShapeDtypeStruct(out_shape, out_dtype), scratch_shapes=[] if nk == 1 else [pltpu.VMEM((tm, tn), F32)],
        compiler_params=_cp(("parallel", "parallel", "arbitrary")))(a, b)


def _tile_spec(tm, w, colf, nrep):
    if hasattr(colf, 'base'):
        assert colf.base % nrep == 0
        return pl.BlockSpec((tm, w * nrep), functools.partial(lambda i, b: (i, b), b=colf.base // nrep))
    return pl.BlockSpec((tm, w), lambda i: (i, 0))


def _head_cols(colf, w, r):
    return slice(r * w, (r + 1) * w) if hasattr(colf, 'base') else slice(None)


def _row_specs(cf, tm, tiles, params, nrep):
    nctx = cf.LC // tm
    specs = [_tile_spec(tm, w, colf, nrep) for arr, w, colf, _ in tiles]
    for arr, kind, _ in params:
        nd = arr.ndim
        if kind == 'shared':
            specs.append(pl.BlockSpec(arr.shape, functools.partial(lambda i, nd: (0,) * nd, nd=nd)))
        else:
            specs.append(pl.BlockSpec((None,) + arr.shape[1:],
                                      functools.partial(lambda i, nd, nctx: (jnp.where(i >= nctx, 1, 0),) + (0,) * (nd - 1), nd=nd, nctx=nctx)))
    return specs


def _row_fwd(cf, name, f, tiles, params, outs, tm, nrep=1):
    nt, npar = len(tiles), len(params)

    def body(*refs):
        pv = [r[...] for r in refs[nt:nt + npar]]
        for r in range(nrep):
            tv = [x[:, _head_cols(t[2], t[1], r)].astype(F32) for x, t in zip(refs[:nt], tiles)]
            res = f(*tv, *pv)
            for o, v, spec in zip(refs[nt + npar:], res, outs):
                o[:, _head_cols(spec[1], spec[0], r)] = v.astype(o.dtype)

    out_specs = [_tile_spec(tm, w, colf, nrep) for w, colf, _, _ in outs]
    out_shape = [jax.ShapeDtypeStruct((cf.T, tw), dt) for _, _, tw, dt in outs]
    return pl.pallas_call(
        body, name=name, grid=(cf.T // tm,), in_specs=_row_specs(cf, tm, tiles, params, nrep), out_specs=out_specs,
        out_shape=out_shape, compiler_params=_cp(("arbitrary",)))(*[t[0] for t in tiles], *[p[0] for p in params])


def _row_bwd(cf, name, f, tiles, params, cts, tgrads, tm, nrep=1):
    nt, npar, nc = len(tiles), len(params), len(cts)
    tdiff = [k for k, t in enumerate(tiles) if t[3]]
    pdiff = [k for k, p in enumerate(params) if p[2]]
    nctx = cf.LC // tm

    def body(*refs):
        i = pl.program_id(0)
        pv = [x[...] for x in refs[nt:nt + npar]]
        outs = refs[nt + npar + nc:]
        psum = None
        for r in range(nrep):
            tv = [x[:, _head_cols(t[2], t[1], r)].astype(F32) for x, t in zip(refs[:nt], tiles)]
            cv = tuple(x[:, _head_cols(c[2], c[1], r)].astype(F32) for x, c in zip(refs[nt + npar:nt + npar + nc], cts))

            def g(*diff, tv=tv):
                tv2, pv2 = list(tv), list(pv)
                for k, v in zip(tdiff, diff[:len(tdiff)]):
                    tv2[k] = v
                for k, v in zip(pdiff, diff[len(tdiff):]):
                    pv2[k] = v
                return tuple(f(*tv2, *pv2))

            _, vjp_fn = jax.vjp(g, *[tv[k] for k in tdiff], *[pv[k] for k in pdiff])
            grads = vjp_fn(cv)
            for o, gv, spec in zip(outs[:len(tdiff)], grads[:len(tdiff)], tgrads):
                o[:, _head_cols(spec[1], spec[0], r)] = gv.astype(o.dtype)
            pg = grads[len(tdiff):]
            psum = list(pg) if psum is None else [a + b for a, b in zip(psum, pg)]
        for n_, (o, gv) in enumerate(zip(outs[len(tdiff):], psum)):
            first = (i == 0) if params[pdiff[n_]][1] == 'shared' else jnp.logical_or(i == 0, i == nctx)

            @pl.when(first)
            def _():
                o[...] = gv

            @pl.when(jnp.logical_not(first))
            def _():
                o[...] += gv

    in_specs = _row_specs(cf, tm, tiles, params, nrep)
    in_specs += [_tile_spec(tm, w, colf, nrep) for _, w, colf in cts]
    out_specs = [_tile_spec(tm, w, colf, nrep) for w, colf, _, _ in tgrads]
    out_shape = [jax.ShapeDtypeStruct((cf.T, tw), dt) for _, _, tw, dt in tgrads]
    out_specs += _row_specs(cf, tm, [], [params[k] for k in pdiff], nrep)
    out_shape += [jax.ShapeDtypeStruct(params[k][0].shape, F32) for k in pdiff]
    res = pl.pallas_call(
        body, name=name, grid=(cf.T // tm,), in_specs=in_specs, out_specs=out_specs, out_shape=out_shape,
        compiler_params=_cp(("arbitrary",)))(*[t[0] for t in tiles], *[p[0] for p in params], *[c[0] for c in cts])
    return res[:len(tdiff)], res[len(tdiff):]


def _c0(r):
    return 0


def _col(base):
    def col(r):
        return base + r
    col.base = base
    return col


def _rms(x, g):
    return x * lax.rsqrt(jnp.mean(x * x, axis=-1, keepdims=True) + EPS) * g


def _sigmoid(x):
    return 1.0 / (1.0 + jnp.exp(-x))


def _silu(x):
    return x * _sigmoid(x)


def _f_norm_mod(x, g, sh, sc):
    return (_rms(x, g) * (1 + sc) + sh,)


def _f_resid_norm_mod(x, y, gate, g, sh, sc):
    x1 = x + gate * y
    return (x1, _rms(x1, g) * (1 + sc) + sh)


@jax.custom_vjp
def _swap_halves(t):
    return pltpu.roll(t, HEAD_DIM // 2, axis=1)


def _swap_fwd(t):
    return _swap_halves(t), None


def _swap_bwd(_, g):
    return (pltpu.roll(g, HEAD_DIM // 2, axis=1),)


_swap_halves.defvjp(_swap_fwd, _swap_bwd)


def _rope(t, cs, sn):
    return t * cs + _swap_halves(t) * sn


def _f_prep_norm(t, cs, sn, g):
    return (_rope(_rms(t, g), cs, sn),)


def _f_prep_plain(t, cs, sn):
    return (_rope(t, cs, sn),)


def _f_prep_scaled(t, cs, sn):
    return (_rope(t * (HEAD_DIM ** -0.5), cs, sn),)


def _log_sigmoid(x):
    return jnp.minimum(x, 0.0) - jnp.log(1.0 + jnp.exp(-jnp.abs(x)))


N_DECAY = 8


def _gla_masks(d, width):
    C, SB = GLA_CHUNK, GLA_SUB
    r = lax.broadcasted_iota(jnp.int32, (C, C), 0)
    m = lax.broadcasted_iota(jnp.int32, (C, C), 1)
    rr = lax.broadcasted_iota(jnp.int32, (C, width), 0)
    allowed = (m <= r) if d == 0 else (m >= r)
    blocks, vis = [allowed], []
    for b in range(C // SB):
        blocks.append((m < SB * b) if d == 0 else (m >= SB * (b + 1)))
        vis.append((rr < SB * (b + 1)) if d == 0 else (rr >= SB * b))
    cm = jnp.concatenate([x.astype(F32) for x in blocks] + [jnp.ones((C, C), F32)], axis=0)
    return cm, allowed, vis


def _gla_decays(la, d):
    C, SB = GLA_CHUNK, GLA_SUB
    nsb = C // SB
    cm, _, vis = _gla_masks(d, la.shape[-1])
    cums = jnp.dot(cm, la, precision=lax.Precision.HIGH, preferred_element_type=F32)
    cum, tot = cums[0:C], cums[(1 + nsb) * C:]
    refs = [cums[(1 + b) * C:(2 + b) * C] for b in range(nsb)]
    e1 = jnp.concatenate([jnp.exp(cum[b * SB:(b + 1) * SB] - refs[b][b * SB:(b + 1) * SB]) for b in range(nsb)], axis=0)
    e2 = [jnp.where(vis[b], jnp.exp(jnp.where(vis[b], refs[b] - cum, 0.0)), 0.0) for b in range(nsb)]
    return [e1] + e2 + [jnp.exp(cum), jnp.exp(tot - cum), jnp.exp(tot)]


def _f_gla_pre(ga, *per_pair):
    gab = ga.astype(BF16)
    outs = [[], []]
    for p in range(len(per_pair) // 4):
        gf, gb, bf, bb = per_pair[4 * p:4 * p + 4]
        for d, (gm, bm) in enumerate(((gf, bf), (gb, bb))):
            la = _log_sigmoid(jnp.dot(gab, gm.astype(BF16), preferred_element_type=F32) + bm) / GLA_TAU
            outs[d] += _gla_decays(la, d)
    return tuple(jnp.concatenate(o, axis=-1) for o in outs)


def _f_gated_norm(o, g, n):
    return (_rms(o, n) * _silu(g),)


def _f_first(x, g, sh, sc):
    return (x, _rms(x, g) * (1 + sc) + sh)


def _loss_grad(cf, x1, yff, gate, gfin, tgt):
    tm, T, D = cf.TM, cf.T, cf.D
    nctx = cf.LC // tm

    def lossf(x1v, yv, gt, gf, tg):
        y = _rms(x1v + gt * yv, gf)
        e = y - tg
        return 0.5 * jnp.sum(jnp.mean(e * e, axis=-1, keepdims=True), axis=0, keepdims=True)

    def body(x1_ref, y_ref, gt_ref, gf_ref, tg_ref, dx_ref, dy_ref, dgt_ref, dgf_ref, ls_ref):
        i = pl.program_id(0)
        lat = (i >= nctx).astype(F32)
        val, vjp_fn = jax.vjp(lossf, x1_ref[...], y_ref[...].astype(F32), gt_ref[...], gf_ref[...], tg_ref[...])
        dx, dy, dgt, dgf, _ = vjp_fn(jnp.ones((1, 1), F32) * lat)
        dx_ref[...] = dx
        dy_ref[...] = dy.astype(dy_ref.dtype)
        first_s = jnp.logical_or(i == 0, i == nctx)

        @pl.when(first_s)
        def _():
            dgt_ref[...] = dgt

        @pl.when(jnp.logical_not(first_s))
        def _():
            dgt_ref[...] += dgt

        @pl.when(i == 0)
        def _():
            dgf_ref[...] = dgf
            ls_ref[...] = jnp.zeros_like(ls_ref) + val * lat

        @pl.when(i != 0)
        def _():
            dgf_ref[...] += dgf
            ls_ref[...] += val * lat

    row = pl.BlockSpec((tm, D), lambda i: (i, 0))
    strm = pl.BlockSpec((None, 1, D), lambda i: (jnp.where(i >= nctx, 1, 0), 0, 0))
    one = pl.BlockSpec((1, D), lambda i: (0, 0))
    return pl.pallas_call(
        body, name="loss_grad", grid=(T // tm,), in_specs=[row, row, strm, one, row],
        out_specs=[row, row, strm, one, pl.BlockSpec((8, LANE), lambda i: (0, 0))],
        out_shape=[jax.ShapeDtypeStruct((T, D), F32), jax.ShapeDtypeStruct((T, D), BF16),
                   jax.ShapeDtypeStruct((2, 1, D), F32), jax.ShapeDtypeStruct((1, D), F32),
                   jax.ShapeDtypeStruct((8, LANE), F32)],
        compiler_params=_cp(("arbitrary",)))(x1, yff, gate, gfin, tgt)


def _att_mask(cf, is_latent, rows):
    col = lax.broadcasted_iota(jnp.int32, (rows, cf.T), 1)
    return jnp.logical_or(col < cf.LC, is_latent)


def _stack_heads(ref, G):
    return jnp.concatenate([ref[:, j * LANE:(j + 1) * LANE] for j in range(G)], axis=0)


def _att_probs(q, k, mask):
    s = lax.dot_general(q, k, (_DN['nt'], ((), ())), preferred_element_type=F32) * (HEAD_DIM ** -0.5)
    s = jnp.where(mask, s, NEG)
    e = jnp.exp(s - jnp.max(s, axis=-1, keepdims=True))
    return e / jnp.sum(e, axis=-1, keepdims=True)


def _att_fwd(cf, q, k, z):
    tq, T, G = cf.TQ, cf.T, cf.G
    vb = cf.OFF['av'] // LANE

    def body(q_ref, k_ref, v_ref, o_ref):
        mask = _att_mask(cf, pl.program_id(1) >= cf.LC // tq, tq)
        kv, vv = k_ref[...], v_ref[...].astype(BF16)
        for j in range(G):
            p = _att_probs(q_ref[:, j * LANE:(j + 1) * LANE], kv, mask)
            o_ref[:, j * LANE:(j + 1) * LANE] = jnp.dot(p.astype(BF16), vv, preferred_element_type=F32).astype(o_ref.dtype)

    return pl.pallas_call(
        body, name="att_fwd", grid=(cf.HKV, T // tq),
        in_specs=[pl.BlockSpec((tq, G * LANE), lambda g, i: (i, g)), pl.BlockSpec((T, LANE), lambda g, i: (0, g)),
                  pl.BlockSpec((T, LANE), lambda g, i: (0, vb + g))],
        out_specs=pl.BlockSpec((tq, G * LANE), lambda g, i: (i, g)),
        out_shape=jax.ShapeDtypeStruct((T, cf.HQ * LANE), BF16), compiler_params=_cp(("arbitrary", "arbitrary")))(q, k, z)


def _att_bwd(cf, q, k, z, dcat):
    tq, T, G = cf.TM, cf.T, cf.G
    vb = cf.OFF['av'] // LANE
    sc = HEAD_DIM ** -0.5

    def body(q_ref, k_ref, v_ref, do_ref, dq_ref, dk_ref, dv_ref):
        i = pl.program_id(1)
        mask = _att_mask(cf, i >= cf.LC // tq, G * tq)
        kv, vv = k_ref[...], v_ref[...].astype(BF16)
        q4, do4 = _stack_heads(q_ref, G), _stack_heads(do_ref, G)
        p = _att_probs(q4, kv, mask)
        dv = lax.dot_general(p.astype(BF16), do4, (_DN['tn'], ((), ())), preferred_element_type=F32)
        dp = lax.dot_general(do4, vv, (_DN['nt'], ((), ())), preferred_element_type=F32)
        dsb = (p * (dp - jnp.sum(dp * p, axis=-1, keepdims=True)) * sc).astype(BF16)
        dq = jnp.dot(dsb, kv, preferred_element_type=F32)
        dk = lax.dot_general(dsb, q4, (_DN['tn'], ((), ())), preferred_element_type=F32)
        for j in range(G):
            dq_ref[:, j * LANE:(j + 1) * LANE] = dq[j * tq:(j + 1) * tq]

        @pl.when(i == 0)
        def _():
            dk_ref[...] = dk
            dv_ref[...] = dv

        @pl.when(i != 0)
        def _():
            dk_ref[...] += dk
            dv_ref[...] += dv

    qs = pl.BlockSpec((tq, G * LANE), lambda g, i: (i, g))
    ks = pl.BlockSpec((T, LANE), lambda g, i: (0, g))
    return pl.pallas_call(
        body, name="att_bwd", grid=(cf.HKV, T // tq),
        in_specs=[qs, ks, pl.BlockSpec((T, LANE), lambda g, i: (0, vb + g)), qs],
        out_specs=[qs, ks, ks],
        out_shape=[jax.ShapeDtypeStruct((T, cf.HQ * LANE), F32), jax.ShapeDtypeStruct((T, cf.HKV * LANE), F32),
                   jax.ShapeDtypeStruct((T, cf.HKV * LANE), F32)],
        compiler_params=_cp(("arbitrary", "arbitrary")))(q, k, z, dcat)


def _ret_masks(cf, i, tq, lgf, lgb):
    T, LC = cf.T, cf.LC
    row = (lax.broadcasted_iota(jnp.int32, (tq, 1), 0) + i * tq)
    col = lax.broadcasted_iota(jnp.int32, (1, T), 1)

    def pb(n):
        return jnp.where(n < LC, LC - 1 - n, T + LC - 1 - n).astype(F32)

    relf = row.astype(F32) - col.astype(F32)
    relb = pb(row) - pb(col)
    rf, rb = jnp.maximum(relf, 0.0), jnp.maximum(relb, 0.0)
    mf = jnp.where(relf >= 0, jnp.exp(lgf * rf), 0.0)
    mb = jnp.where(relb >= 0, jnp.exp(lgb * rb), 0.0)
    return mf, mb, rf, rb


def _ret_fwd(cf, q, k, z, lg):
    tq, T = cf.TQ, cf.T
    vb = cf.OFF['rv'] // LANE

    def body(lg_ref, q_ref, k_ref, v_ref, o_ref):
        h, i = pl.program_id(0), pl.program_id(1)
        mf, mb, _, _ = _ret_masks(cf, i, tq, lg_ref[0, h], lg_ref[1, h])
        a = lax.dot_general(q_ref[...], k_ref[...], (_DN['nt'], ((), ())), preferred_element_type=F32)
        p = (a * (mf + mb)).astype(BF16)
        o_ref[...] = jnp.dot(p, v_ref[...].astype(BF16), preferred_element_type=F32)

    return pl.pallas_call(
        body, name="ret_fwd", grid=(cf.HR, T // tq),
        in_specs=[pl.BlockSpec(memory_space=pltpu.SMEM), pl.BlockSpec((tq, LANE), lambda h, i: (i, h)),
                  pl.BlockSpec((T, LANE), lambda h, i: (0, h)), pl.BlockSpec((T, LANE), lambda h, i: (0, vb + h))],
        out_specs=pl.BlockSpec((tq, LANE), lambda h, i: (i, h)),
        out_shape=jax.ShapeDtypeStruct((T, cf.HR * LANE), F32), compiler_params=_cp(("arbitrary", "arbitrary")))(lg, q, k, z)


def _ret_bwd(cf, q, k, z, lg, do):
    tq, T = cf.TQ, cf.T
    vb = cf.OFF['rv'] // LANE

    def body(lg_ref, q_ref, k_ref, v_ref, do_ref, dq_ref, dk_ref, dv_ref, dlg_ref):
        h, i = pl.program_id(0), pl.program_id(1)
        mf, mb, rf, rb = _ret_masks(cf, i, tq, lg_ref[0, h], lg_ref[1, h])
        qv, kv, vv = q_ref[...], k_ref[...], v_ref[...].astype(BF16)
        dob = do_ref[...].astype(BF16)
        a = lax.dot_general(qv, kv, (_DN['nt'], ((), ())), preferred_element_type=F32)
        m = mf + mb
        p = (a * m).astype(BF16)
        dv = lax.dot_general(p, dob, (_DN['tn'], ((), ())), preferred_element_type=F32)
        dp = lax.dot_general(dob, vv, (_DN['nt'], ((), ())), preferred_element_type=F32)
        da = (dp * m).astype(BF16)
        dq_ref[...] = jnp.dot(da, kv, preferred_element_type=F32)
        dk = lax.dot_general(da, qv, (_DN['tn'], ((), ())), preferred_element_type=F32)
        dm = dp * a
        dlf = jnp.sum(jnp.sum(dm * mf * rf, axis=-1, keepdims=True), axis=0, keepdims=True)
        dlb = jnp.sum(jnp.sum(dm * mb * rb, axis=-1, keepdims=True), axis=0, keepdims=True)
        rid = lax.broadcasted_iota(jnp.int32, (8, LANE), 0)
        dl = jnp.where(rid == 0, dlf, jnp.where(rid == 1, dlb, 0.0))

        @pl.when(i == 0)
        def _():
            dk_ref[...] = dk
            dv_ref[...] = dv
            dlg_ref[...] = dl

        @pl.when(i != 0)
        def _():
            dk_ref[...] += dk
            dv_ref[...] += dv
            dlg_ref[...] += dl

    qs = pl.BlockSpec((tq, LANE), lambda h, i: (i, h))
    ks = pl.BlockSpec((T, LANE), lambda h, i: (0, h))
    return pl.pallas_call(
        body, name="ret_bwd", grid=(cf.HR, T // tq),
        in_specs=[pl.BlockSpec(memory_space=pltpu.SMEM), qs, ks, pl.BlockSpec((T, LANE), lambda h, i: (0, vb + h)), qs],
        out_specs=[qs, ks, ks, pl.BlockSpec((None, 8, LANE), lambda h, i: (h, 0, 0))],
        out_shape=[jax.ShapeDtypeStruct((T, cf.HR * LANE), F32)] * 3 + [jax.ShapeDtypeStruct((cf.HR, 8, LANE), F32)],
        compiler_params=_cp(("arbitrary", "arbitrary")))(lg, q, k, z, do)


def _gla_step(q, k, v, es, st, lmask, allowed):
    C, SB = GLA_CHUNK, GLA_SUB
    nsb = C // SB
    e1, e2, e3, e4, e5 = es[0], es[1:1 + nsb], es[1 + nsb], es[2 + nsb], es[3 + nsb]
    qs = q * lmask * ((HEAD_DIM // 2) ** -0.5)
    ks = k * lmask
    qt = qs * e1
    rows = [lax.dot_general(qt[b * SB:(b + 1) * SB], ks * e2[b], (_DN['nt'], ((), ())), precision=lax.Precision.HIGH,
                            preferred_element_type=F32) for b in range(nsb)]
    att = jnp.where(allowed, jnp.concatenate(rows, axis=0), 0.0)
    o = jnp.dot(att.astype(BF16), v.astype(BF16), preferred_element_type=F32)
    o += lax.dot_general((qs * e3).astype(BF16), st.astype(BF16), (_DN['nt'], ((), ())), preferred_element_type=F32)
    kd = (ks * e4).astype(BF16)
    st_new = st * jnp.concatenate([e5, e5], axis=0) + lax.dot_general(v.astype(BF16), kd, (_DN['tn'], ((), ())), preferred_element_type=F32)
    return o, st_new


def _gla_allowed(d):
    r = lax.broadcasted_iota(jnp.int32, (GLA_CHUNK, GLA_CHUNK), 0)
    m = lax.broadcasted_iota(jnp.int32, (GLA_CHUNK, GLA_CHUNK), 1)
    return (m <= r) if d == 0 else (m >= r)


def _gla_chunk_id(cf, s, d):
    if d == 0:
        return s
    nct, nc = cf.LC // GLA_CHUNK, cf.T // GLA_CHUNK
    return jnp.where(s < nct, nct - 1 - s, nc + nct - 1 - s)


def _gla_lmask(h):
    return (lax.broadcasted_iota(jnp.int32, (1, LANE), 1) // (LANE // 2) == h).astype(F32)


_GLA_CHAINS = [(h, d) for h in range(2) for d in range(2)]


def _gla_row_specs(cf, nc, reverse):
    def rowblk(s, d):
        return _gla_chunk_id(cf, nc - 1 - s if reverse else s, d)

    def spec(width, base, d, per_pair=1):
        return pl.BlockSpec((GLA_CHUNK, width), functools.partial(lambda p, s, base, d: (rowblk(s, d), base + per_pair * p), base=base, d=d))

    def state(d):
        return pl.BlockSpec((2, None, LANE, LANE), functools.partial(lambda p, s, d: (p, rowblk(s, d), 0, 0), d=d))

    return spec, state


def _gla_fwd(cf, z, ef, eb):
    T, C = cf.T, GLA_CHUNK
    nc = T // C
    qb, kb, vb = cf.OFF['gq'] // LANE, cf.OFF['gk'] // LANE, cf.OFF['gv'] // (2 * LANE)
    spec, state = _gla_row_specs(cf, nc, False)

    def body(qf, kf, vf, e_f, qb_, kb_, vb_, e_b, of, sf, ob, sb, st_scr):
        @pl.when(pl.program_id(1) == 0)
        def _():
            st_scr[...] = jnp.zeros_like(st_scr)

        io = [(qf, kf, vf, e_f, of, sf), (qb_, kb_, vb_, e_b, ob, sb)]
        for ci, (h, d) in enumerate(_GLA_CHAINS):
            q, k, v, e, o_ref, s_ref = io[d]
            cols = slice(h * LANE, (h + 1) * LANE)
            es = [e[:, n * LANE:(n + 1) * LANE] for n in range(N_DECAY)]
            st = st_scr[ci]
            s_ref[h] = st
            o, stn = _gla_step(q[...], k[...], v[:, cols], es, st, _gla_lmask(h), _gla_allowed(d))
            st_scr[ci] = stn
            o_ref[:, cols] = o

    ins, outs = [], []
    for d in range(2):
        ins += [spec(LANE, qb, d), spec(LANE, kb, d), spec(2 * LANE, vb, d), spec(N_DECAY * LANE, 0, d)]
        outs += [spec(2 * LANE, 0, d), state(d)]
    oshape = [jax.ShapeDtypeStruct((T, cf.HG * LANE), F32), jax.ShapeDtypeStruct((cf.HG, nc, LANE, LANE), F32)]
    return pl.pallas_call(
        body, name="gla_fwd", grid=(cf.HG // 2, nc), in_specs=ins, out_specs=outs, out_shape=oshape * 2,
        scratch_shapes=[pltpu.VMEM((4, LANE, LANE), F32)],
        compiler_params=_cp(("arbitrary", "arbitrary")))(z, z, z, ef, z, z, z, eb)


def _gla_bwd(cf, z, ef, eb, sf, sb, do):
    T, C = cf.T, GLA_CHUNK
    nc = T // C
    qb, kb, vb = cf.OFF['gq'] // LANE, cf.OFF['gk'] // LANE, cf.OFF['gv'] // (2 * LANE)
    spec, state = _gla_row_specs(cf, nc, True)

    def body(*refs):
        ins = [refs[0:6], refs[6:12]]
        outs = [refs[12:16], refs[16:20]]
        dst_scr = refs[20]

        @pl.when(pl.program_id(1) == 0)
        def _():
            dst_scr[...] = jnp.zeros_like(dst_scr)

        acc = [None, None]
        for ci, (h, d) in enumerate(_GLA_CHAINS):
            q, k, v, e, s_ref, do_ref = ins[d]
            cols = slice(h * LANE, (h + 1) * LANE)
            es = [e[:, n * LANE:(n + 1) * LANE] for n in range(N_DECAY)]
            step = functools.partial(_gla_step, lmask=_gla_lmask(h), allowed=_gla_allowed(d))
            _, vjp_fn = jax.vjp(step, q[...], k[...], v[:, cols], es, s_ref[h])
            dq, dk, dv, des, dst = vjp_fn((do_ref[:, cols], dst_scr[ci]))
            dst_scr[ci] = dst
            outs[d][2][:, cols] = dv
            part = [dq, dk] + list(des)
            acc[d] = part if acc[d] is None else [a + b for a, b in zip(acc[d], part)]
        for d in range(2):
            dq_ref, dk_ref, _, de_ref = outs[d]
            dq_ref[...] = acc[d][0]
            dk_ref[...] = acc[d][1]
            for n in range(N_DECAY):
                de_ref[:, n * LANE:(n + 1) * LANE] = acc[d][2 + n]

    in_specs, out_specs = [], []
    for d in range(2):
        in_specs += [spec(LANE, qb, d), spec(LANE, kb, d), spec(2 * LANE, vb, d), spec(N_DECAY * LANE, 0, d), state(d), spec(2 * LANE, 0, d)]
        out_specs += [spec(LANE, 0, d), spec(LANE, 0, d), spec(2 * LANE, 0, d), spec(N_DECAY * LANE, 0, d)]
    npair = cf.HG // 2
    oshape = [jax.ShapeDtypeStruct((T, npair * LANE), F32), jax.ShapeDtypeStruct((T, npair * LANE), F32),
              jax.ShapeDtypeStruct((T, cf.HG * LANE), F32), jax.ShapeDtypeStruct((T, npair * N_DECAY * LANE), F32)]
    return pl.pallas_call(
        body, name="gla_bwd", grid=(npair, nc), in_specs=in_specs, out_specs=out_specs, out_shape=oshape * 2,
        scratch_shapes=[pltpu.VMEM((4, LANE, LANE), F32)],
        compiler_params=_cp(("arbitrary", "arbitrary")))(z, z, z, ef, sf, do, z, z, z, eb, sb, do)


def _conv_parts(cf, a, w_ref):
    T, LC = cf.T, cf.LC
    rid = lax.broadcasted_iota(jnp.int32, a.shape, 0)
    first = jnp.logical_or(rid == 0, rid == LC)
    last = jnp.logical_or(rid == LC - 1, rid == T - 1)
    ap = jnp.where(first, 0.0, pltpu.roll(a, 1, axis=0))
    an = jnp.where(last, 0.0, pltpu.roll(a, T - 1, axis=0))
    w0, w1, w2, b = w_ref[0:1, :], w_ref[1:2, :], w_ref[2:3, :], w_ref[3:4, :]
    ac = ap * w0 + a * w1 + an * w2 + b
    return ap, an, ac, first, last, (w0, w1, w2)


def _conv_fwd(cf, u, wb):
    T, Fd = cf.T, cf.F
    tc = _tile(Fd, 512)
    nj = Fd // tc

    def body(a_ref, v_ref, w_ref, t_ref):
        _, _, ac, _, _, _ = _conv_parts(cf, a_ref[...], w_ref)
        t_ref[...] = (_silu(ac) * v_ref[...]).astype(t_ref.dtype)

    return pl.pallas_call(
        body, name="conv_fwd", grid=(nj,),
        in_specs=[pl.BlockSpec((T, tc), lambda j: (0, j)), pl.BlockSpec((T, tc), lambda j: (0, nj + j)),
                  pl.BlockSpec((8, tc), lambda j: (0, j))],
        out_specs=pl.BlockSpec((T, tc), lambda j: (0, j)), out_shape=jax.ShapeDtypeStruct((T, Fd), BF16),
        compiler_params=_cp(("parallel",)))(u, u, wb)


def _conv_bwd(cf, u, wb, dt):
    T, Fd = cf.T, cf.F
    tc = _tile(Fd, 256)
    nj = Fd // tc

    def body(a_ref, v_ref, w_ref, dt_ref, du_ref, dw_ref):
        a, v, dtv = a_ref[...], v_ref[...], dt_ref[...].astype(F32)
        ap, an, ac, first, last, (w0, w1, w2) = _conv_parts(cf, a, w_ref)
        sg = _sigmoid(ac)
        du_ref[1] = (dtv * ac * sg).astype(du_ref.dtype)
        dac = dtv * v * (sg * (1.0 + ac * (1.0 - sg)))
        from_next = pltpu.roll(jnp.where(first, 0.0, dac), T - 1, axis=0)
        from_prev = pltpu.roll(jnp.where(last, 0.0, dac), 1, axis=0)
        du_ref[0] = (dac * w1 + from_next * w0 + from_prev * w2).astype(du_ref.dtype)
        rows = [jnp.sum(dac * ap, axis=0, keepdims=True), jnp.sum(dac * a, axis=0, keepdims=True),
                jnp.sum(dac * an, axis=0, keepdims=True), jnp.sum(dac, axis=0, keepdims=True)]
        rid = lax.broadcasted_iota(jnp.int32, (8, tc), 0)
        dw = jnp.zeros((8, tc), F32)
        for n_, rw in enumerate(rows):
            dw = jnp.where(rid == n_, rw, dw)
        dw_ref[...] = dw

    col = pl.BlockSpec((T, tc), lambda j: (0, j))
    return pl.pallas_call(
        body, name="conv_bwd", grid=(nj,),
        in_specs=[col, pl.BlockSpec((T, tc), lambda j: (0, nj + j)), pl.BlockSpec((8, tc), lambda j: (0, j)), col],
        out_specs=[pl.BlockSpec((2, T, tc), lambda j: (0, 0, j)), pl.BlockSpec((8, tc), lambda j: (0, j))],
        out_shape=[jax.ShapeDtypeStruct((2, T, Fd), BF16), jax.ShapeDtypeStruct((8, Fd), F32)],
        compiler_params=_cp(("parallel",)))(u, u, wb, dt)


def _me():
    x, y, c = lax.axis_index("x"), lax.axis_index("y"), lax.axis_index("c")
    return x, y, c, 4 * x + 2 * y + c


def _peer(x, y, c, k):
    px = 1 - x if (k >> 2) & 1 else x
    py = 1 - y if (k >> 1) & 1 else y
    pc = 1 - c if k & 1 else c
    return (px, py, pc), 4 * px + 2 * py + pc


def _rcopy(src, dst, ss, rs, tgt):
    return pltpu.make_async_remote_copy(src_ref=src, dst_ref=dst, send_sem=ss, recv_sem=rs, device_id=tgt,
                                        device_id_type=pl.DeviceIdType.MESH)


def _ag_small(name, v):
    R, Cc = v.shape

    def body(v_ref, o_ref, ssem, rsem, lsem):
        x, y, c, me = _me()
        loc = pltpu.make_async_copy(v_ref, o_ref.at[me], lsem)
        loc.start()
        sends = []
        for k in range(1, N_DEV):
            tgt, _ = _peer(x, y, c, k)
            cp = _rcopy(v_ref, o_ref.at[me], ssem.at[k - 1], rsem.at[k - 1], tgt)
            cp.start()
            sends.append(cp)
        for k in range(1, N_DEV):
            tgt, pi = _peer(x, y, c, k)
            _rcopy(v_ref, o_ref.at[pi], ssem.at[k - 1], rsem.at[k - 1], tgt).wait_recv()
        for cp in sends:
            cp.wait_send()
        loc.wait()

    vm = pl.BlockSpec(memory_space=pltpu.VMEM)
    return pl.pallas_call(
        body, name=name, in_specs=[vm], out_specs=vm, out_shape=jax.ShapeDtypeStruct((N_DEV, R, Cc), v.dtype),
        scratch_shapes=[pltpu.SemaphoreType.DMA((N_DEV - 1,)), pltpu.SemaphoreType.DMA((N_DEV - 1,)), pltpu.SemaphoreType.DMA],
        compiler_params=pltpu.CompilerParams(vmem_limit_bytes=VMEM_LIMIT))(v)


_KINDS = ['in', 'out', 'up', 'down']


def _shard_shape(cf, kind):
    D, Fd = cf.D, cf.F
    return {'in': (D, cf.NINS), 'out': (D // N_DEV, D), 'up': (D, 2 * Fd // N_DEV), 'down': (Fd // N_DEV, D)}[kind]


def _whole_shape(cf, kind):
    D, Fd = cf.D, cf.F
    return {'in': (N_DEV, D, cf.NINS), 'out': (D, D), 'up': (D, 2 * Fd), 'down': (Fd, D)}[kind]


def _part(ref, cf, kind, idx):
    r, cdim = _shard_shape(cf, kind)
    if kind == 'in':
        return ref.at[idx]
    if kind == 'up':
        return ref.at[:, pl.ds(pl.multiple_of(idx * cdim, cdim), cdim)]
    return ref.at[pl.ds(pl.multiple_of(idx * r, r), r), :]


N_BARRIER_IDS = 8


def _handshake(x, y, c):
    barrier = pltpu.get_barrier_semaphore()
    for k in range(1, N_DEV):
        pl.semaphore_signal(barrier, inc=1, device_id=_peer(x, y, c, k)[0], device_id_type=pl.DeviceIdType.MESH)
    pl.semaphore_wait(barrier, N_DEV - 1)


def _seq_kernel(body, name, seq, out_type, nsem=N_DEV - 1, num_cores=1):
    return pl.kernel(
        body, out_type=out_type, mesh=plsc.ScalarSubcoreMesh(axis_name="sq", num_cores=num_cores), name=name,
        scratch_types=[pltpu.SemaphoreType.DMA((nsem,)), pltpu.SemaphoreType.DMA((nsem,)), pltpu.SemaphoreType.DMA],
        compiler_params=pltpu.CompilerParams(collective_id=seq % N_BARRIER_IDS))


def _seq_gather(cf, kind, l, seq, src):
    rows = _shard_shape(cf, kind)[0] // 2

    def body(src_ref, land_ref, ssem, rsem, lsem):
        mine_of = pl.ds(pl.multiple_of(lax.axis_index("sq") * rows, rows), rows)
        x, y, c, me = _me()
        _handshake(x, y, c)
        sib, xn, yn, dg = (x, y, 1 - c), (1 - x, y, c), (x, 1 - y, c), (1 - x, 1 - y, c)

        def blk(dev, half=None):
            part = _part(land_ref, cf, kind, 4 * dev[0] + 2 * dev[1] + dev[2]).at[mine_of]
            return part if half is None else part.at[pl.ds(half * (rows // 2), rows // 2)]

        def copy(n, src, dst, to):
            return _rcopy(src, dst, ssem.at[n], rsem.at[n], to)

        src, mine = src_ref.at[l].at[mine_of], blk((x, y, c))
        loc = pltpu.make_async_copy(src, mine, lsem)
        loc.start()
        loc.wait()
        sends = [copy(0, src, mine, sib), copy(1, src, mine, xn), copy(2, src, mine, yn)]
        for cp in sends:
            cp.start()

        def arrived(n, got):
            copy(n, got, got, sib).wait_recv()

        def pass_on(n, part, to):
            sends.append(copy(n, part, part, to))
            sends[-1].start()

        arrived(1, blk(xn))
        pass_on(3, blk(xn, 0), yn)
        pass_on(5, blk(xn), sib)
        arrived(2, blk(yn))
        pass_on(4, blk(yn, 1), xn)
        pass_on(6, blk(yn), sib)
        arrived(3, blk(dg, 0))
        pass_on(7, blk(dg, 0), sib)
        arrived(4, blk(dg, 1))
        pass_on(8, blk(dg, 1), sib)
        other = lambda d: (d[0], d[1], 1 - c)
        for n, got in [(0, blk(other((x, y, c)))), (5, blk(other(xn))), (6, blk(other(yn))), (7, blk(other(dg), 0)), (8, blk(other(dg), 1))]:
            copy(n, got, got, sib).wait_recv()
        for cp in sends:
            cp.wait_send()

    return _seq_kernel(body, "seq_gather_%s_%d" % (kind, l), seq, jax.ShapeDtypeStruct(_whole_shape(cf, kind), BF16), nsem=9, num_cores=2)(src)


def _seq_scatter(cf, kind, l, seq, g):
    rows = _shard_shape(cf, kind)[0] // 2

    def body(g_ref, recv_ref, ssem, rsem, lsem):
        x, y, c, me = _me()
        _handshake(x, y, c)
        mine_of = pl.ds(pl.multiple_of(lax.axis_index("sq") * rows, rows), rows)
        slab = lambda j: _rs_slab(g_ref, cf, kind, j).at[mine_of]
        slot = lambda s: recv_ref.at[s].at[mine_of]
        loc = pltpu.make_async_copy(slab(me), slot(me), lsem)
        loc.start()
        loc.wait()
        sends = []
        for k in range(1, N_DEV):
            tgt, pi = _peer(x, y, c, k)
            sends.append(_rcopy(slab(pi), slot(me), ssem.at[k - 1], rsem.at[k - 1], tgt))
            sends[-1].start()
        for k in range(1, N_DEV):
            tgt, pi = _peer(x, y, c, k)
            _rcopy(slab(pi), slot(pi), ssem.at[k - 1], rsem.at[k - 1], tgt).wait_recv()
        for cp in sends:
            cp.wait_send()

    return _seq_kernel(body, "seq_scatter_%s_%d" % (kind, l), seq, jax.ShapeDtypeStruct((N_DEV,) + _shard_shape(cf, kind), BF16),
                       num_cores=2)(g)


def _rs_slab(ref, cf, kind, j):
    return ref.at[j] if kind in ('in', 'up') else _part(ref, cf, kind, j)


def _adam_vals(w, g, m, v):
    m2 = ADAM_B1 * m + (1.0 - ADAM_B1) * g
    v2 = ADAM_B2 * v + (1.0 - ADAM_B2) * (g * g)
    mh = m2 / (1.0 - ADAM_B1 ** ADAM_STEP)
    vh = v2 / (1.0 - ADAM_B2 ** ADAM_STEP)
    return -ADAM_LR * (mh / (jnp.sqrt(vh) + ADAM_EPS) + ADAM_WD * w), m2, v2


def _row_tile(R, Cc, budget_elems):
    t = max(16, min(R, (budget_elems // max(Cc, 1)) // 16 * 16))
    while t > 16 and R % t:
        t -= 16
    return t if R % t == 0 else R


def _cast_bf16(name, w, after):
    Dp, R, Cc = w.shape
    tr = _row_tile(R, Cc, 1 << 20)

    def body(w_ref, after_ref, o_ref):
        o_ref[...] = w_ref[...].astype(BF16)

    spec = pl.BlockSpec((None, tr, Cc), lambda l, i: (l, i, 0))
    return pl.pallas_call(body, name=name, grid=(Dp, R // tr), in_specs=[spec, pl.BlockSpec(memory_space=pl.ANY)], out_specs=spec,
                          out_shape=jax.ShapeDtypeStruct(w.shape, BF16), compiler_params=_cp(("parallel", "parallel")))(w, after)


def _unshard_in(cf, name, g):
    D, ns, nz = cf.D, cf.NINS, cf.NZ
    tr = _tile(D, 256, 16)

    def body(g_ref, o_ref):
        for j in range(N_DEV):
            o_ref[:, ns * j:ns * (j + 1)] = g_ref[j]
        o_ref[:, N_DEV * ns:] = jnp.zeros((tr, nz - N_DEV * ns), o_ref.dtype)

    return pl.pallas_call(body, name=name, grid=(D // tr,), in_specs=[pl.BlockSpec((N_DEV, tr, ns), lambda i: (0, i, 0))],
                          out_specs=pl.BlockSpec((tr, nz), lambda i: (i, 0)), out_shape=jax.ShapeDtypeStruct((D, nz), g.dtype),
                          compiler_params=_cp(("parallel",)))(g)


def _slabs_in(cf, name, gw):
    D, ns, nz = cf.D, cf.NINS, cf.NZ
    tr = _tile(D, 256, 16)

    def body(x_ref, o_ref):
        for j in range(N_DEV):
            o_ref[j] = x_ref[:, ns * j:ns * (j + 1)]

    return pl.pallas_call(body, name=name, grid=(D // tr,), in_specs=[pl.BlockSpec((tr, nz), lambda i: (i, 0))],
                          out_specs=pl.BlockSpec((N_DEV, tr, ns), lambda i: (0, i, 0)), out_shape=jax.ShapeDtypeStruct((N_DEV, D, ns), gw.dtype),
                          compiler_params=_cp(("parallel",)))(gw)


def _sum_adam(name, recv, w, m, v, layer, prev, after):
    Dp, R, Cc = w.shape
    tr = _row_tile(R, Cc, 1 << 18)

    def body(r_ref, w_ref, m_ref, v_ref, *rest):
        g_ref, d_ref, mo_ref, vo_ref = rest[-4:]
        g = r_ref[0].astype(F32)
        for s in range(1, N_DEV):
            g = g + r_ref[s].astype(F32)
        dl, m2, v2 = _adam_vals(w_ref[...], g, m_ref[...], v_ref[...])
        g_ref[...] = g
        d_ref[...] = dl
        mo_ref[...] = m2
        vo_ref[...] = v2

    spec = pl.BlockSpec((None, tr, Cc), lambda i: (layer, i, 0))
    anyspec = pl.BlockSpec(memory_space=pl.ANY)
    extra = [after] + (list(prev) if prev is not None else [])
    aliases = {5 + n: n for n in range(4)} if prev is not None else {}
    return pl.pallas_call(body, name=name, grid=(R // tr,), in_specs=[pl.BlockSpec((N_DEV, tr, Cc), lambda i: (0, i, 0)), spec, spec, spec] + [anyspec] * len(extra),
                          out_specs=[spec] * 4, out_shape=[jax.ShapeDtypeStruct(w.shape, F32)] * 4, input_output_aliases=aliases,
                          compiler_params=_cp(("parallel",)))(recv, w, m, v, *extra)


def _adam(name, w, g, m, v):
    R, Cc = w.shape
    tr = _row_tile(R, Cc, 1 << 18)

    def body(w_ref, g_ref, m_ref, v_ref, d_ref, mo_ref, vo_ref):
        dl, m2, v2 = _adam_vals(w_ref[...], g_ref[...], m_ref[...], v_ref[...])
        d_ref[...] = dl
        mo_ref[...] = m2
        vo_ref[...] = v2

    spec = pl.BlockSpec((tr, Cc), lambda i: (i, 0))
    return pl.pallas_call(body, name=name, grid=(R // tr,), in_specs=[spec] * 4, out_specs=[spec] * 3,
                          out_shape=[jax.ShapeDtypeStruct(w.shape, F32)] * 3, compiler_params=_cp(("parallel",)))(w, g, m, v)


def _sum8(name, a):
    n, R, Cc = a.shape

    def body(a_ref, o_ref):
        s = a_ref[0]
        for k in range(1, n):
            s = s + a_ref[k]
        o_ref[...] = s

    return pl.pallas_call(body, name=name, in_specs=[pl.BlockSpec(memory_space=pltpu.VMEM)],
                          out_specs=pl.BlockSpec(memory_space=pltpu.VMEM), out_shape=jax.ShapeDtypeStruct((R, Cc), F32),
                          compiler_params=pltpu.CompilerParams(vmem_limit_bytes=VMEM_LIMIT))(a)


def _ada_fwd(cf, c9, ada_w):
    D = cf.D
    NS = ada_w.shape[-1]
    tk = _tile(D, 512)
    nk = D // tk

    def body(c_ref, w_ref, o_ref):
        kk = pl.program_id(1)
        s = _silu(c_ref[...]).astype(BF16)
        part = jnp.dot(s, w_ref[...].astype(BF16), preferred_element_type=F32)

        @pl.when(kk == 0)
        def _():
            o_ref[...] = part

        @pl.when(kk != 0)
        def _():
            o_ref[...] += part

    return pl.pallas_call(
        body, name="ada_fwd", grid=(DEPTH, nk),
        in_specs=[pl.BlockSpec((16, tk), lambda l, k: (0, k)), pl.BlockSpec((None, tk, NS), lambda l, k: (l, k, 0))],
        out_specs=pl.BlockSpec((None, 16, NS), lambda l, k: (l, 0, 0)),
        out_shape=jax.ShapeDtypeStruct((DEPTH, 16, NS), F32), compiler_params=_cp(("parallel", "arbitrary")))(c9, ada_w)


def _ada_bwd(cf, c9, ada_w, dm9):
    D = cf.D
    NS = ada_w.shape[-1]
    tk = _tile(D, 512)
    nk = D // tk

    def body(c_ref, w_ref, dm_ref, gw_ref, ds_ref):
        cv = c_ref[...]
        sg = _sigmoid(cv)
        dmb = dm_ref[...].astype(BF16)
        gw_ref[...] = lax.dot_general((cv * sg).astype(BF16), dmb, (_DN['tn'], ((), ())), preferred_element_type=F32)
        ds = lax.dot_general(dmb, w_ref[...].astype(BF16), (_DN['nt'], ((), ())), preferred_element_type=F32)
        ds_ref[...] = ds * (sg * (1.0 + cv * (1.0 - sg)))

    return pl.pallas_call(
        body, name="ada_bwd", grid=(DEPTH, nk),
        in_specs=[pl.BlockSpec((16, tk), lambda l, k: (0, k)), pl.BlockSpec((None, tk, NS), lambda l, k: (l, k, 0)),
                  pl.BlockSpec((None, 16, NS), lambda l, k: (l, 0, 0))],
        out_specs=[pl.BlockSpec((None, tk, NS), lambda l, k: (l, k, 0)), pl.BlockSpec((None, 16, tk), lambda l, k: (l, 0, k))],
        out_shape=[jax.ShapeDtypeStruct((DEPTH, D, NS), F32), jax.ShapeDtypeStruct((DEPTH, 16, D), F32)],
        compiler_params=_cp(("parallel", "parallel")))(c9, ada_w, dm9)


def _rope_tables(cf):
    L, LC = cf.L, cf.LC
    rows = L // GRID_W
    row = jnp.repeat(jnp.arange(rows, dtype=F32), GRID_W)
    col = jnp.tile(jnp.arange(GRID_W, dtype=F32), rows)
    nf = HEAD_DIM // 4
    inv = ROPE_THETA ** (-jnp.arange(nf, dtype=F32) / nf)
    ang = jnp.concatenate([row[:, None] * inv, col[:, None] * inv], axis=-1)
    cos, sin = jnp.cos(ang), jnp.sin(ang)
    cs = jnp.concatenate([jnp.ones((LC, HEAD_DIM), F32), jnp.concatenate([cos, cos], -1)], 0)
    sn = jnp.concatenate([jnp.zeros((LC, HEAD_DIM), F32), jnp.concatenate([-sin, sin], -1)], 0)
    return cs, sn


def _prep_tiles(cf, z, cs, sn, key):
    b = cf.OFF[key] // LANE
    return [(z, LANE, _col(b), True), (cs, LANE, _c0, False), (sn, LANE, _c0, False)]


_PREP = {'aq': _f_prep_norm, 'ak': _f_prep_norm, 'rq': _f_prep_plain, 'rk': _f_prep_scaled}


def _prep_fwd(cf, z, cs, sn, key, g):
    nh = cf.W[key] // LANE
    params = [(g, 'shared', True)] if g is not None else []
    return _row_fwd(cf, "prep_fwd_" + key, _PREP[key], _prep_tiles(cf, z, cs, sn, key), params,
                    [(LANE, _col(0), cf.W[key], BF16)], cf.TQ, nrep=nh)[0]


def _prep_bwd(cf, z, cs, sn, key, g, dt):
    nh = cf.W[key] // LANE
    params = [(g, 'shared', True)] if g is not None else []
    tg, pg = _row_bwd(cf, "prep_bwd_" + key, _PREP[key], _prep_tiles(cf, z, cs, sn, key), params,
                      [(dt, LANE, _col(0))], [(LANE, _col(0), cf.W[key], BF16)], cf.TQ, nrep=nh)
    return tg[0], (pg[0] if g is not None else None)


def _gate_params(cf, gup, gb):
    K = gup.shape[-1]
    gf = jnp.zeros((LANE, K), F32).at[0:GLA_RANK].set(gup[0])
    gbm = jnp.zeros((LANE, K), F32).at[GLA_RANK:2 * GLA_RANK].set(gup[1])
    out = []
    for p in range(K // LANE):
        cols = slice(p * LANE, (p + 1) * LANE)
        out += [(gf[:, cols], 'shared', True), (gbm[:, cols], 'shared', True), (gb[0:1, cols], 'shared', True), (gb[1:2, cols], 'shared', True)]
    return out


def _mix_tiles(cf, z, o, key):
    return [(o, LANE, _col(0), True), (z, LANE, _col(cf.OFF[key] // LANE), True)]


def _mid_io(cf, l, W, mod, x, y, norm2_g=None):
    tiles = [(x, cf.D, _c0, True), (y, cf.D, _c0, True)]
    n2 = W['norm2_g'][l] if norm2_g is None else norm2_g
    params = [(mod[2], 'stream', True), (n2, 'shared', True), (mod[3], 'stream', True), (mod[4], 'stream', True)]
    return tiles, params


def _launch_scatter(cf, kind, l, seq, g, nxt):
    g, nxt = lax.optimization_barrier((g, nxt))
    return _seq_scatter(cf, kind, l, seq, g), nxt


class _BigWeights:
    def __init__(self, cf, shards):
        self.cf = cf
        self.whole = {(kind, l): _seq_gather(cf, kind, l, l * len(_KINDS) + n, shards[n])
                      for l in range(DEPTH) for n, kind in enumerate(_KINDS)}
        self.w_in = {}

    def get(self, kind, l, after=None):
        cf = self.cf
        if kind != 'in':
            return self.whole[(kind, l)]
        if l not in self.w_in:
            whole, _ = lax.optimization_barrier((self.whole[(kind, l)], after))
            self.w_in[l] = _unshard_in(cf, "unshard_in_%d" % l, whole)
        return self.w_in[l]


def _layer_fwd(cf, l, W, big, mod, x, h, cs, sn):
    T, D, Fd = cf.T, cf.D, cf.F
    z = _mm("z_%d" % l, h, big.get('in', l, h), 'nn', T, cf.NZ, D, F32, tm=T, tn=768, tk=D)
    qa = _prep_fwd(cf, z, cs, sn, 'aq', W['q_norm_g'][l])
    ka = _prep_fwd(cf, z, cs, sn, 'ak', W['k_norm_g'][l])
    qr = _prep_fwd(cf, z, cs, sn, 'rq', None)
    kr = _prep_fwd(cf, z, cs, sn, 'rk', None)
    o_att = _att_fwd(cf, qa, ka, z)
    o_ret = _ret_fwd(cf, qr, kr, z, W['ret_log_decay'][l])
    gates = _gate_params(cf, W['gla_gate_up'][l], W['gla_gate_b'][l])
    ga_tile = [(z, LANE, _col(cf.OFF['ga'] // LANE), True)]
    we = (cf.HG // 2) * N_DECAY * LANE
    ef, eb = _row_fwd(cf, "gates_fwd_%d" % l, _f_gla_pre, ga_tile, gates, [(we, _c0, we, F32), (we, _c0, we, F32)], GLA_CHUNK)
    o_f, sf, o_b, sb = _gla_fwd(cf, z, ef, eb)
    o_gla = o_f + o_b
    cat_r = _row_fwd(cf, "mixr_fwd_%d" % l, _f_gated_norm, _mix_tiles(cf, z, o_ret, 'rg'), [(W['ret_norm_g'][l], 'shared', True)],
                     [(LANE, _col(0), cf.HR * LANE, BF16)], cf.TQ, nrep=cf.HR)[0]
    cat_g = _row_fwd(cf, "mixg_fwd_%d" % l, _f_gated_norm, _mix_tiles(cf, z, o_gla, 'gr'), [(W['gla_norm_g'][l], 'shared', True)],
                     [(LANE, _col(0), cf.HG * LANE, BF16)], cf.TQ, nrep=cf.HG)[0]
    cat = jnp.concatenate([o_att, cat_r, cat_g], axis=-1)
    y = _mm("y_%d" % l, cat, big.get('out', l, cat), 'nn', T, D, D, F32, tm=T, tn=512, tk=D)
    tiles, params = _mid_io(cf, l, W, mod, x, y)
    x1, h2 = _row_fwd(cf, "mid_fwd_%d" % l, _f_resid_norm_mod, tiles, params, [(D, _c0, D, F32), (D, _c0, D, BF16)], cf.TM)
    u = _mm("u_%d" % l, h2, big.get('up', l, h2), 'nn', T, 2 * Fd, D, F32, tm=T, tn=512, tk=D)
    t = _conv_fwd(cf, u, W['conv_wb'][l])
    yff = _mm("yff_%d" % l, t, big.get('down', l, t), 'nn', T, D, Fd, F32, tm=T, tn=1024, tk=512)
    return dict(x=x, h=h, z=z, qa=qa, ka=ka, qr=qr, kr=kr, ef=ef, eb=eb, sf=sf, sb=sb, o_ret=o_ret, o_gla=o_gla, cat=cat, y=y,
                x1=x1, h2=h2, u=u, t=t, yff=yff, gates=gates)


def _layer_bwd(cf, l, W, big, mod, sv, dx1, dyff, cs, sn):
    T, D, Fd = cf.T, cf.D, cf.F
    g, rs = {}, {}
    sq = 2 * len(_KINDS) + (DEPTH - 1 - l) * len(_KINDS)
    gwd = _mm("gwd_%d" % l, sv['t'], dyff, 'tn', Fd, D, T, BF16, tm=1408, tn=2048, tk=T)
    rs['down'], wb = _launch_scatter(cf, 'down', l, sq, gwd, W['conv_wb'][l])
    dt = _mm("dt_%d" % l, dyff, big.get('down', l), 'nt', T, Fd, D, BF16, tm=T, tn=1408, tk=D)
    du, g['conv_wb'] = _conv_bwd(cf, sv['u'], wb, dt)
    cu = 2 * Fd // N_DEV
    half = Fd // cu
    gwu = _mm("gwu_%d" % l, sv['h2'], du, 'tn', D, 2 * Fd, T, BF16, tm=D, tn=cu, tk=T, out_shape=(N_DEV, D, cu),
              b_spec=pl.BlockSpec((None, T, cu), lambda i, j, k: (j // half, 0, j % half)),
              out_spec=pl.BlockSpec((None, D, cu), lambda i, j, k: (j, i, 0)))
    rs['up'], n2 = _launch_scatter(cf, 'up', l, sq + 1, gwu, W['norm2_g'][l])
    dh2 = _mm("dh2_%d" % l, du, big.get('up', l), 'nt', T, D, 2 * Fd, BF16, tm=T, tn=1024, tk=cu,
              a_spec=pl.BlockSpec((None, T, cu), lambda i, j, k: (k // half, 0, k % half)))
    tiles, params = _mid_io(cf, l, W, mod, sv['x'], sv['y'], n2)
    (dx, dy), (g['m2'], g['norm2_g'], g['m3'], g['m4']) = _row_bwd(
        cf, "mid_bwd_%d" % l, _f_resid_norm_mod, tiles, params, [(dx1, D, _c0), (dh2, D, _c0)],
        [(D, _c0, D, F32), (D, _c0, D, BF16)], cf.TM)
    gwo = _mm("gwo_%d" % l, sv['cat'], dy, 'tn', D, D, T, BF16, tm=D, tn=1024, tk=T)
    rs['out'], rn = _launch_scatter(cf, 'out', l, sq + 2, gwo, W['ret_norm_g'][l])
    dcat = _mm("dcat_%d" % l, dy, big.get('out', l), 'nt', T, D, D, BF16, tm=T, tn=1024, tk=D)
    z = sv['z']
    (do_ret, drg), (g['ret_norm_g'],) = _row_bwd(
        cf, "mixr_bwd_%d" % l, _f_gated_norm, _mix_tiles(cf, z, sv['o_ret'], 'rg'), [(rn, 'shared', True)],
        [(dcat, LANE, _col(cf.HQ))], [(LANE, _col(0), cf.HR * LANE, F32), (LANE, _col(0), cf.HR * LANE, BF16)], cf.TQ, nrep=cf.HR)
    (do_gla, dgr), (g['gla_norm_g'],) = _row_bwd(
        cf, "mixg_bwd_%d" % l, _f_gated_norm, _mix_tiles(cf, z, sv['o_gla'], 'gr'), [(W['gla_norm_g'][l], 'shared', True)],
        [(dcat, LANE, _col(cf.HQ + cf.HR))], [(LANE, _col(0), cf.HG * LANE, F32), (LANE, _col(0), cf.HG * LANE, BF16)], cf.TQ, nrep=cf.HG)
    dqa, dka, dav = _att_bwd(cf, sv['qa'], sv['ka'], z, dcat)
    dqr, dkr, drv, dlg = _ret_bwd(cf, sv['qr'], sv['kr'], z, W['ret_log_decay'][l], do_ret)
    g['ret_log_decay'] = dlg[:, 0:2, 0].T
    dq_f, dk_f, dv_f, def_, dq_b, dk_b, dv_b, deb = _gla_bwd(cf, z, sv['ef'], sv['eb'], sv['sf'], sv['sb'], do_gla)
    dgq, dgk, dgv = dq_f + dq_b, dk_f + dk_b, dv_f + dv_b
    we = (cf.HG // 2) * N_DECAY * LANE
    ga_tile = [(z, LANE, _col(cf.OFF['ga'] // LANE), True)]
    (dga,), gg = _row_bwd(cf, "gates_bwd_%d" % l, _f_gla_pre, ga_tile, sv['gates'],
                          [(def_, we, _c0), (deb, we, _c0)], [(LANE, _c0, LANE, BF16)], GLA_CHUNK)
    ggf, ggb, gbf, gbb = [jnp.concatenate(gg[n::4], axis=-1) for n in range(4)]
    g['gla_gate_up'] = jnp.stack([ggf[0:GLA_RANK], ggb[GLA_RANK:2 * GLA_RANK]])
    g['gla_gate_b'] = jnp.concatenate([gbf, gbb], axis=0)
    daq, g['q_norm_g'] = _prep_bwd(cf, z, cs, sn, 'aq', W['q_norm_g'][l], dqa)
    dak, g['k_norm_g'] = _prep_bwd(cf, z, cs, sn, 'ak', W['k_norm_g'][l], dka)
    drq, _ = _prep_bwd(cf, z, cs, sn, 'rq', None, dqr)
    drk, _ = _prep_bwd(cf, z, cs, sn, 'rk', None, dkr)
    pad = jnp.zeros((T, cf.NZ - cf.OFF['ga'] - LANE), BF16)
    dz = jnp.concatenate([daq, dak, dav.astype(BF16), drq, drk, drv.astype(BF16), drg, dgq.astype(BF16), dgk.astype(BF16),
                          dgv.astype(BF16), dgr, dga, pad], axis=-1)
    gwi = _mm("gwi_%d" % l, sv['h'], dz, 'tn', D, cf.NZ, T, BF16, tm=D, tn=768, tk=T)
    rs['in'], g['norm1_g_tied'] = _launch_scatter(cf, 'in', l, sq + 3, _slabs_in(cf, "slabs_in_%d" % l, gwi), W['norm1_g'][l])
    dh =_mm("dh_%d" % l, dz, big.get('in', l), 'nt', T, D, cf.NZ, BF16, tm=T, tn=1024, tk=1792)
    g['rs'] = rs
    return dx, dh, g


_WEIGHTS = ['c_ctx', 'ada_w', 'ada_b', 'norm1_g', 'w_in', 'q_norm_g', 'k_norm_g', 'ret_log_decay', 'ret_norm_g',
            'gla_gate_up', 'gla_gate_b', 'gla_norm_g', 'w_out', 'norm2_g', 'w_up', 'conv_w', 'conv_b', 'w_down', 'final_norm_g']
_BIG = ['w_in', 'w_out', 'w_up', 'w_down']
_SMALL = [n for n in _WEIGHTS if n not in _BIG and n != 'ada_w']
_COL_SHARDED = ['gla_gate_up', 'gla_gate_b', 'conv_w']


def _pack(arrs):
    rows = []
    for a in arrs:
        flat = a.reshape(-1)
        n = flat.shape[0]
        rows.append(jnp.pad(flat, (0, -n % LANE)).reshape(-1, LANE))
    packed = jnp.concatenate(rows, axis=0)
    return jnp.pad(packed, ((0, -packed.shape[0] % 8), (0, 0)))


def _unpack(packed, shapes):
    lead = packed.shape[:-2]
    out, r = [], 0
    for s in shapes:
        n = int(np.prod(s))
        nr = -(-n // LANE)
        out.append(packed[..., r:r + nr, :].reshape(lead + (nr * LANE,))[..., :n].reshape(lead + tuple(s)))
        r += nr
    return out


def _unshard_last(a):
    return jnp.moveaxis(a, 0, -2).reshape(a.shape[1:-1] + (N_DEV * a.shape[-1],))


def _step(cf, x, c, ctx, loss_target, w, m, v):
    T, D, Fd, L, LC = cf.T, cf.D, cf.F, cf.L, cf.LC
    _, _, _, me = _me()
    NS = w['ada_w'].shape[-1]

    c_all = _ag_small("ag_c", jnp.pad(c, ((0, 7), (0, 0))))[:, 0, :]
    c9 = jnp.concatenate([c_all, w['c_ctx'][None], jnp.zeros((7, D), F32)], axis=0)
    pm = _ada_fwd(cf, c9, w['ada_w'])
    pm_all = _ag_small("ag_mod", pm.reshape(DEPTH * 16, NS)).reshape(N_DEV, DEPTH, 16, NS)
    mod_all = _unshard_last(pm_all) + w['ada_b'][:, None, :]
    mod_own = lax.dynamic_index_in_dim(mod_all, me, axis=1, keepdims=False)
    mods = []
    for l in range(DEPTH):
        mods.append([jnp.stack([mod_all[l, 8, k * D:(k + 1) * D], mod_own[l, k * D:(k + 1) * D]])[:, None, :] for k in range(N_MOD)])

    shard_shapes = [w[n].shape for n in _COL_SHARDED]
    got = _ag_small("ag_smallw", _pack([w[n] for n in _COL_SHARDED]))
    full = dict(zip(_COL_SHARDED, [_unshard_last(a) for a in _unpack(got, shard_shapes)]))

    small_done = full['conv_w'] + mod_all[0, 0, 0]
    big = _BigWeights(cf, [_cast_bf16("cast_" + n, w[n], c if n == 'w_in' else small_done) for n in _BIG])
    conv_wb = jnp.concatenate([full['conv_w'], w['conv_b'][:, None, :], jnp.zeros((DEPTH, 4, Fd), F32)], axis=1)
    W = dict(conv_wb=conv_wb, gla_gate_up=full['gla_gate_up'], gla_gate_b=full['gla_gate_b'], ret_log_decay=w['ret_log_decay'])
    for n in ['q_norm_g', 'k_norm_g', 'ret_norm_g', 'gla_norm_g', 'norm1_g', 'norm2_g']:
        W[n] = w[n][:, None, :]

    cs, sn = _rope_tables(cf)
    x0 = jnp.concatenate([ctx[0], x[0]], axis=0)
    pre_tiles = [(x0, D, _c0, True)]

    def pre_params(n1):
        return [(n1, 'shared', True), (mods[0][0], 'stream', True), (mods[0][1], 'stream', True)]

    def tr_params(n1):
        return [(mods[0][5], 'stream', True), (n1, 'shared', True), (mods[1][0], 'stream', True), (mods[1][1], 'stream', True)]

    h0 = _row_fwd(cf, "pre_fwd", _f_norm_mod, pre_tiles, pre_params(W['norm1_g'][0]), [(D, _c0, D, BF16)], cf.TM)[0]
    sv0 = _layer_fwd(cf, 0, W, big, mods[0], x0, h0, cs, sn)
    tr_tiles = [(sv0['x1'], D, _c0, True), (sv0['yff'], D, _c0, True)]
    xb, hb = _row_fwd(cf, "tr_fwd", _f_resid_norm_mod, tr_tiles, tr_params(W['norm1_g'][1]), [(D, _c0, D, F32), (D, _c0, D, BF16)], cf.TM)
    sv1 = _layer_fwd(cf, 1, W, big, mods[1], xb, hb, cs, sn)
    tgt = jnp.concatenate([jnp.zeros((LC, D), F32), loss_target[0]], axis=0)
    dx1, dyff, dm5_1, g_final, ls = _loss_grad(cf, sv1['x1'], sv1['yff'], mods[1][5], w['final_norm_g'][None], tgt)
    loss = lax.psum(ls[0, 0], ("x", "y", "c"))

    dxb, dhb, g1 = _layer_bwd(cf, 1, W, big, mods[1], sv1, dx1, dyff, cs, sn)
    (dx1_0, dyff_0), (dm5_0, gn1_1, dm0_1, dm1_1) = _row_bwd(
        cf, "tr_bwd", _f_resid_norm_mod, tr_tiles, tr_params(g1['norm1_g_tied']), [(dxb, D, _c0), (dhb, D, _c0)],
        [(D, _c0, D, F32), (D, _c0, D, BF16)], cf.TM)
    dx0, dh0, g0 = _layer_bwd(cf, 0, W, big, mods[0], sv0, dx1_0, dyff_0, cs, sn)
    (dxa,), (gn1_0, dm0_0, dm1_0) = _row_bwd(cf, "pre_bwd", _f_first, pre_tiles, pre_params(g0['norm1_g_tied']), [(dx0, D, _c0), (dh0, D, _c0)],
                                            [(D, _c0, D, F32)], cf.TM)
    grad_x = dxa[LC:][None]

    dmod = jnp.stack([jnp.concatenate([dm0_0, dm1_0, g0['m2'], g0['m3'], g0['m4'], dm5_0], axis=-1)[:, 0],
                      jnp.concatenate([dm0_1, dm1_1, g1['m2'], g1['m3'], g1['m4'], dm5_1], axis=-1)[:, 0]])
    dm_all = _ag_small("ag_dmod", jnp.pad(dmod.reshape(2 * DEPTH, N_MOD * D), ((0, 8 - 2 * DEPTH), (0, 0))))
    dm_all = dm_all[:, :2 * DEPTH].reshape(N_DEV, DEPTH, 2, N_MOD * D)
    dctx = _sum8("sum_dmodc", jnp.pad(dm_all[:, :, 0], ((0, 0), (0, 8 - DEPTH), (0, 0))))[:DEPTH]
    dm9 = jnp.concatenate([jnp.moveaxis(dm_all[:, :, 1], 0, 1), dctx[:, None]], axis=1)
    g_ada_b = _sum8("sum_adab", jnp.pad(jnp.moveaxis(dm9, 1, 0), ((0, 0), (0, 8 - DEPTH), (0, 0))))[:DEPTH]
    dm9s = lax.dynamic_slice_in_dim(jnp.pad(dm9, ((0, 0), (0, 7), (0, 0))), me * NS, NS, axis=2)
    g_ada_w, dsil = _ada_bwd(cf, c9, w['ada_w'], dm9s)
    g_cctx_part = dsil[0, 8]
    for l in range(1, DEPTH):
        g_cctx_part = g_cctx_part + dsil[l, 8]

    def both(key):
        return jnp.stack([g0[key], g1[key]])

    gsmall = dict(c_ctx=g_cctx_part, norm1_g=jnp.stack([gn1_0[0], gn1_1[0]]), q_norm_g=both('q_norm_g')[:, 0],
                  k_norm_g=both('k_norm_g')[:, 0], ret_log_decay=both('ret_log_decay'), ret_norm_g=both('ret_norm_g')[:, 0],
                  gla_gate_up=both('gla_gate_up'), gla_gate_b=both('gla_gate_b'), gla_norm_g=both('gla_norm_g')[:, 0],
                  norm2_g=both('norm2_g')[:, 0], conv_w=both('conv_wb')[:, 0:3], conv_b=both('conv_wb')[:, 3], final_norm_g=g_final[0])
    snames = [n for n in _SMALL if n != 'ada_b']
    sshapes = [gsmall[n].shape for n in snames]
    gs_all = _ag_small("ag_gsmall", _pack([gsmall[n] for n in snames]))
    gs = dict(zip(snames, _unpack(_sum8("sum_gsmall", gs_all), sshapes)))
    gs['ada_b'] = g_ada_b
    for n in _COL_SHARDED:
        ns_ = w[n].shape[-1]
        gs[n] = lax.dynamic_slice_in_dim(gs[n], me * ns_, ns_, axis=gs[n].ndim - 1)

    out_g, out_d, out_m, out_v = {}, {}, {}, {}

    after, done = gs_all, {}
    for l, gl in ((1, g1), (0, g0)):
        for kind in reversed(_KINDS):
            n = 'w_' + kind
            done[n] = _sum_adam("adam_%s_%d" % (n, l), gl['rs'][kind], w[n], m[n], v[n], l, done.get(n), after)
            after = done[n][0]
    for n in _BIG:
        out_g[n], out_d[n], out_m[n], out_v[n] = done[n]
    aw = [a.reshape(DEPTH * D, NS) for a in (w['ada_w'], g_ada_w, m['ada_w'], v['ada_w'])]
    out_g['ada_w'] = g_ada_w
    out_d['ada_w'], out_m['ada_w'], out_v['ada_w'] = [a.reshape(DEPTH, D, NS) for a in _adam("adam_ada_w", *aw)]
    shp = [w[n].shape for n in _SMALL]
    packed = [_pack([src[n] for n in _SMALL]) for src in (w, gs, m, v)]
    res = _adam("adam_small", *packed)
    for dst, pk in zip((out_d, out_m, out_v), res):
        dst.update(zip(_SMALL, _unpack(pk, shp)))
    out_g.update({n: gs[n] for n in _SMALL})
    return (loss, grad_x, *[out_g[n] for n in _WEIGHTS], *[out_d[n] for n in _WEIGHTS], *[out_m[n] for n in _WEIGHTS],
            *[out_v[n] for n in _WEIGHTS])


def kernel(x, c, ctx, c_ctx, ada_w, ada_b, norm1_g, w_in, q_norm_g, k_norm_g, ret_log_decay, ret_norm_g, gla_gate_up, gla_gate_b, gla_norm_g, w_out, norm2_g, w_up, conv_w, conv_b, w_down, final_norm_g, loss_target, m_c_ctx, m_ada_w, m_ada_b, m_norm1_g, m_w_in, m_q_norm_g, m_k_norm_g, m_ret_log_decay, m_ret_norm_g, m_gla_gate_up, m_gla_gate_b, m_gla_norm_g, m_w_out, m_norm2_g, m_w_up, m_conv_w, m_conv_b, m_w_down, m_final_norm_g, v_c_ctx, v_ada_w, v_ada_b, v_norm1_g, v_w_in, v_q_norm_g, v_k_norm_g, v_ret_log_decay, v_ret_norm_g, v_gla_gate_up, v_gla_gate_b, v_gla_norm_g, v_w_out, v_norm2_g, v_w_up, v_conv_w, v_conv_b, v_w_down, v_final_norm_g):
    w = dict(c_ctx=c_ctx, ada_w=ada_w, ada_b=ada_b, norm1_g=norm1_g, w_in=w_in, q_norm_g=q_norm_g, k_norm_g=k_norm_g,
             ret_log_decay=ret_log_decay, ret_norm_g=ret_norm_g, gla_gate_up=gla_gate_up, gla_gate_b=gla_gate_b,
             gla_norm_g=gla_norm_g, w_out=w_out, norm2_g=norm2_g, w_up=w_up, conv_w=conv_w, conv_b=conv_b, w_down=w_down,
             final_norm_g=final_norm_g)
    m = dict(c_ctx=m_c_ctx, ada_w=m_ada_w, ada_b=m_ada_b, norm1_g=m_norm1_g, w_in=m_w_in, q_norm_g=m_q_norm_g,
             k_norm_g=m_k_norm_g, ret_log_decay=m_ret_log_decay, ret_norm_g=m_ret_norm_g, gla_gate_up=m_gla_gate_up,
             gla_gate_b=m_gla_gate_b, gla_norm_g=m_gla_norm_g, w_out=m_w_out, norm2_g=m_norm2_g, w_up=m_w_up,
             conv_w=m_conv_w, conv_b=m_conv_b, w_down=m_w_down, final_norm_g=m_final_norm_g)
    v = dict(c_ctx=v_c_ctx, ada_w=v_ada_w, ada_b=v_ada_b, norm1_g=v_norm1_g, w_in=v_w_in, q_norm_g=v_q_norm_g,
             k_norm_g=v_k_norm_g, ret_log_decay=v_ret_log_decay, ret_norm_g=v_ret_norm_g, gla_gate_up=v_gla_gate_up,
             gla_gate_b=v_gla_gate_b, gla_norm_g=v_gla_norm_g, w_out=v_w_out, norm2_g=v_norm2_g, w_up=v_w_up,
             conv_w=v_conv_w, conv_b=v_conv_b, w_down=v_w_down, final_norm_g=v_final_norm_g)
    return _step(_cfg(), x, c, ctx, loss_target, w, m, v)
```
